```python
import math
import jax, jax.numpy as jnp
from jax import lax
import numpy as np

D_MODEL = 2048
BATCH = 8
SEQ = 4096
DEPTH = 2

PLE_DIM = 256
D_FF = 4 * D_MODEL
NORM_EPS = 1e-6
CHUNK = 64

A_WIDTH = D_MODEL // 2
A_HEAD_DIM = 128
A_HEADS = A_WIDTH // A_HEAD_DIM

B_WIDTH = D_MODEL // 2
B_HEAD_DIM = 128
B_HEADS = B_WIDTH // B_HEAD_DIM
CONV_WIDTH = 4

S5_GROUP = 16
S5_GROUPS = D_MODEL // S5_GROUP
S5_STATE = 64
DT_MIN = 0.001
DT_MAX = 0.1

N_EVEN = (DEPTH + 1) // 2
N_ODD = DEPTH // 2

IN_SIZES = [A_WIDTH, A_WIDTH, A_WIDTH, A_WIDTH, 3 * B_WIDTH, B_WIDTH, B_HEADS, B_HEADS]
IN_COLS = sum(IN_SIZES)
IN_CUTS = [int(c) for c in np.cumsum(IN_SIZES)[:-1]]

kernel_name = "hgrn2_deltanet_s5_hybrid_trunk"


def rms_norm(x, g):
    xf = x.astype(jnp.float32)
    y = xf * lax.rsqrt(jnp.mean(xf * xf, axis=-1, keepdims=True) + NORM_EPS)
    return (y * g.astype(jnp.float32)).astype(x.dtype)


def l2_norm(t):
    return t * lax.rsqrt(jnp.sum(t * t, axis=-1, keepdims=True) + NORM_EPS)


def to_head_chunks(t, heads):
    b, s, _ = t.shape
    return t.reshape(b, s // CHUNK, CHUNK, heads, -1).transpose(0, 3, 1, 2, 4)


def from_head_chunks(t):
    b, h, n, c, d = t.shape
    return t.transpose(0, 2, 3, 1, 4).reshape(b, n * c, h * d)


def to_scalar_chunks(t):
    b, s, h = t.shape
    return t.reshape(b, s // CHUNK, CHUNK, h).transpose(0, 3, 1, 2)


def causal_depthwise_conv(x, w):
    ch = x.shape[-1]
    return lax.conv_general_dilated(
        x, w.astype(x.dtype)[:, None, :], window_strides=(1,),
        padding=((CONV_WIDTH - 1, 0),), dimension_numbers=('NWC', 'WIO', 'NWC'),
        feature_group_count=ch)


def hgrn2_recurrence(q, k, v, log_f):
    bsz, heads, _, _, dk = q.shape
    dv = v.shape[-1]
    cum = jnp.cumsum(log_f, axis=3)
    cum_end = cum[:, :, :, -1:, :]
    q_dec = q * jnp.exp(cum)
    k_dec = k * jnp.exp(cum_end - cum)
    chunk_decay = jnp.exp(cum_end[:, :, :, 0, :])
    causal = jnp.tril(jnp.ones((CHUNK, CHUNK), dtype=bool))[:, :, None]

    def chunk_step(state, inputs):
        q_c, k_c, v_c, cum_c, qd_c, kd_c, dec_c = inputs
        rel = cum_c[:, :, :, None, :] - cum_c[:, :, None, :, :]
        pair_decay = jnp.exp(jnp.where(causal, rel, -jnp.inf))
        scores = jnp.einsum('bhtd,bhsd,bhtsd->bhts', q_c, k_c, pair_decay)
        out = (jnp.einsum('bhts,bhsv->bhtv', scores, v_c)
               + jnp.einsum('bhtd,bhdv->bhtv', qd_c, state))
        state = state * dec_c[..., None] + jnp.einsum('bhsd,bhsv->bhdv', kd_c, v_c)
        return state, out

    state0 = jnp.zeros((bsz, heads, dk, dv), jnp.float32)
    xs = tuple(jnp.moveaxis(t, 2, 0) for t in (q, k, v, cum, q_dec, k_dec, chunk_decay))
    _, out = lax.scan(chunk_step, state0, xs)
    return jnp.moveaxis(out, 0, 2)


def gated_delta_rule(q, k, v, log_a, beta):
    bsz, heads, _, _, dk = q.shape
    dv = v.shape[-1]
    cum = jnp.cumsum(log_a, axis=-1)
    causal = jnp.tril(jnp.ones((CHUNK, CHUNK), dtype=bool))
    strict = jnp.tril(jnp.ones((CHUNK, CHUNK), dtype=bool), k=-1)
    decay = jnp.exp(jnp.where(causal, cum[..., :, None] - cum[..., None, :], -jnp.inf))
    k_beta = k * beta[..., None]
    lower = jnp.where(strict, jnp.einsum('bhntd,bhnsd->bhnts', k_beta, k) * decay, 0.0)
    rhs = jnp.concatenate([v * beta[..., None], k_beta * jnp.exp(cum)[..., None]], axis=-1)
    sol = lax.linalg.triangular_solve(lower + jnp.eye(CHUNK, dtype=lower.dtype), rhs,
                                      left_side=True, lower=True, unit_diagonal=True)
    u, w = sol[..., :dv], sol[..., dv:]
    intra = jnp.einsum('bhntd,bhnsd->bhnts', q, k) * decay
    q_dec = q * jnp.exp(cum)[..., None]
    k_dec = k * jnp.exp(cum[..., -1:] - cum)[..., None]
    chunk_decay = jnp.exp(cum[..., -1])

    def chunk_step(state, inputs):
        qd_c, kd_c, u_c, w_c, a_c, dec_c = inputs
        v_new = u_c - jnp.einsum('bhcd,bhdv->bhcv', w_c, state)
        out = (jnp.einsum('bhcd,bhdv->bhcv', qd_c, state)
               + jnp.einsum('bhts,bhsv->bhtv', a_c, v_new))
        state = state * dec_c[..., None, None] + jnp.einsum('bhsd,bhsv->bhdv', kd_c, v_new)
        return state, out

    state0 = jnp.zeros((bsz, heads, dk, dv), jnp.float32)
    xs = tuple(jnp.moveaxis(t, 2, 0) for t in (q_dec, k_dec, u, w, intra, chunk_decay))
    _, out = lax.scan(chunk_step, state0, xs)
    return jnp.moveaxis(out, 0, 2)


def hgrn2_deltanet_mixer(hn, w_in, w_out, lb, g_norm_a, conv_w, a_log, dt_bias, g_norm_b):
    f32 = jnp.float32
    proj = hn @ w_in
    q_a, f_a, i_a, g_a, qkv_b, z_b, a_b, b_b = jnp.split(proj, IN_CUTS, axis=-1)

    forget = lb + (1.0 - lb) * jax.nn.sigmoid(f_a.astype(f32))
    o_a = hgrn2_recurrence(to_head_chunks(q_a.astype(f32), A_HEADS),
                           to_head_chunks(1.0 - forget, A_HEADS),
                           to_head_chunks(i_a.astype(f32), A_HEADS),
                           to_head_chunks(jnp.log(forget), A_HEADS))
    o_a = from_head_chunks(rms_norm(o_a, g_norm_a)) * jax.nn.silu(g_a.astype(f32))

    qkv = jax.nn.silu(causal_depthwise_conv(qkv_b, conv_w)).astype(f32)
    q_b, k_b, v_b = jnp.split(qkv, 3, axis=-1)
    q_b = l2_norm(to_head_chunks(q_b, B_HEADS)) * (B_HEAD_DIM ** -0.5)
    k_b = l2_norm(to_head_chunks(k_b, B_HEADS))
    v_b = to_head_chunks(v_b, B_HEADS)
    beta = jax.nn.sigmoid(to_scalar_chunks(b_b.astype(f32)))
    log_a = -jnp.exp(a_log.astype(f32)) * jax.nn.softplus(a_b.astype(f32) + dt_bias.astype(f32))
    o_b = gated_delta_rule(q_b, k_b, v_b, to_scalar_chunks(log_a), beta)
    o_b = from_head_chunks(rms_norm(o_b, g_norm_b)) * jax.nn.silu(z_b.astype(f32))

    merged = jnp.concatenate([o_a, o_b], axis=-1).astype(hn.dtype)
    return merged @ w_out


def ssm_combine(e1, e2):
    a1, b1 = e1
    a2, b2 = e2
    return a1 * a2, a2 * b1 + b2


def s5_mixer(hn, a_re, a_im, b_re, b_im, c_re, c_im, d_skip, log_dt, w_glu, b_glu, w_out):
    f32 = jnp.float32
    bsz, s, _ = hn.shape
    u = hn.astype(f32)
    u_g = u.reshape(bsz, s, S5_GROUPS, S5_GROUP).astype(jnp.complex64)
    lam = lax.complex(a_re.astype(f32), a_im.astype(f32))
    step = jnp.exp(log_dt.astype(f32))[:, None]
    lam_bar = jnp.exp(lam * step)
    b_mat = lax.complex(b_re.astype(f32), b_im.astype(f32))
    b_bar = ((lam_bar - 1.0) / lam)[..., None] * b_mat
    c_mat = lax.complex(c_re.astype(f32), c_im.astype(f32))
    bu = jnp.einsum('gpc,bsgc->bsgp', b_bar, u_g)
    a_elems = jnp.broadcast_to(lam_bar, (1, s) + lam_bar.shape)
    _, states = lax.associative_scan(ssm_combine, (a_elems, bu), axis=1)
    y = jnp.einsum('gcp,bsgp->bsgc', c_mat, states).real.reshape(bsz, s, D_MODEL)
    y = y + d_skip.astype(f32) * u
    act = jax.nn.gelu(y)
    glu = act * jax.nn.sigmoid(act @ w_glu.astype(f32) + b_glu.astype(f32))
    return glu.astype(hn.dtype) @ w_out


def _fwd_setup_inputs(seed: int = 0) -> dict:
    key = jax.random.key(seed)
    ks = iter(jax.random.split(key, 40))
    f32 = jnp.float32

    def nrm(shape, scale):
        return scale * jax.random.normal(next(ks), shape, f32)

    def unif(shape, lo, hi):
        return jax.random.uniform(next(ks), shape, f32, minval=lo, maxval=hi)

    x = nrm((BATCH, SEQ, D_MODEL), 1.0)
    p = nrm((DEPTH, BATCH, SEQ, PLE_DIM), 1.0)
    norm_mix = 1.0 + nrm((DEPTH, D_MODEL), 0.02)
    norm_mlp = 1.0 + nrm((DEPTH, D_MODEL), 0.02)
    norm_ple = 1.0 + nrm((DEPTH, D_MODEL), 0.02)
    w_in_e = nrm((N_EVEN, D_MODEL, IN_COLS), D_MODEL ** -0.5)
    w_out_e = nrm((N_EVEN, D_MODEL, D_MODEL), D_MODEL ** -0.5)
    hgrn_lb = nrm((DEPTH + 1, A_WIDTH), 0.1)
    g_norm_a = 1.0 + nrm((N_EVEN, A_HEAD_DIM), 0.02)
    conv_w = nrm((N_EVEN, CONV_WIDTH, 3 * B_WIDTH), CONV_WIDTH ** -0.5)
    a_log = jnp.log(unif((N_EVEN, B_HEADS), 1.0, 16.0))
    dt = jnp.exp(unif((N_EVEN, B_HEADS), math.log(DT_MIN), math.log(DT_MAX)))
    dt_bias = dt + jnp.log(-jnp.expm1(-dt))
    g_norm_b = 1.0 + nrm((N_EVEN, B_HEAD_DIM), 0.02)
    s5_a_re = -0.5 + nrm((N_ODD, S5_GROUPS, S5_STATE), 0.01)
    s5_a_im = (math.pi * jnp.arange(S5_STATE, dtype=f32))[None, None, :] + nrm((N_ODD, S5_GROUPS, S5_STATE), 0.01)
    s5_b_re = nrm((N_ODD, S5_GROUPS, S5_STATE, S5_GROUP), (2 * S5_GROUP) ** -0.5)
    s5_b_im = nrm((N_ODD, S5_GROUPS, S5_STATE, S5_GROUP), (2 * S5_GROUP) ** -0.5)
    s5_c_re = nrm((N_ODD, S5_GROUPS, S5_GROUP, S5_STATE), S5_STATE ** -0.5)
    s5_c_im = nrm((N_ODD, S5_GROUPS, S5_GROUP, S5_STATE), S5_STATE ** -0.5)
    s5_d = nrm((N_ODD, D_MODEL), 1.0)
    s5_log_dt = unif((N_ODD, S5_GROUPS), math.log(DT_MIN), math.log(DT_MAX))
    w_glu = nrm((N_ODD, D_MODEL, D_MODEL), D_MODEL ** -0.5)
    b_glu = nrm((N_ODD, D_MODEL), 0.01)
    w_out_o = nrm((N_ODD, D_MODEL, D_MODEL), D_MODEL ** -0.5)
    w_up = nrm((DEPTH, D_MODEL, D_FF), D_MODEL ** -0.5)
    w_down = nrm((DEPTH, D_FF, D_MODEL), D_FF ** -0.5)
    w_ple_gate = nrm((DEPTH, D_MODEL, D_MODEL), D_MODEL ** -0.5)
    w_ple_proj = nrm((DEPTH, PLE_DIM, D_MODEL), PLE_DIM ** -0.5)
    final_norm = 1.0 + nrm((D_MODEL,), 0.02)
    return {"x": x, "p": p, "norm_mix": norm_mix, "norm_mlp": norm_mlp, "norm_ple": norm_ple,
            "w_in_e": w_in_e, "w_out_e": w_out_e, "hgrn_lb": hgrn_lb, "g_norm_a": g_norm_a,
            "conv_w": conv_w, "a_log": a_log, "dt_bias": dt_bias, "g_norm_b": g_norm_b,
            "s5_a_re": s5_a_re, "s5_a_im": s5_a_im, "s5_b_re": s5_b_re, "s5_b_im": s5_b_im,
            "s5_c_re": s5_c_re, "s5_c_im": s5_c_im, "s5_d": s5_d, "s5_log_dt": s5_log_dt,
            "w_glu": w_glu, "b_glu": b_glu, "w_out_o": w_out_o, "w_up": w_up, "w_down": w_down,
            "w_ple_gate": w_ple_gate, "w_ple_proj": w_ple_proj, "final_norm": final_norm}


def _fwd_reference(x, p, norm_mix, norm_mlp, norm_ple, w_in_e, w_out_e, hgrn_lb, g_norm_a,
              conv_w, a_log, dt_bias, g_norm_b, s5_a_re, s5_a_im, s5_b_re, s5_b_im,
              s5_c_re, s5_c_im, s5_d, s5_log_dt, w_glu, b_glu, w_out_o, w_up, w_down,
              w_ple_gate, w_ple_proj, final_norm):
    lower_bounds = jnp.cumsum(jax.nn.softmax(hgrn_lb.astype(jnp.float32), axis=0), axis=0)
    h = x
    for i in range(DEPTH):
        j = i // 2
        hn = rms_norm(h, norm_mix[i])
        if i % 2 == 0:
            mix = hgrn2_deltanet_mixer(hn, w_in_e[j], w_out_e[j], lower_bounds[i], g_norm_a[j],
                                       conv_w[j], a_log[j], dt_bias[j], g_norm_b[j])
        else:
            mix = s5_mixer(hn, s5_a_re[j], s5_a_im[j], s5_b_re[j], s5_b_im[j], s5_c_re[j],
                           s5_c_im[j], s5_d[j], s5_log_dt[j], w_glu[j], b_glu[j], w_out_o[j])
        h = h + mix.astype(h.dtype)
        hn = rms_norm(h, norm_mlp[i])
        h = h + (jnp.square(jax.nn.relu(hn @ w_up[i])) @ w_down[i]).astype(h.dtype)
        gate = jax.nn.sigmoid(rms_norm(h, norm_ple[i]) @ w_ple_gate[i])
        h = h + (gate * (p[i] @ w_ple_proj[i])).astype(h.dtype)
    return rms_norm(h, final_norm)


import jax as _jax
import jax.numpy as _jnp

TWIN_FORMAT = 'train_step'
FWD_PARAMS = ['x', 'p', 'norm_mix', 'norm_mlp', 'norm_ple', 'w_in_e', 'w_out_e', 'hgrn_lb', 'g_norm_a', 'conv_w', 'a_log', 'dt_bias', 'g_norm_b', 's5_a_re', 's5_a_im', 's5_b_re', 's5_b_im', 's5_c_re', 's5_c_im', 's5_d', 's5_log_dt', 'w_glu', 'b_glu', 'w_out_o', 'w_up', 'w_down', 'w_ple_gate', 'w_ple_proj', 'final_norm']
TWIN_WEIGHTS = ['norm_mix', 'norm_mlp', 'norm_ple', 'w_in_e', 'w_out_e', 'hgrn_lb', 'g_norm_a', 'conv_w', 'a_log', 'dt_bias', 'g_norm_b', 's5_a_re', 's5_a_im', 's5_b_re', 's5_b_im', 's5_c_re', 's5_c_im', 's5_d', 's5_log_dt', 'w_glu', 'b_glu', 'w_out_o', 'w_up', 'w_down', 'w_ple_gate', 'w_ple_proj', 'final_norm']
TWIN_DIFF_INPUT = 'x'
TWIN_INPUTS = ['x', 'p', 'norm_mix', 'norm_mlp', 'norm_ple', 'w_in_e', 'w_out_e', 'hgrn_lb', 'g_norm_a', 'conv_w', 'a_log', 'dt_bias', 'g_norm_b', 's5_a_re', 's5_a_im', 's5_b_re', 's5_b_im', 's5_c_re', 's5_c_im', 's5_d', 's5_log_dt', 'w_glu', 'b_glu', 'w_out_o', 'w_up', 'w_down', 'w_ple_gate', 'w_ple_proj', 'final_norm', 'loss_target', 'm_norm_mix', 'm_norm_mlp', 'm_norm_ple', 'm_w_in_e', 'm_w_out_e', 'm_hgrn_lb', 'm_g_norm_a', 'm_conv_w', 'm_a_log', 'm_dt_bias', 'm_g_norm_b', 'm_s5_a_re', 'm_s5_a_im', 'm_s5_b_re', 'm_s5_b_im', 'm_s5_c_re', 'm_s5_c_im', 'm_s5_d', 'm_s5_log_dt', 'm_w_glu', 'm_b_glu', 'm_w_out_o', 'm_w_up', 'm_w_down', 'm_w_ple_gate', 'm_w_ple_proj', 'm_final_norm', 'v_norm_mix', 'v_norm_mlp', 'v_norm_ple', 'v_w_in_e', 'v_w_out_e', 'v_hgrn_lb', 'v_g_norm_a', 'v_conv_w', 'v_a_log', 'v_dt_bias', 'v_g_norm_b', 'v_s5_a_re', 'v_s5_a_im', 'v_s5_b_re', 'v_s5_b_im', 'v_s5_c_re', 'v_s5_c_im', 'v_s5_d', 'v_s5_log_dt', 'v_w_glu', 'v_b_glu', 'v_w_out_o', 'v_w_up', 'v_w_down', 'v_w_ple_gate', 'v_w_ple_proj', 'v_final_norm']
TWIN_OUTPUTS = ['loss', 'grad_x', 'grad_norm_mix', 'grad_norm_mlp', 'grad_norm_ple', 'grad_w_in_e', 'grad_w_out_e', 'grad_hgrn_lb', 'grad_g_norm_a', 'grad_conv_w', 'grad_a_log', 'grad_dt_bias', 'grad_g_norm_b', 'grad_s5_a_re', 'grad_s5_a_im', 'grad_s5_b_re', 'grad_s5_b_im', 'grad_s5_c_re', 'grad_s5_c_im', 'grad_s5_d', 'grad_s5_log_dt', 'grad_w_glu', 'grad_b_glu', 'grad_w_out_o', 'grad_w_up', 'grad_w_down', 'grad_w_ple_gate', 'grad_w_ple_proj', 'grad_final_norm', 'delta_norm_mix', 'delta_norm_mlp', 'delta_norm_ple', 'delta_w_in_e', 'delta_w_out_e', 'delta_hgrn_lb', 'delta_g_norm_a', 'delta_conv_w', 'delta_a_log', 'delta_dt_bias', 'delta_g_norm_b', 'delta_s5_a_re', 'delta_s5_a_im', 'delta_s5_b_re', 'delta_s5_b_im', 'delta_s5_c_re', 'delta_s5_c_im', 'delta_s5_d', 'delta_s5_log_dt', 'delta_w_glu', 'delta_b_glu', 'delta_w_out_o', 'delta_w_up', 'delta_w_down', 'delta_w_ple_gate', 'delta_w_ple_proj', 'delta_final_norm', 'new_m_norm_mix', 'new_m_norm_mlp', 'new_m_norm_ple', 'new_m_w_in_e', 'new_m_w_out_e', 'new_m_hgrn_lb', 'new_m_g_norm_a', 'new_m_conv_w', 'new_m_a_log', 'new_m_dt_bias', 'new_m_g_norm_b', 'new_m_s5_a_re', 'new_m_s5_a_im', 'new_m_s5_b_re', 'new_m_s5_b_im', 'new_m_s5_c_re', 'new_m_s5_c_im', 'new_m_s5_d', 'new_m_s5_log_dt', 'new_m_w_glu', 'new_m_b_glu', 'new_m_w_out_o', 'new_m_w_up', 'new_m_w_down', 'new_m_w_ple_gate', 'new_m_w_ple_proj', 'new_m_final_norm', 'new_v_norm_mix', 'new_v_norm_mlp', 'new_v_norm_ple', 'new_v_w_in_e', 'new_v_w_out_e', 'new_v_hgrn_lb', 'new_v_g_norm_a', 'new_v_conv_w', 'new_v_a_log', 'new_v_dt_bias', 'new_v_g_norm_b', 'new_v_s5_a_re', 'new_v_s5_a_im', 'new_v_s5_b_re', 'new_v_s5_b_im', 'new_v_s5_c_re', 'new_v_s5_c_im', 'new_v_s5_d', 'new_v_s5_log_dt', 'new_v_w_glu', 'new_v_b_glu', 'new_v_w_out_o', 'new_v_w_up', 'new_v_w_down', 'new_v_w_ple_gate', 'new_v_w_ple_proj', 'new_v_final_norm']
TWIN_LEAF_KINDS = {'loss': 'loss', 'grad_x': 'grad_x', 'grad_norm_mix': 'grad_w', 'grad_norm_mlp': 'grad_w', 'grad_norm_ple': 'grad_w', 'grad_w_in_e': 'grad_w', 'grad_w_out_e': 'grad_w', 'grad_hgrn_lb': 'grad_w', 'grad_g_norm_a': 'grad_w', 'grad_conv_w': 'grad_w', 'grad_a_log': 'grad_w', 'grad_dt_bias': 'grad_w', 'grad_g_norm_b': 'grad_w', 'grad_s5_a_re': 'grad_w', 'grad_s5_a_im': 'grad_w', 'grad_s5_b_re': 'grad_w', 'grad_s5_b_im': 'grad_w', 'grad_s5_c_re': 'grad_w', 'grad_s5_c_im': 'grad_w', 'grad_s5_d': 'grad_w', 'grad_s5_log_dt': 'grad_w', 'grad_w_glu': 'grad_w', 'grad_b_glu': 'grad_w', 'grad_w_out_o': 'grad_w', 'grad_w_up': 'grad_w', 'grad_w_down': 'grad_w', 'grad_w_ple_gate': 'grad_w', 'grad_w_ple_proj': 'grad_w', 'grad_final_norm': 'grad_w', 'delta_norm_mix': 'delta_w', 'delta_norm_mlp': 'delta_w', 'delta_norm_ple': 'delta_w', 'delta_w_in_e': 'delta_w', 'delta_w_out_e': 'delta_w', 'delta_hgrn_lb': 'delta_w', 'delta_g_norm_a': 'delta_w', 'delta_conv_w': 'delta_w', 'delta_a_log': 'delta_w', 'delta_dt_bias': 'delta_w', 'delta_g_norm_b': 'delta_w', 'delta_s5_a_re': 'delta_w', 'delta_s5_a_im': 'delta_w', 'delta_s5_b_re': 'delta_w', 'delta_s5_b_im': 'delta_w', 'delta_s5_c_re': 'delta_w', 'delta_s5_c_im': 'delta_w', 'delta_s5_d': 'delta_w', 'delta_s5_log_dt': 'delta_w', 'delta_w_glu': 'delta_w', 'delta_b_glu': 'delta_w', 'delta_w_out_o': 'delta_w', 'delta_w_up': 'delta_w', 'delta_w_down': 'delta_w', 'delta_w_ple_gate': 'delta_w', 'delta_w_ple_proj': 'delta_w', 'delta_final_norm': 'delta_w', 'new_m_norm_mix': 'new_m', 'new_m_norm_mlp': 'new_m', 'new_m_norm_ple': 'new_m', 'new_m_w_in_e': 'new_m', 'new_m_w_out_e': 'new_m', 'new_m_hgrn_lb': 'new_m', 'new_m_g_norm_a': 'new_m', 'new_m_conv_w': 'new_m', 'new_m_a_log': 'new_m', 'new_m_dt_bias': 'new_m', 'new_m_g_norm_b': 'new_m', 'new_m_s5_a_re': 'new_m', 'new_m_s5_a_im': 'new_m', 'new_m_s5_b_re': 'new_m', 'new_m_s5_b_im': 'new_m', 'new_m_s5_c_re': 'new_m', 'new_m_s5_c_im': 'new_m', 'new_m_s5_d': 'new_m', 'new_m_s5_log_dt': 'new_m', 'new_m_w_glu': 'new_m', 'new_m_b_glu': 'new_m', 'new_m_w_out_o': 'new_m', 'new_m_w_up': 'new_m', 'new_m_w_down': 'new_m', 'new_m_w_ple_gate': 'new_m', 'new_m_w_ple_proj': 'new_m', 'new_m_final_norm': 'new_m', 'new_v_norm_mix': 'new_v', 'new_v_norm_mlp': 'new_v', 'new_v_norm_ple': 'new_v', 'new_v_w_in_e': 'new_v', 'new_v_w_out_e': 'new_v', 'new_v_hgrn_lb': 'new_v', 'new_v_g_norm_a': 'new_v', 'new_v_conv_w': 'new_v', 'new_v_a_log': 'new_v', 'new_v_dt_bias': 'new_v', 'new_v_g_norm_b': 'new_v', 'new_v_s5_a_re': 'new_v', 'new_v_s5_a_im': 'new_v', 'new_v_s5_b_re': 'new_v', 'new_v_s5_b_im': 'new_v', 'new_v_s5_c_re': 'new_v', 'new_v_s5_c_im': 'new_v', 'new_v_s5_d': 'new_v', 'new_v_s5_log_dt': 'new_v', 'new_v_w_glu': 'new_v', 'new_v_b_glu': 'new_v', 'new_v_w_out_o': 'new_v', 'new_v_w_up': 'new_v', 'new_v_w_down': 'new_v', 'new_v_w_ple_gate': 'new_v', 'new_v_w_ple_proj': 'new_v', 'new_v_final_norm': 'new_v'}


def _forward(args):
    return _fwd_reference(*[args[k] for k in FWD_PARAMS])


def _output_shape():
    def fwd():
        inp = _fwd_setup_inputs(0)
        return _fwd_reference(*[inp[k] for k in FWD_PARAMS])
    out = _jax.eval_shape(fwd)
    return out.shape, out.dtype

N_MICROBATCH = 1
ADAM_LR = 0.001
ADAM_B1 = 0.9
ADAM_B2 = 0.999
ADAM_EPS = 1e-08
ADAM_WD = 0.01
ADAM_STEP = 10
PER_EXAMPLE_BATCH_AXIS = {'x': 0, 'p': 1, 'loss_target': 0}
SHARED_INPUTS = []
_WEIGHT_DTYPES = {'norm_mix': _jnp.float32, 'norm_mlp': _jnp.float32, 'norm_ple': _jnp.float32, 'w_in_e': _jnp.float32, 'w_out_e': _jnp.float32, 'hgrn_lb': _jnp.float32, 'g_norm_a': _jnp.float32, 'conv_w': _jnp.float32, 'a_log': _jnp.float32, 'dt_bias': _jnp.float32, 'g_norm_b': _jnp.float32, 's5_a_re': _jnp.float32, 's5_a_im': _jnp.float32, 's5_b_re': _jnp.float32, 's5_b_im': _jnp.float32, 's5_c_re': _jnp.float32, 's5_c_im': _jnp.float32, 's5_d': _jnp.float32, 's5_log_dt': _jnp.float32, 'w_glu': _jnp.float32, 'b_glu': _jnp.float32, 'w_out_o': _jnp.float32, 'w_up': _jnp.float32, 'w_down': _jnp.float32, 'w_ple_gate': _jnp.float32, 'w_ple_proj': _jnp.float32, 'final_norm': _jnp.float32}
MOMENT_SCALE = {'norm_mix': 7.225974e-02, 'norm_mlp': 6.598692e-02, 'norm_ple': 1.024725e-02, 'w_in_e': 4.754690e-02, 'w_out_e': 4.875536e-02, 'hgrn_lb': 1.600731e-02, 'g_norm_a': 1.388512e-01, 'conv_w': 3.680209e-02, 'a_log': 1.993511e-01, 'dt_bias': 1.947035e-01, 'g_norm_b': 1.373627e-01, 's5_a_re': 2.545721e-03, 's5_a_im': 2.130561e-03, 's5_b_re': 1.337416e-03, 's5_b_im': 1.176142e-03, 's5_c_re': 1.859512e-03, 's5_c_im': 1.670644e-03, 's5_d': 2.425473e-02, 's5_log_dt': 6.195389e-01, 'w_glu': 5.399395e-03, 'b_glu': 9.482404e-03, 'w_out_o': 2.206464e-02, 'w_up': 3.317505e-02, 'w_down': 6.852579e-02, 'w_ple_gate': 1.011912e-02, 'w_ple_proj': 2.488445e-02, 'final_norm': 1.621255e+01}


def _to_microbatches(a, axis):
    t = _jnp.moveaxis(a, axis, 0)
    t = t.reshape((N_MICROBATCH, t.shape[0] // N_MICROBATCH) + t.shape[1:])
    return _jnp.moveaxis(t, 1, axis + 1)


def setup_inputs(seed: int = 0) -> dict:
    inp = _fwd_setup_inputs(seed)
    key = _jax.random.fold_in(_jax.random.key(seed), 7919)
    shape, _ = _output_shape()
    out = dict(inp)
    out["loss_target"] = _jax.random.normal(_jax.random.fold_in(key, 0), shape, _jnp.float32)
    for i, name in enumerate(TWIN_WEIGHTS):
        w = inp[name].astype(_jnp.float32)
        if MOMENT_SCALE is None:
            s = _jnp.sqrt(_jnp.mean(_jnp.square(w)) + 1e-30)
        else:
            s = MOMENT_SCALE[name]
        km, kv = _jax.random.split(_jax.random.fold_in(key, i + 1))
        out[name] = w
        out["m_" + name] = s * _jax.random.normal(km, w.shape, _jnp.float32)
        out["v_" + name] = (s * s) * _jax.random.uniform(kv, w.shape, _jnp.float32, 0.5, 1.5)
    if N_MICROBATCH > 1:
        for name, axis in PER_EXAMPLE_BATCH_AXIS.items():
            out[name] = _to_microbatches(out[name], axis)
    return {'x': out['x'], 'p': out['p'], 'norm_mix': out['norm_mix'], 'norm_mlp': out['norm_mlp'], 'norm_ple': out['norm_ple'], 'w_in_e': out['w_in_e'], 'w_out_e': out['w_out_e'], 'hgrn_lb': out['hgrn_lb'], 'g_norm_a': out['g_norm_a'], 'conv_w': out['conv_w'], 'a_log': out['a_log'], 'dt_bias': out['dt_bias'], 'g_norm_b': out['g_norm_b'], 's5_a_re': out['s5_a_re'], 's5_a_im': out['s5_a_im'], 's5_b_re': out['s5_b_re'], 's5_b_im': out['s5_b_im'], 's5_c_re': out['s5_c_re'], 's5_c_im': out['s5_c_im'], 's5_d': out['s5_d'], 's5_log_dt': out['s5_log_dt'], 'w_glu': out['w_glu'], 'b_glu': out['b_glu'], 'w_out_o': out['w_out_o'], 'w_up': out['w_up'], 'w_down': out['w_down'], 'w_ple_gate': out['w_ple_gate'], 'w_ple_proj': out['w_ple_proj'], 'final_norm': out['final_norm'], 'loss_target': out['loss_target'], 'm_norm_mix': out['m_norm_mix'], 'm_norm_mlp': out['m_norm_mlp'], 'm_norm_ple': out['m_norm_ple'], 'm_w_in_e': out['m_w_in_e'], 'm_w_out_e': out['m_w_out_e'], 'm_hgrn_lb': out['m_hgrn_lb'], 'm_g_norm_a': out['m_g_norm_a'], 'm_conv_w': out['m_conv_w'], 'm_a_log': out['m_a_log'], 'm_dt_bias': out['m_dt_bias'], 'm_g_norm_b': out['m_g_norm_b'], 'm_s5_a_re': out['m_s5_a_re'], 'm_s5_a_im': out['m_s5_a_im'], 'm_s5_b_re': out['m_s5_b_re'], 'm_s5_b_im': out['m_s5_b_im'], 'm_s5_c_re': out['m_s5_c_re'], 'm_s5_c_im': out['m_s5_c_im'], 'm_s5_d': out['m_s5_d'], 'm_s5_log_dt': out['m_s5_log_dt'], 'm_w_glu': out['m_w_glu'], 'm_b_glu': out['m_b_glu'], 'm_w_out_o': out['m_w_out_o'], 'm_w_up': out['m_w_up'], 'm_w_down': out['m_w_down'], 'm_w_ple_gate': out['m_w_ple_gate'], 'm_w_ple_proj': out['m_w_ple_proj'], 'm_final_norm': out['m_final_norm'], 'v_norm_mix': out['v_norm_mix'], 'v_norm_mlp': out['v_norm_mlp'], 'v_norm_ple': out['v_norm_ple'], 'v_w_in_e': out['v_w_in_e'], 'v_w_out_e': out['v_w_out_e'], 'v_hgrn_lb': out['v_hgrn_lb'], 'v_g_norm_a': out['v_g_norm_a'], 'v_conv_w': out['v_conv_w'], 'v_a_log': out['v_a_log'], 'v_dt_bias': out['v_dt_bias'], 'v_g_norm_b': out['v_g_norm_b'], 'v_s5_a_re': out['v_s5_a_re'], 'v_s5_a_im': out['v_s5_a_im'], 'v_s5_b_re': out['v_s5_b_re'], 'v_s5_b_im': out['v_s5_b_im'], 'v_s5_c_re': out['v_s5_c_re'], 'v_s5_c_im': out['v_s5_c_im'], 'v_s5_d': out['v_s5_d'], 'v_s5_log_dt': out['v_s5_log_dt'], 'v_w_glu': out['v_w_glu'], 'v_b_glu': out['v_b_glu'], 'v_w_out_o': out['v_w_out_o'], 'v_w_up': out['v_w_up'], 'v_w_down': out['v_w_down'], 'v_w_ple_gate': out['v_w_ple_gate'], 'v_w_ple_proj': out['v_w_ple_proj'], 'v_final_norm': out['v_final_norm']}


def _loss(weights, diff, rest, loss_target):
    with _jax.named_scope("forward"):
        args = {**rest, TWIN_DIFF_INPUT: diff, **{k: w.astype(_WEIGHT_DTYPES[k]) for k, w in weights.items()}}
        y = _forward(args)
    with _jax.named_scope("loss_head"):
        err = _jnp.square(y.astype(_jnp.float32) - loss_target)
        return 0.5 * _jnp.sum(_jnp.mean(err, axis=-1)) if err.ndim else 0.5 * err


def _adamw(w, g, m, v):
    m = ADAM_B1 * m + (1.0 - ADAM_B1) * g
    v = ADAM_B2 * v + (1.0 - ADAM_B2) * _jnp.square(g)
    m_hat = m / (1.0 - ADAM_B1 ** ADAM_STEP)
    v_hat = v / (1.0 - ADAM_B2 ** ADAM_STEP)
    delta = -ADAM_LR * (m_hat / (_jnp.sqrt(v_hat) + ADAM_EPS) + ADAM_WD * w)
    return delta, m, v


def reference(x, p, norm_mix, norm_mlp, norm_ple, w_in_e, w_out_e, hgrn_lb, g_norm_a, conv_w, a_log, dt_bias, g_norm_b, s5_a_re, s5_a_im, s5_b_re, s5_b_im, s5_c_re, s5_c_im, s5_d, s5_log_dt, w_glu, b_glu, w_out_o, w_up, w_down, w_ple_gate, w_ple_proj, final_norm, loss_target, m_norm_mix, m_norm_mlp, m_norm_ple, m_w_in_e, m_w_out_e, m_hgrn_lb, m_g_norm_a, m_conv_w, m_a_log, m_dt_bias, m_g_norm_b, m_s5_a_re, m_s5_a_im, m_s5_b_re, m_s5_b_im, m_s5_c_re, m_s5_c_im, m_s5_d, m_s5_log_dt, m_w_glu, m_b_glu, m_w_out_o, m_w_up, m_w_down, m_w_ple_gate, m_w_ple_proj, m_final_norm, v_norm_mix, v_norm_mlp, v_norm_ple, v_w_in_e, v_w_out_e, v_hgrn_lb, v_g_norm_a, v_conv_w, v_a_log, v_dt_bias, v_g_norm_b, v_s5_a_re, v_s5_a_im, v_s5_b_re, v_s5_b_im, v_s5_c_re, v_s5_c_im, v_s5_d, v_s5_log_dt, v_w_glu, v_b_glu, v_w_out_o, v_w_up, v_w_down, v_w_ple_gate, v_w_ple_proj, v_final_norm):
    given = dict(x=x, p=p, norm_mix=norm_mix, norm_mlp=norm_mlp, norm_ple=norm_ple, w_in_e=w_in_e, w_out_e=w_out_e, hgrn_lb=hgrn_lb, g_norm_a=g_norm_a, conv_w=conv_w, a_log=a_log, dt_bias=dt_bias, g_norm_b=g_norm_b, s5_a_re=s5_a_re, s5_a_im=s5_a_im, s5_b_re=s5_b_re, s5_b_im=s5_b_im, s5_c_re=s5_c_re, s5_c_im=s5_c_im, s5_d=s5_d, s5_log_dt=s5_log_dt, w_glu=w_glu, b_glu=b_glu, w_out_o=w_out_o, w_up=w_up, w_down=w_down, w_ple_gate=w_ple_gate, w_ple_proj=w_ple_proj, final_norm=final_norm, loss_target=loss_target, m_norm_mix=m_norm_mix, m_norm_mlp=m_norm_mlp, m_norm_ple=m_norm_ple, m_w_in_e=m_w_in_e, m_w_out_e=m_w_out_e, m_hgrn_lb=m_hgrn_lb, m_g_norm_a=m_g_norm_a, m_conv_w=m_conv_w, m_a_log=m_a_log, m_dt_bias=m_dt_bias, m_g_norm_b=m_g_norm_b, m_s5_a_re=m_s5_a_re, m_s5_a_im=m_s5_a_im, m_s5_b_re=m_s5_b_re, m_s5_b_im=m_s5_b_im, m_s5_c_re=m_s5_c_re, m_s5_c_im=m_s5_c_im, m_s5_d=m_s5_d, m_s5_log_dt=m_s5_log_dt, m_w_glu=m_w_glu, m_b_glu=m_b_glu, m_w_out_o=m_w_out_o, m_w_up=m_w_up, m_w_down=m_w_down, m_w_ple_gate=m_w_ple_gate, m_w_ple_proj=m_w_ple_proj, m_final_norm=m_final_norm, v_norm_mix=v_norm_mix, v_norm_mlp=v_norm_mlp, v_norm_ple=v_norm_ple, v_w_in_e=v_w_in_e, v_w_out_e=v_w_out_e, v_hgrn_lb=v_hgrn_lb, v_g_norm_a=v_g_norm_a, v_conv_w=v_conv_w, v_a_log=v_a_log, v_dt_bias=v_dt_bias, v_g_norm_b=v_g_norm_b, v_s5_a_re=v_s5_a_re, v_s5_a_im=v_s5_a_im, v_s5_b_re=v_s5_b_re, v_s5_b_im=v_s5_b_im, v_s5_c_re=v_s5_c_re, v_s5_c_im=v_s5_c_im, v_s5_d=v_s5_d, v_s5_log_dt=v_s5_log_dt, v_w_glu=v_w_glu, v_b_glu=v_b_glu, v_w_out_o=v_w_out_o, v_w_up=v_w_up, v_w_down=v_w_down, v_w_ple_gate=v_w_ple_gate, v_w_ple_proj=v_w_ple_proj, v_final_norm=v_final_norm)
    weights = {n: given[n] for n in TWIN_WEIGHTS}
    shared = {n: given[n] for n in SHARED_INPUTS}
    per_example = {n: given[n] for n in ['x', 'p']}
    grad_fn = _jax.value_and_grad(_loss, argnums=(0, 1))

    def one_microbatch(ex, loss_target):
        ex = dict(ex)
        diff = ex.pop(TWIN_DIFF_INPUT)
        return grad_fn(weights, diff, {**shared, **ex}, loss_target)

    if N_MICROBATCH == 1:
        loss, (grad_w, grad_x) = one_microbatch(per_example, given["loss_target"])
    else:
        def body(carry, xs):
            loss_sum, grad_sum = carry
            l_k, (gw_k, gx_k) = one_microbatch(xs[0], xs[1])
            with _jax.named_scope("update"):
                return (loss_sum + l_k, _jax.tree.map(_jnp.add, grad_sum, gw_k)), gx_k

        init = (_jnp.zeros((), _jnp.float32), _jax.tree.map(_jnp.zeros_like, weights))
        (loss, grad_w), grad_x = _jax.lax.scan(body, init, (per_example, given["loss_target"]))
    with _jax.named_scope("update"):
        delta_w, new_m, new_v = {}, {}, {}
        for n in TWIN_WEIGHTS:
            delta_w[n], new_m[n], new_v[n] = _adamw(weights[n], grad_w[n], given["m_" + n], given["v_" + n])
    return (loss, grad_x, *[grad_w[n] for n in TWIN_WEIGHTS], *[delta_w[n] for n in TWIN_WEIGHTS],
            *[new_m[n] for n in TWIN_WEIGHTS], *[new_v[n] for n in TWIN_WEIGHTS])
```

```python
import functools
import math

import jax
import jax.numpy as jnp
from jax import lax
from jax.experimental import pallas as pl
from jax.experimental.pallas import tpu as pltpu

F32 = jnp.float32
BF16 = jnp.bfloat16

D_MODEL = 2048
SEQ = 4096
NORM_EPS = 1e-6
CHUNK = 64
HEAD = 128
VMEM_LIMIT = 48 * 1024 * 1024


_NN = (((1,), (0,)), ((), ()))
_NT = (((1,), (1,)), ((), ()))
_TN = (((0,), (0,)), ((), ()))
_HI = lax.Precision.HIGHEST
_NEG = -1e30


def _cparams(sem, **kw):
    return pltpu.CompilerParams(dimension_semantics=sem, vmem_limit_bytes=VMEM_LIMIT, **kw)


PACK_COLS = 2048


def _view_shape(view):
    kind, _, r = view
    return (4 * r, PACK_COLS) if kind == "row" else (r, 4 * PACK_COLS)


def _view_spec(view, rb, cb, row_of, col_of):
    kind, off, r = view
    assert off % rb == 0 and r % rb == 0 and PACK_COLS % cb == 0, (view, rb, cb)
    if kind == "row":
        nrb = r // rb
        return pl.BlockSpec((None, rb, cb), lambda i, j, k: (row_of(i, j, k) // nrb,
                                                             off // rb + row_of(i, j, k) % nrb, col_of(i, j, k)))
    ncb = PACK_COLS // cb
    return pl.BlockSpec((None, rb, cb), lambda i, j, k: (col_of(i, j, k) // ncb,
                                                         off // rb + row_of(i, j, k), col_of(i, j, k) % ncb))


def _mm(a, b, *, dims="nn", epilogue=None, extras=(), out_dtypes=(F32,), bm=1024, bn=1024, bk=512, name,
        b_view=None, out_into=None, out_view=None):
    b_shape = _view_shape(b_view) if b_view is not None else b.shape
    if dims == "tn":
        (K, M), (K2, N) = a.shape, b_shape
    elif dims == "nt":
        (M, K), (N, K2) = a.shape, b_shape
    else:
        (M, K), (K2, N) = a.shape, b_shape
    assert K == K2, (a.shape, b_shape, dims)
    if b_view is not None and dims == "nt":
        bn = min(bn, b_view[2])
    if out_view is not None:
        bm = min(bm, out_view[2])
    bm, bn, bk = min(bm, M), min(bn, N), min(bk, K)
    assert M % bm == 0 and N % bn == 0 and K % bk == 0, (M, N, K, bm, bn, bk)
    nk = K // bk
    ii, jj, kk = (lambda i, j, k: i), (lambda i, j, k: j), (lambda i, j, k: k)
    if dims == "tn":
        a_spec = pl.BlockSpec((bk, bm), lambda i, j, k: (k, i))
        dn = _TN
    else:
        a_spec = pl.BlockSpec((bm, bk), lambda i, j, k: (i, k))
        dn = _NT if dims == "nt" else _NN
    if dims == "nt":
        b_spec = _view_spec(b_view, bn, bk, jj, kk) if b_view else pl.BlockSpec((bn, bk), lambda i, j, k: (j, k))
    else:
        b_spec = _view_spec(b_view, bk, bn, kk, jj) if b_view else pl.BlockSpec((bk, bn), lambda i, j, k: (k, j))
    e_specs = []
    for e in extras:
        if e.shape == (M, N):
            e_specs.append(pl.BlockSpec((bm, bn), lambda i, j, k: (i, j)))
        else:
            assert e.shape == (1, N), e.shape
            e_specs.append(pl.BlockSpec((1, bn), lambda i, j, k: (0, j)))
    ne, no = len(extras), len(out_dtypes)
    if epilogue is None:
        epilogue = lambda acc: (acc,)
    into = out_into is not None

    def body(a_ref, b_ref, *rest):
        e_refs, rest = rest[:ne], rest[ne + (1 if into else 0):]
        o_refs, acc = rest[:no], rest[no]
        k = pl.program_id(2)

        @pl.when(k == 0)
        def _():
            acc[...] = jnp.zeros_like(acc)

        acc[...] += lax.dot_general(a_ref[...].astype(BF16), b_ref[...].astype(BF16), dn,
                                    preferred_element_type=F32)

        @pl.when(k == nk - 1)
        def _():
            outs = epilogue(acc[...], *[e[...] for e in e_refs])
            for o, v in zip(o_refs, outs):
                o[...] = v.astype(o.dtype)

    if into:
        assert no == 1 and _view_shape(out_view) == (M, N), (out_view, M, N)
        out_specs = [_view_spec(out_view, bm, bn, ii, jj)]
        out_shape = [jax.ShapeDtypeStruct(out_into.shape, out_into.dtype)]
        extra_in, extra_specs, alias = [out_into], [pl.BlockSpec(memory_space=pl.ANY)], {2 + ne: 0}
    else:
        out_specs = [pl.BlockSpec((bm, bn), lambda i, j, k: (i, j)) for _ in out_dtypes]
        out_shape = [jax.ShapeDtypeStruct((M, N), dt) for dt in out_dtypes]
        extra_in, extra_specs, alias = [], [], {}
    outs = pl.pallas_call(
        body, name=name,
        grid=(M // bm, N // bn, nk),
        in_specs=[a_spec, b_spec] + e_specs + extra_specs,
        out_specs=out_specs, out_shape=out_shape,
        scratch_shapes=[pltpu.VMEM((bm, bn), F32)],
        input_output_aliases=alias,
        compiler_params=_cparams(("parallel", "parallel", "arbitrary")),
    )(a, b, *extras, *extra_in)
    return outs[0] if no == 1 else tuple(outs)


def _rowwise(fn, tiled, bcast, outs, sums=(), *, rows=256, name):
    S = tiled[0].shape[0]
    rows = min(rows, S)
    assert S % rows == 0
    nt, nb, no, ns = len(tiled), len(bcast), len(outs), len(sums)

    def body(*refs):
        t_refs, b_refs = refs[:nt], refs[nt:nt + nb]
        o_refs, s_refs = refs[nt + nb:nt + nb + no], refs[nt + nb + no:]
        res = fn(*[r[...] for r in t_refs], *[r[...] for r in b_refs])
        res = res if isinstance(res, (tuple, list)) else (res,)
        for o, v in zip(o_refs, res[:no]):
            o[...] = v.astype(o.dtype)
        if ns:
            @pl.when(pl.program_id(0) == 0)
            def _():
                for s in s_refs:
                    s[...] = jnp.zeros_like(s)
            for s, v in zip(s_refs, res[no:]):
                s[...] += v

    res = pl.pallas_call(
        body, name=name,
        grid=(S // rows,),
        in_specs=[pl.BlockSpec((rows, t.shape[1]), lambda i: (i, 0)) for t in tiled]
        + [pl.BlockSpec(b.shape, lambda i, nd=b.ndim: (0,) * nd) for b in bcast],
        out_specs=[pl.BlockSpec((rows, c), lambda i: (i, 0)) for c, _ in outs]
        + [pl.BlockSpec((1, c), lambda i: (0, 0)) for c in sums],
        out_shape=[jax.ShapeDtypeStruct((S, c), dt) for c, dt in outs]
        + [jax.ShapeDtypeStruct((1, c), F32) for c in sums],
        compiler_params=_cparams(("arbitrary",)),
    )(*tiled, *bcast)
    return res[0] if len(res) == 1 else tuple(res)


def _rms(x, g):
    return x * lax.rsqrt(jnp.mean(x * x, axis=-1, keepdims=True) + NORM_EPS) * g


def _dot(a, b, dn=_NN):
    return lax.dot_general(a.astype(BF16), b.astype(BF16), dn, preferred_element_type=F32)


def _dot_hi(a, b, dn=_NN):
    return lax.dot_general(a, b, dn, precision=_HI, preferred_element_type=F32)


def _iota(shape, dim):
    return lax.broadcasted_iota(jnp.int32, shape, dim)


def _tril(n, strict=False):
    r, c = _iota((n, n), 0), _iota((n, n), 1)
    return (r > c) if strict else (r >= c)


@functools.partial(jax.custom_vjp, nondiff_argnums=(1,))
def _roll_rows(x, r):
    return pltpu.roll(x, r, 0)


def _roll_rows_fwd(x, r):
    return pltpu.roll(x, r, 0), None


def _roll_rows_bwd(r, _, g):
    return (pltpu.roll(g, (g.shape[0] - r) % g.shape[0], 0),)


_roll_rows.defvjp(_roll_rows_fwd, _roll_rows_bwd)


def _head_norm_gate(o, gn, gate):
    return _rms(o, gn) * jax.nn.silu(gate)


_SUB = 16


def _hgrn_chunk(q, fpre, v, gate, lb, gna, st):
    c = q.shape[0]
    forget = lb + (1.0 - lb) * jax.nn.sigmoid(fpre)
    k = 1.0 - forget
    logf = jnp.log(forget)
    cum = _dot_hi(_tril(c).astype(F32), logf)
    cum_end = jnp.sum(logf, axis=0, keepdims=True)
    o = _dot(q * jnp.exp(cum), st, _NT)
    st_new = st * jnp.exp(cum_end) + _dot(v, k * jnp.exp(cum_end - cum), _TN)
    t = _iota((c, 1), 0)
    s_off = jnp.zeros((c, c), F32)
    for i in range(1, c // _SUB):
        before = t < i * _SUB
        c_i = jnp.sum(jnp.where(before, logf, 0.0), axis=0, keepdims=True)
        in_blk = (t >= i * _SUB) & (t < (i + 1) * _SUB)
        qi = jnp.where(in_blk, q * jnp.exp(jnp.minimum(cum - c_i, 0.0)), 0.0)
        ki = jnp.where(before, k * jnp.exp(jnp.minimum(c_i - cum, 0.0)), 0.0)
        s_off = s_off + _dot(qi, ki, _NT)
    o = o + _dot(s_off, v)
    tmod = t % _SUB
    for r in range(_SUB):
        kr, cr, vr = (k, cum, v) if r == 0 else (_roll_rows(k, r), _roll_rows(cum, r), _roll_rows(v, r))
        w = q * kr * jnp.exp(jnp.where(tmod >= r, cum - cr, _NEG))
        o = o + jnp.sum(w, axis=1, keepdims=True) * vr
    return _head_norm_gate(o, gna, gate), st_new


def _hgrn_fwd(proj, lb, gna):
    S = proj.shape[0]
    n_chunks, heads = S // CHUNK, 8

    def body(q_ref, f_ref, v_ref, g_ref, lb_ref, gna_ref, o_ref, st_out, st):
        n, h = pl.program_id(0), pl.program_id(1)

        @pl.when(n == 0)
        def _():
            st[h] = jnp.zeros((HEAD, HEAD), F32)

        st0 = st[h]
        st_out[...] = st0
        o, st1 = _hgrn_chunk(q_ref[...], f_ref[...], v_ref[...], g_ref[...], lb_ref[...], gna_ref[...], st0)
        o_ref[...] = o.astype(o_ref.dtype)
        st[h] = st1

    sec = lambda s: pl.BlockSpec((CHUNK, HEAD), lambda n, h, s=s: (n, 8 * s + h))
    return pl.pallas_call(
        body, name="hgrn_fwd", grid=(n_chunks, heads),
        in_specs=[sec(0), sec(1), sec(2), sec(3),
                  pl.BlockSpec((1, HEAD), lambda n, h: (0, h)), pl.BlockSpec((1, HEAD), lambda n, h: (0, 0))],
        out_specs=[pl.BlockSpec((CHUNK, HEAD), lambda n, h: (n, h)),
                   pl.BlockSpec((None, None, HEAD, HEAD), lambda n, h: (n, h, 0, 0))],
        out_shape=[jax.ShapeDtypeStruct((S, 1024), BF16),
                   jax.ShapeDtypeStruct((n_chunks, heads, HEAD, HEAD), F32)],
        scratch_shapes=[pltpu.VMEM((heads, HEAD, HEAD), F32)],
        compiler_params=_cparams(("arbitrary", "arbitrary")),
    )(proj, proj, proj, proj, lb, gna)


def _hgrn_bwd(proj, lb, gna, states, d_o):
    S = proj.shape[0]
    n_chunks, heads = S // CHUNK, 8

    def body(q_ref, f_ref, v_ref, g_ref, lb_ref, gna_ref, st_ref, do_ref,
             dq_ref, df_ref, dv_ref, dg_ref, dlb_ref, dgna_ref, dst):
        n, h = pl.program_id(0), pl.program_id(1)

        @pl.when(n == 0)
        def _():
            dst[h] = jnp.zeros((HEAD, HEAD), F32)
            dlb_ref[h] = jnp.zeros((1, HEAD), F32)

        @pl.when((n == 0) & (h == 0))
        def _():
            dgna_ref[...] = jnp.zeros_like(dgna_ref)

        _, vjp = jax.vjp(_hgrn_chunk, q_ref[...], f_ref[...], v_ref[...], g_ref[...], lb_ref[...], gna_ref[...],
                         st_ref[...])
        dq, df, dv, dg, dlb, dgna, dst0 = vjp((do_ref[...], dst[h]))
        dq_ref[...] = dq.astype(dq_ref.dtype)
        df_ref[...] = df.astype(df_ref.dtype)
        dv_ref[...] = dv.astype(dv_ref.dtype)
        dg_ref[...] = dg.astype(dg_ref.dtype)
        dlb_ref[h] += dlb
        dgna_ref[...] += dgna
        dst[h] = dst0

    rev = lambda n: n_chunks - 1 - n
    sec = lambda s: pl.BlockSpec((CHUNK, HEAD), lambda n, h, s=s: (rev(n), 8 * s + h))
    out = pl.BlockSpec((CHUNK, HEAD), lambda n, h: (rev(n), h))
    return pl.pallas_call(
        body, name="hgrn_bwd", grid=(n_chunks, heads),
        in_specs=[sec(0), sec(1), sec(2), sec(3),
                  pl.BlockSpec((1, HEAD), lambda n, h: (0, h)), pl.BlockSpec((1, HEAD), lambda n, h: (0, 0)),
                  pl.BlockSpec((None, None, HEAD, HEAD), lambda n, h: (rev(n), h, 0, 0)),
                  pl.BlockSpec((CHUNK, HEAD), lambda n, h: (rev(n), h))],
        out_specs=[out, out, out, out,
                   pl.BlockSpec((heads, 1, HEAD), lambda n, h: (0, 0, 0)),
                   pl.BlockSpec((1, HEAD), lambda n, h: (0, 0))],
        out_shape=[jax.ShapeDtypeStruct((S, 1024), BF16)] * 4
        + [jax.ShapeDtypeStruct((heads, 1, HEAD), F32), jax.ShapeDtypeStruct((1, HEAD), F32)],
        scratch_shapes=[pltpu.VMEM((heads, HEAD, HEAD), F32)],
        compiler_params=_cparams(("arbitrary", "arbitrary")),
    )(proj, proj, proj, proj, lb, gna, states, d_o)


def _l2n(t):
    return t * lax.rsqrt(jnp.sum(t * t, axis=-1, keepdims=True) + NORM_EPS)


def _unit_lower_solve(low, rhs_list):
    c = low.shape[0]
    sols = [r - _dot_hi(low, r) for r in rhs_list]
    p = low
    span = 2
    while span < c:
        p = _dot_hi(p, p)
        sols = [s + _dot_hi(p, s) for s in sols]
        span *= 2
    return sols


def _gdn_chunk(qc, kc, v, z, tail, h, arow, dtrow, gnb, st):
    c = qc.shape[0]
    lane = _iota(tail.shape, 1)
    la_all = arow * jax.nn.softplus(tail + dtrow)
    la = jnp.sum(jnp.where(lane == h, la_all, 0.0), axis=1, keepdims=True)
    beta = jnp.sum(jnp.where(lane == h + 8, jax.nn.sigmoid(tail), 0.0), axis=1, keepdims=True)
    tri = _tril(c).astype(F32)
    cum = _dot_hi(tri, jnp.broadcast_to(la, (c, HEAD)))
    cmat = _dot_hi(tri, jnp.broadcast_to(la, (c, c)))
    cum_end = jnp.sum(jnp.broadcast_to(la, (c, HEAD)), axis=0, keepdims=True)
    decay = jnp.exp(jnp.where(_tril(c), cmat - cmat.T, _NEG))
    q = _l2n(qc) * (HEAD ** -0.5)
    k = _l2n(kc)
    k_beta = k * beta
    low = jnp.where(_tril(c, strict=True), _dot(k_beta, k, _NT) * decay, 0.0)
    u, w = _unit_lower_solve(low, [v * beta, k_beta * jnp.exp(cum)])
    intra = _dot(q, k, _NT) * decay
    v_new = u - _dot(w, st, _NT)
    o = _dot(q * jnp.exp(cum), st, _NT) + _dot(intra, v_new)
    st_new = st * jnp.exp(cum_end) + _dot(v_new, k * jnp.exp(cum_end - cum), _TN)
    return _head_norm_gate(o, gnb, z), st_new


def _gdn_specs(n_of):
    qkv = lambda s: pl.BlockSpec((CHUNK, HEAD), lambda n, h, s=s: (n_of(n), 8 * s + h))
    return [qkv(0), qkv(1), qkv(2),
            pl.BlockSpec((CHUNK, HEAD), lambda n, h: (n_of(n), 56 + h)),
            pl.BlockSpec((CHUNK, HEAD), lambda n, h: (n_of(n), 64)),
            pl.BlockSpec((1, HEAD), lambda n, h: (0, 0)), pl.BlockSpec((1, HEAD), lambda n, h: (0, 0)),
            pl.BlockSpec((1, HEAD), lambda n, h: (0, 0))]


def _gdn_fwd(act, proj, arow, dtrow, gnb):
    S = act.shape[0]
    n_chunks, heads = S // CHUNK, 8

    def body(q_ref, k_ref, v_ref, z_ref, t_ref, a_ref, dt_ref, gnb_ref, o_ref, st_out, st):
        n, h = pl.program_id(0), pl.program_id(1)

        @pl.when(n == 0)
        def _():
            st[h] = jnp.zeros((HEAD, HEAD), F32)

        st0 = st[h]
        st_out[...] = st0
        o, st1 = _gdn_chunk(q_ref[...], k_ref[...], v_ref[...], z_ref[...], t_ref[...], h, a_ref[...],
                            dt_ref[...], gnb_ref[...], st0)
        o_ref[...] = o.astype(o_ref.dtype)
        st[h] = st1

    return pl.pallas_call(
        body, name="gdn_fwd", grid=(n_chunks, heads),
        in_specs=_gdn_specs(lambda n: n),
        out_specs=[pl.BlockSpec((CHUNK, HEAD), lambda n, h: (n, h)),
                   pl.BlockSpec((None, None, HEAD, HEAD), lambda n, h: (n, h, 0, 0))],
        out_shape=[jax.ShapeDtypeStruct((S, 1024), BF16),
                   jax.ShapeDtypeStruct((n_chunks, heads, HEAD, HEAD), F32)],
        scratch_shapes=[pltpu.VMEM((heads, HEAD, HEAD), F32)],
        compiler_params=_cparams(("arbitrary", "arbitrary")),
    )(act, act, act, proj, proj, arow, dtrow, gnb)


def _gdn_bwd(act, proj, arow, dtrow, gnb, states, d_o):
    S = act.shape[0]
    n_chunks, heads = S // CHUNK, 8

    def body(q_ref, k_ref, v_ref, z_ref, t_ref, a_ref, dt_ref, gnb_ref, st_ref, do_ref,
             dq_ref, dk_ref, dv_ref, dz_ref, dt_out, da_ref, ddt_ref, dgnb_ref, dst, dtail_acc):
        n, h = pl.program_id(0), pl.program_id(1)

        @pl.when(n == 0)
        def _():
            dst[h] = jnp.zeros((HEAD, HEAD), F32)

        @pl.when((n == 0) & (h == 0))
        def _():
            da_ref[...] = jnp.zeros_like(da_ref)
            ddt_ref[...] = jnp.zeros_like(ddt_ref)
            dgnb_ref[...] = jnp.zeros_like(dgnb_ref)

        fn = lambda qc, kc, v, z, tail, arow, dtrow, gnb, st: _gdn_chunk(qc, kc, v, z, tail, h, arow, dtrow, gnb, st)
        _, vjp = jax.vjp(fn, q_ref[...], k_ref[...], v_ref[...], z_ref[...], t_ref[...], a_ref[...], dt_ref[...],
                         gnb_ref[...], st_ref[...])
        dq, dk, dv, dz, dtail, da, ddt, dgnb, dst0 = vjp((do_ref[...], dst[h]))
        dq_ref[...] = dq
        dk_ref[...] = dk
        dv_ref[...] = dv
        dz_ref[...] = dz.astype(dz_ref.dtype)

        @pl.when(h == 0)
        def _():
            dtail_acc[...] = dtail

        @pl.when(h > 0)
        def _():
            dtail_acc[...] += dtail

        @pl.when(h == heads - 1)
        def _():
            dt_out[...] = dtail_acc[...].astype(dt_out.dtype)

        da_ref[...] += da
        ddt_ref[...] += ddt
        dgnb_ref[...] += dgnb
        dst[h] = dst0

    rev = lambda n: n_chunks - 1 - n
    out = pl.BlockSpec((CHUNK, HEAD), lambda n, h: (rev(n), h))
    row = pl.BlockSpec((1, HEAD), lambda n, h: (0, 0))
    return pl.pallas_call(
        body, name="gdn_bwd", grid=(n_chunks, heads),
        in_specs=_gdn_specs(rev)
        + [pl.BlockSpec((None, None, HEAD, HEAD), lambda n, h: (rev(n), h, 0, 0)),
           pl.BlockSpec((CHUNK, HEAD), lambda n, h: (rev(n), 8 + h))],
        out_specs=[out, out, out, out, pl.BlockSpec((CHUNK, HEAD), lambda n, h: (rev(n), 0)), row, row, row],
        out_shape=[jax.ShapeDtypeStruct((S, 1024), F32)] * 3
        + [jax.ShapeDtypeStruct((S, 1024), BF16), jax.ShapeDtypeStruct((S, HEAD), BF16)]
        + [jax.ShapeDtypeStruct((1, HEAD), F32)] * 3,
        scratch_shapes=[pltpu.VMEM((heads, HEAD, HEAD), F32), pltpu.VMEM((CHUNK, HEAD), F32)],
        compiler_params=_cparams(("arbitrary", "arbitrary")),
    )(act, act, act, proj, proj, arow, dtrow, gnb, states, d_o)


def _conv_silu(x, w):
    t = _iota((x.shape[0], 1), 0)
    tap = _iota(w.shape, 0)
    y = jnp.zeros_like(x)
    for r in range(4):
        w_r = jnp.sum(jnp.where(tap == 3 - r, w, 0.0), axis=0, keepdims=True)
        y = y + (x if r == 0 else jnp.where(t >= r, _roll_rows(x, r), 0.0)) * w_r
    return jax.nn.silu(y)


_CONV_COLS = 128


def _conv_fwd(proj, conv_w):
    S = proj.shape[0]
    nb = 3072 // _CONV_COLS
    off = 4096 // _CONV_COLS

    def body(x_ref, w_ref, o_ref):
        o_ref[...] = _conv_silu(x_ref[...], w_ref[...])

    return pl.pallas_call(
        body, name="conv_fwd", grid=(nb,),
        in_specs=[pl.BlockSpec((S, _CONV_COLS), lambda j: (0, off + j)), pl.BlockSpec((4, _CONV_COLS), lambda j: (0, j))],
        out_specs=pl.BlockSpec((S, _CONV_COLS), lambda j: (0, j)),
        out_shape=jax.ShapeDtypeStruct((S, 3072), F32),
        compiler_params=_cparams(("parallel",)),
    )(proj, conv_w)


def _conv_bwd(proj, conv_w, dq, dk, dv):
    S = proj.shape[0]
    nb = 3072 // _CONV_COLS
    off = 4096 // _CONV_COLS
    per = 1024 // _CONV_COLS

    def body(x_ref, w_ref, dq_ref, dk_ref, dv_ref, dx_ref, dw_ref):
        j = pl.program_id(0)
        _, vjp = jax.vjp(_conv_silu, x_ref[...], w_ref[...])
        d = jnp.where(j < per, dq_ref[...], jnp.where(j < 2 * per, dk_ref[...], dv_ref[...]))
        dx, dw = vjp(d)
        dx_ref[...] = dx.astype(dx_ref.dtype)
        dw_ref[...] = dw

    dsp = lambda s: pl.BlockSpec((S, _CONV_COLS), lambda j, s=s: (0, jnp.clip(j - s * per, 0, per - 1)))
    return pl.pallas_call(
        body, name="conv_bwd", grid=(nb,),
        in_specs=[pl.BlockSpec((S, _CONV_COLS), lambda j: (0, off + j)), pl.BlockSpec((4, _CONV_COLS), lambda j: (0, j)),
                  dsp(0), dsp(1), dsp(2)],
        out_specs=[pl.BlockSpec((S, _CONV_COLS), lambda j: (0, j)), pl.BlockSpec((4, _CONV_COLS), lambda j: (0, j))],
        out_shape=[jax.ShapeDtypeStruct((S, 3072), BF16), jax.ShapeDtypeStruct((4, 3072), F32)],
        compiler_params=_cparams(("parallel",)),
    )(proj, conv_w, dq, dk, dv)


_S5_T = 512
_S5_L = 512
_S5_NB = 16


def _s5_tile(xr, xi, pw_r, pw_i, cr, ci, reverse):
    t8 = _iota((8, 1), 0)
    for sh in (1, 2, 4):
        row = (8 - sh) if reverse else (sh - 1)
        ar, ai = pw_r[row:row + 1, :], pw_i[row:row + 1, :]
        if reverse:
            keep, amt = t8 < 8 - sh, 8 - sh
        else:
            keep, amt = t8 >= sh, sh
        sr = jnp.where(keep, pltpu.roll(xr, amt, 0), 0.0)
        si = jnp.where(keep, pltpu.roll(xi, amt, 0), 0.0)
        xr, xi = xr + ar * sr - ai * si, xi + ar * si + ai * sr
    xr, xi = xr + pw_r * cr - pw_i * ci, xi + pw_r * ci + pw_i * cr
    return xr, xi


def _s5_fwd(u, bre, bim, cre, cim, pw_r, pw_i, dskip):
    S = u.shape[0]
    T = min(_S5_T, S)
    nt = S // T

    def body(u_ref, bre_ref, bim_ref, cre_ref, cim_ref, pr_ref, pi_ref, d_ref, y_ref, xr_ref, xi_ref,
             bu_r, bu_i, car_r, car_i):
        @pl.when(pl.program_id(1) == 0)
        def _():
            car_r[...] = jnp.zeros_like(car_r)
            car_i[...] = jnp.zeros_like(car_i)

        uu = u_ref[...]
        bu_r[...] = _dot(uu, bre_ref[...])
        bu_i[...] = _dot(uu, bim_ref[...])
        pw_r, pw_i = pr_ref[...], pi_ref[...]

        def tile(i, carry):
            r0 = pl.multiple_of(i * 8, 8)
            xr, xi = _s5_tile(bu_r[pl.ds(r0, 8), :], bu_i[pl.ds(r0, 8), :], pw_r, pw_i, carry[0], carry[1], False)
            xr_ref[pl.ds(r0, 8), :] = xr
            xi_ref[pl.ds(r0, 8), :] = xi
            return xr[7:8, :], xi[7:8, :]

        cr, ci = lax.fori_loop(0, T // 8, tile, (car_r[...], car_i[...]))
        car_r[...] = cr
        car_i[...] = ci
        y_ref[...] = _dot(xr_ref[...], cre_ref[...]) - _dot(xi_ref[...], cim_ref[...]) + d_ref[...] * uu

    blk3 = lambda a, b: pl.BlockSpec((None, a, b), lambda j, t: (j, 0, 0))
    return pl.pallas_call(
        body, name="s5_fwd", grid=(_S5_NB, nt),
        in_specs=[pl.BlockSpec((T, HEAD), lambda j, t: (t, j)),
                  blk3(HEAD, _S5_L), blk3(HEAD, _S5_L), blk3(_S5_L, HEAD), blk3(_S5_L, HEAD),
                  blk3(8, _S5_L), blk3(8, _S5_L), pl.BlockSpec((1, HEAD), lambda j, t: (0, j))],
        out_specs=[pl.BlockSpec((T, HEAD), lambda j, t: (t, j)),
                   pl.BlockSpec((T, _S5_L), lambda j, t: (t, j)), pl.BlockSpec((T, _S5_L), lambda j, t: (t, j))],
        out_shape=[jax.ShapeDtypeStruct((S, D_MODEL), F32),
                   jax.ShapeDtypeStruct((S, _S5_NB * _S5_L), F32), jax.ShapeDtypeStruct((S, _S5_NB * _S5_L), F32)],
        scratch_shapes=[pltpu.VMEM((T, _S5_L), F32), pltpu.VMEM((T, _S5_L), F32),
                        pltpu.VMEM((1, _S5_L), F32), pltpu.VMEM((1, _S5_L), F32)],
        compiler_params=_cparams(("parallel", "arbitrary")),
    )(u, bre, bim, cre, cim, pw_r, pw_i, dskip)


def _s5_bwd(dy, u, xre, xim, bre, bim, cre, cim, qw_r, qw_i, dskip):
    S = u.shape[0]
    T = min(_S5_T, S)
    nt = S // T
    nt8 = T // 8

    def body(dy_ref, u_ref, xr_ref, xi_ref, xpr_ref, xpi_ref, bre_ref, bim_ref, cre_ref, cim_ref, qr_ref, qi_ref,
             d_ref, du_ref, dbr_ref, dbi_ref, dcr_ref, dci_ref, dlr_ref, dli_ref, dd_ref,
             g_r, g_i, car_r, car_i):
        t = pl.program_id(1)

        @pl.when(t == 0)
        def _():
            car_r[...] = jnp.zeros_like(car_r)
            car_i[...] = jnp.zeros_like(car_i)
            for r in (dbr_ref, dbi_ref, dcr_ref, dci_ref, dlr_ref, dli_ref, dd_ref):
                r[...] = jnp.zeros_like(r)

        dyy, uu = dy_ref[...], u_ref[...]
        g_r[...] = _dot(dyy, cre_ref[...], _NT)
        g_i[...] = -_dot(dyy, cim_ref[...], _NT)
        qw_r, qw_i = qr_ref[...], qi_ref[...]
        t8 = _iota((8, 1), 0)
        first = t == nt - 1

        def tile(r0, prev_r, prev_i, carry, acc):
            gr, gi = _s5_tile(g_r[pl.ds(r0, 8), :], g_i[pl.ds(r0, 8), :], qw_r, qw_i, carry[0], carry[1], True)
            g_r[pl.ds(r0, 8), :] = gr
            g_i[pl.ds(r0, 8), :] = gi
            xpr = jnp.where(t8 >= 1, pltpu.roll(xr_ref[pl.ds(r0, 8), :], 1, 0), prev_r)
            xpi = jnp.where(t8 >= 1, pltpu.roll(xi_ref[pl.ds(r0, 8), :], 1, 0), prev_i)
            acc = (acc[0] + gr * xpr + gi * xpi, acc[1] + gi * xpr - gr * xpi)
            return (gr[0:1, :], gi[0:1, :]), acc

        def step(ii, state):
            carry, acc = state
            r0 = pl.multiple_of((nt8 - 1 - ii) * 8, 8)
            p0 = pl.multiple_of((nt8 - 2 - ii) * 8, 8)
            prev_r = xr_ref[pl.ds(p0, 8), :][7:8, :]
            prev_i = xi_ref[pl.ds(p0, 8), :][7:8, :]
            return tile(r0, prev_r, prev_i, carry, acc)

        zero = jnp.zeros((8, _S5_L), F32)
        state = lax.fori_loop(0, nt8 - 1, step, ((car_r[...], car_i[...]), (zero, zero)))
        prev_r = jnp.where(first, 0.0, xpr_ref[...][7:8, :])
        prev_i = jnp.where(first, 0.0, xpi_ref[...][7:8, :])
        (cr, ci), (ar, ai) = tile(0, prev_r, prev_i, *state)
        car_r[...] = cr
        car_i[...] = ci
        dlr_ref[...] += ar
        dli_ref[...] += ai
        gr, gi = g_r[...], g_i[...]
        du_ref[...] = _dot(gr, bre_ref[...], _NT) + _dot(gi, bim_ref[...], _NT) + d_ref[...] * dyy
        dbr_ref[...] += _dot(uu, gr, _TN)
        dbi_ref[...] += _dot(uu, gi, _TN)
        dcr_ref[...] += _dot(xr_ref[...], dyy, _TN)
        dci_ref[...] -= _dot(xi_ref[...], dyy, _TN)
        dd_ref[...] += jnp.sum(dyy * uu, axis=0, keepdims=True)

    rev = lambda t: nt - 1 - t
    blk3 = lambda a, b: pl.BlockSpec((None, a, b), lambda j, t: (j, 0, 0))
    tl = pl.BlockSpec((T, HEAD), lambda j, t: (rev(t), j))
    xs = pl.BlockSpec((T, _S5_L), lambda j, t: (rev(t), j))
    xp = pl.BlockSpec((8, _S5_L), lambda j, t: (jnp.maximum(rev(t) * nt8 - 1, 0), j))
    return pl.pallas_call(
        body, name="s5_bwd", grid=(_S5_NB, nt),
        in_specs=[tl, tl, xs, xs, xp, xp, blk3(HEAD, _S5_L), blk3(HEAD, _S5_L), blk3(_S5_L, HEAD), blk3(_S5_L, HEAD),
                  blk3(8, _S5_L), blk3(8, _S5_L), pl.BlockSpec((1, HEAD), lambda j, t: (0, j))],
        out_specs=[tl, blk3(HEAD, _S5_L), blk3(HEAD, _S5_L), blk3(_S5_L, HEAD), blk3(_S5_L, HEAD),
                   blk3(8, _S5_L), blk3(8, _S5_L), pl.BlockSpec((1, HEAD), lambda j, t: (0, j))],
        out_shape=[jax.ShapeDtypeStruct((S, D_MODEL), F32),
                   jax.ShapeDtypeStruct((_S5_NB, HEAD, _S5_L), F32), jax.ShapeDtypeStruct((_S5_NB, HEAD, _S5_L), F32),
                   jax.ShapeDtypeStruct((_S5_NB, _S5_L, HEAD), F32), jax.ShapeDtypeStruct((_S5_NB, _S5_L, HEAD), F32),
                   jax.ShapeDtypeStruct((_S5_NB, 8, _S5_L), F32), jax.ShapeDtypeStruct((_S5_NB, 8, _S5_L), F32),
                   jax.ShapeDtypeStruct((1, D_MODEL), F32)],
        scratch_shapes=[pltpu.VMEM((T, _S5_L), F32), pltpu.VMEM((T, _S5_L), F32),
                        pltpu.VMEM((1, _S5_L), F32), pltpu.VMEM((1, _S5_L), F32)],
        compiler_params=_cparams(("parallel", "arbitrary")),
    )(dy, u, xre, xim, xre, xim, bre, bim, cre, cim, qw_r, qw_i, dskip)


def _s5_params(a_re, a_im, log_dt, b_re, b_im, c_re, c_im):
    step = jnp.exp(log_dt)[:, None]
    mag = jnp.exp(a_re * step)
    lr, li = mag * jnp.cos(a_im * step), mag * jnp.sin(a_im * step)
    den = a_re * a_re + a_im * a_im
    nr, ni = lr - 1.0, li
    kr, ki = (nr * a_re + ni * a_im) / den, (ni * a_re - nr * a_im) / den
    bbr = kr[..., None] * b_re - ki[..., None] * b_im
    bbi = kr[..., None] * b_im + ki[..., None] * b_re
    eye = jnp.eye(8, dtype=F32)

    def blk_b(bb):
        t = bb.reshape(_S5_NB, 8, 64, 16).transpose(0, 1, 3, 2)
        return (t[:, :, :, None, :] * eye[None, :, None, :, None]).reshape(_S5_NB, HEAD, _S5_L)

    def blk_c(cc):
        t = cc.reshape(_S5_NB, 8, 16, 64).transpose(0, 1, 3, 2)
        return (t[:, :, :, None, :] * eye[None, :, None, :, None]).reshape(_S5_NB, _S5_L, HEAD)

    return (blk_b(bbr), blk_b(bbi), blk_c(c_re), blk_c(c_im),
            lr.reshape(_S5_NB, _S5_L), li.reshape(_S5_NB, _S5_L))


def _s5_powers(lr, li):
    pr, pi = [lr], [li]
    for _ in range(7):
        pr, pi = pr + [pr[-1] * lr - pi[-1] * li], pi + [pr[-1] * li + pi[-1] * lr]
    return jnp.stack(pr, axis=1), jnp.stack(pi, axis=1)


_MESH = pl.DeviceIdType.MESH
_ANY = pl.BlockSpec(memory_space=pl.ANY)


def _place():
    x, y, c = lax.axis_index("x"), lax.axis_index("y"), lax.axis_index("c")
    return x, y, c, [(1 - x, y), (x, 1 - y), (1 - x, 1 - y)]


def _comm_call(body, arrs, out_shapes, n_remote, name):
    n = len(arrs)
    return pl.pallas_call(
        body, name=name,
        in_specs=[_ANY] * n, out_specs=[_ANY] * n, out_shape=out_shapes,
        scratch_shapes=[pltpu.SemaphoreType.DMA((n * n_remote,)), pltpu.SemaphoreType.DMA((n * n_remote,)),
                        pltpu.SemaphoreType.DMA((n,))],
    )(*arrs)


def _ag_xy(arrs, name):
    n = len(arrs)

    def body(*refs):
        ins, outs, (ssem, rsem, lsem) = refs[:n], refs[n:2 * n], refs[2 * n:]
        x, y, c, chips = _place()
        me = 2 * x + y
        copies = []
        for a in range(n):
            loc = pltpu.make_async_copy(ins[a], outs[a].at[me], lsem.at[a])
            loc.start()
            copies.append(loc)
            for j, (px, py) in enumerate(chips):
                cp = pltpu.make_async_remote_copy(src_ref=ins[a], dst_ref=outs[a].at[me], send_sem=ssem.at[3 * a + j],
                                                  recv_sem=rsem.at[3 * a + j], device_id=(px, py, c), device_id_type=_MESH)
                cp.start()
                copies.append(cp)
        for cp in copies:
            cp.wait()

    return _comm_call(body, arrs, [jax.ShapeDtypeStruct((4,) + a.shape, a.dtype) for a in arrs], 3, name)


def _rs_xy(arrs, name):
    n = len(arrs)

    def body(*refs):
        ins, outs, (ssem, rsem, lsem) = refs[:n], refs[n:2 * n], refs[2 * n:]
        x, y, c, chips = _place()
        me = 2 * x + y
        copies = []
        for a in range(n):
            loc = pltpu.make_async_copy(ins[a].at[me], outs[a].at[3], lsem.at[a])
            loc.start()
            copies.append(loc)
            for j, (px, py) in enumerate(chips):
                cp = pltpu.make_async_remote_copy(src_ref=ins[a].at[2 * px + py], dst_ref=outs[a].at[j],
                                                  send_sem=ssem.at[3 * a + j], recv_sem=rsem.at[3 * a + j],
                                                  device_id=(px, py, c), device_id_type=_MESH)
                cp.start()
                copies.append(cp)
        for cp in copies:
            cp.wait()

    return _comm_call(body, arrs, [jax.ShapeDtypeStruct(a.shape, a.dtype) for a in arrs], 3, name)


def _sibling_swap(arrs, name):
    n = len(arrs)

    def body(*refs):
        ins, outs, (ssem, rsem, _) = refs[:n], refs[n:2 * n], refs[2 * n:]
        x, y, c, _ = _place()
        copies = []
        for a in range(n):
            cp = pltpu.make_async_remote_copy(src_ref=ins[a], dst_ref=outs[a], send_sem=ssem.at[a], recv_sem=rsem.at[a],
                                              device_id=(x, y, 1 - c), device_id_type=_MESH)
            cp.start()
            copies.append(cp)
        for cp in copies:
            cp.wait()

    return _comm_call(body, arrs, [jax.ShapeDtypeStruct(a.shape, a.dtype) for a in arrs], 1, name)


def _gather8(buf, name):
    def body(in_ref, out_ref, ssem, rsem, lsem):
        x, y, c, _ = _place()
        me = 4 * x + 2 * y + c
        loc = pltpu.make_async_copy(in_ref, out_ref.at[me], lsem.at[0])
        loc.start()
        copies = [loc]
        for d in range(1, 8):
            dx, dy, dc = (d >> 2) & 1, (d >> 1) & 1, d & 1
            peer = (1 - x if dx else x, 1 - y if dy else y, 1 - c if dc else c)
            cp = pltpu.make_async_remote_copy(src_ref=in_ref, dst_ref=out_ref.at[me], send_sem=ssem.at[d - 1],
                                              recv_sem=rsem.at[d - 1], device_id=peer, device_id_type=_MESH)
            cp.start()
            copies.append(cp)
        for cp in copies:
            cp.wait()

    return _comm_call(body, [buf], [jax.ShapeDtypeStruct((8,) + buf.shape, buf.dtype)], 7, name)[0]


def _sum_slots(arr, name, rows=128):
    k, R, C = arr.shape
    rows = min(rows, R)
    assert R % rows == 0

    def body(in_ref, o_ref):
        acc = in_ref[0].astype(F32)
        for s in range(1, k):
            acc = acc + in_ref[s].astype(F32)
        o_ref[...] = acc

    return pl.pallas_call(
        body, name=name, grid=(R // rows,),
        in_specs=[pl.BlockSpec((k, rows, C), lambda i: (0, i, 0))],
        out_specs=pl.BlockSpec((rows, C), lambda i: (i, 0)),
        out_shape=jax.ShapeDtypeStruct((R, C), F32),
        compiler_params=_cparams(("parallel",)),
    )(arr)


ADAM_LR, ADAM_B1, ADAM_B2, ADAM_EPS, ADAM_WD, ADAM_STEP = 0.001, 0.9, 0.999, 1e-08, 0.01, 10


def _adam_math(w, g, m, v):
    m = ADAM_B1 * m + (1.0 - ADAM_B1) * g
    v = ADAM_B2 * v + (1.0 - ADAM_B2) * jnp.square(g)
    m_hat = m / (1.0 - ADAM_B1 ** ADAM_STEP)
    v_hat = v / (1.0 - ADAM_B2 ** ADAM_STEP)
    delta = -ADAM_LR * (m_hat / (jnp.sqrt(v_hat) + ADAM_EPS) + ADAM_WD * w)
    return delta, m, v


def _adamw(w, m, v, gparts, name, g_off=0, rows=128):
    R, C = w.shape
    rows = min(rows, R)
    assert R % rows == 0 and g_off % rows == 0
    ng = len(gparts)

    def body(w_ref, m_ref, v_ref, *rest):
        g_refs, (go, do, mo, vo) = rest[:ng], rest[ng:]
        g = g_refs[0][...]
        for r in g_refs[1:]:
            g = g + r[...]
        delta, m1, v1 = _adam_math(w_ref[...], g, m_ref[...], v_ref[...])
        go[...] = g
        do[...] = delta
        mo[...] = m1
        vo[...] = v1

    blk = pl.BlockSpec((rows, C), lambda i: (i, 0))
    return pl.pallas_call(
        body, name=name, grid=(R // rows,),
        in_specs=[blk] * 3 + [pl.BlockSpec((rows, C), lambda i: (g_off // rows + i, 0))] * ng,
        out_specs=[blk] * 4, out_shape=[jax.ShapeDtypeStruct((R, C), F32)] * 4,
        compiler_params=_cparams(("parallel",)),
    )(w, m, v, *gparts)


_WEIGHTS = ['norm_mix', 'norm_mlp', 'norm_ple', 'w_in_e', 'w_out_e', 'hgrn_lb', 'g_norm_a', 'conv_w', 'a_log',
            'dt_bias', 'g_norm_b', 's5_a_re', 's5_a_im', 's5_b_re', 's5_b_im', 's5_c_re', 's5_c_im', 's5_d',
            's5_log_dt', 'w_glu', 'b_glu', 'w_out_o', 'w_up', 'w_down', 'w_ple_gate', 'w_ple_proj', 'final_norm']
_INPUTS = ['x', 'p'] + _WEIGHTS + ['loss_target'] + ['m_' + n for n in _WEIGHTS] + ['v_' + n for n in _WEIGHTS]
_FAMILY = [('w_up', 'col', 2048), ('w_down', 'row', 2048), ('w_ple_gate', 'row', 512),
           ('w_out_e', 'row', 512), ('w_glu', 'row', 512), ('w_out_o', 'row', 512)]
_IN_PAD = 8320
_IN_BLK = 640


def _rms_bwd_fn(x, d_hn, d_res, g):
    _, vjp = jax.vjp(_rms, x, g)
    dx, dg = vjp(d_hn.astype(F32))
    return dx + d_res, dg


def _add_res(acc, h):
    return (acc + h,)


def _pack_rows(parts, lanes=128, mult=256):
    flat = jnp.concatenate([q.reshape(-1).astype(F32) for q in parts])
    n = flat.shape[0]
    rows = -(-n // (lanes * mult)) * mult
    return jnp.pad(flat, (0, rows * lanes - n)).reshape(rows, lanes)


def _unpack_rows(buf, shapes):
    flat, out, off = buf.reshape(-1), [], 0
    for s in shapes:
        n = math.prod(s)
        out.append(flat[off:off + n].reshape(s))
        off += n
    return out


def _step(a):
    S = a['x'].shape[1]
    x, tgt = a['x'][0], a['loss_target'][0]
    xi, yi = lax.axis_index("x"), lax.axis_index("y")
    me = 2 * xi + yi
    row = lambda t: t.reshape(1, -1)

    views, packs, off = {}, [], 0
    for nm, kind, r in _FAMILY:
        for l in range(a[nm].shape[0]):
            views[nm, l] = (kind, off, r)
            packs.append(a[nm][l].astype(BF16))
            off += r
    rt = off
    small_sh = jnp.concatenate([a['conv_w'][0].reshape(-1), a['s5_d'][0], a['b_glu'][0]]).reshape(8, 512)
    gp, g_in, g_pp, g_small = _ag_xy(
        [jnp.concatenate(packs, axis=0), a['w_in_e'][0].astype(BF16), a['w_ple_proj'].reshape(512, 512).astype(BF16),
         small_sh], name="gather_weights")
    win = jnp.pad(g_in.transpose(1, 0, 2).reshape(D_MODEL, 8208), ((0, 0), (0, _IN_PAD - 8208)))
    w_pp = g_pp.reshape(4, 2, 256, 512).transpose(1, 2, 0, 3).reshape(2, 256, D_MODEL)
    g_small = g_small.reshape(4, 4096)
    conv_full = g_small[:, :3072].reshape(4, 4, 768).transpose(1, 0, 2).reshape(4, 3072)
    s5_d_full = g_small[:, 3072:3584].reshape(1, D_MODEL)
    b_glu_full = g_small[:, 3584:].reshape(1, D_MODEL)

    lb, lb_vjp = jax.vjp(lambda hl: jax.nn.softmax(hl, axis=0)[0:1], a['hgrn_lb'])
    pad_h = lambda t: jnp.pad(t, ((0, 0), (0, HEAD - t.shape[1])))
    (arow, dtrow), gdn_vjp = jax.vjp(lambda al, dt: (pad_h(-jnp.exp(al)), pad_h(dt)), a['a_log'], a['dt_bias'])
    s5p, s5_vjp = jax.vjp(_s5_params, a['s5_a_re'][0], a['s5_a_im'][0], a['s5_log_dt'][0], a['s5_b_re'][0],
                          a['s5_b_im'][0], a['s5_c_re'][0], a['s5_c_im'][0])
    bre, bim, cre, cim = [t.astype(BF16) for t in s5p[:4]]
    pw_r, pw_i = _s5_powers(s5p[4], s5p[5])
    gna, gnb = a['g_norm_a'], a['g_norm_b']

    def norm_cast(h, g, dt, name):
        return _rowwise(_rms, [h], [row(g)], [(D_MODEL, dt)], name=name)

    def mlp_ple_fwd(h, l):
        hn = norm_cast(h, a['norm_mlp'][l], BF16, f"rms_mlp{l}")
        up, act = _mm(hn, gp, b_view=views['w_up', l], out_dtypes=(BF16, BF16), name=f"up{l}",
                      epilogue=lambda acc: (acc, jnp.square(jnp.maximum(acc, 0.0))))
        h2 = _mm(act, gp, b_view=views['w_down', l], epilogue=_add_res, extras=(h,), name=f"down{l}")
        hnp = norm_cast(h2, a['norm_ple'][l], BF16, f"rms_ple{l}")
        pp = _mm(a['p'][l, 0], w_pp[l], name=f"ple_proj{l}")
        h3, gpre = _mm(hnp, gp, b_view=views['w_ple_gate', l], extras=(h2, pp), out_dtypes=(F32, F32), name=f"ple_gate{l}",
                       bm=512, epilogue=lambda acc, h2, pp: (h2 + jax.nn.sigmoid(acc) * pp, acc))
        return h3, (h, hn, up, act, h2, hnp, pp, gpre)

    hn0 = norm_cast(x, a['norm_mix'][0], BF16, "rms_mix0")
    proj = _mm(hn0, win, bn=_IN_BLK, name="in_proj")
    o_a, st_a = _hgrn_fwd(proj, lb, gna)
    act_b = _conv_fwd(proj, conv_full)
    o_b, st_b = _gdn_fwd(act_b, proj, arow, dtrow, gnb)
    merged = jnp.concatenate([o_a, o_b], axis=1)
    h1 = _mm(merged, gp, b_view=views['w_out_e', 0], epilogue=_add_res, extras=(x,), name="out_e")
    h3, saved0 = mlp_ple_fwd(h1, 0)

    u = norm_cast(h3, a['norm_mix'][1], F32, "rms_mix1")
    y, xre, xim = _s5_fwd(u, bre, bim, cre, cim, pw_r, pw_i, s5_d_full)
    actg = _rowwise(jax.nn.gelu, [y], [], [(D_MODEL, BF16)], name="gelu")
    glu, z = _mm(actg, gp, b_view=views['w_glu', 0], extras=(actg, b_glu_full), out_dtypes=(BF16, F32), name="glu",
                 epilogue=lambda acc, act, b: (act.astype(F32) * jax.nn.sigmoid(acc + b), acc + b))
    h4 = _mm(glu, gp, b_view=views['w_out_o', 0], epilogue=_add_res, extras=(h3,), name="out_o")
    h6, saved1 = mlp_ple_fwd(h4, 1)

    def head(h, t, g):
        def f(h, g):
            e = _rms(h, g) - t
            return 0.5 * jnp.sum(jnp.mean(e * e, axis=-1))
        val, vjp = jax.vjp(f, h, g)
        dh, dg = vjp(jnp.ones((), F32))
        return dh, dg, jnp.full((1, HEAD), val)

    dh, d_final, loss_part = _rowwise(head, [h6, tgt], [row(a['final_norm'])], [(D_MODEL, F32)], sums=(D_MODEL, HEAD),
                                      name="loss_head")
    loss = lax.psum(loss_part[0, 0], ("x", "y", "c"))

    gbuf = jnp.zeros((4, rt, PACK_COLS), BF16)

    def into(gbuf, lhs, rhs, key, name):
        return _mm(lhs, rhs, dims="tn", out_into=gbuf, out_view=views[key], out_dtypes=(BF16,), name=name)

    def mlp_ple_bwd(dh, l, saved, gbuf):
        h, hn, up, act, h2, hnp, pp, gpre = saved

        def gate_bwd(d, gpre, pp):
            s = jax.nn.sigmoid(gpre)
            return d * s, d * pp * s * (1.0 - s)

        d_pp, d_gp = _rowwise(gate_bwd, [dh, gpre, pp], [], [(D_MODEL, BF16), (D_MODEL, BF16)], name=f"ple_bwd{l}")
        d_wpp = _mm(a['p'][l, 0], d_pp, dims="tn", name=f"d_ple_proj{l}")
        gbuf = into(gbuf, hnp, d_gp, ('w_ple_gate', l), f"d_ple_gate{l}")
        d_hnp = _mm(d_gp, gp, dims="nt", b_view=views['w_ple_gate', l], name=f"ple_gate_t{l}")
        dh2, d_nple = _rowwise(_rms_bwd_fn, [h2, d_hnp, dh], [row(a['norm_ple'][l])], [(D_MODEL, F32)], sums=(D_MODEL,),
                               name=f"rms_ple_bwd{l}")
        d_up = _mm(dh2, gp, dims="nt", b_view=views['w_down', l], extras=(up,), out_dtypes=(BF16,), name=f"down_t{l}",
                   epilogue=lambda acc, up: (acc * 2.0 * jnp.maximum(up.astype(F32), 0.0),))
        gbuf = into(gbuf, act, dh2, ('w_down', l), f"d_down{l}")
        gbuf = into(gbuf, hn, d_up, ('w_up', l), f"d_up{l}")
        d_hn = _mm(d_up, gp, dims="nt", b_view=views['w_up', l], name=f"up_t{l}")
        dh1, d_nmlp = _rowwise(_rms_bwd_fn, [h, d_hn, dh2], [row(a['norm_mlp'][l])], [(D_MODEL, F32)], sums=(D_MODEL,),
                               name=f"rms_mlp_bwd{l}")
        return dh1, gbuf, d_wpp, d_nple, d_nmlp

    dh4, gbuf, d_wpp1, d_nple1, d_nmlp1 = mlp_ple_bwd(dh, 1, saved1, gbuf)

    d_glu = _mm(dh4, gp, dims="nt", b_view=views['w_out_o', 0], name="out_o_t")
    gbuf = into(gbuf, glu, dh4, ('w_out_o', 0), "d_out_o")

    def glu_bwd(d, z, act):
        s, act = jax.nn.sigmoid(z), act.astype(F32)
        dz = d * act * s * (1.0 - s)
        return dz, d * s, jnp.sum(dz, axis=0, keepdims=True)

    d_z, d_actp, d_bglu = _rowwise(glu_bwd, [d_glu, z, actg], [], [(D_MODEL, BF16), (D_MODEL, F32)], sums=(D_MODEL,),
                                   name="glu_bwd")
    gbuf = into(gbuf, actg, d_z, ('w_glu', 0), "d_glu")

    def gelu_bwd(acc, dap, y):
        _, vjp = jax.vjp(jax.nn.gelu, y)
        return vjp(acc + dap)

    dy = _mm(d_z, gp, dims="nt", b_view=views['w_glu', 0], extras=(d_actp, y), epilogue=gelu_bwd, name="glu_t")
    du, d_bre, d_bim, d_cre, d_cim, d_lr, d_li, d_s5d = _s5_bwd(dy, u, xre, xim, bre, bim, cre, cim,
                                                                 pw_r[:, ::-1], -pw_i[:, ::-1], s5_d_full)
    dh3, d_nmix1 = _rowwise(_rms_bwd_fn, [h3, du, dh4], [row(a['norm_mix'][1])], [(D_MODEL, F32)], sums=(D_MODEL,),
                            name="rms_mix1_bwd")
    d_are, d_aim, d_logdt, d_sbre, d_sbim, d_scre, d_scim = s5_vjp(
        (d_bre, d_bim, d_cre, d_cim, d_lr.sum(axis=1), d_li.sum(axis=1)))

    dh1, gbuf, d_wpp0, d_nple0, d_nmlp0 = mlp_ple_bwd(dh3, 0, saved0, gbuf)

    d_merged = _mm(dh1, gp, dims="nt", b_view=views['w_out_e', 0], name="out_e_t")
    gbuf = into(gbuf, merged, dh1, ('w_out_e', 0), "d_out_e")
    dq, df, dv, dg, d_lb, d_gna = _hgrn_bwd(proj, lb, gna, st_a, d_merged)
    dqb, dkb, dvb, dzb, d_tail, d_arow, d_dtrow, d_gnb = _gdn_bwd(act_b, proj, arow, dtrow, gnb, st_b, d_merged)
    d_pre, d_conv = _conv_bwd(proj, conv_full, dqb, dkb, dvb)
    d_proj = jnp.concatenate([dq, df, dv, dg, d_pre, dzb, d_tail], axis=1)
    d_win = _mm(hn0, d_proj, dims="tn", bn=_IN_BLK, out_dtypes=(BF16,), name="d_in_proj")
    d_hn0 = _mm(d_proj, win, dims="nt", bk=_IN_BLK, name="in_proj_t")
    grad_x, d_nmix0 = _rowwise(_rms_bwd_fn, [x, d_hn0, dh1], [row(a['norm_mix'][0])], [(D_MODEL, F32)], sums=(D_MODEL,),
                               name="rms_mix0_bwd")
    (d_hlb,) = lb_vjp(d_lb.reshape(1, 1024))
    d_alog, d_dtb = gdn_vjp((d_arow, d_dtrow))

    d_win_sh = d_win[:, :8208].reshape(D_MODEL, 4, 2052).transpose(1, 0, 2)
    d_wpp_sh = jnp.stack([d_wpp0, d_wpp1]).astype(BF16).reshape(2, 256, 4, 512).transpose(2, 0, 1, 3).reshape(4, 512, 512)
    parts = [_sum_slots(t, name=f"sum_partials{i}")
             for i, t in enumerate(_rs_xy([gbuf, d_win_sh, d_wpp_sh], name="scatter_grads"))]
    others = _sibling_swap(parts, name="swap_sums")

    small = {
        'norm_mix': jnp.concatenate([d_nmix0, d_nmix1]), 'norm_mlp': jnp.concatenate([d_nmlp0, d_nmlp1]),
        'norm_ple': jnp.concatenate([d_nple0, d_nple1]), 'hgrn_lb': d_hlb, 'g_norm_a': d_gna, 'conv_w': d_conv,
        'a_log': d_alog, 'dt_bias': d_dtb, 'g_norm_b': d_gnb, 's5_a_re': d_are, 's5_a_im': d_aim, 's5_b_re': d_sbre,
        's5_b_im': d_sbim, 's5_c_re': d_scre, 's5_c_im': d_scim, 's5_d': d_s5d, 's5_log_dt': d_logdt,
        'b_glu': d_bglu, 'final_norm': d_final}
    s_names = list(small)
    s_shapes = [tuple(small[n].shape) for n in s_names]
    reduced = _sum_slots(_gather8(_pack_rows([small[n] for n in s_names]), name="gather_small"), name="sum_small")
    sg = dict(zip(s_names, _unpack_rows(reduced, s_shapes)))
    sg['conv_w'] = lax.dynamic_slice_in_dim(sg['conv_w'], me * 768, 768, axis=1)
    sg['s5_d'] = lax.dynamic_slice_in_dim(sg['s5_d'], me * 512, 512, axis=1)
    sg['b_glu'] = lax.dynamic_slice_in_dim(sg['b_glu'], me * 512, 512, axis=1)
    sg = {n: sg[n].reshape(a[n].shape) for n in s_names}
    w_pack, m_pack, v_pack, g_pack = [_pack_rows([src[pre + n] for n in s_names])
                                      for src, pre in ((a, ''), (a, 'm_'), (a, 'v_'), (sg, ''))]
    sd, sm, sv = _rowwise(_adam_math, [w_pack, g_pack, m_pack, v_pack], [], [(128, F32)] * 3, name="adam_small")
    w_shapes = [tuple(a[n].shape) for n in s_names]
    res = {n: (sg[n], d_, m_, v_) for n, d_, m_, v_ in zip(s_names, _unpack_rows(sd, w_shapes), _unpack_rows(sm, w_shapes),
                                                          _unpack_rows(sv, w_shapes))}

    def adam_big(nm, gparts, g_off, shape2d):
        outs = _adamw(a[nm].reshape(shape2d), a['m_' + nm].reshape(shape2d), a['v_' + nm].reshape(shape2d), gparts,
                      name="adam_" + nm, g_off=g_off)
        res[nm] = tuple(o.reshape(a[nm].shape) for o in outs)

    for nm, _, r in _FAMILY:
        adam_big(nm, [parts[0], others[0]], views[nm, 0][1], (a[nm].shape[0] * r, PACK_COLS))
    adam_big('w_in_e', [parts[1], others[1]], 0, (D_MODEL, 2052))
    adam_big('w_ple_proj', [parts[2], others[2]], 0, (512, 512))

    return (loss, grad_x[None], *[res[n][0] for n in _WEIGHTS], *[res[n][1] for n in _WEIGHTS],
            *[res[n][2] for n in _WEIGHTS], *[res[n][3] for n in _WEIGHTS])


def kernel(x, p, norm_mix, norm_mlp, norm_ple, w_in_e, w_out_e, hgrn_lb, g_norm_a, conv_w, a_log, dt_bias, g_norm_b, s5_a_re, s5_a_im, s5_b_re, s5_b_im, s5_c_re, s5_c_im, s5_d, s5_log_dt, w_glu, b_glu, w_out_o, w_up, w_down, w_ple_gate, w_ple_proj, final_norm, loss_target, m_norm_mix, m_norm_mlp, m_norm_ple, m_w_in_e, m_w_out_e, m_hgrn_lb, m_g_norm_a, m_conv_w, m_a_log, m_dt_bias, m_g_norm_b, m_s5_a_re, m_s5_a_im, m_s5_b_re, m_s5_b_im, m_s5_c_re, m_s5_c_im, m_s5_d, m_s5_log_dt, m_w_glu, m_b_glu, m_w_out_o, m_w_up, m_w_down, m_w_ple_gate, m_w_ple_proj, m_final_norm, v_norm_mix, v_norm_mlp, v_norm_ple, v_w_in_e, v_w_out_e, v_hgrn_lb, v_g_norm_a, v_conv_w, v_a_log, v_dt_bias, v_g_norm_b, v_s5_a_re, v_s5_a_im, v_s5_b_re, v_s5_b_im, v_s5_c_re, v_s5_c_im, v_s5_d, v_s5_log_dt, v_w_glu, v_b_glu, v_w_out_o, v_w_up, v_w_down, v_w_ple_gate, v_w_ple_proj, v_final_norm):
    args = (x, p, norm_mix, norm_mlp, norm_ple, w_in_e, w_out_e, hgrn_lb, g_norm_a, conv_w, a_log, dt_bias, g_norm_b, s5_a_re, s5_a_im, s5_b_re, s5_b_im, s5_c_re, s5_c_im, s5_d, s5_log_dt, w_glu, b_glu, w_out_o, w_up, w_down, w_ple_gate, w_ple_proj, final_norm, loss_target, m_norm_mix, m_norm_mlp, m_norm_ple, m_w_in_e, m_w_out_e, m_hgrn_lb, m_g_norm_a, m_conv_w, m_a_log, m_dt_bias, m_g_norm_b, m_s5_a_re, m_s5_a_im, m_s5_b_re, m_s5_b_im, m_s5_c_re, m_s5_c_im, m_s5_d, m_s5_log_dt, m_w_glu, m_b_glu, m_w_out_o, m_w_up, m_w_down, m_w_ple_gate, m_w_ple_proj, m_final_norm, v_norm_mix, v_norm_mlp, v_norm_ple, v_w_in_e, v_w_out_e, v_hgrn_lb, v_g_norm_a, v_conv_w, v_a_log, v_dt_bias, v_g_norm_b, v_s5_a_re, v_s5_a_im, v_s5_b_re, v_s5_b_im, v_s5_c_re, v_s5_c_im, v_s5_d, v_s5_log_dt, v_w_glu, v_b_glu, v_w_out_o, v_w_up, v_w_down, v_w_ple_gate, v_w_ple_proj, v_final_norm)
    return _step(dict(zip(_INPUTS, args)))
```

```python
import functools
import math

import jax
import jax.numpy as jnp
from jax import lax
from jax.experimental import pallas as pl
from jax.experimental.pallas import tpu as pltpu

F32 = jnp.float32
BF16 = jnp.bfloat16

D_MODEL = 2048
SEQ = 4096
NORM_EPS = 1e-6
CHUNK = 64
HEAD = 128
VMEM_LIMIT = 56 * 1024 * 1024


_NN = (((1,), (0,)), ((), ()))
_NT = (((1,), (1,)), ((), ()))
_TN = (((0,), (0,)), ((), ()))
_HI = lax.Precision.HIGHEST
_NEG = -1e30


def _cparams(sem, **kw):
    return pltpu.CompilerParams(dimension_semantics=sem, vmem_limit_bytes=VMEM_LIMIT, **kw)


PACK_COLS = 2048


def _view_shape(view):
    kind, _, r = view
    return (4 * r, PACK_COLS) if kind == "row" else (r, 4 * PACK_COLS)


def _view_spec(view, rb, cb, row_of, col_of):
    kind, off, r = view
    assert off % rb == 0 and r % rb == 0 and PACK_COLS % cb == 0, (view, rb, cb)
    if kind == "row":
        nrb = r // rb
        return pl.BlockSpec((None, rb, cb), lambda i, j, k: (row_of(i, j, k) // nrb,
                                                             off // rb + row_of(i, j, k) % nrb, col_of(i, j, k)))
    ncb = PACK_COLS // cb
    return pl.BlockSpec((None, rb, cb), lambda i, j, k: (col_of(i, j, k) // ncb,
                                                         off // rb + row_of(i, j, k), col_of(i, j, k) % ncb))


def _mm(a, b, *, dims="nn", epilogue=None, extras=(), out_dtypes=(F32,), bm=1024, bn=1024, bk=1024, name,
        b_view=None, out_into=None, out_view=None):
    b_shape = _view_shape(b_view) if b_view is not None else b.shape
    if dims == "tn":
        (K, M), (K2, N) = a.shape, b_shape
    elif dims == "nt":
        (M, K), (N, K2) = a.shape, b_shape
    else:
        (M, K), (K2, N) = a.shape, b_shape
    assert K == K2, (a.shape, b_shape, dims)
    if b_view is not None and dims == "nt":
        bn = min(bn, b_view[2])
    if b_view is not None and dims != "nt":
        bk = min(bk, b_view[2])
    if out_view is not None:
        bm = min(bm, out_view[2])
    bm, bn, bk = min(bm, M), min(bn, N), min(bk, K)
    assert M % bm == 0 and N % bn == 0 and K % bk == 0, (M, N, K, bm, bn, bk)
    nk = K // bk
    ii, jj, kk = (lambda i, j, k: i), (lambda i, j, k: j), (lambda i, j, k: k)
    if dims == "tn":
        a_spec = pl.BlockSpec((bk, bm), lambda i, j, k: (k, i))
        dn = _TN
    else:
        a_spec = pl.BlockSpec((bm, bk), lambda i, j, k: (i, k))
        dn = _NT if dims == "nt" else _NN
    if dims == "nt":
        b_spec = _view_spec(b_view, bn, bk, jj, kk) if b_view else pl.BlockSpec((bn, bk), lambda i, j, k: (j, k))
    else:
        b_spec = _view_spec(b_view, bk, bn, kk, jj) if b_view else pl.BlockSpec((bk, bn), lambda i, j, k: (k, j))
    e_specs = []
    for e in extras:
        if e.shape == (M, N):
            e_specs.append(pl.BlockSpec((bm, bn), lambda i, j, k: (i, j)))
        else:
            assert e.shape == (1, N), e.shape
            e_specs.append(pl.BlockSpec((1, bn), lambda i, j, k: (0, j)))
    ne, no = len(extras), len(out_dtypes)
    if epilogue is None:
        epilogue = lambda acc: (acc,)
    into = out_into is not None

    def body(a_ref, b_ref, *rest):
        e_refs, rest = rest[:ne], rest[ne + (1 if into else 0):]
        o_refs, acc = rest[:no], rest[no]
        k = pl.program_id(2)

        @pl.when(k == 0)
        def _():
            acc[...] = jnp.zeros_like(acc)

        acc[...] += lax.dot_general(a_ref[...].astype(BF16), b_ref[...].astype(BF16), dn,
                                    preferred_element_type=F32)

        @pl.when(k == nk - 1)
        def _():
            outs = epilogue(acc[...], *[e[...] for e in e_refs])
            for o, v in zip(o_refs, outs):
                o[...] = v.astype(o.dtype)

    if into:
        assert no == 1 and _view_shape(out_view) == (M, N), (out_view, M, N)
        out_specs = [_view_spec(out_view, bm, bn, ii, jj)]
        out_shape = [jax.ShapeDtypeStruct(out_into.shape, out_into.dtype)]
        extra_in, extra_specs, alias = [out_into], [pl.BlockSpec(memory_space=pl.ANY)], {2 + ne: 0}
    else:
        out_specs = [pl.BlockSpec((bm, bn), lambda i, j, k: (i, j)) for _ in out_dtypes]
        out_shape = [jax.ShapeDtypeStruct((M, N), dt) for dt in out_dtypes]
        extra_in, extra_specs, alias = [], [], {}
    outs = pl.pallas_call(
        body, name=name,
        grid=(M // bm, N // bn, nk),
        in_specs=[a_spec, b_spec] + e_specs + extra_specs,
        out_specs=out_specs, out_shape=out_shape,
        scratch_shapes=[pltpu.VMEM((bm, bn), F32)],
        input_output_aliases=alias,
        compiler_params=_cparams(("parallel", "parallel", "arbitrary")),
    )(a, b, *extras, *extra_in)
    return outs[0] if no == 1 else tuple(outs)


def _rowwise(fn, tiled, bcast, outs, sums=(), *, rows=256, name):
    S = tiled[0].shape[0]
    rows = min(rows, S)
    assert S % rows == 0
    nt, nb, no, ns = len(tiled), len(bcast), len(outs), len(sums)

    def body(*refs):
        t_refs, b_refs = refs[:nt], refs[nt:nt + nb]
        o_refs, s_refs = refs[nt + nb:nt + nb + no], refs[nt + nb + no:]
        res = fn(*[r[...] for r in t_refs], *[r[...] for r in b_refs])
        res = res if isinstance(res, (tuple, list)) else (res,)
        for o, v in zip(o_refs, res[:no]):
            o[...] = v.astype(o.dtype)
        if ns:
            @pl.when(pl.program_id(0) == 0)
            def _():
                for s in s_refs:
                    s[...] = jnp.zeros_like(s)
            for s, v in zip(s_refs, res[no:]):
                s[...] += v

    res = pl.pallas_call(
        body, name=name,
        grid=(S // rows,),
        in_specs=[pl.BlockSpec((rows, t.shape[1]), lambda i: (i, 0)) for t in tiled]
        + [pl.BlockSpec(b.shape, lambda i, nd=b.ndim: (0,) * nd) for b in bcast],
        out_specs=[pl.BlockSpec((rows, c), lambda i: (i, 0)) for c, _ in outs]
        + [pl.BlockSpec((1, c), lambda i: (0, 0)) for c in sums],
        out_shape=[jax.ShapeDtypeStruct((S, c), dt) for c, dt in outs]
        + [jax.ShapeDtypeStruct((1, c), F32) for c in sums],
        compiler_params=_cparams(("arbitrary",)),
    )(*tiled, *bcast)
    return res[0] if len(res) == 1 else tuple(res)


def _rms(x, g):
    return x * lax.rsqrt(jnp.mean(x * x, axis=-1, keepdims=True) + NORM_EPS) * g


def _mxu(a, b, dn):
    if a.ndim == 3:
        ((ca,), (cb,)), _ = dn
        dn = (((ca + 1,), (cb + 1,)), ((0,), (0,)))
    return lax.dot_general(a, b, dn, preferred_element_type=F32)


def _split(a):
    hi = a.astype(BF16)
    return hi, (a - hi.astype(F32)).astype(BF16)


def _passes(a, b, dn, three):
    if not three:
        return _mxu(a.astype(BF16), b.astype(BF16), dn)
    (ah, al), (bh, bl) = _split(a), _split(b)
    return _mxu(ah, bh, dn) + (_mxu(ah, bl, dn) + _mxu(al, bh, dn))


def _dot_grads(a, b, g, dn, three):
    if dn == _NN:
        return _passes(g, b, _NT, three), _passes(a, g, _TN, three)
    if dn == _NT:
        return _passes(g, b, _NN, three), _passes(g, a, _TN, three)
    assert dn == _TN
    return _passes(b, g, _NT, three), _passes(a, g, _NN, three)


@functools.partial(jax.custom_vjp, nondiff_argnums=(2,))
def _dot(a, b, dn=_NN):
    return _passes(a, b, dn, False)


_dot.defvjp(lambda a, b, dn: (_passes(a, b, dn, False), (a, b)),
            lambda dn, res, g: _dot_grads(res[0], res[1], g, dn, False))


@functools.partial(jax.custom_vjp, nondiff_argnums=(2,))
def _dot3(a, b, dn=_NN):
    return _passes(a, b, dn, True)


_dot3.defvjp(lambda a, b, dn: (_passes(a, b, dn, True), (a, b)),
             lambda dn, res, g: _dot_grads(res[0], res[1], g, dn, True))


def _tri_times(x, dn):
    tri = _tril(x.shape[-2]).astype(BF16)
    if x.ndim == 3:
        tri = jnp.broadcast_to(tri, (x.shape[0],) + tri.shape)
    hi, lo = _split(x)
    lo2 = (x - hi.astype(F32) - lo.astype(F32)).astype(BF16)
    return _mxu(tri, hi, dn) + (_mxu(tri, lo, dn) + _mxu(tri, lo2, dn))


@jax.custom_vjp
def _cumsum_rows(x):
    return _tri_times(x, _NN)


_cumsum_rows.defvjp(lambda x: (_tri_times(x, _NN), None), lambda _, g: (_tri_times(g, _TN),))


def _iota(shape, dim):
    return lax.broadcasted_iota(jnp.int32, shape, dim)


def _tril(n, strict=False):
    r, c = _iota((n, n), 0), _iota((n, n), 1)
    return (r > c) if strict else (r >= c)


@functools.partial(jax.custom_vjp, nondiff_argnums=(1,))
def _roll_rows(x, r):
    return pltpu.roll(x, r, 0)


def _roll_rows_fwd(x, r):
    return pltpu.roll(x, r, 0), None


def _roll_rows_bwd(r, _, g):
    return (pltpu.roll(g, (g.shape[0] - r) % g.shape[0], 0),)


_roll_rows.defvjp(_roll_rows_fwd, _roll_rows_bwd)


def _head_norm_gate(o, gn, gate):
    return _rms(o, gn) * jax.nn.silu(gate)


_SUB = 16
_HPS = 4


def _hgrn_chunk(q, fpre, v, gate, lb, gna, st):
    c = q.shape[0]
    forget = lb + (1.0 - lb) * jax.nn.sigmoid(fpre)
    k = 1.0 - forget
    logf = jnp.log(forget)
    cum = _cumsum_rows(logf)
    cum_end = jnp.sum(logf, axis=0, keepdims=True)
    o = _dot(q * jnp.exp(cum), st, _NT)
    st_new = st * jnp.exp(cum_end) + _dot(v, k * jnp.exp(cum_end - cum), _TN)
    t = _iota((c, 1), 0)
    s_off = jnp.zeros((c, c), F32)
    for i in range(1, c // _SUB):
        before = t < i * _SUB
        c_i = jnp.sum(jnp.where(before, logf, 0.0), axis=0, keepdims=True)
        in_blk = (t >= i * _SUB) & (t < (i + 1) * _SUB)
        qi = jnp.where(in_blk, q * jnp.exp(jnp.minimum(cum - c_i, 0.0)), 0.0)
        ki = jnp.where(before, k * jnp.exp(jnp.minimum(c_i - cum, 0.0)), 0.0)
        s_off = s_off + _dot(qi, ki, _NT)
    o = o + _dot(s_off, v)
    tmod = t % _SUB
    for r in range(_SUB):
        kr, cr, vr = (k, cum, v) if r == 0 else (_roll_rows(k, r), _roll_rows(cum, r), _roll_rows(v, r))
        w = q * kr * jnp.exp(jnp.where(tmod >= r, cum - cr, _NEG))
        o = o + jnp.sum(w, axis=1, keepdims=True) * vr
    return _head_norm_gate(o, gna, gate), st_new


def _hgrn_fwd(proj, lb, gna):
    S = proj.shape[0]
    n_chunks, heads = S // CHUNK, 8

    def body(q_ref, f_ref, v_ref, g_ref, lb_ref, gna_ref, o_ref, st_out, st):
        n, hb = pl.program_id(0), pl.program_id(1)

        @pl.when(n == 0)
        def _():
            st[hb] = jnp.zeros((_HPS, HEAD, HEAD), F32)

        sls = [slice(hh * HEAD, (hh + 1) * HEAD) for hh in range(_HPS)]
        st0 = [st[hb, hh] for hh in range(_HPS)]
        ins = [(q_ref[:, sl], f_ref[:, sl], v_ref[:, sl], g_ref[:, sl], lb_ref[:, sl]) for sl in sls]
        gna = gna_ref[...]
        res = [_hgrn_chunk(*ins[hh], gna, st0[hh]) for hh in range(_HPS)]
        for hh, sl in enumerate(sls):
            st_out[hh] = st0[hh]
            o_ref[:, sl] = res[hh][0].astype(o_ref.dtype)
            st[hb, hh] = res[hh][1]

    wide = _HPS * HEAD
    sec = lambda s: pl.BlockSpec((CHUNK, wide), lambda n, h, s=s: (n, (8 // _HPS) * s + h))
    return pl.pallas_call(
        body, name="hgrn_fwd", grid=(n_chunks, heads // _HPS),
        in_specs=[sec(0), sec(1), sec(2), sec(3),
                  pl.BlockSpec((1, wide), lambda n, h: (0, h)), pl.BlockSpec((1, HEAD), lambda n, h: (0, 0))],
        out_specs=[pl.BlockSpec((CHUNK, wide), lambda n, h: (n, h)),
                   pl.BlockSpec((None, _HPS, HEAD, HEAD), lambda n, h: (n, h, 0, 0))],
        out_shape=[jax.ShapeDtypeStruct((S, 1024), BF16),
                   jax.ShapeDtypeStruct((n_chunks, heads, HEAD, HEAD), F32)],
        scratch_shapes=[pltpu.VMEM((heads // _HPS, _HPS, HEAD, HEAD), F32)],
        compiler_params=_cparams(("arbitrary", "arbitrary")),
    )(proj, proj, proj, proj, lb, gna)


def _hgrn_bwd(proj, lb, gna, states, d_o):
    S = proj.shape[0]
    n_chunks, heads = S // CHUNK, 8

    def body(q_ref, f_ref, v_ref, g_ref, lb_ref, gna_ref, st_ref, do_ref,
             dq_ref, df_ref, dv_ref, dg_ref, dlb_ref, dgna_ref, dst):
        n, hb = pl.program_id(0), pl.program_id(1)

        @pl.when(n == 0)
        def _():
            dst[hb] = jnp.zeros((_HPS, HEAD, HEAD), F32)
            dlb_ref[hb] = jnp.zeros((_HPS, 1, HEAD), F32)

        @pl.when((n == 0) & (hb == 0))
        def _():
            dgna_ref[...] = jnp.zeros_like(dgna_ref)

        sls = [slice(hh * HEAD, (hh + 1) * HEAD) for hh in range(_HPS)]
        gna = gna_ref[...]
        ins = [(q_ref[:, sl], f_ref[:, sl], v_ref[:, sl], g_ref[:, sl], lb_ref[:, sl], gna, st_ref[hh])
               for hh, sl in enumerate(sls)]
        cts = [(do_ref[:, sl], dst[hb, hh]) for hh, sl in enumerate(sls)]
        grads = [jax.vjp(_hgrn_chunk, *ins[hh])[1](cts[hh]) for hh in range(_HPS)]
        dgna_sum = jnp.zeros((1, HEAD), F32)
        for hh, sl in enumerate(sls):
            dq, df, dv, dg, dlb, dgna, dst0 = grads[hh]
            dq_ref[:, sl] = dq.astype(dq_ref.dtype)
            df_ref[:, sl] = df.astype(df_ref.dtype)
            dv_ref[:, sl] = dv.astype(dv_ref.dtype)
            dg_ref[:, sl] = dg.astype(dg_ref.dtype)
            dlb_ref[hb, hh] += dlb
            dgna_sum = dgna_sum + dgna
            dst[hb, hh] = dst0
        dgna_ref[...] += dgna_sum

    rev = lambda n: n_chunks - 1 - n
    wide = _HPS * HEAD
    sec = lambda s: pl.BlockSpec((CHUNK, wide), lambda n, h, s=s: (rev(n), (8 // _HPS) * s + h))
    out = pl.BlockSpec((CHUNK, wide), lambda n, h: (rev(n), h))
    return pl.pallas_call(
        body, name="hgrn_bwd", grid=(n_chunks, heads // _HPS),
        in_specs=[sec(0), sec(1), sec(2), sec(3),
                  pl.BlockSpec((1, wide), lambda n, h: (0, h)), pl.BlockSpec((1, HEAD), lambda n, h: (0, 0)),
                  pl.BlockSpec((None, _HPS, HEAD, HEAD), lambda n, h: (rev(n), h, 0, 0)),
                  pl.BlockSpec((CHUNK, wide), lambda n, h: (rev(n), h))],
        out_specs=[out, out, out, out,
                   pl.BlockSpec((heads // _HPS, _HPS, 1, HEAD), lambda n, h: (0, 0, 0, 0)),
                   pl.BlockSpec((1, HEAD), lambda n, h: (0, 0))],
        out_shape=[jax.ShapeDtypeStruct((S, 1024), BF16)] * 4
        + [jax.ShapeDtypeStruct((heads // _HPS, _HPS, 1, HEAD), F32), jax.ShapeDtypeStruct((1, HEAD), F32)],
        scratch_shapes=[pltpu.VMEM((heads // _HPS, _HPS, HEAD, HEAD), F32)],
        compiler_params=_cparams(("arbitrary", "arbitrary")),
    )(proj, proj, proj, proj, lb, gna, states, d_o)


def _l2n(t):
    return t * lax.rsqrt(jnp.sum(t * t, axis=-1, keepdims=True) + NORM_EPS)


def _unit_lower_inverse(low):
    c = low.shape[-1]
    inv = (_iota((c, c), 0) == _iota((c, c), 1)).astype(F32) - low
    p = low
    span = 2
    while span < c:
        p = _dot3(p, p)
        inv = inv + _dot3(inv, p)
        span *= 2
    return inv


def _gdn_chunk(qc, kc, v, z, tail, h0, arow, dtrow, gnb, st):
    nb, c = qc.shape[0], qc.shape[1]
    head = h0 + _iota((nb, c, HEAD), 0)
    lane = _iota((nb, c, HEAD), 2)
    la_all = arow * jax.nn.softplus(tail + dtrow)
    la = jnp.sum(jnp.where(lane == head, la_all[None], 0.0), axis=2, keepdims=True)
    beta = jnp.sum(jnp.where(lane == head + 8, jax.nn.sigmoid(tail)[None], 0.0), axis=2, keepdims=True)
    la_b = jnp.broadcast_to(la, (nb, c, HEAD))
    cum = _cumsum_rows(la_b)
    cmat = _cumsum_rows(jnp.broadcast_to(la, (nb, c, c)))
    cum_end = jnp.sum(la_b, axis=1, keepdims=True)
    decay = jnp.exp(jnp.where(_tril(c), cmat - jnp.swapaxes(cmat, 1, 2), _NEG))
    q = _l2n(qc) * (HEAD ** -0.5)
    k = _l2n(kc)
    k_beta = k * beta
    low = jnp.where(_tril(c, strict=True), _dot(k_beta, k, _NT) * decay, 0.0)
    inv = _unit_lower_inverse(low)
    u, w = _dot(inv, v * beta), _dot(inv, k_beta * jnp.exp(cum))
    intra = _dot(q, k, _NT) * decay
    v_new = u - _dot(w, st, _NT)
    o = _dot(q * jnp.exp(cum), st, _NT) + _dot(intra, v_new)
    st_new = st * jnp.exp(cum_end) + _dot(v_new, k * jnp.exp(cum_end - cum), _TN)
    return _head_norm_gate(o, gnb, z), st_new


def _gdn_specs(n_of):
    wide = _HPS * HEAD
    qkv = lambda s: pl.BlockSpec((CHUNK, wide), lambda n, h, s=s: (n_of(n), (8 // _HPS) * s + h))
    return [qkv(0), qkv(1), qkv(2),
            pl.BlockSpec((CHUNK, wide), lambda n, h: (n_of(n), 56 // _HPS + h)),
            pl.BlockSpec((CHUNK, HEAD), lambda n, h: (n_of(n), 64)),
            pl.BlockSpec((1, HEAD), lambda n, h: (0, 0)), pl.BlockSpec((1, HEAD), lambda n, h: (0, 0)),
            pl.BlockSpec((1, HEAD), lambda n, h: (0, 0))]


def _gdn_fwd(act, proj, arow, dtrow, gnb):
    S = act.shape[0]
    n_chunks, heads = S // CHUNK, 8

    def body(q_ref, k_ref, v_ref, z_ref, t_ref, a_ref, dt_ref, gnb_ref, o_ref, st_out, st):
        n, hb = pl.program_id(0), pl.program_id(1)

        @pl.when(n == 0)
        def _():
            st[hb] = jnp.zeros((_HPS, HEAD, HEAD), F32)

        sls = [slice(hh * HEAD, (hh + 1) * HEAD) for hh in range(_HPS)]
        heads_of = lambda ref: jnp.stack([ref[:, sl] for sl in sls])
        st0 = st[hb]
        o, st1 = _gdn_chunk(heads_of(q_ref), heads_of(k_ref), heads_of(v_ref), heads_of(z_ref), t_ref[...], hb * _HPS,
                            a_ref[...], dt_ref[...], gnb_ref[...], st0)
        st_out[...] = st0
        st[hb] = st1
        for hh, sl in enumerate(sls):
            o_ref[:, sl] = o[hh].astype(o_ref.dtype)

    return pl.pallas_call(
        body, name="gdn_fwd", grid=(n_chunks, heads // _HPS),
        in_specs=_gdn_specs(lambda n: n),
        out_specs=[pl.BlockSpec((CHUNK, _HPS * HEAD), lambda n, h: (n, h)),
                   pl.BlockSpec((None, _HPS, HEAD, HEAD), lambda n, h: (n, h, 0, 0))],
        out_shape=[jax.ShapeDtypeStruct((S, 1024), BF16),
                   jax.ShapeDtypeStruct((n_chunks, heads, HEAD, HEAD), F32)],
        scratch_shapes=[pltpu.VMEM((heads // _HPS, _HPS, HEAD, HEAD), F32)],
        compiler_params=_cparams(("arbitrary", "arbitrary")),
    )(act, act, act, proj, proj, arow, dtrow, gnb)


def _gdn_bwd(act, proj, arow, dtrow, gnb, states, d_o):
    S = act.shape[0]
    n_chunks, heads = S // CHUNK, 8

    def body(q_ref, k_ref, v_ref, z_ref, t_ref, a_ref, dt_ref, gnb_ref, st_ref, do_ref,
             dq_ref, dk_ref, dv_ref, dz_ref, dt_out, da_ref, ddt_ref, dgnb_ref, dst, dtail_acc):
        n, hb = pl.program_id(0), pl.program_id(1)
        n_hb = heads // _HPS

        @pl.when(n == 0)
        def _():
            dst[hb] = jnp.zeros((_HPS, HEAD, HEAD), F32)

        @pl.when((n == 0) & (hb == 0))
        def _():
            da_ref[...] = jnp.zeros_like(da_ref)
            ddt_ref[...] = jnp.zeros_like(ddt_ref)
            dgnb_ref[...] = jnp.zeros_like(dgnb_ref)

        sls = [slice(hh * HEAD, (hh + 1) * HEAD) for hh in range(_HPS)]
        heads_of = lambda ref: jnp.stack([ref[:, sl] for sl in sls])
        fn = lambda qc, kc, v, z, tail, arow, dtrow, gnb, st: _gdn_chunk(qc, kc, v, z, tail, hb * _HPS, arow, dtrow,
                                                                        gnb, st)
        _, vjp = jax.vjp(fn, heads_of(q_ref), heads_of(k_ref), heads_of(v_ref), heads_of(z_ref), t_ref[...],
                         a_ref[...], dt_ref[...], gnb_ref[...], st_ref[...])
        dq, dk, dv, dz, dtail_sum, da_sum, ddt_sum, dgnb_sum, dst0 = vjp((heads_of(do_ref), dst[hb]))
        dst[hb] = dst0
        for hh, sl in enumerate(sls):
            dq_ref[:, sl] = dq[hh]
            dk_ref[:, sl] = dk[hh]
            dv_ref[:, sl] = dv[hh]
            dz_ref[:, sl] = dz[hh].astype(dz_ref.dtype)

        @pl.when(hb == 0)
        def _():
            dtail_acc[...] = dtail_sum

        @pl.when(hb > 0)
        def _():
            dtail_acc[...] += dtail_sum

        @pl.when(hb == n_hb - 1)
        def _():
            dt_out[...] = dtail_acc[...].astype(dt_out.dtype)

        da_ref[...] += da_sum
        ddt_ref[...] += ddt_sum
        dgnb_ref[...] += dgnb_sum

    rev = lambda n: n_chunks - 1 - n
    out = pl.BlockSpec((CHUNK, _HPS * HEAD), lambda n, h: (rev(n), h))
    row = pl.BlockSpec((1, HEAD), lambda n, h: (0, 0))
    return pl.pallas_call(
        body, name="gdn_bwd", grid=(n_chunks, heads // _HPS),
        in_specs=_gdn_specs(rev)
        + [pl.BlockSpec((None, _HPS, HEAD, HEAD), lambda n, h: (rev(n), h, 0, 0)),
           pl.BlockSpec((CHUNK, _HPS * HEAD), lambda n, h: (rev(n), 8 // _HPS + h))],
        out_specs=[out, out, out, out, pl.BlockSpec((CHUNK, HEAD), lambda n, h: (rev(n), 0)), row, row, row],
        out_shape=[jax.ShapeDtypeStruct((S, 1024), F32)] * 3
        + [jax.ShapeDtypeStruct((S, 1024), BF16), jax.ShapeDtypeStruct((S, HEAD), BF16)]
        + [jax.ShapeDtypeStruct((1, HEAD), F32)] * 3,
        scratch_shapes=[pltpu.VMEM((heads // _HPS, _HPS, HEAD, HEAD), F32), pltpu.VMEM((CHUNK, HEAD), F32)],
        compiler_params=_cparams(("arbitrary", "arbitrary")),
    )(act, act, act, proj, proj, arow, dtrow, gnb, states, d_o)


def _conv_silu(x, w):
    t = _iota((x.shape[0], 1), 0)
    tap = _iota(w.shape, 0)
    y = jnp.zeros_like(x)
    for r in range(4):
        w_r = jnp.sum(jnp.where(tap == 3 - r, w, 0.0), axis=0, keepdims=True)
        y = y + (x if r == 0 else jnp.where(t >= r, _roll_rows(x, r), 0.0)) * w_r
    return jax.nn.silu(y)


_CONV_COLS = 128


def _conv_fwd(proj, conv_w):
    S = proj.shape[0]
    nb = 3072 // _CONV_COLS
    off = 4096 // _CONV_COLS

    def body(x_ref, w_ref, o_ref):
        o_ref[...] = _conv_silu(x_ref[...], w_ref[...])

    return pl.pallas_call(
        body, name="conv_fwd", grid=(nb,),
        in_specs=[pl.BlockSpec((S, _CONV_COLS), lambda j: (0, off + j)), pl.BlockSpec((4, _CONV_COLS), lambda j: (0, j))],
        out_specs=pl.BlockSpec((S, _CONV_COLS), lambda j: (0, j)),
        out_shape=jax.ShapeDtypeStruct((S, 3072), F32),
        compiler_params=_cparams(("parallel",)),
    )(proj, conv_w)


def _conv_bwd(proj, conv_w, dq, dk, dv):
    S = proj.shape[0]
    nb = 3072 // _CONV_COLS
    off = 4096 // _CONV_COLS
    per = 1024 // _CONV_COLS

    def body(x_ref, w_ref, dq_ref, dk_ref, dv_ref, dx_ref, dw_ref):
        j = pl.program_id(0)
        _, vjp = jax.vjp(_conv_silu, x_ref[...], w_ref[...])
        d = jnp.where(j < per, dq_ref[...], jnp.where(j < 2 * per, dk_ref[...], dv_ref[...]))
        dx, dw = vjp(d)
        dx_ref[...] = dx.astype(dx_ref.dtype)
        dw_ref[...] = dw

    dsp = lambda s: pl.BlockSpec((S, _CONV_COLS), lambda j, s=s: (0, jnp.clip(j - s * per, 0, per - 1)))
    return pl.pallas_call(
        body, name="conv_bwd", grid=(nb,),
        in_specs=[pl.BlockSpec((S, _CONV_COLS), lambda j: (0, off + j)), pl.BlockSpec((4, _CONV_COLS), lambda j: (0, j)),
                  dsp(0), dsp(1), dsp(2)],
        out_specs=[pl.BlockSpec((S, _CONV_COLS), lambda j: (0, j)), pl.BlockSpec((4, _CONV_COLS), lambda j: (0, j))],
        out_shape=[jax.ShapeDtypeStruct((S, 3072), BF16), jax.ShapeDtypeStruct((4, 3072), F32)],
        compiler_params=_cparams(("parallel",)),
    )(proj, conv_w, dq, dk, dv)


_S5_T = 512
_S5_L = 512
_S5_NB = 16


def _s5_tile(xr, xi, pw_r, pw_i, cr, ci, reverse):
    t8 = _iota((8, 1), 0)
    for sh in (1, 2, 4):
        row = (8 - sh) if reverse else (sh - 1)
        ar, ai = pw_r[row:row + 1, :], pw_i[row:row + 1, :]
        if reverse:
            keep, amt = t8 < 8 - sh, 8 - sh
        else:
            keep, amt = t8 >= sh, sh
        sr = jnp.where(keep, pltpu.roll(xr, amt, 0), 0.0)
        si = jnp.where(keep, pltpu.roll(xi, amt, 0), 0.0)
        xr, xi = xr + ar * sr - ai * si, xi + ar * si + ai * sr
    xr, xi = xr + pw_r * cr - pw_i * ci, xi + pw_r * ci + pw_i * cr
    return xr, xi


def _s5_fwd(u, bre, bim, cre, cim, pw_r, pw_i, dskip):
    S = u.shape[0]
    T = min(_S5_T, S)
    nt = S // T

    def body(u_ref, bre_ref, bim_ref, cre_ref, cim_ref, pr_ref, pi_ref, d_ref, y_ref, xr_ref, xi_ref,
             bu_r, bu_i, car_r, car_i):
        @pl.when(pl.program_id(1) == 0)
        def _():
            car_r[...] = jnp.zeros_like(car_r)
            car_i[...] = jnp.zeros_like(car_i)

        uu = u_ref[...]
        bu_r[...] = _dot(uu, bre_ref[...])
        bu_i[...] = _dot(uu, bim_ref[...])
        pw_r, pw_i = pr_ref[...], pi_ref[...]

        def tile(i, carry):
            r0 = pl.multiple_of(i * 8, 8)
            xr, xi = _s5_tile(bu_r[pl.ds(r0, 8), :], bu_i[pl.ds(r0, 8), :], pw_r, pw_i, carry[0], carry[1], False)
            xr_ref[pl.ds(r0, 8), :] = xr
            xi_ref[pl.ds(r0, 8), :] = xi
            return xr[7:8, :], xi[7:8, :]

        cr, ci = lax.fori_loop(0, T // 8, tile, (car_r[...], car_i[...]))
        car_r[...] = cr
        car_i[...] = ci
        y_ref[...] = _dot(xr_ref[...], cre_ref[...]) - _dot(xi_ref[...], cim_ref[...]) + d_ref[...] * uu

    blk3 = lambda a, b: pl.BlockSpec((None, a, b), lambda j, t: (j, 0, 0))
    return pl.pallas_call(
        body, name="s5_fwd", grid=(_S5_NB, nt),
        in_specs=[pl.BlockSpec((T, HEAD), lambda j, t: (t, j)),
                  blk3(HEAD, _S5_L), blk3(HEAD, _S5_L), blk3(_S5_L, HEAD), blk3(_S5_L, HEAD),
                  blk3(8, _S5_L), blk3(8, _S5_L), pl.BlockSpec((1, HEAD), lambda j, t: (0, j))],
        out_specs=[pl.BlockSpec((T, HEAD), lambda j, t: (t, j)),
                   pl.BlockSpec((T, _S5_L), lambda j, t: (t, j)), pl.BlockSpec((T, _S5_L), lambda j, t: (t, j))],
        out_shape=[jax.ShapeDtypeStruct((S, D_MODEL), F32),
                   jax.ShapeDtypeStruct((S, _S5_NB * _S5_L), F32), jax.ShapeDtypeStruct((S, _S5_NB * _S5_L), F32)],
        scratch_shapes=[pltpu.VMEM((T, _S5_L), F32), pltpu.VMEM((T, _S5_L), F32),
                        pltpu.VMEM((1, _S5_L), F32), pltpu.VMEM((1, _S5_L), F32)],
        compiler_params=_cparams(("parallel", "arbitrary")),
    )(u, bre, bim, cre, cim, pw_r, pw_i, dskip)


def _s5_bwd(dy, u, xre, xim, bre, bim, cre, cim, qw_r, qw_i, dskip):
    S = u.shape[0]
    T = min(_S5_T, S)
    nt = S // T
    nt8 = T // 8

    def body(dy_ref, u_ref, xr_ref, xi_ref, xpr_ref, xpi_ref, bre_ref, bim_ref, cre_ref, cim_ref, qr_ref, qi_ref,
             d_ref, du_ref, dbr_ref, dbi_ref, dcr_ref, dci_ref, dlr_ref, dli_ref, dd_ref,
             g_r, g_i, car_r, car_i):
        t = pl.program_id(1)

        @pl.when(t == 0)
        def _():
            car_r[...] = jnp.zeros_like(car_r)
            car_i[...] = jnp.zeros_like(car_i)
            for r in (dbr_ref, dbi_ref, dcr_ref, dci_ref, dlr_ref, dli_ref, dd_ref):
                r[...] = jnp.zeros_like(r)

        dyy, uu = dy_ref[...], u_ref[...]
        g_r[...] = _dot(dyy, cre_ref[...], _NT)
        g_i[...] = -_dot(dyy, cim_ref[...], _NT)
        qw_r, qw_i = qr_ref[...], qi_ref[...]
        t8 = _iota((8, 1), 0)
        first = t == nt - 1

        def tile(r0, prev_r, prev_i, carry, acc):
            gr, gi = _s5_tile(g_r[pl.ds(r0, 8), :], g_i[pl.ds(r0, 8), :], qw_r, qw_i, carry[0], carry[1], True)
            g_r[pl.ds(r0, 8), :] = gr
            g_i[pl.ds(r0, 8), :] = gi
            xpr = jnp.where(t8 >= 1, pltpu.roll(xr_ref[pl.ds(r0, 8), :], 1, 0), prev_r)
            xpi = jnp.where(t8 >= 1, pltpu.roll(xi_ref[pl.ds(r0, 8), :], 1, 0), prev_i)
            acc = (acc[0] + gr * xpr + gi * xpi, acc[1] + gi * xpr - gr * xpi)
            return (gr[0:1, :], gi[0:1, :]), acc

        def step(ii, state):
            carry, acc = state
            r0 = pl.multiple_of((nt8 - 1 - ii) * 8, 8)
            p0 = pl.multiple_of((nt8 - 2 - ii) * 8, 8)
            prev_r = xr_ref[pl.ds(p0, 8), :][7:8, :]
            prev_i = xi_ref[pl.ds(p0, 8), :][7:8, :]
            return tile(r0, prev_r, prev_i, carry, acc)

        zero = jnp.zeros((8, _S5_L), F32)
        state = lax.fori_loop(0, nt8 - 1, step, ((car_r[...], car_i[...]), (zero, zero)))
        prev_r = jnp.where(first, 0.0, xpr_ref[...][7:8, :])
        prev_i = jnp.where(first, 0.0, xpi_ref[...][7:8, :])
        (cr, ci), (ar, ai) = tile(0, prev_r, prev_i, *state)
        car_r[...] = cr
        car_i[...] = ci
        dlr_ref[...] += ar
        dli_ref[...] += ai
        gr, gi = g_r[...], g_i[...]
        du_ref[...] = _dot(gr, bre_ref[...], _NT) + _dot(gi, bim_ref[...], _NT) + d_ref[...] * dyy
        dbr_ref[...] += _dot(uu, gr, _TN)
        dbi_ref[...] += _dot(uu, gi, _TN)
        dcr_ref[...] += _dot(xr_ref[...], dyy, _TN)
        dci_ref[...] -= _dot(xi_ref[...], dyy, _TN)
        dd_ref[...] += jnp.sum(dyy * uu, axis=0, keepdims=True)

    rev = lambda t: nt - 1 - t
    blk3 = lambda a, b: pl.BlockSpec((None, a, b), lambda j, t: (j, 0, 0))
    tl = pl.BlockSpec((T, HEAD), lambda j, t: (rev(t), j))
    xs = pl.BlockSpec((T, _S5_L), lambda j, t: (rev(t), j))
    xp = pl.BlockSpec((8, _S5_L), lambda j, t: (jnp.maximum(rev(t) * nt8 - 1, 0), j))
    return pl.pallas_call(
        body, name="s5_bwd", grid=(_S5_NB, nt),
        in_specs=[tl, tl, xs, xs, xp, xp, blk3(HEAD, _S5_L), blk3(HEAD, _S5_L), blk3(_S5_L, HEAD), blk3(_S5_L, HEAD),
                  blk3(8, _S5_L), blk3(8, _S5_L), pl.BlockSpec((1, HEAD), lambda j, t: (0, j))],
        out_specs=[tl, blk3(HEAD, _S5_L), blk3(HEAD, _S5_L), blk3(_S5_L, HEAD), blk3(_S5_L, HEAD),
                   blk3(8, _S5_L), blk3(8, _S5_L), pl.BlockSpec((1, HEAD), lambda j, t: (0, j))],
        out_shape=[jax.ShapeDtypeStruct((S, D_MODEL), F32),
                   jax.ShapeDtypeStruct((_S5_NB, HEAD, _S5_L), F32), jax.ShapeDtypeStruct((_S5_NB, HEAD, _S5_L), F32),
                   jax.ShapeDtypeStruct((_S5_NB, _S5_L, HEAD), F32), jax.ShapeDtypeStruct((_S5_NB, _S5_L, HEAD), F32),
                   jax.ShapeDtypeStruct((_S5_NB, 8, _S5_L), F32), jax.ShapeDtypeStruct((_S5_NB, 8, _S5_L), F32),
                   jax.ShapeDtypeStruct((1, D_MODEL), F32)],
        scratch_shapes=[pltpu.VMEM((T, _S5_L), F32), pltpu.VMEM((T, _S5_L), F32),
                        pltpu.VMEM((1, _S5_L), F32), pltpu.VMEM((1, _S5_L), F32)],
        compiler_params=_cparams(("parallel", "arbitrary")),
    )(dy, u, xre, xim, xre, xim, bre, bim, cre, cim, qw_r, qw_i, dskip)


def _s5_params(a_re, a_im, log_dt, b_re, b_im, c_re, c_im):
    step = jnp.exp(log_dt)[:, None]
    mag = jnp.exp(a_re * step)
    lr, li = mag * jnp.cos(a_im * step), mag * jnp.sin(a_im * step)
    den = a_re * a_re + a_im * a_im
    nr, ni = lr - 1.0, li
    kr, ki = (nr * a_re + ni * a_im) / den, (ni * a_re - nr * a_im) / den
    bbr = kr[..., None] * b_re - ki[..., None] * b_im
    bbi = kr[..., None] * b_im + ki[..., None] * b_re
    eye = jnp.eye(8, dtype=F32)

    def blk_b(bb):
        t = bb.reshape(_S5_NB, 8, 64, 16).transpose(0, 1, 3, 2)
        return (t[:, :, :, None, :] * eye[None, :, None, :, None]).reshape(_S5_NB, HEAD, _S5_L)

    def blk_c(cc):
        t = cc.reshape(_S5_NB, 8, 16, 64).transpose(0, 1, 3, 2)
        return (t[:, :, :, None, :] * eye[None, :, None, :, None]).reshape(_S5_NB, _S5_L, HEAD)

    return (blk_b(bbr), blk_b(bbi), blk_c(c_re), blk_c(c_im),
            lr.reshape(_S5_NB, _S5_L), li.reshape(_S5_NB, _S5_L))


def _s5_powers(lr, li):
    pr, pi = [lr], [li]
    for _ in range(7):
        pr, pi = pr + [pr[-1] * lr - pi[-1] * li], pi + [pr[-1] * li + pi[-1] * lr]
    return jnp.stack(pr, axis=1), jnp.stack(pi, axis=1)


_MESH = pl.DeviceIdType.MESH
_ANY = pl.BlockSpec(memory_space=pl.ANY)


def _place():
    x, y, c = lax.axis_index("x"), lax.axis_index("y"), lax.axis_index("c")
    return x, y, c, [(1 - x, y), (x, 1 - y), (1 - x, 1 - y)]


def _comm_call(body, arrs, out_shapes, n_remote, name):
    n = len(arrs)
    return pl.pallas_call(
        body, name=name,
        in_specs=[_ANY] * n, out_specs=[_ANY] * n, out_shape=out_shapes,
        scratch_shapes=[pltpu.SemaphoreType.DMA((n * n_remote,)), pltpu.SemaphoreType.DMA((n * n_remote,)),
                        pltpu.SemaphoreType.DMA((n,))],
    )(*arrs)


def _ag_xy(arrs, name):
    n = len(arrs)

    def body(*refs):
        ins, outs, (ssem, rsem, lsem) = refs[:n], refs[n:2 * n], refs[2 * n:]
        x, y, c, chips = _place()
        me, sib = 2 * x + y, (x, y, 1 - c)
        halves = [a.shape[0] // 2 for a in arrs]
        mine = [pl.ds(pl.multiple_of(c * h, 8), h) for h in halves]
        theirs = [pl.ds(pl.multiple_of((1 - c) * h, 8), h) for h in halves]

        def copy(a, k, src, dst, to):
            return pltpu.make_async_remote_copy(src_ref=src, dst_ref=dst, send_sem=ssem.at[6 * a + k],
                                                recv_sem=rsem.at[6 * a + k], device_id=to, device_id_type=_MESH)

        sends = []
        for a in range(n):
            loc = pltpu.make_async_copy(ins[a], outs[a].at[me], lsem.at[a])
            loc.start()
            sends.append(loc)
        for j, (px, py) in enumerate(chips):
            for a in range(n):
                cp = copy(a, j, ins[a].at[mine[a]], outs[a].at[me, mine[a]], (px, py, c))
                cp.start()
                sends.append(cp)
        for j, (px, py) in enumerate(chips):
            for a in range(n):
                landed = outs[a].at[2 * px + py, mine[a]]
                copy(a, j, landed, landed, (px, py, c)).wait_recv()
                fwd = copy(a, 3 + j, landed, landed, sib)
                fwd.start()
                sends.append(fwd)
        for j, (px, py) in enumerate(chips):
            for a in range(n):
                passed = outs[a].at[2 * px + py, theirs[a]]
                copy(a, 3 + j, passed, passed, sib).wait_recv()
        for cp in sends[:n]:
            cp.wait()
        for cp in sends[n:]:
            cp.wait_send()

    assert all(a.shape[0] % 16 == 0 for a in arrs)
    return pl.pallas_call(
        body, name=name,
        in_specs=[_ANY] * n, out_specs=[_ANY] * n,
        out_shape=[jax.ShapeDtypeStruct((4,) + a.shape, a.dtype) for a in arrs],
        scratch_shapes=[pltpu.SemaphoreType.DMA((6 * n,)), pltpu.SemaphoreType.DMA((6 * n,)),
                        pltpu.SemaphoreType.DMA((n,))],
    )(*arrs)


def _pair_split(arrs, name):
    n = len(arrs)

    def body(*refs):
        ins, outs, (ssem, rsem, _) = refs[:n], refs[n:2 * n], refs[2 * n:]
        x, y, c, _ = _place()
        copies = []
        for a in range(n):
            h = arrs[a].shape[1] // 2
            give = ins[a].at[:, pl.ds(pl.multiple_of((1 - c) * h, 8), h), :]
            cp = pltpu.make_async_remote_copy(src_ref=give, dst_ref=outs[a], send_sem=ssem.at[a], recv_sem=rsem.at[a],
                                              device_id=(x, y, 1 - c), device_id_type=_MESH)
            cp.start()
            copies.append(cp)
        for cp in copies:
            cp.wait()

    shapes = [jax.ShapeDtypeStruct((4, a.shape[1] // 2, a.shape[2]), a.dtype) for a in arrs]
    return _comm_call(body, arrs, shapes, 1, name)


def _pair_join(arrs, name):
    n = len(arrs)

    def body(*refs):
        ins, outs, (ssem, rsem, lsem) = refs[:n], refs[n:2 * n], refs[2 * n:]
        x, y, c, _ = _place()
        copies = []
        for a in range(n):
            h = arrs[a].shape[0]
            mine = pl.ds(pl.multiple_of(c * h, 8), h)
            loc = pltpu.make_async_copy(ins[a], outs[a].at[mine], lsem.at[a])
            cp = pltpu.make_async_remote_copy(src_ref=ins[a], dst_ref=outs[a].at[mine], send_sem=ssem.at[a],
                                              recv_sem=rsem.at[a], device_id=(x, y, 1 - c), device_id_type=_MESH)
            loc.start()
            cp.start()
            copies += [loc, cp]
        for cp in copies:
            cp.wait()

    shapes = [jax.ShapeDtypeStruct((2 * a.shape[0], a.shape[1]), a.dtype) for a in arrs]
    return _comm_call(body, arrs, shapes, 1, name)


def _pair_sum(full, recv, c_idx, name, rows=128):
    _, R, C = full.shape
    half = R // 2
    rows = min(rows, half)
    nb = half // rows
    assert half % rows == 0

    def body(c_ref, a_ref, b_ref, o_ref):
        o_ref[...] = (a_ref[...].astype(F32) + b_ref[...].astype(F32)).astype(o_ref.dtype)

    return pl.pallas_call(
        body, name=name,
        grid_spec=pltpu.PrefetchScalarGridSpec(
            num_scalar_prefetch=1, grid=(4, nb),
            in_specs=[pl.BlockSpec((None, rows, C), lambda s, i, c: (s, c[0] * nb + i, 0)),
                      pl.BlockSpec((None, rows, C), lambda s, i, c: (s, i, 0))],
            out_specs=pl.BlockSpec((None, rows, C), lambda s, i, c: (s, i, 0))),
        out_shape=jax.ShapeDtypeStruct(recv.shape, recv.dtype),
        compiler_params=_cparams(("parallel", "parallel")),
    )(c_idx, full, recv)


def _rs_xy(arrs, name):
    n = len(arrs)

    def body(*refs):
        ins, outs, (ssem, rsem, lsem) = refs[:n], refs[n:2 * n], refs[2 * n:]
        x, y, c, chips = _place()
        me = 2 * x + y
        copies = []
        for a in range(n):
            loc = pltpu.make_async_copy(ins[a].at[me], outs[a].at[3], lsem.at[a])
            loc.start()
            copies.append(loc)
            for j, (px, py) in enumerate(chips):
                cp = pltpu.make_async_remote_copy(src_ref=ins[a].at[2 * px + py], dst_ref=outs[a].at[j],
                                                  send_sem=ssem.at[3 * a + j], recv_sem=rsem.at[3 * a + j],
                                                  device_id=(px, py, c), device_id_type=_MESH)
                cp.start()
                copies.append(cp)
        for cp in copies:
            cp.wait()

    return _comm_call(body, arrs, [jax.ShapeDtypeStruct(a.shape, a.dtype) for a in arrs], 3, name)


def _gather8(buf, name):
    def body(in_ref, out_ref, ssem, rsem, lsem):
        x, y, c, _ = _place()
        me = 4 * x + 2 * y + c
        loc = pltpu.make_async_copy(in_ref, out_ref.at[me], lsem.at[0])
        loc.start()
        copies = [loc]
        for d in range(1, 8):
            dx, dy, dc = (d >> 2) & 1, (d >> 1) & 1, d & 1
            peer = (1 - x if dx else x, 1 - y if dy else y, 1 - c if dc else c)
            cp = pltpu.make_async_remote_copy(src_ref=in_ref, dst_ref=out_ref.at[me], send_sem=ssem.at[d - 1],
                                              recv_sem=rsem.at[d - 1], device_id=peer, device_id_type=_MESH)
            cp.start()
            copies.append(cp)
        for cp in copies:
            cp.wait()

    return _comm_call(body, [buf], [jax.ShapeDtypeStruct((8,) + buf.shape, buf.dtype)], 7, name)[0]


def _sum_slots(arr, name, rows=128):
    k, R, C = arr.shape
    rows = min(rows, R)
    assert R % rows == 0

    def body(in_ref, o_ref):
        acc = in_ref[0].astype(F32)
        for s in range(1, k):
            acc = acc + in_ref[s].astype(F32)
        o_ref[...] = acc

    return pl.pallas_call(
        body, name=name, grid=(R // rows,),
        in_specs=[pl.BlockSpec((k, rows, C), lambda i: (0, i, 0))],
        out_specs=pl.BlockSpec((rows, C), lambda i: (i, 0)),
        out_shape=jax.ShapeDtypeStruct((R, C), F32),
        compiler_params=_cparams(("parallel",)),
    )(arr)


ADAM_LR, ADAM_B1, ADAM_B2, ADAM_EPS, ADAM_WD, ADAM_STEP = 0.001, 0.9, 0.999, 1e-08, 0.01, 10


def _adam_math(w, g, m, v):
    m = ADAM_B1 * m + (1.0 - ADAM_B1) * g
    v = ADAM_B2 * v + (1.0 - ADAM_B2) * jnp.square(g)
    m_hat = m / (1.0 - ADAM_B1 ** ADAM_STEP)
    v_hat = v / (1.0 - ADAM_B2 ** ADAM_STEP)
    delta = -ADAM_LR * (m_hat / (jnp.sqrt(v_hat) + ADAM_EPS) + ADAM_WD * w)
    return delta, m, v


def _adamw(w, m, v, gparts, name, g_off=0, rows=128):
    R, C = w.shape
    rows = min(rows, R)
    assert R % rows == 0 and g_off % rows == 0
    ng = len(gparts)

    def body(w_ref, m_ref, v_ref, *rest):
        g_refs, (go, do, mo, vo) = rest[:ng], rest[ng:]
        g = g_refs[0][...]
        for r in g_refs[1:]:
            g = g + r[...]
        delta, m1, v1 = _adam_math(w_ref[...], g, m_ref[...], v_ref[...])
        go[...] = g
        do[...] = delta
        mo[...] = m1
        vo[...] = v1

    blk = pl.BlockSpec((rows, C), lambda i: (i, 0))
    return pl.pallas_call(
        body, name=name, grid=(R // rows,),
        in_specs=[blk] * 3 + [pl.BlockSpec((rows, C), lambda i: (g_off // rows + i, 0))] * ng,
        out_specs=[blk] * 4, out_shape=[jax.ShapeDtypeStruct((R, C), F32)] * 4,
        compiler_params=_cparams(("parallel",)),
    )(w, m, v, *gparts)


_WEIGHTS = ['norm_mix', 'norm_mlp', 'norm_ple', 'w_in_e', 'w_out_e', 'hgrn_lb', 'g_norm_a', 'conv_w', 'a_log',
            'dt_bias', 'g_norm_b', 's5_a_re', 's5_a_im', 's5_b_re', 's5_b_im', 's5_c_re', 's5_c_im', 's5_d',
            's5_log_dt', 'w_glu', 'b_glu', 'w_out_o', 'w_up', 'w_down', 'w_ple_gate', 'w_ple_proj', 'final_norm']
_INPUTS = ['x', 'p'] + _WEIGHTS + ['loss_target'] + ['m_' + n for n in _WEIGHTS] + ['v_' + n for n in _WEIGHTS]
_FAMILY = [('w_up', 'col', 2048), ('w_down', 'row', 2048), ('w_ple_gate', 'row', 512),
           ('w_out_e', 'row', 512), ('w_glu', 'row', 512), ('w_out_o', 'row', 512)]
_IN_PAD = 8320
_IN_BLK = 640


def _rms_bwd_fn(x, d_hn, d_res, g):
    _, vjp = jax.vjp(_rms, x, g)
    dx, dg = vjp(d_hn.astype(F32))
    return dx + d_res, dg


def _add_res(acc, h):
    return (acc + h,)


def _pack_rows(parts, lanes=128, mult=256):
    flat = jnp.concatenate([q.reshape(-1).astype(F32) for q in parts])
    n = flat.shape[0]
    rows = -(-n // (lanes * mult)) * mult
    return jnp.pad(flat, (0, rows * lanes - n)).reshape(rows, lanes)


def _unpack_rows(buf, shapes):
    flat, out, off = buf.reshape(-1), [], 0
    for s in shapes:
        n = math.prod(s)
        out.append(flat[off:off + n].reshape(s))
        off += n
    return out


def _step(a):
    S = a['x'].shape[1]
    x, tgt = a['x'][0], a['loss_target'][0]
    xi, yi = lax.axis_index("x"), lax.axis_index("y")
    me = 2 * xi + yi
    row = lambda t: t.reshape(1, -1)

    views, packs, off = {}, [], 0
    for nm, kind, r in _FAMILY:
        for l in range(a[nm].shape[0]):
            views[nm, l] = (kind, off, r)
            packs.append(a[nm][l].astype(BF16))
            off += r
    rt = off
    small_sh = jnp.concatenate([a['conv_w'][0].reshape(-1), a['s5_d'][0], a['b_glu'][0]]).reshape(16, 256)
    gp, g_in, g_pp, g_small = _ag_xy(
        [jnp.concatenate(packs, axis=0), a['w_in_e'][0].astype(BF16), a['w_ple_proj'].reshape(512, 512).astype(BF16),
         small_sh], name="gather_weights")
    win = jnp.pad(g_in.transpose(1, 0, 2).reshape(D_MODEL, 8208), ((0, 0), (0, _IN_PAD - 8208)))
    w_pp = g_pp.reshape(4, 2, 256, 512).transpose(1, 2, 0, 3).reshape(2, 256, D_MODEL)
    g_small = g_small.reshape(4, 4096)
    conv_full = g_small[:, :3072].reshape(4, 4, 768).transpose(1, 0, 2).reshape(4, 3072)
    s5_d_full = g_small[:, 3072:3584].reshape(1, D_MODEL)
    b_glu_full = g_small[:, 3584:].reshape(1, D_MODEL)

    lb, lb_vjp = jax.vjp(lambda hl: jax.nn.softmax(hl, axis=0)[0:1], a['hgrn_lb'])
    pad_h = lambda t: jnp.pad(t, ((0, 0), (0, HEAD - t.shape[1])))
    (arow, dtrow), gdn_vjp = jax.vjp(lambda al, dt: (pad_h(-jnp.exp(al)), pad_h(dt)), a['a_log'], a['dt_bias'])
    s5p, s5_vjp = jax.vjp(_s5_params, a['s5_a_re'][0], a['s5_a_im'][0], a['s5_log_dt'][0], a['s5_b_re'][0],
                          a['s5_b_im'][0], a['s5_c_re'][0], a['s5_c_im'][0])
    bre, bim, cre, cim = [t.astype(BF16) for t in s5p[:4]]
    pw_r, pw_i = _s5_powers(s5p[4], s5p[5])
    gna, gnb = a['g_norm_a'], a['g_norm_b']

    def norm_cast(h, g, dt, name):
        return _rowwise(_rms, [h], [row(g)], [(D_MODEL, dt)], name=name)

    def mlp_ple_fwd(h, l):
        hn = norm_cast(h, a['norm_mlp'][l], BF16, f"rms_mlp{l}")
        up, act = _mm(hn, gp, b_view=views['w_up', l], out_dtypes=(BF16, BF16), name=f"up{l}",
                      epilogue=lambda acc: (acc, jnp.square(jnp.maximum(acc, 0.0))))
        h2 = _mm(act, gp, b_view=views['w_down', l], epilogue=_add_res, extras=(h,), name=f"down{l}")
        hnp = norm_cast(h2, a['norm_ple'][l], BF16, f"rms_ple{l}")
        pp = _mm(a['p'][l, 0], w_pp[l], name=f"ple_proj{l}")
        h3, gpre = _mm(hnp, gp, b_view=views['w_ple_gate', l], extras=(h2, pp), out_dtypes=(F32, F32), name=f"ple_gate{l}",
                       bm=512, epilogue=lambda acc, h2, pp: (h2 + jax.nn.sigmoid(acc) * pp, acc))
        return h3, (h, hn, up, act, h2, hnp, pp, gpre)

    hn0 = norm_cast(x, a['norm_mix'][0], BF16, "rms_mix0")
    proj = _mm(hn0, win, bn=_IN_BLK, name="in_proj")
    o_a, st_a = _hgrn_fwd(proj, lb, gna)
    act_b = _conv_fwd(proj, conv_full)
    o_b, st_b = _gdn_fwd(act_b, proj, arow, dtrow, gnb)
    merged = jnp.concatenate([o_a, o_b], axis=1)
    h1 = _mm(merged, gp, b_view=views['w_out_e', 0], epilogue=_add_res, extras=(x,), name="out_e")
    h3, saved0 = mlp_ple_fwd(h1, 0)

    u = norm_cast(h3, a['norm_mix'][1], F32, "rms_mix1")
    y, xre, xim = _s5_fwd(u, bre, bim, cre, cim, pw_r, pw_i, s5_d_full)
    actg = _rowwise(jax.nn.gelu, [y], [], [(D_MODEL, BF16)], name="gelu")
    glu, z = _mm(actg, gp, b_view=views['w_glu', 0], extras=(actg, b_glu_full), out_dtypes=(BF16, F32), name="glu",
                 epilogue=lambda acc, act, b: (act.astype(F32) * jax.nn.sigmoid(acc + b), acc + b))
    h4 = _mm(glu, gp, b_view=views['w_out_o', 0], epilogue=_add_res, extras=(h3,), name="out_o")
    h6, saved1 = mlp_ple_fwd(h4, 1)

    def head(h, t, g):
        def f(h, g):
            e = _rms(h, g) - t
            return 0.5 * jnp.sum(jnp.mean(e * e, axis=-1))
        val, vjp = jax.vjp(f, h, g)
        dh, dg = vjp(jnp.ones((), F32))
        return dh, dg, jnp.full((1, HEAD), val)

    dh, d_final, loss_part = _rowwise(head, [h6, tgt], [row(a['final_norm'])], [(D_MODEL, F32)], sums=(D_MODEL, HEAD),
                                      name="loss_head")
    loss = lax.psum(loss_part[0, 0], ("x", "y", "c"))

    gbuf = jnp.zeros((4, rt, PACK_COLS), BF16)

    def into(gbuf, lhs, rhs, key, name):
        return _mm(lhs, rhs, dims="tn", out_into=gbuf, out_view=views[key], out_dtypes=(BF16,), name=name)

    def mlp_ple_bwd(dh, l, saved, gbuf):
        h, hn, up, act, h2, hnp, pp, gpre = saved

        def gate_bwd(d, gpre, pp):
            s = jax.nn.sigmoid(gpre)
            return d * s, d * pp * s * (1.0 - s)

        d_pp, d_gp = _rowwise(gate_bwd, [dh, gpre, pp], [], [(D_MODEL, BF16), (D_MODEL, BF16)], name=f"ple_bwd{l}")
        d_wpp = _mm(a['p'][l, 0], d_pp, dims="tn", name=f"d_ple_proj{l}")
        gbuf = into(gbuf, hnp, d_gp, ('w_ple_gate', l), f"d_ple_gate{l}")
        d_hnp = _mm(d_gp, gp, dims="nt", b_view=views['w_ple_gate', l], name=f"ple_gate_t{l}")
        dh2, d_nple = _rowwise(_rms_bwd_fn, [h2, d_hnp, dh], [row(a['norm_ple'][l])], [(D_MODEL, F32)], sums=(D_MODEL,),
                               name=f"rms_ple_bwd{l}")
        d_up = _mm(dh2, gp, dims="nt", b_view=views['w_down', l], extras=(up,), out_dtypes=(BF16,), name=f"down_t{l}",
                   epilogue=lambda acc, up: (acc * 2.0 * jnp.maximum(up.astype(F32), 0.0),))
        gbuf = into(gbuf, act, dh2, ('w_down', l), f"d_down{l}")
        gbuf = into(gbuf, hn, d_up, ('w_up', l), f"d_up{l}")
        d_hn = _mm(d_up, gp, dims="nt", b_view=views['w_up', l], name=f"up_t{l}")
        dh1, d_nmlp = _rowwise(_rms_bwd_fn, [h, d_hn, dh2], [row(a['norm_mlp'][l])], [(D_MODEL, F32)], sums=(D_MODEL,),
                               name=f"rms_mlp_bwd{l}")
        return dh1, gbuf, d_wpp, d_nple, d_nmlp

    dh4, gbuf, d_wpp1, d_nple1, d_nmlp1 = mlp_ple_bwd(dh, 1, saved1, gbuf)

    d_glu = _mm(dh4, gp, dims="nt", b_view=views['w_out_o', 0], name="out_o_t")
    gbuf = into(gbuf, glu, dh4, ('w_out_o', 0), "d_out_o")

    def glu_bwd(d, z, act):
        s, act = jax.nn.sigmoid(z), act.astype(F32)
        dz = d * act * s * (1.0 - s)
        return dz, d * s, jnp.sum(dz, axis=0, keepdims=True)

    d_z, d_actp, d_bglu = _rowwise(glu_bwd, [d_glu, z, actg], [], [(D_MODEL, BF16), (D_MODEL, F32)], sums=(D_MODEL,),
                                   name="glu_bwd")
    gbuf = into(gbuf, actg, d_z, ('w_glu', 0), "d_glu")

    def gelu_bwd(acc, dap, y):
        _, vjp = jax.vjp(jax.nn.gelu, y)
        return vjp(acc + dap)

    dy = _mm(d_z, gp, dims="nt", b_view=views['w_glu', 0], extras=(d_actp, y), epilogue=gelu_bwd, name="glu_t")
    du, d_bre, d_bim, d_cre, d_cim, d_lr, d_li, d_s5d = _s5_bwd(dy, u, xre, xim, bre, bim, cre, cim,
                                                                 pw_r[:, ::-1], -pw_i[:, ::-1], s5_d_full)
    dh3, d_nmix1 = _rowwise(_rms_bwd_fn, [h3, du, dh4], [row(a['norm_mix'][1])], [(D_MODEL, F32)], sums=(D_MODEL,),
                            name="rms_mix1_bwd")
    d_are, d_aim, d_logdt, d_sbre, d_sbim, d_scre, d_scim = s5_vjp(
        (d_bre, d_bim, d_cre, d_cim, d_lr.sum(axis=1), d_li.sum(axis=1)))

    dh1, gbuf, d_wpp0, d_nple0, d_nmlp0 = mlp_ple_bwd(dh3, 0, saved0, gbuf)

    d_merged = _mm(dh1, gp, dims="nt", b_view=views['w_out_e', 0], name="out_e_t")
    gbuf = into(gbuf, merged, dh1, ('w_out_e', 0), "d_out_e")
    dq, df, dv, dg, d_lb, d_gna = _hgrn_bwd(proj, lb, gna, st_a, d_merged)
    dqb, dkb, dvb, dzb, d_tail, d_arow, d_dtrow, d_gnb = _gdn_bwd(act_b, proj, arow, dtrow, gnb, st_b, d_merged)
    d_pre, d_conv = _conv_bwd(proj, conv_full, dqb, dkb, dvb)
    d_proj = jnp.concatenate([dq, df, dv, dg, d_pre, dzb, d_tail], axis=1)
    d_win = _mm(hn0, d_proj, dims="tn", bn=_IN_BLK, out_dtypes=(BF16,), name="d_in_proj")
    d_hn0 = _mm(d_proj, win, dims="nt", bk=_IN_BLK, name="in_proj_t")
    grad_x, d_nmix0 = _rowwise(_rms_bwd_fn, [x, d_hn0, dh1], [row(a['norm_mix'][0])], [(D_MODEL, F32)], sums=(D_MODEL,),
                               name="rms_mix0_bwd")
    (d_hlb,) = lb_vjp(d_lb.reshape(1, 1024))
    d_alog, d_dtb = gdn_vjp((d_arow, d_dtrow))

    d_win_sh = d_win[:, :8208].reshape(D_MODEL, 4, 2052).transpose(1, 0, 2)
    d_wpp_sh = jnp.stack([d_wpp0, d_wpp1]).astype(BF16).reshape(2, 256, 4, 512).transpose(2, 0, 1, 3).reshape(4, 512, 512)
    partials = [gbuf, d_win_sh, d_wpp_sh]
    c_idx = lax.axis_index("c").astype(jnp.int32).reshape(1)
    chip_sums = [_pair_sum(f, r, c_idx, name=f"pair_sum{i}")
                 for i, (f, r) in enumerate(zip(partials, _pair_split(partials, name="pair_split")))]
    halves = [_sum_slots(t, name=f"sum_partials{i}") for i, t in enumerate(_rs_xy(chip_sums, name="scatter_grads"))]
    grads = _pair_join(halves, name="pair_join")

    small = {
        'norm_mix': jnp.concatenate([d_nmix0, d_nmix1]), 'norm_mlp': jnp.concatenate([d_nmlp0, d_nmlp1]),
        'norm_ple': jnp.concatenate([d_nple0, d_nple1]), 'hgrn_lb': d_hlb, 'g_norm_a': d_gna, 'conv_w': d_conv,
        'a_log': d_alog, 'dt_bias': d_dtb, 'g_norm_b': d_gnb, 's5_a_re': d_are, 's5_a_im': d_aim, 's5_b_re': d_sbre,
        's5_b_im': d_sbim, 's5_c_re': d_scre, 's5_c_im': d_scim, 's5_d': d_s5d, 's5_log_dt': d_logdt,
        'b_glu': d_bglu, 'final_norm': d_final}
    s_names = list(small)
    s_shapes = [tuple(small[n].shape) for n in s_names]
    reduced = _sum_slots(_gather8(_pack_rows([small[n] for n in s_names]), name="gather_small"), name="sum_small")
    sg = dict(zip(s_names, _unpack_rows(reduced, s_shapes)))
    sg['conv_w'] = lax.dynamic_slice_in_dim(sg['conv_w'], me * 768, 768, axis=1)
    sg['s5_d'] = lax.dynamic_slice_in_dim(sg['s5_d'], me * 512, 512, axis=1)
    sg['b_glu'] = lax.dynamic_slice_in_dim(sg['b_glu'], me * 512, 512, axis=1)
    sg = {n: sg[n].reshape(a[n].shape) for n in s_names}
    w_pack, m_pack, v_pack, g_pack = [_pack_rows([src[pre + n] for n in s_names])
                                      for src, pre in ((a, ''), (a, 'm_'), (a, 'v_'), (sg, ''))]
    sd, sm, sv = _rowwise(_adam_math, [w_pack, g_pack, m_pack, v_pack], [], [(128, F32)] * 3, name="adam_small")
    w_shapes = [tuple(a[n].shape) for n in s_names]
    res = {n: (sg[n], d_, m_, v_) for n, d_, m_, v_ in zip(s_names, _unpack_rows(sd, w_shapes), _unpack_rows(sm, w_shapes),
                                                          _unpack_rows(sv, w_shapes))}

    def adam_big(nm, gparts, g_off, shape2d):
        outs = _adamw(a[nm].reshape(shape2d), a['m_' + nm].reshape(shape2d), a['v_' + nm].reshape(shape2d), gparts,
                      name="adam_" + nm, g_off=g_off)
        res[nm] = tuple(o.reshape(a[nm].shape) for o in outs)

    for nm, _, r in _FAMILY:
        adam_big(nm, [grads[0]], views[nm, 0][1], (a[nm].shape[0] * r, PACK_COLS))
    adam_big('w_in_e', [grads[1]], 0, (D_MODEL, 2052))
    adam_big('w_ple_proj', [grads[2]], 0, (512, 512))

    return (loss, grad_x[None], *[res[n][0] for n in _WEIGHTS], *[res[n][1] for n in _WEIGHTS],
            *[res[n][2] for n in _WEIGHTS], *[res[n][3] for n in _WEIGHTS])


def kernel(x, p, norm_mix, norm_mlp, norm_ple, w_in_e, w_out_e, hgrn_lb, g_norm_a, conv_w, a_log, dt_bias, g_norm_b, s5_a_re, s5_a_im, s5_b_re, s5_b_im, s5_c_re, s5_c_im, s5_d, s5_log_dt, w_glu, b_glu, w_out_o, w_up, w_down, w_ple_gate, w_ple_proj, final_norm, loss_target, m_norm_mix, m_norm_mlp, m_norm_ple, m_w_in_e, m_w_out_e, m_hgrn_lb, m_g_norm_a, m_conv_w, m_a_log, m_dt_bias, m_g_norm_b, m_s5_a_re, m_s5_a_im, m_s5_b_re, m_s5_b_im, m_s5_c_re, m_s5_c_im, m_s5_d, m_s5_log_dt, m_w_glu, m_b_glu, m_w_out_o, m_w_up, m_w_down, m_w_ple_gate, m_w_ple_proj, m_final_norm, v_norm_mix, v_norm_mlp, v_norm_ple, v_w_in_e, v_w_out_e, v_hgrn_lb, v_g_norm_a, v_conv_w, v_a_log, v_dt_bias, v_g_norm_b, v_s5_a_re, v_s5_a_im, v_s5_b_re, v_s5_b_im, v_s5_c_re, v_s5_c_im, v_s5_d, v_s5_log_dt, v_w_glu, v_b_glu, v_w_out_o, v_w_up, v_w_down, v_w_ple_gate, v_w_ple_proj, v_final_norm):
    args = (x, p, norm_mix, norm_mlp, norm_ple, w_in_e, w_out_e, hgrn_lb, g_norm_a, conv_w, a_log, dt_bias, g_norm_b, s5_a_re, s5_a_im, s5_b_re, s5_b_im, s5_c_re, s5_c_im, s5_d, s5_log_dt, w_glu, b_glu, w_out_o, w_up, w_down, w_ple_gate, w_ple_proj, final_norm, loss_target, m_norm_mix, m_norm_mlp, m_norm_ple, m_w_in_e, m_w_out_e, m_hgrn_lb, m_g_norm_a, m_conv_w, m_a_log, m_dt_bias, m_g_norm_b, m_s5_a_re, m_s5_a_im, m_s5_b_re, m_s5_b_im, m_s5_c_re, m_s5_c_im, m_s5_d, m_s5_log_dt, m_w_glu, m_b_glu, m_w_out_o, m_w_up, m_w_down, m_w_ple_gate, m_w_ple_proj, m_final_norm, v_norm_mix, v_norm_mlp, v_norm_ple, v_w_in_e, v_w_out_e, v_hgrn_lb, v_g_norm_a, v_conv_w, v_a_log, v_dt_bias, v_g_norm_b, v_s5_a_re, v_s5_a_im, v_s5_b_re, v_s5_b_im, v_s5_c_re, v_s5_c_im, v_s5_d, v_s5_log_dt, v_w_glu, v_b_glu, v_w_out_o, v_w_up, v_w_down, v_w_ple_gate, v_w_ple_proj, v_final_norm)
    return _step(dict(zip(_INPUTS, args)))
```

```python
import functools
import math

import jax
import jax.numpy as jnp
from jax import lax
from jax.experimental import pallas as pl
from jax.experimental.pallas import tpu as pltpu

F32 = jnp.float32
BF16 = jnp.bfloat16

D_MODEL = 2048
SEQ = 4096
NORM_EPS = 1e-6
CHUNK = 64
HEAD = 128
VMEM_LIMIT = 56 * 1024 * 1024


_NN = (((1,), (0,)), ((), ()))
_NT = (((1,), (1,)), ((), ()))
_TN = (((0,), (0,)), ((), ()))
_HI = lax.Precision.HIGHEST
_NEG = -1e30


def _cparams(sem, **kw):
    return pltpu.CompilerParams(dimension_semantics=sem, vmem_limit_bytes=VMEM_LIMIT, **kw)


PACK_COLS = 2048


def _view_shape(view):
    kind, _, r = view
    return (4 * r, PACK_COLS) if kind == "row" else (r, 4 * PACK_COLS)


def _view_spec(view, rb, cb, row_of, col_of):
    kind, off, r = view
    assert off % rb == 0 and r % rb == 0 and PACK_COLS % cb == 0, (view, rb, cb)
    if kind == "row":
        nrb = r // rb
        return pl.BlockSpec((None, rb, cb), lambda i, j, k: (row_of(i, j, k) // nrb,
                                                             off // rb + row_of(i, j, k) % nrb, col_of(i, j, k)))
    ncb = PACK_COLS // cb
    return pl.BlockSpec((None, rb, cb), lambda i, j, k: (col_of(i, j, k) // ncb,
                                                         off // rb + row_of(i, j, k), col_of(i, j, k) % ncb))


def _mm(a, b, *, dims="nn", epilogue=None, extras=(), out_dtypes=(F32,), bm=1024, bn=1024, bk=1024, name,
        b_view=None, out_into=None, out_view=None):
    b_shape = _view_shape(b_view) if b_view is not None else b.shape
    if dims == "tn":
        (K, M), (K2, N) = a.shape, b_shape
    elif dims == "nt":
        (M, K), (N, K2) = a.shape, b_shape
    else:
        (M, K), (K2, N) = a.shape, b_shape
    assert K == K2, (a.shape, b_shape, dims)
    if b_view is not None and dims == "nt":
        bn = min(bn, b_view[2])
    if b_view is not None and dims != "nt":
        bk = min(bk, b_view[2])
    if out_view is not None:
        bm = min(bm, out_view[2])
    bm, bn, bk = min(bm, M), min(bn, N), min(bk, K)
    assert M % bm == 0 and N % bn == 0 and K % bk == 0, (M, N, K, bm, bn, bk)
    nk = K // bk
    ii, jj, kk = (lambda i, j, k: i), (lambda i, j, k: j), (lambda i, j, k: k)
    if dims == "tn":
        a_spec = pl.BlockSpec((bk, bm), lambda i, j, k: (k, i))
        dn = _TN
    else:
        a_spec = pl.BlockSpec((bm, bk), lambda i, j, k: (i, k))
        dn = _NT if dims == "nt" else _NN
    if dims == "nt":
        b_spec = _view_spec(b_view, bn, bk, jj, kk) if b_view else pl.BlockSpec((bn, bk), lambda i, j, k: (j, k))
    else:
        b_spec = _view_spec(b_view, bk, bn, kk, jj) if b_view else pl.BlockSpec((bk, bn), lambda i, j, k: (k, j))
    e_specs = []
    for e in extras:
        if e.shape == (M, N):
            e_specs.append(pl.BlockSpec((bm, bn), lambda i, j, k: (i, j)))
        else:
            assert e.shape == (1, N), e.shape
            e_specs.append(pl.BlockSpec((1, bn), lambda i, j, k: (0, j)))
    ne, no = len(extras), len(out_dtypes)
    if epilogue is None:
        epilogue = lambda acc: (acc,)
    into = out_into is not None

    def body(a_ref, b_ref, *rest):
        e_refs, rest = rest[:ne], rest[ne + (1 if into else 0):]
        o_refs, acc = rest[:no], rest[no]
        k = pl.program_id(2)

        @pl.when(k == 0)
        def _():
            acc[...] = jnp.zeros_like(acc)

        acc[...] += lax.dot_general(a_ref[...].astype(BF16), b_ref[...].astype(BF16), dn,
                                    preferred_element_type=F32)

        @pl.when(k == nk - 1)
        def _():
            outs = epilogue(acc[...], *[e[...] for e in e_refs])
            for o, v in zip(o_refs, outs):
                o[...] = v.astype(o.dtype)

    if into:
        assert no == 1 and _view_shape(out_view) == (M, N), (out_view, M, N)
        out_specs = [_view_spec(out_view, bm, bn, ii, jj)]
        out_shape = [jax.ShapeDtypeStruct(out_into.shape, out_into.dtype)]
        extra_in, extra_specs, alias = [out_into], [pl.BlockSpec(memory_space=pl.ANY)], {2 + ne: 0}
    else:
        out_specs = [pl.BlockSpec((bm, bn), lambda i, j, k: (i, j)) for _ in out_dtypes]
        out_shape = [jax.ShapeDtypeStruct((M, N), dt) for dt in out_dtypes]
        extra_in, extra_specs, alias = [], [], {}
    outs = pl.pallas_call(
        body, name=name,
        grid=(M // bm, N // bn, nk),
        in_specs=[a_spec, b_spec] + e_specs + extra_specs,
        out_specs=out_specs, out_shape=out_shape,
        scratch_shapes=[pltpu.VMEM((bm, bn), F32)],
        input_output_aliases=alias,
        compiler_params=_cparams(("parallel", "parallel", "arbitrary")),
    )(a, b, *extras, *extra_in)
    return outs[0] if no == 1 else tuple(outs)


def _rowwise(fn, tiled, bcast, outs, sums=(), *, rows=256, name):
    S = tiled[0].shape[0]
    rows = min(rows, S)
    assert S % rows == 0
    nt, nb, no, ns = len(tiled), len(bcast), len(outs), len(sums)

    def body(*refs):
        t_refs, b_refs = refs[:nt], refs[nt:nt + nb]
        o_refs, s_refs = refs[nt + nb:nt + nb + no], refs[nt + nb + no:]
        res = fn(*[r[...] for r in t_refs], *[r[...] for r in b_refs])
        res = res if isinstance(res, (tuple, list)) else (res,)
        for o, v in zip(o_refs, res[:no]):
            o[...] = v.astype(o.dtype)
        if ns:
            @pl.when(pl.program_id(0) == 0)
            def _():
                for s in s_refs:
                    s[...] = jnp.zeros_like(s)
            for s, v in zip(s_refs, res[no:]):
                s[...] += v

    res = pl.pallas_call(
        body, name=name,
        grid=(S // rows,),
        in_specs=[pl.BlockSpec((rows, t.shape[1]), lambda i: (i, 0)) for t in tiled]
        + [pl.BlockSpec(b.shape, lambda i, nd=b.ndim: (0,) * nd) for b in bcast],
        out_specs=[pl.BlockSpec((rows, c), lambda i: (i, 0)) for c, _ in outs]
        + [pl.BlockSpec((1, c), lambda i: (0, 0)) for c in sums],
        out_shape=[jax.ShapeDtypeStruct((S, c), dt) for c, dt in outs]
        + [jax.ShapeDtypeStruct((1, c), F32) for c in sums],
        compiler_params=_cparams(("arbitrary",)),
    )(*tiled, *bcast)
    return res[0] if len(res) == 1 else tuple(res)


def _rms(x, g):
    return x * lax.rsqrt(jnp.mean(x * x, axis=-1, keepdims=True) + NORM_EPS) * g


def _mxu(a, b, dn):
    if a.ndim == 3:
        ((ca,), (cb,)), _ = dn
        dn = (((ca + 1,), (cb + 1,)), ((0,), (0,)))
    return lax.dot_general(a, b, dn, preferred_element_type=F32)


def _split(a):
    hi = a.astype(BF16)
    return hi, (a - hi.astype(F32)).astype(BF16)


def _passes(a, b, dn, three):
    if not three:
        return _mxu(a.astype(BF16), b.astype(BF16), dn)
    (ah, al), (bh, bl) = _split(a), _split(b)
    return _mxu(ah, bh, dn) + (_mxu(ah, bl, dn) + _mxu(al, bh, dn))


def _dot_grads(a, b, g, dn, three):
    if dn == _NN:
        return _passes(g, b, _NT, three), _passes(a, g, _TN, three)
    if dn == _NT:
        return _passes(g, b, _NN, three), _passes(g, a, _TN, three)
    assert dn == _TN
    return _passes(b, g, _NT, three), _passes(a, g, _NN, three)


@functools.partial(jax.custom_vjp, nondiff_argnums=(2,))
def _dot(a, b, dn=_NN):
    return _passes(a, b, dn, False)


_dot.defvjp(lambda a, b, dn: (_passes(a, b, dn, False), (a, b)),
            lambda dn, res, g: _dot_grads(res[0], res[1], g, dn, False))


@functools.partial(jax.custom_vjp, nondiff_argnums=(2,))
def _dot3(a, b, dn=_NN):
    return _passes(a, b, dn, True)


_dot3.defvjp(lambda a, b, dn: (_passes(a, b, dn, True), (a, b)),
             lambda dn, res, g: _dot_grads(res[0], res[1], g, dn, True))


def _tri_times(x, dn):
    tri = _tril(x.shape[-2]).astype(BF16)
    if x.ndim == 3:
        tri = jnp.broadcast_to(tri, (x.shape[0],) + tri.shape)
    hi, lo = _split(x)
    lo2 = (x - hi.astype(F32) - lo.astype(F32)).astype(BF16)
    return _mxu(tri, hi, dn) + (_mxu(tri, lo, dn) + _mxu(tri, lo2, dn))


@jax.custom_vjp
def _cumsum_rows(x):
    return _tri_times(x, _NN)


_cumsum_rows.defvjp(lambda x: (_tri_times(x, _NN), None), lambda _, g: (_tri_times(g, _TN),))


def _iota(shape, dim):
    return lax.broadcasted_iota(jnp.int32, shape, dim)


def _tril(n, strict=False):
    r, c = _iota((n, n), 0), _iota((n, n), 1)
    return (r > c) if strict else (r >= c)


@functools.partial(jax.custom_vjp, nondiff_argnums=(1,))
def _roll_rows(x, r):
    return pltpu.roll(x, r, 0)


def _roll_rows_fwd(x, r):
    return pltpu.roll(x, r, 0), None


def _roll_rows_bwd(r, _, g):
    return (pltpu.roll(g, (g.shape[0] - r) % g.shape[0], 0),)


_roll_rows.defvjp(_roll_rows_fwd, _roll_rows_bwd)


def _head_norm_gate(o, gn, gate):
    return _rms(o, gn) * jax.nn.silu(gate)


_SUB = 16
_HPS = 8


def _hgrn_chunk(q, fpre, v, gate, lb, gna, st):
    c = q.shape[0]
    forget = lb + (1.0 - lb) * jax.nn.sigmoid(fpre)
    k = 1.0 - forget
    logf = jnp.log(forget)
    cum = _cumsum_rows(logf)
    cum_end = jnp.sum(logf, axis=0, keepdims=True)
    o = _dot(q * jnp.exp(cum), st, _NT)
    st_new = st * jnp.exp(cum_end) + _dot(v, k * jnp.exp(cum_end - cum), _TN)
    t = _iota((c, 1), 0)
    s_off = jnp.zeros((c, c), F32)
    for i in range(1, c // _SUB):
        before = t < i * _SUB
        c_i = jnp.sum(jnp.where(before, logf, 0.0), axis=0, keepdims=True)
        in_blk = (t >= i * _SUB) & (t < (i + 1) * _SUB)
        qi = jnp.where(in_blk, q * jnp.exp(jnp.minimum(cum - c_i, 0.0)), 0.0)
        ki = jnp.where(before, k * jnp.exp(jnp.minimum(c_i - cum, 0.0)), 0.0)
        s_off = s_off + _dot(qi, ki, _NT)
    o = o + _dot(s_off, v)
    tmod = t % _SUB
    for r in range(_SUB):
        kr, cr, vr = (k, cum, v) if r == 0 else (_roll_rows(k, r), _roll_rows(cum, r), _roll_rows(v, r))
        w = q * kr * jnp.exp(jnp.where(tmod >= r, cum - cr, _NEG))
        o = o + jnp.sum(w, axis=1, keepdims=True) * vr
    return _head_norm_gate(o, gna, gate), st_new


def _hgrn_fwd(proj, lb, gna):
    S = proj.shape[0]
    n_chunks, heads = S // CHUNK, 8

    def body(q_ref, f_ref, v_ref, g_ref, lb_ref, gna_ref, o_ref, st_out, st):
        n, hb = pl.program_id(0), pl.program_id(1)

        @pl.when(n == 0)
        def _():
            st[hb] = jnp.zeros((_HPS, HEAD, HEAD), F32)

        sls = [slice(hh * HEAD, (hh + 1) * HEAD) for hh in range(_HPS)]
        st0 = [st[hb, hh] for hh in range(_HPS)]
        ins = [(q_ref[:, sl], f_ref[:, sl], v_ref[:, sl], g_ref[:, sl], lb_ref[:, sl]) for sl in sls]
        gna = gna_ref[...]
        res = [_hgrn_chunk(*ins[hh], gna, st0[hh]) for hh in range(_HPS)]
        for hh, sl in enumerate(sls):
            st_out[hh] = st0[hh]
            o_ref[:, sl] = res[hh][0].astype(o_ref.dtype)
            st[hb, hh] = res[hh][1]

    wide = _HPS * HEAD
    sec = lambda s: pl.BlockSpec((CHUNK, wide), lambda n, h, s=s: (n, (8 // _HPS) * s + h))
    return pl.pallas_call(
        body, name="hgrn_fwd", grid=(n_chunks, heads // _HPS),
        in_specs=[sec(0), sec(1), sec(2), sec(3),
                  pl.BlockSpec((1, wide), lambda n, h: (0, h)), pl.BlockSpec((1, HEAD), lambda n, h: (0, 0))],
        out_specs=[pl.BlockSpec((CHUNK, wide), lambda n, h: (n, h)),
                   pl.BlockSpec((None, _HPS, HEAD, HEAD), lambda n, h: (n, h, 0, 0))],
        out_shape=[jax.ShapeDtypeStruct((S, 1024), BF16),
                   jax.ShapeDtypeStruct((n_chunks, heads, HEAD, HEAD), F32)],
        scratch_shapes=[pltpu.VMEM((heads // _HPS, _HPS, HEAD, HEAD), F32)],
        compiler_params=_cparams(("arbitrary", "arbitrary")),
    )(proj, proj, proj, proj, lb, gna)


def _hgrn_bwd(proj, lb, gna, states, d_o):
    S = proj.shape[0]
    n_chunks, heads = S // CHUNK, 8

    def body(q_ref, f_ref, v_ref, g_ref, lb_ref, gna_ref, st_ref, do_ref,
             dq_ref, df_ref, dv_ref, dg_ref, dlb_ref, dgna_ref, dst):
        n, hb = pl.program_id(0), pl.program_id(1)

        @pl.when(n == 0)
        def _():
            dst[hb] = jnp.zeros((_HPS, HEAD, HEAD), F32)
            dlb_ref[hb] = jnp.zeros((_HPS, 1, HEAD), F32)

        @pl.when((n == 0) & (hb == 0))
        def _():
            dgna_ref[...] = jnp.zeros_like(dgna_ref)

        sls = [slice(hh * HEAD, (hh + 1) * HEAD) for hh in range(_HPS)]
        gna = gna_ref[...]
        ins = [(q_ref[:, sl], f_ref[:, sl], v_ref[:, sl], g_ref[:, sl], lb_ref[:, sl], gna, st_ref[hh])
               for hh, sl in enumerate(sls)]
        cts = [(do_ref[:, sl], dst[hb, hh]) for hh, sl in enumerate(sls)]
        grads = [jax.vjp(_hgrn_chunk, *ins[hh])[1](cts[hh]) for hh in range(_HPS)]
        dgna_sum = jnp.zeros((1, HEAD), F32)
        for hh, sl in enumerate(sls):
            dq, df, dv, dg, dlb, dgna, dst0 = grads[hh]
            dq_ref[:, sl] = dq.astype(dq_ref.dtype)
            df_ref[:, sl] = df.astype(df_ref.dtype)
            dv_ref[:, sl] = dv.astype(dv_ref.dtype)
            dg_ref[:, sl] = dg.astype(dg_ref.dtype)
            dlb_ref[hb, hh] += dlb
            dgna_sum = dgna_sum + dgna
            dst[hb, hh] = dst0
        dgna_ref[...] += dgna_sum

    rev = lambda n: n_chunks - 1 - n
    wide = _HPS * HEAD
    sec = lambda s: pl.BlockSpec((CHUNK, wide), lambda n, h, s=s: (rev(n), (8 // _HPS) * s + h))
    out = pl.BlockSpec((CHUNK, wide), lambda n, h: (rev(n), h))
    return pl.pallas_call(
        body, name="hgrn_bwd", grid=(n_chunks, heads // _HPS),
        in_specs=[sec(0), sec(1), sec(2), sec(3),
                  pl.BlockSpec((1, wide), lambda n, h: (0, h)), pl.BlockSpec((1, HEAD), lambda n, h: (0, 0)),
                  pl.BlockSpec((None, _HPS, HEAD, HEAD), lambda n, h: (rev(n), h, 0, 0)),
                  pl.BlockSpec((CHUNK, wide), lambda n, h: (rev(n), h))],
        out_specs=[out, out, out, out,
                   pl.BlockSpec((heads // _HPS, _HPS, 1, HEAD), lambda n, h: (0, 0, 0, 0)),
                   pl.BlockSpec((1, HEAD), lambda n, h: (0, 0))],
        out_shape=[jax.ShapeDtypeStruct((S, 1024), BF16)] * 4
        + [jax.ShapeDtypeStruct((heads // _HPS, _HPS, 1, HEAD), F32), jax.ShapeDtypeStruct((1, HEAD), F32)],
        scratch_shapes=[pltpu.VMEM((heads // _HPS, _HPS, HEAD, HEAD), F32)],
        compiler_params=_cparams(("arbitrary", "arbitrary")),
    )(proj, proj, proj, proj, lb, gna, states, d_o)


def _l2n(t):
    return t * lax.rsqrt(jnp.sum(t * t, axis=-1, keepdims=True) + NORM_EPS)


def _unit_lower_inverse(low):
    c = low.shape[-1]
    inv = (_iota((c, c), 0) == _iota((c, c), 1)).astype(F32) - low
    p = low
    span = 2
    while span < c:
        p = _dot3(p, p)
        inv = inv + _dot3(inv, p)
        span *= 2
    return inv


def _gdn_chunk(qc, kc, v, z, tail, h0, arow, dtrow, gnb, st):
    nb, c = qc.shape[0], qc.shape[1]
    head = h0 + _iota((nb, c, HEAD), 0)
    lane = _iota((nb, c, HEAD), 2)
    la_all = arow * jax.nn.softplus(tail + dtrow)
    la = jnp.sum(jnp.where(lane == head, la_all[None], 0.0), axis=2, keepdims=True)
    beta = jnp.sum(jnp.where(lane == head + 8, jax.nn.sigmoid(tail)[None], 0.0), axis=2, keepdims=True)
    la_b = jnp.broadcast_to(la, (nb, c, HEAD))
    cum = _cumsum_rows(la_b)
    cmat = _cumsum_rows(jnp.broadcast_to(la, (nb, c, c)))
    cum_end = jnp.sum(la_b, axis=1, keepdims=True)
    decay = jnp.exp(jnp.where(_tril(c), cmat - jnp.swapaxes(cmat, 1, 2), _NEG))
    q = _l2n(qc) * (HEAD ** -0.5)
    k = _l2n(kc)
    k_beta = k * beta
    low = jnp.where(_tril(c, strict=True), _dot(k_beta, k, _NT) * decay, 0.0)
    inv = _unit_lower_inverse(low)
    u, w = _dot(inv, v * beta), _dot(inv, k_beta * jnp.exp(cum))
    intra = _dot(q, k, _NT) * decay
    v_new = u - _dot(w, st, _NT)
    o = _dot(q * jnp.exp(cum), st, _NT) + _dot(intra, v_new)
    st_new = st * jnp.exp(cum_end) + _dot(v_new, k * jnp.exp(cum_end - cum), _TN)
    return _head_norm_gate(o, gnb, z), st_new


def _gdn_specs(n_of):
    wide = _HPS * HEAD
    qkv = lambda s: pl.BlockSpec((CHUNK, wide), lambda n, h, s=s: (n_of(n), (8 // _HPS) * s + h))
    return [qkv(0), qkv(1), qkv(2),
            pl.BlockSpec((CHUNK, wide), lambda n, h: (n_of(n), 56 // _HPS + h)),
            pl.BlockSpec((CHUNK, HEAD), lambda n, h: (n_of(n), 64)),
            pl.BlockSpec((1, HEAD), lambda n, h: (0, 0)), pl.BlockSpec((1, HEAD), lambda n, h: (0, 0)),
            pl.BlockSpec((1, HEAD), lambda n, h: (0, 0))]


def _gdn_fwd(act, proj, arow, dtrow, gnb):
    S = act.shape[0]
    n_chunks, heads = S // CHUNK, 8

    def body(q_ref, k_ref, v_ref, z_ref, t_ref, a_ref, dt_ref, gnb_ref, o_ref, st_out, st):
        n, hb = pl.program_id(0), pl.program_id(1)

        @pl.when(n == 0)
        def _():
            st[hb] = jnp.zeros((_HPS, HEAD, HEAD), F32)

        sls = [slice(hh * HEAD, (hh + 1) * HEAD) for hh in range(_HPS)]
        heads_of = lambda ref: jnp.stack([ref[:, sl] for sl in sls])
        st0 = st[hb]
        o, st1 = _gdn_chunk(heads_of(q_ref), heads_of(k_ref), heads_of(v_ref), heads_of(z_ref), t_ref[...], hb * _HPS,
                            a_ref[...], dt_ref[...], gnb_ref[...], st0)
        st_out[...] = st0
        st[hb] = st1
        for hh, sl in enumerate(sls):
            o_ref[:, sl] = o[hh].astype(o_ref.dtype)

    return pl.pallas_call(
        body, name="gdn_fwd", grid=(n_chunks, heads // _HPS),
        in_specs=_gdn_specs(lambda n: n),
        out_specs=[pl.BlockSpec((CHUNK, _HPS * HEAD), lambda n, h: (n, h)),
                   pl.BlockSpec((None, _HPS, HEAD, HEAD), lambda n, h: (n, h, 0, 0))],
        out_shape=[jax.ShapeDtypeStruct((S, 1024), BF16),
                   jax.ShapeDtypeStruct((n_chunks, heads, HEAD, HEAD), F32)],
        scratch_shapes=[pltpu.VMEM((heads // _HPS, _HPS, HEAD, HEAD), F32)],
        compiler_params=_cparams(("arbitrary", "arbitrary")),
    )(act, act, act, proj, proj, arow, dtrow, gnb)


def _gdn_bwd(act, proj, arow, dtrow, gnb, states, d_o):
    S = act.shape[0]
    n_chunks, heads = S // CHUNK, 8

    def body(q_ref, k_ref, v_ref, z_ref, t_ref, a_ref, dt_ref, gnb_ref, st_ref, do_ref,
             dq_ref, dk_ref, dv_ref, dz_ref, dt_out, da_ref, ddt_ref, dgnb_ref, dst, dtail_acc):
        n, hb = pl.program_id(0), pl.program_id(1)
        n_hb = heads // _HPS

        @pl.when(n == 0)
        def _():
            dst[hb] = jnp.zeros((_HPS, HEAD, HEAD), F32)

        @pl.when((n == 0) & (hb == 0))
        def _():
            da_ref[...] = jnp.zeros_like(da_ref)
            ddt_ref[...] = jnp.zeros_like(ddt_ref)
            dgnb_ref[...] = jnp.zeros_like(dgnb_ref)

        sls = [slice(hh * HEAD, (hh + 1) * HEAD) for hh in range(_HPS)]
        heads_of = lambda ref: jnp.stack([ref[:, sl] for sl in sls])
        fn = lambda qc, kc, v, z, tail, arow, dtrow, gnb, st: _gdn_chunk(qc, kc, v, z, tail, hb * _HPS, arow, dtrow,
                                                                        gnb, st)
        _, vjp = jax.vjp(fn, heads_of(q_ref), heads_of(k_ref), heads_of(v_ref), heads_of(z_ref), t_ref[...],
                         a_ref[...], dt_ref[...], gnb_ref[...], st_ref[...])
        dq, dk, dv, dz, dtail_sum, da_sum, ddt_sum, dgnb_sum, dst0 = vjp((heads_of(do_ref), dst[hb]))
        dst[hb] = dst0
        for hh, sl in enumerate(sls):
            dq_ref[:, sl] = dq[hh]
            dk_ref[:, sl] = dk[hh]
            dv_ref[:, sl] = dv[hh]
            dz_ref[:, sl] = dz[hh].astype(dz_ref.dtype)

        @pl.when(hb == 0)
        def _():
            dtail_acc[...] = dtail_sum

        @pl.when(hb > 0)
        def _():
            dtail_acc[...] += dtail_sum

        @pl.when(hb == n_hb - 1)
        def _():
            dt_out[...] = dtail_acc[...].astype(dt_out.dtype)

        da_ref[...] += da_sum
        ddt_ref[...] += ddt_sum
        dgnb_ref[...] += dgnb_sum

    rev = lambda n: n_chunks - 1 - n
    out = pl.BlockSpec((CHUNK, _HPS * HEAD), lambda n, h: (rev(n), h))
    row = pl.BlockSpec((1, HEAD), lambda n, h: (0, 0))
    return pl.pallas_call(
        body, name="gdn_bwd", grid=(n_chunks, heads // _HPS),
        in_specs=_gdn_specs(rev)
        + [pl.BlockSpec((None, _HPS, HEAD, HEAD), lambda n, h: (rev(n), h, 0, 0)),
           pl.BlockSpec((CHUNK, _HPS * HEAD), lambda n, h: (rev(n), 8 // _HPS + h))],
        out_specs=[out, out, out, out, pl.BlockSpec((CHUNK, HEAD), lambda n, h: (rev(n), 0)), row, row, row],
        out_shape=[jax.ShapeDtypeStruct((S, 1024), F32)] * 3
        + [jax.ShapeDtypeStruct((S, 1024), BF16), jax.ShapeDtypeStruct((S, HEAD), BF16)]
        + [jax.ShapeDtypeStruct((1, HEAD), F32)] * 3,
        scratch_shapes=[pltpu.VMEM((heads // _HPS, _HPS, HEAD, HEAD), F32), pltpu.VMEM((CHUNK, HEAD), F32)],
        compiler_params=_cparams(("arbitrary", "arbitrary")),
    )(act, act, act, proj, proj, arow, dtrow, gnb, states, d_o)


def _conv_silu(x, w):
    t = _iota((x.shape[0], 1), 0)
    tap = _iota(w.shape, 0)
    y = jnp.zeros_like(x)
    for r in range(4):
        w_r = jnp.sum(jnp.where(tap == 3 - r, w, 0.0), axis=0, keepdims=True)
        y = y + (x if r == 0 else jnp.where(t >= r, _roll_rows(x, r), 0.0)) * w_r
    return jax.nn.silu(y)


_CONV_COLS = 128


def _conv_fwd(proj, conv_w):
    S = proj.shape[0]
    nb = 3072 // _CONV_COLS
    off = 4096 // _CONV_COLS

    def body(x_ref, w_ref, o_ref):
        o_ref[...] = _conv_silu(x_ref[...], w_ref[...])

    return pl.pallas_call(
        body, name="conv_fwd", grid=(nb,),
        in_specs=[pl.BlockSpec((S, _CONV_COLS), lambda j: (0, off + j)), pl.BlockSpec((4, _CONV_COLS), lambda j: (0, j))],
        out_specs=pl.BlockSpec((S, _CONV_COLS), lambda j: (0, j)),
        out_shape=jax.ShapeDtypeStruct((S, 3072), F32),
        compiler_params=_cparams(("parallel",)),
    )(proj, conv_w)


def _conv_bwd(proj, conv_w, dq, dk, dv):
    S = proj.shape[0]
    nb = 3072 // _CONV_COLS
    off = 4096 // _CONV_COLS
    per = 1024 // _CONV_COLS

    def body(x_ref, w_ref, dq_ref, dk_ref, dv_ref, dx_ref, dw_ref):
        j = pl.program_id(0)
        _, vjp = jax.vjp(_conv_silu, x_ref[...], w_ref[...])
        d = jnp.where(j < per, dq_ref[...], jnp.where(j < 2 * per, dk_ref[...], dv_ref[...]))
        dx, dw = vjp(d)
        dx_ref[...] = dx.astype(dx_ref.dtype)
        dw_ref[...] = dw

    dsp = lambda s: pl.BlockSpec((S, _CONV_COLS), lambda j, s=s: (0, jnp.clip(j - s * per, 0, per - 1)))
    return pl.pallas_call(
        body, name="conv_bwd", grid=(nb,),
        in_specs=[pl.BlockSpec((S, _CONV_COLS), lambda j: (0, off + j)), pl.BlockSpec((4, _CONV_COLS), lambda j: (0, j)),
                  dsp(0), dsp(1), dsp(2)],
        out_specs=[pl.BlockSpec((S, _CONV_COLS), lambda j: (0, j)), pl.BlockSpec((4, _CONV_COLS), lambda j: (0, j))],
        out_shape=[jax.ShapeDtypeStruct((S, 3072), BF16), jax.ShapeDtypeStruct((4, 3072), F32)],
        compiler_params=_cparams(("parallel",)),
    )(proj, conv_w, dq, dk, dv)


_S5_T = 512
_S5_L = 512
_S5_NB = 16
_S5_U = 4
_S5_UB = 3


def _s5_tile(xr, xi, pw_r, pw_i, cr, ci, reverse):
    t8 = _iota((8, 1), 0)
    for sh in (1, 2, 4):
        row = (8 - sh) if reverse else (sh - 1)
        ar, ai = pw_r[row:row + 1, :], pw_i[row:row + 1, :]
        if reverse:
            keep, amt = t8 < 8 - sh, 8 - sh
        else:
            keep, amt = t8 >= sh, sh
        sr = jnp.where(keep, pltpu.roll(xr, amt, 0), 0.0)
        si = jnp.where(keep, pltpu.roll(xi, amt, 0), 0.0)
        xr, xi = xr + ar * sr - ai * si, xi + ar * si + ai * sr
    xr, xi = xr + pw_r * cr - pw_i * ci, xi + pw_r * ci + pw_i * cr
    return xr, xi


def _s5_fwd(u, bre, bim, cre, cim, pw_r, pw_i, dskip):
    S = u.shape[0]
    T = min(_S5_T, S)
    nt = S // T

    def body(u_ref, bre_ref, bim_ref, cre_ref, cim_ref, pr_ref, pi_ref, d_ref, y_ref, xr_ref, xi_ref,
             bu_r, bu_i, car_r, car_i):
        @pl.when(pl.program_id(1) == 0)
        def _():
            car_r[...] = jnp.zeros_like(car_r)
            car_i[...] = jnp.zeros_like(car_i)

        uu = u_ref[...]
        bu_r[...] = _dot(uu, bre_ref[...])
        bu_i[...] = _dot(uu, bim_ref[...])
        pw_r, pw_i = pr_ref[...], pi_ref[...]

        def tiles(i, carry):
            ins = []
            for k in range(_S5_U):
                r0 = pl.multiple_of((i * _S5_U + k) * 8, 8)
                ins.append((r0, bu_r[pl.ds(r0, 8), :], bu_i[pl.ds(r0, 8), :]))
            outs = []
            for r0, br, bi in ins:
                xr, xi = _s5_tile(br, bi, pw_r, pw_i, carry[0], carry[1], False)
                carry = (xr[7:8, :], xi[7:8, :])
                outs.append((r0, xr, xi))
            for r0, xr, xi in outs:
                xr_ref[pl.ds(r0, 8), :] = xr
                xi_ref[pl.ds(r0, 8), :] = xi
            return carry

        cr, ci = lax.fori_loop(0, T // (8 * _S5_U), tiles, (car_r[...], car_i[...]))
        car_r[...] = cr
        car_i[...] = ci
        y_ref[...] = _dot(xr_ref[...], cre_ref[...]) - _dot(xi_ref[...], cim_ref[...]) + d_ref[...] * uu

    blk3 = lambda a, b: pl.BlockSpec((None, a, b), lambda j, t: (j, 0, 0))
    return pl.pallas_call(
        body, name="s5_fwd", grid=(_S5_NB, nt),
        in_specs=[pl.BlockSpec((T, HEAD), lambda j, t: (t, j)),
                  blk3(HEAD, _S5_L), blk3(HEAD, _S5_L), blk3(_S5_L, HEAD), blk3(_S5_L, HEAD),
                  blk3(8, _S5_L), blk3(8, _S5_L), pl.BlockSpec((1, HEAD), lambda j, t: (0, j))],
        out_specs=[pl.BlockSpec((T, HEAD), lambda j, t: (t, j)),
                   pl.BlockSpec((T, _S5_L), lambda j, t: (t, j)), pl.BlockSpec((T, _S5_L), lambda j, t: (t, j))],
        out_shape=[jax.ShapeDtypeStruct((S, D_MODEL), F32),
                   jax.ShapeDtypeStruct((S, _S5_NB * _S5_L), F32), jax.ShapeDtypeStruct((S, _S5_NB * _S5_L), F32)],
        scratch_shapes=[pltpu.VMEM((T, _S5_L), F32), pltpu.VMEM((T, _S5_L), F32),
                        pltpu.VMEM((1, _S5_L), F32), pltpu.VMEM((1, _S5_L), F32)],
        compiler_params=_cparams(("parallel", "arbitrary")),
    )(u, bre, bim, cre, cim, pw_r, pw_i, dskip)


def _s5_bwd(dy, u, xre, xim, bre, bim, cre, cim, qw_r, qw_i, dskip):
    S = u.shape[0]
    T = min(_S5_T, S)
    nt = S // T
    nt8 = T // 8

    def body(dy_ref, u_ref, xr_ref, xi_ref, xpr_ref, xpi_ref, bre_ref, bim_ref, cre_ref, cim_ref, qr_ref, qi_ref,
             d_ref, du_ref, dbr_ref, dbi_ref, dcr_ref, dci_ref, dlr_ref, dli_ref, dd_ref,
             g_r, g_i, car_r, car_i):
        t = pl.program_id(1)

        @pl.when(t == 0)
        def _():
            car_r[...] = jnp.zeros_like(car_r)
            car_i[...] = jnp.zeros_like(car_i)
            for r in (dbr_ref, dbi_ref, dcr_ref, dci_ref, dlr_ref, dli_ref, dd_ref):
                r[...] = jnp.zeros_like(r)

        dyy, uu = dy_ref[...], u_ref[...]
        g_r[...] = _dot(dyy, cre_ref[...], _NT)
        g_i[...] = -_dot(dyy, cim_ref[...], _NT)
        qw_r, qw_i = qr_ref[...], qi_ref[...]
        t8 = _iota((8, 1), 0)
        first = t == nt - 1

        def load(r0, prev_r, prev_i):
            rows = pl.ds(r0, 8)
            return r0, g_r[rows, :], g_i[rows, :], xr_ref[rows, :], xi_ref[rows, :], prev_r, prev_i

        def run(loaded, carry, acc):
            done = []
            for r0, dr, di, xr, xi, prev_r, prev_i in loaded:
                gr, gi = _s5_tile(dr, di, qw_r, qw_i, carry[0], carry[1], True)
                carry = (gr[0:1, :], gi[0:1, :])
                xpr = jnp.where(t8 >= 1, pltpu.roll(xr, 1, 0), prev_r)
                xpi = jnp.where(t8 >= 1, pltpu.roll(xi, 1, 0), prev_i)
                acc = (acc[0] + gr * xpr + gi * xpi, acc[1] + gi * xpr - gr * xpi)
                done.append((r0, gr, gi))
            for r0, gr, gi in done:
                g_r[pl.ds(r0, 8), :] = gr
                g_i[pl.ds(r0, 8), :] = gi
            return carry, acc

        def step(ii, state):
            loaded = []
            for k in range(_S5_UB):
                idx = nt8 - 1 - (ii * _S5_UB + k)
                r0 = pl.multiple_of(idx * 8, 8)
                p0 = pl.multiple_of((idx - 1) * 8, 8)
                loaded.append(load(r0, xr_ref[pl.ds(p0, 8), :][7:8, :], xi_ref[pl.ds(p0, 8), :][7:8, :]))
            return run(loaded, *state)

        zero = jnp.zeros((8, _S5_L), F32)
        assert (nt8 - 1) % _S5_UB == 0
        state = lax.fori_loop(0, (nt8 - 1) // _S5_UB, step, ((car_r[...], car_i[...]), (zero, zero)))
        prev_r = jnp.where(first, 0.0, xpr_ref[...][7:8, :])
        prev_i = jnp.where(first, 0.0, xpi_ref[...][7:8, :])
        (cr, ci), (ar, ai) = run([load(0, prev_r, prev_i)], *state)
        car_r[...] = cr
        car_i[...] = ci
        dlr_ref[...] += ar
        dli_ref[...] += ai
        gr, gi = g_r[...], g_i[...]
        du_ref[...] = _dot(gr, bre_ref[...], _NT) + _dot(gi, bim_ref[...], _NT) + d_ref[...] * dyy
        dbr_ref[...] += _dot(uu, gr, _TN)
        dbi_ref[...] += _dot(uu, gi, _TN)
        dcr_ref[...] += _dot(xr_ref[...], dyy, _TN)
        dci_ref[...] -= _dot(xi_ref[...], dyy, _TN)
        dd_ref[...] += jnp.sum(dyy * uu, axis=0, keepdims=True)

    rev = lambda t: nt - 1 - t
    blk3 = lambda a, b: pl.BlockSpec((None, a, b), lambda j, t: (j, 0, 0))
    tl = pl.BlockSpec((T, HEAD), lambda j, t: (rev(t), j))
    xs = pl.BlockSpec((T, _S5_L), lambda j, t: (rev(t), j))
    xp = pl.BlockSpec((8, _S5_L), lambda j, t: (jnp.maximum(rev(t) * nt8 - 1, 0), j))
    return pl.pallas_call(
        body, name="s5_bwd", grid=(_S5_NB, nt),
        in_specs=[tl, tl, xs, xs, xp, xp, blk3(HEAD, _S5_L), blk3(HEAD, _S5_L), blk3(_S5_L, HEAD), blk3(_S5_L, HEAD),
                  blk3(8, _S5_L), blk3(8, _S5_L), pl.BlockSpec((1, HEAD), lambda j, t: (0, j))],
        out_specs=[tl, blk3(HEAD, _S5_L), blk3(HEAD, _S5_L), blk3(_S5_L, HEAD), blk3(_S5_L, HEAD),
                   blk3(8, _S5_L), blk3(8, _S5_L), pl.BlockSpec((1, HEAD), lambda j, t: (0, j))],
        out_shape=[jax.ShapeDtypeStruct((S, D_MODEL), F32),
                   jax.ShapeDtypeStruct((_S5_NB, HEAD, _S5_L), F32), jax.ShapeDtypeStruct((_S5_NB, HEAD, _S5_L), F32),
                   jax.ShapeDtypeStruct((_S5_NB, _S5_L, HEAD), F32), jax.ShapeDtypeStruct((_S5_NB, _S5_L, HEAD), F32),
                   jax.ShapeDtypeStruct((_S5_NB, 8, _S5_L), F32), jax.ShapeDtypeStruct((_S5_NB, 8, _S5_L), F32),
                   jax.ShapeDtypeStruct((1, D_MODEL), F32)],
        scratch_shapes=[pltpu.VMEM((T, _S5_L), F32), pltpu.VMEM((T, _S5_L), F32),
                        pltpu.VMEM((1, _S5_L), F32), pltpu.VMEM((1, _S5_L), F32)],
        compiler_params=_cparams(("parallel", "arbitrary")),
    )(dy, u, xre, xim, xre, xim, bre, bim, cre, cim, qw_r, qw_i, dskip)


def _s5_params(a_re, a_im, log_dt, b_re, b_im, c_re, c_im):
    step = jnp.exp(log_dt)[:, None]
    mag = jnp.exp(a_re * step)
    lr, li = mag * jnp.cos(a_im * step), mag * jnp.sin(a_im * step)
    den = a_re * a_re + a_im * a_im
    nr, ni = lr - 1.0, li
    kr, ki = (nr * a_re + ni * a_im) / den, (ni * a_re - nr * a_im) / den
    bbr = kr[..., None] * b_re - ki[..., None] * b_im
    bbi = kr[..., None] * b_im + ki[..., None] * b_re
    eye = jnp.eye(8, dtype=F32)

    def blk_b(bb):
        t = bb.reshape(_S5_NB, 8, 64, 16).transpose(0, 1, 3, 2)
        return (t[:, :, :, None, :] * eye[None, :, None, :, None]).reshape(_S5_NB, HEAD, _S5_L)

    def blk_c(cc):
        t = cc.reshape(_S5_NB, 8, 16, 64).transpose(0, 1, 3, 2)
        return (t[:, :, :, None, :] * eye[None, :, None, :, None]).reshape(_S5_NB, _S5_L, HEAD)

    return (blk_b(bbr), blk_b(bbi), blk_c(c_re), blk_c(c_im),
            lr.reshape(_S5_NB, _S5_L), li.reshape(_S5_NB, _S5_L))


def _s5_powers(lr, li):
    pr, pi = [lr], [li]
    for _ in range(7):
        pr, pi = pr + [pr[-1] * lr - pi[-1] * li], pi + [pr[-1] * li + pi[-1] * lr]
    return jnp.stack(pr, axis=1), jnp.stack(pi, axis=1)


_MESH = pl.DeviceIdType.MESH
_ANY = pl.BlockSpec(memory_space=pl.ANY)


def _place():
    x, y, c = lax.axis_index("x"), lax.axis_index("y"), lax.axis_index("c")
    return x, y, c, [(1 - x, y), (x, 1 - y), (1 - x, 1 - y)]


def _comm_call(body, arrs, out_shapes, n_remote, name):
    n = len(arrs)
    return pl.pallas_call(
        body, name=name,
        in_specs=[_ANY] * n, out_specs=[_ANY] * n, out_shape=out_shapes,
        scratch_shapes=[pltpu.SemaphoreType.DMA((n * n_remote,)), pltpu.SemaphoreType.DMA((n * n_remote,)),
                        pltpu.SemaphoreType.DMA((n,))],
    )(*arrs)


def _ag_xy(arrs, name):
    n = len(arrs)

    def body(*refs):
        ins, outs, (ssem, rsem, lsem) = refs[:n], refs[n:2 * n], refs[2 * n:]
        x, y, c, chips = _place()
        me, sib = 2 * x + y, (x, y, 1 - c)

        def copy(a, k, src, dst, to):
            return pltpu.make_async_remote_copy(src_ref=src, dst_ref=dst, send_sem=ssem.at[6 * a + k],
                                                recv_sem=rsem.at[6 * a + k], device_id=to, device_id_type=_MESH)

        sends = []
        for a in range(n):
            loc = pltpu.make_async_copy(ins[a], outs[a].at[me], lsem.at[a])
            loc.start()
            sends.append(loc)
        for j, (px, py) in enumerate(chips):
            for a in range(n):
                cp = copy(a, j, ins[a].at[c], outs[a].at[me, c], (px, py, c))
                cp.start()
                sends.append(cp)
        for j, (px, py) in enumerate(chips):
            for a in range(n):
                landed = outs[a].at[2 * px + py, c]
                copy(a, j, landed, landed, (px, py, c)).wait_recv()
                fwd = copy(a, 3 + j, landed, landed, sib)
                fwd.start()
                sends.append(fwd)
        for j, (px, py) in enumerate(chips):
            for a in range(n):
                passed = outs[a].at[2 * px + py, 1 - c]
                copy(a, 3 + j, passed, passed, sib).wait_recv()
        for cp in sends[:n]:
            cp.wait()
        for cp in sends[n:]:
            cp.wait_send()

    assert all(a.ndim == 3 and a.shape[0] == 2 for a in arrs)
    return pl.pallas_call(
        body, name=name,
        in_specs=[_ANY] * n, out_specs=[_ANY] * n,
        out_shape=[jax.ShapeDtypeStruct((4,) + a.shape, a.dtype) for a in arrs],
        scratch_shapes=[pltpu.SemaphoreType.DMA((6 * n,)), pltpu.SemaphoreType.DMA((6 * n,)),
                        pltpu.SemaphoreType.DMA((n,))],
    )(*arrs)


def _pair_split(arrs, name):
    n = len(arrs)

    def body(*refs):
        ins, outs, (ssem, rsem, _) = refs[:n], refs[n:2 * n], refs[2 * n:]
        x, y, c, _ = _place()
        copies = []
        for a in range(n):
            for s in range(4):
                cp = pltpu.make_async_remote_copy(src_ref=ins[a].at[s, 1 - c], dst_ref=outs[a].at[s],
                                                  send_sem=ssem.at[4 * a + s], recv_sem=rsem.at[4 * a + s],
                                                  device_id=(x, y, 1 - c), device_id_type=_MESH)
                cp.start()
                copies.append(cp)
        for cp in copies:
            cp.wait()

    shapes = [jax.ShapeDtypeStruct((4,) + a.shape[2:], a.dtype) for a in arrs]
    return _comm_call(body, arrs, shapes, 4, name)


def _pair_join(arrs, name):
    n = len(arrs)

    def body(*refs):
        ins, outs, (ssem, rsem, lsem) = refs[:n], refs[n:2 * n], refs[2 * n:]
        x, y, c, _ = _place()
        copies = []
        for a in range(n):
            loc = pltpu.make_async_copy(ins[a], outs[a].at[c], lsem.at[a])
            cp = pltpu.make_async_remote_copy(src_ref=ins[a], dst_ref=outs[a].at[c], send_sem=ssem.at[a],
                                              recv_sem=rsem.at[a], device_id=(x, y, 1 - c), device_id_type=_MESH)
            loc.start()
            cp.start()
            copies += [loc, cp]
        for cp in copies:
            cp.wait()

    shapes = [jax.ShapeDtypeStruct((2,) + a.shape, a.dtype) for a in arrs]
    return _comm_call(body, arrs, shapes, 1, name)


def _pair_sum(full, recv, c_idx, name, rows=128):
    _, _, half, C = full.shape
    rows = min(rows, half)
    nb = half // rows
    assert half % rows == 0

    def body(c_ref, a_ref, b_ref, o_ref):
        o_ref[...] = (a_ref[...].astype(F32) + b_ref[...].astype(F32)).astype(o_ref.dtype)

    return pl.pallas_call(
        body, name=name,
        grid_spec=pltpu.PrefetchScalarGridSpec(
            num_scalar_prefetch=1, grid=(4, nb),
            in_specs=[pl.BlockSpec((None, None, rows, C), lambda s, i, c: (s, c[0], i, 0)),
                      pl.BlockSpec((None, rows, C), lambda s, i, c: (s, i, 0))],
            out_specs=pl.BlockSpec((None, rows, C), lambda s, i, c: (s, i, 0))),
        out_shape=jax.ShapeDtypeStruct(recv.shape, recv.dtype),
        compiler_params=_cparams(("parallel", "parallel")),
    )(c_idx, full, recv)


def _rs_xy(arrs, name):
    n = len(arrs)

    def body(*refs):
        ins, outs, (ssem, rsem, lsem) = refs[:n], refs[n:2 * n], refs[2 * n:]
        x, y, c, chips = _place()
        me = 2 * x + y
        copies = []
        for a in range(n):
            loc = pltpu.make_async_copy(ins[a].at[me], outs[a].at[3], lsem.at[a])
            loc.start()
            copies.append(loc)
            for j, (px, py) in enumerate(chips):
                cp = pltpu.make_async_remote_copy(src_ref=ins[a].at[2 * px + py], dst_ref=outs[a].at[j],
                                                  send_sem=ssem.at[3 * a + j], recv_sem=rsem.at[3 * a + j],
                                                  device_id=(px, py, c), device_id_type=_MESH)
                cp.start()
                copies.append(cp)
        for cp in copies:
            cp.wait()

    return _comm_call(body, arrs, [jax.ShapeDtypeStruct(a.shape, a.dtype) for a in arrs], 3, name)


def _gather8(buf, name):
    def body(in_ref, out_ref, ssem, rsem, lsem):
        x, y, c, _ = _place()
        me = 4 * x + 2 * y + c
        loc = pltpu.make_async_copy(in_ref, out_ref.at[me], lsem.at[0])
        loc.start()
        copies = [loc]
        for d in range(1, 8):
            dx, dy, dc = (d >> 2) & 1, (d >> 1) & 1, d & 1
            peer = (1 - x if dx else x, 1 - y if dy else y, 1 - c if dc else c)
            cp = pltpu.make_async_remote_copy(src_ref=in_ref, dst_ref=out_ref.at[me], send_sem=ssem.at[d - 1],
                                              recv_sem=rsem.at[d - 1], device_id=peer, device_id_type=_MESH)
            cp.start()
            copies.append(cp)
        for cp in copies:
            cp.wait()

    return _comm_call(body, [buf], [jax.ShapeDtypeStruct((8,) + buf.shape, buf.dtype)], 7, name)[0]


def _sum_slots(arr, name, rows=128):
    k, R, C = arr.shape
    rows = min(rows, R)
    assert R % rows == 0

    def body(in_ref, o_ref):
        acc = in_ref[0].astype(F32)
        for s in range(1, k):
            acc = acc + in_ref[s].astype(F32)
        o_ref[...] = acc

    return pl.pallas_call(
        body, name=name, grid=(R // rows,),
        in_specs=[pl.BlockSpec((k, rows, C), lambda i: (0, i, 0))],
        out_specs=pl.BlockSpec((rows, C), lambda i: (i, 0)),
        out_shape=jax.ShapeDtypeStruct((R, C), F32),
        compiler_params=_cparams(("parallel",)),
    )(arr)


ADAM_LR, ADAM_B1, ADAM_B2, ADAM_EPS, ADAM_WD, ADAM_STEP = 0.001, 0.9, 0.999, 1e-08, 0.01, 10


def _adam_math(w, g, m, v):
    m = ADAM_B1 * m + (1.0 - ADAM_B1) * g
    v = ADAM_B2 * v + (1.0 - ADAM_B2) * jnp.square(g)
    m_hat = m / (1.0 - ADAM_B1 ** ADAM_STEP)
    v_hat = v / (1.0 - ADAM_B2 ** ADAM_STEP)
    delta = -ADAM_LR * (m_hat / (jnp.sqrt(v_hat) + ADAM_EPS) + ADAM_WD * w)
    return delta, m, v


def _adamw(w, m, v, gparts, name, g_off=0, rows=128):
    R, C = w.shape
    rows = min(rows, R)
    assert R % rows == 0 and g_off % rows == 0
    ng = len(gparts)

    def body(w_ref, m_ref, v_ref, *rest):
        g_refs, (go, do, mo, vo) = rest[:ng], rest[ng:]
        g = g_refs[0][...]
        for r in g_refs[1:]:
            g = g + r[...]
        delta, m1, v1 = _adam_math(w_ref[...], g, m_ref[...], v_ref[...])
        go[...] = g
        do[...] = delta
        mo[...] = m1
        vo[...] = v1

    blk = pl.BlockSpec((rows, C), lambda i: (i, 0))
    return pl.pallas_call(
        body, name=name, grid=(R // rows,),
        in_specs=[blk] * 3 + [pl.BlockSpec((rows, C), lambda i: (g_off // rows + i, 0))] * ng,
        out_specs=[blk] * 4, out_shape=[jax.ShapeDtypeStruct((R, C), F32)] * 4,
        compiler_params=_cparams(("parallel",)),
    )(w, m, v, *gparts)


_WEIGHTS = ['norm_mix', 'norm_mlp', 'norm_ple', 'w_in_e', 'w_out_e', 'hgrn_lb', 'g_norm_a', 'conv_w', 'a_log',
            'dt_bias', 'g_norm_b', 's5_a_re', 's5_a_im', 's5_b_re', 's5_b_im', 's5_c_re', 's5_c_im', 's5_d',
            's5_log_dt', 'w_glu', 'b_glu', 'w_out_o', 'w_up', 'w_down', 'w_ple_gate', 'w_ple_proj', 'final_norm']
_INPUTS = ['x', 'p'] + _WEIGHTS + ['loss_target'] + ['m_' + n for n in _WEIGHTS] + ['v_' + n for n in _WEIGHTS]
_FAMILY = [('w_up', 'col', 2048), ('w_down', 'row', 2048), ('w_ple_gate', 'row', 512),
           ('w_out_e', 'row', 512), ('w_glu', 'row', 512), ('w_out_o', 'row', 512)]
_IN_PAD = 8320
_IN_BLK = 640


def _rms_bwd_fn(x, d_hn, d_res, g):
    _, vjp = jax.vjp(_rms, x, g)
    dx, dg = vjp(d_hn.astype(F32))
    return dx + d_res, dg


def _add_res(acc, h):
    return (acc + h,)


def _pack_rows(parts, lanes=128, mult=256):
    flat = jnp.concatenate([q.reshape(-1).astype(F32) for q in parts])
    n = flat.shape[0]
    rows = -(-n // (lanes * mult)) * mult
    return jnp.pad(flat, (0, rows * lanes - n)).reshape(rows, lanes)


def _unpack_rows(buf, shapes):
    flat, out, off = buf.reshape(-1), [], 0
    for s in shapes:
        n = math.prod(s)
        out.append(flat[off:off + n].reshape(s))
        off += n
    return out


def _step(a):
    S = a['x'].shape[1]
    x, tgt = a['x'][0], a['loss_target'][0]
    xi, yi = lax.axis_index("x"), lax.axis_index("y")
    me = 2 * xi + yi
    row = lambda t: t.reshape(1, -1)

    views, packs, off = {}, [], 0
    for nm, kind, r in _FAMILY:
        for l in range(a[nm].shape[0]):
            views[nm, l] = (kind, off, r)
            packs.append(a[nm][l].astype(BF16))
            off += r
    rt = off
    small_sh = jnp.concatenate([a['conv_w'][0].reshape(-1), a['s5_d'][0], a['b_glu'][0]]).reshape(16, 256)
    shards = [jnp.concatenate(packs, axis=0), a['w_in_e'][0].astype(BF16),
              a['w_ple_proj'].reshape(512, 512).astype(BF16), small_sh]
    gathered = _ag_xy([t.reshape(2, t.shape[0] // 2, t.shape[1]) for t in shards], name="gather_weights")
    gp, g_in, g_pp, g_small = [g.reshape((4,) + t.shape) for g, t in zip(gathered, shards)]
    win = jnp.pad(g_in.transpose(1, 0, 2).reshape(D_MODEL, 8208), ((0, 0), (0, _IN_PAD - 8208)))
    w_pp = g_pp.reshape(4, 2, 256, 512).transpose(1, 2, 0, 3).reshape(2, 256, D_MODEL)
    g_small = g_small.reshape(4, 4096)
    conv_full = g_small[:, :3072].reshape(4, 4, 768).transpose(1, 0, 2).reshape(4, 3072)
    s5_d_full = g_small[:, 3072:3584].reshape(1, D_MODEL)
    b_glu_full = g_small[:, 3584:].reshape(1, D_MODEL)

    lb, lb_vjp = jax.vjp(lambda hl: jax.nn.softmax(hl, axis=0)[0:1], a['hgrn_lb'])
    pad_h = lambda t: jnp.pad(t, ((0, 0), (0, HEAD - t.shape[1])))
    (arow, dtrow), gdn_vjp = jax.vjp(lambda al, dt: (pad_h(-jnp.exp(al)), pad_h(dt)), a['a_log'], a['dt_bias'])
    s5p, s5_vjp = jax.vjp(_s5_params, a['s5_a_re'][0], a['s5_a_im'][0], a['s5_log_dt'][0], a['s5_b_re'][0],
                          a['s5_b_im'][0], a['s5_c_re'][0], a['s5_c_im'][0])
    bre, bim, cre, cim = [t.astype(BF16) for t in s5p[:4]]
    pw_r, pw_i = _s5_powers(s5p[4], s5p[5])
    gna, gnb = a['g_norm_a'], a['g_norm_b']

    def norm_cast(h, g, dt, name):
        return _rowwise(_rms, [h], [row(g)], [(D_MODEL, dt)], name=name)

    def mlp_ple_fwd(h, l):
        hn = norm_cast(h, a['norm_mlp'][l], BF16, f"rms_mlp{l}")
        up, act = _mm(hn, gp, b_view=views['w_up', l], out_dtypes=(BF16, BF16), name=f"up{l}",
                      epilogue=lambda acc: (acc, jnp.square(jnp.maximum(acc, 0.0))))
        h2 = _mm(act, gp, b_view=views['w_down', l], epilogue=_add_res, extras=(h,), name=f"down{l}")
        hnp = norm_cast(h2, a['norm_ple'][l], BF16, f"rms_ple{l}")
        pp = _mm(a['p'][l, 0], w_pp[l], name=f"ple_proj{l}")
        h3, gpre = _mm(hnp, gp, b_view=views['w_ple_gate', l], extras=(h2, pp), out_dtypes=(F32, F32), name=f"ple_gate{l}",
                       bm=512, epilogue=lambda acc, h2, pp: (h2 + jax.nn.sigmoid(acc) * pp, acc))
        return h3, (h, hn, up, act, h2, hnp, pp, gpre)

    hn0 = norm_cast(x, a['norm_mix'][0], BF16, "rms_mix0")
    proj = _mm(hn0, win, bn=_IN_BLK, name="in_proj")
    o_a, st_a = _hgrn_fwd(proj, lb, gna)
    act_b = _conv_fwd(proj, conv_full)
    o_b, st_b = _gdn_fwd(act_b, proj, arow, dtrow, gnb)
    merged = jnp.concatenate([o_a, o_b], axis=1)
    h1 = _mm(merged, gp, b_view=views['w_out_e', 0], epilogue=_add_res, extras=(x,), name="out_e")
    h3, saved0 = mlp_ple_fwd(h1, 0)

    u = norm_cast(h3, a['norm_mix'][1], F32, "rms_mix1")
    y, xre, xim = _s5_fwd(u, bre, bim, cre, cim, pw_r, pw_i, s5_d_full)
    actg = _rowwise(jax.nn.gelu, [y], [], [(D_MODEL, BF16)], name="gelu")
    glu, z = _mm(actg, gp, b_view=views['w_glu', 0], extras=(actg, b_glu_full), out_dtypes=(BF16, F32), name="glu",
                 epilogue=lambda acc, act, b: (act.astype(F32) * jax.nn.sigmoid(acc + b), acc + b))
    h4 = _mm(glu, gp, b_view=views['w_out_o', 0], epilogue=_add_res, extras=(h3,), name="out_o")
    h6, saved1 = mlp_ple_fwd(h4, 1)

    def head(h, t, g):
        def f(h, g):
            e = _rms(h, g) - t
            return 0.5 * jnp.sum(jnp.mean(e * e, axis=-1))
        val, vjp = jax.vjp(f, h, g)
        dh, dg = vjp(jnp.ones((), F32))
        return dh, dg, jnp.full((1, HEAD), val)

    dh, d_final, loss_part = _rowwise(head, [h6, tgt], [row(a['final_norm'])], [(D_MODEL, F32)], sums=(D_MODEL, HEAD),
                                      name="loss_head")
    loss = lax.psum(loss_part[0, 0], ("x", "y", "c"))

    gbuf = lax.empty((4, rt, PACK_COLS), BF16)

    def into(gbuf, lhs, rhs, key, name):
        return _mm(lhs, rhs, dims="tn", out_into=gbuf, out_view=views[key], out_dtypes=(BF16,), name=name)

    def mlp_ple_bwd(dh, l, saved, gbuf):
        h, hn, up, act, h2, hnp, pp, gpre = saved

        def gate_bwd(d, gpre, pp):
            s = jax.nn.sigmoid(gpre)
            return d * s, d * pp * s * (1.0 - s)

        d_pp, d_gp = _rowwise(gate_bwd, [dh, gpre, pp], [], [(D_MODEL, BF16), (D_MODEL, BF16)], name=f"ple_bwd{l}")
        d_wpp = _mm(a['p'][l, 0], d_pp, dims="tn", name=f"d_ple_proj{l}")
        gbuf = into(gbuf, hnp, d_gp, ('w_ple_gate', l), f"d_ple_gate{l}")
        d_hnp = _mm(d_gp, gp, dims="nt", b_view=views['w_ple_gate', l], name=f"ple_gate_t{l}")
        dh2, d_nple = _rowwise(_rms_bwd_fn, [h2, d_hnp, dh], [row(a['norm_ple'][l])], [(D_MODEL, F32)], sums=(D_MODEL,),
                               name=f"rms_ple_bwd{l}")
        d_up = _mm(dh2, gp, dims="nt", b_view=views['w_down', l], extras=(up,), out_dtypes=(BF16,), name=f"down_t{l}",
                   epilogue=lambda acc, up: (acc * 2.0 * jnp.maximum(up.astype(F32), 0.0),))
        gbuf = into(gbuf, act, dh2, ('w_down', l), f"d_down{l}")
        gbuf = into(gbuf, hn, d_up, ('w_up', l), f"d_up{l}")
        d_hn = _mm(d_up, gp, dims="nt", b_view=views['w_up', l], name=f"up_t{l}")
        dh1, d_nmlp = _rowwise(_rms_bwd_fn, [h, d_hn, dh2], [row(a['norm_mlp'][l])], [(D_MODEL, F32)], sums=(D_MODEL,),
                               name=f"rms_mlp_bwd{l}")
        return dh1, gbuf, d_wpp, d_nple, d_nmlp

    dh4, gbuf, d_wpp1, d_nple1, d_nmlp1 = mlp_ple_bwd(dh, 1, saved1, gbuf)

    d_glu = _mm(dh4, gp, dims="nt", b_view=views['w_out_o', 0], name="out_o_t")
    gbuf = into(gbuf, glu, dh4, ('w_out_o', 0), "d_out_o")

    def glu_bwd(d, z, act):
        s, act = jax.nn.sigmoid(z), act.astype(F32)
        dz = d * act * s * (1.0 - s)
        return dz, d * s, jnp.sum(dz, axis=0, keepdims=True)

    d_z, d_actp, d_bglu = _rowwise(glu_bwd, [d_glu, z, actg], [], [(D_MODEL, BF16), (D_MODEL, F32)], sums=(D_MODEL,),
                                   name="glu_bwd")
    gbuf = into(gbuf, actg, d_z, ('w_glu', 0), "d_glu")

    def gelu_bwd(acc, dap, y):
        _, vjp = jax.vjp(jax.nn.gelu, y)
        return vjp(acc + dap)

    dy = _mm(d_z, gp, dims="nt", b_view=views['w_glu', 0], extras=(d_actp, y), epilogue=gelu_bwd, name="glu_t")
    du, d_bre, d_bim, d_cre, d_cim, d_lr, d_li, d_s5d = _s5_bwd(dy, u, xre, xim, bre, bim, cre, cim,
                                                                 pw_r[:, ::-1], -pw_i[:, ::-1], s5_d_full)
    dh3, d_nmix1 = _rowwise(_rms_bwd_fn, [h3, du, dh4], [row(a['norm_mix'][1])], [(D_MODEL, F32)], sums=(D_MODEL,),
                            name="rms_mix1_bwd")
    d_are, d_aim, d_logdt, d_sbre, d_sbim, d_scre, d_scim = s5_vjp(
        (d_bre, d_bim, d_cre, d_cim, d_lr.sum(axis=1), d_li.sum(axis=1)))

    dh1, gbuf, d_wpp0, d_nple0, d_nmlp0 = mlp_ple_bwd(dh3, 0, saved0, gbuf)

    d_merged = _mm(dh1, gp, dims="nt", b_view=views['w_out_e', 0], name="out_e_t")
    gbuf = into(gbuf, merged, dh1, ('w_out_e', 0), "d_out_e")
    dq, df, dv, dg, d_lb, d_gna = _hgrn_bwd(proj, lb, gna, st_a, d_merged)
    dqb, dkb, dvb, dzb, d_tail, d_arow, d_dtrow, d_gnb = _gdn_bwd(act_b, proj, arow, dtrow, gnb, st_b, d_merged)
    d_pre, d_conv = _conv_bwd(proj, conv_full, dqb, dkb, dvb)
    d_proj = jnp.concatenate([dq, df, dv, dg, d_pre, dzb, d_tail], axis=1)
    d_win = _mm(hn0, d_proj, dims="tn", bn=_IN_BLK, out_dtypes=(BF16,), name="d_in_proj")
    d_hn0 = _mm(d_proj, win, dims="nt", bk=_IN_BLK, name="in_proj_t")
    grad_x, d_nmix0 = _rowwise(_rms_bwd_fn, [x, d_hn0, dh1], [row(a['norm_mix'][0])], [(D_MODEL, F32)], sums=(D_MODEL,),
                               name="rms_mix0_bwd")
    (d_hlb,) = lb_vjp(d_lb.reshape(1, 1024))
    d_alog, d_dtb = gdn_vjp((d_arow, d_dtrow))

    d_win_sh = d_win[:, :8208].reshape(D_MODEL, 4, 2052).transpose(1, 0, 2)
    d_wpp_sh = jnp.stack([d_wpp0, d_wpp1]).astype(BF16).reshape(2, 256, 4, 512).transpose(2, 0, 1, 3).reshape(4, 512, 512)
    partials = [t.reshape(4, 2, t.shape[1] // 2, t.shape[2]) for t in (gbuf, d_win_sh, d_wpp_sh)]
    c_idx = lax.axis_index("c").astype(jnp.int32).reshape(1)
    chip_sums = [_pair_sum(f, r, c_idx, name=f"pair_sum{i}")
                 for i, (f, r) in enumerate(zip(partials, _pair_split(partials, name="pair_split")))]
    halves = [_sum_slots(t, name=f"sum_partials{i}") for i, t in enumerate(_rs_xy(chip_sums, name="scatter_grads"))]
    grads = [g.reshape(2 * g.shape[1], g.shape[2]) for g in _pair_join(halves, name="pair_join")]

    small = {
        'norm_mix': jnp.concatenate([d_nmix0, d_nmix1]), 'norm_mlp': jnp.concatenate([d_nmlp0, d_nmlp1]),
        'norm_ple': jnp.concatenate([d_nple0, d_nple1]), 'hgrn_lb': d_hlb, 'g_norm_a': d_gna, 'conv_w': d_conv,
        'a_log': d_alog, 'dt_bias': d_dtb, 'g_norm_b': d_gnb, 's5_a_re': d_are, 's5_a_im': d_aim, 's5_b_re': d_sbre,
        's5_b_im': d_sbim, 's5_c_re': d_scre, 's5_c_im': d_scim, 's5_d': d_s5d, 's5_log_dt': d_logdt,
        'b_glu': d_bglu, 'final_norm': d_final}
    s_names = list(small)
    s_shapes = [tuple(small[n].shape) for n in s_names]
    reduced = _sum_slots(_gather8(_pack_rows([small[n] for n in s_names]), name="gather_small"), name="sum_small")
    sg = dict(zip(s_names, _unpack_rows(reduced, s_shapes)))
    sg['conv_w'] = lax.dynamic_slice_in_dim(sg['conv_w'], me * 768, 768, axis=1)
    sg['s5_d'] = lax.dynamic_slice_in_dim(sg['s5_d'], me * 512, 512, axis=1)
    sg['b_glu'] = lax.dynamic_slice_in_dim(sg['b_glu'], me * 512, 512, axis=1)
    sg = {n: sg[n].reshape(a[n].shape) for n in s_names}
    w_pack, m_pack, v_pack, g_pack = [_pack_rows([src[pre + n] for n in s_names])
                                      for src, pre in ((a, ''), (a, 'm_'), (a, 'v_'), (sg, ''))]
    sd, sm, sv = _rowwise(_adam_math, [w_pack, g_pack, m_pack, v_pack], [], [(128, F32)] * 3, name="adam_small")
    w_shapes = [tuple(a[n].shape) for n in s_names]
    res = {n: (sg[n], d_, m_, v_) for n, d_, m_, v_ in zip(s_names, _unpack_rows(sd, w_shapes), _unpack_rows(sm, w_shapes),
                                                          _unpack_rows(sv, w_shapes))}

    def adam_big(nm, gparts, g_off, shape2d):
        outs = _adamw(a[nm].reshape(shape2d), a['m_' + nm].reshape(shape2d), a['v_' + nm].reshape(shape2d), gparts,
                      name="adam_" + nm, g_off=g_off)
        res[nm] = tuple(o.reshape(a[nm].shape) for o in outs)

    for nm, _, r in _FAMILY:
        adam_big(nm, [grads[0]], views[nm, 0][1], (a[nm].shape[0] * r, PACK_COLS))
    adam_big('w_in_e', [grads[1]], 0, (D_MODEL, 2052))
    adam_big('w_ple_proj', [grads[2]], 0, (512, 512))

    return (loss, grad_x[None], *[res[n][0] for n in _WEIGHTS], *[res[n][1] for n in _WEIGHTS],
            *[res[n][2] for n in _WEIGHTS], *[res[n][3] for n in _WEIGHTS])


def kernel(x, p, norm_mix, norm_mlp, norm_ple, w_in_e, w_out_e, hgrn_lb, g_norm_a, conv_w, a_log, dt_bias, g_norm_b, s5_a_re, s5_a_im, s5_b_re, s5_b_im, s5_c_re, s5_c_im, s5_d, s5_log_dt, w_glu, b_glu, w_out_o, w_up, w_down, w_ple_gate, w_ple_proj, final_norm, loss_target, m_norm_mix, m_norm_mlp, m_norm_ple, m_w_in_e, m_w_out_e, m_hgrn_lb, m_g_norm_a, m_conv_w, m_a_log, m_dt_bias, m_g_norm_b, m_s5_a_re, m_s5_a_im, m_s5_b_re, m_s5_b_im, m_s5_c_re, m_s5_c_im, m_s5_d, m_s5_log_dt, m_w_glu, m_b_glu, m_w_out_o, m_w_up, m_w_down, m_w_ple_gate, m_w_ple_proj, m_final_norm, v_norm_mix, v_norm_mlp, v_norm_ple, v_w_in_e, v_w_out_e, v_hgrn_lb, v_g_norm_a, v_conv_w, v_a_log, v_dt_bias, v_g_norm_b, v_s5_a_re, v_s5_a_im, v_s5_b_re, v_s5_b_im, v_s5_c_re, v_s5_c_im, v_s5_d, v_s5_log_dt, v_w_glu, v_b_glu, v_w_out_o, v_w_up, v_w_down, v_w_ple_gate, v_w_ple_proj, v_final_norm):
    args = (x, p, norm_mix, norm_mlp, norm_ple, w_in_e, w_out_e, hgrn_lb, g_norm_a, conv_w, a_log, dt_bias, g_norm_b, s5_a_re, s5_a_im, s5_b_re, s5_b_im, s5_c_re, s5_c_im, s5_d, s5_log_dt, w_glu, b_glu, w_out_o, w_up, w_down, w_ple_gate, w_ple_proj, final_norm, loss_target, m_norm_mix, m_norm_mlp, m_norm_ple, m_w_in_e, m_w_out_e, m_hgrn_lb, m_g_norm_a, m_conv_w, m_a_log, m_dt_bias, m_g_norm_b, m_s5_a_re, m_s5_a_im, m_s5_b_re, m_s5_b_im, m_s5_c_re, m_s5_c_im, m_s5_d, m_s5_log_dt, m_w_glu, m_b_glu, m_w_out_o, m_w_up, m_w_down, m_w_ple_gate, m_w_ple_proj, m_final_norm, v_norm_mix, v_norm_mlp, v_norm_ple, v_w_in_e, v_w_out_e, v_hgrn_lb, v_g_norm_a, v_conv_w, v_a_log, v_dt_bias, v_g_norm_b, v_s5_a_re, v_s5_a_im, v_s5_b_re, v_s5_b_im, v_s5_c_re, v_s5_c_im, v_s5_d, v_s5_log_dt, v_w_glu, v_b_glu, v_w_out_o, v_w_up, v_w_down, v_w_ple_gate, v_w_ple_proj, v_final_norm)
    return _step(dict(zip(_INPUTS, args)))
```

```python
import functools
import math

import jax
import jax.numpy as jnp
from jax import lax
from jax.experimental import pallas as pl
from jax.experimental.pallas import tpu as pltpu

F32 = jnp.float32
BF16 = jnp.bfloat16

D_MODEL = 2048
SEQ = 4096
NORM_EPS = 1e-6
CHUNK = 64
HEAD = 128
VMEM_LIMIT = 56 * 1024 * 1024


_NN = (((1,), (0,)), ((), ()))
_NT = (((1,), (1,)), ((), ()))
_TN = (((0,), (0,)), ((), ()))
_HI = lax.Precision.HIGHEST
_NEG = -1e30


def _cparams(sem, **kw):
    return pltpu.CompilerParams(dimension_semantics=sem, vmem_limit_bytes=VMEM_LIMIT, **kw)


PACK_COLS = 2048


def _view_shape(view):
    kind, _, r = view
    return (4 * r, PACK_COLS) if kind == "row" else (r, 4 * PACK_COLS)


def _view_spec(view, rb, cb, row_of, col_of):
    kind, off, r = view
    assert off % rb == 0 and r % rb == 0 and PACK_COLS % cb == 0, (view, rb, cb)
    if kind == "row":
        nrb = r // rb
        return pl.BlockSpec((None, rb, cb), lambda i, j, k: (row_of(i, j, k) // nrb,
                                                             off // rb + row_of(i, j, k) % nrb, col_of(i, j, k)))
    ncb = PACK_COLS // cb
    return pl.BlockSpec((None, rb, cb), lambda i, j, k: (col_of(i, j, k) // ncb,
                                                         off // rb + row_of(i, j, k), col_of(i, j, k) % ncb))


def _mm(a, b, *, dims="nn", epilogue=None, extras=(), out_dtypes=(F32,), bm=1024, bn=1024, bk=1024, name,
        b_view=None, out_into=None, out_view=None):
    b_shape = _view_shape(b_view) if b_view is not None else b.shape
    if dims == "tn":
        (K, M), (K2, N) = a.shape, b_shape
    elif dims == "nt":
        (M, K), (N, K2) = a.shape, b_shape
    else:
        (M, K), (K2, N) = a.shape, b_shape
    assert K == K2, (a.shape, b_shape, dims)
    if b_view is not None and dims == "nt":
        bn = min(bn, b_view[2])
    if b_view is not None and dims != "nt":
        bk = min(bk, b_view[2])
    if out_view is not None:
        bm = min(bm, out_view[2])
    bm, bn, bk = min(bm, M), min(bn, N), min(bk, K)
    assert M % bm == 0 and N % bn == 0 and K % bk == 0, (M, N, K, bm, bn, bk)
    nk = K // bk
    ii, jj, kk = (lambda i, j, k: i), (lambda i, j, k: j), (lambda i, j, k: k)
    if dims == "tn":
        a_spec = pl.BlockSpec((bk, bm), lambda i, j, k: (k, i))
        dn = _TN
    else:
        a_spec = pl.BlockSpec((bm, bk), lambda i, j, k: (i, k))
        dn = _NT if dims == "nt" else _NN
    if dims == "nt":
        b_spec = _view_spec(b_view, bn, bk, jj, kk) if b_view else pl.BlockSpec((bn, bk), lambda i, j, k: (j, k))
    else:
        b_spec = _view_spec(b_view, bk, bn, kk, jj) if b_view else pl.BlockSpec((bk, bn), lambda i, j, k: (k, j))
    e_specs = []
    for e in extras:
        if e.shape == (M, N):
            e_specs.append(pl.BlockSpec((bm, bn), lambda i, j, k: (i, j)))
        else:
            assert e.shape == (1, N), e.shape
            e_specs.append(pl.BlockSpec((1, bn), lambda i, j, k: (0, j)))
    ne, no = len(extras), len(out_dtypes)
    if epilogue is None:
        epilogue = lambda acc: (acc,)
    into = out_into is not None

    def body(a_ref, b_ref, *rest):
        e_refs, rest = rest[:ne], rest[ne + (1 if into else 0):]
        o_refs, acc = rest[:no], rest[no]
        k = pl.program_id(2)

        @pl.when(k == 0)
        def _():
            acc[...] = jnp.zeros_like(acc)

        acc[...] += lax.dot_general(a_ref[...].astype(BF16), b_ref[...].astype(BF16), dn,
                                    preferred_element_type=F32)

        @pl.when(k == nk - 1)
        def _():
            outs = epilogue(acc[...], *[e[...] for e in e_refs])
            for o, v in zip(o_refs, outs):
                o[...] = v.astype(o.dtype)

    if into:
        assert no == 1 and _view_shape(out_view) == (M, N), (out_view, M, N)
        out_specs = [_view_spec(out_view, bm, bn, ii, jj)]
        out_shape = [jax.ShapeDtypeStruct(out_into.shape, out_into.dtype)]
        extra_in, extra_specs, alias = [out_into], [pl.BlockSpec(memory_space=pl.ANY)], {2 + ne: 0}
    else:
        out_specs = [pl.BlockSpec((bm, bn), lambda i, j, k: (i, j)) for _ in out_dtypes]
        out_shape = [jax.ShapeDtypeStruct((M, N), dt) for dt in out_dtypes]
        extra_in, extra_specs, alias = [], [], {}
    outs = pl.pallas_call(
        body, name=name,
        grid=(M // bm, N // bn, nk),
        in_specs=[a_spec, b_spec] + e_specs + extra_specs,
        out_specs=out_specs, out_shape=out_shape,
        scratch_shapes=[pltpu.VMEM((bm, bn), F32)],
        input_output_aliases=alias,
        compiler_params=_cparams(("parallel", "parallel", "arbitrary")),
    )(a, b, *extras, *extra_in)
    return outs[0] if no == 1 else tuple(outs)


def _rowwise(fn, tiled, bcast, outs, sums=(), *, rows=256, name):
    S = tiled[0].shape[0]
    rows = min(rows, S)
    assert S % rows == 0
    nt, nb, no, ns = len(tiled), len(bcast), len(outs), len(sums)

    def body(*refs):
        t_refs, b_refs = refs[:nt], refs[nt:nt + nb]
        o_refs, s_refs = refs[nt + nb:nt + nb + no], refs[nt + nb + no:]
        res = fn(*[r[...] for r in t_refs], *[r[...] for r in b_refs])
        res = res if isinstance(res, (tuple, list)) else (res,)
        for o, v in zip(o_refs, res[:no]):
            o[...] = v.astype(o.dtype)
        if ns:
            @pl.when(pl.program_id(0) == 0)
            def _():
                for s in s_refs:
                    s[...] = jnp.zeros_like(s)
            for s, v in zip(s_refs, res[no:]):
                s[...] += v

    res = pl.pallas_call(
        body, name=name,
        grid=(S // rows,),
        in_specs=[pl.BlockSpec((rows, t.shape[1]), lambda i: (i, 0)) for t in tiled]
        + [pl.BlockSpec(b.shape, lambda i, nd=b.ndim: (0,) * nd) for b in bcast],
        out_specs=[pl.BlockSpec((rows, c), lambda i: (i, 0)) for c, _ in outs]
        + [pl.BlockSpec((1, c), lambda i: (0, 0)) for c in sums],
        out_shape=[jax.ShapeDtypeStruct((S, c), dt) for c, dt in outs]
        + [jax.ShapeDtypeStruct((1, c), F32) for c in sums],
        compiler_params=_cparams(("arbitrary",)),
    )(*tiled, *bcast)
    return res[0] if len(res) == 1 else tuple(res)


def _rms(x, g):
    return x * lax.rsqrt(jnp.mean(x * x, axis=-1, keepdims=True) + NORM_EPS) * g


def _mxu(a, b, dn):
    if a.ndim == 3:
        ((ca,), (cb,)), _ = dn
        dn = (((ca + 1,), (cb + 1,)), ((0,), (0,)))
    return lax.dot_general(a, b, dn, preferred_element_type=F32)


def _split(a):
    hi = a.astype(BF16)
    return hi, (a - hi.astype(F32)).astype(BF16)


def _passes(a, b, dn, three):
    if not three:
        return _mxu(a.astype(BF16), b.astype(BF16), dn)
    (ah, al), (bh, bl) = _split(a), _split(b)
    return _mxu(ah, bh, dn) + (_mxu(ah, bl, dn) + _mxu(al, bh, dn))


def _dot_grads(a, b, g, dn, three):
    if dn == _NN:
        return _passes(g, b, _NT, three), _passes(a, g, _TN, three)
    if dn == _NT:
        return _passes(g, b, _NN, three), _passes(g, a, _TN, three)
    assert dn == _TN
    return _passes(b, g, _NT, three), _passes(a, g, _NN, three)


@functools.partial(jax.custom_vjp, nondiff_argnums=(2,))
def _dot(a, b, dn=_NN):
    return _passes(a, b, dn, False)


_dot.defvjp(lambda a, b, dn: (_passes(a, b, dn, False), (a, b)),
            lambda dn, res, g: _dot_grads(res[0], res[1], g, dn, False))


@functools.partial(jax.custom_vjp, nondiff_argnums=(2,))
def _dot3(a, b, dn=_NN):
    return _passes(a, b, dn, True)


_dot3.defvjp(lambda a, b, dn: (_passes(a, b, dn, True), (a, b)),
             lambda dn, res, g: _dot_grads(res[0], res[1], g, dn, True))


def _tri_times(x, dn):
    tri = _tril(x.shape[-2]).astype(BF16)
    if x.ndim == 3:
        tri = jnp.broadcast_to(tri, (x.shape[0],) + tri.shape)
    hi, lo = _split(x)
    lo2 = (x - hi.astype(F32) - lo.astype(F32)).astype(BF16)
    return _mxu(tri, hi, dn) + (_mxu(tri, lo, dn) + _mxu(tri, lo2, dn))


@jax.custom_vjp
def _cumsum_rows(x):
    return _tri_times(x, _NN)


_cumsum_rows.defvjp(lambda x: (_tri_times(x, _NN), None), lambda _, g: (_tri_times(g, _TN),))


def _iota(shape, dim):
    return lax.broadcasted_iota(jnp.int32, shape, dim)


def _tril(n, strict=False):
    r, c = _iota((n, n), 0), _iota((n, n), 1)
    return (r > c) if strict else (r >= c)


@functools.partial(jax.custom_vjp, nondiff_argnums=(1,))
def _roll_rows(x, r):
    return pltpu.roll(x, r, 0)


def _roll_rows_fwd(x, r):
    return pltpu.roll(x, r, 0), None


def _roll_rows_bwd(r, _, g):
    return (pltpu.roll(g, (g.shape[0] - r) % g.shape[0], 0),)


_roll_rows.defvjp(_roll_rows_fwd, _roll_rows_bwd)


def _head_norm_gate(o, gn, gate):
    return _rms(o, gn) * jax.nn.silu(gate)


_SUB = 16
_HPS = 8


def _hgrn_chunk(q, fpre, v, gate, lb, gna, st):
    c = q.shape[0]
    forget = lb + (1.0 - lb) * jax.nn.sigmoid(fpre)
    k = 1.0 - forget
    logf = jnp.log(forget)
    cum = _cumsum_rows(logf)
    cum_end = jnp.sum(logf, axis=0, keepdims=True)
    o = _dot(q * jnp.exp(cum), st, _NT)
    st_new = st * jnp.exp(cum_end) + _dot(v, k * jnp.exp(cum_end - cum), _TN)
    t = _iota((c, 1), 0)
    s_off = jnp.zeros((c, c), F32)
    for i in range(1, c // _SUB):
        before = t < i * _SUB
        c_i = jnp.sum(jnp.where(before, logf, 0.0), axis=0, keepdims=True)
        in_blk = (t >= i * _SUB) & (t < (i + 1) * _SUB)
        qi = jnp.where(in_blk, q * jnp.exp(jnp.minimum(cum - c_i, 0.0)), 0.0)
        ki = jnp.where(before, k * jnp.exp(jnp.minimum(c_i - cum, 0.0)), 0.0)
        s_off = s_off + _dot(qi, ki, _NT)
    o = o + _dot(s_off, v)
    tmod = t % _SUB
    for r in range(_SUB):
        kr, cr, vr = (k, cum, v) if r == 0 else (_roll_rows(k, r), _roll_rows(cum, r), _roll_rows(v, r))
        w = q * kr * jnp.exp(jnp.where(tmod >= r, cum - cr, _NEG))
        o = o + jnp.sum(w, axis=1, keepdims=True) * vr
    return _head_norm_gate(o, gna, gate), st_new


def _hgrn_fwd(proj, lb, gna):
    S = proj.shape[0]
    n_chunks, heads = S // CHUNK, 8

    def body(q_ref, f_ref, v_ref, g_ref, lb_ref, gna_ref, o_ref, st_out, st):
        n, hb = pl.program_id(0), pl.program_id(1)

        @pl.when(n == 0)
        def _():
            st[hb] = jnp.zeros((_HPS, HEAD, HEAD), F32)

        sls = [slice(hh * HEAD, (hh + 1) * HEAD) for hh in range(_HPS)]
        st0 = [st[hb, hh] for hh in range(_HPS)]
        ins = [(q_ref[:, sl], f_ref[:, sl], v_ref[:, sl], g_ref[:, sl], lb_ref[:, sl]) for sl in sls]
        gna = gna_ref[...]
        res = [_hgrn_chunk(*ins[hh], gna, st0[hh]) for hh in range(_HPS)]
        for hh, sl in enumerate(sls):
            st_out[hh] = st0[hh]
            o_ref[:, sl] = res[hh][0].astype(o_ref.dtype)
            st[hb, hh] = res[hh][1]

    wide = _HPS * HEAD
    sec = lambda s: pl.BlockSpec((CHUNK, wide), lambda n, h, s=s: (n, (8 // _HPS) * s + h))
    return pl.pallas_call(
        body, name="hgrn_fwd", grid=(n_chunks, heads // _HPS),
        in_specs=[sec(0), sec(1), sec(2), sec(3),
                  pl.BlockSpec((1, wide), lambda n, h: (0, h)), pl.BlockSpec((1, HEAD), lambda n, h: (0, 0))],
        out_specs=[pl.BlockSpec((CHUNK, wide), lambda n, h: (n, h)),
                   pl.BlockSpec((None, _HPS, HEAD, HEAD), lambda n, h: (n, h, 0, 0))],
        out_shape=[jax.ShapeDtypeStruct((S, 1024), BF16),
                   jax.ShapeDtypeStruct((n_chunks, heads, HEAD, HEAD), F32)],
        scratch_shapes=[pltpu.VMEM((heads // _HPS, _HPS, HEAD, HEAD), F32)],
        compiler_params=_cparams(("arbitrary", "arbitrary")),
    )(proj, proj, proj, proj, lb, gna)


def _hgrn_bwd(proj, lb, gna, states, d_o):
    S = proj.shape[0]
    n_chunks, heads = S // CHUNK, 8

    def body(q_ref, f_ref, v_ref, g_ref, lb_ref, gna_ref, st_ref, do_ref,
             dq_ref, df_ref, dv_ref, dg_ref, dlb_ref, dgna_ref, dst):
        n, hb = pl.program_id(0), pl.program_id(1)

        @pl.when(n == 0)
        def _():
            dst[hb] = jnp.zeros((_HPS, HEAD, HEAD), F32)
            dlb_ref[hb] = jnp.zeros((_HPS, 1, HEAD), F32)

        @pl.when((n == 0) & (hb == 0))
        def _():
            dgna_ref[...] = jnp.zeros_like(dgna_ref)

        sls = [slice(hh * HEAD, (hh + 1) * HEAD) for hh in range(_HPS)]
        gna = gna_ref[...]
        ins = [(q_ref[:, sl], f_ref[:, sl], v_ref[:, sl], g_ref[:, sl], lb_ref[:, sl], gna, st_ref[hh])
               for hh, sl in enumerate(sls)]
        cts = [(do_ref[:, sl], dst[hb, hh]) for hh, sl in enumerate(sls)]
        grads = [jax.vjp(_hgrn_chunk, *ins[hh])[1](cts[hh]) for hh in range(_HPS)]
        dgna_sum = jnp.zeros((1, HEAD), F32)
        for hh, sl in enumerate(sls):
            dq, df, dv, dg, dlb, dgna, dst0 = grads[hh]
            dq_ref[:, sl] = dq.astype(dq_ref.dtype)
            df_ref[:, sl] = df.astype(df_ref.dtype)
            dv_ref[:, sl] = dv.astype(dv_ref.dtype)
            dg_ref[:, sl] = dg.astype(dg_ref.dtype)
            dlb_ref[hb, hh] += dlb
            dgna_sum = dgna_sum + dgna
            dst[hb, hh] = dst0
        dgna_ref[...] += dgna_sum

    rev = lambda n: n_chunks - 1 - n
    wide = _HPS * HEAD
    sec = lambda s: pl.BlockSpec((CHUNK, wide), lambda n, h, s=s: (rev(n), (8 // _HPS) * s + h))
    out = pl.BlockSpec((CHUNK, wide), lambda n, h: (rev(n), h))
    return pl.pallas_call(
        body, name="hgrn_bwd", grid=(n_chunks, heads // _HPS),
        in_specs=[sec(0), sec(1), sec(2), sec(3),
                  pl.BlockSpec((1, wide), lambda n, h: (0, h)), pl.BlockSpec((1, HEAD), lambda n, h: (0, 0)),
                  pl.BlockSpec((None, _HPS, HEAD, HEAD), lambda n, h: (rev(n), h, 0, 0)),
                  pl.BlockSpec((CHUNK, wide), lambda n, h: (rev(n), h))],
        out_specs=[out, out, out, out,
                   pl.BlockSpec((heads // _HPS, _HPS, 1, HEAD), lambda n, h: (0, 0, 0, 0)),
                   pl.BlockSpec((1, HEAD), lambda n, h: (0, 0))],
        out_shape=[jax.ShapeDtypeStruct((S, 1024), BF16)] * 4
        + [jax.ShapeDtypeStruct((heads // _HPS, _HPS, 1, HEAD), F32), jax.ShapeDtypeStruct((1, HEAD), F32)],
        scratch_shapes=[pltpu.VMEM((heads // _HPS, _HPS, HEAD, HEAD), F32)],
        compiler_params=_cparams(("arbitrary", "arbitrary")),
    )(proj, proj, proj, proj, lb, gna, states, d_o)


def _l2n(t):
    return t * lax.rsqrt(jnp.sum(t * t, axis=-1, keepdims=True) + NORM_EPS)


def _unit_lower_inverse(low):
    c = low.shape[-1]
    inv = (_iota((c, c), 0) == _iota((c, c), 1)).astype(F32) - low
    p = low
    span = 2
    while span < c:
        p = _dot3(p, p)
        inv = inv + _dot3(inv, p)
        span *= 2
    return inv


def _gdn_chunk(qc, kc, v, z, tail, h0, arow, dtrow, gnb, st):
    nb, c = qc.shape[0], qc.shape[1]
    head = h0 + _iota((nb, c, HEAD), 0)
    lane = _iota((nb, c, HEAD), 2)
    la_all = arow * jax.nn.softplus(tail + dtrow)
    la = jnp.sum(jnp.where(lane == head, la_all[None], 0.0), axis=2, keepdims=True)
    beta = jnp.sum(jnp.where(lane == head + 8, jax.nn.sigmoid(tail)[None], 0.0), axis=2, keepdims=True)
    la_b = jnp.broadcast_to(la, (nb, c, HEAD))
    cum = _cumsum_rows(la_b)
    cmat = _cumsum_rows(jnp.broadcast_to(la, (nb, c, c)))
    cum_end = jnp.sum(la_b, axis=1, keepdims=True)
    decay = jnp.exp(jnp.where(_tril(c), cmat - jnp.swapaxes(cmat, 1, 2), _NEG))
    q = _l2n(qc) * (HEAD ** -0.5)
    k = _l2n(kc)
    k_beta = k * beta
    low = jnp.where(_tril(c, strict=True), _dot(k_beta, k, _NT) * decay, 0.0)
    inv = _unit_lower_inverse(low)
    u, w = _dot(inv, v * beta), _dot(inv, k_beta * jnp.exp(cum))
    intra = _dot(q, k, _NT) * decay
    v_new = u - _dot(w, st, _NT)
    o = _dot(q * jnp.exp(cum), st, _NT) + _dot(intra, v_new)
    st_new = st * jnp.exp(cum_end) + _dot(v_new, k * jnp.exp(cum_end - cum), _TN)
    return _head_norm_gate(o, gnb, z), st_new


def _gdn_specs(n_of):
    wide = _HPS * HEAD
    qkv = lambda s: pl.BlockSpec((CHUNK, wide), lambda n, h, s=s: (n_of(n), (8 // _HPS) * s + h))
    return [qkv(0), qkv(1), qkv(2),
            pl.BlockSpec((CHUNK, wide), lambda n, h: (n_of(n), 56 // _HPS + h)),
            pl.BlockSpec((CHUNK, HEAD), lambda n, h: (n_of(n), 64)),
            pl.BlockSpec((1, HEAD), lambda n, h: (0, 0)), pl.BlockSpec((1, HEAD), lambda n, h: (0, 0)),
            pl.BlockSpec((1, HEAD), lambda n, h: (0, 0))]


def _gdn_fwd(act, proj, arow, dtrow, gnb):
    S = act.shape[0]
    n_chunks, heads = S // CHUNK, 8

    def body(q_ref, k_ref, v_ref, z_ref, t_ref, a_ref, dt_ref, gnb_ref, o_ref, st_out, st):
        n, hb = pl.program_id(0), pl.program_id(1)

        @pl.when(n == 0)
        def _():
            st[hb] = jnp.zeros((_HPS, HEAD, HEAD), F32)

        sls = [slice(hh * HEAD, (hh + 1) * HEAD) for hh in range(_HPS)]
        heads_of = lambda ref: jnp.stack([ref[:, sl] for sl in sls])
        st0 = st[hb]
        o, st1 = _gdn_chunk(heads_of(q_ref), heads_of(k_ref), heads_of(v_ref), heads_of(z_ref), t_ref[...], hb * _HPS,
                            a_ref[...], dt_ref[...], gnb_ref[...], st0)
        st_out[...] = st0
        st[hb] = st1
        for hh, sl in enumerate(sls):
            o_ref[:, sl] = o[hh].astype(o_ref.dtype)

    return pl.pallas_call(
        body, name="gdn_fwd", grid=(n_chunks, heads // _HPS),
        in_specs=_gdn_specs(lambda n: n),
        out_specs=[pl.BlockSpec((CHUNK, _HPS * HEAD), lambda n, h: (n, h)),
                   pl.BlockSpec((None, _HPS, HEAD, HEAD), lambda n, h: (n, h, 0, 0))],
        out_shape=[jax.ShapeDtypeStruct((S, 1024), BF16),
                   jax.ShapeDtypeStruct((n_chunks, heads, HEAD, HEAD), F32)],
        scratch_shapes=[pltpu.VMEM((heads // _HPS, _HPS, HEAD, HEAD), F32)],
        compiler_params=_cparams(("arbitrary", "arbitrary")),
    )(act, act, act, proj, proj, arow, dtrow, gnb)


def _gdn_bwd(act, proj, arow, dtrow, gnb, states, d_o):
    S = act.shape[0]
    n_chunks, heads = S // CHUNK, 8

    def body(q_ref, k_ref, v_ref, z_ref, t_ref, a_ref, dt_ref, gnb_ref, st_ref, do_ref,
             dq_ref, dk_ref, dv_ref, dz_ref, dt_out, da_ref, ddt_ref, dgnb_ref, dst, dtail_acc):
        n, hb = pl.program_id(0), pl.program_id(1)
        n_hb = heads // _HPS

        @pl.when(n == 0)
        def _():
            dst[hb] = jnp.zeros((_HPS, HEAD, HEAD), F32)

        @pl.when((n == 0) & (hb == 0))
        def _():
            da_ref[...] = jnp.zeros_like(da_ref)
            ddt_ref[...] = jnp.zeros_like(ddt_ref)
            dgnb_ref[...] = jnp.zeros_like(dgnb_ref)

        sls = [slice(hh * HEAD, (hh + 1) * HEAD) for hh in range(_HPS)]
        heads_of = lambda ref: jnp.stack([ref[:, sl] for sl in sls])
        fn = lambda qc, kc, v, z, tail, arow, dtrow, gnb, st: _gdn_chunk(qc, kc, v, z, tail, hb * _HPS, arow, dtrow,
                                                                        gnb, st)
        _, vjp = jax.vjp(fn, heads_of(q_ref), heads_of(k_ref), heads_of(v_ref), heads_of(z_ref), t_ref[...],
                         a_ref[...], dt_ref[...], gnb_ref[...], st_ref[...])
        dq, dk, dv, dz, dtail_sum, da_sum, ddt_sum, dgnb_sum, dst0 = vjp((heads_of(do_ref), dst[hb]))
        dst[hb] = dst0
        for hh, sl in enumerate(sls):
            dq_ref[:, sl] = dq[hh]
            dk_ref[:, sl] = dk[hh]
            dv_ref[:, sl] = dv[hh]
            dz_ref[:, sl] = dz[hh].astype(dz_ref.dtype)

        @pl.when(hb == 0)
        def _():
            dtail_acc[...] = dtail_sum

        @pl.when(hb > 0)
        def _():
            dtail_acc[...] += dtail_sum

        @pl.when(hb == n_hb - 1)
        def _():
            dt_out[...] = dtail_acc[...].astype(dt_out.dtype)

        da_ref[...] += da_sum
        ddt_ref[...] += ddt_sum
        dgnb_ref[...] += dgnb_sum

    rev = lambda n: n_chunks - 1 - n
    out = pl.BlockSpec((CHUNK, _HPS * HEAD), lambda n, h: (rev(n), h))
    row = pl.BlockSpec((1, HEAD), lambda n, h: (0, 0))
    return pl.pallas_call(
        body, name="gdn_bwd", grid=(n_chunks, heads // _HPS),
        in_specs=_gdn_specs(rev)
        + [pl.BlockSpec((None, _HPS, HEAD, HEAD), lambda n, h: (rev(n), h, 0, 0)),
           pl.BlockSpec((CHUNK, _HPS * HEAD), lambda n, h: (rev(n), 8 // _HPS + h))],
        out_specs=[out, out, out, out, pl.BlockSpec((CHUNK, HEAD), lambda n, h: (rev(n), 0)), row, row, row],
        out_shape=[jax.ShapeDtypeStruct((S, 1024), F32)] * 3
        + [jax.ShapeDtypeStruct((S, 1024), BF16), jax.ShapeDtypeStruct((S, HEAD), BF16)]
        + [jax.ShapeDtypeStruct((1, HEAD), F32)] * 3,
        scratch_shapes=[pltpu.VMEM((heads // _HPS, _HPS, HEAD, HEAD), F32), pltpu.VMEM((CHUNK, HEAD), F32)],
        compiler_params=_cparams(("arbitrary", "arbitrary")),
    )(act, act, act, proj, proj, arow, dtrow, gnb, states, d_o)


def _conv_silu(x, w):
    t = _iota((x.shape[0], 1), 0)
    tap = _iota(w.shape, 0)
    y = jnp.zeros_like(x)
    for r in range(4):
        w_r = jnp.sum(jnp.where(tap == 3 - r, w, 0.0), axis=0, keepdims=True)
        y = y + (x if r == 0 else jnp.where(t >= r, _roll_rows(x, r), 0.0)) * w_r
    return jax.nn.silu(y)


_CONV_COLS = 128


def _conv_fwd(proj, conv_w):
    S = proj.shape[0]
    nb = 3072 // _CONV_COLS
    off = 4096 // _CONV_COLS

    def body(x_ref, w_ref, o_ref):
        o_ref[...] = _conv_silu(x_ref[...], w_ref[...])

    return pl.pallas_call(
        body, name="conv_fwd", grid=(nb,),
        in_specs=[pl.BlockSpec((S, _CONV_COLS), lambda j: (0, off + j)), pl.BlockSpec((4, _CONV_COLS), lambda j: (0, j))],
        out_specs=pl.BlockSpec((S, _CONV_COLS), lambda j: (0, j)),
        out_shape=jax.ShapeDtypeStruct((S, 3072), F32),
        compiler_params=_cparams(("parallel",)),
    )(proj, conv_w)


def _conv_bwd(proj, conv_w, dq, dk, dv):
    S = proj.shape[0]
    nb = 3072 // _CONV_COLS
    off = 4096 // _CONV_COLS
    per = 1024 // _CONV_COLS

    def body(x_ref, w_ref, dq_ref, dk_ref, dv_ref, dx_ref, dw_ref):
        j = pl.program_id(0)
        _, vjp = jax.vjp(_conv_silu, x_ref[...], w_ref[...])
        d = jnp.where(j < per, dq_ref[...], jnp.where(j < 2 * per, dk_ref[...], dv_ref[...]))
        dx, dw = vjp(d)
        dx_ref[...] = dx.astype(dx_ref.dtype)
        dw_ref[...] = dw

    dsp = lambda s: pl.BlockSpec((S, _CONV_COLS), lambda j, s=s: (0, jnp.clip(j - s * per, 0, per - 1)))
    return pl.pallas_call(
        body, name="conv_bwd", grid=(nb,),
        in_specs=[pl.BlockSpec((S, _CONV_COLS), lambda j: (0, off + j)), pl.BlockSpec((4, _CONV_COLS), lambda j: (0, j)),
                  dsp(0), dsp(1), dsp(2)],
        out_specs=[pl.BlockSpec((S, _CONV_COLS), lambda j: (0, j)), pl.BlockSpec((4, _CONV_COLS), lambda j: (0, j))],
        out_shape=[jax.ShapeDtypeStruct((S, 3072), BF16), jax.ShapeDtypeStruct((4, 3072), F32)],
        compiler_params=_cparams(("parallel",)),
    )(proj, conv_w, dq, dk, dv)


_S5_T = 512
_S5_L = 512
_S5_NB = 16
_S5_U = 4
_S5_UB = 3


def _s5_tile(xr, xi, pw_r, pw_i, cr, ci, reverse):
    t8 = _iota((8, 1), 0)
    for sh in (1, 2, 4):
        row = (8 - sh) if reverse else (sh - 1)
        ar, ai = pw_r[row:row + 1, :], pw_i[row:row + 1, :]
        if reverse:
            keep, amt = t8 < 8 - sh, 8 - sh
        else:
            keep, amt = t8 >= sh, sh
        sr = jnp.where(keep, pltpu.roll(xr, amt, 0), 0.0)
        si = jnp.where(keep, pltpu.roll(xi, amt, 0), 0.0)
        xr, xi = xr + ar * sr - ai * si, xi + ar * si + ai * sr
    xr, xi = xr + pw_r * cr - pw_i * ci, xi + pw_r * ci + pw_i * cr
    return xr, xi


def _s5_fwd(u, bre, bim, cre, cim, pw_r, pw_i, dskip):
    S = u.shape[0]
    T = min(_S5_T, S)
    nt = S // T

    def body(u_ref, bre_ref, bim_ref, cre_ref, cim_ref, pr_ref, pi_ref, d_ref, y_ref, xr_ref, xi_ref,
             bu_r, bu_i, car_r, car_i):
        @pl.when(pl.program_id(1) == 0)
        def _():
            car_r[...] = jnp.zeros_like(car_r)
            car_i[...] = jnp.zeros_like(car_i)

        uu = u_ref[...]
        bu_r[...] = _dot(uu, bre_ref[...])
        bu_i[...] = _dot(uu, bim_ref[...])
        pw_r, pw_i = pr_ref[...], pi_ref[...]

        def tiles(i, carry):
            ins = []
            for k in range(_S5_U):
                r0 = pl.multiple_of((i * _S5_U + k) * 8, 8)
                ins.append((r0, bu_r[pl.ds(r0, 8), :], bu_i[pl.ds(r0, 8), :]))
            outs = []
            for r0, br, bi in ins:
                xr, xi = _s5_tile(br, bi, pw_r, pw_i, carry[0], carry[1], False)
                carry = (xr[7:8, :], xi[7:8, :])
                outs.append((r0, xr, xi))
            for r0, xr, xi in outs:
                xr_ref[pl.ds(r0, 8), :] = xr
                xi_ref[pl.ds(r0, 8), :] = xi
            return carry

        cr, ci = lax.fori_loop(0, T // (8 * _S5_U), tiles, (car_r[...], car_i[...]))
        car_r[...] = cr
        car_i[...] = ci
        y_ref[...] = _dot(xr_ref[...], cre_ref[...]) - _dot(xi_ref[...], cim_ref[...]) + d_ref[...] * uu

    blk3 = lambda a, b: pl.BlockSpec((None, a, b), lambda j, t: (j, 0, 0))
    return pl.pallas_call(
        body, name="s5_fwd", grid=(_S5_NB, nt),
        in_specs=[pl.BlockSpec((T, HEAD), lambda j, t: (t, j)),
                  blk3(HEAD, _S5_L), blk3(HEAD, _S5_L), blk3(_S5_L, HEAD), blk3(_S5_L, HEAD),
                  blk3(8, _S5_L), blk3(8, _S5_L), pl.BlockSpec((1, HEAD), lambda j, t: (0, j))],
        out_specs=[pl.BlockSpec((T, HEAD), lambda j, t: (t, j)),
                   pl.BlockSpec((T, _S5_L), lambda j, t: (t, j)), pl.BlockSpec((T, _S5_L), lambda j, t: (t, j))],
        out_shape=[jax.ShapeDtypeStruct((S, D_MODEL), F32),
                   jax.ShapeDtypeStruct((S, _S5_NB * _S5_L), F32), jax.ShapeDtypeStruct((S, _S5_NB * _S5_L), F32)],
        scratch_shapes=[pltpu.VMEM((T, _S5_L), F32), pltpu.VMEM((T, _S5_L), F32),
                        pltpu.VMEM((1, _S5_L), F32), pltpu.VMEM((1, _S5_L), F32)],
        compiler_params=_cparams(("parallel", "arbitrary")),
    )(u, bre, bim, cre, cim, pw_r, pw_i, dskip)


def _s5_bwd(dy, u, xre, xim, bre, bim, cre, cim, qw_r, qw_i, dskip):
    S = u.shape[0]
    T = min(_S5_T, S)
    nt = S // T
    nt8 = T // 8

    def body(dy_ref, u_ref, xr_ref, xi_ref, xpr_ref, xpi_ref, bre_ref, bim_ref, cre_ref, cim_ref, qr_ref, qi_ref,
             d_ref, du_ref, dbr_ref, dbi_ref, dcr_ref, dci_ref, dlr_ref, dli_ref, dd_ref,
             g_r, g_i, car_r, car_i):
        t = pl.program_id(1)

        @pl.when(t == 0)
        def _():
            car_r[...] = jnp.zeros_like(car_r)
            car_i[...] = jnp.zeros_like(car_i)
            for r in (dbr_ref, dbi_ref, dcr_ref, dci_ref, dlr_ref, dli_ref, dd_ref):
                r[...] = jnp.zeros_like(r)

        dyy, uu = dy_ref[...], u_ref[...]
        g_r[...] = _dot(dyy, cre_ref[...], _NT)
        g_i[...] = -_dot(dyy, cim_ref[...], _NT)
        qw_r, qw_i = qr_ref[...], qi_ref[...]
        t8 = _iota((8, 1), 0)
        first = t == nt - 1

        def load(r0, prev_r, prev_i):
            rows = pl.ds(r0, 8)
            return r0, g_r[rows, :], g_i[rows, :], xr_ref[rows, :], xi_ref[rows, :], prev_r, prev_i

        def run(loaded, carry, acc):
            done = []
            for r0, dr, di, xr, xi, prev_r, prev_i in loaded:
                gr, gi = _s5_tile(dr, di, qw_r, qw_i, carry[0], carry[1], True)
                carry = (gr[0:1, :], gi[0:1, :])
                xpr = jnp.where(t8 >= 1, pltpu.roll(xr, 1, 0), prev_r)
                xpi = jnp.where(t8 >= 1, pltpu.roll(xi, 1, 0), prev_i)
                acc = (acc[0] + gr * xpr + gi * xpi, acc[1] + gi * xpr - gr * xpi)
                done.append((r0, gr, gi))
            for r0, gr, gi in done:
                g_r[pl.ds(r0, 8), :] = gr
                g_i[pl.ds(r0, 8), :] = gi
            return carry, acc

        def step(ii, state):
            loaded = []
            for k in range(_S5_UB):
                idx = nt8 - 1 - (ii * _S5_UB + k)
                r0 = pl.multiple_of(idx * 8, 8)
                p0 = pl.multiple_of((idx - 1) * 8, 8)
                loaded.append(load(r0, xr_ref[pl.ds(p0, 8), :][7:8, :], xi_ref[pl.ds(p0, 8), :][7:8, :]))
            return run(loaded, *state)

        zero = jnp.zeros((8, _S5_L), F32)
        assert (nt8 - 1) % _S5_UB == 0
        state = lax.fori_loop(0, (nt8 - 1) // _S5_UB, step, ((car_r[...], car_i[...]), (zero, zero)))
        prev_r = jnp.where(first, 0.0, xpr_ref[...][7:8, :])
        prev_i = jnp.where(first, 0.0, xpi_ref[...][7:8, :])
        (cr, ci), (ar, ai) = run([load(0, prev_r, prev_i)], *state)
        car_r[...] = cr
        car_i[...] = ci
        dlr_ref[...] += ar
        dli_ref[...] += ai
        gr, gi = g_r[...], g_i[...]
        du_ref[...] = _dot(gr, bre_ref[...], _NT) + _dot(gi, bim_ref[...], _NT) + d_ref[...] * dyy
        dbr_ref[...] += _dot(uu, gr, _TN)
        dbi_ref[...] += _dot(uu, gi, _TN)
        dcr_ref[...] += _dot(xr_ref[...], dyy, _TN)
        dci_ref[...] -= _dot(xi_ref[...], dyy, _TN)
        dd_ref[...] += jnp.sum(dyy * uu, axis=0, keepdims=True)

    rev = lambda t: nt - 1 - t
    blk3 = lambda a, b: pl.BlockSpec((None, a, b), lambda j, t: (j, 0, 0))
    tl = pl.BlockSpec((T, HEAD), lambda j, t: (rev(t), j))
    xs = pl.BlockSpec((T, _S5_L), lambda j, t: (rev(t), j))
    xp = pl.BlockSpec((8, _S5_L), lambda j, t: (jnp.maximum(rev(t) * nt8 - 1, 0), j))
    return pl.pallas_call(
        body, name="s5_bwd", grid=(_S5_NB, nt),
        in_specs=[tl, tl, xs, xs, xp, xp, blk3(HEAD, _S5_L), blk3(HEAD, _S5_L), blk3(_S5_L, HEAD), blk3(_S5_L, HEAD),
                  blk3(8, _S5_L), blk3(8, _S5_L), pl.BlockSpec((1, HEAD), lambda j, t: (0, j))],
        out_specs=[tl, blk3(HEAD, _S5_L), blk3(HEAD, _S5_L), blk3(_S5_L, HEAD), blk3(_S5_L, HEAD),
                   blk3(8, _S5_L), blk3(8, _S5_L), pl.BlockSpec((1, HEAD), lambda j, t: (0, j))],
        out_shape=[jax.ShapeDtypeStruct((S, D_MODEL), F32),
                   jax.ShapeDtypeStruct((_S5_NB, HEAD, _S5_L), F32), jax.ShapeDtypeStruct((_S5_NB, HEAD, _S5_L), F32),
                   jax.ShapeDtypeStruct((_S5_NB, _S5_L, HEAD), F32), jax.ShapeDtypeStruct((_S5_NB, _S5_L, HEAD), F32),
                   jax.ShapeDtypeStruct((_S5_NB, 8, _S5_L), F32), jax.ShapeDtypeStruct((_S5_NB, 8, _S5_L), F32),
                   jax.ShapeDtypeStruct((1, D_MODEL), F32)],
        scratch_shapes=[pltpu.VMEM((T, _S5_L), F32), pltpu.VMEM((T, _S5_L), F32),
                        pltpu.VMEM((1, _S5_L), F32), pltpu.VMEM((1, _S5_L), F32)],
        compiler_params=_cparams(("parallel", "arbitrary")),
    )(dy, u, xre, xim, xre, xim, bre, bim, cre, cim, qw_r, qw_i, dskip)


def _s5_params(a_re, a_im, log_dt, b_re, b_im, c_re, c_im):
    step = jnp.exp(log_dt)[:, None]
    mag = jnp.exp(a_re * step)
    lr, li = mag * jnp.cos(a_im * step), mag * jnp.sin(a_im * step)
    den = a_re * a_re + a_im * a_im
    nr, ni = lr - 1.0, li
    kr, ki = (nr * a_re + ni * a_im) / den, (ni * a_re - nr * a_im) / den
    bbr = kr[..., None] * b_re - ki[..., None] * b_im
    bbi = kr[..., None] * b_im + ki[..., None] * b_re
    eye = jnp.eye(8, dtype=F32)

    def blk_b(bb):
        t = bb.reshape(_S5_NB, 8, 64, 16).transpose(0, 1, 3, 2)
        return (t[:, :, :, None, :] * eye[None, :, None, :, None]).reshape(_S5_NB, HEAD, _S5_L)

    def blk_c(cc):
        t = cc.reshape(_S5_NB, 8, 16, 64).transpose(0, 1, 3, 2)
        return (t[:, :, :, None, :] * eye[None, :, None, :, None]).reshape(_S5_NB, _S5_L, HEAD)

    return (blk_b(bbr), blk_b(bbi), blk_c(c_re), blk_c(c_im),
            lr.reshape(_S5_NB, _S5_L), li.reshape(_S5_NB, _S5_L))


def _s5_powers(lr, li):
    pr, pi = [lr], [li]
    for _ in range(7):
        pr, pi = pr + [pr[-1] * lr - pi[-1] * li], pi + [pr[-1] * li + pi[-1] * lr]
    return jnp.stack(pr, axis=1), jnp.stack(pi, axis=1)


_MESH = pl.DeviceIdType.MESH
_ANY = pl.BlockSpec(memory_space=pl.ANY)


def _place():
    x, y, c = lax.axis_index("x"), lax.axis_index("y"), lax.axis_index("c")
    return x, y, c, [(1 - x, y), (x, 1 - y), (1 - x, 1 - y)]


def _comm_call(body, arrs, out_shapes, n_remote, name):
    n = len(arrs)
    return pl.pallas_call(
        body, name=name,
        in_specs=[_ANY] * n, out_specs=[_ANY] * n, out_shape=out_shapes,
        scratch_shapes=[pltpu.SemaphoreType.DMA((n * n_remote,)), pltpu.SemaphoreType.DMA((n * n_remote,)),
                        pltpu.SemaphoreType.DMA((n,))],
    )(*arrs)


def _own_slot(shard, me_idx, name, rows=256):
    _, half, C = shard.shape
    rows = min(rows, half)
    assert half % rows == 0

    def body(me_ref, in_ref, o_ref):
        o_ref[...] = in_ref[...]

    return pl.pallas_call(
        body, name=name,
        grid_spec=pltpu.PrefetchScalarGridSpec(
            num_scalar_prefetch=1, grid=(2, half // rows),
            in_specs=[pl.BlockSpec((None, rows, C), lambda h, i, me: (h, i, 0))],
            out_specs=pl.BlockSpec((None, None, rows, C), lambda h, i, me: (me[0], h, i, 0))),
        out_shape=jax.ShapeDtypeStruct((4,) + shard.shape, shard.dtype),
        compiler_params=_cparams(("parallel", "parallel")),
    )(me_idx, shard)


def _ag_xy(bufs, name):
    n = len(bufs)

    def body(*refs):
        outs, (ssem, rsem) = refs[n:2 * n], refs[2 * n:]
        x, y, c, chips = _place()
        me, sib = 2 * x + y, (x, y, 1 - c)

        def copy(a, k, src, dst, to):
            return pltpu.make_async_remote_copy(src_ref=src, dst_ref=dst, send_sem=ssem.at[6 * a + k],
                                                recv_sem=rsem.at[6 * a + k], device_id=to, device_id_type=_MESH)

        sends = []
        for j, (px, py) in enumerate(chips):
            for a in range(n):
                cp = copy(a, j, outs[a].at[me, c], outs[a].at[me, c], (px, py, c))
                cp.start()
                sends.append(cp)
        for j, (px, py) in enumerate(chips):
            for a in range(n):
                landed = outs[a].at[2 * px + py, c]
                copy(a, j, landed, landed, (px, py, c)).wait_recv()
                fwd = copy(a, 3 + j, landed, landed, sib)
                fwd.start()
                sends.append(fwd)
        for j, (px, py) in enumerate(chips):
            for a in range(n):
                passed = outs[a].at[2 * px + py, 1 - c]
                copy(a, 3 + j, passed, passed, sib).wait_recv()
        for cp in sends:
            cp.wait_send()

    return pl.pallas_call(
        body, name=name,
        in_specs=[_ANY] * n, out_specs=[_ANY] * n,
        out_shape=[jax.ShapeDtypeStruct(b.shape, b.dtype) for b in bufs],
        input_output_aliases={a: a for a in range(n)},
        scratch_shapes=[pltpu.SemaphoreType.DMA((6 * n,)), pltpu.SemaphoreType.DMA((6 * n,))],
    )(*bufs)


def _pair_split(arrs, name):
    n = len(arrs)

    def body(*refs):
        ins, outs, (ssem, rsem, _) = refs[:n], refs[n:2 * n], refs[2 * n:]
        x, y, c, _ = _place()
        copies = []
        for a in range(n):
            for s in range(4):
                cp = pltpu.make_async_remote_copy(src_ref=ins[a].at[s, 1 - c], dst_ref=outs[a].at[s],
                                                  send_sem=ssem.at[4 * a + s], recv_sem=rsem.at[4 * a + s],
                                                  device_id=(x, y, 1 - c), device_id_type=_MESH)
                cp.start()
                copies.append(cp)
        for cp in copies:
            cp.wait()

    shapes = [jax.ShapeDtypeStruct((4,) + a.shape[2:], a.dtype) for a in arrs]
    return _comm_call(body, arrs, shapes, 4, name)


def _pair_swap(arrs, name):
    n = len(arrs)

    def body(*refs):
        ins, outs, (ssem, rsem, _) = refs[:n], refs[n:2 * n], refs[2 * n:]
        x, y, c, _ = _place()
        copies = []
        for a in range(n):
            cp = pltpu.make_async_remote_copy(src_ref=ins[a], dst_ref=outs[a], send_sem=ssem.at[a], recv_sem=rsem.at[a],
                                              device_id=(x, y, 1 - c), device_id_type=_MESH)
            cp.start()
            copies.append(cp)
        for cp in copies:
            cp.wait()

    return _comm_call(body, arrs, [jax.ShapeDtypeStruct(a.shape, a.dtype) for a in arrs], 1, name)


def _pair_sum(full, recv, c_idx, name, rows=128):
    _, _, half, C = full.shape
    rows = min(rows, half)
    nb = half // rows
    assert half % rows == 0

    def body(c_ref, a_ref, b_ref, o_ref):
        o_ref[...] = (a_ref[...].astype(F32) + b_ref[...].astype(F32)).astype(o_ref.dtype)

    return pl.pallas_call(
        body, name=name,
        grid_spec=pltpu.PrefetchScalarGridSpec(
            num_scalar_prefetch=1, grid=(4, nb),
            in_specs=[pl.BlockSpec((None, None, rows, C), lambda s, i, c: (s, c[0], i, 0)),
                      pl.BlockSpec((None, rows, C), lambda s, i, c: (s, i, 0))],
            out_specs=pl.BlockSpec((None, rows, C), lambda s, i, c: (s, i, 0))),
        out_shape=jax.ShapeDtypeStruct(recv.shape, recv.dtype),
        compiler_params=_cparams(("parallel", "parallel")),
    )(c_idx, full, recv)


def _rs_xy(arrs, name):
    n = len(arrs)

    def body(*refs):
        ins, outs, (ssem, rsem, lsem) = refs[:n], refs[n:2 * n], refs[2 * n:]
        x, y, c, chips = _place()
        me = 2 * x + y
        copies = []
        for a in range(n):
            loc = pltpu.make_async_copy(ins[a].at[me], outs[a].at[3], lsem.at[a])
            loc.start()
            copies.append(loc)
            for j, (px, py) in enumerate(chips):
                cp = pltpu.make_async_remote_copy(src_ref=ins[a].at[2 * px + py], dst_ref=outs[a].at[j],
                                                  send_sem=ssem.at[3 * a + j], recv_sem=rsem.at[3 * a + j],
                                                  device_id=(px, py, c), device_id_type=_MESH)
                cp.start()
                copies.append(cp)
        for cp in copies:
            cp.wait()

    return _comm_call(body, arrs, [jax.ShapeDtypeStruct(a.shape, a.dtype) for a in arrs], 3, name)


def _gather8(buf, name):
    def body(in_ref, out_ref, ssem, rsem, lsem):
        x, y, c, _ = _place()
        me = 4 * x + 2 * y + c
        loc = pltpu.make_async_copy(in_ref, out_ref.at[me], lsem.at[0])
        loc.start()
        copies = [loc]
        for d in range(1, 8):
            dx, dy, dc = (d >> 2) & 1, (d >> 1) & 1, d & 1
            peer = (1 - x if dx else x, 1 - y if dy else y, 1 - c if dc else c)
            cp = pltpu.make_async_remote_copy(src_ref=in_ref, dst_ref=out_ref.at[me], send_sem=ssem.at[d - 1],
                                              recv_sem=rsem.at[d - 1], device_id=peer, device_id_type=_MESH)
            cp.start()
            copies.append(cp)
        for cp in copies:
            cp.wait()

    return _comm_call(body, [buf], [jax.ShapeDtypeStruct((8,) + buf.shape, buf.dtype)], 7, name)[0]


def _sum_slots(arr, name, rows=128):
    k, R, C = arr.shape
    rows = min(rows, R)
    assert R % rows == 0

    def body(in_ref, o_ref):
        acc = in_ref[0].astype(F32)
        for s in range(1, k):
            acc = acc + in_ref[s].astype(F32)
        o_ref[...] = acc

    return pl.pallas_call(
        body, name=name, grid=(R // rows,),
        in_specs=[pl.BlockSpec((k, rows, C), lambda i: (0, i, 0))],
        out_specs=pl.BlockSpec((rows, C), lambda i: (i, 0)),
        out_shape=jax.ShapeDtypeStruct((R, C), F32),
        compiler_params=_cparams(("parallel",)),
    )(arr)


ADAM_LR, ADAM_B1, ADAM_B2, ADAM_EPS, ADAM_WD, ADAM_STEP = 0.001, 0.9, 0.999, 1e-08, 0.01, 10


def _adam_math(w, g, m, v):
    m = ADAM_B1 * m + (1.0 - ADAM_B1) * g
    v = ADAM_B2 * v + (1.0 - ADAM_B2) * jnp.square(g)
    m_hat = m / (1.0 - ADAM_B1 ** ADAM_STEP)
    v_hat = v / (1.0 - ADAM_B2 ** ADAM_STEP)
    delta = -ADAM_LR * (m_hat / (jnp.sqrt(v_hat) + ADAM_EPS) + ADAM_WD * w)
    return delta, m, v


def _adamw(w, m, v, mine, other, c_idx, name, g_off=0, rows=128):
    R, C = w.shape
    half = mine.shape[0]
    rows = min(rows, R)
    assert R % rows == 0 and g_off % rows == 0 and half % rows == 0
    nbh, b0 = half // rows, g_off // rows

    def body(c_ref, w_ref, m_ref, v_ref, mine_ref, other_ref, go, do, mo, vo):
        in_my_half = (b0 + pl.program_id(0)) // nbh == c_ref[0]
        g = jnp.where(in_my_half, mine_ref[...], other_ref[...])
        delta, m1, v1 = _adam_math(w_ref[...], g, m_ref[...], v_ref[...])
        go[...] = g
        do[...] = delta
        mo[...] = m1
        vo[...] = v1

    blk = pl.BlockSpec((rows, C), lambda i, c: (i, 0))
    gblk = pl.BlockSpec((rows, C), lambda i, c: ((b0 + i) % nbh, 0))
    return pl.pallas_call(
        body, name=name,
        grid_spec=pltpu.PrefetchScalarGridSpec(
            num_scalar_prefetch=1, grid=(R // rows,),
            in_specs=[blk] * 3 + [gblk] * 2, out_specs=[blk] * 4),
        out_shape=[jax.ShapeDtypeStruct((R, C), F32)] * 4,
        compiler_params=_cparams(("parallel",)),
    )(c_idx, w, m, v, mine, other)


_WEIGHTS = ['norm_mix', 'norm_mlp', 'norm_ple', 'w_in_e', 'w_out_e', 'hgrn_lb', 'g_norm_a', 'conv_w', 'a_log',
            'dt_bias', 'g_norm_b', 's5_a_re', 's5_a_im', 's5_b_re', 's5_b_im', 's5_c_re', 's5_c_im', 's5_d',
            's5_log_dt', 'w_glu', 'b_glu', 'w_out_o', 'w_up', 'w_down', 'w_ple_gate', 'w_ple_proj', 'final_norm']
_INPUTS = ['x', 'p'] + _WEIGHTS + ['loss_target'] + ['m_' + n for n in _WEIGHTS] + ['v_' + n for n in _WEIGHTS]
_FAMILY = [('w_up', 'col', 2048), ('w_down', 'row', 2048), ('w_ple_gate', 'row', 512),
           ('w_out_e', 'row', 512), ('w_glu', 'row', 512), ('w_out_o', 'row', 512)]
_IN_PAD = 8320
_IN_BLK = 640


def _rms_bwd_fn(x, d_hn, d_res, g):
    _, vjp = jax.vjp(_rms, x, g)
    dx, dg = vjp(d_hn.astype(F32))
    return dx + d_res, dg


def _add_res(acc, h):
    return (acc + h,)


def _pack_rows(parts, lanes=128, mult=256):
    flat = jnp.concatenate([q.reshape(-1).astype(F32) for q in parts])
    n = flat.shape[0]
    rows = -(-n // (lanes * mult)) * mult
    return jnp.pad(flat, (0, rows * lanes - n)).reshape(rows, lanes)


def _unpack_rows(buf, shapes):
    flat, out, off = buf.reshape(-1), [], 0
    for s in shapes:
        n = math.prod(s)
        out.append(flat[off:off + n].reshape(s))
        off += n
    return out


def _step(a):
    S = a['x'].shape[1]
    x, tgt = a['x'][0], a['loss_target'][0]
    xi, yi = lax.axis_index("x"), lax.axis_index("y")
    me = 2 * xi + yi
    row = lambda t: t.reshape(1, -1)

    views, packs, off = {}, [], 0
    for nm, kind, r in _FAMILY:
        for l in range(a[nm].shape[0]):
            views[nm, l] = (kind, off, r)
            packs.append(a[nm][l].astype(BF16))
            off += r
    rt = off
    small_sh = jnp.concatenate([a['conv_w'][0].reshape(-1), a['s5_d'][0], a['b_glu'][0]]).reshape(16, 256)
    shards = [jnp.concatenate(packs, axis=0), a['w_in_e'][0].astype(BF16),
              a['w_ple_proj'].reshape(512, 512).astype(BF16), small_sh]
    me_idx = me.astype(jnp.int32).reshape(1)
    gathered = _ag_xy([_own_slot(t.reshape(2, t.shape[0] // 2, t.shape[1]), me_idx, name=f"own_slot{i}")
                       for i, t in enumerate(shards)], name="gather_weights")
    gp, g_in, g_pp, g_small = [g.reshape((4,) + t.shape) for g, t in zip(gathered, shards)]
    win = jnp.pad(g_in.transpose(1, 0, 2).reshape(D_MODEL, 8208), ((0, 0), (0, _IN_PAD - 8208)))
    w_pp = g_pp.reshape(4, 2, 256, 512).transpose(1, 2, 0, 3).reshape(2, 256, D_MODEL)
    g_small = g_small.reshape(4, 4096)
    conv_full = g_small[:, :3072].reshape(4, 4, 768).transpose(1, 0, 2).reshape(4, 3072)
    s5_d_full = g_small[:, 3072:3584].reshape(1, D_MODEL)
    b_glu_full = g_small[:, 3584:].reshape(1, D_MODEL)

    lb, lb_vjp = jax.vjp(lambda hl: jax.nn.softmax(hl, axis=0)[0:1], a['hgrn_lb'])
    pad_h = lambda t: jnp.pad(t, ((0, 0), (0, HEAD - t.shape[1])))
    (arow, dtrow), gdn_vjp = jax.vjp(lambda al, dt: (pad_h(-jnp.exp(al)), pad_h(dt)), a['a_log'], a['dt_bias'])
    s5p, s5_vjp = jax.vjp(_s5_params, a['s5_a_re'][0], a['s5_a_im'][0], a['s5_log_dt'][0], a['s5_b_re'][0],
                          a['s5_b_im'][0], a['s5_c_re'][0], a['s5_c_im'][0])
    bre, bim, cre, cim = [t.astype(BF16) for t in s5p[:4]]
    pw_r, pw_i = _s5_powers(s5p[4], s5p[5])
    gna, gnb = a['g_norm_a'], a['g_norm_b']

    def norm_cast(h, g, dt, name):
        return _rowwise(_rms, [h], [row(g)], [(D_MODEL, dt)], name=name)

    def mlp_ple_fwd(h, l):
        hn = norm_cast(h, a['norm_mlp'][l], BF16, f"rms_mlp{l}")
        up, act = _mm(hn, gp, b_view=views['w_up', l], out_dtypes=(BF16, BF16), name=f"up{l}",
                      epilogue=lambda acc: (acc, jnp.square(jnp.maximum(acc, 0.0))))
        h2 = _mm(act, gp, b_view=views['w_down', l], epilogue=_add_res, extras=(h,), name=f"down{l}")
        hnp = norm_cast(h2, a['norm_ple'][l], BF16, f"rms_ple{l}")
        pp = _mm(a['p'][l, 0], w_pp[l], name=f"ple_proj{l}")
        h3, gpre = _mm(hnp, gp, b_view=views['w_ple_gate', l], extras=(h2, pp), out_dtypes=(F32, F32), name=f"ple_gate{l}",
                       bm=512, epilogue=lambda acc, h2, pp: (h2 + jax.nn.sigmoid(acc) * pp, acc))
        return h3, (h, hn, up, act, h2, hnp, pp, gpre)

    hn0 = norm_cast(x, a['norm_mix'][0], BF16, "rms_mix0")
    proj = _mm(hn0, win, bn=_IN_BLK, name="in_proj")
    o_a, st_a = _hgrn_fwd(proj, lb, gna)
    act_b = _conv_fwd(proj, conv_full)
    o_b, st_b = _gdn_fwd(act_b, proj, arow, dtrow, gnb)
    merged = jnp.concatenate([o_a, o_b], axis=1)
    h1 = _mm(merged, gp, b_view=views['w_out_e', 0], epilogue=_add_res, extras=(x,), name="out_e")
    h3, saved0 = mlp_ple_fwd(h1, 0)

    u = norm_cast(h3, a['norm_mix'][1], F32, "rms_mix1")
    y, xre, xim = _s5_fwd(u, bre, bim, cre, cim, pw_r, pw_i, s5_d_full)
    actg = _rowwise(jax.nn.gelu, [y], [], [(D_MODEL, BF16)], name="gelu")
    glu, z = _mm(actg, gp, b_view=views['w_glu', 0], extras=(actg, b_glu_full), out_dtypes=(BF16, F32), name="glu",
                 epilogue=lambda acc, act, b: (act.astype(F32) * jax.nn.sigmoid(acc + b), acc + b))
    h4 = _mm(glu, gp, b_view=views['w_out_o', 0], epilogue=_add_res, extras=(h3,), name="out_o")
    h6, saved1 = mlp_ple_fwd(h4, 1)

    def head(h, t, g):
        def f(h, g):
            e = _rms(h, g) - t
            return 0.5 * jnp.sum(jnp.mean(e * e, axis=-1))
        val, vjp = jax.vjp(f, h, g)
        dh, dg = vjp(jnp.ones((), F32))
        return dh, dg, jnp.full((1, HEAD), val)

    dh, d_final, loss_part = _rowwise(head, [h6, tgt], [row(a['final_norm'])], [(D_MODEL, F32)], sums=(D_MODEL, HEAD),
                                      name="loss_head")
    loss = lax.psum(loss_part[0, 0], ("x", "y", "c"))

    gbuf = lax.empty((4, rt, PACK_COLS), BF16)

    def into(gbuf, lhs, rhs, key, name):
        return _mm(lhs, rhs, dims="tn", out_into=gbuf, out_view=views[key], out_dtypes=(BF16,), name=name)

    def mlp_ple_bwd(dh, l, saved, gbuf):
        h, hn, up, act, h2, hnp, pp, gpre = saved

        def gate_bwd(d, gpre, pp):
            s = jax.nn.sigmoid(gpre)
            return d * s, d * pp * s * (1.0 - s)

        d_pp, d_gp = _rowwise(gate_bwd, [dh, gpre, pp], [], [(D_MODEL, BF16), (D_MODEL, BF16)], name=f"ple_bwd{l}")
        d_wpp = _mm(a['p'][l, 0], d_pp, dims="tn", name=f"d_ple_proj{l}")
        gbuf = into(gbuf, hnp, d_gp, ('w_ple_gate', l), f"d_ple_gate{l}")
        d_hnp = _mm(d_gp, gp, dims="nt", b_view=views['w_ple_gate', l], name=f"ple_gate_t{l}")
        dh2, d_nple = _rowwise(_rms_bwd_fn, [h2, d_hnp, dh], [row(a['norm_ple'][l])], [(D_MODEL, F32)], sums=(D_MODEL,),
                               name=f"rms_ple_bwd{l}")
        d_up = _mm(dh2, gp, dims="nt", b_view=views['w_down', l], extras=(up,), out_dtypes=(BF16,), name=f"down_t{l}",
                   epilogue=lambda acc, up: (acc * 2.0 * jnp.maximum(up.astype(F32), 0.0),))
        gbuf = into(gbuf, act, dh2, ('w_down', l), f"d_down{l}")
        gbuf = into(gbuf, hn, d_up, ('w_up', l), f"d_up{l}")
        d_hn = _mm(d_up, gp, dims="nt", b_view=views['w_up', l], name=f"up_t{l}")
        dh1, d_nmlp = _rowwise(_rms_bwd_fn, [h, d_hn, dh2], [row(a['norm_mlp'][l])], [(D_MODEL, F32)], sums=(D_MODEL,),
                               name=f"rms_mlp_bwd{l}")
        return dh1, gbuf, d_wpp, d_nple, d_nmlp

    dh4, gbuf, d_wpp1, d_nple1, d_nmlp1 = mlp_ple_bwd(dh, 1, saved1, gbuf)

    d_glu = _mm(dh4, gp, dims="nt", b_view=views['w_out_o', 0], name="out_o_t")
    gbuf = into(gbuf, glu, dh4, ('w_out_o', 0), "d_out_o")

    def glu_bwd(d, z, act):
        s, act = jax.nn.sigmoid(z), act.astype(F32)
        dz = d * act * s * (1.0 - s)
        return dz, d * s, jnp.sum(dz, axis=0, keepdims=True)

    d_z, d_actp, d_bglu = _rowwise(glu_bwd, [d_glu, z, actg], [], [(D_MODEL, BF16), (D_MODEL, F32)], sums=(D_MODEL,),
                                   name="glu_bwd")
    gbuf = into(gbuf, actg, d_z, ('w_glu', 0), "d_glu")

    def gelu_bwd(acc, dap, y):
        _, vjp = jax.vjp(jax.nn.gelu, y)
        return vjp(acc + dap)

    dy = _mm(d_z, gp, dims="nt", b_view=views['w_glu', 0], extras=(d_actp, y), epilogue=gelu_bwd, name="glu_t")
    du, d_bre, d_bim, d_cre, d_cim, d_lr, d_li, d_s5d = _s5_bwd(dy, u, xre, xim, bre, bim, cre, cim,
                                                                 pw_r[:, ::-1], -pw_i[:, ::-1], s5_d_full)
    dh3, d_nmix1 = _rowwise(_rms_bwd_fn, [h3, du, dh4], [row(a['norm_mix'][1])], [(D_MODEL, F32)], sums=(D_MODEL,),
                            name="rms_mix1_bwd")
    d_are, d_aim, d_logdt, d_sbre, d_sbim, d_scre, d_scim = s5_vjp(
        (d_bre, d_bim, d_cre, d_cim, d_lr.sum(axis=1), d_li.sum(axis=1)))

    dh1, gbuf, d_wpp0, d_nple0, d_nmlp0 = mlp_ple_bwd(dh3, 0, saved0, gbuf)

    d_merged = _mm(dh1, gp, dims="nt", b_view=views['w_out_e', 0], name="out_e_t")
    gbuf = into(gbuf, merged, dh1, ('w_out_e', 0), "d_out_e")
    dq, df, dv, dg, d_lb, d_gna = _hgrn_bwd(proj, lb, gna, st_a, d_merged)
    dqb, dkb, dvb, dzb, d_tail, d_arow, d_dtrow, d_gnb = _gdn_bwd(act_b, proj, arow, dtrow, gnb, st_b, d_merged)
    d_pre, d_conv = _conv_bwd(proj, conv_full, dqb, dkb, dvb)
    d_proj = jnp.concatenate([dq, df, dv, dg, d_pre, dzb, d_tail], axis=1)
    d_win = _mm(hn0, d_proj, dims="tn", bn=_IN_BLK, out_dtypes=(BF16,), name="d_in_proj")
    d_hn0 = _mm(d_proj, win, dims="nt", bk=_IN_BLK, name="in_proj_t")
    grad_x, d_nmix0 = _rowwise(_rms_bwd_fn, [x, d_hn0, dh1], [row(a['norm_mix'][0])], [(D_MODEL, F32)], sums=(D_MODEL,),
                               name="rms_mix0_bwd")
    (d_hlb,) = lb_vjp(d_lb.reshape(1, 1024))
    d_alog, d_dtb = gdn_vjp((d_arow, d_dtrow))

    d_win_sh = d_win[:, :8208].reshape(D_MODEL, 4, 2052).transpose(1, 0, 2)
    d_wpp_sh = jnp.stack([d_wpp0, d_wpp1]).astype(BF16).reshape(2, 256, 4, 512).transpose(2, 0, 1, 3).reshape(4, 512, 512)
    partials = [t.reshape(4, 2, t.shape[1] // 2, t.shape[2]) for t in (gbuf, d_win_sh, d_wpp_sh)]
    c_idx = lax.axis_index("c").astype(jnp.int32).reshape(1)
    chip_sums = [_pair_sum(f, r, c_idx, name=f"pair_sum{i}")
                 for i, (f, r) in enumerate(zip(partials, _pair_split(partials, name="pair_split")))]
    halves = [_sum_slots(t, name=f"sum_partials{i}") for i, t in enumerate(_rs_xy(chip_sums, name="scatter_grads"))]
    others = _pair_swap(halves, name="swap_halves")

    small = {
        'norm_mix': jnp.concatenate([d_nmix0, d_nmix1]), 'norm_mlp': jnp.concatenate([d_nmlp0, d_nmlp1]),
        'norm_ple': jnp.concatenate([d_nple0, d_nple1]), 'hgrn_lb': d_hlb, 'g_norm_a': d_gna, 'conv_w': d_conv,
        'a_log': d_alog, 'dt_bias': d_dtb, 'g_norm_b': d_gnb, 's5_a_re': d_are, 's5_a_im': d_aim, 's5_b_re': d_sbre,
        's5_b_im': d_sbim, 's5_c_re': d_scre, 's5_c_im': d_scim, 's5_d': d_s5d, 's5_log_dt': d_logdt,
        'b_glu': d_bglu, 'final_norm': d_final}
    s_names = list(small)
    s_shapes = [tuple(small[n].shape) for n in s_names]
    reduced = _sum_slots(_gather8(_pack_rows([small[n] for n in s_names]), name="gather_small"), name="sum_small")
    sg = dict(zip(s_names, _unpack_rows(reduced, s_shapes)))
    sg['conv_w'] = lax.dynamic_slice_in_dim(sg['conv_w'], me * 768, 768, axis=1)
    sg['s5_d'] = lax.dynamic_slice_in_dim(sg['s5_d'], me * 512, 512, axis=1)
    sg['b_glu'] = lax.dynamic_slice_in_dim(sg['b_glu'], me * 512, 512, axis=1)
    sg = {n: sg[n].reshape(a[n].shape) for n in s_names}
    w_pack, m_pack, v_pack, g_pack = [_pack_rows([src[pre + n] for n in s_names])
                                      for src, pre in ((a, ''), (a, 'm_'), (a, 'v_'), (sg, ''))]
    sd, sm, sv = _rowwise(_adam_math, [w_pack, g_pack, m_pack, v_pack], [], [(128, F32)] * 3, name="adam_small")
    w_shapes = [tuple(a[n].shape) for n in s_names]
    res = {n: (sg[n], d_, m_, v_) for n, d_, m_, v_ in zip(s_names, _unpack_rows(sd, w_shapes), _unpack_rows(sm, w_shapes),
                                                          _unpack_rows(sv, w_shapes))}

    def adam_big(nm, k, g_off, shape2d):
        outs = _adamw(a[nm].reshape(shape2d), a['m_' + nm].reshape(shape2d), a['v_' + nm].reshape(shape2d), halves[k],
                      others[k], c_idx, name="adam_" + nm, g_off=g_off)
        res[nm] = tuple(o.reshape(a[nm].shape) for o in outs)

    for nm, _, r in _FAMILY:
        adam_big(nm, 0, views[nm, 0][1], (a[nm].shape[0] * r, PACK_COLS))
    adam_big('w_in_e', 1, 0, (D_MODEL, 2052))
    adam_big('w_ple_proj', 2, 0, (512, 512))

    return (loss, grad_x[None], *[res[n][0] for n in _WEIGHTS], *[res[n][1] for n in _WEIGHTS],
            *[res[n][2] for n in _WEIGHTS], *[res[n][3] for n in _WEIGHTS])


def kernel(x, p, norm_mix, norm_mlp, norm_ple, w_in_e, w_out_e, hgrn_lb, g_norm_a, conv_w, a_log, dt_bias, g_norm_b, s5_a_re, s5_a_im, s5_b_re, s5_b_im, s5_c_re, s5_c_im, s5_d, s5_log_dt, w_glu, b_glu, w_out_o, w_up, w_down, w_ple_gate, w_ple_proj, final_norm, loss_target, m_norm_mix, m_norm_mlp, m_norm_ple, m_w_in_e, m_w_out_e, m_hgrn_lb, m_g_norm_a, m_conv_w, m_a_log, m_dt_bias, m_g_norm_b, m_s5_a_re, m_s5_a_im, m_s5_b_re, m_s5_b_im, m_s5_c_re, m_s5_c_im, m_s5_d, m_s5_log_dt, m_w_glu, m_b_glu, m_w_out_o, m_w_up, m_w_down, m_w_ple_gate, m_w_ple_proj, m_final_norm, v_norm_mix, v_norm_mlp, v_norm_ple, v_w_in_e, v_w_out_e, v_hgrn_lb, v_g_norm_a, v_conv_w, v_a_log, v_dt_bias, v_g_norm_b, v_s5_a_re, v_s5_a_im, v_s5_b_re, v_s5_b_im, v_s5_c_re, v_s5_c_im, v_s5_d, v_s5_log_dt, v_w_glu, v_b_glu, v_w_out_o, v_w_up, v_w_down, v_w_ple_gate, v_w_ple_proj, v_final_norm):
    args = (x, p, norm_mix, norm_mlp, norm_ple, w_in_e, w_out_e, hgrn_lb, g_norm_a, conv_w, a_log, dt_bias, g_norm_b, s5_a_re, s5_a_im, s5_b_re, s5_b_im, s5_c_re, s5_c_im, s5_d, s5_log_dt, w_glu, b_glu, w_out_o, w_up, w_down, w_ple_gate, w_ple_proj, final_norm, loss_target, m_norm_mix, m_norm_mlp, m_norm_ple, m_w_in_e, m_w_out_e, m_hgrn_lb, m_g_norm_a, m_conv_w, m_a_log, m_dt_bias, m_g_norm_b, m_s5_a_re, m_s5_a_im, m_s5_b_re, m_s5_b_im, m_s5_c_re, m_s5_c_im, m_s5_d, m_s5_log_dt, m_w_glu, m_b_glu, m_w_out_o, m_w_up, m_w_down, m_w_ple_gate, m_w_ple_proj, m_final_norm, v_norm_mix, v_norm_mlp, v_norm_ple, v_w_in_e, v_w_out_e, v_hgrn_lb, v_g_norm_a, v_conv_w, v_a_log, v_dt_bias, v_g_norm_b, v_s5_a_re, v_s5_a_im, v_s5_b_re, v_s5_b_im, v_s5_c_re, v_s5_c_im, v_s5_d, v_s5_log_dt, v_w_glu, v_b_glu, v_w_out_o, v_w_up, v_w_down, v_w_ple_gate, v_w_ple_proj, v_final_norm)
    return _step(dict(zip(_INPUTS, args)))
```

```python
import functools
import math

import jax
import jax.numpy as jnp
from jax import lax
from jax.experimental import pallas as pl
from jax.experimental.pallas import tpu as pltpu

F32 = jnp.float32
BF16 = jnp.bfloat16

D_MODEL = 2048
SEQ = 4096
NORM_EPS = 1e-6
CHUNK = 64
HEAD = 128
VMEM_LIMIT = 56 * 1024 * 1024


_NN = (((1,), (0,)), ((), ()))
_NT = (((1,), (1,)), ((), ()))
_TN = (((0,), (0,)), ((), ()))
_HI = lax.Precision.HIGHEST
_NEG = -1e30


def _cparams(sem, **kw):
    return pltpu.CompilerParams(dimension_semantics=sem, vmem_limit_bytes=VMEM_LIMIT, **kw)


PACK_COLS = 2048


def _view_shape(view):
    kind, _, r = view
    return (4 * r, PACK_COLS) if kind == "row" else (r, 4 * PACK_COLS)


def _view_spec(view, rb, cb, row_of, col_of):
    kind, off, r = view
    assert off % rb == 0 and r % rb == 0 and PACK_COLS % cb == 0, (view, rb, cb)
    if kind == "row":
        nrb = r // rb
        return pl.BlockSpec((None, rb, cb), lambda i, j, k: (row_of(i, j, k) // nrb,
                                                             off // rb + row_of(i, j, k) % nrb, col_of(i, j, k)))
    ncb = PACK_COLS // cb
    return pl.BlockSpec((None, rb, cb), lambda i, j, k: (col_of(i, j, k) // ncb,
                                                         off // rb + row_of(i, j, k), col_of(i, j, k) % ncb))


def _mm(a, b, *, dims="nn", epilogue=None, extras=(), out_dtypes=(F32,), bm=1024, bn=1024, bk=2048, name,
        b_view=None, out_into=None, out_view=None):
    b_shape = _view_shape(b_view) if b_view is not None else b.shape
    if dims == "tn":
        (K, M), (K2, N) = a.shape, b_shape
    elif dims == "nt":
        (M, K), (N, K2) = a.shape, b_shape
    else:
        (M, K), (K2, N) = a.shape, b_shape
    assert K == K2, (a.shape, b_shape, dims)
    if b_view is not None and dims == "nt":
        bn = min(bn, b_view[2])
    if b_view is not None and dims != "nt":
        bk = min(bk, b_view[2])
    if out_view is not None:
        bm = min(bm, out_view[2])
    bm, bn, bk = min(bm, M), min(bn, N), min(bk, K)
    assert M % bm == 0 and N % bn == 0 and K % bk == 0, (M, N, K, bm, bn, bk)
    nk = K // bk
    ii, jj, kk = (lambda i, j, k: i), (lambda i, j, k: j), (lambda i, j, k: k)
    if dims == "tn":
        a_spec = pl.BlockSpec((bk, bm), lambda i, j, k: (k, i))
        dn = _TN
    else:
        a_spec = pl.BlockSpec((bm, bk), lambda i, j, k: (i, k))
        dn = _NT if dims == "nt" else _NN
    if dims == "nt":
        b_spec = _view_spec(b_view, bn, bk, jj, kk) if b_view else pl.BlockSpec((bn, bk), lambda i, j, k: (j, k))
    else:
        b_spec = _view_spec(b_view, bk, bn, kk, jj) if b_view else pl.BlockSpec((bk, bn), lambda i, j, k: (k, j))
    e_specs = []
    for e in extras:
        if e.shape == (M, N):
            e_specs.append(pl.BlockSpec((bm, bn), lambda i, j, k: (i, j)))
        else:
            assert e.shape == (1, N), e.shape
            e_specs.append(pl.BlockSpec((1, bn), lambda i, j, k: (0, j)))
    ne, no = len(extras), len(out_dtypes)
    if epilogue is None:
        epilogue = lambda acc: (acc,)
    into = out_into is not None

    def body(a_ref, b_ref, *rest):
        e_refs, rest = rest[:ne], rest[ne + (1 if into else 0):]
        o_refs = rest[:no]
        part = lax.dot_general(a_ref[...].astype(BF16), b_ref[...].astype(BF16), dn, preferred_element_type=F32)

        def finish(total):
            outs = epilogue(total, *[e[...] for e in e_refs])
            for o, v in zip(o_refs, outs):
                o[...] = v.astype(o.dtype)

        if nk == 1:
            finish(part)
            return
        acc, k = rest[no], pl.program_id(2)

        @pl.when(k == 0)
        def _():
            acc[...] = part

        @pl.when((k > 0) & (k < nk - 1))
        def _():
            acc[...] += part

        @pl.when(k == nk - 1)
        def _():
            finish(acc[...] + part)

    if into:
        assert no == 1 and _view_shape(out_view) == (M, N), (out_view, M, N)
        out_specs = [_view_spec(out_view, bm, bn, ii, jj)]
        out_shape = [jax.ShapeDtypeStruct(out_into.shape, out_into.dtype)]
        extra_in, extra_specs, alias = [out_into], [pl.BlockSpec(memory_space=pl.ANY)], {2 + ne: 0}
    else:
        out_specs = [pl.BlockSpec((bm, bn), lambda i, j, k: (i, j)) for _ in out_dtypes]
        out_shape = [jax.ShapeDtypeStruct((M, N), dt) for dt in out_dtypes]
        extra_in, extra_specs, alias = [], [], {}
    outs = pl.pallas_call(
        body, name=name,
        grid=(M // bm, N // bn, nk),
        in_specs=[a_spec, b_spec] + e_specs + extra_specs,
        out_specs=out_specs, out_shape=out_shape,
        scratch_shapes=[pltpu.VMEM((bm, bn), F32)] if nk > 1 else [],
        input_output_aliases=alias,
        compiler_params=_cparams(("parallel", "parallel", "arbitrary")),
    )(a, b, *extras, *extra_in)
    return outs[0] if no == 1 else tuple(outs)


def _rowwise(fn, tiled, bcast, outs, sums=(), *, rows=256, name):
    S = tiled[0].shape[0]
    rows = min(rows, S)
    assert S % rows == 0
    nt, nb, no, ns = len(tiled), len(bcast), len(outs), len(sums)

    def body(*refs):
        t_refs, b_refs = refs[:nt], refs[nt:nt + nb]
        o_refs, s_refs = refs[nt + nb:nt + nb + no], refs[nt + nb + no:]
        res = fn(*[r[...] for r in t_refs], *[r[...] for r in b_refs])
        res = res if isinstance(res, (tuple, list)) else (res,)
        for o, v in zip(o_refs, res[:no]):
            o[...] = v.astype(o.dtype)
        if ns:
            @pl.when(pl.program_id(0) == 0)
            def _():
                for s in s_refs:
                    s[...] = jnp.zeros_like(s)
            for s, v in zip(s_refs, res[no:]):
                s[...] += v

    res = pl.pallas_call(
        body, name=name,
        grid=(S // rows,),
        in_specs=[pl.BlockSpec((rows, t.shape[1]), lambda i: (i, 0)) for t in tiled]
        + [pl.BlockSpec(b.shape, lambda i, nd=b.ndim: (0,) * nd) for b in bcast],
        out_specs=[pl.BlockSpec((rows, c), lambda i: (i, 0)) for c, _ in outs]
        + [pl.BlockSpec((1, c), lambda i: (0, 0)) for c in sums],
        out_shape=[jax.ShapeDtypeStruct((S, c), dt) for c, dt in outs]
        + [jax.ShapeDtypeStruct((1, c), F32) for c in sums],
        compiler_params=_cparams(("arbitrary",)),
    )(*tiled, *bcast)
    return res[0] if len(res) == 1 else tuple(res)


def _rms(x, g):
    return x * lax.rsqrt(jnp.mean(x * x, axis=-1, keepdims=True) + NORM_EPS) * g


def _mxu(a, b, dn):
    if a.ndim == 3:
        ((ca,), (cb,)), _ = dn
        dn = (((ca + 1,), (cb + 1,)), ((0,), (0,)))
    return lax.dot_general(a, b, dn, preferred_element_type=F32)


def _split(a):
    hi = a.astype(BF16)
    return hi, (a - hi.astype(F32)).astype(BF16)


def _passes(a, b, dn, three):
    if not three:
        return _mxu(a.astype(BF16), b.astype(BF16), dn)
    (ah, al), (bh, bl) = _split(a), _split(b)
    return _mxu(ah, bh, dn) + (_mxu(ah, bl, dn) + _mxu(al, bh, dn))


def _dot_grads(a, b, g, dn, three):
    if dn == _NN:
        return _passes(g, b, _NT, three), _passes(a, g, _TN, three)
    if dn == _NT:
        return _passes(g, b, _NN, three), _passes(g, a, _TN, three)
    assert dn == _TN
    return _passes(b, g, _NT, three), _passes(a, g, _NN, three)


@functools.partial(jax.custom_vjp, nondiff_argnums=(2,))
def _dot(a, b, dn=_NN):
    return _passes(a, b, dn, False)


_dot.defvjp(lambda a, b, dn: (_passes(a, b, dn, False), (a, b)),
            lambda dn, res, g: _dot_grads(res[0], res[1], g, dn, False))


@functools.partial(jax.custom_vjp, nondiff_argnums=(2,))
def _dot3(a, b, dn=_NN):
    return _passes(a, b, dn, True)


_dot3.defvjp(lambda a, b, dn: (_passes(a, b, dn, True), (a, b)),
             lambda dn, res, g: _dot_grads(res[0], res[1], g, dn, True))


def _tri_times(x, dn):
    tri = _tril(x.shape[-2]).astype(BF16)
    if x.ndim == 3:
        tri = jnp.broadcast_to(tri, (x.shape[0],) + tri.shape)
    hi, lo = _split(x)
    lo2 = (x - hi.astype(F32) - lo.astype(F32)).astype(BF16)
    return _mxu(tri, hi, dn) + (_mxu(tri, lo, dn) + _mxu(tri, lo2, dn))


@jax.custom_vjp
def _cumsum_rows(x):
    return _tri_times(x, _NN)


_cumsum_rows.defvjp(lambda x: (_tri_times(x, _NN), None), lambda _, g: (_tri_times(g, _TN),))


def _iota(shape, dim):
    return lax.broadcasted_iota(jnp.int32, shape, dim)


def _tril(n, strict=False):
    r, c = _iota((n, n), 0), _iota((n, n), 1)
    return (r > c) if strict else (r >= c)


@functools.partial(jax.custom_vjp, nondiff_argnums=(1,))
def _roll_rows(x, r):
    return pltpu.roll(x, r, 0)


def _roll_rows_fwd(x, r):
    return pltpu.roll(x, r, 0), None


def _roll_rows_bwd(r, _, g):
    return (pltpu.roll(g, (g.shape[0] - r) % g.shape[0], 0),)


_roll_rows.defvjp(_roll_rows_fwd, _roll_rows_bwd)


def _head_norm_gate(o, gn, gate):
    return _rms(o, gn) * jax.nn.silu(gate)


_SUB = 16
_HPS = 8


def _hgrn_chunk(q, fpre, v, gate, lb, gna, st):
    c = q.shape[0]
    forget = lb + (1.0 - lb) * jax.nn.sigmoid(fpre)
    k = 1.0 - forget
    logf = jnp.log(forget)
    cum = _cumsum_rows(logf)
    cum_end = jnp.sum(logf, axis=0, keepdims=True)
    o = _dot(q * jnp.exp(cum), st, _NT)
    st_new = st * jnp.exp(cum_end) + _dot(v, k * jnp.exp(cum_end - cum), _TN)
    t = _iota((c, 1), 0)
    s_off = jnp.zeros((c, c), F32)
    for i in range(1, c // _SUB):
        before = t < i * _SUB
        c_i = jnp.sum(jnp.where(before, logf, 0.0), axis=0, keepdims=True)
        in_blk = (t >= i * _SUB) & (t < (i + 1) * _SUB)
        qi = jnp.where(in_blk, q * jnp.exp(jnp.minimum(cum - c_i, 0.0)), 0.0)
        ki = jnp.where(before, k * jnp.exp(jnp.minimum(c_i - cum, 0.0)), 0.0)
        s_off = s_off + _dot(qi, ki, _NT)
    o = o + _dot(s_off, v)
    tmod = t % _SUB
    for r in range(_SUB):
        kr, cr, vr = (k, cum, v) if r == 0 else (_roll_rows(k, r), _roll_rows(cum, r), _roll_rows(v, r))
        w = q * kr * jnp.exp(jnp.where(tmod >= r, cum - cr, _NEG))
        o = o + jnp.sum(w, axis=1, keepdims=True) * vr
    return _head_norm_gate(o, gna, gate), st_new


def _hgrn_fwd(proj, lb, gna):
    S = proj.shape[0]
    n_chunks, heads = S // CHUNK, 8

    def body(q_ref, f_ref, v_ref, g_ref, lb_ref, gna_ref, o_ref, st_out, st):
        n, hb = pl.program_id(0), pl.program_id(1)

        @pl.when(n == 0)
        def _():
            st[hb] = jnp.zeros((_HPS, HEAD, HEAD), F32)

        sls = [slice(hh * HEAD, (hh + 1) * HEAD) for hh in range(_HPS)]
        st0 = [st[hb, hh] for hh in range(_HPS)]
        ins = [(q_ref[:, sl], f_ref[:, sl], v_ref[:, sl], g_ref[:, sl], lb_ref[:, sl]) for sl in sls]
        gna = gna_ref[...]
        res = [_hgrn_chunk(*ins[hh], gna, st0[hh]) for hh in range(_HPS)]
        for hh, sl in enumerate(sls):
            st_out[hh] = st0[hh]
            o_ref[:, sl] = res[hh][0].astype(o_ref.dtype)
            st[hb, hh] = res[hh][1]

    wide = _HPS * HEAD
    sec = lambda s: pl.BlockSpec((CHUNK, wide), lambda n, h, s=s: (n, (8 // _HPS) * s + h))
    return pl.pallas_call(
        body, name="hgrn_fwd", grid=(n_chunks, heads // _HPS),
        in_specs=[sec(0), sec(1), sec(2), sec(3),
                  pl.BlockSpec((1, wide), lambda n, h: (0, h)), pl.BlockSpec((1, HEAD), lambda n, h: (0, 0))],
        out_specs=[pl.BlockSpec((CHUNK, wide), lambda n, h: (n, h)),
                   pl.BlockSpec((None, _HPS, HEAD, HEAD), lambda n, h: (n, h, 0, 0))],
        out_shape=[jax.ShapeDtypeStruct((S, 1024), BF16),
                   jax.ShapeDtypeStruct((n_chunks, heads, HEAD, HEAD), F32)],
        scratch_shapes=[pltpu.VMEM((heads // _HPS, _HPS, HEAD, HEAD), F32)],
        compiler_params=_cparams(("arbitrary", "arbitrary")),
    )(proj, proj, proj, proj, lb, gna)


def _hgrn_bwd(proj, lb, gna, states, d_o):
    S = proj.shape[0]
    n_chunks, heads = S // CHUNK, 8

    def body(q_ref, f_ref, v_ref, g_ref, lb_ref, gna_ref, st_ref, do_ref,
             dq_ref, df_ref, dv_ref, dg_ref, dlb_ref, dgna_ref, dst):
        n, hb = pl.program_id(0), pl.program_id(1)

        @pl.when(n == 0)
        def _():
            dst[hb] = jnp.zeros((_HPS, HEAD, HEAD), F32)
            dlb_ref[hb] = jnp.zeros((_HPS, 1, HEAD), F32)

        @pl.when((n == 0) & (hb == 0))
        def _():
            dgna_ref[...] = jnp.zeros_like(dgna_ref)

        sls = [slice(hh * HEAD, (hh + 1) * HEAD) for hh in range(_HPS)]
        gna = gna_ref[...]
        ins = [(q_ref[:, sl], f_ref[:, sl], v_ref[:, sl], g_ref[:, sl], lb_ref[:, sl], gna, st_ref[hh])
               for hh, sl in enumerate(sls)]
        cts = [(do_ref[:, sl], dst[hb, hh]) for hh, sl in enumerate(sls)]
        grads = [jax.vjp(_hgrn_chunk, *ins[hh])[1](cts[hh]) for hh in range(_HPS)]
        dgna_sum = jnp.zeros((1, HEAD), F32)
        for hh, sl in enumerate(sls):
            dq, df, dv, dg, dlb, dgna, dst0 = grads[hh]
            dq_ref[:, sl] = dq.astype(dq_ref.dtype)
            df_ref[:, sl] = df.astype(df_ref.dtype)
            dv_ref[:, sl] = dv.astype(dv_ref.dtype)
            dg_ref[:, sl] = dg.astype(dg_ref.dtype)
            dlb_ref[hb, hh] += dlb
            dgna_sum = dgna_sum + dgna
            dst[hb, hh] = dst0
        dgna_ref[...] += dgna_sum

    rev = lambda n: n_chunks - 1 - n
    wide = _HPS * HEAD
    sec = lambda s: pl.BlockSpec((CHUNK, wide), lambda n, h, s=s: (rev(n), (8 // _HPS) * s + h))
    out = pl.BlockSpec((CHUNK, wide), lambda n, h: (rev(n), h))
    return pl.pallas_call(
        body, name="hgrn_bwd", grid=(n_chunks, heads // _HPS),
        in_specs=[sec(0), sec(1), sec(2), sec(3),
                  pl.BlockSpec((1, wide), lambda n, h: (0, h)), pl.BlockSpec((1, HEAD), lambda n, h: (0, 0)),
                  pl.BlockSpec((None, _HPS, HEAD, HEAD), lambda n, h: (rev(n), h, 0, 0)),
                  pl.BlockSpec((CHUNK, wide), lambda n, h: (rev(n), h))],
        out_specs=[out, out, out, out,
                   pl.BlockSpec((heads // _HPS, _HPS, 1, HEAD), lambda n, h: (0, 0, 0, 0)),
                   pl.BlockSpec((1, HEAD), lambda n, h: (0, 0))],
        out_shape=[jax.ShapeDtypeStruct((S, 1024), BF16)] * 4
        + [jax.ShapeDtypeStruct((heads // _HPS, _HPS, 1, HEAD), F32), jax.ShapeDtypeStruct((1, HEAD), F32)],
        scratch_shapes=[pltpu.VMEM((heads // _HPS, _HPS, HEAD, HEAD), F32)],
        compiler_params=_cparams(("arbitrary", "arbitrary")),
    )(proj, proj, proj, proj, lb, gna, states, d_o)


def _l2n(t):
    return t * lax.rsqrt(jnp.sum(t * t, axis=-1, keepdims=True) + NORM_EPS)


def _unit_lower_inverse(low):
    c = low.shape[-1]
    inv = (_iota((c, c), 0) == _iota((c, c), 1)).astype(F32) - low
    p = low
    span = 2
    while span < c:
        p = _dot3(p, p)
        inv = inv + _dot3(inv, p)
        span *= 2
    return inv


def _gdn_chunk(qc, kc, v, z, tail, h0, arow, dtrow, gnb, st):
    nb, c = qc.shape[0], qc.shape[1]
    head = h0 + _iota((nb, c, HEAD), 0)
    lane = _iota((nb, c, HEAD), 2)
    la_all = arow * jax.nn.softplus(tail + dtrow)
    la = jnp.sum(jnp.where(lane == head, la_all[None], 0.0), axis=2, keepdims=True)
    beta = jnp.sum(jnp.where(lane == head + 8, jax.nn.sigmoid(tail)[None], 0.0), axis=2, keepdims=True)
    la_b = jnp.broadcast_to(la, (nb, c, HEAD))
    cum = _cumsum_rows(la_b)
    cmat = _cumsum_rows(jnp.broadcast_to(la, (nb, c, c)))
    cum_end = jnp.sum(la_b, axis=1, keepdims=True)
    decay = jnp.exp(jnp.where(_tril(c), cmat - jnp.swapaxes(cmat, 1, 2), _NEG))
    q = _l2n(qc) * (HEAD ** -0.5)
    k = _l2n(kc)
    k_beta = k * beta
    low = jnp.where(_tril(c, strict=True), _dot(k_beta, k, _NT) * decay, 0.0)
    inv = _unit_lower_inverse(low)
    u, w = _dot(inv, v * beta), _dot(inv, k_beta * jnp.exp(cum))
    intra = _dot(q, k, _NT) * decay
    v_new = u - _dot(w, st, _NT)
    o = _dot(q * jnp.exp(cum), st, _NT) + _dot(intra, v_new)
    st_new = st * jnp.exp(cum_end) + _dot(v_new, k * jnp.exp(cum_end - cum), _TN)
    return _head_norm_gate(o, gnb, z), st_new


def _gdn_specs(n_of):
    wide = _HPS * HEAD
    qkv = lambda s: pl.BlockSpec((CHUNK, wide), lambda n, h, s=s: (n_of(n), (8 // _HPS) * s + h))
    return [qkv(0), qkv(1), qkv(2),
            pl.BlockSpec((CHUNK, wide), lambda n, h: (n_of(n), 56 // _HPS + h)),
            pl.BlockSpec((CHUNK, HEAD), lambda n, h: (n_of(n), 64)),
            pl.BlockSpec((1, HEAD), lambda n, h: (0, 0)), pl.BlockSpec((1, HEAD), lambda n, h: (0, 0)),
            pl.BlockSpec((1, HEAD), lambda n, h: (0, 0))]


def _gdn_fwd(act, proj, arow, dtrow, gnb):
    S = act.shape[0]
    n_chunks, heads = S // CHUNK, 8

    def body(q_ref, k_ref, v_ref, z_ref, t_ref, a_ref, dt_ref, gnb_ref, o_ref, st_out, st):
        n, hb = pl.program_id(0), pl.program_id(1)

        @pl.when(n == 0)
        def _():
            st[hb] = jnp.zeros((_HPS, HEAD, HEAD), F32)

        sls = [slice(hh * HEAD, (hh + 1) * HEAD) for hh in range(_HPS)]
        heads_of = lambda ref: jnp.stack([ref[:, sl] for sl in sls])
        st0 = st[hb]
        o, st1 = _gdn_chunk(heads_of(q_ref), heads_of(k_ref), heads_of(v_ref), heads_of(z_ref), t_ref[...], hb * _HPS,
                            a_ref[...], dt_ref[...], gnb_ref[...], st0)
        st_out[...] = st0
        st[hb] = st1
        for hh, sl in enumerate(sls):
            o_ref[:, sl] = o[hh].astype(o_ref.dtype)

    return pl.pallas_call(
        body, name="gdn_fwd", grid=(n_chunks, heads // _HPS),
        in_specs=_gdn_specs(lambda n: n),
        out_specs=[pl.BlockSpec((CHUNK, _HPS * HEAD), lambda n, h: (n, h)),
                   pl.BlockSpec((None, _HPS, HEAD, HEAD), lambda n, h: (n, h, 0, 0))],
        out_shape=[jax.ShapeDtypeStruct((S, 1024), BF16),
                   jax.ShapeDtypeStruct((n_chunks, heads, HEAD, HEAD), F32)],
        scratch_shapes=[pltpu.VMEM((heads // _HPS, _HPS, HEAD, HEAD), F32)],
        compiler_params=_cparams(("arbitrary", "arbitrary")),
    )(act, act, act, proj, proj, arow, dtrow, gnb)


def _gdn_bwd(act, proj, arow, dtrow, gnb, states, d_o):
    S = act.shape[0]
    n_chunks, heads = S // CHUNK, 8

    def body(q_ref, k_ref, v_ref, z_ref, t_ref, a_ref, dt_ref, gnb_ref, st_ref, do_ref,
             dq_ref, dk_ref, dv_ref, dz_ref, dt_out, da_ref, ddt_ref, dgnb_ref, dst, dtail_acc):
        n, hb = pl.program_id(0), pl.program_id(1)
        n_hb = heads // _HPS

        @pl.when(n == 0)
        def _():
            dst[hb] = jnp.zeros((_HPS, HEAD, HEAD), F32)

        @pl.when((n == 0) & (hb == 0))
        def _():
            da_ref[...] = jnp.zeros_like(da_ref)
            ddt_ref[...] = jnp.zeros_like(ddt_ref)
            dgnb_ref[...] = jnp.zeros_like(dgnb_ref)

        sls = [slice(hh * HEAD, (hh + 1) * HEAD) for hh in range(_HPS)]
        heads_of = lambda ref: jnp.stack([ref[:, sl] for sl in sls])
        fn = lambda qc, kc, v, z, tail, arow, dtrow, gnb, st: _gdn_chunk(qc, kc, v, z, tail, hb * _HPS, arow, dtrow,
                                                                        gnb, st)
        _, vjp = jax.vjp(fn, heads_of(q_ref), heads_of(k_ref), heads_of(v_ref), heads_of(z_ref), t_ref[...],
                         a_ref[...], dt_ref[...], gnb_ref[...], st_ref[...])
        dq, dk, dv, dz, dtail_sum, da_sum, ddt_sum, dgnb_sum, dst0 = vjp((heads_of(do_ref), dst[hb]))
        dst[hb] = dst0
        for hh, sl in enumerate(sls):
            dq_ref[:, sl] = dq[hh]
            dk_ref[:, sl] = dk[hh]
            dv_ref[:, sl] = dv[hh]
            dz_ref[:, sl] = dz[hh].astype(dz_ref.dtype)

        @pl.when(hb == 0)
        def _():
            dtail_acc[...] = dtail_sum

        @pl.when(hb > 0)
        def _():
            dtail_acc[...] += dtail_sum

        @pl.when(hb == n_hb - 1)
        def _():
            dt_out[...] = dtail_acc[...].astype(dt_out.dtype)

        da_ref[...] += da_sum
        ddt_ref[...] += ddt_sum
        dgnb_ref[...] += dgnb_sum

    rev = lambda n: n_chunks - 1 - n
    out = pl.BlockSpec((CHUNK, _HPS * HEAD), lambda n, h: (rev(n), h))
    row = pl.BlockSpec((1, HEAD), lambda n, h: (0, 0))
    return pl.pallas_call(
        body, name="gdn_bwd", grid=(n_chunks, heads // _HPS),
        in_specs=_gdn_specs(rev)
        + [pl.BlockSpec((None, _HPS, HEAD, HEAD), lambda n, h: (rev(n), h, 0, 0)),
           pl.BlockSpec((CHUNK, _HPS * HEAD), lambda n, h: (rev(n), 8 // _HPS + h))],
        out_specs=[out, out, out, out, pl.BlockSpec((CHUNK, HEAD), lambda n, h: (rev(n), 0)), row, row, row],
        out_shape=[jax.ShapeDtypeStruct((S, 1024), F32)] * 3
        + [jax.ShapeDtypeStruct((S, 1024), BF16), jax.ShapeDtypeStruct((S, HEAD), BF16)]
        + [jax.ShapeDtypeStruct((1, HEAD), F32)] * 3,
        scratch_shapes=[pltpu.VMEM((heads // _HPS, _HPS, HEAD, HEAD), F32), pltpu.VMEM((CHUNK, HEAD), F32)],
        compiler_params=_cparams(("arbitrary", "arbitrary")),
    )(act, act, act, proj, proj, arow, dtrow, gnb, states, d_o)


def _conv_silu(x, w):
    t = _iota((x.shape[0], 1), 0)
    tap = _iota(w.shape, 0)
    y = jnp.zeros_like(x)
    for r in range(4):
        w_r = jnp.sum(jnp.where(tap == 3 - r, w, 0.0), axis=0, keepdims=True)
        y = y + (x if r == 0 else jnp.where(t >= r, _roll_rows(x, r), 0.0)) * w_r
    return jax.nn.silu(y)


_CONV_COLS = 128


def _conv_fwd(proj, conv_w):
    S = proj.shape[0]
    nb = 3072 // _CONV_COLS
    off = 4096 // _CONV_COLS

    def body(x_ref, w_ref, o_ref):
        o_ref[...] = _conv_silu(x_ref[...], w_ref[...])

    return pl.pallas_call(
        body, name="conv_fwd", grid=(nb,),
        in_specs=[pl.BlockSpec((S, _CONV_COLS), lambda j: (0, off + j)), pl.BlockSpec((4, _CONV_COLS), lambda j: (0, j))],
        out_specs=pl.BlockSpec((S, _CONV_COLS), lambda j: (0, j)),
        out_shape=jax.ShapeDtypeStruct((S, 3072), F32),
        compiler_params=_cparams(("parallel",)),
    )(proj, conv_w)


def _conv_bwd(proj, conv_w, dq, dk, dv):
    S = proj.shape[0]
    nb = 3072 // _CONV_COLS
    off = 4096 // _CONV_COLS
    per = 1024 // _CONV_COLS

    def body(x_ref, w_ref, dq_ref, dk_ref, dv_ref, dx_ref, dw_ref):
        j = pl.program_id(0)
        _, vjp = jax.vjp(_conv_silu, x_ref[...], w_ref[...])
        d = jnp.where(j < per, dq_ref[...], jnp.where(j < 2 * per, dk_ref[...], dv_ref[...]))
        dx, dw = vjp(d)
        dx_ref[...] = dx.astype(dx_ref.dtype)
        dw_ref[...] = dw

    dsp = lambda s: pl.BlockSpec((S, _CONV_COLS), lambda j, s=s: (0, jnp.clip(j - s * per, 0, per - 1)))
    return pl.pallas_call(
        body, name="conv_bwd", grid=(nb,),
        in_specs=[pl.BlockSpec((S, _CONV_COLS), lambda j: (0, off + j)), pl.BlockSpec((4, _CONV_COLS), lambda j: (0, j)),
                  dsp(0), dsp(1), dsp(2)],
        out_specs=[pl.BlockSpec((S, _CONV_COLS), lambda j: (0, j)), pl.BlockSpec((4, _CONV_COLS), lambda j: (0, j))],
        out_shape=[jax.ShapeDtypeStruct((S, 3072), BF16), jax.ShapeDtypeStruct((4, 3072), F32)],
        compiler_params=_cparams(("parallel",)),
    )(proj, conv_w, dq, dk, dv)


_S5_T = 512
_S5_L = 512
_S5_NB = 16
_S5_U = 4
_S5_UB = 3


def _s5_tile(xr, xi, pw_r, pw_i, cr, ci, reverse):
    t8 = _iota((8, 1), 0)
    for sh in (1, 2, 4):
        row = (8 - sh) if reverse else (sh - 1)
        ar, ai = pw_r[row:row + 1, :], pw_i[row:row + 1, :]
        if reverse:
            keep, amt = t8 < 8 - sh, 8 - sh
        else:
            keep, amt = t8 >= sh, sh
        sr = jnp.where(keep, pltpu.roll(xr, amt, 0), 0.0)
        si = jnp.where(keep, pltpu.roll(xi, amt, 0), 0.0)
        xr, xi = xr + ar * sr - ai * si, xi + ar * si + ai * sr
    xr, xi = xr + pw_r * cr - pw_i * ci, xi + pw_r * ci + pw_i * cr
    return xr, xi


def _s5_fwd(u, bre, bim, cre, cim, pw_r, pw_i, dskip):
    S = u.shape[0]
    T = min(_S5_T, S)
    nt = S // T

    def body(u_ref, bre_ref, bim_ref, cre_ref, cim_ref, pr_ref, pi_ref, d_ref, y_ref, xr_ref, xi_ref,
             bu_r, bu_i, car_r, car_i):
        @pl.when(pl.program_id(1) == 0)
        def _():
            car_r[...] = jnp.zeros_like(car_r)
            car_i[...] = jnp.zeros_like(car_i)

        uu = u_ref[...]
        bu_r[...] = _dot(uu, bre_ref[...])
        bu_i[...] = _dot(uu, bim_ref[...])
        pw_r, pw_i = pr_ref[...], pi_ref[...]

        def tiles(i, carry):
            ins = []
            for k in range(_S5_U):
                r0 = pl.multiple_of((i * _S5_U + k) * 8, 8)
                ins.append((r0, bu_r[pl.ds(r0, 8), :], bu_i[pl.ds(r0, 8), :]))
            outs = []
            for r0, br, bi in ins:
                xr, xi = _s5_tile(br, bi, pw_r, pw_i, carry[0], carry[1], False)
                carry = (xr[7:8, :], xi[7:8, :])
                outs.append((r0, xr, xi))
            for r0, xr, xi in outs:
                xr_ref[pl.ds(r0, 8), :] = xr
                xi_ref[pl.ds(r0, 8), :] = xi
            return carry

        cr, ci = lax.fori_loop(0, T // (8 * _S5_U), tiles, (car_r[...], car_i[...]))
        car_r[...] = cr
        car_i[...] = ci
        y_ref[...] = _dot(xr_ref[...], cre_ref[...]) - _dot(xi_ref[...], cim_ref[...]) + d_ref[...] * uu

    blk3 = lambda a, b: pl.BlockSpec((None, a, b), lambda j, t: (j, 0, 0))
    return pl.pallas_call(
        body, name="s5_fwd", grid=(_S5_NB, nt),
        in_specs=[pl.BlockSpec((T, HEAD), lambda j, t: (t, j)),
                  blk3(HEAD, _S5_L), blk3(HEAD, _S5_L), blk3(_S5_L, HEAD), blk3(_S5_L, HEAD),
                  blk3(8, _S5_L), blk3(8, _S5_L), pl.BlockSpec((1, HEAD), lambda j, t: (0, j))],
        out_specs=[pl.BlockSpec((T, HEAD), lambda j, t: (t, j)),
                   pl.BlockSpec((T, _S5_L), lambda j, t: (t, j)), pl.BlockSpec((T, _S5_L), lambda j, t: (t, j))],
        out_shape=[jax.ShapeDtypeStruct((S, D_MODEL), F32),
                   jax.ShapeDtypeStruct((S, _S5_NB * _S5_L), F32), jax.ShapeDtypeStruct((S, _S5_NB * _S5_L), F32)],
        scratch_shapes=[pltpu.VMEM((T, _S5_L), F32), pltpu.VMEM((T, _S5_L), F32),
                        pltpu.VMEM((1, _S5_L), F32), pltpu.VMEM((1, _S5_L), F32)],
        compiler_params=_cparams(("parallel", "arbitrary")),
    )(u, bre, bim, cre, cim, pw_r, pw_i, dskip)


def _s5_bwd(dy, u, xre, xim, bre, bim, cre, cim, qw_r, qw_i, dskip):
    S = u.shape[0]
    T = min(_S5_T, S)
    nt = S // T
    nt8 = T // 8

    def body(dy_ref, u_ref, xr_ref, xi_ref, xpr_ref, xpi_ref, bre_ref, bim_ref, cre_ref, cim_ref, qr_ref, qi_ref,
             d_ref, du_ref, dbr_ref, dbi_ref, dcr_ref, dci_ref, dlr_ref, dli_ref, dd_ref,
             g_r, g_i, car_r, car_i):
        t = pl.program_id(1)

        @pl.when(t == 0)
        def _():
            car_r[...] = jnp.zeros_like(car_r)
            car_i[...] = jnp.zeros_like(car_i)
            for r in (dbr_ref, dbi_ref, dcr_ref, dci_ref, dlr_ref, dli_ref, dd_ref):
                r[...] = jnp.zeros_like(r)

        dyy, uu = dy_ref[...], u_ref[...]
        g_r[...] = _dot(dyy, cre_ref[...], _NT)
        g_i[...] = -_dot(dyy, cim_ref[...], _NT)
        qw_r, qw_i = qr_ref[...], qi_ref[...]
        t8 = _iota((8, 1), 0)
        first = t == nt - 1

        def load(r0, prev_r, prev_i):
            rows = pl.ds(r0, 8)
            return r0, g_r[rows, :], g_i[rows, :], xr_ref[rows, :], xi_ref[rows, :], prev_r, prev_i

        def run(loaded, carry, acc):
            done = []
            for r0, dr, di, xr, xi, prev_r, prev_i in loaded:
                gr, gi = _s5_tile(dr, di, qw_r, qw_i, carry[0], carry[1], True)
                carry = (gr[0:1, :], gi[0:1, :])
                xpr = jnp.where(t8 >= 1, pltpu.roll(xr, 1, 0), prev_r)
                xpi = jnp.where(t8 >= 1, pltpu.roll(xi, 1, 0), prev_i)
                acc = (acc[0] + gr * xpr + gi * xpi, acc[1] + gi * xpr - gr * xpi)
                done.append((r0, gr, gi))
            for r0, gr, gi in done:
                g_r[pl.ds(r0, 8), :] = gr
                g_i[pl.ds(r0, 8), :] = gi
            return carry, acc

        def step(ii, state):
            loaded = []
            for k in range(_S5_UB):
                idx = nt8 - 1 - (ii * _S5_UB + k)
                r0 = pl.multiple_of(idx * 8, 8)
                p0 = pl.multiple_of((idx - 1) * 8, 8)
                loaded.append(load(r0, xr_ref[pl.ds(p0, 8), :][7:8, :], xi_ref[pl.ds(p0, 8), :][7:8, :]))
            return run(loaded, *state)

        zero = jnp.zeros((8, _S5_L), F32)
        assert (nt8 - 1) % _S5_UB == 0
        state = lax.fori_loop(0, (nt8 - 1) // _S5_UB, step, ((car_r[...], car_i[...]), (zero, zero)))
        prev_r = jnp.where(first, 0.0, xpr_ref[...][7:8, :])
        prev_i = jnp.where(first, 0.0, xpi_ref[...][7:8, :])
        (cr, ci), (ar, ai) = run([load(0, prev_r, prev_i)], *state)
        car_r[...] = cr
        car_i[...] = ci
        dlr_ref[...] += ar
        dli_ref[...] += ai
        gr, gi = g_r[...], g_i[...]
        du_ref[...] = _dot(gr, bre_ref[...], _NT) + _dot(gi, bim_ref[...], _NT) + d_ref[...] * dyy
        dbr_ref[...] += _dot(uu, gr, _TN)
        dbi_ref[...] += _dot(uu, gi, _TN)
        dcr_ref[...] += _dot(xr_ref[...], dyy, _TN)
        dci_ref[...] -= _dot(xi_ref[...], dyy, _TN)
        dd_ref[...] += jnp.sum(dyy * uu, axis=0, keepdims=True)

    rev = lambda t: nt - 1 - t
    blk3 = lambda a, b: pl.BlockSpec((None, a, b), lambda j, t: (j, 0, 0))
    tl = pl.BlockSpec((T, HEAD), lambda j, t: (rev(t), j))
    xs = pl.BlockSpec((T, _S5_L), lambda j, t: (rev(t), j))
    xp = pl.BlockSpec((8, _S5_L), lambda j, t: (jnp.maximum(rev(t) * nt8 - 1, 0), j))
    return pl.pallas_call(
        body, name="s5_bwd", grid=(_S5_NB, nt),
        in_specs=[tl, tl, xs, xs, xp, xp, blk3(HEAD, _S5_L), blk3(HEAD, _S5_L), blk3(_S5_L, HEAD), blk3(_S5_L, HEAD),
                  blk3(8, _S5_L), blk3(8, _S5_L), pl.BlockSpec((1, HEAD), lambda j, t: (0, j))],
        out_specs=[tl, blk3(HEAD, _S5_L), blk3(HEAD, _S5_L), blk3(_S5_L, HEAD), blk3(_S5_L, HEAD),
                   blk3(8, _S5_L), blk3(8, _S5_L), pl.BlockSpec((1, HEAD), lambda j, t: (0, j))],
        out_shape=[jax.ShapeDtypeStruct((S, D_MODEL), F32),
                   jax.ShapeDtypeStruct((_S5_NB, HEAD, _S5_L), F32), jax.ShapeDtypeStruct((_S5_NB, HEAD, _S5_L), F32),
                   jax.ShapeDtypeStruct((_S5_NB, _S5_L, HEAD), F32), jax.ShapeDtypeStruct((_S5_NB, _S5_L, HEAD), F32),
                   jax.ShapeDtypeStruct((_S5_NB, 8, _S5_L), F32), jax.ShapeDtypeStruct((_S5_NB, 8, _S5_L), F32),
                   jax.ShapeDtypeStruct((1, D_MODEL), F32)],
        scratch_shapes=[pltpu.VMEM((T, _S5_L), F32), pltpu.VMEM((T, _S5_L), F32),
                        pltpu.VMEM((1, _S5_L), F32), pltpu.VMEM((1, _S5_L), F32)],
        compiler_params=_cparams(("parallel", "arbitrary")),
    )(dy, u, xre, xim, xre, xim, bre, bim, cre, cim, qw_r, qw_i, dskip)


def _s5_params(a_re, a_im, log_dt, b_re, b_im, c_re, c_im):
    step = jnp.exp(log_dt)[:, None]
    mag = jnp.exp(a_re * step)
    lr, li = mag * jnp.cos(a_im * step), mag * jnp.sin(a_im * step)
    den = a_re * a_re + a_im * a_im
    nr, ni = lr - 1.0, li
    kr, ki = (nr * a_re + ni * a_im) / den, (ni * a_re - nr * a_im) / den
    bbr = kr[..., None] * b_re - ki[..., None] * b_im
    bbi = kr[..., None] * b_im + ki[..., None] * b_re
    eye = jnp.eye(8, dtype=F32)

    def blk_b(bb):
        t = bb.reshape(_S5_NB, 8, 64, 16).transpose(0, 1, 3, 2)
        return (t[:, :, :, None, :] * eye[None, :, None, :, None]).reshape(_S5_NB, HEAD, _S5_L)

    def blk_c(cc):
        t = cc.reshape(_S5_NB, 8, 16, 64).transpose(0, 1, 3, 2)
        return (t[:, :, :, None, :] * eye[None, :, None, :, None]).reshape(_S5_NB, _S5_L, HEAD)

    return (blk_b(bbr), blk_b(bbi), blk_c(c_re), blk_c(c_im),
            lr.reshape(_S5_NB, _S5_L), li.reshape(_S5_NB, _S5_L))


def _s5_powers(lr, li):
    pr, pi = [lr], [li]
    for _ in range(7):
        pr, pi = pr + [pr[-1] * lr - pi[-1] * li], pi + [pr[-1] * li + pi[-1] * lr]
    return jnp.stack(pr, axis=1), jnp.stack(pi, axis=1)


_MESH = pl.DeviceIdType.MESH
_ANY = pl.BlockSpec(memory_space=pl.ANY)


def _place():
    x, y, c = lax.axis_index("x"), lax.axis_index("y"), lax.axis_index("c")
    return x, y, c, [(1 - x, y), (x, 1 - y), (1 - x, 1 - y)]


def _comm_call(body, arrs, out_shapes, n_remote, name):
    n = len(arrs)
    return pl.pallas_call(
        body, name=name,
        in_specs=[_ANY] * n, out_specs=[_ANY] * n, out_shape=out_shapes,
        scratch_shapes=[pltpu.SemaphoreType.DMA((n * n_remote,)), pltpu.SemaphoreType.DMA((n * n_remote,)),
                        pltpu.SemaphoreType.DMA((n,))],
    )(*arrs)


def _own_slot(shard, me_idx, name, rows=256):
    _, half, C = shard.shape
    rows = min(rows, half)
    assert half % rows == 0

    def body(me_ref, in_ref, o_ref):
        o_ref[...] = in_ref[...]

    return pl.pallas_call(
        body, name=name,
        grid_spec=pltpu.PrefetchScalarGridSpec(
            num_scalar_prefetch=1, grid=(2, half // rows),
            in_specs=[pl.BlockSpec((None, rows, C), lambda h, i, me: (h, i, 0))],
            out_specs=pl.BlockSpec((None, None, rows, C), lambda h, i, me: (me[0], h, i, 0))),
        out_shape=jax.ShapeDtypeStruct((4,) + shard.shape, shard.dtype),
        compiler_params=_cparams(("parallel", "parallel")),
    )(me_idx, shard)


def _ag_xy(bufs, name):
    n = len(bufs)

    def body(*refs):
        outs, (ssem, rsem) = refs[n:2 * n], refs[2 * n:]
        x, y, c, chips = _place()
        me, sib = 2 * x + y, (x, y, 1 - c)

        def copy(a, k, src, dst, to):
            return pltpu.make_async_remote_copy(src_ref=src, dst_ref=dst, send_sem=ssem.at[6 * a + k],
                                                recv_sem=rsem.at[6 * a + k], device_id=to, device_id_type=_MESH)

        sends = []
        for j, (px, py) in enumerate(chips):
            for a in range(n):
                cp = copy(a, j, outs[a].at[me, c], outs[a].at[me, c], (px, py, c))
                cp.start()
                sends.append(cp)
        for j, (px, py) in enumerate(chips):
            for a in range(n):
                landed = outs[a].at[2 * px + py, c]
                copy(a, j, landed, landed, (px, py, c)).wait_recv()
                fwd = copy(a, 3 + j, landed, landed, sib)
                fwd.start()
                sends.append(fwd)
        for j, (px, py) in enumerate(chips):
            for a in range(n):
                passed = outs[a].at[2 * px + py, 1 - c]
                copy(a, 3 + j, passed, passed, sib).wait_recv()
        for cp in sends:
            cp.wait_send()

    return pl.pallas_call(
        body, name=name,
        in_specs=[_ANY] * n, out_specs=[_ANY] * n,
        out_shape=[jax.ShapeDtypeStruct(b.shape, b.dtype) for b in bufs],
        input_output_aliases={a: a for a in range(n)},
        scratch_shapes=[pltpu.SemaphoreType.DMA((6 * n,)), pltpu.SemaphoreType.DMA((6 * n,))],
    )(*bufs)


def _pair_split(arrs, name):
    n = len(arrs)

    def body(*refs):
        ins, outs, (ssem, rsem, _) = refs[:n], refs[n:2 * n], refs[2 * n:]
        x, y, c, _ = _place()
        copies = []
        for a in range(n):
            for s in range(4):
                cp = pltpu.make_async_remote_copy(src_ref=ins[a].at[s, 1 - c], dst_ref=outs[a].at[s],
                                                  send_sem=ssem.at[4 * a + s], recv_sem=rsem.at[4 * a + s],
                                                  device_id=(x, y, 1 - c), device_id_type=_MESH)
                cp.start()
                copies.append(cp)
        for cp in copies:
            cp.wait()

    shapes = [jax.ShapeDtypeStruct((4,) + a.shape[2:], a.dtype) for a in arrs]
    return _comm_call(body, arrs, shapes, 4, name)


def _pair_swap(arrs, name):
    n = len(arrs)

    def body(*refs):
        ins, outs, (ssem, rsem, _) = refs[:n], refs[n:2 * n], refs[2 * n:]
        x, y, c, _ = _place()
        copies = []
        for a in range(n):
            cp = pltpu.make_async_remote_copy(src_ref=ins[a], dst_ref=outs[a], send_sem=ssem.at[a], recv_sem=rsem.at[a],
                                              device_id=(x, y, 1 - c), device_id_type=_MESH)
            cp.start()
            copies.append(cp)
        for cp in copies:
            cp.wait()

    return _comm_call(body, arrs, [jax.ShapeDtypeStruct(a.shape, a.dtype) for a in arrs], 1, name)


def _pair_sum(full, recv, c_idx, name, rows=128):
    _, _, half, C = full.shape
    rows = min(rows, half)
    nb = half // rows
    assert half % rows == 0

    def body(c_ref, a_ref, b_ref, o_ref):
        o_ref[...] = (a_ref[...].astype(F32) + b_ref[...].astype(F32)).astype(o_ref.dtype)

    return pl.pallas_call(
        body, name=name,
        grid_spec=pltpu.PrefetchScalarGridSpec(
            num_scalar_prefetch=1, grid=(4, nb),
            in_specs=[pl.BlockSpec((None, None, rows, C), lambda s, i, c: (s, c[0], i, 0)),
                      pl.BlockSpec((None, rows, C), lambda s, i, c: (s, i, 0))],
            out_specs=pl.BlockSpec((None, rows, C), lambda s, i, c: (s, i, 0))),
        out_shape=jax.ShapeDtypeStruct(recv.shape, recv.dtype),
        compiler_params=_cparams(("parallel", "parallel")),
    )(c_idx, full, recv)


def _rs_xy(arrs, name):
    n = len(arrs)

    def body(*refs):
        ins, outs, (ssem, rsem, lsem) = refs[:n], refs[n:2 * n], refs[2 * n:]
        x, y, c, chips = _place()
        me = 2 * x + y
        copies = []
        for a in range(n):
            loc = pltpu.make_async_copy(ins[a].at[me], outs[a].at[3], lsem.at[a])
            loc.start()
            copies.append(loc)
            for j, (px, py) in enumerate(chips):
                cp = pltpu.make_async_remote_copy(src_ref=ins[a].at[2 * px + py], dst_ref=outs[a].at[j],
                                                  send_sem=ssem.at[3 * a + j], recv_sem=rsem.at[3 * a + j],
                                                  device_id=(px, py, c), device_id_type=_MESH)
                cp.start()
                copies.append(cp)
        for cp in copies:
            cp.wait()

    return _comm_call(body, arrs, [jax.ShapeDtypeStruct(a.shape, a.dtype) for a in arrs], 3, name)


def _gather8(buf, name):
    def body(in_ref, out_ref, ssem, rsem, lsem):
        x, y, c, _ = _place()
        me = 4 * x + 2 * y + c
        loc = pltpu.make_async_copy(in_ref, out_ref.at[me], lsem.at[0])
        loc.start()
        copies = [loc]
        for d in range(1, 8):
            dx, dy, dc = (d >> 2) & 1, (d >> 1) & 1, d & 1
            peer = (1 - x if dx else x, 1 - y if dy else y, 1 - c if dc else c)
            cp = pltpu.make_async_remote_copy(src_ref=in_ref, dst_ref=out_ref.at[me], send_sem=ssem.at[d - 1],
                                              recv_sem=rsem.at[d - 1], device_id=peer, device_id_type=_MESH)
            cp.start()
            copies.append(cp)
        for cp in copies:
            cp.wait()

    return _comm_call(body, [buf], [jax.ShapeDtypeStruct((8,) + buf.shape, buf.dtype)], 7, name)[0]


def _sum_slots(arr, name, rows=128):
    k, R, C = arr.shape
    rows = min(rows, R)
    assert R % rows == 0

    def body(in_ref, o_ref):
        acc = in_ref[0].astype(F32)
        for s in range(1, k):
            acc = acc + in_ref[s].astype(F32)
        o_ref[...] = acc

    return pl.pallas_call(
        body, name=name, grid=(R // rows,),
        in_specs=[pl.BlockSpec((k, rows, C), lambda i: (0, i, 0))],
        out_specs=pl.BlockSpec((rows, C), lambda i: (i, 0)),
        out_shape=jax.ShapeDtypeStruct((R, C), F32),
        compiler_params=_cparams(("parallel",)),
    )(arr)


ADAM_LR, ADAM_B1, ADAM_B2, ADAM_EPS, ADAM_WD, ADAM_STEP = 0.001, 0.9, 0.999, 1e-08, 0.01, 10


def _adam_math(w, g, m, v):
    m = ADAM_B1 * m + (1.0 - ADAM_B1) * g
    v = ADAM_B2 * v + (1.0 - ADAM_B2) * jnp.square(g)
    m_hat = m / (1.0 - ADAM_B1 ** ADAM_STEP)
    v_hat = v / (1.0 - ADAM_B2 ** ADAM_STEP)
    delta = -ADAM_LR * (m_hat / (jnp.sqrt(v_hat) + ADAM_EPS) + ADAM_WD * w)
    return delta, m, v


def _adamw(w, m, v, mine, other, c_idx, name, g_off=0, rows=128):
    R, C = w.shape
    half = mine.shape[0]
    rows = min(rows, R)
    assert R % rows == 0 and g_off % rows == 0 and half % rows == 0
    nbh, b0 = half // rows, g_off // rows

    def body(c_ref, w_ref, m_ref, v_ref, mine_ref, other_ref, go, do, mo, vo):
        in_my_half = (b0 + pl.program_id(0)) // nbh == c_ref[0]
        g = jnp.where(in_my_half, mine_ref[...], other_ref[...])
        delta, m1, v1 = _adam_math(w_ref[...], g, m_ref[...], v_ref[...])
        go[...] = g
        do[...] = delta
        mo[...] = m1
        vo[...] = v1

    blk = pl.BlockSpec((rows, C), lambda i, c: (i, 0))
    gblk = pl.BlockSpec((rows, C), lambda i, c: ((b0 + i) % nbh, 0))
    return pl.pallas_call(
        body, name=name,
        grid_spec=pltpu.PrefetchScalarGridSpec(
            num_scalar_prefetch=1, grid=(R // rows,),
            in_specs=[blk] * 3 + [gblk] * 2, out_specs=[blk] * 4),
        out_shape=[jax.ShapeDtypeStruct((R, C), F32)] * 4,
        compiler_params=_cparams(("parallel",)),
    )(c_idx, w, m, v, mine, other)


_WEIGHTS = ['norm_mix', 'norm_mlp', 'norm_ple', 'w_in_e', 'w_out_e', 'hgrn_lb', 'g_norm_a', 'conv_w', 'a_log',
            'dt_bias', 'g_norm_b', 's5_a_re', 's5_a_im', 's5_b_re', 's5_b_im', 's5_c_re', 's5_c_im', 's5_d',
            's5_log_dt', 'w_glu', 'b_glu', 'w_out_o', 'w_up', 'w_down', 'w_ple_gate', 'w_ple_proj', 'final_norm']
_INPUTS = ['x', 'p'] + _WEIGHTS + ['loss_target'] + ['m_' + n for n in _WEIGHTS] + ['v_' + n for n in _WEIGHTS]
_FAMILY = [('w_up', 'col', 2048), ('w_down', 'row', 2048), ('w_ple_gate', 'row', 512),
           ('w_out_e', 'row', 512), ('w_glu', 'row', 512), ('w_out_o', 'row', 512)]
_IN_PAD = 8320
_IN_BLK = 1664


def _rms_bwd_fn(x, d_hn, d_res, g):
    _, vjp = jax.vjp(_rms, x, g)
    dx, dg = vjp(d_hn.astype(F32))
    return dx + d_res, dg


def _rms_bwd_both(x, d_hn, d_res, g):
    dx, dg = _rms_bwd_fn(x, d_hn, d_res, g)
    return dx, dx, dg


def _add_res(acc, h):
    return (acc + h,)


def _pack_rows(parts, lanes=128, mult=256):
    flat = jnp.concatenate([q.reshape(-1).astype(F32) for q in parts])
    n = flat.shape[0]
    rows = -(-n // (lanes * mult)) * mult
    return jnp.pad(flat, (0, rows * lanes - n)).reshape(rows, lanes)


def _unpack_rows(buf, shapes):
    flat, out, off = buf.reshape(-1), [], 0
    for s in shapes:
        n = math.prod(s)
        out.append(flat[off:off + n].reshape(s))
        off += n
    return out


def _step(a):
    S = a['x'].shape[1]
    x, tgt = a['x'][0], a['loss_target'][0]
    xi, yi = lax.axis_index("x"), lax.axis_index("y")
    me = 2 * xi + yi
    row = lambda t: t.reshape(1, -1)

    views, packs, off = {}, [], 0
    for nm, kind, r in _FAMILY:
        for l in range(a[nm].shape[0]):
            views[nm, l] = (kind, off, r)
            packs.append(a[nm][l].astype(BF16))
            off += r
    rt = off
    small_sh = jnp.concatenate([a['conv_w'][0].reshape(-1), a['s5_d'][0], a['b_glu'][0]]).reshape(16, 256)
    shards = [jnp.concatenate(packs, axis=0), a['w_in_e'][0].astype(BF16),
              a['w_ple_proj'].reshape(512, 512).astype(BF16), small_sh]
    me_idx = me.astype(jnp.int32).reshape(1)
    gathered = _ag_xy([_own_slot(t.reshape(2, t.shape[0] // 2, t.shape[1]), me_idx, name=f"own_slot{i}")
                       for i, t in enumerate(shards)], name="gather_weights")
    gp, g_in, g_pp, g_small = [g.reshape((4,) + t.shape) for g, t in zip(gathered, shards)]
    win = jnp.pad(g_in.transpose(1, 0, 2).reshape(D_MODEL, 8208), ((0, 0), (0, _IN_PAD - 8208)))
    w_pp = g_pp.reshape(4, 2, 256, 512).transpose(1, 2, 0, 3).reshape(2, 256, D_MODEL)
    g_small = g_small.reshape(4, 4096)
    conv_full = g_small[:, :3072].reshape(4, 4, 768).transpose(1, 0, 2).reshape(4, 3072)
    s5_d_full = g_small[:, 3072:3584].reshape(1, D_MODEL)
    b_glu_full = g_small[:, 3584:].reshape(1, D_MODEL)

    lb, lb_vjp = jax.vjp(lambda hl: jax.nn.softmax(hl, axis=0)[0:1], a['hgrn_lb'])
    pad_h = lambda t: jnp.pad(t, ((0, 0), (0, HEAD - t.shape[1])))
    (arow, dtrow), gdn_vjp = jax.vjp(lambda al, dt: (pad_h(-jnp.exp(al)), pad_h(dt)), a['a_log'], a['dt_bias'])
    s5p, s5_vjp = jax.vjp(_s5_params, a['s5_a_re'][0], a['s5_a_im'][0], a['s5_log_dt'][0], a['s5_b_re'][0],
                          a['s5_b_im'][0], a['s5_c_re'][0], a['s5_c_im'][0])
    bre, bim, cre, cim = [t.astype(BF16) for t in s5p[:4]]
    pw_r, pw_i = _s5_powers(s5p[4], s5p[5])
    gna, gnb = a['g_norm_a'], a['g_norm_b']

    def norm_cast(h, g, dt, name):
        return _rowwise(_rms, [h], [row(g)], [(D_MODEL, dt)], name=name)

    def mlp_ple_fwd(h, l):
        hn = norm_cast(h, a['norm_mlp'][l], BF16, f"rms_mlp{l}")
        up, act = _mm(hn, gp, b_view=views['w_up', l], out_dtypes=(BF16, BF16), name=f"up{l}",
                      epilogue=lambda acc: (acc, jnp.square(jnp.maximum(acc, 0.0))))
        h2 = _mm(act, gp, b_view=views['w_down', l], epilogue=_add_res, extras=(h,), name=f"down{l}")
        hnp = norm_cast(h2, a['norm_ple'][l], BF16, f"rms_ple{l}")
        pp = _mm(a['p'][l, 0], w_pp[l], name=f"ple_proj{l}")
        h3, gpre = _mm(hnp, gp, b_view=views['w_ple_gate', l], extras=(h2, pp), out_dtypes=(F32, F32), name=f"ple_gate{l}",
                       bm=512, epilogue=lambda acc, h2, pp: (h2 + jax.nn.sigmoid(acc) * pp, acc))
        return h3, (h, hn, up, act, h2, hnp, pp, gpre)

    hn0 = norm_cast(x, a['norm_mix'][0], BF16, "rms_mix0")
    proj = _mm(hn0, win, bn=_IN_BLK, name="in_proj")
    o_a, st_a = _hgrn_fwd(proj, lb, gna)
    act_b = _conv_fwd(proj, conv_full)
    o_b, st_b = _gdn_fwd(act_b, proj, arow, dtrow, gnb)
    merged = jnp.concatenate([o_a, o_b], axis=1)
    h1 = _mm(merged, gp, b_view=views['w_out_e', 0], epilogue=_add_res, extras=(x,), name="out_e")
    h3, saved0 = mlp_ple_fwd(h1, 0)

    u = norm_cast(h3, a['norm_mix'][1], F32, "rms_mix1")
    y, xre, xim = _s5_fwd(u, bre, bim, cre, cim, pw_r, pw_i, s5_d_full)
    actg = _rowwise(jax.nn.gelu, [y], [], [(D_MODEL, BF16)], name="gelu")
    glu, z = _mm(actg, gp, b_view=views['w_glu', 0], extras=(actg, b_glu_full), out_dtypes=(BF16, F32), name="glu",
                 epilogue=lambda acc, act, b: (act.astype(F32) * jax.nn.sigmoid(acc + b), acc + b))
    h4 = _mm(glu, gp, b_view=views['w_out_o', 0], epilogue=_add_res, extras=(h3,), name="out_o")
    h6, saved1 = mlp_ple_fwd(h4, 1)

    def head(h, t, g):
        def f(h, g):
            e = _rms(h, g) - t
            return 0.5 * jnp.sum(jnp.mean(e * e, axis=-1))
        val, vjp = jax.vjp(f, h, g)
        dh, dg = vjp(jnp.ones((), F32))
        return dh, dg, jnp.full((1, HEAD), val)

    dh, d_final, loss_part = _rowwise(head, [h6, tgt], [row(a['final_norm'])], [(D_MODEL, F32)], sums=(D_MODEL, HEAD),
                                      name="loss_head")
    loss = lax.psum(loss_part[0, 0], ("x", "y", "c"))

    gbuf = lax.empty((4, rt, PACK_COLS), BF16)

    def into(gbuf, lhs, rhs, key, name):
        return _mm(lhs, rhs, dims="tn", out_into=gbuf, out_view=views[key], out_dtypes=(BF16,), name=name)

    def mlp_ple_bwd(dh, l, saved, gbuf):
        h, hn, up, act, h2, hnp, pp, gpre = saved

        def gate_bwd(d, gpre, pp):
            s = jax.nn.sigmoid(gpre)
            return d * s, d * pp * s * (1.0 - s)

        d_pp, d_gp = _rowwise(gate_bwd, [dh, gpre, pp], [], [(D_MODEL, BF16), (D_MODEL, BF16)], name=f"ple_bwd{l}")
        d_wpp = _mm(a['p'][l, 0], d_pp, dims="tn", name=f"d_ple_proj{l}")
        gbuf = into(gbuf, hnp, d_gp, ('w_ple_gate', l), f"d_ple_gate{l}")
        d_hnp = _mm(d_gp, gp, dims="nt", b_view=views['w_ple_gate', l], name=f"ple_gate_t{l}")
        dh2, dh2_b, d_nple = _rowwise(_rms_bwd_both, [h2, d_hnp, dh], [row(a['norm_ple'][l])],
                                      [(D_MODEL, F32), (D_MODEL, BF16)], sums=(D_MODEL,), name=f"rms_ple_bwd{l}")
        d_up = _mm(dh2_b, gp, dims="nt", b_view=views['w_down', l], extras=(up,), out_dtypes=(BF16,), name=f"down_t{l}",
                   epilogue=lambda acc, up: (acc * 2.0 * jnp.maximum(up.astype(F32), 0.0),))
        gbuf = into(gbuf, act, dh2_b, ('w_down', l), f"d_down{l}")
        gbuf = into(gbuf, hn, d_up, ('w_up', l), f"d_up{l}")
        d_hn = _mm(d_up, gp, dims="nt", b_view=views['w_up', l], name=f"up_t{l}")
        dh1, dh1_b, d_nmlp = _rowwise(_rms_bwd_both, [h, d_hn, dh2], [row(a['norm_mlp'][l])],
                                      [(D_MODEL, F32), (D_MODEL, BF16)], sums=(D_MODEL,), name=f"rms_mlp_bwd{l}")
        return dh1, dh1_b, gbuf, d_wpp, d_nple, d_nmlp

    dh4, dh4_b, gbuf, d_wpp1, d_nple1, d_nmlp1 = mlp_ple_bwd(dh, 1, saved1, gbuf)

    d_glu = _mm(dh4_b, gp, dims="nt", b_view=views['w_out_o', 0], name="out_o_t")
    gbuf = into(gbuf, glu, dh4_b, ('w_out_o', 0), "d_out_o")

    def glu_bwd(d, z, act):
        s, act = jax.nn.sigmoid(z), act.astype(F32)
        dz = d * act * s * (1.0 - s)
        return dz, d * s, jnp.sum(dz, axis=0, keepdims=True)

    d_z, d_actp, d_bglu = _rowwise(glu_bwd, [d_glu, z, actg], [], [(D_MODEL, BF16), (D_MODEL, F32)], sums=(D_MODEL,),
                                   name="glu_bwd")
    gbuf = into(gbuf, actg, d_z, ('w_glu', 0), "d_glu")

    def gelu_bwd(acc, dap, y):
        _, vjp = jax.vjp(jax.nn.gelu, y)
        return vjp(acc + dap)

    dy = _mm(d_z, gp, dims="nt", b_view=views['w_glu', 0], extras=(d_actp, y), epilogue=gelu_bwd, name="glu_t")
    du, d_bre, d_bim, d_cre, d_cim, d_lr, d_li, d_s5d = _s5_bwd(dy, u, xre, xim, bre, bim, cre, cim,
                                                                 pw_r[:, ::-1], -pw_i[:, ::-1], s5_d_full)
    dh3, d_nmix1 = _rowwise(_rms_bwd_fn, [h3, du, dh4], [row(a['norm_mix'][1])], [(D_MODEL, F32)], sums=(D_MODEL,),
                            name="rms_mix1_bwd")
    d_are, d_aim, d_logdt, d_sbre, d_sbim, d_scre, d_scim = s5_vjp(
        (d_bre, d_bim, d_cre, d_cim, d_lr.sum(axis=1), d_li.sum(axis=1)))

    dh1, dh1_b, gbuf, d_wpp0, d_nple0, d_nmlp0 = mlp_ple_bwd(dh3, 0, saved0, gbuf)

    d_merged = _mm(dh1_b, gp, dims="nt", b_view=views['w_out_e', 0], name="out_e_t")
    gbuf = into(gbuf, merged, dh1_b, ('w_out_e', 0), "d_out_e")
    dq, df, dv, dg, d_lb, d_gna = _hgrn_bwd(proj, lb, gna, st_a, d_merged)
    dqb, dkb, dvb, dzb, d_tail, d_arow, d_dtrow, d_gnb = _gdn_bwd(act_b, proj, arow, dtrow, gnb, st_b, d_merged)
    d_pre, d_conv = _conv_bwd(proj, conv_full, dqb, dkb, dvb)
    d_proj = jnp.concatenate([dq, df, dv, dg, d_pre, dzb, d_tail], axis=1)
    d_win = _mm(hn0, d_proj, dims="tn", bn=_IN_BLK, out_dtypes=(BF16,), name="d_in_proj")
    d_hn0 = _mm(d_proj, win, dims="nt", bk=_IN_BLK, name="in_proj_t")
    grad_x, d_nmix0 = _rowwise(_rms_bwd_fn, [x, d_hn0, dh1], [row(a['norm_mix'][0])], [(D_MODEL, F32)], sums=(D_MODEL,),
                               name="rms_mix0_bwd")
    (d_hlb,) = lb_vjp(d_lb.reshape(1, 1024))
    d_alog, d_dtb = gdn_vjp((d_arow, d_dtrow))

    d_win_sh = d_win[:, :8208].reshape(D_MODEL, 4, 2052).transpose(1, 0, 2)
    d_wpp_sh = jnp.stack([d_wpp0, d_wpp1]).astype(BF16).reshape(2, 256, 4, 512).transpose(2, 0, 1, 3).reshape(4, 512, 512)
    partials = [t.reshape(4, 2, t.shape[1] // 2, t.shape[2]) for t in (gbuf, d_win_sh, d_wpp_sh)]
    c_idx = lax.axis_index("c").astype(jnp.int32).reshape(1)
    chip_sums = [_pair_sum(f, r, c_idx, name=f"pair_sum{i}")
                 for i, (f, r) in enumerate(zip(partials, _pair_split(partials, name="pair_split")))]
    halves = [_sum_slots(t, name=f"sum_partials{i}") for i, t in enumerate(_rs_xy(chip_sums, name="scatter_grads"))]
    others = _pair_swap(halves, name="swap_halves")

    small = {
        'norm_mix': jnp.concatenate([d_nmix0, d_nmix1]), 'norm_mlp': jnp.concatenate([d_nmlp0, d_nmlp1]),
        'norm_ple': jnp.concatenate([d_nple0, d_nple1]), 'hgrn_lb': d_hlb, 'g_norm_a': d_gna, 'conv_w': d_conv,
        'a_log': d_alog, 'dt_bias': d_dtb, 'g_norm_b': d_gnb, 's5_a_re': d_are, 's5_a_im': d_aim, 's5_b_re': d_sbre,
        's5_b_im': d_sbim, 's5_c_re': d_scre, 's5_c_im': d_scim, 's5_d': d_s5d, 's5_log_dt': d_logdt,
        'b_glu': d_bglu, 'final_norm': d_final}
    s_names = list(small)
    s_shapes = [tuple(small[n].shape) for n in s_names]
    reduced = _sum_slots(_gather8(_pack_rows([small[n] for n in s_names]), name="gather_small"), name="sum_small")
    sg = dict(zip(s_names, _unpack_rows(reduced, s_shapes)))
    sg['conv_w'] = lax.dynamic_slice_in_dim(sg['conv_w'], me * 768, 768, axis=1)
    sg['s5_d'] = lax.dynamic_slice_in_dim(sg['s5_d'], me * 512, 512, axis=1)
    sg['b_glu'] = lax.dynamic_slice_in_dim(sg['b_glu'], me * 512, 512, axis=1)
    sg = {n: sg[n].reshape(a[n].shape) for n in s_names}
    w_pack, m_pack, v_pack, g_pack = [_pack_rows([src[pre + n] for n in s_names])
                                      for src, pre in ((a, ''), (a, 'm_'), (a, 'v_'), (sg, ''))]
    sd, sm, sv = _rowwise(_adam_math, [w_pack, g_pack, m_pack, v_pack], [], [(128, F32)] * 3, name="adam_small")
    w_shapes = [tuple(a[n].shape) for n in s_names]
    res = {n: (sg[n], d_, m_, v_) for n, d_, m_, v_ in zip(s_names, _unpack_rows(sd, w_shapes), _unpack_rows(sm, w_shapes),
                                                          _unpack_rows(sv, w_shapes))}

    def adam_big(nm, k, g_off, shape2d):
        outs = _adamw(a[nm].reshape(shape2d), a['m_' + nm].reshape(shape2d), a['v_' + nm].reshape(shape2d), halves[k],
                      others[k], c_idx, name="adam_" + nm, g_off=g_off)
        res[nm] = tuple(o.reshape(a[nm].shape) for o in outs)

    for nm, _, r in _FAMILY:
        adam_big(nm, 0, views[nm, 0][1], (a[nm].shape[0] * r, PACK_COLS))
    adam_big('w_in_e', 1, 0, (D_MODEL, 2052))
    adam_big('w_ple_proj', 2, 0, (512, 512))

    return (loss, grad_x[None], *[res[n][0] for n in _WEIGHTS], *[res[n][1] for n in _WEIGHTS],
            *[res[n][2] for n in _WEIGHTS], *[res[n][3] for n in _WEIGHTS])


def kernel(x, p, norm_mix, norm_mlp, norm_ple, w_in_e, w_out_e, hgrn_lb, g_norm_a, conv_w, a_log, dt_bias, g_norm_b, s5_a_re, s5_a_im, s5_b_re, s5_b_im, s5_c_re, s5_c_im, s5_d, s5_log_dt, w_glu, b_glu, w_out_o, w_up, w_down, w_ple_gate, w_ple_proj, final_norm, loss_target, m_norm_mix, m_norm_mlp, m_norm_ple, m_w_in_e, m_w_out_e, m_hgrn_lb, m_g_norm_a, m_conv_w, m_a_log, m_dt_bias, m_g_norm_b, m_s5_a_re, m_s5_a_im, m_s5_b_re, m_s5_b_im, m_s5_c_re, m_s5_c_im, m_s5_d, m_s5_log_dt, m_w_glu, m_b_glu, m_w_out_o, m_w_up, m_w_down, m_w_ple_gate, m_w_ple_proj, m_final_norm, v_norm_mix, v_norm_mlp, v_norm_ple, v_w_in_e, v_w_out_e, v_hgrn_lb, v_g_norm_a, v_conv_w, v_a_log, v_dt_bias, v_g_norm_b, v_s5_a_re, v_s5_a_im, v_s5_b_re, v_s5_b_im, v_s5_c_re, v_s5_c_im, v_s5_d, v_s5_log_dt, v_w_glu, v_b_glu, v_w_out_o, v_w_up, v_w_down, v_w_ple_gate, v_w_ple_proj, v_final_norm):
    args = (x, p, norm_mix, norm_mlp, norm_ple, w_in_e, w_out_e, hgrn_lb, g_norm_a, conv_w, a_log, dt_bias, g_norm_b, s5_a_re, s5_a_im, s5_b_re, s5_b_im, s5_c_re, s5_c_im, s5_d, s5_log_dt, w_glu, b_glu, w_out_o, w_up, w_down, w_ple_gate, w_ple_proj, final_norm, loss_target, m_norm_mix, m_norm_mlp, m_norm_ple, m_w_in_e, m_w_out_e, m_hgrn_lb, m_g_norm_a, m_conv_w, m_a_log, m_dt_bias, m_g_norm_b, m_s5_a_re, m_s5_a_im, m_s5_b_re, m_s5_b_im, m_s5_c_re, m_s5_c_im, m_s5_d, m_s5_log_dt, m_w_glu, m_b_glu, m_w_out_o, m_w_up, m_w_down, m_w_ple_gate, m_w_ple_proj, m_final_norm, v_norm_mix, v_norm_mlp, v_norm_ple, v_w_in_e, v_w_out_e, v_hgrn_lb, v_g_norm_a, v_conv_w, v_a_log, v_dt_bias, v_g_norm_b, v_s5_a_re, v_s5_a_im, v_s5_b_re, v_s5_b_im, v_s5_c_re, v_s5_c_im, v_s5_d, v_s5_log_dt, v_w_glu, v_b_glu, v_w_out_o, v_w_up, v_w_down, v_w_ple_gate, v_w_ple_proj, v_final_norm)
    return _step(dict(zip(_INPUTS, args)))
```

```python
import functools
import math

import jax
import jax.numpy as jnp
from jax import lax
from jax.experimental import pallas as pl
from jax.experimental.pallas import tpu as pltpu

F32 = jnp.float32
BF16 = jnp.bfloat16

D_MODEL = 2048
SEQ = 4096
NORM_EPS = 1e-6
CHUNK = 64
HEAD = 128
VMEM_LIMIT = 56 * 1024 * 1024


_NN = (((1,), (0,)), ((), ()))
_NT = (((1,), (1,)), ((), ()))
_TN = (((0,), (0,)), ((), ()))
_HI = lax.Precision.HIGHEST
_NEG = -1e30


def _cparams(sem, **kw):
    return pltpu.CompilerParams(dimension_semantics=sem, vmem_limit_bytes=VMEM_LIMIT, **kw)


PACK_COLS = 2048


def _view_shape(view):
    kind, _, r = view
    return (4 * r, PACK_COLS) if kind == "row" else (r, 4 * PACK_COLS)


def _view_spec(view, rb, cb, row_of, col_of):
    kind, off, r = view
    assert off % rb == 0 and r % rb == 0 and PACK_COLS % cb == 0, (view, rb, cb)
    if kind == "row":
        nrb = r // rb
        return pl.BlockSpec((None, rb, cb), lambda i, j, k: (row_of(i, j, k) // nrb,
                                                             off // rb + row_of(i, j, k) % nrb, col_of(i, j, k)))
    ncb = PACK_COLS // cb
    return pl.BlockSpec((None, rb, cb), lambda i, j, k: (col_of(i, j, k) // ncb,
                                                         off // rb + row_of(i, j, k), col_of(i, j, k) % ncb))


def _mm(a, b, *, dims="nn", epilogue=None, extras=(), out_dtypes=(F32,), bm=1024, bn=1024, bk=2048, name,
        b_view=None, out_into=None, out_view=None, after=()):
    b_shape = _view_shape(b_view) if b_view is not None else b.shape
    if dims == "tn":
        (K, M), (K2, N) = a.shape, b_shape
    elif dims == "nt":
        (M, K), (N, K2) = a.shape, b_shape
    else:
        (M, K), (K2, N) = a.shape, b_shape
    assert K == K2, (a.shape, b_shape, dims)
    if b_view is not None and dims == "nt":
        bn = min(bn, b_view[2])
    if b_view is not None and dims != "nt":
        bk = min(bk, b_view[2])
    if out_view is not None:
        bm = min(bm, out_view[2])
    bm, bn, bk = min(bm, M), min(bn, N), min(bk, K)
    assert M % bm == 0 and N % bn == 0 and K % bk == 0, (M, N, K, bm, bn, bk)
    nk = K // bk
    ii, jj, kk = (lambda i, j, k: i), (lambda i, j, k: j), (lambda i, j, k: k)
    if dims == "tn":
        a_spec = pl.BlockSpec((bk, bm), lambda i, j, k: (k, i))
        dn = _TN
    else:
        a_spec = pl.BlockSpec((bm, bk), lambda i, j, k: (i, k))
        dn = _NT if dims == "nt" else _NN
    if dims == "nt":
        b_spec = _view_spec(b_view, bn, bk, jj, kk) if b_view else pl.BlockSpec((bn, bk), lambda i, j, k: (j, k))
    else:
        b_spec = _view_spec(b_view, bk, bn, kk, jj) if b_view else pl.BlockSpec((bk, bn), lambda i, j, k: (k, j))
    e_specs = []
    for e in extras:
        if e.shape == (M, N):
            e_specs.append(pl.BlockSpec((bm, bn), lambda i, j, k: (i, j)))
        else:
            assert e.shape == (1, N), e.shape
            e_specs.append(pl.BlockSpec((1, bn), lambda i, j, k: (0, j)))
    ne, no = len(extras), len(out_dtypes)
    if epilogue is None:
        epilogue = lambda acc: (acc,)
    into = out_into is not None

    def body(a_ref, b_ref, *rest):
        e_refs, rest = rest[:ne], rest[ne + (1 if into else 0) + len(after):]
        o_refs = rest[:no]
        part = lax.dot_general(a_ref[...].astype(BF16), b_ref[...].astype(BF16), dn, preferred_element_type=F32)

        def finish(total):
            outs = epilogue(total, *[e[...] for e in e_refs])
            for o, v in zip(o_refs, outs):
                o[...] = v.astype(o.dtype)

        if nk == 1:
            finish(part)
            return
        acc, k = rest[no], pl.program_id(2)

        @pl.when(k == 0)
        def _():
            acc[...] = part

        @pl.when((k > 0) & (k < nk - 1))
        def _():
            acc[...] += part

        @pl.when(k == nk - 1)
        def _():
            finish(acc[...] + part)

    if into:
        assert no == 1 and _view_shape(out_view) == (M, N), (out_view, M, N)
        out_specs = [_view_spec(out_view, bm, bn, ii, jj)]
        out_shape = [jax.ShapeDtypeStruct(out_into.shape, out_into.dtype)]
        extra_in, extra_specs, alias = [out_into], [pl.BlockSpec(memory_space=pl.ANY)], {2 + ne: 0}
    else:
        out_specs = [pl.BlockSpec((bm, bn), lambda i, j, k: (i, j)) for _ in out_dtypes]
        out_shape = [jax.ShapeDtypeStruct((M, N), dt) for dt in out_dtypes]
        extra_in, extra_specs, alias = [], [], {}
    outs = pl.pallas_call(
        body, name=name,
        grid=(M // bm, N // bn, nk),
        in_specs=[a_spec, b_spec] + e_specs + extra_specs + [pl.BlockSpec(memory_space=pl.ANY)] * len(after),
        out_specs=out_specs, out_shape=out_shape,
        scratch_shapes=[pltpu.VMEM((bm, bn), F32)] if nk > 1 else [],
        input_output_aliases=alias,
        compiler_params=_cparams(("parallel", "parallel", "arbitrary")),
    )(a, b, *extras, *extra_in, *after)
    return outs[0] if no == 1 else tuple(outs)


def _rowwise(fn, tiled, bcast, outs, sums=(), *, rows=256, name):
    S = tiled[0].shape[0]
    rows = min(rows, S)
    assert S % rows == 0
    nt, nb, no, ns = len(tiled), len(bcast), len(outs), len(sums)

    def body(*refs):
        t_refs, b_refs = refs[:nt], refs[nt:nt + nb]
        o_refs, s_refs = refs[nt + nb:nt + nb + no], refs[nt + nb + no:]
        res = fn(*[r[...] for r in t_refs], *[r[...] for r in b_refs])
        res = res if isinstance(res, (tuple, list)) else (res,)
        for o, v in zip(o_refs, res[:no]):
            o[...] = v.astype(o.dtype)
        if ns:
            @pl.when(pl.program_id(0) == 0)
            def _():
                for s in s_refs:
                    s[...] = jnp.zeros_like(s)
            for s, v in zip(s_refs, res[no:]):
                s[...] += v

    res = pl.pallas_call(
        body, name=name,
        grid=(S // rows,),
        in_specs=[pl.BlockSpec((rows, t.shape[1]), lambda i: (i, 0)) for t in tiled]
        + [pl.BlockSpec(b.shape, lambda i, nd=b.ndim: (0,) * nd) for b in bcast],
        out_specs=[pl.BlockSpec((rows, c), lambda i: (i, 0)) for c, _ in outs]
        + [pl.BlockSpec((1, c), lambda i: (0, 0)) for c in sums],
        out_shape=[jax.ShapeDtypeStruct((S, c), dt) for c, dt in outs]
        + [jax.ShapeDtypeStruct((1, c), F32) for c in sums],
        compiler_params=_cparams(("arbitrary",)),
    )(*tiled, *bcast)
    return res[0] if len(res) == 1 else tuple(res)


def _rms(x, g):
    return x * lax.rsqrt(jnp.mean(x * x, axis=-1, keepdims=True) + NORM_EPS) * g


def _mxu(a, b, dn):
    if a.ndim == 3:
        ((ca,), (cb,)), _ = dn
        dn = (((ca + 1,), (cb + 1,)), ((0,), (0,)))
    return lax.dot_general(a, b, dn, preferred_element_type=F32)


def _split(a):
    hi = a.astype(BF16)
    return hi, (a - hi.astype(F32)).astype(BF16)


def _passes(a, b, dn, three):
    if not three:
        return _mxu(a.astype(BF16), b.astype(BF16), dn)
    (ah, al), (bh, bl) = _split(a), _split(b)
    return _mxu(ah, bh, dn) + (_mxu(ah, bl, dn) + _mxu(al, bh, dn))


def _dot_grads(a, b, g, dn, three):
    if dn == _NN:
        return _passes(g, b, _NT, three), _passes(a, g, _TN, three)
    if dn == _NT:
        return _passes(g, b, _NN, three), _passes(g, a, _TN, three)
    assert dn == _TN
    return _passes(b, g, _NT, three), _passes(a, g, _NN, three)


@functools.partial(jax.custom_vjp, nondiff_argnums=(2,))
def _dot(a, b, dn=_NN):
    return _passes(a, b, dn, False)


_dot.defvjp(lambda a, b, dn: (_passes(a, b, dn, False), (a, b)),
            lambda dn, res, g: _dot_grads(res[0], res[1], g, dn, False))


@functools.partial(jax.custom_vjp, nondiff_argnums=(2,))
def _dot3(a, b, dn=_NN):
    return _passes(a, b, dn, True)


_dot3.defvjp(lambda a, b, dn: (_passes(a, b, dn, True), (a, b)),
             lambda dn, res, g: _dot_grads(res[0], res[1], g, dn, True))


def _tri_times(x, dn):
    tri = _tril(x.shape[-2]).astype(BF16)
    if x.ndim == 3:
        tri = jnp.broadcast_to(tri, (x.shape[0],) + tri.shape)
    hi, lo = _split(x)
    lo2 = (x - hi.astype(F32) - lo.astype(F32)).astype(BF16)
    return _mxu(tri, hi, dn) + (_mxu(tri, lo, dn) + _mxu(tri, lo2, dn))


@jax.custom_vjp
def _cumsum_rows(x):
    return _tri_times(x, _NN)


_cumsum_rows.defvjp(lambda x: (_tri_times(x, _NN), None), lambda _, g: (_tri_times(g, _TN),))


def _iota(shape, dim):
    return lax.broadcasted_iota(jnp.int32, shape, dim)


def _tril(n, strict=False):
    r, c = _iota((n, n), 0), _iota((n, n), 1)
    return (r > c) if strict else (r >= c)


@functools.partial(jax.custom_vjp, nondiff_argnums=(1,))
def _roll_rows(x, r):
    return pltpu.roll(x, r, 0)


def _roll_rows_fwd(x, r):
    return pltpu.roll(x, r, 0), None


def _roll_rows_bwd(r, _, g):
    return (pltpu.roll(g, (g.shape[0] - r) % g.shape[0], 0),)


_roll_rows.defvjp(_roll_rows_fwd, _roll_rows_bwd)


def _head_norm_gate(o, gn, gate):
    return _rms(o, gn) * jax.nn.silu(gate)


_SUB = 16
_HPS = 8


def _hgrn_chunk(q, fpre, v, gate, lb, gna, st):
    c = q.shape[0]
    forget = lb + (1.0 - lb) * jax.nn.sigmoid(fpre)
    k = 1.0 - forget
    logf = jnp.log(forget)
    cum = _cumsum_rows(logf)
    cum_end = jnp.sum(logf, axis=0, keepdims=True)
    o = _dot(q * jnp.exp(cum), st, _NT)
    st_new = st * jnp.exp(cum_end) + _dot(v, k * jnp.exp(cum_end - cum), _TN)
    t = _iota((c, 1), 0)
    s_off = jnp.zeros((c, c), F32)
    for i in range(1, c // _SUB):
        before = t < i * _SUB
        c_i = jnp.sum(jnp.where(before, logf, 0.0), axis=0, keepdims=True)
        in_blk = (t >= i * _SUB) & (t < (i + 1) * _SUB)
        qi = jnp.where(in_blk, q * jnp.exp(jnp.minimum(cum - c_i, 0.0)), 0.0)
        ki = jnp.where(before, k * jnp.exp(jnp.minimum(c_i - cum, 0.0)), 0.0)
        s_off = s_off + _dot(qi, ki, _NT)
    o = o + _dot(s_off, v)
    tmod = t % _SUB
    for r in range(_SUB):
        kr, cr, vr = (k, cum, v) if r == 0 else (_roll_rows(k, r), _roll_rows(cum, r), _roll_rows(v, r))
        w = q * kr * jnp.exp(jnp.where(tmod >= r, cum - cr, _NEG))
        o = o + jnp.sum(w, axis=1, keepdims=True) * vr
    return _head_norm_gate(o, gna, gate), st_new


def _hgrn_fwd(proj, lb, gna):
    S = proj.shape[0]
    n_chunks, heads = S // CHUNK, 8

    def body(q_ref, f_ref, v_ref, g_ref, lb_ref, gna_ref, o_ref, st_out, st):
        n, hb = pl.program_id(0), pl.program_id(1)

        @pl.when(n == 0)
        def _():
            st[hb] = jnp.zeros((_HPS, HEAD, HEAD), F32)

        sls = [slice(hh * HEAD, (hh + 1) * HEAD) for hh in range(_HPS)]
        st0 = [st[hb, hh] for hh in range(_HPS)]
        ins = [(q_ref[:, sl], f_ref[:, sl], v_ref[:, sl], g_ref[:, sl], lb_ref[:, sl]) for sl in sls]
        gna = gna_ref[...]
        res = [_hgrn_chunk(*ins[hh], gna, st0[hh]) for hh in range(_HPS)]
        for hh, sl in enumerate(sls):
            st_out[hh] = st0[hh]
            o_ref[:, sl] = res[hh][0].astype(o_ref.dtype)
            st[hb, hh] = res[hh][1]

    wide = _HPS * HEAD
    sec = lambda s: pl.BlockSpec((CHUNK, wide), lambda n, h, s=s: (n, (8 // _HPS) * s + h))
    return pl.pallas_call(
        body, name="hgrn_fwd", grid=(n_chunks, heads // _HPS),
        in_specs=[sec(0), sec(1), sec(2), sec(3),
                  pl.BlockSpec((1, wide), lambda n, h: (0, h)), pl.BlockSpec((1, HEAD), lambda n, h: (0, 0))],
        out_specs=[pl.BlockSpec((CHUNK, wide), lambda n, h: (n, h)),
                   pl.BlockSpec((None, _HPS, HEAD, HEAD), lambda n, h: (n, h, 0, 0))],
        out_shape=[jax.ShapeDtypeStruct((S, 1024), BF16),
                   jax.ShapeDtypeStruct((n_chunks, heads, HEAD, HEAD), F32)],
        scratch_shapes=[pltpu.VMEM((heads // _HPS, _HPS, HEAD, HEAD), F32)],
        compiler_params=_cparams(("arbitrary", "arbitrary")),
    )(proj, proj, proj, proj, lb, gna)


def _hgrn_bwd(proj, lb, gna, states, d_o):
    S = proj.shape[0]
    n_chunks, heads = S // CHUNK, 8

    def body(q_ref, f_ref, v_ref, g_ref, lb_ref, gna_ref, st_ref, do_ref,
             dq_ref, df_ref, dv_ref, dg_ref, dlb_ref, dgna_ref, dst):
        n, hb = pl.program_id(0), pl.program_id(1)

        @pl.when(n == 0)
        def _():
            dst[hb] = jnp.zeros((_HPS, HEAD, HEAD), F32)
            dlb_ref[hb] = jnp.zeros((_HPS, 1, HEAD), F32)

        @pl.when((n == 0) & (hb == 0))
        def _():
            dgna_ref[...] = jnp.zeros_like(dgna_ref)

        sls = [slice(hh * HEAD, (hh + 1) * HEAD) for hh in range(_HPS)]
        gna = gna_ref[...]
        ins = [(q_ref[:, sl], f_ref[:, sl], v_ref[:, sl], g_ref[:, sl], lb_ref[:, sl], gna, st_ref[hh])
               for hh, sl in enumerate(sls)]
        cts = [(do_ref[:, sl], dst[hb, hh]) for hh, sl in enumerate(sls)]
        grads = [jax.vjp(_hgrn_chunk, *ins[hh])[1](cts[hh]) for hh in range(_HPS)]
        dgna_sum = jnp.zeros((1, HEAD), F32)
        for hh, sl in enumerate(sls):
            dq, df, dv, dg, dlb, dgna, dst0 = grads[hh]
            dq_ref[:, sl] = dq.astype(dq_ref.dtype)
            df_ref[:, sl] = df.astype(df_ref.dtype)
            dv_ref[:, sl] = dv.astype(dv_ref.dtype)
            dg_ref[:, sl] = dg.astype(dg_ref.dtype)
            dlb_ref[hb, hh] += dlb
            dgna_sum = dgna_sum + dgna
            dst[hb, hh] = dst0
        dgna_ref[...] += dgna_sum

    rev = lambda n: n_chunks - 1 - n
    wide = _HPS * HEAD
    sec = lambda s: pl.BlockSpec((CHUNK, wide), lambda n, h, s=s: (rev(n), (8 // _HPS) * s + h))
    out = pl.BlockSpec((CHUNK, wide), lambda n, h: (rev(n), h))
    return pl.pallas_call(
        body, name="hgrn_bwd", grid=(n_chunks, heads // _HPS),
        in_specs=[sec(0), sec(1), sec(2), sec(3),
                  pl.BlockSpec((1, wide), lambda n, h: (0, h)), pl.BlockSpec((1, HEAD), lambda n, h: (0, 0)),
                  pl.BlockSpec((None, _HPS, HEAD, HEAD), lambda n, h: (rev(n), h, 0, 0)),
                  pl.BlockSpec((CHUNK, wide), lambda n, h: (rev(n), h))],
        out_specs=[out, out, out, out,
                   pl.BlockSpec((heads // _HPS, _HPS, 1, HEAD), lambda n, h: (0, 0, 0, 0)),
                   pl.BlockSpec((1, HEAD), lambda n, h: (0, 0))],
        out_shape=[jax.ShapeDtypeStruct((S, 1024), BF16)] * 4
        + [jax.ShapeDtypeStruct((heads // _HPS, _HPS, 1, HEAD), F32), jax.ShapeDtypeStruct((1, HEAD), F32)],
        scratch_shapes=[pltpu.VMEM((heads // _HPS, _HPS, HEAD, HEAD), F32)],
        compiler_params=_cparams(("arbitrary", "arbitrary")),
    )(proj, proj, proj, proj, lb, gna, states, d_o)


def _l2n(t):
    return t * lax.rsqrt(jnp.sum(t * t, axis=-1, keepdims=True) + NORM_EPS)


def _unit_lower_inverse(low):
    c = low.shape[-1]
    inv = (_iota((c, c), 0) == _iota((c, c), 1)).astype(F32) - low
    p = low
    span = 2
    while span < c:
        p = _dot3(p, p)
        inv = inv + _dot3(inv, p)
        span *= 2
    return inv


def _gdn_chunk(qc, kc, v, z, tail, h0, arow, dtrow, gnb, st):
    nb, c = qc.shape[0], qc.shape[1]
    head = h0 + _iota((nb, c, HEAD), 0)
    lane = _iota((nb, c, HEAD), 2)
    la_all = arow * jax.nn.softplus(tail + dtrow)
    la = jnp.sum(jnp.where(lane == head, la_all[None], 0.0), axis=2, keepdims=True)
    beta = jnp.sum(jnp.where(lane == head + 8, jax.nn.sigmoid(tail)[None], 0.0), axis=2, keepdims=True)
    la_b = jnp.broadcast_to(la, (nb, c, HEAD))
    cum = _cumsum_rows(la_b)
    cmat = _cumsum_rows(jnp.broadcast_to(la, (nb, c, c)))
    cum_end = jnp.sum(la_b, axis=1, keepdims=True)
    decay = jnp.exp(jnp.where(_tril(c), cmat - jnp.swapaxes(cmat, 1, 2), _NEG))
    q = _l2n(qc) * (HEAD ** -0.5)
    k = _l2n(kc)
    k_beta = k * beta
    low = jnp.where(_tril(c, strict=True), _dot(k_beta, k, _NT) * decay, 0.0)
    inv = _unit_lower_inverse(low)
    u, w = _dot(inv, v * beta), _dot(inv, k_beta * jnp.exp(cum))
    intra = _dot(q, k, _NT) * decay
    v_new = u - _dot(w, st, _NT)
    o = _dot(q * jnp.exp(cum), st, _NT) + _dot(intra, v_new)
    st_new = st * jnp.exp(cum_end) + _dot(v_new, k * jnp.exp(cum_end - cum), _TN)
    return _head_norm_gate(o, gnb, z), st_new


def _gdn_specs(n_of):
    wide = _HPS * HEAD
    qkv = lambda s: pl.BlockSpec((CHUNK, wide), lambda n, h, s=s: (n_of(n), (8 // _HPS) * s + h))
    return [qkv(0), qkv(1), qkv(2),
            pl.BlockSpec((CHUNK, wide), lambda n, h: (n_of(n), 56 // _HPS + h)),
            pl.BlockSpec((CHUNK, HEAD), lambda n, h: (n_of(n), 64)),
            pl.BlockSpec((1, HEAD), lambda n, h: (0, 0)), pl.BlockSpec((1, HEAD), lambda n, h: (0, 0)),
            pl.BlockSpec((1, HEAD), lambda n, h: (0, 0))]


def _gdn_fwd(act, proj, arow, dtrow, gnb):
    S = act.shape[0]
    n_chunks, heads = S // CHUNK, 8

    def body(q_ref, k_ref, v_ref, z_ref, t_ref, a_ref, dt_ref, gnb_ref, o_ref, st_out, st):
        n, hb = pl.program_id(0), pl.program_id(1)

        @pl.when(n == 0)
        def _():
            st[hb] = jnp.zeros((_HPS, HEAD, HEAD), F32)

        sls = [slice(hh * HEAD, (hh + 1) * HEAD) for hh in range(_HPS)]
        heads_of = lambda ref: jnp.stack([ref[:, sl] for sl in sls])
        st0 = st[hb]
        o, st1 = _gdn_chunk(heads_of(q_ref), heads_of(k_ref), heads_of(v_ref), heads_of(z_ref), t_ref[...], hb * _HPS,
                            a_ref[...], dt_ref[...], gnb_ref[...], st0)
        st_out[...] = st0
        st[hb] = st1
        for hh, sl in enumerate(sls):
            o_ref[:, sl] = o[hh].astype(o_ref.dtype)

    return pl.pallas_call(
        body, name="gdn_fwd", grid=(n_chunks, heads // _HPS),
        in_specs=_gdn_specs(lambda n: n),
        out_specs=[pl.BlockSpec((CHUNK, _HPS * HEAD), lambda n, h: (n, h)),
                   pl.BlockSpec((None, _HPS, HEAD, HEAD), lambda n, h: (n, h, 0, 0))],
        out_shape=[jax.ShapeDtypeStruct((S, 1024), BF16),
                   jax.ShapeDtypeStruct((n_chunks, heads, HEAD, HEAD), F32)],
        scratch_shapes=[pltpu.VMEM((heads // _HPS, _HPS, HEAD, HEAD), F32)],
        compiler_params=_cparams(("arbitrary", "arbitrary")),
    )(act, act, act, proj, proj, arow, dtrow, gnb)


def _gdn_bwd(act, proj, arow, dtrow, gnb, states, d_o):
    S = act.shape[0]
    n_chunks, heads = S // CHUNK, 8

    def body(q_ref, k_ref, v_ref, z_ref, t_ref, a_ref, dt_ref, gnb_ref, st_ref, do_ref,
             dq_ref, dk_ref, dv_ref, dz_ref, dt_out, da_ref, ddt_ref, dgnb_ref, dst, dtail_acc):
        n, hb = pl.program_id(0), pl.program_id(1)
        n_hb = heads // _HPS

        @pl.when(n == 0)
        def _():
            dst[hb] = jnp.zeros((_HPS, HEAD, HEAD), F32)

        @pl.when((n == 0) & (hb == 0))
        def _():
            da_ref[...] = jnp.zeros_like(da_ref)
            ddt_ref[...] = jnp.zeros_like(ddt_ref)
            dgnb_ref[...] = jnp.zeros_like(dgnb_ref)

        sls = [slice(hh * HEAD, (hh + 1) * HEAD) for hh in range(_HPS)]
        heads_of = lambda ref: jnp.stack([ref[:, sl] for sl in sls])
        fn = lambda qc, kc, v, z, tail, arow, dtrow, gnb, st: _gdn_chunk(qc, kc, v, z, tail, hb * _HPS, arow, dtrow,
                                                                        gnb, st)
        _, vjp = jax.vjp(fn, heads_of(q_ref), heads_of(k_ref), heads_of(v_ref), heads_of(z_ref), t_ref[...],
                         a_ref[...], dt_ref[...], gnb_ref[...], st_ref[...])
        dq, dk, dv, dz, dtail_sum, da_sum, ddt_sum, dgnb_sum, dst0 = vjp((heads_of(do_ref), dst[hb]))
        dst[hb] = dst0
        for hh, sl in enumerate(sls):
            dq_ref[:, sl] = dq[hh]
            dk_ref[:, sl] = dk[hh]
            dv_ref[:, sl] = dv[hh]
            dz_ref[:, sl] = dz[hh].astype(dz_ref.dtype)

        @pl.when(hb == 0)
        def _():
            dtail_acc[...] = dtail_sum

        @pl.when(hb > 0)
        def _():
            dtail_acc[...] += dtail_sum

        @pl.when(hb == n_hb - 1)
        def _():
            dt_out[...] = dtail_acc[...].astype(dt_out.dtype)

        da_ref[...] += da_sum
        ddt_ref[...] += ddt_sum
        dgnb_ref[...] += dgnb_sum

    rev = lambda n: n_chunks - 1 - n
    out = pl.BlockSpec((CHUNK, _HPS * HEAD), lambda n, h: (rev(n), h))
    row = pl.BlockSpec((1, HEAD), lambda n, h: (0, 0))
    return pl.pallas_call(
        body, name="gdn_bwd", grid=(n_chunks, heads // _HPS),
        in_specs=_gdn_specs(rev)
        + [pl.BlockSpec((None, _HPS, HEAD, HEAD), lambda n, h: (rev(n), h, 0, 0)),
           pl.BlockSpec((CHUNK, _HPS * HEAD), lambda n, h: (rev(n), 8 // _HPS + h))],
        out_specs=[out, out, out, out, pl.BlockSpec((CHUNK, HEAD), lambda n, h: (rev(n), 0)), row, row, row],
        out_shape=[jax.ShapeDtypeStruct((S, 1024), F32)] * 3
        + [jax.ShapeDtypeStruct((S, 1024), BF16), jax.ShapeDtypeStruct((S, HEAD), BF16)]
        + [jax.ShapeDtypeStruct((1, HEAD), F32)] * 3,
        scratch_shapes=[pltpu.VMEM((heads // _HPS, _HPS, HEAD, HEAD), F32), pltpu.VMEM((CHUNK, HEAD), F32)],
        compiler_params=_cparams(("arbitrary", "arbitrary")),
    )(act, act, act, proj, proj, arow, dtrow, gnb, states, d_o)


def _conv_silu(x, w):
    t = _iota((x.shape[0], 1), 0)
    tap = _iota(w.shape, 0)
    y = jnp.zeros_like(x)
    for r in range(4):
        w_r = jnp.sum(jnp.where(tap == 3 - r, w, 0.0), axis=0, keepdims=True)
        y = y + (x if r == 0 else jnp.where(t >= r, _roll_rows(x, r), 0.0)) * w_r
    return jax.nn.silu(y)


_CONV_COLS = 128


def _conv_fwd(proj, conv_w):
    S = proj.shape[0]
    nb = 3072 // _CONV_COLS
    off = 4096 // _CONV_COLS

    def body(x_ref, w_ref, o_ref):
        o_ref[...] = _conv_silu(x_ref[...], w_ref[...])

    return pl.pallas_call(
        body, name="conv_fwd", grid=(nb,),
        in_specs=[pl.BlockSpec((S, _CONV_COLS), lambda j: (0, off + j)), pl.BlockSpec((4, _CONV_COLS), lambda j: (0, j))],
        out_specs=pl.BlockSpec((S, _CONV_COLS), lambda j: (0, j)),
        out_shape=jax.ShapeDtypeStruct((S, 3072), F32),
        compiler_params=_cparams(("parallel",)),
    )(proj, conv_w)


def _conv_bwd(proj, conv_w, dq, dk, dv):
    S = proj.shape[0]
    nb = 3072 // _CONV_COLS
    off = 4096 // _CONV_COLS
    per = 1024 // _CONV_COLS

    def body(x_ref, w_ref, dq_ref, dk_ref, dv_ref, dx_ref, dw_ref):
        j = pl.program_id(0)
        _, vjp = jax.vjp(_conv_silu, x_ref[...], w_ref[...])
        d = jnp.where(j < per, dq_ref[...], jnp.where(j < 2 * per, dk_ref[...], dv_ref[...]))
        dx, dw = vjp(d)
        dx_ref[...] = dx.astype(dx_ref.dtype)
        dw_ref[...] = dw

    dsp = lambda s: pl.BlockSpec((S, _CONV_COLS), lambda j, s=s: (0, jnp.clip(j - s * per, 0, per - 1)))
    return pl.pallas_call(
        body, name="conv_bwd", grid=(nb,),
        in_specs=[pl.BlockSpec((S, _CONV_COLS), lambda j: (0, off + j)), pl.BlockSpec((4, _CONV_COLS), lambda j: (0, j)),
                  dsp(0), dsp(1), dsp(2)],
        out_specs=[pl.BlockSpec((S, _CONV_COLS), lambda j: (0, j)), pl.BlockSpec((4, _CONV_COLS), lambda j: (0, j))],
        out_shape=[jax.ShapeDtypeStruct((S, 3072), BF16), jax.ShapeDtypeStruct((4, 3072), F32)],
        compiler_params=_cparams(("parallel",)),
    )(proj, conv_w, dq, dk, dv)


_S5_T = 512
_S5_L = 512
_S5_NB = 16
_S5_U = 4
_S5_UB = 3


def _s5_tile(xr, xi, pw_r, pw_i, cr, ci, reverse):
    t8 = _iota((8, 1), 0)
    for sh in (1, 2, 4):
        row = (8 - sh) if reverse else (sh - 1)
        ar, ai = pw_r[row:row + 1, :], pw_i[row:row + 1, :]
        if reverse:
            keep, amt = t8 < 8 - sh, 8 - sh
        else:
            keep, amt = t8 >= sh, sh
        sr = jnp.where(keep, pltpu.roll(xr, amt, 0), 0.0)
        si = jnp.where(keep, pltpu.roll(xi, amt, 0), 0.0)
        xr, xi = xr + ar * sr - ai * si, xi + ar * si + ai * sr
    xr, xi = xr + pw_r * cr - pw_i * ci, xi + pw_r * ci + pw_i * cr
    return xr, xi


def _s5_fwd(u, bre, bim, cre, cim, pw_r, pw_i, dskip):
    S = u.shape[0]
    T = min(_S5_T, S)
    nt = S // T

    def body(u_ref, bre_ref, bim_ref, cre_ref, cim_ref, pr_ref, pi_ref, d_ref, y_ref, xr_ref, xi_ref,
             bu_r, bu_i, car_r, car_i):
        @pl.when(pl.program_id(1) == 0)
        def _():
            car_r[...] = jnp.zeros_like(car_r)
            car_i[...] = jnp.zeros_like(car_i)

        uu = u_ref[...]
        bu_r[...] = _dot(uu, bre_ref[...])
        bu_i[...] = _dot(uu, bim_ref[...])
        pw_r, pw_i = pr_ref[...], pi_ref[...]

        def tiles(i, carry):
            ins = []
            for k in range(_S5_U):
                r0 = pl.multiple_of((i * _S5_U + k) * 8, 8)
                ins.append((r0, bu_r[pl.ds(r0, 8), :], bu_i[pl.ds(r0, 8), :]))
            outs = []
            for r0, br, bi in ins:
                xr, xi = _s5_tile(br, bi, pw_r, pw_i, carry[0], carry[1], False)
                carry = (xr[7:8, :], xi[7:8, :])
                outs.append((r0, xr, xi))
            for r0, xr, xi in outs:
                xr_ref[pl.ds(r0, 8), :] = xr
                xi_ref[pl.ds(r0, 8), :] = xi
            return carry

        cr, ci = lax.fori_loop(0, T // (8 * _S5_U), tiles, (car_r[...], car_i[...]))
        car_r[...] = cr
        car_i[...] = ci
        y_ref[...] = _dot(xr_ref[...], cre_ref[...]) - _dot(xi_ref[...], cim_ref[...]) + d_ref[...] * uu

    blk3 = lambda a, b: pl.BlockSpec((None, a, b), lambda j, t: (j, 0, 0))
    return pl.pallas_call(
        body, name="s5_fwd", grid=(_S5_NB, nt),
        in_specs=[pl.BlockSpec((T, HEAD), lambda j, t: (t, j)),
                  blk3(HEAD, _S5_L), blk3(HEAD, _S5_L), blk3(_S5_L, HEAD), blk3(_S5_L, HEAD),
                  blk3(8, _S5_L), blk3(8, _S5_L), pl.BlockSpec((1, HEAD), lambda j, t: (0, j))],
        out_specs=[pl.BlockSpec((T, HEAD), lambda j, t: (t, j)),
                   pl.BlockSpec((T, _S5_L), lambda j, t: (t, j)), pl.BlockSpec((T, _S5_L), lambda j, t: (t, j))],
        out_shape=[jax.ShapeDtypeStruct((S, D_MODEL), F32),
                   jax.ShapeDtypeStruct((S, _S5_NB * _S5_L), F32), jax.ShapeDtypeStruct((S, _S5_NB * _S5_L), F32)],
        scratch_shapes=[pltpu.VMEM((T, _S5_L), F32), pltpu.VMEM((T, _S5_L), F32),
                        pltpu.VMEM((1, _S5_L), F32), pltpu.VMEM((1, _S5_L), F32)],
        compiler_params=_cparams(("parallel", "arbitrary")),
    )(u, bre, bim, cre, cim, pw_r, pw_i, dskip)


def _s5_bwd(dy, u, xre, xim, bre, bim, cre, cim, qw_r, qw_i, dskip):
    S = u.shape[0]
    T = min(_S5_T, S)
    nt = S // T
    nt8 = T // 8

    def body(dy_ref, u_ref, xr_ref, xi_ref, xpr_ref, xpi_ref, bre_ref, bim_ref, cre_ref, cim_ref, qr_ref, qi_ref,
             d_ref, du_ref, dbr_ref, dbi_ref, dcr_ref, dci_ref, dlr_ref, dli_ref, dd_ref,
             g_r, g_i, car_r, car_i):
        t = pl.program_id(1)

        @pl.when(t == 0)
        def _():
            car_r[...] = jnp.zeros_like(car_r)
            car_i[...] = jnp.zeros_like(car_i)
            for r in (dbr_ref, dbi_ref, dcr_ref, dci_ref, dlr_ref, dli_ref, dd_ref):
                r[...] = jnp.zeros_like(r)

        dyy, uu = dy_ref[...], u_ref[...]
        g_r[...] = _dot(dyy, cre_ref[...], _NT)
        g_i[...] = -_dot(dyy, cim_ref[...], _NT)
        qw_r, qw_i = qr_ref[...], qi_ref[...]
        t8 = _iota((8, 1), 0)
        first = t == nt - 1

        def load(r0, prev_r, prev_i):
            rows = pl.ds(r0, 8)
            return r0, g_r[rows, :], g_i[rows, :], xr_ref[rows, :], xi_ref[rows, :], prev_r, prev_i

        def run(loaded, carry, acc):
            done = []
            for r0, dr, di, xr, xi, prev_r, prev_i in loaded:
                gr, gi = _s5_tile(dr, di, qw_r, qw_i, carry[0], carry[1], True)
                carry = (gr[0:1, :], gi[0:1, :])
                xpr = jnp.where(t8 >= 1, pltpu.roll(xr, 1, 0), prev_r)
                xpi = jnp.where(t8 >= 1, pltpu.roll(xi, 1, 0), prev_i)
                acc = (acc[0] + gr * xpr + gi * xpi, acc[1] + gi * xpr - gr * xpi)
                done.append((r0, gr, gi))
            for r0, gr, gi in done:
                g_r[pl.ds(r0, 8), :] = gr
                g_i[pl.ds(r0, 8), :] = gi
            return carry, acc

        def step(ii, state):
            loaded = []
            for k in range(_S5_UB):
                idx = nt8 - 1 - (ii * _S5_UB + k)
                r0 = pl.multiple_of(idx * 8, 8)
                p0 = pl.multiple_of((idx - 1) * 8, 8)
                loaded.append(load(r0, xr_ref[pl.ds(p0, 8), :][7:8, :], xi_ref[pl.ds(p0, 8), :][7:8, :]))
            return run(loaded, *state)

        zero = jnp.zeros((8, _S5_L), F32)
        assert (nt8 - 1) % _S5_UB == 0
        state = lax.fori_loop(0, (nt8 - 1) // _S5_UB, step, ((car_r[...], car_i[...]), (zero, zero)))
        prev_r = jnp.where(first, 0.0, xpr_ref[...][7:8, :])
        prev_i = jnp.where(first, 0.0, xpi_ref[...][7:8, :])
        (cr, ci), (ar, ai) = run([load(0, prev_r, prev_i)], *state)
        car_r[...] = cr
        car_i[...] = ci
        dlr_ref[...] += ar
        dli_ref[...] += ai
        gr, gi = g_r[...], g_i[...]
        du_ref[...] = _dot(gr, bre_ref[...], _NT) + _dot(gi, bim_ref[...], _NT) + d_ref[...] * dyy
        dbr_ref[...] += _dot(uu, gr, _TN)
        dbi_ref[...] += _dot(uu, gi, _TN)
        dcr_ref[...] += _dot(xr_ref[...], dyy, _TN)
        dci_ref[...] -= _dot(xi_ref[...], dyy, _TN)
        dd_ref[...] += jnp.sum(dyy * uu, axis=0, keepdims=True)

    rev = lambda t: nt - 1 - t
    blk3 = lambda a, b: pl.BlockSpec((None, a, b), lambda j, t: (j, 0, 0))
    tl = pl.BlockSpec((T, HEAD), lambda j, t: (rev(t), j))
    xs = pl.BlockSpec((T, _S5_L), lambda j, t: (rev(t), j))
    xp = pl.BlockSpec((8, _S5_L), lambda j, t: (jnp.maximum(rev(t) * nt8 - 1, 0), j))
    return pl.pallas_call(
        body, name="s5_bwd", grid=(_S5_NB, nt),
        in_specs=[tl, tl, xs, xs, xp, xp, blk3(HEAD, _S5_L), blk3(HEAD, _S5_L), blk3(_S5_L, HEAD), blk3(_S5_L, HEAD),
                  blk3(8, _S5_L), blk3(8, _S5_L), pl.BlockSpec((1, HEAD), lambda j, t: (0, j))],
        out_specs=[tl, blk3(HEAD, _S5_L), blk3(HEAD, _S5_L), blk3(_S5_L, HEAD), blk3(_S5_L, HEAD),
                   blk3(8, _S5_L), blk3(8, _S5_L), pl.BlockSpec((1, HEAD), lambda j, t: (0, j))],
        out_shape=[jax.ShapeDtypeStruct((S, D_MODEL), F32),
                   jax.ShapeDtypeStruct((_S5_NB, HEAD, _S5_L), F32), jax.ShapeDtypeStruct((_S5_NB, HEAD, _S5_L), F32),
                   jax.ShapeDtypeStruct((_S5_NB, _S5_L, HEAD), F32), jax.ShapeDtypeStruct((_S5_NB, _S5_L, HEAD), F32),
                   jax.ShapeDtypeStruct((_S5_NB, 8, _S5_L), F32), jax.ShapeDtypeStruct((_S5_NB, 8, _S5_L), F32),
                   jax.ShapeDtypeStruct((1, D_MODEL), F32)],
        scratch_shapes=[pltpu.VMEM((T, _S5_L), F32), pltpu.VMEM((T, _S5_L), F32),
                        pltpu.VMEM((1, _S5_L), F32), pltpu.VMEM((1, _S5_L), F32)],
        compiler_params=_cparams(("parallel", "arbitrary")),
    )(dy, u, xre, xim, xre, xim, bre, bim, cre, cim, qw_r, qw_i, dskip)


def _s5_params(a_re, a_im, log_dt, b_re, b_im, c_re, c_im):
    step = jnp.exp(log_dt)[:, None]
    mag = jnp.exp(a_re * step)
    lr, li = mag * jnp.cos(a_im * step), mag * jnp.sin(a_im * step)
    den = a_re * a_re + a_im * a_im
    nr, ni = lr - 1.0, li
    kr, ki = (nr * a_re + ni * a_im) / den, (ni * a_re - nr * a_im) / den
    bbr = kr[..., None] * b_re - ki[..., None] * b_im
    bbi = kr[..., None] * b_im + ki[..., None] * b_re
    eye = jnp.eye(8, dtype=F32)

    def blk_b(bb):
        t = bb.reshape(_S5_NB, 8, 64, 16).transpose(0, 1, 3, 2)
        return (t[:, :, :, None, :] * eye[None, :, None, :, None]).reshape(_S5_NB, HEAD, _S5_L)

    def blk_c(cc):
        t = cc.reshape(_S5_NB, 8, 16, 64).transpose(0, 1, 3, 2)
        return (t[:, :, :, None, :] * eye[None, :, None, :, None]).reshape(_S5_NB, _S5_L, HEAD)

    return (blk_b(bbr), blk_b(bbi), blk_c(c_re), blk_c(c_im),
            lr.reshape(_S5_NB, _S5_L), li.reshape(_S5_NB, _S5_L))


def _s5_powers(lr, li):
    pr, pi = [lr], [li]
    for _ in range(7):
        pr, pi = pr + [pr[-1] * lr - pi[-1] * li], pi + [pr[-1] * li + pi[-1] * lr]
    return jnp.stack(pr, axis=1), jnp.stack(pi, axis=1)


_MESH = pl.DeviceIdType.MESH
_ANY = pl.BlockSpec(memory_space=pl.ANY)


def _place():
    x, y, c = lax.axis_index("x"), lax.axis_index("y"), lax.axis_index("c")
    return x, y, c, [(1 - x, y), (x, 1 - y), (1 - x, 1 - y)]


def _comm_call(body, arrs, out_shapes, n_remote, name):
    n = len(arrs)
    return pl.pallas_call(
        body, name=name,
        in_specs=[_ANY] * n, out_specs=[_ANY] * n, out_shape=out_shapes,
        scratch_shapes=[pltpu.SemaphoreType.DMA((n * n_remote,)), pltpu.SemaphoreType.DMA((n * n_remote,)),
                        pltpu.SemaphoreType.DMA((n,))],
    )(*arrs)


def _own_slot(shard, me_idx, name, rows=256):
    _, half, C = shard.shape
    rows = min(rows, half)
    assert half % rows == 0

    def body(me_ref, in_ref, o_ref):
        o_ref[...] = in_ref[...]

    return pl.pallas_call(
        body, name=name,
        grid_spec=pltpu.PrefetchScalarGridSpec(
            num_scalar_prefetch=1, grid=(2, half // rows),
            in_specs=[pl.BlockSpec((None, rows, C), lambda h, i, me: (h, i, 0))],
            out_specs=pl.BlockSpec((None, None, rows, C), lambda h, i, me: (me[0], h, i, 0))),
        out_shape=jax.ShapeDtypeStruct((4,) + shard.shape, shard.dtype),
        compiler_params=_cparams(("parallel", "parallel")),
    )(me_idx, shard)


def _ag_xy(bufs, name):
    n = len(bufs)

    def body(*refs):
        outs, (ssem, rsem) = refs[n:2 * n], refs[2 * n:]
        x, y, c, chips = _place()
        me, sib = 2 * x + y, (x, y, 1 - c)

        def copy(a, k, src, dst, to):
            return pltpu.make_async_remote_copy(src_ref=src, dst_ref=dst, send_sem=ssem.at[6 * a + k],
                                                recv_sem=rsem.at[6 * a + k], device_id=to, device_id_type=_MESH)

        sends = []
        for j, (px, py) in enumerate(chips):
            for a in range(n):
                cp = copy(a, j, outs[a].at[me, c], outs[a].at[me, c], (px, py, c))
                cp.start()
                sends.append(cp)
        for j, (px, py) in enumerate(chips):
            for a in range(n):
                landed = outs[a].at[2 * px + py, c]
                copy(a, j, landed, landed, (px, py, c)).wait_recv()
                fwd = copy(a, 3 + j, landed, landed, sib)
                fwd.start()
                sends.append(fwd)
        for j, (px, py) in enumerate(chips):
            for a in range(n):
                passed = outs[a].at[2 * px + py, 1 - c]
                copy(a, 3 + j, passed, passed, sib).wait_recv()
        for cp in sends:
            cp.wait_send()

    return pl.pallas_call(
        body, name=name,
        in_specs=[_ANY] * n, out_specs=[_ANY] * n,
        out_shape=[jax.ShapeDtypeStruct(b.shape, b.dtype) for b in bufs],
        input_output_aliases={a: a for a in range(n)},
        scratch_shapes=[pltpu.SemaphoreType.DMA((6 * n,)), pltpu.SemaphoreType.DMA((6 * n,))],
    )(*bufs)


def _pair_split(arrs, name):
    n = len(arrs)

    def body(*refs):
        ins, outs, (ssem, rsem, _) = refs[:n], refs[n:2 * n], refs[2 * n:]
        x, y, c, _ = _place()
        copies = []
        for a in range(n):
            for s in range(4):
                cp = pltpu.make_async_remote_copy(src_ref=ins[a].at[s, 1 - c], dst_ref=outs[a].at[s],
                                                  send_sem=ssem.at[4 * a + s], recv_sem=rsem.at[4 * a + s],
                                                  device_id=(x, y, 1 - c), device_id_type=_MESH)
                cp.start()
                copies.append(cp)
        for cp in copies:
            cp.wait()

    shapes = [jax.ShapeDtypeStruct((4,) + a.shape[2:], a.dtype) for a in arrs]
    return _comm_call(body, arrs, shapes, 4, name)


def _pair_swap(arrs, name):
    n = len(arrs)

    def body(*refs):
        ins, outs, (ssem, rsem, _) = refs[:n], refs[n:2 * n], refs[2 * n:]
        x, y, c, _ = _place()
        copies = []
        for a in range(n):
            cp = pltpu.make_async_remote_copy(src_ref=ins[a], dst_ref=outs[a], send_sem=ssem.at[a], recv_sem=rsem.at[a],
                                              device_id=(x, y, 1 - c), device_id_type=_MESH)
            cp.start()
            copies.append(cp)
        for cp in copies:
            cp.wait()

    return _comm_call(body, arrs, [jax.ShapeDtypeStruct(a.shape, a.dtype) for a in arrs], 1, name)


def _pair_sum(full, recv, c_idx, name, rows=128):
    _, _, half, C = full.shape
    rows = min(rows, half)
    nb = half // rows
    assert half % rows == 0

    def body(c_ref, a_ref, b_ref, o_ref):
        o_ref[...] = (a_ref[...].astype(F32) + b_ref[...].astype(F32)).astype(o_ref.dtype)

    return pl.pallas_call(
        body, name=name,
        grid_spec=pltpu.PrefetchScalarGridSpec(
            num_scalar_prefetch=1, grid=(4, nb),
            in_specs=[pl.BlockSpec((None, None, rows, C), lambda s, i, c: (s, c[0], i, 0)),
                      pl.BlockSpec((None, rows, C), lambda s, i, c: (s, i, 0))],
            out_specs=pl.BlockSpec((None, rows, C), lambda s, i, c: (s, i, 0))),
        out_shape=jax.ShapeDtypeStruct(recv.shape, recv.dtype),
        compiler_params=_cparams(("parallel", "parallel")),
    )(c_idx, full, recv)


_HBM = pl.BlockSpec(memory_space=pltpu.HBM)
_SEM = pl.BlockSpec(memory_space=pltpu.SEMAPHORE)
_SPLIT = dict(has_side_effects=pltpu.SideEffectType.DATAFLOW_SIDE_EFFECTING)


def _hbm(t):
    return pltpu.with_memory_space_constraint(t, pltpu.HBM)


def _ag_copy(buf_ref, ssem, rsem, j, chip, c, me):
    px, py = chip
    return pltpu.make_async_remote_copy(src_ref=buf_ref.at[me, c], dst_ref=buf_ref.at[me, c], send_sem=ssem.at[j],
                                        recv_sem=rsem.at[j], device_id=(px, py, c), device_id_type=_MESH)


def _ag_start(buf, after, name):
    def body(buf_ref, after_ref, ssem, rsem, thru, token):
        x, y, c, chips = _place()
        for j, chip in enumerate(chips):
            _ag_copy(buf_ref, ssem, rsem, j, chip, c, 2 * x + y).start()
        token[...] = jnp.zeros_like(token)

    return pl.pallas_call(
        body, name=name,
        out_shape=(pltpu.SemaphoreType.DMA((3,)), pltpu.SemaphoreType.DMA((3,)), pltpu.HBM(buf.shape, buf.dtype),
                   jax.ShapeDtypeStruct((8, HEAD), F32)),
        in_specs=(_HBM, _ANY), out_specs=(_SEM, _SEM, _HBM, pl.BlockSpec(memory_space=pltpu.VMEM)),
        input_output_aliases={0: 2}, compiler_params=pltpu.CompilerParams(**_SPLIT),
    )(_hbm(buf), after)


def _ag_wait(ssem, rsem, buf, after, name):
    def body(buf_ref, ssem, rsem, after_ref, out_ref):
        x, y, c, chips = _place()
        for j, chip in enumerate(chips):
            cp = _ag_copy(buf_ref, ssem, rsem, j, chip, c, 2 * x + y)
            cp.wait_send()
            cp.wait_recv()

    return pl.pallas_call(
        body, name=name, out_shape=(pltpu.HBM(buf.shape, buf.dtype),),
        in_specs=(_HBM, _SEM, _SEM, _ANY), out_specs=(_HBM,),
        input_output_aliases={0: 0}, compiler_params=pltpu.CompilerParams(**_SPLIT),
    )(buf, ssem, rsem, after)[0]


def _pair_forward(buf, name):
    def body(in_ref, out_ref, ssem, rsem):
        x, y, c, chips = _place()
        copies = []
        for j, (px, py) in enumerate(chips):
            cp = pltpu.make_async_remote_copy(src_ref=out_ref.at[2 * px + py, c], dst_ref=out_ref.at[2 * px + py, c],
                                              send_sem=ssem.at[j], recv_sem=rsem.at[j], device_id=(x, y, 1 - c),
                                              device_id_type=_MESH)
            cp.start()
            copies.append(cp)
        for cp in copies:
            cp.wait()

    return pl.pallas_call(
        body, name=name, in_specs=[_ANY], out_specs=_ANY, out_shape=jax.ShapeDtypeStruct(buf.shape, buf.dtype),
        input_output_aliases={0: 0},
        scratch_shapes=[pltpu.SemaphoreType.DMA((3,)), pltpu.SemaphoreType.DMA((3,))],
    )(buf)


def _rs_copy(src_ref, land_ref, ssem, rsem, j, chip, c):
    px, py = chip
    return pltpu.make_async_remote_copy(src_ref=src_ref.at[2 * px + py], dst_ref=land_ref.at[j], send_sem=ssem.at[j],
                                        recv_sem=rsem.at[j], device_id=(px, py, c), device_id_type=_MESH)


def _rs_start(part, after, name):
    def body(part_ref, land_ref, after_ref, ssem, rsem, part_thru, land_thru, token):
        x, y, c, chips = _place()
        for j, chip in enumerate(chips):
            _rs_copy(part_ref, land_ref, ssem, rsem, j, chip, c).start()
        token[...] = jnp.zeros_like(token)

    land = jax.ShapeDtypeStruct((3,) + part.shape[1:], part.dtype)
    return pl.pallas_call(
        body, name=name,
        out_shape=(pltpu.SemaphoreType.DMA((3,)), pltpu.SemaphoreType.DMA((3,)), pltpu.HBM(part.shape, part.dtype),
                   pltpu.HBM(land.shape, land.dtype), jax.ShapeDtypeStruct((8, HEAD), F32)),
        in_specs=(_HBM, _HBM, _ANY), out_specs=(_SEM, _SEM, _HBM, _HBM, pl.BlockSpec(memory_space=pltpu.VMEM)),
        input_output_aliases={0: 2, 1: 3}, compiler_params=pltpu.CompilerParams(**_SPLIT),
    )(_hbm(part), _hbm(lax.empty(land.shape, land.dtype)), after)


def _rs_wait(ssem, rsem, part, land, after, name):
    def body(part_ref, land_ref, ssem, rsem, after_ref, part_out, land_out):
        x, y, c, chips = _place()
        for j, chip in enumerate(chips):
            cp = _rs_copy(part_ref, land_ref, ssem, rsem, j, chip, c)
            cp.wait_send()
            cp.wait_recv()

    return pl.pallas_call(
        body, name=name, out_shape=(pltpu.HBM(part.shape, part.dtype), pltpu.HBM(land.shape, land.dtype)),
        in_specs=(_HBM, _HBM, _SEM, _SEM, _ANY), out_specs=(_HBM, _HBM),
        input_output_aliases={0: 0, 1: 1}, compiler_params=pltpu.CompilerParams(**_SPLIT),
    )(part, land, ssem, rsem, after)


def _sum_own_recv(part, land, me_idx, name, rows=128):
    _, half, C = part.shape
    rows = min(rows, half)
    assert half % rows == 0

    def body(me_ref, own_ref, land_ref, o_ref):
        acc = own_ref[...].astype(F32)
        for s in range(3):
            acc = acc + land_ref[s].astype(F32)
        o_ref[...] = acc

    return pl.pallas_call(
        body, name=name,
        grid_spec=pltpu.PrefetchScalarGridSpec(
            num_scalar_prefetch=1, grid=(half // rows,),
            in_specs=[pl.BlockSpec((None, rows, C), lambda i, me: (me[0], i, 0)),
                      pl.BlockSpec((3, rows, C), lambda i, me: (0, i, 0))],
            out_specs=pl.BlockSpec((rows, C), lambda i, me: (i, 0))),
        out_shape=jax.ShapeDtypeStruct((half, C), F32),
        compiler_params=_cparams(("parallel",)),
    )(me_idx, part, land)


def _gather8(buf, name):
    def body(in_ref, out_ref, ssem, rsem, lsem):
        x, y, c, _ = _place()
        me = 4 * x + 2 * y + c
        loc = pltpu.make_async_copy(in_ref, out_ref.at[me], lsem.at[0])
        loc.start()
        copies = [loc]
        for d in range(1, 8):
            dx, dy, dc = (d >> 2) & 1, (d >> 1) & 1, d & 1
            peer = (1 - x if dx else x, 1 - y if dy else y, 1 - c if dc else c)
            cp = pltpu.make_async_remote_copy(src_ref=in_ref, dst_ref=out_ref.at[me], send_sem=ssem.at[d - 1],
                                              recv_sem=rsem.at[d - 1], device_id=peer, device_id_type=_MESH)
            cp.start()
            copies.append(cp)
        for cp in copies:
            cp.wait()

    return _comm_call(body, [buf], [jax.ShapeDtypeStruct((8,) + buf.shape, buf.dtype)], 7, name)[0]


def _sum_slots(arr, name, rows=128):
    k, R, C = arr.shape
    rows = min(rows, R)
    assert R % rows == 0

    def body(in_ref, o_ref):
        acc = in_ref[0].astype(F32)
        for s in range(1, k):
            acc = acc + in_ref[s].astype(F32)
        o_ref[...] = acc

    return pl.pallas_call(
        body, name=name, grid=(R // rows,),
        in_specs=[pl.BlockSpec((k, rows, C), lambda i: (0, i, 0))],
        out_specs=pl.BlockSpec((rows, C), lambda i: (i, 0)),
        out_shape=jax.ShapeDtypeStruct((R, C), F32),
        compiler_params=_cparams(("parallel",)),
    )(arr)


ADAM_LR, ADAM_B1, ADAM_B2, ADAM_EPS, ADAM_WD, ADAM_STEP = 0.001, 0.9, 0.999, 1e-08, 0.01, 10


def _adam_math(w, g, m, v):
    m = ADAM_B1 * m + (1.0 - ADAM_B1) * g
    v = ADAM_B2 * v + (1.0 - ADAM_B2) * jnp.square(g)
    m_hat = m / (1.0 - ADAM_B1 ** ADAM_STEP)
    v_hat = v / (1.0 - ADAM_B2 ** ADAM_STEP)
    delta = -ADAM_LR * (m_hat / (jnp.sqrt(v_hat) + ADAM_EPS) + ADAM_WD * w)
    return delta, m, v


def _adamw(w, m, v, layer, mine, other, c_idx, name, g_off=0, prev=None, rows=128):
    _, R, C = w.shape
    half = mine.shape[0]
    rows = min(rows, R)
    assert R % rows == 0 and g_off % rows == 0 and half % rows == 0
    nbh, b0 = half // rows, g_off // rows

    def body(c_ref, w_ref, m_ref, v_ref, mine_ref, other_ref, *rest):
        go, do, mo, vo = rest[-4:]
        in_my_half = (b0 + pl.program_id(0)) // nbh == c_ref[0]
        g = jnp.where(in_my_half, mine_ref[...], other_ref[...])
        delta, m1, v1 = _adam_math(w_ref[...], g, m_ref[...], v_ref[...])
        go[...] = g
        do[...] = delta
        mo[...] = m1
        vo[...] = v1

    blk = pl.BlockSpec((None, rows, C), lambda i, c: (layer, i, 0))
    gblk = pl.BlockSpec((rows, C), lambda i, c: ((b0 + i) % nbh, 0))
    carried = list(prev) if prev is not None else []
    return pl.pallas_call(
        body, name=name,
        grid_spec=pltpu.PrefetchScalarGridSpec(
            num_scalar_prefetch=1, grid=(R // rows,),
            in_specs=[blk] * 3 + [gblk] * 2 + [_ANY] * len(carried), out_specs=[blk] * 4),
        out_shape=[jax.ShapeDtypeStruct(w.shape, F32)] * 4,
        input_output_aliases={6 + k: k for k in range(len(carried))},
        compiler_params=_cparams(("parallel",)),
    )(c_idx, w, m, v, mine, other, *carried)


_WEIGHTS = ['norm_mix', 'norm_mlp', 'norm_ple', 'w_in_e', 'w_out_e', 'hgrn_lb', 'g_norm_a', 'conv_w', 'a_log',
            'dt_bias', 'g_norm_b', 's5_a_re', 's5_a_im', 's5_b_re', 's5_b_im', 's5_c_re', 's5_c_im', 's5_d',
            's5_log_dt', 'w_glu', 'b_glu', 'w_out_o', 'w_up', 'w_down', 'w_ple_gate', 'w_ple_proj', 'final_norm']
_INPUTS = ['x', 'p'] + _WEIGHTS + ['loss_target'] + ['m_' + n for n in _WEIGHTS] + ['v_' + n for n in _WEIGHTS]
_FAMILY = [('w_up', 0, 'col', 2048), ('w_down', 0, 'row', 2048), ('w_ple_gate', 0, 'row', 512), ('w_out_e', 0, 'row', 512),
           ('w_up', 1, 'col', 2048), ('w_down', 1, 'row', 2048), ('w_ple_gate', 1, 'row', 512), ('w_glu', 0, 'row', 512),
           ('w_out_o', 0, 'row', 512)]
_PACK = {('w_up', 0): 0, ('w_down', 0): 0, ('w_ple_gate', 0): 0, ('w_out_e', 0): 0,
         ('w_up', 1): 1, ('w_down', 1): 1, ('w_ple_gate', 1): 1, ('w_glu', 0): 1, ('w_out_o', 0): 1}
_IN_PAD = 8320
_IN_BLK = 1664


def _rms_bwd_fn(x, d_hn, d_res, g):
    _, vjp = jax.vjp(_rms, x, g)
    dx, dg = vjp(d_hn.astype(F32))
    return dx + d_res, dg


def _rms_bwd_both(x, d_hn, d_res, g):
    dx, dg = _rms_bwd_fn(x, d_hn, d_res, g)
    return dx, dx, dg


def _add_res(acc, h):
    return (acc + h,)


def _pack_rows(parts, lanes=128, mult=256):
    flat = jnp.concatenate([q.reshape(-1).astype(F32) for q in parts])
    n = flat.shape[0]
    rows = -(-n // (lanes * mult)) * mult
    return jnp.pad(flat, (0, rows * lanes - n)).reshape(rows, lanes)


def _unpack_rows(buf, shapes):
    flat, out, off = buf.reshape(-1), [], 0
    for s in shapes:
        n = math.prod(s)
        out.append(flat[off:off + n].reshape(s))
        off += n
    return out


def _step(a):
    S = a['x'].shape[1]
    x, tgt = a['x'][0], a['loss_target'][0]
    xi, yi = lax.axis_index("x"), lax.axis_index("y")
    me = 2 * xi + yi
    row = lambda t: t.reshape(1, -1)

    views, pack_of, packs, rt = {}, {}, [[], []], [0, 0]
    for nm, l, kind, r in _FAMILY:
        views[nm, l], pack_of[nm, l] = (kind, rt[_PACK[nm, l]], r), _PACK[nm, l]
        packs[_PACK[nm, l]].append(a[nm][l].astype(BF16))
        rt[_PACK[nm, l]] += r
    small_sh = jnp.concatenate([a['conv_w'][0].reshape(-1), a['s5_d'][0], a['b_glu'][0]]).reshape(16, 256)
    shards = [jnp.concatenate(packs[0], axis=0), jnp.concatenate(packs[1], axis=0), a['w_in_e'][0].astype(BF16),
              a['w_ple_proj'].reshape(512, 512).astype(BF16), small_sh]
    me_idx = me.astype(jnp.int32).reshape(1)
    c_idx = lax.axis_index("c").astype(jnp.int32).reshape(1)
    slots = [_own_slot(t.reshape(2, t.shape[0] // 2, t.shape[1]), me_idx, name=f"own_slot{i}")
             for i, t in enumerate(shards)]
    whole = lambda g, t: g.reshape((4,) + t.shape)
    g_in, g_pp, g_small = [whole(g, t) for g, t in zip(_ag_xy(slots[2:], name="gather_first"), shards[2:])]
    gw = {}
    W = lambda nm, l: dict(b=gw[pack_of[nm, l]], b_view=views[nm, l])
    sems0 = _ag_start(slots[0], g_in, name="gather0_start")
    win = jnp.pad(g_in.transpose(1, 0, 2).reshape(D_MODEL, 8208), ((0, 0), (0, _IN_PAD - 8208)))
    w_pp = g_pp.reshape(4, 2, 256, 512).transpose(1, 2, 0, 3).reshape(2, 256, D_MODEL)
    g_small = g_small.reshape(4, 4096)
    conv_full = g_small[:, :3072].reshape(4, 4, 768).transpose(1, 0, 2).reshape(4, 3072)
    s5_d_full = g_small[:, 3072:3584].reshape(1, D_MODEL)
    b_glu_full = g_small[:, 3584:].reshape(1, D_MODEL)

    lb, lb_vjp = jax.vjp(lambda hl: jax.nn.softmax(hl, axis=0)[0:1], a['hgrn_lb'])
    pad_h = lambda t: jnp.pad(t, ((0, 0), (0, HEAD - t.shape[1])))
    (arow, dtrow), gdn_vjp = jax.vjp(lambda al, dt: (pad_h(-jnp.exp(al)), pad_h(dt)), a['a_log'], a['dt_bias'])
    s5p, s5_vjp = jax.vjp(_s5_params, a['s5_a_re'][0], a['s5_a_im'][0], a['s5_log_dt'][0], a['s5_b_re'][0],
                          a['s5_b_im'][0], a['s5_c_re'][0], a['s5_c_im'][0])
    bre, bim, cre, cim = [t.astype(BF16) for t in s5p[:4]]
    pw_r, pw_i = _s5_powers(s5p[4], s5p[5])
    gna, gnb = a['g_norm_a'], a['g_norm_b']

    def norm_cast(h, g, dt, name):
        return _rowwise(_rms, [h], [row(g)], [(D_MODEL, dt)], name=name)

    def mlp_ple_fwd(h, l):
        hn = norm_cast(h, a['norm_mlp'][l], BF16, f"rms_mlp{l}")
        up, act = _mm(hn, **W('w_up', l), out_dtypes=(BF16, BF16), name=f"up{l}",
                      epilogue=lambda acc: (acc, jnp.square(jnp.maximum(acc, 0.0))))
        h2 = _mm(act, **W('w_down', l), epilogue=_add_res, extras=(h,), name=f"down{l}")
        hnp = norm_cast(h2, a['norm_ple'][l], BF16, f"rms_ple{l}")
        pp = _mm(a['p'][l, 0], w_pp[l], name=f"ple_proj{l}")
        h3, gpre = _mm(hnp, **W('w_ple_gate', l), extras=(h2, pp), out_dtypes=(F32, F32), name=f"ple_gate{l}",
                       bm=512, epilogue=lambda acc, h2, pp: (h2 + jax.nn.sigmoid(acc) * pp, acc))
        return h3, (h, hn, up, act, h2, hnp, pp, gpre)

    hn0 = norm_cast(x, a['norm_mix'][0], BF16, "rms_mix0")
    proj = _mm(hn0, win, bn=_IN_BLK, name="in_proj", after=(sems0[3],))
    o_a, st_a = _hgrn_fwd(proj, lb, gna)
    act_b = _conv_fwd(proj, conv_full)
    o_b, st_b = _gdn_fwd(act_b, proj, arow, dtrow, gnb)
    merged = jnp.concatenate([o_a, o_b], axis=1)
    landed0 = _ag_wait(sems0[0], sems0[1], sems0[2], merged, name="gather0_wait")
    gw[0] = whole(_pair_forward(landed0, name="gather0_pass"), shards[0])
    sems1 = _ag_start(slots[1], gw[0], name="gather1_start")
    h1 = _mm(merged, **W('w_out_e', 0), epilogue=_add_res, extras=(x,), name="out_e", after=(sems1[3],))
    h3, saved0 = mlp_ple_fwd(h1, 0)

    u = norm_cast(h3, a['norm_mix'][1], F32, "rms_mix1")
    y, xre, xim = _s5_fwd(u, bre, bim, cre, cim, pw_r, pw_i, s5_d_full)
    actg = _rowwise(jax.nn.gelu, [y], [], [(D_MODEL, BF16)], name="gelu")
    landed1 = _ag_wait(sems1[0], sems1[1], sems1[2], actg, name="gather1_wait")
    gw[1] = whole(_pair_forward(landed1, name="gather1_pass"), shards[1])
    glu, z = _mm(actg, **W('w_glu', 0), extras=(actg, b_glu_full), out_dtypes=(BF16, F32), name="glu",
                 epilogue=lambda acc, act, b: (act.astype(F32) * jax.nn.sigmoid(acc + b), acc + b))
    h4 = _mm(glu, **W('w_out_o', 0), epilogue=_add_res, extras=(h3,), name="out_o")
    h6, saved1 = mlp_ple_fwd(h4, 1)

    def head(h, t, g):
        def f(h, g):
            e = _rms(h, g) - t
            return 0.5 * jnp.sum(jnp.mean(e * e, axis=-1))
        val, vjp = jax.vjp(f, h, g)
        dh, dg = vjp(jnp.ones((), F32))
        return dh, dg, jnp.full((1, HEAD), val)

    dh, d_final, loss_part = _rowwise(head, [h6, tgt], [row(a['final_norm'])], [(D_MODEL, F32)], sums=(D_MODEL, HEAD),
                                      name="loss_head")
    loss = lax.psum(loss_part[0, 0], ("x", "y", "c"))

    gb = {k: lax.empty((4, rt[k], PACK_COLS), BF16) for k in (0, 1)}

    def into(lhs, rhs, key, name, after=()):
        k = pack_of[key]
        gb[k] = _mm(lhs, rhs, dims="tn", out_into=gb[k], out_view=views[key], out_dtypes=(BF16,), name=name, after=after)

    def mlp_ple_bwd(dh, l, saved, after=()):
        h, hn, up, act, h2, hnp, pp, gpre = saved

        def gate_bwd(d, gpre, pp):
            s = jax.nn.sigmoid(gpre)
            return d * s, d * pp * s * (1.0 - s)

        d_pp, d_gp = _rowwise(gate_bwd, [dh, gpre, pp], [], [(D_MODEL, BF16), (D_MODEL, BF16)], name=f"ple_bwd{l}")
        d_wpp = _mm(a['p'][l, 0], d_pp, dims="tn", name=f"d_ple_proj{l}", after=after)
        into(hnp, d_gp, ('w_ple_gate', l), f"d_ple_gate{l}")
        d_hnp = _mm(d_gp, **W('w_ple_gate', l), dims="nt", name=f"ple_gate_t{l}")
        dh2, dh2_b, d_nple = _rowwise(_rms_bwd_both, [h2, d_hnp, dh], [row(a['norm_ple'][l])],
                                      [(D_MODEL, F32), (D_MODEL, BF16)], sums=(D_MODEL,), name=f"rms_ple_bwd{l}")
        d_up = _mm(dh2_b, **W('w_down', l), dims="nt", extras=(up,), out_dtypes=(BF16,), name=f"down_t{l}",
                   epilogue=lambda acc, up: (acc * 2.0 * jnp.maximum(up.astype(F32), 0.0),))
        into(act, dh2_b, ('w_down', l), f"d_down{l}")
        into(hn, d_up, ('w_up', l), f"d_up{l}")
        d_hn = _mm(d_up, **W('w_up', l), dims="nt", name=f"up_t{l}")
        dh1, dh1_b, d_nmlp = _rowwise(_rms_bwd_both, [h, d_hn, dh2], [row(a['norm_mlp'][l])],
                                      [(D_MODEL, F32), (D_MODEL, BF16)], sums=(D_MODEL,), name=f"rms_mlp_bwd{l}")
        return dh1, dh1_b, d_wpp, d_nple, d_nmlp

    dh4, dh4_b, d_wpp1, d_nple1, d_nmlp1 = mlp_ple_bwd(dh, 1, saved1)

    d_glu = _mm(dh4_b, **W('w_out_o', 0), dims="nt", name="out_o_t")
    into(glu, dh4_b, ('w_out_o', 0), "d_out_o")

    def glu_bwd(d, z, act):
        s, act = jax.nn.sigmoid(z), act.astype(F32)
        dz = d * act * s * (1.0 - s)
        return dz, d * s, jnp.sum(dz, axis=0, keepdims=True)

    d_z, d_actp, d_bglu = _rowwise(glu_bwd, [d_glu, z, actg], [], [(D_MODEL, BF16), (D_MODEL, F32)], sums=(D_MODEL,),
                                   name="glu_bwd")
    into(actg, d_z, ('w_glu', 0), "d_glu")

    def reduce_start(bufs, tag):
        parts = [t.reshape(4, 2, t.shape[1] // 2, t.shape[2]) for t in bufs]
        recv = _pair_split(parts, name=f"pair_split{tag}")
        sums = [_pair_sum(f, r, c_idx, name=f"pair_sum{tag}_{i}") for i, (f, r) in enumerate(zip(parts, recv))]
        return [_rs_start(s, c_idx, name=f"scatter{tag}_{i}_start") for i, s in enumerate(sums)]

    def reduce_finish(started, after, tag):
        halves = []
        for i, (ssem, rsem, part, land, _) in enumerate(started):
            part, land = _rs_wait(ssem, rsem, part, land, after, name=f"scatter{tag}_{i}_wait")
            halves.append(_sum_own_recv(part, land, me_idx, name=f"sum_partials{tag}_{i}"))
        return list(zip(halves, _pair_swap(halves, name=f"swap_halves{tag}")))

    started1 = reduce_start([gb[1]], 1)

    def gelu_bwd(acc, dap, y):
        _, vjp = jax.vjp(jax.nn.gelu, y)
        return vjp(acc + dap)

    dy = _mm(d_z, **W('w_glu', 0), dims="nt", extras=(d_actp, y), epilogue=gelu_bwd, name="glu_t",
             after=(started1[0][4],))
    du, d_bre, d_bim, d_cre, d_cim, d_lr, d_li, d_s5d = _s5_bwd(dy, u, xre, xim, bre, bim, cre, cim,
                                                                 pw_r[:, ::-1], -pw_i[:, ::-1], s5_d_full)
    dh3, d_nmix1 = _rowwise(_rms_bwd_fn, [h3, du, dh4], [row(a['norm_mix'][1])], [(D_MODEL, F32)], sums=(D_MODEL,),
                            name="rms_mix1_bwd")
    d_are, d_aim, d_logdt, d_sbre, d_sbim, d_scre, d_scim = s5_vjp(
        (d_bre, d_bim, d_cre, d_cim, d_lr.sum(axis=1), d_li.sum(axis=1)))

    dh1, dh1_b, d_wpp0, d_nple0, d_nmlp0 = mlp_ple_bwd(dh3, 0, saved0)

    d_merged = _mm(dh1_b, **W('w_out_e', 0), dims="nt", name="out_e_t")
    into(merged, dh1_b, ('w_out_e', 0), "d_out_e")
    dq, df, dv, dg, d_lb, d_gna = _hgrn_bwd(proj, lb, gna, st_a, d_merged)
    dqb, dkb, dvb, dzb, d_tail, d_arow, d_dtrow, d_gnb = _gdn_bwd(act_b, proj, arow, dtrow, gnb, st_b, d_merged)
    d_pre, d_conv = _conv_bwd(proj, conv_full, dqb, dkb, dvb)
    d_proj = jnp.concatenate([dq, df, dv, dg, d_pre, dzb, d_tail], axis=1)
    d_win = _mm(hn0, d_proj, dims="tn", bn=_IN_BLK, out_dtypes=(BF16,), name="d_in_proj")
    d_hn0 = _mm(d_proj, win, dims="nt", bk=_IN_BLK, name="in_proj_t")
    grad_x, d_nmix0 = _rowwise(_rms_bwd_fn, [x, d_hn0, dh1], [row(a['norm_mix'][0])], [(D_MODEL, F32)], sums=(D_MODEL,),
                               name="rms_mix0_bwd")
    (d_hlb,) = lb_vjp(d_lb.reshape(1, 1024))
    d_alog, d_dtb = gdn_vjp((d_arow, d_dtrow))

    d_win_sh = d_win[:, :8208].reshape(D_MODEL, 4, 2052).transpose(1, 0, 2)
    d_wpp_sh = jnp.stack([d_wpp0, d_wpp1]).astype(BF16).reshape(2, 256, 4, 512).transpose(2, 0, 1, 3).reshape(4, 512, 512)
    started0 = reduce_start([gb[0], d_win_sh, d_wpp_sh], 0)
    red = {1: reduce_finish(started1, started0[-1][4], 1)[0]}
    res = {}

    def adam_layer(nm, layer, key, g_off=None):
        g_off = views[nm, layer][1] if g_off is None else g_off
        res[nm] = tuple(_adamw(a[nm], a['m_' + nm], a['v_' + nm], layer, *red[key], c_idx, name=f"adam_{nm}{layer}",
                               g_off=g_off, prev=res.get(nm)))

    for nm, layer in (('w_glu', 0), ('w_out_o', 0), ('w_up', 1), ('w_down', 1), ('w_ple_gate', 1)):
        adam_layer(nm, layer, 1)

    small = {
        'norm_mix': jnp.concatenate([d_nmix0, d_nmix1]), 'norm_mlp': jnp.concatenate([d_nmlp0, d_nmlp1]),
        'norm_ple': jnp.concatenate([d_nple0, d_nple1]), 'hgrn_lb': d_hlb, 'g_norm_a': d_gna, 'conv_w': d_conv,
        'a_log': d_alog, 'dt_bias': d_dtb, 'g_norm_b': d_gnb, 's5_a_re': d_are, 's5_a_im': d_aim, 's5_b_re': d_sbre,
        's5_b_im': d_sbim, 's5_c_re': d_scre, 's5_c_im': d_scim, 's5_d': d_s5d, 's5_log_dt': d_logdt,
        'b_glu': d_bglu, 'final_norm': d_final}
    s_names = list(small)
    s_shapes = [tuple(small[n].shape) for n in s_names]
    reduced = _sum_slots(_gather8(_pack_rows([small[n] for n in s_names]), name="gather_small"), name="sum_small")
    sg = dict(zip(s_names, _unpack_rows(reduced, s_shapes)))
    sg['conv_w'] = lax.dynamic_slice_in_dim(sg['conv_w'], me * 768, 768, axis=1)
    sg['s5_d'] = lax.dynamic_slice_in_dim(sg['s5_d'], me * 512, 512, axis=1)
    sg['b_glu'] = lax.dynamic_slice_in_dim(sg['b_glu'], me * 512, 512, axis=1)
    sg = {n: sg[n].reshape(a[n].shape) for n in s_names}
    w_pack, m_pack, v_pack, g_pack = [_pack_rows([src[pre + n] for n in s_names])
                                      for src, pre in ((a, ''), (a, 'm_'), (a, 'v_'), (sg, ''))]
    sd, sm, sv = _rowwise(_adam_math, [w_pack, g_pack, m_pack, v_pack], [], [(128, F32)] * 3, name="adam_small")
    w_shapes = [tuple(a[n].shape) for n in s_names]
    res.update({n: (sg[n], d_, m_, v_) for n, d_, m_, v_ in zip(s_names, _unpack_rows(sd, w_shapes),
                                                               _unpack_rows(sm, w_shapes), _unpack_rows(sv, w_shapes))})

    red[0], red['in'], red['pp'] = reduce_finish(started0, sd, 0)
    for nm in ('w_up', 'w_down', 'w_ple_gate', 'w_out_e'):
        adam_layer(nm, 0, 0)
    adam_layer('w_in_e', 0, 'in', 0)
    adam_layer('w_ple_proj', 0, 'pp', 0)
    adam_layer('w_ple_proj', 1, 'pp', 256)

    return (loss, grad_x[None], *[res[n][0] for n in _WEIGHTS], *[res[n][1] for n in _WEIGHTS],
            *[res[n][2] for n in _WEIGHTS], *[res[n][3] for n in _WEIGHTS])


def kernel(x, p, norm_mix, norm_mlp, norm_ple, w_in_e, w_out_e, hgrn_lb, g_norm_a, conv_w, a_log, dt_bias, g_norm_b, s5_a_re, s5_a_im, s5_b_re, s5_b_im, s5_c_re, s5_c_im, s5_d, s5_log_dt, w_glu, b_glu, w_out_o, w_up, w_down, w_ple_gate, w_ple_proj, final_norm, loss_target, m_norm_mix, m_norm_mlp, m_norm_ple, m_w_in_e, m_w_out_e, m_hgrn_lb, m_g_norm_a, m_conv_w, m_a_log, m_dt_bias, m_g_norm_b, m_s5_a_re, m_s5_a_im, m_s5_b_re, m_s5_b_im, m_s5_c_re, m_s5_c_im, m_s5_d, m_s5_log_dt, m_w_glu, m_b_glu, m_w_out_o, m_w_up, m_w_down, m_w_ple_gate, m_w_ple_proj, m_final_norm, v_norm_mix, v_norm_mlp, v_norm_ple, v_w_in_e, v_w_out_e, v_hgrn_lb, v_g_norm_a, v_conv_w, v_a_log, v_dt_bias, v_g_norm_b, v_s5_a_re, v_s5_a_im, v_s5_b_re, v_s5_b_im, v_s5_c_re, v_s5_c_im, v_s5_d, v_s5_log_dt, v_w_glu, v_b_glu, v_w_out_o, v_w_up, v_w_down, v_w_ple_gate, v_w_ple_proj, v_final_norm):
    args = (x, p, norm_mix, norm_mlp, norm_ple, w_in_e, w_out_e, hgrn_lb, g_norm_a, conv_w, a_log, dt_bias, g_norm_b, s5_a_re, s5_a_im, s5_b_re, s5_b_im, s5_c_re, s5_c_im, s5_d, s5_log_dt, w_glu, b_glu, w_out_o, w_up, w_down, w_ple_gate, w_ple_proj, final_norm, loss_target, m_norm_mix, m_norm_mlp, m_norm_ple, m_w_in_e, m_w_out_e, m_hgrn_lb, m_g_norm_a, m_conv_w, m_a_log, m_dt_bias, m_g_norm_b, m_s5_a_re, m_s5_a_im, m_s5_b_re, m_s5_b_im, m_s5_c_re, m_s5_c_im, m_s5_d, m_s5_log_dt, m_w_glu, m_b_glu, m_w_out_o, m_w_up, m_w_down, m_w_ple_gate, m_w_ple_proj, m_final_norm, v_norm_mix, v_norm_mlp, v_norm_ple, v_w_in_e, v_w_out_e, v_hgrn_lb, v_g_norm_a, v_conv_w, v_a_log, v_dt_bias, v_g_norm_b, v_s5_a_re, v_s5_a_im, v_s5_b_re, v_s5_b_im, v_s5_c_re, v_s5_c_im, v_s5_d, v_s5_log_dt, v_w_glu, v_b_glu, v_w_out_o, v_w_up, v_w_down, v_w_ple_gate, v_w_ple_proj, v_final_norm)
    return _step(dict(zip(_INPUTS, args)))
```

```python
import functools
import math

import jax
import jax.numpy as jnp
from jax import lax
from jax.experimental import pallas as pl
from jax.experimental.pallas import tpu as pltpu

F32 = jnp.float32
BF16 = jnp.bfloat16

D_MODEL = 2048
SEQ = 4096
NORM_EPS = 1e-6
CHUNK = 64
HEAD = 128
VMEM_LIMIT = 56 * 1024 * 1024


_NN = (((1,), (0,)), ((), ()))
_NT = (((1,), (1,)), ((), ()))
_TN = (((0,), (0,)), ((), ()))
_HI = lax.Precision.HIGHEST
_NEG = -1e30


def _cparams(sem, **kw):
    return pltpu.CompilerParams(dimension_semantics=sem, vmem_limit_bytes=VMEM_LIMIT, **kw)


PACK_COLS = 2048


def _view_shape(view):
    kind, _, r = view
    return (4 * r, PACK_COLS) if kind == "row" else (r, 4 * PACK_COLS)


def _view_spec(view, rb, cb, row_of, col_of):
    kind, off, r = view
    assert off % rb == 0 and r % rb == 0 and PACK_COLS % cb == 0, (view, rb, cb)
    if kind == "row":
        nrb = r // rb
        return pl.BlockSpec((None, rb, cb), lambda i, j, k: (row_of(i, j, k) // nrb,
                                                             off // rb + row_of(i, j, k) % nrb, col_of(i, j, k)))
    ncb = PACK_COLS // cb
    return pl.BlockSpec((None, rb, cb), lambda i, j, k: (col_of(i, j, k) // ncb,
                                                         off // rb + row_of(i, j, k), col_of(i, j, k) % ncb))


def _mm(a, b, *, dims="nn", epilogue=None, extras=(), out_dtypes=(F32,), bm=1024, bn=1024, bk=2048, name,
        b_view=None, out_into=None, out_view=None, after=()):
    b_shape = _view_shape(b_view) if b_view is not None else b.shape
    if dims == "tn":
        (K, M), (K2, N) = a.shape, b_shape
    elif dims == "nt":
        (M, K), (N, K2) = a.shape, b_shape
    else:
        (M, K), (K2, N) = a.shape, b_shape
    assert K == K2, (a.shape, b_shape, dims)
    if b_view is not None and dims == "nt":
        bn = min(bn, b_view[2])
    if b_view is not None and dims != "nt":
        bk = min(bk, b_view[2])
    if out_view is not None:
        bm = min(bm, out_view[2])
    bm, bn, bk = min(bm, M), min(bn, N), min(bk, K)
    assert M % bm == 0 and N % bn == 0 and K % bk == 0, (M, N, K, bm, bn, bk)
    nk = K // bk
    ii, jj, kk = (lambda i, j, k: i), (lambda i, j, k: j), (lambda i, j, k: k)
    if dims == "tn":
        a_spec = pl.BlockSpec((bk, bm), lambda i, j, k: (k, i))
        dn = _TN
    else:
        a_spec = pl.BlockSpec((bm, bk), lambda i, j, k: (i, k))
        dn = _NT if dims == "nt" else _NN
    if dims == "nt":
        b_spec = _view_spec(b_view, bn, bk, jj, kk) if b_view else pl.BlockSpec((bn, bk), lambda i, j, k: (j, k))
    else:
        b_spec = _view_spec(b_view, bk, bn, kk, jj) if b_view else pl.BlockSpec((bk, bn), lambda i, j, k: (k, j))
    e_specs = []
    for e in extras:
        if e.shape == (M, N):
            e_specs.append(pl.BlockSpec((bm, bn), lambda i, j, k: (i, j)))
        else:
            assert e.shape == (1, N), e.shape
            e_specs.append(pl.BlockSpec((1, bn), lambda i, j, k: (0, j)))
    ne, no = len(extras), len(out_dtypes)
    if epilogue is None:
        epilogue = lambda acc: (acc,)
    into = out_into is not None

    def body(a_ref, b_ref, *rest):
        e_refs, rest = rest[:ne], rest[ne + (1 if into else 0) + len(after):]
        o_refs = rest[:no]
        part = lax.dot_general(a_ref[...].astype(BF16), b_ref[...].astype(BF16), dn, preferred_element_type=F32)

        def finish(total):
            outs = epilogue(total, *[e[...] for e in e_refs])
            for o, v in zip(o_refs, outs):
                o[...] = v.astype(o.dtype)

        if nk == 1:
            finish(part)
            return
        acc, k = rest[no], pl.program_id(2)

        @pl.when(k == 0)
        def _():
            acc[...] = part

        @pl.when((k > 0) & (k < nk - 1))
        def _():
            acc[...] += part

        @pl.when(k == nk - 1)
        def _():
            finish(acc[...] + part)

    if into:
        assert no == 1 and _view_shape(out_view) == (M, N), (out_view, M, N)
        out_specs = [_view_spec(out_view, bm, bn, ii, jj)]
        out_shape = [jax.ShapeDtypeStruct(out_into.shape, out_into.dtype)]
        extra_in, extra_specs, alias = [out_into], [pl.BlockSpec(memory_space=pl.ANY)], {2 + ne: 0}
    else:
        out_specs = [pl.BlockSpec((bm, bn), lambda i, j, k: (i, j)) for _ in out_dtypes]
        out_shape = [jax.ShapeDtypeStruct((M, N), dt) for dt in out_dtypes]
        extra_in, extra_specs, alias = [], [], {}
    outs = pl.pallas_call(
        body, name=name,
        grid=(M // bm, N // bn, nk),
        in_specs=[a_spec, b_spec] + e_specs + extra_specs + [pl.BlockSpec(memory_space=pl.ANY)] * len(after),
        out_specs=out_specs, out_shape=out_shape,
        scratch_shapes=[pltpu.VMEM((bm, bn), F32)] if nk > 1 else [],
        input_output_aliases=alias,
        compiler_params=_cparams(("parallel", "parallel", "arbitrary")),
    )(a, b, *extras, *extra_in, *after)
    return outs[0] if no == 1 else tuple(outs)


def _rowwise(fn, tiled, bcast, outs, sums=(), *, rows=256, name):
    S = tiled[0].shape[0]
    rows = min(rows, S)
    assert S % rows == 0
    nt, nb, no, ns = len(tiled), len(bcast), len(outs), len(sums)

    def body(*refs):
        t_refs, b_refs = refs[:nt], refs[nt:nt + nb]
        o_refs, s_refs = refs[nt + nb:nt + nb + no], refs[nt + nb + no:]
        res = fn(*[r[...] for r in t_refs], *[r[...] for r in b_refs])
        res = res if isinstance(res, (tuple, list)) else (res,)
        for o, v in zip(o_refs, res[:no]):
            o[...] = v.astype(o.dtype)
        if ns:
            @pl.when(pl.program_id(0) == 0)
            def _():
                for s in s_refs:
                    s[...] = jnp.zeros_like(s)
            for s, v in zip(s_refs, res[no:]):
                s[...] += v

    res = pl.pallas_call(
        body, name=name,
        grid=(S // rows,),
        in_specs=[pl.BlockSpec((rows, t.shape[1]), lambda i: (i, 0)) for t in tiled]
        + [pl.BlockSpec(b.shape, lambda i, nd=b.ndim: (0,) * nd) for b in bcast],
        out_specs=[pl.BlockSpec((rows, c), lambda i: (i, 0)) for c, _ in outs]
        + [pl.BlockSpec((1, c), lambda i: (0, 0)) for c in sums],
        out_shape=[jax.ShapeDtypeStruct((S, c), dt) for c, dt in outs]
        + [jax.ShapeDtypeStruct((1, c), F32) for c in sums],
        compiler_params=_cparams(("arbitrary",)),
    )(*tiled, *bcast)
    return res[0] if len(res) == 1 else tuple(res)


def _rms(x, g):
    return x * lax.rsqrt(jnp.mean(x * x, axis=-1, keepdims=True) + NORM_EPS) * g


def _mxu(a, b, dn):
    if a.ndim == 3:
        ((ca,), (cb,)), _ = dn
        dn = (((ca + 1,), (cb + 1,)), ((0,), (0,)))
    return lax.dot_general(a, b, dn, preferred_element_type=F32)


def _split(a):
    hi = a.astype(BF16)
    return hi, (a - hi.astype(F32)).astype(BF16)


def _passes(a, b, dn, three):
    if not three:
        return _mxu(a.astype(BF16), b.astype(BF16), dn)
    (ah, al), (bh, bl) = _split(a), _split(b)
    return _mxu(ah, bh, dn) + (_mxu(ah, bl, dn) + _mxu(al, bh, dn))


def _dot_grads(a, b, g, dn, three):
    if dn == _NN:
        return _passes(g, b, _NT, three), _passes(a, g, _TN, three)
    if dn == _NT:
        return _passes(g, b, _NN, three), _passes(g, a, _TN, three)
    assert dn == _TN
    return _passes(b, g, _NT, three), _passes(a, g, _NN, three)


@functools.partial(jax.custom_vjp, nondiff_argnums=(2,))
def _dot(a, b, dn=_NN):
    return _passes(a, b, dn, False)


_dot.defvjp(lambda a, b, dn: (_passes(a, b, dn, False), (a, b)),
            lambda dn, res, g: _dot_grads(res[0], res[1], g, dn, False))


@functools.partial(jax.custom_vjp, nondiff_argnums=(2,))
def _dot3(a, b, dn=_NN):
    return _passes(a, b, dn, True)


_dot3.defvjp(lambda a, b, dn: (_passes(a, b, dn, True), (a, b)),
             lambda dn, res, g: _dot_grads(res[0], res[1], g, dn, True))


def _tri_times(x, dn):
    tri = _tril(x.shape[-2]).astype(BF16)
    if x.ndim == 3:
        tri = jnp.broadcast_to(tri, (x.shape[0],) + tri.shape)
    hi, lo = _split(x)
    lo2 = (x - hi.astype(F32) - lo.astype(F32)).astype(BF16)
    return _mxu(tri, hi, dn) + (_mxu(tri, lo, dn) + _mxu(tri, lo2, dn))


@jax.custom_vjp
def _cumsum_rows(x):
    return _tri_times(x, _NN)


_cumsum_rows.defvjp(lambda x: (_tri_times(x, _NN), None), lambda _, g: (_tri_times(g, _TN),))


def _iota(shape, dim):
    return lax.broadcasted_iota(jnp.int32, shape, dim)


def _tril(n, strict=False):
    r, c = _iota((n, n), 0), _iota((n, n), 1)
    return (r > c) if strict else (r >= c)


@functools.partial(jax.custom_vjp, nondiff_argnums=(1,))
def _roll_rows(x, r):
    return pltpu.roll(x, r, 0)


def _roll_rows_fwd(x, r):
    return pltpu.roll(x, r, 0), None


def _roll_rows_bwd(r, _, g):
    return (pltpu.roll(g, (g.shape[0] - r) % g.shape[0], 0),)


_roll_rows.defvjp(_roll_rows_fwd, _roll_rows_bwd)


def _head_norm_gate(o, gn, gate):
    return _rms(o, gn) * jax.nn.silu(gate)


_SUB = 16
_HPS = 8


def _hgrn_chunk(q, fpre, v, gate, lb, gna, st):
    c = q.shape[0]
    forget = lb + (1.0 - lb) * jax.nn.sigmoid(fpre)
    k = 1.0 - forget
    logf = jnp.log(forget)
    cum = _cumsum_rows(logf)
    cum_end = jnp.sum(logf, axis=0, keepdims=True)
    o = _dot(q * jnp.exp(cum), st, _NT)
    st_new = st * jnp.exp(cum_end) + _dot(v, k * jnp.exp(cum_end - cum), _TN)
    t = _iota((c, 1), 0)
    s_off = jnp.zeros((c, c), F32)
    for i in range(1, c // _SUB):
        before = t < i * _SUB
        c_i = jnp.sum(jnp.where(before, logf, 0.0), axis=0, keepdims=True)
        in_blk = (t >= i * _SUB) & (t < (i + 1) * _SUB)
        qi = jnp.where(in_blk, q * jnp.exp(jnp.minimum(cum - c_i, 0.0)), 0.0)
        ki = jnp.where(before, k * jnp.exp(jnp.minimum(c_i - cum, 0.0)), 0.0)
        s_off = s_off + _dot(qi, ki, _NT)
    o = o + _dot(s_off, v)
    tmod = t % _SUB
    for r in range(_SUB):
        kr, cr, vr = (k, cum, v) if r == 0 else (_roll_rows(k, r), _roll_rows(cum, r), _roll_rows(v, r))
        w = q * kr * jnp.exp(jnp.where(tmod >= r, cum - cr, _NEG))
        o = o + jnp.sum(w, axis=1, keepdims=True) * vr
    return _head_norm_gate(o, gna, gate), st_new


def _hgrn_fwd(proj, lb, gna):
    S = proj.shape[0]
    n_chunks, heads = S // CHUNK, 8

    def body(q_ref, f_ref, v_ref, g_ref, lb_ref, gna_ref, o_ref, st_out, st):
        n, hb = pl.program_id(0), pl.program_id(1)

        @pl.when(n == 0)
        def _():
            st[hb] = jnp.zeros((_HPS, HEAD, HEAD), F32)

        sls = [slice(hh * HEAD, (hh + 1) * HEAD) for hh in range(_HPS)]
        st0 = [st[hb, hh] for hh in range(_HPS)]
        ins = [(q_ref[:, sl], f_ref[:, sl], v_ref[:, sl], g_ref[:, sl], lb_ref[:, sl]) for sl in sls]
        gna = gna_ref[...]
        res = [_hgrn_chunk(*ins[hh], gna, st0[hh]) for hh in range(_HPS)]
        for hh, sl in enumerate(sls):
            st_out[hh] = st0[hh]
            o_ref[:, sl] = res[hh][0].astype(o_ref.dtype)
            st[hb, hh] = res[hh][1]

    wide = _HPS * HEAD
    sec = lambda s: pl.BlockSpec((CHUNK, wide), lambda n, h, s=s: (n, (8 // _HPS) * s + h))
    return pl.pallas_call(
        body, name="hgrn_fwd", grid=(n_chunks, heads // _HPS),
        in_specs=[sec(0), sec(1), sec(2), sec(3),
                  pl.BlockSpec((1, wide), lambda n, h: (0, h)), pl.BlockSpec((1, HEAD), lambda n, h: (0, 0))],
        out_specs=[pl.BlockSpec((CHUNK, wide), lambda n, h: (n, h)),
                   pl.BlockSpec((None, _HPS, HEAD, HEAD), lambda n, h: (n, h, 0, 0))],
        out_shape=[jax.ShapeDtypeStruct((S, 1024), BF16),
                   jax.ShapeDtypeStruct((n_chunks, heads, HEAD, HEAD), F32)],
        scratch_shapes=[pltpu.VMEM((heads // _HPS, _HPS, HEAD, HEAD), F32)],
        compiler_params=_cparams(("arbitrary", "arbitrary")),
    )(proj, proj, proj, proj, lb, gna)


def _hgrn_bwd(proj, lb, gna, states, d_o, after=()):
    S = proj.shape[0]
    n_chunks, heads = S // CHUNK, 8

    def body(q_ref, f_ref, v_ref, g_ref, lb_ref, gna_ref, st_ref, do_ref, *rest):
        dq_ref, df_ref, dv_ref, dg_ref, dlb_ref, dgna_ref, dst = rest[len(after):]
        n, hb = pl.program_id(0), pl.program_id(1)

        @pl.when(n == 0)
        def _():
            dst[hb] = jnp.zeros((_HPS, HEAD, HEAD), F32)
            dlb_ref[hb] = jnp.zeros((_HPS, 1, HEAD), F32)

        @pl.when((n == 0) & (hb == 0))
        def _():
            dgna_ref[...] = jnp.zeros_like(dgna_ref)

        sls = [slice(hh * HEAD, (hh + 1) * HEAD) for hh in range(_HPS)]
        gna = gna_ref[...]
        ins = [(q_ref[:, sl], f_ref[:, sl], v_ref[:, sl], g_ref[:, sl], lb_ref[:, sl], gna, st_ref[hh])
               for hh, sl in enumerate(sls)]
        cts = [(do_ref[:, sl], dst[hb, hh]) for hh, sl in enumerate(sls)]
        grads = [jax.vjp(_hgrn_chunk, *ins[hh])[1](cts[hh]) for hh in range(_HPS)]
        dgna_sum = jnp.zeros((1, HEAD), F32)
        for hh, sl in enumerate(sls):
            dq, df, dv, dg, dlb, dgna, dst0 = grads[hh]
            dq_ref[:, sl] = dq.astype(dq_ref.dtype)
            df_ref[:, sl] = df.astype(df_ref.dtype)
            dv_ref[:, sl] = dv.astype(dv_ref.dtype)
            dg_ref[:, sl] = dg.astype(dg_ref.dtype)
            dlb_ref[hb, hh] += dlb
            dgna_sum = dgna_sum + dgna
            dst[hb, hh] = dst0
        dgna_ref[...] += dgna_sum

    rev = lambda n: n_chunks - 1 - n
    wide = _HPS * HEAD
    sec = lambda s: pl.BlockSpec((CHUNK, wide), lambda n, h, s=s: (rev(n), (8 // _HPS) * s + h))
    out = pl.BlockSpec((CHUNK, wide), lambda n, h: (rev(n), h))
    return pl.pallas_call(
        body, name="hgrn_bwd", grid=(n_chunks, heads // _HPS),
        in_specs=[sec(0), sec(1), sec(2), sec(3),
                  pl.BlockSpec((1, wide), lambda n, h: (0, h)), pl.BlockSpec((1, HEAD), lambda n, h: (0, 0)),
                  pl.BlockSpec((None, _HPS, HEAD, HEAD), lambda n, h: (rev(n), h, 0, 0)),
                  pl.BlockSpec((CHUNK, wide), lambda n, h: (rev(n), h))] + [pl.BlockSpec(memory_space=pl.ANY)] * len(after),
        out_specs=[out, out, out, out,
                   pl.BlockSpec((heads // _HPS, _HPS, 1, HEAD), lambda n, h: (0, 0, 0, 0)),
                   pl.BlockSpec((1, HEAD), lambda n, h: (0, 0))],
        out_shape=[jax.ShapeDtypeStruct((S, 1024), BF16)] * 4
        + [jax.ShapeDtypeStruct((heads // _HPS, _HPS, 1, HEAD), F32), jax.ShapeDtypeStruct((1, HEAD), F32)],
        scratch_shapes=[pltpu.VMEM((heads // _HPS, _HPS, HEAD, HEAD), F32)],
        compiler_params=_cparams(("arbitrary", "arbitrary")),
    )(proj, proj, proj, proj, lb, gna, states, d_o, *after)


def _l2n(t):
    return t * lax.rsqrt(jnp.sum(t * t, axis=-1, keepdims=True) + NORM_EPS)


def _unit_lower_inverse(low):
    c = low.shape[-1]
    inv = (_iota((c, c), 0) == _iota((c, c), 1)).astype(F32) - low
    p = low
    span = 2
    while span < c:
        p = _dot3(p, p)
        inv = inv + _dot3(inv, p)
        span *= 2
    return inv


def _gdn_chunk(qc, kc, v, z, tail, h0, arow, dtrow, gnb, st):
    nb, c = qc.shape[0], qc.shape[1]
    head = h0 + _iota((nb, c, HEAD), 0)
    lane = _iota((nb, c, HEAD), 2)
    la_all = arow * jax.nn.softplus(tail + dtrow)
    la = jnp.sum(jnp.where(lane == head, la_all[None], 0.0), axis=2, keepdims=True)
    beta = jnp.sum(jnp.where(lane == head + 8, jax.nn.sigmoid(tail)[None], 0.0), axis=2, keepdims=True)
    la_b = jnp.broadcast_to(la, (nb, c, HEAD))
    cum = _cumsum_rows(la_b)
    cmat = _cumsum_rows(jnp.broadcast_to(la, (nb, c, c)))
    cum_end = jnp.sum(la_b, axis=1, keepdims=True)
    decay = jnp.exp(jnp.where(_tril(c), cmat - jnp.swapaxes(cmat, 1, 2), _NEG))
    q = _l2n(qc) * (HEAD ** -0.5)
    k = _l2n(kc)
    k_beta = k * beta
    low = jnp.where(_tril(c, strict=True), _dot(k_beta, k, _NT) * decay, 0.0)
    inv = _unit_lower_inverse(low)
    u, w = _dot(inv, v * beta), _dot(inv, k_beta * jnp.exp(cum))
    intra = _dot(q, k, _NT) * decay
    v_new = u - _dot(w, st, _NT)
    o = _dot(q * jnp.exp(cum), st, _NT) + _dot(intra, v_new)
    st_new = st * jnp.exp(cum_end) + _dot(v_new, k * jnp.exp(cum_end - cum), _TN)
    return _head_norm_gate(o, gnb, z), st_new


def _gdn_specs(n_of):
    wide = _HPS * HEAD
    qkv = lambda s: pl.BlockSpec((CHUNK, wide), lambda n, h, s=s: (n_of(n), (8 // _HPS) * s + h))
    return [qkv(0), qkv(1), qkv(2),
            pl.BlockSpec((CHUNK, wide), lambda n, h: (n_of(n), 56 // _HPS + h)),
            pl.BlockSpec((CHUNK, HEAD), lambda n, h: (n_of(n), 64)),
            pl.BlockSpec((1, HEAD), lambda n, h: (0, 0)), pl.BlockSpec((1, HEAD), lambda n, h: (0, 0)),
            pl.BlockSpec((1, HEAD), lambda n, h: (0, 0))]


def _gdn_fwd(act, proj, arow, dtrow, gnb):
    S = act.shape[0]
    n_chunks, heads = S // CHUNK, 8

    def body(q_ref, k_ref, v_ref, z_ref, t_ref, a_ref, dt_ref, gnb_ref, o_ref, st_out, st):
        n, hb = pl.program_id(0), pl.program_id(1)

        @pl.when(n == 0)
        def _():
            st[hb] = jnp.zeros((_HPS, HEAD, HEAD), F32)

        sls = [slice(hh * HEAD, (hh + 1) * HEAD) for hh in range(_HPS)]
        heads_of = lambda ref: jnp.stack([ref[:, sl] for sl in sls])
        st0 = st[hb]
        o, st1 = _gdn_chunk(heads_of(q_ref), heads_of(k_ref), heads_of(v_ref), heads_of(z_ref), t_ref[...], hb * _HPS,
                            a_ref[...], dt_ref[...], gnb_ref[...], st0)
        st_out[...] = st0
        st[hb] = st1
        for hh, sl in enumerate(sls):
            o_ref[:, sl] = o[hh].astype(o_ref.dtype)

    return pl.pallas_call(
        body, name="gdn_fwd", grid=(n_chunks, heads // _HPS),
        in_specs=_gdn_specs(lambda n: n),
        out_specs=[pl.BlockSpec((CHUNK, _HPS * HEAD), lambda n, h: (n, h)),
                   pl.BlockSpec((None, _HPS, HEAD, HEAD), lambda n, h: (n, h, 0, 0))],
        out_shape=[jax.ShapeDtypeStruct((S, 1024), BF16),
                   jax.ShapeDtypeStruct((n_chunks, heads, HEAD, HEAD), F32)],
        scratch_shapes=[pltpu.VMEM((heads // _HPS, _HPS, HEAD, HEAD), F32)],
        compiler_params=_cparams(("arbitrary", "arbitrary")),
    )(act, act, act, proj, proj, arow, dtrow, gnb)


def _gdn_bwd(act, proj, arow, dtrow, gnb, states, d_o):
    S = act.shape[0]
    n_chunks, heads = S // CHUNK, 8

    def body(q_ref, k_ref, v_ref, z_ref, t_ref, a_ref, dt_ref, gnb_ref, st_ref, do_ref,
             dq_ref, dk_ref, dv_ref, dz_ref, dt_out, da_ref, ddt_ref, dgnb_ref, dst, dtail_acc):
        n, hb = pl.program_id(0), pl.program_id(1)
        n_hb = heads // _HPS

        @pl.when(n == 0)
        def _():
            dst[hb] = jnp.zeros((_HPS, HEAD, HEAD), F32)

        @pl.when((n == 0) & (hb == 0))
        def _():
            da_ref[...] = jnp.zeros_like(da_ref)
            ddt_ref[...] = jnp.zeros_like(ddt_ref)
            dgnb_ref[...] = jnp.zeros_like(dgnb_ref)

        sls = [slice(hh * HEAD, (hh + 1) * HEAD) for hh in range(_HPS)]
        heads_of = lambda ref: jnp.stack([ref[:, sl] for sl in sls])
        fn = lambda qc, kc, v, z, tail, arow, dtrow, gnb, st: _gdn_chunk(qc, kc, v, z, tail, hb * _HPS, arow, dtrow,
                                                                        gnb, st)
        _, vjp = jax.vjp(fn, heads_of(q_ref), heads_of(k_ref), heads_of(v_ref), heads_of(z_ref), t_ref[...],
                         a_ref[...], dt_ref[...], gnb_ref[...], st_ref[...])
        dq, dk, dv, dz, dtail_sum, da_sum, ddt_sum, dgnb_sum, dst0 = vjp((heads_of(do_ref), dst[hb]))
        dst[hb] = dst0
        for hh, sl in enumerate(sls):
            dq_ref[:, sl] = dq[hh]
            dk_ref[:, sl] = dk[hh]
            dv_ref[:, sl] = dv[hh]
            dz_ref[:, sl] = dz[hh].astype(dz_ref.dtype)

        @pl.when(hb == 0)
        def _():
            dtail_acc[...] = dtail_sum

        @pl.when(hb > 0)
        def _():
            dtail_acc[...] += dtail_sum

        @pl.when(hb == n_hb - 1)
        def _():
            dt_out[...] = dtail_acc[...].astype(dt_out.dtype)

        da_ref[...] += da_sum
        ddt_ref[...] += ddt_sum
        dgnb_ref[...] += dgnb_sum

    rev = lambda n: n_chunks - 1 - n
    out = pl.BlockSpec((CHUNK, _HPS * HEAD), lambda n, h: (rev(n), h))
    row = pl.BlockSpec((1, HEAD), lambda n, h: (0, 0))
    return pl.pallas_call(
        body, name="gdn_bwd", grid=(n_chunks, heads // _HPS),
        in_specs=_gdn_specs(rev)
        + [pl.BlockSpec((None, _HPS, HEAD, HEAD), lambda n, h: (rev(n), h, 0, 0)),
           pl.BlockSpec((CHUNK, _HPS * HEAD), lambda n, h: (rev(n), 8 // _HPS + h))],
        out_specs=[out, out, out, out, pl.BlockSpec((CHUNK, HEAD), lambda n, h: (rev(n), 0)), row, row, row],
        out_shape=[jax.ShapeDtypeStruct((S, 1024), F32)] * 3
        + [jax.ShapeDtypeStruct((S, 1024), BF16), jax.ShapeDtypeStruct((S, HEAD), BF16)]
        + [jax.ShapeDtypeStruct((1, HEAD), F32)] * 3,
        scratch_shapes=[pltpu.VMEM((heads // _HPS, _HPS, HEAD, HEAD), F32), pltpu.VMEM((CHUNK, HEAD), F32)],
        compiler_params=_cparams(("arbitrary", "arbitrary")),
    )(act, act, act, proj, proj, arow, dtrow, gnb, states, d_o)


def _conv_silu(x, w):
    t = _iota((x.shape[0], 1), 0)
    tap = _iota(w.shape, 0)
    y = jnp.zeros_like(x)
    for r in range(4):
        w_r = jnp.sum(jnp.where(tap == 3 - r, w, 0.0), axis=0, keepdims=True)
        y = y + (x if r == 0 else jnp.where(t >= r, _roll_rows(x, r), 0.0)) * w_r
    return jax.nn.silu(y)


_CONV_COLS = 128


def _conv_fwd(proj, conv_w):
    S = proj.shape[0]
    nb = 3072 // _CONV_COLS
    off = 4096 // _CONV_COLS

    def body(x_ref, w_ref, o_ref):
        o_ref[...] = _conv_silu(x_ref[...], w_ref[...])

    return pl.pallas_call(
        body, name="conv_fwd", grid=(nb,),
        in_specs=[pl.BlockSpec((S, _CONV_COLS), lambda j: (0, off + j)), pl.BlockSpec((4, _CONV_COLS), lambda j: (0, j))],
        out_specs=pl.BlockSpec((S, _CONV_COLS), lambda j: (0, j)),
        out_shape=jax.ShapeDtypeStruct((S, 3072), F32),
        compiler_params=_cparams(("parallel",)),
    )(proj, conv_w)


def _conv_bwd(proj, conv_w, dq, dk, dv):
    S = proj.shape[0]
    nb = 3072 // _CONV_COLS
    off = 4096 // _CONV_COLS
    per = 1024 // _CONV_COLS

    def body(x_ref, w_ref, dq_ref, dk_ref, dv_ref, dx_ref, dw_ref):
        j = pl.program_id(0)
        _, vjp = jax.vjp(_conv_silu, x_ref[...], w_ref[...])
        d = jnp.where(j < per, dq_ref[...], jnp.where(j < 2 * per, dk_ref[...], dv_ref[...]))
        dx, dw = vjp(d)
        dx_ref[...] = dx.astype(dx_ref.dtype)
        dw_ref[...] = dw

    dsp = lambda s: pl.BlockSpec((S, _CONV_COLS), lambda j, s=s: (0, jnp.clip(j - s * per, 0, per - 1)))
    return pl.pallas_call(
        body, name="conv_bwd", grid=(nb,),
        in_specs=[pl.BlockSpec((S, _CONV_COLS), lambda j: (0, off + j)), pl.BlockSpec((4, _CONV_COLS), lambda j: (0, j)),
                  dsp(0), dsp(1), dsp(2)],
        out_specs=[pl.BlockSpec((S, _CONV_COLS), lambda j: (0, j)), pl.BlockSpec((4, _CONV_COLS), lambda j: (0, j))],
        out_shape=[jax.ShapeDtypeStruct((S, 3072), BF16), jax.ShapeDtypeStruct((4, 3072), F32)],
        compiler_params=_cparams(("parallel",)),
    )(proj, conv_w, dq, dk, dv)


_S5_T = 512
_S5_L = 512
_S5_NB = 16
_S5_U = 4
_S5_UB = 3


def _s5_tile(xr, xi, pw_r, pw_i, cr, ci, reverse):
    t8 = _iota((8, 1), 0)
    for sh in (1, 2, 4):
        row = (8 - sh) if reverse else (sh - 1)
        ar, ai = pw_r[row:row + 1, :], pw_i[row:row + 1, :]
        if reverse:
            keep, amt = t8 < 8 - sh, 8 - sh
        else:
            keep, amt = t8 >= sh, sh
        sr = jnp.where(keep, pltpu.roll(xr, amt, 0), 0.0)
        si = jnp.where(keep, pltpu.roll(xi, amt, 0), 0.0)
        xr, xi = xr + ar * sr - ai * si, xi + ar * si + ai * sr
    xr, xi = xr + pw_r * cr - pw_i * ci, xi + pw_r * ci + pw_i * cr
    return xr, xi


def _s5_fwd(u, bre, bim, cre, cim, pw_r, pw_i, dskip):
    S = u.shape[0]
    T = min(_S5_T, S)
    nt = S // T

    def body(u_ref, bre_ref, bim_ref, cre_ref, cim_ref, pr_ref, pi_ref, d_ref, y_ref, xr_ref, xi_ref,
             bu_r, bu_i, car_r, car_i):
        @pl.when(pl.program_id(1) == 0)
        def _():
            car_r[...] = jnp.zeros_like(car_r)
            car_i[...] = jnp.zeros_like(car_i)

        uu = u_ref[...]
        bu_r[...] = _dot(uu, bre_ref[...])
        bu_i[...] = _dot(uu, bim_ref[...])
        pw_r, pw_i = pr_ref[...], pi_ref[...]

        def tiles(i, carry):
            ins = []
            for k in range(_S5_U):
                r0 = pl.multiple_of((i * _S5_U + k) * 8, 8)
                ins.append((r0, bu_r[pl.ds(r0, 8), :], bu_i[pl.ds(r0, 8), :]))
            outs = []
            for r0, br, bi in ins:
                xr, xi = _s5_tile(br, bi, pw_r, pw_i, carry[0], carry[1], False)
                carry = (xr[7:8, :], xi[7:8, :])
                outs.append((r0, xr, xi))
            for r0, xr, xi in outs:
                xr_ref[pl.ds(r0, 8), :] = xr
                xi_ref[pl.ds(r0, 8), :] = xi
            return carry

        cr, ci = lax.fori_loop(0, T // (8 * _S5_U), tiles, (car_r[...], car_i[...]))
        car_r[...] = cr
        car_i[...] = ci
        y_ref[...] = _dot(xr_ref[...], cre_ref[...]) - _dot(xi_ref[...], cim_ref[...]) + d_ref[...] * uu

    blk3 = lambda a, b: pl.BlockSpec((None, a, b), lambda j, t: (j, 0, 0))
    return pl.pallas_call(
        body, name="s5_fwd", grid=(_S5_NB, nt),
        in_specs=[pl.BlockSpec((T, HEAD), lambda j, t: (t, j)),
                  blk3(HEAD, _S5_L), blk3(HEAD, _S5_L), blk3(_S5_L, HEAD), blk3(_S5_L, HEAD),
                  blk3(8, _S5_L), blk3(8, _S5_L), pl.BlockSpec((1, HEAD), lambda j, t: (0, j))],
        out_specs=[pl.BlockSpec((T, HEAD), lambda j, t: (t, j)),
                   pl.BlockSpec((T, _S5_L), lambda j, t: (t, j)), pl.BlockSpec((T, _S5_L), lambda j, t: (t, j))],
        out_shape=[jax.ShapeDtypeStruct((S, D_MODEL), F32),
                   jax.ShapeDtypeStruct((S, _S5_NB * _S5_L), F32), jax.ShapeDtypeStruct((S, _S5_NB * _S5_L), F32)],
        scratch_shapes=[pltpu.VMEM((T, _S5_L), F32), pltpu.VMEM((T, _S5_L), F32),
                        pltpu.VMEM((1, _S5_L), F32), pltpu.VMEM((1, _S5_L), F32)],
        compiler_params=_cparams(("parallel", "arbitrary")),
    )(u, bre, bim, cre, cim, pw_r, pw_i, dskip)


def _s5_bwd(dy, u, xre, xim, bre, bim, cre, cim, qw_r, qw_i, dskip):
    S = u.shape[0]
    T = min(_S5_T, S)
    nt = S // T
    nt8 = T // 8

    def body(dy_ref, u_ref, xr_ref, xi_ref, xpr_ref, xpi_ref, bre_ref, bim_ref, cre_ref, cim_ref, qr_ref, qi_ref,
             d_ref, du_ref, dbr_ref, dbi_ref, dcr_ref, dci_ref, dlr_ref, dli_ref, dd_ref,
             g_r, g_i, car_r, car_i):
        t = pl.program_id(1)

        @pl.when(t == 0)
        def _():
            car_r[...] = jnp.zeros_like(car_r)
            car_i[...] = jnp.zeros_like(car_i)
            for r in (dbr_ref, dbi_ref, dcr_ref, dci_ref, dlr_ref, dli_ref, dd_ref):
                r[...] = jnp.zeros_like(r)

        dyy, uu = dy_ref[...], u_ref[...]
        g_r[...] = _dot(dyy, cre_ref[...], _NT)
        g_i[...] = -_dot(dyy, cim_ref[...], _NT)
        qw_r, qw_i = qr_ref[...], qi_ref[...]
        t8 = _iota((8, 1), 0)
        first = t == nt - 1

        def load(r0, prev_r, prev_i):
            rows = pl.ds(r0, 8)
            return r0, g_r[rows, :], g_i[rows, :], xr_ref[rows, :], xi_ref[rows, :], prev_r, prev_i

        def run(loaded, carry, acc):
            done = []
            for r0, dr, di, xr, xi, prev_r, prev_i in loaded:
                gr, gi = _s5_tile(dr, di, qw_r, qw_i, carry[0], carry[1], True)
                carry = (gr[0:1, :], gi[0:1, :])
                xpr = jnp.where(t8 >= 1, pltpu.roll(xr, 1, 0), prev_r)
                xpi = jnp.where(t8 >= 1, pltpu.roll(xi, 1, 0), prev_i)
                acc = (acc[0] + gr * xpr + gi * xpi, acc[1] + gi * xpr - gr * xpi)
                done.append((r0, gr, gi))
            for r0, gr, gi in done:
                g_r[pl.ds(r0, 8), :] = gr
                g_i[pl.ds(r0, 8), :] = gi
            return carry, acc

        def step(ii, state):
            loaded = []
            for k in range(_S5_UB):
                idx = nt8 - 1 - (ii * _S5_UB + k)
                r0 = pl.multiple_of(idx * 8, 8)
                p0 = pl.multiple_of((idx - 1) * 8, 8)
                loaded.append(load(r0, xr_ref[pl.ds(p0, 8), :][7:8, :], xi_ref[pl.ds(p0, 8), :][7:8, :]))
            return run(loaded, *state)

        zero = jnp.zeros((8, _S5_L), F32)
        assert (nt8 - 1) % _S5_UB == 0
        state = lax.fori_loop(0, (nt8 - 1) // _S5_UB, step, ((car_r[...], car_i[...]), (zero, zero)))
        prev_r = jnp.where(first, 0.0, xpr_ref[...][7:8, :])
        prev_i = jnp.where(first, 0.0, xpi_ref[...][7:8, :])
        (cr, ci), (ar, ai) = run([load(0, prev_r, prev_i)], *state)
        car_r[...] = cr
        car_i[...] = ci
        dlr_ref[...] += ar
        dli_ref[...] += ai
        gr, gi = g_r[...], g_i[...]
        du_ref[...] = _dot(gr, bre_ref[...], _NT) + _dot(gi, bim_ref[...], _NT) + d_ref[...] * dyy
        dbr_ref[...] += _dot(uu, gr, _TN)
        dbi_ref[...] += _dot(uu, gi, _TN)
        dcr_ref[...] += _dot(xr_ref[...], dyy, _TN)
        dci_ref[...] -= _dot(xi_ref[...], dyy, _TN)
        dd_ref[...] += jnp.sum(dyy * uu, axis=0, keepdims=True)

    rev = lambda t: nt - 1 - t
    blk3 = lambda a, b: pl.BlockSpec((None, a, b), lambda j, t: (j, 0, 0))
    tl = pl.BlockSpec((T, HEAD), lambda j, t: (rev(t), j))
    xs = pl.BlockSpec((T, _S5_L), lambda j, t: (rev(t), j))
    xp = pl.BlockSpec((8, _S5_L), lambda j, t: (jnp.maximum(rev(t) * nt8 - 1, 0), j))
    return pl.pallas_call(
        body, name="s5_bwd", grid=(_S5_NB, nt),
        in_specs=[tl, tl, xs, xs, xp, xp, blk3(HEAD, _S5_L), blk3(HEAD, _S5_L), blk3(_S5_L, HEAD), blk3(_S5_L, HEAD),
                  blk3(8, _S5_L), blk3(8, _S5_L), pl.BlockSpec((1, HEAD), lambda j, t: (0, j))],
        out_specs=[tl, blk3(HEAD, _S5_L), blk3(HEAD, _S5_L), blk3(_S5_L, HEAD), blk3(_S5_L, HEAD),
                   blk3(8, _S5_L), blk3(8, _S5_L), pl.BlockSpec((1, HEAD), lambda j, t: (0, j))],
        out_shape=[jax.ShapeDtypeStruct((S, D_MODEL), F32),
                   jax.ShapeDtypeStruct((_S5_NB, HEAD, _S5_L), F32), jax.ShapeDtypeStruct((_S5_NB, HEAD, _S5_L), F32),
                   jax.ShapeDtypeStruct((_S5_NB, _S5_L, HEAD), F32), jax.ShapeDtypeStruct((_S5_NB, _S5_L, HEAD), F32),
                   jax.ShapeDtypeStruct((_S5_NB, 8, _S5_L), F32), jax.ShapeDtypeStruct((_S5_NB, 8, _S5_L), F32),
                   jax.ShapeDtypeStruct((1, D_MODEL), F32)],
        scratch_shapes=[pltpu.VMEM((T, _S5_L), F32), pltpu.VMEM((T, _S5_L), F32),
                        pltpu.VMEM((1, _S5_L), F32), pltpu.VMEM((1, _S5_L), F32)],
        compiler_params=_cparams(("parallel", "arbitrary")),
    )(dy, u, xre, xim, xre, xim, bre, bim, cre, cim, qw_r, qw_i, dskip)


def _s5_params(a_re, a_im, log_dt, b_re, b_im, c_re, c_im):
    step = jnp.exp(log_dt)[:, None]
    mag = jnp.exp(a_re * step)
    lr, li = mag * jnp.cos(a_im * step), mag * jnp.sin(a_im * step)
    den = a_re * a_re + a_im * a_im
    nr, ni = lr - 1.0, li
    kr, ki = (nr * a_re + ni * a_im) / den, (ni * a_re - nr * a_im) / den
    bbr = kr[..., None] * b_re - ki[..., None] * b_im
    bbi = kr[..., None] * b_im + ki[..., None] * b_re
    eye = jnp.eye(8, dtype=F32)

    def blk_b(bb):
        t = bb.reshape(_S5_NB, 8, 64, 16).transpose(0, 1, 3, 2)
        return (t[:, :, :, None, :] * eye[None, :, None, :, None]).reshape(_S5_NB, HEAD, _S5_L)

    def blk_c(cc):
        t = cc.reshape(_S5_NB, 8, 16, 64).transpose(0, 1, 3, 2)
        return (t[:, :, :, None, :] * eye[None, :, None, :, None]).reshape(_S5_NB, _S5_L, HEAD)

    return (blk_b(bbr), blk_b(bbi), blk_c(c_re), blk_c(c_im),
            lr.reshape(_S5_NB, _S5_L), li.reshape(_S5_NB, _S5_L))


def _s5_powers(lr, li):
    pr, pi = [lr], [li]
    for _ in range(7):
        pr, pi = pr + [pr[-1] * lr - pi[-1] * li], pi + [pr[-1] * li + pi[-1] * lr]
    return jnp.stack(pr, axis=1), jnp.stack(pi, axis=1)


_MESH = pl.DeviceIdType.MESH
_ANY = pl.BlockSpec(memory_space=pl.ANY)


def _place():
    x, y, c = lax.axis_index("x"), lax.axis_index("y"), lax.axis_index("c")
    return x, y, c, [(1 - x, y), (x, 1 - y), (1 - x, 1 - y)]


def _comm_call(body, arrs, out_shapes, n_remote, name):
    n = len(arrs)
    return pl.pallas_call(
        body, name=name,
        in_specs=[_ANY] * n, out_specs=[_ANY] * n, out_shape=out_shapes,
        scratch_shapes=[pltpu.SemaphoreType.DMA((n * n_remote,)), pltpu.SemaphoreType.DMA((n * n_remote,)),
                        pltpu.SemaphoreType.DMA((n,))],
    )(*arrs)


def _own_slot(shard, me_idx, name, rows=256):
    _, half, C = shard.shape
    rows = min(rows, half)
    assert half % rows == 0

    def body(me_ref, in_ref, o_ref):
        o_ref[...] = in_ref[...]

    return pl.pallas_call(
        body, name=name,
        grid_spec=pltpu.PrefetchScalarGridSpec(
            num_scalar_prefetch=1, grid=(2, half // rows),
            in_specs=[pl.BlockSpec((None, rows, C), lambda h, i, me: (h, i, 0))],
            out_specs=pl.BlockSpec((None, None, rows, C), lambda h, i, me: (me[0], h, i, 0))),
        out_shape=jax.ShapeDtypeStruct((4,) + shard.shape, shard.dtype),
        compiler_params=_cparams(("parallel", "parallel")),
    )(me_idx, shard)


def _pair_split(arrs, name):
    n = len(arrs)

    def body(*refs):
        ins, outs, (ssem, rsem, _) = refs[:n], refs[n:2 * n], refs[2 * n:]
        x, y, c, _ = _place()
        copies = []
        for a in range(n):
            for s in range(4):
                cp = pltpu.make_async_remote_copy(src_ref=ins[a].at[s, 1 - c], dst_ref=outs[a].at[s],
                                                  send_sem=ssem.at[4 * a + s], recv_sem=rsem.at[4 * a + s],
                                                  device_id=(x, y, 1 - c), device_id_type=_MESH)
                cp.start()
                copies.append(cp)
        for cp in copies:
            cp.wait()

    shapes = [jax.ShapeDtypeStruct((4,) + a.shape[2:], a.dtype) for a in arrs]
    return _comm_call(body, arrs, shapes, 4, name)


def _pair_swap(arrs, name):
    n = len(arrs)

    def body(*refs):
        ins, outs, (ssem, rsem, _) = refs[:n], refs[n:2 * n], refs[2 * n:]
        x, y, c, _ = _place()
        copies = []
        for a in range(n):
            cp = pltpu.make_async_remote_copy(src_ref=ins[a], dst_ref=outs[a], send_sem=ssem.at[a], recv_sem=rsem.at[a],
                                              device_id=(x, y, 1 - c), device_id_type=_MESH)
            cp.start()
            copies.append(cp)
        for cp in copies:
            cp.wait()

    return _comm_call(body, arrs, [jax.ShapeDtypeStruct(a.shape, a.dtype) for a in arrs], 1, name)


def _pair_sum(full, recv, c_idx, name, rows=128):
    _, _, half, C = full.shape
    rows = min(rows, half)
    nb = half // rows
    assert half % rows == 0

    def body(c_ref, a_ref, b_ref, o_ref):
        o_ref[...] = (a_ref[...].astype(F32) + b_ref[...].astype(F32)).astype(o_ref.dtype)

    return pl.pallas_call(
        body, name=name,
        grid_spec=pltpu.PrefetchScalarGridSpec(
            num_scalar_prefetch=1, grid=(4, nb),
            in_specs=[pl.BlockSpec((None, None, rows, C), lambda s, i, c: (s, c[0], i, 0)),
                      pl.BlockSpec((None, rows, C), lambda s, i, c: (s, i, 0))],
            out_specs=pl.BlockSpec((None, rows, C), lambda s, i, c: (s, i, 0))),
        out_shape=jax.ShapeDtypeStruct(recv.shape, recv.dtype),
        compiler_params=_cparams(("parallel", "parallel")),
    )(c_idx, full, recv)


_HBM = pl.BlockSpec(memory_space=pltpu.HBM)
_SEM = pl.BlockSpec(memory_space=pltpu.SEMAPHORE)
_SPLIT = dict(has_side_effects=pltpu.SideEffectType.DATAFLOW_SIDE_EFFECTING)


def _hbm(t):
    return pltpu.with_memory_space_constraint(t, pltpu.HBM)


def _ag_copy(buf_ref, ssem, rsem, j, chip, c, me):
    px, py = chip
    return pltpu.make_async_remote_copy(src_ref=buf_ref.at[me, c], dst_ref=buf_ref.at[me, c], send_sem=ssem.at[j],
                                        recv_sem=rsem.at[j], device_id=(px, py, c), device_id_type=_MESH)


def _ag_start(bufs, after, name):
    n = len(bufs)

    def body(*refs):
        buf_refs, (ssem, rsem), token = refs[:n], refs[n + 1:n + 3], refs[-1]
        x, y, c, chips = _place()
        for a in range(n):
            for j, chip in enumerate(chips):
                _ag_copy(buf_refs[a], ssem, rsem, 3 * a + j, chip, c, 2 * x + y).start()
        token[...] = jnp.zeros_like(token)

    out = pl.pallas_call(
        body, name=name,
        out_shape=(pltpu.SemaphoreType.DMA((3 * n,)), pltpu.SemaphoreType.DMA((3 * n,)),
                   *[pltpu.HBM(b.shape, b.dtype) for b in bufs], jax.ShapeDtypeStruct((8, HEAD), F32)),
        in_specs=(*[_HBM] * n, _ANY), out_specs=(_SEM, _SEM, *[_HBM] * n, pl.BlockSpec(memory_space=pltpu.VMEM)),
        input_output_aliases={a: 2 + a for a in range(n)}, compiler_params=pltpu.CompilerParams(**_SPLIT),
    )(*[_hbm(b) for b in bufs], after)
    return out[0], out[1], list(out[2:2 + n]), out[-1]


def _ag_wait(ssem, rsem, bufs, after, name):
    n = len(bufs)

    def body(*refs):
        buf_refs, ssem, rsem = refs[:n], refs[n], refs[n + 1]
        x, y, c, chips = _place()
        for a in range(n):
            for j, chip in enumerate(chips):
                cp = _ag_copy(buf_refs[a], ssem, rsem, 3 * a + j, chip, c, 2 * x + y)
                cp.wait_send()
                cp.wait_recv()

    return list(pl.pallas_call(
        body, name=name, out_shape=tuple(pltpu.HBM(b.shape, b.dtype) for b in bufs),
        in_specs=(*[_HBM] * n, _SEM, _SEM, _ANY), out_specs=tuple([_HBM] * n),
        input_output_aliases={a: a for a in range(n)}, compiler_params=pltpu.CompilerParams(**_SPLIT),
    )(*bufs, ssem, rsem, after))


def _pair_forward(bufs, name):
    n = len(bufs)

    def body(*refs):
        outs, (ssem, rsem) = refs[n:2 * n], refs[2 * n:]
        x, y, c, chips = _place()
        copies = []
        for a in range(n):
            for j, (px, py) in enumerate(chips):
                landed = outs[a].at[2 * px + py, c]
                cp = pltpu.make_async_remote_copy(src_ref=landed, dst_ref=landed, send_sem=ssem.at[3 * a + j],
                                                  recv_sem=rsem.at[3 * a + j], device_id=(x, y, 1 - c),
                                                  device_id_type=_MESH)
                cp.start()
                copies.append(cp)
        for cp in copies:
            cp.wait()

    return pl.pallas_call(
        body, name=name, in_specs=[_ANY] * n, out_specs=[_ANY] * n,
        out_shape=[jax.ShapeDtypeStruct(b.shape, b.dtype) for b in bufs],
        input_output_aliases={a: a for a in range(n)},
        scratch_shapes=[pltpu.SemaphoreType.DMA((3 * n,)), pltpu.SemaphoreType.DMA((3 * n,))],
    )(*bufs)


def _rs_copy(src_ref, land_ref, ssem, rsem, j, chip, c):
    px, py = chip
    return pltpu.make_async_remote_copy(src_ref=src_ref.at[2 * px + py], dst_ref=land_ref.at[j], send_sem=ssem.at[j],
                                        recv_sem=rsem.at[j], device_id=(px, py, c), device_id_type=_MESH)


def _rs_start(part, after, name):
    def body(part_ref, land_ref, after_ref, ssem, rsem, part_thru, land_thru, token):
        x, y, c, chips = _place()
        for j, chip in enumerate(chips):
            _rs_copy(part_ref, land_ref, ssem, rsem, j, chip, c).start()
        token[...] = jnp.zeros_like(token)

    land = jax.ShapeDtypeStruct((3,) + part.shape[1:], part.dtype)
    return pl.pallas_call(
        body, name=name,
        out_shape=(pltpu.SemaphoreType.DMA((3,)), pltpu.SemaphoreType.DMA((3,)), pltpu.HBM(part.shape, part.dtype),
                   pltpu.HBM(land.shape, land.dtype), jax.ShapeDtypeStruct((8, HEAD), F32)),
        in_specs=(_HBM, _HBM, _ANY), out_specs=(_SEM, _SEM, _HBM, _HBM, pl.BlockSpec(memory_space=pltpu.VMEM)),
        input_output_aliases={0: 2, 1: 3}, compiler_params=pltpu.CompilerParams(**_SPLIT),
    )(_hbm(part), _hbm(lax.empty(land.shape, land.dtype)), after)


def _rs_wait(ssem, rsem, part, land, after, name):
    def body(part_ref, land_ref, ssem, rsem, after_ref, part_out, land_out):
        x, y, c, chips = _place()
        for j, chip in enumerate(chips):
            cp = _rs_copy(part_ref, land_ref, ssem, rsem, j, chip, c)
            cp.wait_send()
            cp.wait_recv()

    return pl.pallas_call(
        body, name=name, out_shape=(pltpu.HBM(part.shape, part.dtype), pltpu.HBM(land.shape, land.dtype)),
        in_specs=(_HBM, _HBM, _SEM, _SEM, _ANY), out_specs=(_HBM, _HBM),
        input_output_aliases={0: 0, 1: 1}, compiler_params=pltpu.CompilerParams(**_SPLIT),
    )(part, land, ssem, rsem, after)


def _sum_own_recv(part, land, me_idx, name, rows=128):
    _, half, C = part.shape
    rows = min(rows, half)
    assert half % rows == 0

    def body(me_ref, own_ref, land_ref, o_ref):
        acc = own_ref[...].astype(F32)
        for s in range(3):
            acc = acc + land_ref[s].astype(F32)
        o_ref[...] = acc

    return pl.pallas_call(
        body, name=name,
        grid_spec=pltpu.PrefetchScalarGridSpec(
            num_scalar_prefetch=1, grid=(half // rows,),
            in_specs=[pl.BlockSpec((None, rows, C), lambda i, me: (me[0], i, 0)),
                      pl.BlockSpec((3, rows, C), lambda i, me: (0, i, 0))],
            out_specs=pl.BlockSpec((rows, C), lambda i, me: (i, 0))),
        out_shape=jax.ShapeDtypeStruct((half, C), F32),
        compiler_params=_cparams(("parallel",)),
    )(me_idx, part, land)


def _sum_slots(arr, name, rows=128):
    k, R, C = arr.shape
    rows = min(rows, R)
    assert R % rows == 0

    def body(in_ref, o_ref):
        acc = in_ref[0].astype(F32)
        for s in range(1, k):
            acc = acc + in_ref[s].astype(F32)
        o_ref[...] = acc

    return pl.pallas_call(
        body, name=name, grid=(R // rows,),
        in_specs=[pl.BlockSpec((k, rows, C), lambda i: (0, i, 0))],
        out_specs=pl.BlockSpec((rows, C), lambda i: (i, 0)),
        out_shape=jax.ShapeDtypeStruct((R, C), F32),
        compiler_params=_cparams(("parallel",)),
    )(arr)


ADAM_LR, ADAM_B1, ADAM_B2, ADAM_EPS, ADAM_WD, ADAM_STEP = 0.001, 0.9, 0.999, 1e-08, 0.01, 10


def _adam_math(w, g, m, v):
    m = ADAM_B1 * m + (1.0 - ADAM_B1) * g
    v = ADAM_B2 * v + (1.0 - ADAM_B2) * jnp.square(g)
    m_hat = m / (1.0 - ADAM_B1 ** ADAM_STEP)
    v_hat = v / (1.0 - ADAM_B2 ** ADAM_STEP)
    delta = -ADAM_LR * (m_hat / (jnp.sqrt(v_hat) + ADAM_EPS) + ADAM_WD * w)
    return delta, m, v


def _adamw(w, m, v, layer, mine, other, c_idx, name, g_off=0, prev=None, rows=128):
    _, R, C = w.shape
    half = mine.shape[0]
    rows = min(rows, R)
    assert R % rows == 0 and g_off % rows == 0 and half % rows == 0
    nbh, b0 = half // rows, g_off // rows

    def body(c_ref, w_ref, m_ref, v_ref, mine_ref, other_ref, *rest):
        go, do, mo, vo = rest[-4:]
        in_my_half = (b0 + pl.program_id(0)) // nbh == c_ref[0]
        g = jnp.where(in_my_half, mine_ref[...], other_ref[...])
        delta, m1, v1 = _adam_math(w_ref[...], g, m_ref[...], v_ref[...])
        go[...] = g
        do[...] = delta
        mo[...] = m1
        vo[...] = v1

    blk = pl.BlockSpec((None, rows, C), lambda i, c: (layer, i, 0))
    gblk = pl.BlockSpec((rows, C), lambda i, c: ((b0 + i) % nbh, 0))
    carried = list(prev) if prev is not None else []
    return pl.pallas_call(
        body, name=name,
        grid_spec=pltpu.PrefetchScalarGridSpec(
            num_scalar_prefetch=1, grid=(R // rows,),
            in_specs=[blk] * 3 + [gblk] * 2 + [_ANY] * len(carried), out_specs=[blk] * 4),
        out_shape=[jax.ShapeDtypeStruct(w.shape, F32)] * 4,
        input_output_aliases={6 + k: k for k in range(len(carried))},
        compiler_params=_cparams(("parallel",)),
    )(c_idx, w, m, v, mine, other, *carried)


_WEIGHTS = ['norm_mix', 'norm_mlp', 'norm_ple', 'w_in_e', 'w_out_e', 'hgrn_lb', 'g_norm_a', 'conv_w', 'a_log',
            'dt_bias', 'g_norm_b', 's5_a_re', 's5_a_im', 's5_b_re', 's5_b_im', 's5_c_re', 's5_c_im', 's5_d',
            's5_log_dt', 'w_glu', 'b_glu', 'w_out_o', 'w_up', 'w_down', 'w_ple_gate', 'w_ple_proj', 'final_norm']
_INPUTS = ['x', 'p'] + _WEIGHTS + ['loss_target'] + ['m_' + n for n in _WEIGHTS] + ['v_' + n for n in _WEIGHTS]
_FAMILY = [('w_up', 0, 'col', 2048), ('w_down', 0, 'row', 2048), ('w_ple_gate', 0, 'row', 512), ('w_out_e', 0, 'row', 512),
           ('w_up', 1, 'col', 2048), ('w_down', 1, 'row', 2048), ('w_ple_gate', 1, 'row', 512), ('w_glu', 0, 'row', 512),
           ('w_out_o', 0, 'row', 512)]
_PACK = {('w_up', 0): 0, ('w_down', 0): 0, ('w_ple_gate', 0): 0, ('w_out_e', 0): 0,
         ('w_up', 1): 1, ('w_down', 1): 1, ('w_ple_gate', 1): 1, ('w_glu', 0): 1, ('w_out_o', 0): 1}
_IN_PAD = 8320
_IN_BLK = 1664


def _rms_bwd_fn(x, d_hn, d_res, g):
    _, vjp = jax.vjp(_rms, x, g)
    dx, dg = vjp(d_hn.astype(F32))
    return dx + d_res, dg


def _rms_bwd_both(x, d_hn, d_res, g):
    dx, dg = _rms_bwd_fn(x, d_hn, d_res, g)
    return dx, dx, dg


def _add_res(acc, h):
    return (acc + h,)


def _pack_rows(parts, lanes=128, mult=256):
    flat = jnp.concatenate([q.reshape(-1).astype(F32) for q in parts])
    n = flat.shape[0]
    rows = -(-n // (lanes * mult)) * mult
    return jnp.pad(flat, (0, rows * lanes - n)).reshape(rows, lanes)


def _unpack_rows(buf, shapes):
    flat, out, off = buf.reshape(-1), [], 0
    for s in shapes:
        n = math.prod(s)
        out.append(flat[off:off + n].reshape(s))
        off += n
    return out


def _step(a):
    S = a['x'].shape[1]
    x, tgt = a['x'][0], a['loss_target'][0]
    xi, yi = lax.axis_index("x"), lax.axis_index("y")
    me = 2 * xi + yi
    row = lambda t: t.reshape(1, -1)

    views, pack_of, packs, rt = {}, {}, [[], []], [0, 0]
    for nm, l, kind, r in _FAMILY:
        views[nm, l], pack_of[nm, l] = (kind, rt[_PACK[nm, l]], r), _PACK[nm, l]
        packs[_PACK[nm, l]].append(a[nm][l].astype(BF16))
        rt[_PACK[nm, l]] += r
    small_sh = jnp.concatenate([a['conv_w'][0].reshape(-1), a['s5_d'][0], a['b_glu'][0]]).reshape(16, 256)
    shards = [jnp.concatenate(packs[0], axis=0), jnp.concatenate(packs[1], axis=0), a['w_in_e'][0].astype(BF16),
              a['w_ple_proj'].reshape(512, 512).astype(BF16), small_sh]
    me_idx = me.astype(jnp.int32).reshape(1)
    c_idx = lax.axis_index("c").astype(jnp.int32).reshape(1)
    own = lambda i: _own_slot(shards[i].reshape(2, shards[i].shape[0] // 2, shards[i].shape[1]), me_idx,
                              name=f"own_slot{i}")
    whole = lambda g, t: g.reshape((4,) + t.shape)
    first = _ag_start([own(2), own(3), own(4)], me_idx, name="gather_first_start")
    slots = [own(0), own(1)]
    landed = _pair_forward(_ag_wait(first[0], first[1], first[2], slots[1], name="gather_first_wait"),
                           name="gather_first_pass")
    g_in, g_pp, g_small = [whole(g, t) for g, t in zip(landed, shards[2:])]
    gw = {}
    W = lambda nm, l: dict(b=gw[pack_of[nm, l]], b_view=views[nm, l])
    sems0 = _ag_start([slots[0]], g_in, name="gather0_start")
    win = jnp.pad(g_in.transpose(1, 0, 2).reshape(D_MODEL, 8208), ((0, 0), (0, _IN_PAD - 8208)))
    w_pp = g_pp.reshape(4, 2, 256, 512).transpose(1, 2, 0, 3).reshape(2, 256, D_MODEL)
    g_small = g_small.reshape(4, 4096)
    conv_full = g_small[:, :3072].reshape(4, 4, 768).transpose(1, 0, 2).reshape(4, 3072)
    s5_d_full = g_small[:, 3072:3584].reshape(1, D_MODEL)
    b_glu_full = g_small[:, 3584:].reshape(1, D_MODEL)

    lb, lb_vjp = jax.vjp(lambda hl: jax.nn.softmax(hl, axis=0)[0:1], a['hgrn_lb'])
    pad_h = lambda t: jnp.pad(t, ((0, 0), (0, HEAD - t.shape[1])))
    (arow, dtrow), gdn_vjp = jax.vjp(lambda al, dt: (pad_h(-jnp.exp(al)), pad_h(dt)), a['a_log'], a['dt_bias'])
    s5p, s5_vjp = jax.vjp(_s5_params, a['s5_a_re'][0], a['s5_a_im'][0], a['s5_log_dt'][0], a['s5_b_re'][0],
                          a['s5_b_im'][0], a['s5_c_re'][0], a['s5_c_im'][0])
    bre, bim, cre, cim = [t.astype(BF16) for t in s5p[:4]]
    pw_r, pw_i = _s5_powers(s5p[4], s5p[5])
    gna, gnb = a['g_norm_a'], a['g_norm_b']

    def norm_cast(h, g, dt, name):
        return _rowwise(_rms, [h], [row(g)], [(D_MODEL, dt)], name=name)

    def mlp_ple_fwd(h, l):
        hn = norm_cast(h, a['norm_mlp'][l], BF16, f"rms_mlp{l}")
        up, act = _mm(hn, **W('w_up', l), out_dtypes=(BF16, BF16), name=f"up{l}",
                      epilogue=lambda acc: (acc, jnp.square(jnp.maximum(acc, 0.0))))
        h2 = _mm(act, **W('w_down', l), epilogue=_add_res, extras=(h,), name=f"down{l}")
        hnp = norm_cast(h2, a['norm_ple'][l], BF16, f"rms_ple{l}")
        pp = _mm(a['p'][l, 0], w_pp[l], name=f"ple_proj{l}")
        h3, gpre = _mm(hnp, **W('w_ple_gate', l), extras=(h2, pp), out_dtypes=(F32, F32), name=f"ple_gate{l}",
                       bm=512, epilogue=lambda acc, h2, pp: (h2 + jax.nn.sigmoid(acc) * pp, acc))
        return h3, (h, hn, up, act, h2, hnp, pp, gpre)

    hn0 = norm_cast(x, a['norm_mix'][0], BF16, "rms_mix0")
    proj = _mm(hn0, win, bn=_IN_BLK, name="in_proj", after=(sems0[3],))
    o_a, st_a = _hgrn_fwd(proj, lb, gna)
    act_b = _conv_fwd(proj, conv_full)
    o_b, st_b = _gdn_fwd(act_b, proj, arow, dtrow, gnb)
    merged = jnp.concatenate([o_a, o_b], axis=1)
    landed0 = _ag_wait(sems0[0], sems0[1], sems0[2], merged, name="gather0_wait")
    gw[0] = whole(_pair_forward(landed0, name="gather0_pass")[0], shards[0])
    sems1 = _ag_start([slots[1]], gw[0], name="gather1_start")
    h1 = _mm(merged, **W('w_out_e', 0), epilogue=_add_res, extras=(x,), name="out_e", after=(sems1[3],))
    h3, saved0 = mlp_ple_fwd(h1, 0)

    u = norm_cast(h3, a['norm_mix'][1], F32, "rms_mix1")
    y, xre, xim = _s5_fwd(u, bre, bim, cre, cim, pw_r, pw_i, s5_d_full)
    actg = _rowwise(jax.nn.gelu, [y], [], [(D_MODEL, BF16)], name="gelu")
    landed1 = _ag_wait(sems1[0], sems1[1], sems1[2], actg, name="gather1_wait")
    gw[1] = whole(_pair_forward(landed1, name="gather1_pass")[0], shards[1])
    glu, z = _mm(actg, **W('w_glu', 0), extras=(actg, b_glu_full), out_dtypes=(BF16, F32), name="glu",
                 epilogue=lambda acc, act, b: (act.astype(F32) * jax.nn.sigmoid(acc + b), acc + b))
    h4 = _mm(glu, **W('w_out_o', 0), epilogue=_add_res, extras=(h3,), name="out_o")
    h6, saved1 = mlp_ple_fwd(h4, 1)

    def head(h, t, g):
        def f(h, g):
            e = _rms(h, g) - t
            return 0.5 * jnp.sum(jnp.mean(e * e, axis=-1))
        val, vjp = jax.vjp(f, h, g)
        dh, dg = vjp(jnp.ones((), F32))
        return dh, dg, jnp.full((1, HEAD), val)

    dh, d_final, loss_part = _rowwise(head, [h6, tgt], [row(a['final_norm'])], [(D_MODEL, F32)], sums=(D_MODEL, HEAD),
                                      name="loss_head")
    loss = lax.psum(loss_part[0, 0], ("x", "y", "c"))

    gb = {k: lax.empty((4, rt[k], PACK_COLS), BF16) for k in (0, 1)}

    def into(lhs, rhs, key, name, after=()):
        k = pack_of[key]
        gb[k] = _mm(lhs, rhs, dims="tn", out_into=gb[k], out_view=views[key], out_dtypes=(BF16,), name=name, after=after)

    def mlp_ple_bwd(dh, l, saved, after=()):
        h, hn, up, act, h2, hnp, pp, gpre = saved

        def gate_bwd(d, gpre, pp):
            s = jax.nn.sigmoid(gpre)
            return d * s, d * pp * s * (1.0 - s)

        d_pp, d_gp = _rowwise(gate_bwd, [dh, gpre, pp], [], [(D_MODEL, BF16), (D_MODEL, BF16)], name=f"ple_bwd{l}")
        d_wpp = _mm(a['p'][l, 0], d_pp, dims="tn", name=f"d_ple_proj{l}", after=after)
        into(hnp, d_gp, ('w_ple_gate', l), f"d_ple_gate{l}")
        d_hnp = _mm(d_gp, **W('w_ple_gate', l), dims="nt", name=f"ple_gate_t{l}")
        dh2, dh2_b, d_nple = _rowwise(_rms_bwd_both, [h2, d_hnp, dh], [row(a['norm_ple'][l])],
                                      [(D_MODEL, F32), (D_MODEL, BF16)], sums=(D_MODEL,), name=f"rms_ple_bwd{l}")
        d_up = _mm(dh2_b, **W('w_down', l), dims="nt", extras=(up,), out_dtypes=(BF16,), name=f"down_t{l}",
                   epilogue=lambda acc, up: (acc * 2.0 * jnp.maximum(up.astype(F32), 0.0),))
        into(act, dh2_b, ('w_down', l), f"d_down{l}")
        into(hn, d_up, ('w_up', l), f"d_up{l}")
        d_hn = _mm(d_up, **W('w_up', l), dims="nt", name=f"up_t{l}")
        dh1, dh1_b, d_nmlp = _rowwise(_rms_bwd_both, [h, d_hn, dh2], [row(a['norm_mlp'][l])],
                                      [(D_MODEL, F32), (D_MODEL, BF16)], sums=(D_MODEL,), name=f"rms_mlp_bwd{l}")
        return dh1, dh1_b, d_wpp, d_nple, d_nmlp

    dh4, dh4_b, d_wpp1, d_nple1, d_nmlp1 = mlp_ple_bwd(dh, 1, saved1)

    d_glu = _mm(dh4_b, **W('w_out_o', 0), dims="nt", name="out_o_t")
    into(glu, dh4_b, ('w_out_o', 0), "d_out_o")

    def glu_bwd(d, z, act):
        s, act = jax.nn.sigmoid(z), act.astype(F32)
        dz = d * act * s * (1.0 - s)
        return dz, d * s, jnp.sum(dz, axis=0, keepdims=True)

    d_z, d_actp, d_bglu = _rowwise(glu_bwd, [d_glu, z, actg], [], [(D_MODEL, BF16), (D_MODEL, F32)], sums=(D_MODEL,),
                                   name="glu_bwd")
    into(actg, d_z, ('w_glu', 0), "d_glu")

    def reduce_start(bufs, tag):
        parts = [t.reshape(4, 2, t.shape[1] // 2, t.shape[2]) for t in bufs]
        recv = _pair_split(parts, name=f"pair_split{tag}")
        sums = [_pair_sum(f, r, c_idx, name=f"pair_sum{tag}_{i}") for i, (f, r) in enumerate(zip(parts, recv))]
        return [_rs_start(s, c_idx, name=f"scatter{tag}_{i}_start") for i, s in enumerate(sums)]

    def reduce_finish(started, after, tag):
        halves = []
        for i, (ssem, rsem, part, land, _) in enumerate(started):
            part, land = _rs_wait(ssem, rsem, part, land, after, name=f"scatter{tag}_{i}_wait")
            halves.append(_sum_own_recv(part, land, me_idx, name=f"sum_partials{tag}_{i}"))
        return list(zip(halves, _pair_swap(halves, name=f"swap_halves{tag}")))

    started1 = reduce_start([gb[1]], 1)

    def gelu_bwd(acc, dap, y):
        _, vjp = jax.vjp(jax.nn.gelu, y)
        return vjp(acc + dap)

    dy = _mm(d_z, **W('w_glu', 0), dims="nt", extras=(d_actp, y), epilogue=gelu_bwd, name="glu_t",
             after=(started1[0][4],))
    du, d_bre, d_bim, d_cre, d_cim, d_lr, d_li, d_s5d = _s5_bwd(dy, u, xre, xim, bre, bim, cre, cim,
                                                                 pw_r[:, ::-1], -pw_i[:, ::-1], s5_d_full)
    dh3, d_nmix1 = _rowwise(_rms_bwd_fn, [h3, du, dh4], [row(a['norm_mix'][1])], [(D_MODEL, F32)], sums=(D_MODEL,),
                            name="rms_mix1_bwd")
    d_are, d_aim, d_logdt, d_sbre, d_sbim, d_scre, d_scim = s5_vjp(
        (d_bre, d_bim, d_cre, d_cim, d_lr.sum(axis=1), d_li.sum(axis=1)))

    dh1, dh1_b, d_wpp0, d_nple0, d_nmlp0 = mlp_ple_bwd(dh3, 0, saved0)

    d_merged = _mm(dh1_b, **W('w_out_e', 0), dims="nt", name="out_e_t")
    into(merged, dh1_b, ('w_out_e', 0), "d_out_e")
    started0 = reduce_start([gb[0]], 0)
    dq, df, dv, dg, d_lb, d_gna = _hgrn_bwd(proj, lb, gna, st_a, d_merged, after=(started0[0][4],))
    dqb, dkb, dvb, dzb, d_tail, d_arow, d_dtrow, d_gnb = _gdn_bwd(act_b, proj, arow, dtrow, gnb, st_b, d_merged)
    d_pre, d_conv = _conv_bwd(proj, conv_full, dqb, dkb, dvb)
    d_proj = jnp.concatenate([dq, df, dv, dg, d_pre, dzb, d_tail], axis=1)
    d_win = _mm(hn0, d_proj, dims="tn", bn=_IN_BLK, out_dtypes=(BF16,), name="d_in_proj")
    d_hn0 = _mm(d_proj, win, dims="nt", bk=_IN_BLK, name="in_proj_t")
    grad_x, d_nmix0 = _rowwise(_rms_bwd_fn, [x, d_hn0, dh1], [row(a['norm_mix'][0])], [(D_MODEL, F32)], sums=(D_MODEL,),
                               name="rms_mix0_bwd")
    (d_hlb,) = lb_vjp(d_lb.reshape(1, 1024))
    d_alog, d_dtb = gdn_vjp((d_arow, d_dtrow))

    d_win_sh = d_win[:, :8208].reshape(D_MODEL, 4, 2052).transpose(1, 0, 2)
    d_wpp_sh = jnp.stack([d_wpp0, d_wpp1]).astype(BF16).reshape(2, 256, 4, 512).transpose(2, 0, 1, 3).reshape(4, 512, 512)
    started2 = reduce_start([d_win_sh, d_wpp_sh], 2)
    small = {
        'norm_mix': jnp.concatenate([d_nmix0, d_nmix1]), 'norm_mlp': jnp.concatenate([d_nmlp0, d_nmlp1]),
        'norm_ple': jnp.concatenate([d_nple0, d_nple1]), 'hgrn_lb': d_hlb, 'g_norm_a': d_gna, 'conv_w': d_conv,
        'a_log': d_alog, 'dt_bias': d_dtb, 'g_norm_b': d_gnb, 's5_a_re': d_are, 's5_a_im': d_aim, 's5_b_re': d_sbre,
        's5_b_im': d_sbim, 's5_c_re': d_scre, 's5_c_im': d_scim, 's5_d': d_s5d, 's5_log_dt': d_logdt,
        'b_glu': d_bglu, 'final_norm': d_final}
    s_names = list(small)
    s_shapes = [tuple(small[n].shape) for n in s_names]
    mine_small = _pack_rows([small[n] for n in s_names])
    chip_small = _rowwise(lambda p, o: p + o, [mine_small, _pair_swap([mine_small], name="swap_small")[0]], [],
                          [(128, F32)], name="add_small")
    n_small = chip_small.shape[0]
    small_go = _ag_start([_own_slot(chip_small.reshape(2, n_small // 2, 128), me_idx, name="own_slot_small")],
                         started2[-1][4], name="gather_small_start")
    red = {1: reduce_finish(started1, small_go[3], 1)[0]}
    red[0] = reduce_finish(started0, small_go[3], 0)[0]
    res = {}

    def adam_layer(nm, layer, key, g_off=None):
        g_off = views[nm, layer][1] if g_off is None else g_off
        res[nm] = tuple(_adamw(a[nm], a['m_' + nm], a['v_' + nm], layer, *red[key], c_idx, name=f"adam_{nm}{layer}",
                               g_off=g_off, prev=res.get(nm)))

    for nm, layer in (('w_glu', 0), ('w_out_o', 0), ('w_up', 1), ('w_down', 1), ('w_ple_gate', 1)):
        adam_layer(nm, layer, 1)
    for nm in ('w_up', 'w_down', 'w_ple_gate', 'w_out_e'):
        adam_layer(nm, 0, 0)

    small_all = _pair_forward(_ag_wait(small_go[0], small_go[1], small_go[2], res['w_out_e'][0],
                                       name="gather_small_wait"), name="gather_small_pass")[0]
    reduced = _sum_slots(small_all.reshape(4, n_small, 128), name="sum_small")
    sg = dict(zip(s_names, _unpack_rows(reduced, s_shapes)))
    sg['conv_w'] = lax.dynamic_slice_in_dim(sg['conv_w'], me * 768, 768, axis=1)
    sg['s5_d'] = lax.dynamic_slice_in_dim(sg['s5_d'], me * 512, 512, axis=1)
    sg['b_glu'] = lax.dynamic_slice_in_dim(sg['b_glu'], me * 512, 512, axis=1)
    sg = {n: sg[n].reshape(a[n].shape) for n in s_names}
    w_pack, m_pack, v_pack, g_pack = [_pack_rows([src[pre + n] for n in s_names])
                                      for src, pre in ((a, ''), (a, 'm_'), (a, 'v_'), (sg, ''))]
    sd, sm, sv = _rowwise(_adam_math, [w_pack, g_pack, m_pack, v_pack], [], [(128, F32)] * 3, name="adam_small")
    w_shapes = [tuple(a[n].shape) for n in s_names]
    res.update({n: (sg[n], d_, m_, v_) for n, d_, m_, v_ in zip(s_names, _unpack_rows(sd, w_shapes),
                                                               _unpack_rows(sm, w_shapes), _unpack_rows(sv, w_shapes))})

    red['in'], red['pp'] = reduce_finish(started2, sd, 2)
    adam_layer('w_in_e', 0, 'in', 0)
    adam_layer('w_ple_proj', 0, 'pp', 0)
    adam_layer('w_ple_proj', 1, 'pp', 256)

    return (loss, grad_x[None], *[res[n][0] for n in _WEIGHTS], *[res[n][1] for n in _WEIGHTS],
            *[res[n][2] for n in _WEIGHTS], *[res[n][3] for n in _WEIGHTS])


def kernel(x, p, norm_mix, norm_mlp, norm_ple, w_in_e, w_out_e, hgrn_lb, g_norm_a, conv_w, a_log, dt_bias, g_norm_b, s5_a_re, s5_a_im, s5_b_re, s5_b_im, s5_c_re, s5_c_im, s5_d, s5_log_dt, w_glu, b_glu, w_out_o, w_up, w_down, w_ple_gate, w_ple_proj, final_norm, loss_target, m_norm_mix, m_norm_mlp, m_norm_ple, m_w_in_e, m_w_out_e, m_hgrn_lb, m_g_norm_a, m_conv_w, m_a_log, m_dt_bias, m_g_norm_b, m_s5_a_re, m_s5_a_im, m_s5_b_re, m_s5_b_im, m_s5_c_re, m_s5_c_im, m_s5_d, m_s5_log_dt, m_w_glu, m_b_glu, m_w_out_o, m_w_up, m_w_down, m_w_ple_gate, m_w_ple_proj, m_final_norm, v_norm_mix, v_norm_mlp, v_norm_ple, v_w_in_e, v_w_out_e, v_hgrn_lb, v_g_norm_a, v_conv_w, v_a_log, v_dt_bias, v_g_norm_b, v_s5_a_re, v_s5_a_im, v_s5_b_re, v_s5_b_im, v_s5_c_re, v_s5_c_im, v_s5_d, v_s5_log_dt, v_w_glu, v_b_glu, v_w_out_o, v_w_up, v_w_down, v_w_ple_gate, v_w_ple_proj, v_final_norm):
    args = (x, p, norm_mix, norm_mlp, norm_ple, w_in_e, w_out_e, hgrn_lb, g_norm_a, conv_w, a_log, dt_bias, g_norm_b, s5_a_re, s5_a_im, s5_b_re, s5_b_im, s5_c_re, s5_c_im, s5_d, s5_log_dt, w_glu, b_glu, w_out_o, w_up, w_down, w_ple_gate, w_ple_proj, final_norm, loss_target, m_norm_mix, m_norm_mlp, m_norm_ple, m_w_in_e, m_w_out_e, m_hgrn_lb, m_g_norm_a, m_conv_w, m_a_log, m_dt_bias, m_g_norm_b, m_s5_a_re, m_s5_a_im, m_s5_b_re, m_s5_b_im, m_s5_c_re, m_s5_c_im, m_s5_d, m_s5_log_dt, m_w_glu, m_b_glu, m_w_out_o, m_w_up, m_w_down, m_w_ple_gate, m_w_ple_proj, m_final_norm, v_norm_mix, v_norm_mlp, v_norm_ple, v_w_in_e, v_w_out_e, v_hgrn_lb, v_g_norm_a, v_conv_w, v_a_log, v_dt_bias, v_g_norm_b, v_s5_a_re, v_s5_a_im, v_s5_b_re, v_s5_b_im, v_s5_c_re, v_s5_c_im, v_s5_d, v_s5_log_dt, v_w_glu, v_b_glu, v_w_out_o, v_w_up, v_w_down, v_w_ple_gate, v_w_ple_proj, v_final_norm)
    return _step(dict(zip(_INPUTS, args)))
```

```python
import functools
import math

import jax
import jax.numpy as jnp
from jax import lax
from jax.experimental import pallas as pl
from jax.experimental.pallas import tpu as pltpu

F32 = jnp.float32
BF16 = jnp.bfloat16

D_MODEL = 2048
SEQ = 4096
NORM_EPS = 1e-6
CHUNK = 64
HEAD = 128
VMEM_LIMIT = 56 * 1024 * 1024


_NN = (((1,), (0,)), ((), ()))
_NT = (((1,), (1,)), ((), ()))
_TN = (((0,), (0,)), ((), ()))
_HI = lax.Precision.HIGHEST
_NEG = -1e30


def _cparams(sem, **kw):
    return pltpu.CompilerParams(dimension_semantics=sem, vmem_limit_bytes=VMEM_LIMIT, **kw)


PACK_COLS = 2048


def _view_shape(view):
    kind, _, r = view
    return (4 * r, PACK_COLS) if kind == "row" else (r, 4 * PACK_COLS)


def _view_spec(view, rb, cb, row_of, col_of):
    kind, off, r = view
    assert off % rb == 0 and r % rb == 0 and PACK_COLS % cb == 0, (view, rb, cb)
    if kind == "row":
        nrb = r // rb
        return pl.BlockSpec((None, rb, cb), lambda i, j, k: (row_of(i, j, k) // nrb,
                                                             off // rb + row_of(i, j, k) % nrb, col_of(i, j, k)))
    ncb = PACK_COLS // cb
    return pl.BlockSpec((None, rb, cb), lambda i, j, k: (col_of(i, j, k) // ncb,
                                                         off // rb + row_of(i, j, k), col_of(i, j, k) % ncb))


def _mm(a, b, *, dims="nn", epilogue=None, extras=(), out_dtypes=(F32,), bm=1024, bn=1024, bk=2048, name,
        b_view=None, out_into=None, out_view=None, after=()):
    b_shape = _view_shape(b_view) if b_view is not None else b.shape
    if dims == "tn":
        (K, M), (K2, N) = a.shape, b_shape
    elif dims == "nt":
        (M, K), (N, K2) = a.shape, b_shape
    else:
        (M, K), (K2, N) = a.shape, b_shape
    assert K == K2, (a.shape, b_shape, dims)
    if b_view is not None and dims == "nt":
        bn = min(bn, b_view[2])
    if b_view is not None and dims != "nt":
        bk = min(bk, b_view[2])
    if out_view is not None:
        bm = min(bm, out_view[2])
    bm, bn, bk = min(bm, M), min(bn, N), min(bk, K)
    assert M % bm == 0 and N % bn == 0 and K % bk == 0, (M, N, K, bm, bn, bk)
    nk = K // bk
    ii, jj, kk = (lambda i, j, k: i), (lambda i, j, k: j), (lambda i, j, k: k)
    if dims == "tn":
        a_spec = pl.BlockSpec((bk, bm), lambda i, j, k: (k, i))
        dn = _TN
    else:
        a_spec = pl.BlockSpec((bm, bk), lambda i, j, k: (i, k))
        dn = _NT if dims == "nt" else _NN
    if dims == "nt":
        b_spec = _view_spec(b_view, bn, bk, jj, kk) if b_view else pl.BlockSpec((bn, bk), lambda i, j, k: (j, k))
    else:
        b_spec = _view_spec(b_view, bk, bn, kk, jj) if b_view else pl.BlockSpec((bk, bn), lambda i, j, k: (k, j))
    e_specs = []
    for e in extras:
        if e.shape == (M, N):
            e_specs.append(pl.BlockSpec((bm, bn), lambda i, j, k: (i, j)))
        else:
            assert e.shape == (1, N), e.shape
            e_specs.append(pl.BlockSpec((1, bn), lambda i, j, k: (0, j)))
    ne, no = len(extras), len(out_dtypes)
    if epilogue is None:
        epilogue = lambda acc: (acc,)
    into = out_into is not None

    def body(a_ref, b_ref, *rest):
        e_refs, rest = rest[:ne], rest[ne + (1 if into else 0) + len(after):]
        o_refs = rest[:no]
        part = lax.dot_general(a_ref[...].astype(BF16), b_ref[...].astype(BF16), dn, preferred_element_type=F32)

        def finish(total):
            outs = epilogue(total, *[e[...] for e in e_refs])
            for o, v in zip(o_refs, outs):
                o[...] = v.astype(o.dtype)

        if nk == 1:
            finish(part)
            return
        acc, k = rest[no], pl.program_id(2)

        @pl.when(k == 0)
        def _():
            acc[...] = part

        @pl.when((k > 0) & (k < nk - 1))
        def _():
            acc[...] += part

        @pl.when(k == nk - 1)
        def _():
            finish(acc[...] + part)

    if into:
        assert no == 1 and _view_shape(out_view) == (M, N), (out_view, M, N)
        out_specs = [_view_spec(out_view, bm, bn, ii, jj)]
        out_shape = [jax.ShapeDtypeStruct(out_into.shape, out_into.dtype)]
        extra_in, extra_specs, alias = [out_into], [pl.BlockSpec(memory_space=pl.ANY)], {2 + ne: 0}
    else:
        out_specs = [pl.BlockSpec((bm, bn), lambda i, j, k: (i, j)) for _ in out_dtypes]
        out_shape = [jax.ShapeDtypeStruct((M, N), dt) for dt in out_dtypes]
        extra_in, extra_specs, alias = [], [], {}
    outs = pl.pallas_call(
        body, name=name,
        grid=(M // bm, N // bn, nk),
        in_specs=[a_spec, b_spec] + e_specs + extra_specs + [pl.BlockSpec(memory_space=pl.ANY)] * len(after),
        out_specs=out_specs, out_shape=out_shape,
        scratch_shapes=[pltpu.VMEM((bm, bn), F32)] if nk > 1 else [],
        input_output_aliases=alias,
        compiler_params=_cparams(("parallel", "parallel", "arbitrary")),
    )(a, b, *extras, *extra_in, *after)
    return outs[0] if no == 1 else tuple(outs)


def _rowwise(fn, tiled, bcast, outs, sums=(), *, rows=256, name):
    S = tiled[0].shape[0]
    rows = min(rows, S)
    assert S % rows == 0
    nt, nb, no, ns = len(tiled), len(bcast), len(outs), len(sums)

    def body(*refs):
        t_refs, b_refs = refs[:nt], refs[nt:nt + nb]
        o_refs, s_refs = refs[nt + nb:nt + nb + no], refs[nt + nb + no:]
        res = fn(*[r[...] for r in t_refs], *[r[...] for r in b_refs])
        res = res if isinstance(res, (tuple, list)) else (res,)
        for o, v in zip(o_refs, res[:no]):
            o[...] = v.astype(o.dtype)
        if ns:
            @pl.when(pl.program_id(0) == 0)
            def _():
                for s in s_refs:
                    s[...] = jnp.zeros_like(s)
            for s, v in zip(s_refs, res[no:]):
                s[...] += v

    res = pl.pallas_call(
        body, name=name,
        grid=(S // rows,),
        in_specs=[pl.BlockSpec((rows, t.shape[1]), lambda i: (i, 0)) for t in tiled]
        + [pl.BlockSpec(b.shape, lambda i, nd=b.ndim: (0,) * nd) for b in bcast],
        out_specs=[pl.BlockSpec((rows, c), lambda i: (i, 0)) for c, _ in outs]
        + [pl.BlockSpec((1, c), lambda i: (0, 0)) for c in sums],
        out_shape=[jax.ShapeDtypeStruct((S, c), dt) for c, dt in outs]
        + [jax.ShapeDtypeStruct((1, c), F32) for c in sums],
        compiler_params=_cparams(("arbitrary",)),
    )(*tiled, *bcast)
    return res[0] if len(res) == 1 else tuple(res)


def _rms(x, g):
    return x * lax.rsqrt(jnp.mean(x * x, axis=-1, keepdims=True) + NORM_EPS) * g


def _mxu(a, b, dn):
    if a.ndim == 3:
        ((ca,), (cb,)), _ = dn
        dn = (((ca + 1,), (cb + 1,)), ((0,), (0,)))
    return lax.dot_general(a, b, dn, preferred_element_type=F32)


def _split(a):
    hi = a.astype(BF16)
    return hi, (a - hi.astype(F32)).astype(BF16)


def _passes(a, b, dn, three):
    if not three:
        return _mxu(a.astype(BF16), b.astype(BF16), dn)
    (ah, al), (bh, bl) = _split(a), _split(b)
    return _mxu(ah, bh, dn) + (_mxu(ah, bl, dn) + _mxu(al, bh, dn))


def _dot_grads(a, b, g, dn, three):
    if dn == _NN:
        return _passes(g, b, _NT, three), _passes(a, g, _TN, three)
    if dn == _NT:
        return _passes(g, b, _NN, three), _passes(g, a, _TN, three)
    assert dn == _TN
    return _passes(b, g, _NT, three), _passes(a, g, _NN, three)


@functools.partial(jax.custom_vjp, nondiff_argnums=(2,))
def _dot(a, b, dn=_NN):
    return _passes(a, b, dn, False)


_dot.defvjp(lambda a, b, dn: (_passes(a, b, dn, False), (a, b)),
            lambda dn, res, g: _dot_grads(res[0], res[1], g, dn, False))


@functools.partial(jax.custom_vjp, nondiff_argnums=(2,))
def _dot3(a, b, dn=_NN):
    return _passes(a, b, dn, True)


_dot3.defvjp(lambda a, b, dn: (_passes(a, b, dn, True), (a, b)),
             lambda dn, res, g: _dot_grads(res[0], res[1], g, dn, True))


def _tri_times(x, dn):
    tri = _tril(x.shape[-2]).astype(BF16)
    if x.ndim == 3:
        tri = jnp.broadcast_to(tri, (x.shape[0],) + tri.shape)
    hi, lo = _split(x)
    lo2 = (x - hi.astype(F32) - lo.astype(F32)).astype(BF16)
    return _mxu(tri, hi, dn) + (_mxu(tri, lo, dn) + _mxu(tri, lo2, dn))


@jax.custom_vjp
def _cumsum_rows(x):
    return _tri_times(x, _NN)


_cumsum_rows.defvjp(lambda x: (_tri_times(x, _NN), None), lambda _, g: (_tri_times(g, _TN),))


def _iota(shape, dim):
    return lax.broadcasted_iota(jnp.int32, shape, dim)


def _tril(n, strict=False):
    r, c = _iota((n, n), 0), _iota((n, n), 1)
    return (r > c) if strict else (r >= c)


@functools.partial(jax.custom_vjp, nondiff_argnums=(1,))
def _roll_rows(x, r):
    return pltpu.roll(x, r, 0)


def _roll_rows_fwd(x, r):
    return pltpu.roll(x, r, 0), None


def _roll_rows_bwd(r, _, g):
    return (pltpu.roll(g, (g.shape[0] - r) % g.shape[0], 0),)


_roll_rows.defvjp(_roll_rows_fwd, _roll_rows_bwd)


def _head_norm_gate(o, gn, gate):
    return _rms(o, gn) * jax.nn.silu(gate)


_SUB = 16
_HPS = 8


def _hgrn_chunk(q, fpre, v, gate, lb, gna, st):
    c = q.shape[0]
    forget = lb + (1.0 - lb) * jax.nn.sigmoid(fpre)
    k = 1.0 - forget
    logf = jnp.log(forget)
    cum = _cumsum_rows(logf)
    cum_end = jnp.sum(logf, axis=0, keepdims=True)
    o = _dot(q * jnp.exp(cum), st, _NT)
    st_new = st * jnp.exp(cum_end) + _dot(v, k * jnp.exp(cum_end - cum), _TN)
    t = _iota((c, 1), 0)
    s_off = jnp.zeros((c, c), F32)
    for i in range(1, c // _SUB):
        before = t < i * _SUB
        c_i = jnp.sum(jnp.where(before, logf, 0.0), axis=0, keepdims=True)
        in_blk = (t >= i * _SUB) & (t < (i + 1) * _SUB)
        qi = jnp.where(in_blk, q * jnp.exp(jnp.minimum(cum - c_i, 0.0)), 0.0)
        ki = jnp.where(before, k * jnp.exp(jnp.minimum(c_i - cum, 0.0)), 0.0)
        s_off = s_off + _dot(qi, ki, _NT)
    o = o + _dot(s_off, v)
    tmod = t % _SUB
    for r in range(_SUB):
        kr, cr, vr = (k, cum, v) if r == 0 else (_roll_rows(k, r), _roll_rows(cum, r), _roll_rows(v, r))
        w = q * kr * jnp.exp(jnp.where(tmod >= r, cum - cr, _NEG))
        o = o + jnp.sum(w, axis=1, keepdims=True) * vr
    return _head_norm_gate(o, gna, gate), st_new


def _hgrn_fwd(proj, lb, gna):
    S = proj.shape[0]
    n_chunks, heads = S // CHUNK, 8

    def body(q_ref, f_ref, v_ref, g_ref, lb_ref, gna_ref, o_ref, st_out, st):
        n, hb = pl.program_id(0), pl.program_id(1)

        @pl.when(n == 0)
        def _():
            st[hb] = jnp.zeros((_HPS, HEAD, HEAD), F32)

        sls = [slice(hh * HEAD, (hh + 1) * HEAD) for hh in range(_HPS)]
        st0 = [st[hb, hh] for hh in range(_HPS)]
        ins = [(q_ref[:, sl], f_ref[:, sl], v_ref[:, sl], g_ref[:, sl], lb_ref[:, sl]) for sl in sls]
        gna = gna_ref[...]
        res = [_hgrn_chunk(*ins[hh], gna, st0[hh]) for hh in range(_HPS)]
        for hh, sl in enumerate(sls):
            st_out[hh] = st0[hh]
            o_ref[:, sl] = res[hh][0].astype(o_ref.dtype)
            st[hb, hh] = res[hh][1]

    wide = _HPS * HEAD
    sec = lambda s: pl.BlockSpec((CHUNK, wide), lambda n, h, s=s: (n, (8 // _HPS) * s + h))
    return pl.pallas_call(
        body, name="hgrn_fwd", grid=(n_chunks, heads // _HPS),
        in_specs=[sec(0), sec(1), sec(2), sec(3),
                  pl.BlockSpec((1, wide), lambda n, h: (0, h)), pl.BlockSpec((1, HEAD), lambda n, h: (0, 0))],
        out_specs=[pl.BlockSpec((CHUNK, wide), lambda n, h: (n, h)),
                   pl.BlockSpec((None, _HPS, HEAD, HEAD), lambda n, h: (n, h, 0, 0))],
        out_shape=[jax.ShapeDtypeStruct((S, 1024), BF16),
                   jax.ShapeDtypeStruct((n_chunks, heads, HEAD, HEAD), F32)],
        scratch_shapes=[pltpu.VMEM((heads // _HPS, _HPS, HEAD, HEAD), F32)],
        compiler_params=_cparams(("arbitrary", "arbitrary")),
    )(proj, proj, proj, proj, lb, gna)


def _hgrn_bwd(proj, lb, gna, states, d_o, after=()):
    S = proj.shape[0]
    n_chunks, heads = S // CHUNK, 8

    def body(q_ref, f_ref, v_ref, g_ref, lb_ref, gna_ref, st_ref, do_ref, *rest):
        dq_ref, df_ref, dv_ref, dg_ref, dlb_ref, dgna_ref, dst = rest[len(after):]
        n, hb = pl.program_id(0), pl.program_id(1)

        @pl.when(n == 0)
        def _():
            dst[hb] = jnp.zeros((_HPS, HEAD, HEAD), F32)
            dlb_ref[hb] = jnp.zeros((_HPS, 1, HEAD), F32)

        @pl.when((n == 0) & (hb == 0))
        def _():
            dgna_ref[...] = jnp.zeros_like(dgna_ref)

        sls = [slice(hh * HEAD, (hh + 1) * HEAD) for hh in range(_HPS)]
        gna = gna_ref[...]
        ins = [(q_ref[:, sl], f_ref[:, sl], v_ref[:, sl], g_ref[:, sl], lb_ref[:, sl], gna, st_ref[hh])
               for hh, sl in enumerate(sls)]
        cts = [(do_ref[:, sl], dst[hb, hh]) for hh, sl in enumerate(sls)]
        grads = [jax.vjp(_hgrn_chunk, *ins[hh])[1](cts[hh]) for hh in range(_HPS)]
        dgna_sum = jnp.zeros((1, HEAD), F32)
        for hh, sl in enumerate(sls):
            dq, df, dv, dg, dlb, dgna, dst0 = grads[hh]
            dq_ref[:, sl] = dq.astype(dq_ref.dtype)
            df_ref[:, sl] = df.astype(df_ref.dtype)
            dv_ref[:, sl] = dv.astype(dv_ref.dtype)
            dg_ref[:, sl] = dg.astype(dg_ref.dtype)
            dlb_ref[hb, hh] += dlb
            dgna_sum = dgna_sum + dgna
            dst[hb, hh] = dst0
        dgna_ref[...] += dgna_sum

    rev = lambda n: n_chunks - 1 - n
    wide = _HPS * HEAD
    sec = lambda s: pl.BlockSpec((CHUNK, wide), lambda n, h, s=s: (rev(n), (8 // _HPS) * s + h))
    out = pl.BlockSpec((CHUNK, wide), lambda n, h: (rev(n), h))
    return pl.pallas_call(
        body, name="hgrn_bwd", grid=(n_chunks, heads // _HPS),
        in_specs=[sec(0), sec(1), sec(2), sec(3),
                  pl.BlockSpec((1, wide), lambda n, h: (0, h)), pl.BlockSpec((1, HEAD), lambda n, h: (0, 0)),
                  pl.BlockSpec((None, _HPS, HEAD, HEAD), lambda n, h: (rev(n), h, 0, 0)),
                  pl.BlockSpec((CHUNK, wide), lambda n, h: (rev(n), h))] + [pl.BlockSpec(memory_space=pl.ANY)] * len(after),
        out_specs=[out, out, out, out,
                   pl.BlockSpec((heads // _HPS, _HPS, 1, HEAD), lambda n, h: (0, 0, 0, 0)),
                   pl.BlockSpec((1, HEAD), lambda n, h: (0, 0))],
        out_shape=[jax.ShapeDtypeStruct((S, 1024), BF16)] * 4
        + [jax.ShapeDtypeStruct((heads // _HPS, _HPS, 1, HEAD), F32), jax.ShapeDtypeStruct((1, HEAD), F32)],
        scratch_shapes=[pltpu.VMEM((heads // _HPS, _HPS, HEAD, HEAD), F32)],
        compiler_params=_cparams(("arbitrary", "arbitrary")),
    )(proj, proj, proj, proj, lb, gna, states, d_o, *after)


def _l2n(t):
    return t * lax.rsqrt(jnp.sum(t * t, axis=-1, keepdims=True) + NORM_EPS)


def _unit_lower_inverse(low):
    c = low.shape[-1]
    inv = (_iota((c, c), 0) == _iota((c, c), 1)).astype(F32) - low
    p = low
    span = 2
    while span < c:
        p = _dot3(p, p)
        inv = inv + _dot3(inv, p)
        span *= 2
    return inv


def _gdn_chunk(qc, kc, v, z, tail, h0, arow, dtrow, gnb, st):
    nb, c = qc.shape[0], qc.shape[1]
    head = h0 + _iota((nb, c, HEAD), 0)
    lane = _iota((nb, c, HEAD), 2)
    la_all = arow * jax.nn.softplus(tail + dtrow)
    la = jnp.sum(jnp.where(lane == head, la_all[None], 0.0), axis=2, keepdims=True)
    beta = jnp.sum(jnp.where(lane == head + 8, jax.nn.sigmoid(tail)[None], 0.0), axis=2, keepdims=True)
    la_b = jnp.broadcast_to(la, (nb, c, HEAD))
    cum = _cumsum_rows(la_b)
    cmat = _cumsum_rows(jnp.broadcast_to(la, (nb, c, c)))
    cum_end = jnp.sum(la_b, axis=1, keepdims=True)
    decay = jnp.exp(jnp.where(_tril(c), cmat - jnp.swapaxes(cmat, 1, 2), _NEG))
    q = _l2n(qc) * (HEAD ** -0.5)
    k = _l2n(kc)
    k_beta = k * beta
    low = jnp.where(_tril(c, strict=True), _dot(k_beta, k, _NT) * decay, 0.0)
    inv = _unit_lower_inverse(low)
    u, w = _dot(inv, v * beta), _dot(inv, k_beta * jnp.exp(cum))
    intra = _dot(q, k, _NT) * decay
    v_new = u - _dot(w, st, _NT)
    o = _dot(q * jnp.exp(cum), st, _NT) + _dot(intra, v_new)
    st_new = st * jnp.exp(cum_end) + _dot(v_new, k * jnp.exp(cum_end - cum), _TN)
    return _head_norm_gate(o, gnb, z), st_new


def _gdn_specs(n_of):
    wide = _HPS * HEAD
    qkv = lambda s: pl.BlockSpec((CHUNK, wide), lambda n, h, s=s: (n_of(n), (8 // _HPS) * s + h))
    return [qkv(0), qkv(1), qkv(2),
            pl.BlockSpec((CHUNK, wide), lambda n, h: (n_of(n), 56 // _HPS + h)),
            pl.BlockSpec((CHUNK, HEAD), lambda n, h: (n_of(n), 64)),
            pl.BlockSpec((1, HEAD), lambda n, h: (0, 0)), pl.BlockSpec((1, HEAD), lambda n, h: (0, 0)),
            pl.BlockSpec((1, HEAD), lambda n, h: (0, 0))]


def _gdn_fwd(act, proj, arow, dtrow, gnb):
    S = act.shape[0]
    n_chunks, heads = S // CHUNK, 8

    def body(q_ref, k_ref, v_ref, z_ref, t_ref, a_ref, dt_ref, gnb_ref, o_ref, st_out, st):
        n, hb = pl.program_id(0), pl.program_id(1)

        @pl.when(n == 0)
        def _():
            st[hb] = jnp.zeros((_HPS, HEAD, HEAD), F32)

        sls = [slice(hh * HEAD, (hh + 1) * HEAD) for hh in range(_HPS)]
        heads_of = lambda ref: jnp.stack([ref[:, sl] for sl in sls])
        st0 = st[hb]
        o, st1 = _gdn_chunk(heads_of(q_ref), heads_of(k_ref), heads_of(v_ref), heads_of(z_ref), t_ref[...], hb * _HPS,
                            a_ref[...], dt_ref[...], gnb_ref[...], st0)
        st_out[...] = st0
        st[hb] = st1
        for hh, sl in enumerate(sls):
            o_ref[:, sl] = o[hh].astype(o_ref.dtype)

    return pl.pallas_call(
        body, name="gdn_fwd", grid=(n_chunks, heads // _HPS),
        in_specs=_gdn_specs(lambda n: n),
        out_specs=[pl.BlockSpec((CHUNK, _HPS * HEAD), lambda n, h: (n, h)),
                   pl.BlockSpec((None, _HPS, HEAD, HEAD), lambda n, h: (n, h, 0, 0))],
        out_shape=[jax.ShapeDtypeStruct((S, 1024), BF16),
                   jax.ShapeDtypeStruct((n_chunks, heads, HEAD, HEAD), F32)],
        scratch_shapes=[pltpu.VMEM((heads // _HPS, _HPS, HEAD, HEAD), F32)],
        compiler_params=_cparams(("arbitrary", "arbitrary")),
    )(act, act, act, proj, proj, arow, dtrow, gnb)


def _gdn_bwd(act, proj, arow, dtrow, gnb, states, d_o):
    S = act.shape[0]
    n_chunks, heads = S // CHUNK, 8

    def body(q_ref, k_ref, v_ref, z_ref, t_ref, a_ref, dt_ref, gnb_ref, st_ref, do_ref,
             dq_ref, dk_ref, dv_ref, dz_ref, dt_out, da_ref, ddt_ref, dgnb_ref, dst, dtail_acc):
        n, hb = pl.program_id(0), pl.program_id(1)
        n_hb = heads // _HPS

        @pl.when(n == 0)
        def _():
            dst[hb] = jnp.zeros((_HPS, HEAD, HEAD), F32)

        @pl.when((n == 0) & (hb == 0))
        def _():
            da_ref[...] = jnp.zeros_like(da_ref)
            ddt_ref[...] = jnp.zeros_like(ddt_ref)
            dgnb_ref[...] = jnp.zeros_like(dgnb_ref)

        sls = [slice(hh * HEAD, (hh + 1) * HEAD) for hh in range(_HPS)]
        heads_of = lambda ref: jnp.stack([ref[:, sl] for sl in sls])
        fn = lambda qc, kc, v, z, tail, arow, dtrow, gnb, st: _gdn_chunk(qc, kc, v, z, tail, hb * _HPS, arow, dtrow,
                                                                        gnb, st)
        _, vjp = jax.vjp(fn, heads_of(q_ref), heads_of(k_ref), heads_of(v_ref), heads_of(z_ref), t_ref[...],
                         a_ref[...], dt_ref[...], gnb_ref[...], st_ref[...])
        dq, dk, dv, dz, dtail_sum, da_sum, ddt_sum, dgnb_sum, dst0 = vjp((heads_of(do_ref), dst[hb]))
        dst[hb] = dst0
        for hh, sl in enumerate(sls):
            dq_ref[:, sl] = dq[hh]
            dk_ref[:, sl] = dk[hh]
            dv_ref[:, sl] = dv[hh]
            dz_ref[:, sl] = dz[hh].astype(dz_ref.dtype)

        @pl.when(hb == 0)
        def _():
            dtail_acc[...] = dtail_sum

        @pl.when(hb > 0)
        def _():
            dtail_acc[...] += dtail_sum

        @pl.when(hb == n_hb - 1)
        def _():
            dt_out[...] = dtail_acc[...].astype(dt_out.dtype)

        da_ref[...] += da_sum
        ddt_ref[...] += ddt_sum
        dgnb_ref[...] += dgnb_sum

    rev = lambda n: n_chunks - 1 - n
    out = pl.BlockSpec((CHUNK, _HPS * HEAD), lambda n, h: (rev(n), h))
    row = pl.BlockSpec((1, HEAD), lambda n, h: (0, 0))
    return pl.pallas_call(
        body, name="gdn_bwd", grid=(n_chunks, heads // _HPS),
        in_specs=_gdn_specs(rev)
        + [pl.BlockSpec((None, _HPS, HEAD, HEAD), lambda n, h: (rev(n), h, 0, 0)),
           pl.BlockSpec((CHUNK, _HPS * HEAD), lambda n, h: (rev(n), 8 // _HPS + h))],
        out_specs=[out, out, out, out, pl.BlockSpec((CHUNK, HEAD), lambda n, h: (rev(n), 0)), row, row, row],
        out_shape=[jax.ShapeDtypeStruct((S, 1024), F32)] * 3
        + [jax.ShapeDtypeStruct((S, 1024), BF16), jax.ShapeDtypeStruct((S, HEAD), BF16)]
        + [jax.ShapeDtypeStruct((1, HEAD), F32)] * 3,
        scratch_shapes=[pltpu.VMEM((heads // _HPS, _HPS, HEAD, HEAD), F32), pltpu.VMEM((CHUNK, HEAD), F32)],
        compiler_params=_cparams(("arbitrary", "arbitrary")),
    )(act, act, act, proj, proj, arow, dtrow, gnb, states, d_o)


def _conv_silu(x, w):
    t = _iota((x.shape[0], 1), 0)
    tap = _iota(w.shape, 0)
    y = jnp.zeros_like(x)
    for r in range(4):
        w_r = jnp.sum(jnp.where(tap == 3 - r, w, 0.0), axis=0, keepdims=True)
        y = y + (x if r == 0 else jnp.where(t >= r, _roll_rows(x, r), 0.0)) * w_r
    return jax.nn.silu(y)


_CONV_COLS = 128


def _conv_fwd(proj, conv_w):
    S = proj.shape[0]
    nb = 3072 // _CONV_COLS
    off = 4096 // _CONV_COLS

    def body(x_ref, w_ref, o_ref):
        o_ref[...] = _conv_silu(x_ref[...], w_ref[...])

    return pl.pallas_call(
        body, name="conv_fwd", grid=(nb,),
        in_specs=[pl.BlockSpec((S, _CONV_COLS), lambda j: (0, off + j)), pl.BlockSpec((4, _CONV_COLS), lambda j: (0, j))],
        out_specs=pl.BlockSpec((S, _CONV_COLS), lambda j: (0, j)),
        out_shape=jax.ShapeDtypeStruct((S, 3072), F32),
        compiler_params=_cparams(("parallel",)),
    )(proj, conv_w)


def _conv_bwd(proj, conv_w, dq, dk, dv):
    S = proj.shape[0]
    nb = 3072 // _CONV_COLS
    off = 4096 // _CONV_COLS
    per = 1024 // _CONV_COLS

    def body(x_ref, w_ref, dq_ref, dk_ref, dv_ref, dx_ref, dw_ref):
        j = pl.program_id(0)
        _, vjp = jax.vjp(_conv_silu, x_ref[...], w_ref[...])
        d = jnp.where(j < per, dq_ref[...], jnp.where(j < 2 * per, dk_ref[...], dv_ref[...]))
        dx, dw = vjp(d)
        dx_ref[...] = dx.astype(dx_ref.dtype)
        dw_ref[...] = dw

    dsp = lambda s: pl.BlockSpec((S, _CONV_COLS), lambda j, s=s: (0, jnp.clip(j - s * per, 0, per - 1)))
    return pl.pallas_call(
        body, name="conv_bwd", grid=(nb,),
        in_specs=[pl.BlockSpec((S, _CONV_COLS), lambda j: (0, off + j)), pl.BlockSpec((4, _CONV_COLS), lambda j: (0, j)),
                  dsp(0), dsp(1), dsp(2)],
        out_specs=[pl.BlockSpec((S, _CONV_COLS), lambda j: (0, j)), pl.BlockSpec((4, _CONV_COLS), lambda j: (0, j))],
        out_shape=[jax.ShapeDtypeStruct((S, 3072), BF16), jax.ShapeDtypeStruct((4, 3072), F32)],
        compiler_params=_cparams(("parallel",)),
    )(proj, conv_w, dq, dk, dv)


_S5_T = 512
_S5_L = 512
_S5_NB = 16
_S5_U = 4
_S5_UB = 3


def _s5_tile(xr, xi, pw_r, pw_i, cr, ci, reverse):
    t8 = _iota((8, 1), 0)
    for sh in (1, 2, 4):
        row = (8 - sh) if reverse else (sh - 1)
        ar, ai = pw_r[row:row + 1, :], pw_i[row:row + 1, :]
        if reverse:
            keep, amt = t8 < 8 - sh, 8 - sh
        else:
            keep, amt = t8 >= sh, sh
        sr = jnp.where(keep, pltpu.roll(xr, amt, 0), 0.0)
        si = jnp.where(keep, pltpu.roll(xi, amt, 0), 0.0)
        xr, xi = xr + ar * sr - ai * si, xi + ar * si + ai * sr
    xr, xi = xr + pw_r * cr - pw_i * ci, xi + pw_r * ci + pw_i * cr
    return xr, xi


def _s5_fwd(u, bre, bim, cre, cim, pw_r, pw_i, dskip):
    S = u.shape[0]
    T = min(_S5_T, S)
    nt = S // T

    def body(u_ref, bre_ref, bim_ref, cre_ref, cim_ref, pr_ref, pi_ref, d_ref, y_ref, xr_ref, xi_ref,
             bu_r, bu_i, car_r, car_i):
        @pl.when(pl.program_id(1) == 0)
        def _():
            car_r[...] = jnp.zeros_like(car_r)
            car_i[...] = jnp.zeros_like(car_i)

        uu = u_ref[...]
        bu_r[...] = _dot(uu, bre_ref[...])
        bu_i[...] = _dot(uu, bim_ref[...])
        pw_r, pw_i = pr_ref[...], pi_ref[...]

        def tiles(i, carry):
            ins = []
            for k in range(_S5_U):
                r0 = pl.multiple_of((i * _S5_U + k) * 8, 8)
                ins.append((r0, bu_r[pl.ds(r0, 8), :], bu_i[pl.ds(r0, 8), :]))
            outs = []
            for r0, br, bi in ins:
                xr, xi = _s5_tile(br, bi, pw_r, pw_i, carry[0], carry[1], False)
                carry = (xr[7:8, :], xi[7:8, :])
                outs.append((r0, xr, xi))
            for r0, xr, xi in outs:
                xr_ref[pl.ds(r0, 8), :] = xr
                xi_ref[pl.ds(r0, 8), :] = xi
            return carry

        cr, ci = lax.fori_loop(0, T // (8 * _S5_U), tiles, (car_r[...], car_i[...]))
        car_r[...] = cr
        car_i[...] = ci
        y_ref[...] = _dot(xr_ref[...], cre_ref[...]) - _dot(xi_ref[...], cim_ref[...]) + d_ref[...] * uu

    blk3 = lambda a, b: pl.BlockSpec((None, a, b), lambda j, t: (j, 0, 0))
    return pl.pallas_call(
        body, name="s5_fwd", grid=(_S5_NB, nt),
        in_specs=[pl.BlockSpec((T, HEAD), lambda j, t: (t, j)),
                  blk3(HEAD, _S5_L), blk3(HEAD, _S5_L), blk3(_S5_L, HEAD), blk3(_S5_L, HEAD),
                  blk3(8, _S5_L), blk3(8, _S5_L), pl.BlockSpec((1, HEAD), lambda j, t: (0, j))],
        out_specs=[pl.BlockSpec((T, HEAD), lambda j, t: (t, j)),
                   pl.BlockSpec((T, _S5_L), lambda j, t: (t, j)), pl.BlockSpec((T, _S5_L), lambda j, t: (t, j))],
        out_shape=[jax.ShapeDtypeStruct((S, D_MODEL), F32),
                   jax.ShapeDtypeStruct((S, _S5_NB * _S5_L), F32), jax.ShapeDtypeStruct((S, _S5_NB * _S5_L), F32)],
        scratch_shapes=[pltpu.VMEM((T, _S5_L), F32), pltpu.VMEM((T, _S5_L), F32),
                        pltpu.VMEM((1, _S5_L), F32), pltpu.VMEM((1, _S5_L), F32)],
        compiler_params=_cparams(("parallel", "arbitrary")),
    )(u, bre, bim, cre, cim, pw_r, pw_i, dskip)


def _s5_bwd(dy, u, xre, xim, bre, bim, cre, cim, qw_r, qw_i, dskip):
    S = u.shape[0]
    T = min(_S5_T, S)
    nt = S // T
    nt8 = T // 8

    def body(dy_ref, u_ref, xr_ref, xi_ref, xpr_ref, xpi_ref, bre_ref, bim_ref, cre_ref, cim_ref, qr_ref, qi_ref,
             d_ref, du_ref, dbr_ref, dbi_ref, dcr_ref, dci_ref, dlr_ref, dli_ref, dd_ref,
             g_r, g_i, car_r, car_i):
        t = pl.program_id(1)

        @pl.when(t == 0)
        def _():
            car_r[...] = jnp.zeros_like(car_r)
            car_i[...] = jnp.zeros_like(car_i)
            for r in (dbr_ref, dbi_ref, dcr_ref, dci_ref, dlr_ref, dli_ref, dd_ref):
                r[...] = jnp.zeros_like(r)

        dyy, uu = dy_ref[...], u_ref[...]
        g_r[...] = _dot(dyy, cre_ref[...], _NT)
        g_i[...] = -_dot(dyy, cim_ref[...], _NT)
        qw_r, qw_i = qr_ref[...], qi_ref[...]
        t8 = _iota((8, 1), 0)
        first = t == nt - 1

        def load(r0, prev_r, prev_i):
            rows = pl.ds(r0, 8)
            return r0, g_r[rows, :], g_i[rows, :], xr_ref[rows, :], xi_ref[rows, :], prev_r, prev_i

        def run(loaded, carry, acc):
            done = []
            for r0, dr, di, xr, xi, prev_r, prev_i in loaded:
                gr, gi = _s5_tile(dr, di, qw_r, qw_i, carry[0], carry[1], True)
                carry = (gr[0:1, :], gi[0:1, :])
                xpr = jnp.where(t8 >= 1, pltpu.roll(xr, 1, 0), prev_r)
                xpi = jnp.where(t8 >= 1, pltpu.roll(xi, 1, 0), prev_i)
                acc = (acc[0] + gr * xpr + gi * xpi, acc[1] + gi * xpr - gr * xpi)
                done.append((r0, gr, gi))
            for r0, gr, gi in done:
                g_r[pl.ds(r0, 8), :] = gr
                g_i[pl.ds(r0, 8), :] = gi
            return carry, acc

        def step(ii, state):
            loaded = []
            for k in range(_S5_UB):
                idx = nt8 - 1 - (ii * _S5_UB + k)
                r0 = pl.multiple_of(idx * 8, 8)
                p0 = pl.multiple_of((idx - 1) * 8, 8)
                loaded.append(load(r0, xr_ref[pl.ds(p0, 8), :][7:8, :], xi_ref[pl.ds(p0, 8), :][7:8, :]))
            return run(loaded, *state)

        zero = jnp.zeros((8, _S5_L), F32)
        assert (nt8 - 1) % _S5_UB == 0
        state = lax.fori_loop(0, (nt8 - 1) // _S5_UB, step, ((car_r[...], car_i[...]), (zero, zero)))
        prev_r = jnp.where(first, 0.0, xpr_ref[...][7:8, :])
        prev_i = jnp.where(first, 0.0, xpi_ref[...][7:8, :])
        (cr, ci), (ar, ai) = run([load(0, prev_r, prev_i)], *state)
        car_r[...] = cr
        car_i[...] = ci
        dlr_ref[...] += ar
        dli_ref[...] += ai
        gr, gi = g_r[...], g_i[...]
        du_ref[...] = _dot(gr, bre_ref[...], _NT) + _dot(gi, bim_ref[...], _NT) + d_ref[...] * dyy
        dbr_ref[...] += _dot(uu, gr, _TN)
        dbi_ref[...] += _dot(uu, gi, _TN)
        dcr_ref[...] += _dot(xr_ref[...], dyy, _TN)
        dci_ref[...] -= _dot(xi_ref[...], dyy, _TN)
        dd_ref[...] += jnp.sum(dyy * uu, axis=0, keepdims=True)

    rev = lambda t: nt - 1 - t
    blk3 = lambda a, b: pl.BlockSpec((None, a, b), lambda j, t: (j, 0, 0))
    tl = pl.BlockSpec((T, HEAD), lambda j, t: (rev(t), j))
    xs = pl.BlockSpec((T, _S5_L), lambda j, t: (rev(t), j))
    xp = pl.BlockSpec((8, _S5_L), lambda j, t: (jnp.maximum(rev(t) * nt8 - 1, 0), j))
    return pl.pallas_call(
        body, name="s5_bwd", grid=(_S5_NB, nt),
        in_specs=[tl, tl, xs, xs, xp, xp, blk3(HEAD, _S5_L), blk3(HEAD, _S5_L), blk3(_S5_L, HEAD), blk3(_S5_L, HEAD),
                  blk3(8, _S5_L), blk3(8, _S5_L), pl.BlockSpec((1, HEAD), lambda j, t: (0, j))],
        out_specs=[tl, blk3(HEAD, _S5_L), blk3(HEAD, _S5_L), blk3(_S5_L, HEAD), blk3(_S5_L, HEAD),
                   blk3(8, _S5_L), blk3(8, _S5_L), pl.BlockSpec((1, HEAD), lambda j, t: (0, j))],
        out_shape=[jax.ShapeDtypeStruct((S, D_MODEL), F32),
                   jax.ShapeDtypeStruct((_S5_NB, HEAD, _S5_L), F32), jax.ShapeDtypeStruct((_S5_NB, HEAD, _S5_L), F32),
                   jax.ShapeDtypeStruct((_S5_NB, _S5_L, HEAD), F32), jax.ShapeDtypeStruct((_S5_NB, _S5_L, HEAD), F32),
                   jax.ShapeDtypeStruct((_S5_NB, 8, _S5_L), F32), jax.ShapeDtypeStruct((_S5_NB, 8, _S5_L), F32),
                   jax.ShapeDtypeStruct((1, D_MODEL), F32)],
        scratch_shapes=[pltpu.VMEM((T, _S5_L), F32), pltpu.VMEM((T, _S5_L), F32),
                        pltpu.VMEM((1, _S5_L), F32), pltpu.VMEM((1, _S5_L), F32)],
        compiler_params=_cparams(("parallel", "arbitrary")),
    )(dy, u, xre, xim, xre, xim, bre, bim, cre, cim, qw_r, qw_i, dskip)


def _s5_params(a_re, a_im, log_dt, b_re, b_im, c_re, c_im):
    step = jnp.exp(log_dt)[:, None]
    mag = jnp.exp(a_re * step)
    lr, li = mag * jnp.cos(a_im * step), mag * jnp.sin(a_im * step)
    den = a_re * a_re + a_im * a_im
    nr, ni = lr - 1.0, li
    kr, ki = (nr * a_re + ni * a_im) / den, (ni * a_re - nr * a_im) / den
    bbr = kr[..., None] * b_re - ki[..., None] * b_im
    bbi = kr[..., None] * b_im + ki[..., None] * b_re
    eye = jnp.eye(8, dtype=F32)

    def blk_b(bb):
        t = bb.reshape(_S5_NB, 8, 64, 16).transpose(0, 1, 3, 2)
        return (t[:, :, :, None, :] * eye[None, :, None, :, None]).reshape(_S5_NB, HEAD, _S5_L)

    def blk_c(cc):
        t = cc.reshape(_S5_NB, 8, 16, 64).transpose(0, 1, 3, 2)
        return (t[:, :, :, None, :] * eye[None, :, None, :, None]).reshape(_S5_NB, _S5_L, HEAD)

    return (blk_b(bbr), blk_b(bbi), blk_c(c_re), blk_c(c_im),
            lr.reshape(_S5_NB, _S5_L), li.reshape(_S5_NB, _S5_L))


def _s5_powers(lr, li):
    pr, pi = [lr], [li]
    for _ in range(7):
        pr, pi = pr + [pr[-1] * lr - pi[-1] * li], pi + [pr[-1] * li + pi[-1] * lr]
    return jnp.stack(pr, axis=1), jnp.stack(pi, axis=1)


_MESH = pl.DeviceIdType.MESH
_ANY = pl.BlockSpec(memory_space=pl.ANY)


def _place():
    x, y, c = lax.axis_index("x"), lax.axis_index("y"), lax.axis_index("c")
    return x, y, c, [(1 - x, y), (x, 1 - y), (1 - x, 1 - y)]


def _comm_call(body, arrs, out_shapes, n_remote, name):
    n = len(arrs)
    return pl.pallas_call(
        body, name=name,
        in_specs=[_ANY] * n, out_specs=[_ANY] * n, out_shape=out_shapes,
        scratch_shapes=[pltpu.SemaphoreType.DMA((n * n_remote,)), pltpu.SemaphoreType.DMA((n * n_remote,)),
                        pltpu.SemaphoreType.DMA((n,))],
    )(*arrs)


def _own_slot(shard, me_idx, name, rows=256):
    _, half, C = shard.shape
    rows = min(rows, half)
    assert half % rows == 0

    def body(me_ref, in_ref, o_ref):
        o_ref[...] = in_ref[...]

    return pl.pallas_call(
        body, name=name,
        grid_spec=pltpu.PrefetchScalarGridSpec(
            num_scalar_prefetch=1, grid=(2, half // rows),
            in_specs=[pl.BlockSpec((None, rows, C), lambda h, i, me: (h, i, 0))],
            out_specs=pl.BlockSpec((None, None, rows, C), lambda h, i, me: (me[0], h, i, 0))),
        out_shape=jax.ShapeDtypeStruct((4,) + shard.shape, shard.dtype),
        compiler_params=_cparams(("parallel", "parallel")),
    )(me_idx, shard)


def _pair_split(arrs, name):
    n = len(arrs)

    def body(*refs):
        ins, outs, (ssem, rsem, _) = refs[:n], refs[n:2 * n], refs[2 * n:]
        x, y, c, _ = _place()
        copies = []
        for a in range(n):
            for s in range(4):
                cp = pltpu.make_async_remote_copy(src_ref=ins[a].at[s, 1 - c], dst_ref=outs[a].at[s],
                                                  send_sem=ssem.at[4 * a + s], recv_sem=rsem.at[4 * a + s],
                                                  device_id=(x, y, 1 - c), device_id_type=_MESH)
                cp.start()
                copies.append(cp)
        for cp in copies:
            cp.wait()

    shapes = [jax.ShapeDtypeStruct((4,) + a.shape[2:], a.dtype) for a in arrs]
    return _comm_call(body, arrs, shapes, 4, name)


def _pair_swap(arrs, name):
    n = len(arrs)

    def body(*refs):
        ins, outs, (ssem, rsem, _) = refs[:n], refs[n:2 * n], refs[2 * n:]
        x, y, c, _ = _place()
        copies = []
        for a in range(n):
            cp = pltpu.make_async_remote_copy(src_ref=ins[a], dst_ref=outs[a], send_sem=ssem.at[a], recv_sem=rsem.at[a],
                                              device_id=(x, y, 1 - c), device_id_type=_MESH)
            cp.start()
            copies.append(cp)
        for cp in copies:
            cp.wait()

    return _comm_call(body, arrs, [jax.ShapeDtypeStruct(a.shape, a.dtype) for a in arrs], 1, name)


def _pair_sum(full, recv, c_idx, name, rows=128):
    _, _, half, C = full.shape
    rows = min(rows, half)
    nb = half // rows
    assert half % rows == 0

    def body(c_ref, a_ref, b_ref, o_ref):
        o_ref[...] = (a_ref[...].astype(F32) + b_ref[...].astype(F32)).astype(o_ref.dtype)

    return pl.pallas_call(
        body, name=name,
        grid_spec=pltpu.PrefetchScalarGridSpec(
            num_scalar_prefetch=1, grid=(4, nb),
            in_specs=[pl.BlockSpec((None, None, rows, C), lambda s, i, c: (s, c[0], i, 0)),
                      pl.BlockSpec((None, rows, C), lambda s, i, c: (s, i, 0))],
            out_specs=pl.BlockSpec((None, rows, C), lambda s, i, c: (s, i, 0))),
        out_shape=jax.ShapeDtypeStruct(recv.shape, recv.dtype),
        compiler_params=_cparams(("parallel", "parallel")),
    )(c_idx, full, recv)


_HBM = pl.BlockSpec(memory_space=pltpu.HBM)
_SEM = pl.BlockSpec(memory_space=pltpu.SEMAPHORE)
_SPLIT = dict(has_side_effects=pltpu.SideEffectType.DATAFLOW_SIDE_EFFECTING)


def _hbm(t):
    return pltpu.with_memory_space_constraint(t, pltpu.HBM)


def _ag_copy(buf_ref, ssem, rsem, j, chip, c, me):
    px, py = chip
    return pltpu.make_async_remote_copy(src_ref=buf_ref.at[me, c], dst_ref=buf_ref.at[me, c], send_sem=ssem.at[j],
                                        recv_sem=rsem.at[j], device_id=(px, py, c), device_id_type=_MESH)


def _ag_start(bufs, after, name):
    n = len(bufs)

    def body(*refs):
        buf_refs, (ssem, rsem), token = refs[:n], refs[n + 1:n + 3], refs[-1]
        x, y, c, chips = _place()
        for a in range(n):
            for j, chip in enumerate(chips):
                _ag_copy(buf_refs[a], ssem, rsem, 3 * a + j, chip, c, 2 * x + y).start()
        token[...] = jnp.zeros_like(token)

    out = pl.pallas_call(
        body, name=name,
        out_shape=(pltpu.SemaphoreType.DMA((3 * n,)), pltpu.SemaphoreType.DMA((3 * n,)),
                   *[pltpu.HBM(b.shape, b.dtype) for b in bufs], jax.ShapeDtypeStruct((8, HEAD), F32)),
        in_specs=(*[_HBM] * n, _ANY), out_specs=(_SEM, _SEM, *[_HBM] * n, pl.BlockSpec(memory_space=pltpu.VMEM)),
        input_output_aliases={a: 2 + a for a in range(n)}, compiler_params=pltpu.CompilerParams(**_SPLIT),
    )(*[_hbm(b) for b in bufs], after)
    return out[0], out[1], list(out[2:2 + n]), out[-1]


def _ag_wait(ssem, rsem, bufs, after, name):
    n = len(bufs)

    def body(*refs):
        buf_refs, ssem, rsem = refs[:n], refs[n], refs[n + 1]
        x, y, c, chips = _place()
        for a in range(n):
            for j, chip in enumerate(chips):
                cp = _ag_copy(buf_refs[a], ssem, rsem, 3 * a + j, chip, c, 2 * x + y)
                cp.wait_send()
                cp.wait_recv()

    return list(pl.pallas_call(
        body, name=name, out_shape=tuple(pltpu.HBM(b.shape, b.dtype) for b in bufs),
        in_specs=(*[_HBM] * n, _SEM, _SEM, _ANY), out_specs=tuple([_HBM] * n),
        input_output_aliases={a: a for a in range(n)}, compiler_params=pltpu.CompilerParams(**_SPLIT),
    )(*bufs, ssem, rsem, after))


def _pair_forward(bufs, name):
    n = len(bufs)

    def body(*refs):
        outs, (ssem, rsem) = refs[n:2 * n], refs[2 * n:]
        x, y, c, chips = _place()
        copies = []
        for a in range(n):
            for j, (px, py) in enumerate(chips):
                landed = outs[a].at[2 * px + py, c]
                cp = pltpu.make_async_remote_copy(src_ref=landed, dst_ref=landed, send_sem=ssem.at[3 * a + j],
                                                  recv_sem=rsem.at[3 * a + j], device_id=(x, y, 1 - c),
                                                  device_id_type=_MESH)
                cp.start()
                copies.append(cp)
        for cp in copies:
            cp.wait()

    return pl.pallas_call(
        body, name=name, in_specs=[_ANY] * n, out_specs=[_ANY] * n,
        out_shape=[jax.ShapeDtypeStruct(b.shape, b.dtype) for b in bufs],
        input_output_aliases={a: a for a in range(n)},
        scratch_shapes=[pltpu.SemaphoreType.DMA((3 * n,)), pltpu.SemaphoreType.DMA((3 * n,))],
    )(*bufs)


def _rs_copy(src_ref, land_ref, ssem, rsem, j, chip, c):
    px, py = chip
    return pltpu.make_async_remote_copy(src_ref=src_ref.at[2 * px + py], dst_ref=land_ref.at[j], send_sem=ssem.at[j],
                                        recv_sem=rsem.at[j], device_id=(px, py, c), device_id_type=_MESH)


def _rs_start(part, after, name):
    def body(part_ref, land_ref, after_ref, ssem, rsem, part_thru, land_thru, token):
        x, y, c, chips = _place()
        for j, chip in enumerate(chips):
            _rs_copy(part_ref, land_ref, ssem, rsem, j, chip, c).start()
        token[...] = jnp.zeros_like(token)

    land = jax.ShapeDtypeStruct((3,) + part.shape[1:], part.dtype)
    return pl.pallas_call(
        body, name=name,
        out_shape=(pltpu.SemaphoreType.DMA((3,)), pltpu.SemaphoreType.DMA((3,)), pltpu.HBM(part.shape, part.dtype),
                   pltpu.HBM(land.shape, land.dtype), jax.ShapeDtypeStruct((8, HEAD), F32)),
        in_specs=(_HBM, _HBM, _ANY), out_specs=(_SEM, _SEM, _HBM, _HBM, pl.BlockSpec(memory_space=pltpu.VMEM)),
        input_output_aliases={0: 2, 1: 3}, compiler_params=pltpu.CompilerParams(**_SPLIT),
    )(_hbm(part), _hbm(lax.empty(land.shape, land.dtype)), after)


def _rs_wait(ssem, rsem, part, land, after, name):
    def body(part_ref, land_ref, ssem, rsem, after_ref, part_out, land_out):
        x, y, c, chips = _place()
        for j, chip in enumerate(chips):
            cp = _rs_copy(part_ref, land_ref, ssem, rsem, j, chip, c)
            cp.wait_send()
            cp.wait_recv()

    return pl.pallas_call(
        body, name=name, out_shape=(pltpu.HBM(part.shape, part.dtype), pltpu.HBM(land.shape, land.dtype)),
        in_specs=(_HBM, _HBM, _SEM, _SEM, _ANY), out_specs=(_HBM, _HBM),
        input_output_aliases={0: 0, 1: 1}, compiler_params=pltpu.CompilerParams(**_SPLIT),
    )(part, land, ssem, rsem, after)


def _sum_own_recv(part, land, me_idx, name, rows=128):
    _, half, C = part.shape
    rows = min(rows, half)
    assert half % rows == 0

    def body(me_ref, own_ref, land_ref, o_ref):
        acc = own_ref[...].astype(F32)
        for s in range(3):
            acc = acc + land_ref[s].astype(F32)
        o_ref[...] = acc

    return pl.pallas_call(
        body, name=name,
        grid_spec=pltpu.PrefetchScalarGridSpec(
            num_scalar_prefetch=1, grid=(half // rows,),
            in_specs=[pl.BlockSpec((None, rows, C), lambda i, me: (me[0], i, 0)),
                      pl.BlockSpec((3, rows, C), lambda i, me: (0, i, 0))],
            out_specs=pl.BlockSpec((rows, C), lambda i, me: (i, 0))),
        out_shape=jax.ShapeDtypeStruct((half, C), F32),
        compiler_params=_cparams(("parallel",)),
    )(me_idx, part, land)


def _sum_slots(arr, name, rows=128):
    k, R, C = arr.shape
    rows = min(rows, R)
    assert R % rows == 0

    def body(in_ref, o_ref):
        acc = in_ref[0].astype(F32)
        for s in range(1, k):
            acc = acc + in_ref[s].astype(F32)
        o_ref[...] = acc

    return pl.pallas_call(
        body, name=name, grid=(R // rows,),
        in_specs=[pl.BlockSpec((k, rows, C), lambda i: (0, i, 0))],
        out_specs=pl.BlockSpec((rows, C), lambda i: (i, 0)),
        out_shape=jax.ShapeDtypeStruct((R, C), F32),
        compiler_params=_cparams(("parallel",)),
    )(arr)


ADAM_LR, ADAM_B1, ADAM_B2, ADAM_EPS, ADAM_WD, ADAM_STEP = 0.001, 0.9, 0.999, 1e-08, 0.01, 10


def _adam_math(w, g, m, v):
    m = ADAM_B1 * m + (1.0 - ADAM_B1) * g
    v = ADAM_B2 * v + (1.0 - ADAM_B2) * jnp.square(g)
    m_hat = m / (1.0 - ADAM_B1 ** ADAM_STEP)
    v_hat = v / (1.0 - ADAM_B2 ** ADAM_STEP)
    delta = -ADAM_LR * (m_hat / (jnp.sqrt(v_hat) + ADAM_EPS) + ADAM_WD * w)
    return delta, m, v


def _adamw(w, m, v, layer, mine, other, c_idx, name, g_off=0, prev=None, after=(), rows=128):
    _, R, C = w.shape
    half = mine.shape[0]
    rows = min(rows, R)
    assert R % rows == 0 and g_off % rows == 0 and half % rows == 0
    nbh, b0 = half // rows, g_off // rows

    def body(c_ref, w_ref, m_ref, v_ref, mine_ref, other_ref, *rest):
        go, do, mo, vo = rest[-4:]
        in_my_half = (b0 + pl.program_id(0)) // nbh == c_ref[0]
        g = jnp.where(in_my_half, mine_ref[...], other_ref[...])
        delta, m1, v1 = _adam_math(w_ref[...], g, m_ref[...], v_ref[...])
        go[...] = g
        do[...] = delta
        mo[...] = m1
        vo[...] = v1

    blk = pl.BlockSpec((None, rows, C), lambda i, c: (layer, i, 0))
    gblk = pl.BlockSpec((rows, C), lambda i, c: ((b0 + i) % nbh, 0))
    carried = list(prev) if prev is not None else []
    return pl.pallas_call(
        body, name=name,
        grid_spec=pltpu.PrefetchScalarGridSpec(
            num_scalar_prefetch=1, grid=(R // rows,),
            in_specs=[blk] * 3 + [gblk] * 2 + [_ANY] * (len(carried) + len(after)), out_specs=[blk] * 4),
        out_shape=[jax.ShapeDtypeStruct(w.shape, F32)] * 4,
        input_output_aliases={6 + k: k for k in range(len(carried))},
        compiler_params=_cparams(("parallel",)),
    )(c_idx, w, m, v, mine, other, *carried, *after)


_WEIGHTS = ['norm_mix', 'norm_mlp', 'norm_ple', 'w_in_e', 'w_out_e', 'hgrn_lb', 'g_norm_a', 'conv_w', 'a_log',
            'dt_bias', 'g_norm_b', 's5_a_re', 's5_a_im', 's5_b_re', 's5_b_im', 's5_c_re', 's5_c_im', 's5_d',
            's5_log_dt', 'w_glu', 'b_glu', 'w_out_o', 'w_up', 'w_down', 'w_ple_gate', 'w_ple_proj', 'final_norm']
_INPUTS = ['x', 'p'] + _WEIGHTS + ['loss_target'] + ['m_' + n for n in _WEIGHTS] + ['v_' + n for n in _WEIGHTS]
_FAMILY = [('w_up', 0, 'col', 2048), ('w_down', 0, 'row', 2048), ('w_ple_gate', 0, 'row', 512), ('w_out_e', 0, 'row', 512),
           ('w_up', 1, 'col', 2048), ('w_down', 1, 'row', 2048), ('w_ple_gate', 1, 'row', 512), ('w_glu', 0, 'row', 512),
           ('w_out_o', 0, 'row', 512)]
_PACK = {('w_up', 0): 0, ('w_down', 0): 0, ('w_ple_gate', 0): 0, ('w_out_e', 0): 0,
         ('w_up', 1): 1, ('w_down', 1): 1, ('w_ple_gate', 1): 1, ('w_glu', 0): 1, ('w_out_o', 0): 1}
_IN_PAD = 8320
_IN_BLK = 1664


def _rms_bwd_fn(x, d_hn, d_res, g):
    _, vjp = jax.vjp(_rms, x, g)
    dx, dg = vjp(d_hn.astype(F32))
    return dx + d_res, dg


def _rms_bwd_both(x, d_hn, d_res, g):
    dx, dg = _rms_bwd_fn(x, d_hn, d_res, g)
    return dx, dx, dg


def _add_res(acc, h):
    return (acc + h,)


def _pack_rows(parts, lanes=128, mult=256):
    flat = jnp.concatenate([q.reshape(-1).astype(F32) for q in parts])
    n = flat.shape[0]
    rows = -(-n // (lanes * mult)) * mult
    return jnp.pad(flat, (0, rows * lanes - n)).reshape(rows, lanes)


def _unpack_rows(buf, shapes):
    flat, out, off = buf.reshape(-1), [], 0
    for s in shapes:
        n = math.prod(s)
        out.append(flat[off:off + n].reshape(s))
        off += n
    return out


def _step(a):
    S = a['x'].shape[1]
    x, tgt = a['x'][0], a['loss_target'][0]
    xi, yi = lax.axis_index("x"), lax.axis_index("y")
    me = 2 * xi + yi
    row = lambda t: t.reshape(1, -1)

    views, pack_of, packs, rt = {}, {}, [[], []], [0, 0]
    for nm, l, kind, r in _FAMILY:
        views[nm, l], pack_of[nm, l] = (kind, rt[_PACK[nm, l]], r), _PACK[nm, l]
        packs[_PACK[nm, l]].append(a[nm][l].astype(BF16))
        rt[_PACK[nm, l]] += r
    small_sh = jnp.concatenate([a['conv_w'][0].reshape(-1), a['s5_d'][0], a['b_glu'][0]]).reshape(16, 256)
    shards = [jnp.concatenate(packs[0], axis=0), jnp.concatenate(packs[1], axis=0), a['w_in_e'][0].astype(BF16),
              a['w_ple_proj'].reshape(512, 512).astype(BF16), small_sh]
    me_idx = me.astype(jnp.int32).reshape(1)
    c_idx = lax.axis_index("c").astype(jnp.int32).reshape(1)
    own = lambda i: _own_slot(shards[i].reshape(2, shards[i].shape[0] // 2, shards[i].shape[1]), me_idx,
                              name=f"own_slot{i}")
    whole = lambda g, t: g.reshape((4,) + t.shape)
    first = _ag_start([own(2), own(3), own(4)], me_idx, name="gather_first_start")
    slots = [own(0), own(1)]
    landed = _pair_forward(_ag_wait(first[0], first[1], first[2], slots[1], name="gather_first_wait"),
                           name="gather_first_pass")
    g_in, g_pp, g_small = [whole(g, t) for g, t in zip(landed, shards[2:])]
    gw = {}
    W = lambda nm, l: dict(b=gw[pack_of[nm, l]], b_view=views[nm, l])
    sems0 = _ag_start([slots[0]], g_in, name="gather0_start")
    win = jnp.pad(g_in.transpose(1, 0, 2).reshape(D_MODEL, 8208), ((0, 0), (0, _IN_PAD - 8208)))
    w_pp = g_pp.reshape(4, 2, 256, 512).transpose(1, 2, 0, 3).reshape(2, 256, D_MODEL)
    g_small = g_small.reshape(4, 4096)
    conv_full = g_small[:, :3072].reshape(4, 4, 768).transpose(1, 0, 2).reshape(4, 3072)
    s5_d_full = g_small[:, 3072:3584].reshape(1, D_MODEL)
    b_glu_full = g_small[:, 3584:].reshape(1, D_MODEL)

    lb, lb_vjp = jax.vjp(lambda hl: jax.nn.softmax(hl, axis=0)[0:1], a['hgrn_lb'])
    pad_h = lambda t: jnp.pad(t, ((0, 0), (0, HEAD - t.shape[1])))
    (arow, dtrow), gdn_vjp = jax.vjp(lambda al, dt: (pad_h(-jnp.exp(al)), pad_h(dt)), a['a_log'], a['dt_bias'])
    s5p, s5_vjp = jax.vjp(_s5_params, a['s5_a_re'][0], a['s5_a_im'][0], a['s5_log_dt'][0], a['s5_b_re'][0],
                          a['s5_b_im'][0], a['s5_c_re'][0], a['s5_c_im'][0])
    bre, bim, cre, cim = [t.astype(BF16) for t in s5p[:4]]
    pw_r, pw_i = _s5_powers(s5p[4], s5p[5])
    gna, gnb = a['g_norm_a'], a['g_norm_b']

    def norm_cast(h, g, dt, name):
        return _rowwise(_rms, [h], [row(g)], [(D_MODEL, dt)], name=name)

    def mlp_ple_fwd(h, l):
        hn = norm_cast(h, a['norm_mlp'][l], BF16, f"rms_mlp{l}")
        up, act = _mm(hn, **W('w_up', l), out_dtypes=(BF16, BF16), name=f"up{l}",
                      epilogue=lambda acc: (acc, jnp.square(jnp.maximum(acc, 0.0))))
        h2 = _mm(act, **W('w_down', l), epilogue=_add_res, extras=(h,), name=f"down{l}")
        hnp = norm_cast(h2, a['norm_ple'][l], BF16, f"rms_ple{l}")
        pp = _mm(a['p'][l, 0], w_pp[l], name=f"ple_proj{l}")
        h3, gpre = _mm(hnp, **W('w_ple_gate', l), extras=(h2, pp), out_dtypes=(F32, F32), name=f"ple_gate{l}",
                       bm=512, epilogue=lambda acc, h2, pp: (h2 + jax.nn.sigmoid(acc) * pp, acc))
        return h3, (h, hn, up, act, h2, hnp, pp, gpre)

    hn0 = norm_cast(x, a['norm_mix'][0], BF16, "rms_mix0")
    proj = _mm(hn0, win, bn=_IN_BLK, name="in_proj", after=(sems0[3],))
    o_a, st_a = _hgrn_fwd(proj, lb, gna)
    act_b = _conv_fwd(proj, conv_full)
    o_b, st_b = _gdn_fwd(act_b, proj, arow, dtrow, gnb)
    merged = jnp.concatenate([o_a, o_b], axis=1)
    landed0 = _ag_wait(sems0[0], sems0[1], sems0[2], merged, name="gather0_wait")
    gw[0] = whole(_pair_forward(landed0, name="gather0_pass")[0], shards[0])
    sems1 = _ag_start([slots[1]], gw[0], name="gather1_start")
    h1 = _mm(merged, **W('w_out_e', 0), epilogue=_add_res, extras=(x,), name="out_e", after=(sems1[3],))
    h3, saved0 = mlp_ple_fwd(h1, 0)

    u = norm_cast(h3, a['norm_mix'][1], F32, "rms_mix1")
    y, xre, xim = _s5_fwd(u, bre, bim, cre, cim, pw_r, pw_i, s5_d_full)
    actg = _rowwise(jax.nn.gelu, [y], [], [(D_MODEL, BF16)], name="gelu")
    landed1 = _ag_wait(sems1[0], sems1[1], sems1[2], actg, name="gather1_wait")
    gw[1] = whole(_pair_forward(landed1, name="gather1_pass")[0], shards[1])
    glu, z = _mm(actg, **W('w_glu', 0), extras=(actg, b_glu_full), out_dtypes=(BF16, F32), name="glu",
                 epilogue=lambda acc, act, b: (act.astype(F32) * jax.nn.sigmoid(acc + b), acc + b))
    h4 = _mm(glu, **W('w_out_o', 0), epilogue=_add_res, extras=(h3,), name="out_o")
    h6, saved1 = mlp_ple_fwd(h4, 1)

    def head(h, t, g):
        def f(h, g):
            e = _rms(h, g) - t
            return 0.5 * jnp.sum(jnp.mean(e * e, axis=-1))
        val, vjp = jax.vjp(f, h, g)
        dh, dg = vjp(jnp.ones((), F32))
        return dh, dg, jnp.full((1, HEAD), val)

    dh, d_final, loss_part = _rowwise(head, [h6, tgt], [row(a['final_norm'])], [(D_MODEL, F32)], sums=(D_MODEL, HEAD),
                                      name="loss_head")
    loss = lax.psum(loss_part[0, 0], ("x", "y", "c"))

    gb = {k: lax.empty((4, rt[k], PACK_COLS), BF16) for k in (0, 1)}

    def into(lhs, rhs, key, name, after=()):
        k = pack_of[key]
        gb[k] = _mm(lhs, rhs, dims="tn", out_into=gb[k], out_view=views[key], out_dtypes=(BF16,), name=name, after=after)

    def mlp_ple_bwd(dh, l, saved, after=()):
        h, hn, up, act, h2, hnp, pp, gpre = saved

        def gate_bwd(d, gpre, pp):
            s = jax.nn.sigmoid(gpre)
            return d * s, d * pp * s * (1.0 - s)

        d_pp, d_gp = _rowwise(gate_bwd, [dh, gpre, pp], [], [(D_MODEL, BF16), (D_MODEL, BF16)], name=f"ple_bwd{l}")
        d_wpp = _mm(a['p'][l, 0], d_pp, dims="tn", name=f"d_ple_proj{l}", after=after)
        into(hnp, d_gp, ('w_ple_gate', l), f"d_ple_gate{l}")
        d_hnp = _mm(d_gp, **W('w_ple_gate', l), dims="nt", name=f"ple_gate_t{l}")
        dh2, dh2_b, d_nple = _rowwise(_rms_bwd_both, [h2, d_hnp, dh], [row(a['norm_ple'][l])],
                                      [(D_MODEL, F32), (D_MODEL, BF16)], sums=(D_MODEL,), name=f"rms_ple_bwd{l}")
        d_up = _mm(dh2_b, **W('w_down', l), dims="nt", extras=(up,), out_dtypes=(BF16,), name=f"down_t{l}",
                   epilogue=lambda acc, up: (acc * 2.0 * jnp.maximum(up.astype(F32), 0.0),))
        into(act, dh2_b, ('w_down', l), f"d_down{l}")
        into(hn, d_up, ('w_up', l), f"d_up{l}")
        d_hn = _mm(d_up, **W('w_up', l), dims="nt", name=f"up_t{l}")
        dh1, dh1_b, d_nmlp = _rowwise(_rms_bwd_both, [h, d_hn, dh2], [row(a['norm_mlp'][l])],
                                      [(D_MODEL, F32), (D_MODEL, BF16)], sums=(D_MODEL,), name=f"rms_mlp_bwd{l}")
        return dh1, dh1_b, d_wpp, d_nple, d_nmlp

    dh4, dh4_b, d_wpp1, d_nple1, d_nmlp1 = mlp_ple_bwd(dh, 1, saved1)

    d_glu = _mm(dh4_b, **W('w_out_o', 0), dims="nt", name="out_o_t")
    into(glu, dh4_b, ('w_out_o', 0), "d_out_o")

    def glu_bwd(d, z, act):
        s, act = jax.nn.sigmoid(z), act.astype(F32)
        dz = d * act * s * (1.0 - s)
        return dz, d * s, jnp.sum(dz, axis=0, keepdims=True)

    d_z, d_actp, d_bglu = _rowwise(glu_bwd, [d_glu, z, actg], [], [(D_MODEL, BF16), (D_MODEL, F32)], sums=(D_MODEL,),
                                   name="glu_bwd")
    into(actg, d_z, ('w_glu', 0), "d_glu")

    def reduce_start(bufs, tag):
        parts = [t.reshape(4, 2, t.shape[1] // 2, t.shape[2]) for t in bufs]
        recv = _pair_split(parts, name=f"pair_split{tag}")
        sums = [_pair_sum(f, r, c_idx, name=f"pair_sum{tag}_{i}") for i, (f, r) in enumerate(zip(parts, recv))]
        started, prev = [], c_idx
        for i, s in enumerate(sums):
            started.append(_rs_start(s, prev, name=f"scatter{tag}_{i}_start"))
            prev = started[-1][4]
        return started

    def reduce_finish(started, after, tag):
        halves = []
        for i, (ssem, rsem, part, land, _) in enumerate(started):
            part, land = _rs_wait(ssem, rsem, part, land, after, name=f"scatter{tag}_{i}_wait")
            halves.append(_sum_own_recv(part, land, me_idx, name=f"sum_partials{tag}_{i}"))
        return list(zip(halves, _pair_swap(halves, name=f"swap_halves{tag}")))

    started1 = reduce_start([gb[1]], 1)

    def gelu_bwd(acc, dap, y):
        _, vjp = jax.vjp(jax.nn.gelu, y)
        return vjp(acc + dap)

    dy = _mm(d_z, **W('w_glu', 0), dims="nt", extras=(d_actp, y), epilogue=gelu_bwd, name="glu_t",
             after=(started1[0][4],))
    du, d_bre, d_bim, d_cre, d_cim, d_lr, d_li, d_s5d = _s5_bwd(dy, u, xre, xim, bre, bim, cre, cim,
                                                                 pw_r[:, ::-1], -pw_i[:, ::-1], s5_d_full)
    dh3, d_nmix1 = _rowwise(_rms_bwd_fn, [h3, du, dh4], [row(a['norm_mix'][1])], [(D_MODEL, F32)], sums=(D_MODEL,),
                            name="rms_mix1_bwd")
    d_are, d_aim, d_logdt, d_sbre, d_sbim, d_scre, d_scim = s5_vjp(
        (d_bre, d_bim, d_cre, d_cim, d_lr.sum(axis=1), d_li.sum(axis=1)))

    dh1, dh1_b, d_wpp0, d_nple0, d_nmlp0 = mlp_ple_bwd(dh3, 0, saved0)

    d_merged = _mm(dh1_b, **W('w_out_e', 0), dims="nt", name="out_e_t")
    into(merged, dh1_b, ('w_out_e', 0), "d_out_e")
    started0 = reduce_start([gb[0]], 0)
    dq, df, dv, dg, d_lb, d_gna = _hgrn_bwd(proj, lb, gna, st_a, d_merged, after=(started0[0][4],))
    dqb, dkb, dvb, dzb, d_tail, d_arow, d_dtrow, d_gnb = _gdn_bwd(act_b, proj, arow, dtrow, gnb, st_b, d_merged)
    d_pre, d_conv = _conv_bwd(proj, conv_full, dqb, dkb, dvb)
    d_proj = jnp.concatenate([dq, df, dv, dg, d_pre, dzb, d_tail], axis=1)
    d_win = _mm(hn0, d_proj, dims="tn", bn=_IN_BLK, out_dtypes=(BF16,), name="d_in_proj")
    d_hn0 = _mm(d_proj, win, dims="nt", bk=_IN_BLK, name="in_proj_t")
    grad_x, d_nmix0 = _rowwise(_rms_bwd_fn, [x, d_hn0, dh1], [row(a['norm_mix'][0])], [(D_MODEL, F32)], sums=(D_MODEL,),
                               name="rms_mix0_bwd")
    (d_hlb,) = lb_vjp(d_lb.reshape(1, 1024))
    d_alog, d_dtb = gdn_vjp((d_arow, d_dtrow))

    d_win_sh = d_win[:, :8208].reshape(D_MODEL, 4, 2052).transpose(1, 0, 2)
    d_wpp_sh = jnp.stack([d_wpp0, d_wpp1]).astype(BF16).reshape(2, 256, 4, 512).transpose(2, 0, 1, 3).reshape(4, 512, 512)
    started2 = reduce_start([d_win_sh, d_wpp_sh], 2)
    small = {
        'norm_mix': jnp.concatenate([d_nmix0, d_nmix1]), 'norm_mlp': jnp.concatenate([d_nmlp0, d_nmlp1]),
        'norm_ple': jnp.concatenate([d_nple0, d_nple1]), 'hgrn_lb': d_hlb, 'g_norm_a': d_gna, 'conv_w': d_conv,
        'a_log': d_alog, 'dt_bias': d_dtb, 'g_norm_b': d_gnb, 's5_a_re': d_are, 's5_a_im': d_aim, 's5_b_re': d_sbre,
        's5_b_im': d_sbim, 's5_c_re': d_scre, 's5_c_im': d_scim, 's5_d': d_s5d, 's5_log_dt': d_logdt,
        'b_glu': d_bglu, 'final_norm': d_final}
    s_names = list(small)
    s_shapes = [tuple(small[n].shape) for n in s_names]
    mine_small = _pack_rows([small[n] for n in s_names])
    chip_small = _rowwise(lambda p, o: p + o, [mine_small, _pair_swap([mine_small], name="swap_small")[0]], [],
                          [(128, F32)], name="add_small")
    n_small = chip_small.shape[0]
    small_go = _ag_start([_own_slot(chip_small.reshape(2, n_small // 2, 128), me_idx, name="own_slot_small")],
                         started2[-1][4], name="gather_small_start")
    red = {1: reduce_finish(started1, small_go[3], 1)[0]}
    red[0] = reduce_finish(started0, small_go[3], 0)[0]
    res, order = {}, []

    def adam_layer(nm, layer, key, g_off=None):
        g_off = views[nm, layer][1] if g_off is None else g_off
        prev = res.get(nm)
        follows = () if not order or (prev is not None and order[-1] is prev[0]) else (order[-1],)
        res[nm] = tuple(_adamw(a[nm], a['m_' + nm], a['v_' + nm], layer, *red[key], c_idx, name=f"adam_{nm}{layer}",
                               g_off=g_off, prev=prev, after=follows))
        order.append(res[nm][0])

    for nm, layer in (('w_glu', 0), ('w_out_o', 0), ('w_up', 1), ('w_down', 1), ('w_ple_gate', 1)):
        adam_layer(nm, layer, 1)
    for nm in ('w_up', 'w_down', 'w_ple_gate', 'w_out_e'):
        adam_layer(nm, 0, 0)

    small_all = _pair_forward(_ag_wait(small_go[0], small_go[1], small_go[2], order[-1], name="gather_small_wait"),
                              name="gather_small_pass")[0]
    reduced = _sum_slots(small_all.reshape(4, n_small, 128), name="sum_small")
    sg = dict(zip(s_names, _unpack_rows(reduced, s_shapes)))
    sg['conv_w'] = lax.dynamic_slice_in_dim(sg['conv_w'], me * 768, 768, axis=1)
    sg['s5_d'] = lax.dynamic_slice_in_dim(sg['s5_d'], me * 512, 512, axis=1)
    sg['b_glu'] = lax.dynamic_slice_in_dim(sg['b_glu'], me * 512, 512, axis=1)
    sg = {n: sg[n].reshape(a[n].shape) for n in s_names}
    w_pack, m_pack, v_pack, g_pack = [_pack_rows([src[pre + n] for n in s_names])
                                      for src, pre in ((a, ''), (a, 'm_'), (a, 'v_'), (sg, ''))]
    sd, sm, sv = _rowwise(_adam_math, [w_pack, g_pack, m_pack, v_pack], [], [(128, F32)] * 3, name="adam_small")
    w_shapes = [tuple(a[n].shape) for n in s_names]
    res.update({n: (sg[n], d_, m_, v_) for n, d_, m_, v_ in zip(s_names, _unpack_rows(sd, w_shapes),
                                                               _unpack_rows(sm, w_shapes), _unpack_rows(sv, w_shapes))})

    red['in'], red['pp'] = reduce_finish(started2, sd, 2)
    adam_layer('w_in_e', 0, 'in', 0)
    adam_layer('w_ple_proj', 0, 'pp', 0)
    adam_layer('w_ple_proj', 1, 'pp', 256)

    return (loss, grad_x[None], *[res[n][0] for n in _WEIGHTS], *[res[n][1] for n in _WEIGHTS],
            *[res[n][2] for n in _WEIGHTS], *[res[n][3] for n in _WEIGHTS])


def kernel(x, p, norm_mix, norm_mlp, norm_ple, w_in_e, w_out_e, hgrn_lb, g_norm_a, conv_w, a_log, dt_bias, g_norm_b, s5_a_re, s5_a_im, s5_b_re, s5_b_im, s5_c_re, s5_c_im, s5_d, s5_log_dt, w_glu, b_glu, w_out_o, w_up, w_down, w_ple_gate, w_ple_proj, final_norm, loss_target, m_norm_mix, m_norm_mlp, m_norm_ple, m_w_in_e, m_w_out_e, m_hgrn_lb, m_g_norm_a, m_conv_w, m_a_log, m_dt_bias, m_g_norm_b, m_s5_a_re, m_s5_a_im, m_s5_b_re, m_s5_b_im, m_s5_c_re, m_s5_c_im, m_s5_d, m_s5_log_dt, m_w_glu, m_b_glu, m_w_out_o, m_w_up, m_w_down, m_w_ple_gate, m_w_ple_proj, m_final_norm, v_norm_mix, v_norm_mlp, v_norm_ple, v_w_in_e, v_w_out_e, v_hgrn_lb, v_g_norm_a, v_conv_w, v_a_log, v_dt_bias, v_g_norm_b, v_s5_a_re, v_s5_a_im, v_s5_b_re, v_s5_b_im, v_s5_c_re, v_s5_c_im, v_s5_d, v_s5_log_dt, v_w_glu, v_b_glu, v_w_out_o, v_w_up, v_w_down, v_w_ple_gate, v_w_ple_proj, v_final_norm):
    args = (x, p, norm_mix, norm_mlp, norm_ple, w_in_e, w_out_e, hgrn_lb, g_norm_a, conv_w, a_log, dt_bias, g_norm_b, s5_a_re, s5_a_im, s5_b_re, s5_b_im, s5_c_re, s5_c_im, s5_d, s5_log_dt, w_glu, b_glu, w_out_o, w_up, w_down, w_ple_gate, w_ple_proj, final_norm, loss_target, m_norm_mix, m_norm_mlp, m_norm_ple, m_w_in_e, m_w_out_e, m_hgrn_lb, m_g_norm_a, m_conv_w, m_a_log, m_dt_bias, m_g_norm_b, m_s5_a_re, m_s5_a_im, m_s5_b_re, m_s5_b_im, m_s5_c_re, m_s5_c_im, m_s5_d, m_s5_log_dt, m_w_glu, m_b_glu, m_w_out_o, m_w_up, m_w_down, m_w_ple_gate, m_w_ple_proj, m_final_norm, v_norm_mix, v_norm_mlp, v_norm_ple, v_w_in_e, v_w_out_e, v_hgrn_lb, v_g_norm_a, v_conv_w, v_a_log, v_dt_bias, v_g_norm_b, v_s5_a_re, v_s5_a_im, v_s5_b_re, v_s5_b_im, v_s5_c_re, v_s5_c_im, v_s5_d, v_s5_log_dt, v_w_glu, v_b_glu, v_w_out_o, v_w_up, v_w_down, v_w_ple_gate, v_w_ple_proj, v_final_norm)
    return _step(dict(zip(_INPUTS, args)))
```

```python
import functools
import math

import jax
import jax.numpy as jnp
from jax import lax
from jax.experimental import pallas as pl
from jax.experimental.pallas import tpu as pltpu

F32 = jnp.float32
BF16 = jnp.bfloat16

D_MODEL = 2048
SEQ = 4096
NORM_EPS = 1e-6
CHUNK = 64
HEAD = 128
VMEM_LIMIT = 56 * 1024 * 1024


_NN = (((1,), (0,)), ((), ()))
_NT = (((1,), (1,)), ((), ()))
_TN = (((0,), (0,)), ((), ()))
_HI = lax.Precision.HIGHEST
_NEG = -1e30


def _cparams(sem, **kw):
    return pltpu.CompilerParams(dimension_semantics=sem, vmem_limit_bytes=VMEM_LIMIT, **kw)


PACK_COLS = 2048


def _view_shape(view):
    kind, _, r = view
    return (4 * r, PACK_COLS) if kind == "row" else (r, 4 * PACK_COLS)


def _view_spec(view, rb, cb, row_of, col_of):
    kind, off, r = view
    assert off % rb == 0 and r % rb == 0 and PACK_COLS % cb == 0, (view, rb, cb)
    if kind == "row":
        nrb = r // rb
        return pl.BlockSpec((None, rb, cb), lambda i, j, k: (row_of(i, j, k) // nrb,
                                                             off // rb + row_of(i, j, k) % nrb, col_of(i, j, k)))
    ncb = PACK_COLS // cb
    return pl.BlockSpec((None, rb, cb), lambda i, j, k: (col_of(i, j, k) // ncb,
                                                         off // rb + row_of(i, j, k), col_of(i, j, k) % ncb))


def _mm(a, b, *, dims="nn", epilogue=None, extras=(), out_dtypes=(F32,), bm=1024, bn=1024, bk=2048, name,
        b_view=None, out_into=None, out_view=None, after=()):
    b_shape = _view_shape(b_view) if b_view is not None else b.shape
    if dims == "tn":
        (K, M), (K2, N) = a.shape, b_shape
    elif dims == "nt":
        (M, K), (N, K2) = a.shape, b_shape
    else:
        (M, K), (K2, N) = a.shape, b_shape
    assert K == K2, (a.shape, b_shape, dims)
    if b_view is not None and dims == "nt":
        bn = min(bn, b_view[2])
    if b_view is not None and dims != "nt":
        bk = min(bk, b_view[2])
    if out_view is not None:
        bm = min(bm, out_view[2])
    bm, bn, bk = min(bm, M), min(bn, N), min(bk, K)
    assert M % bm == 0 and N % bn == 0 and K % bk == 0, (M, N, K, bm, bn, bk)
    nk = K // bk
    ii, jj, kk = (lambda i, j, k: i), (lambda i, j, k: j), (lambda i, j, k: k)
    if dims == "tn":
        a_spec = pl.BlockSpec((bk, bm), lambda i, j, k: (k, i))
        dn = _TN
    else:
        a_spec = pl.BlockSpec((bm, bk), lambda i, j, k: (i, k))
        dn = _NT if dims == "nt" else _NN
    if dims == "nt":
        b_spec = _view_spec(b_view, bn, bk, jj, kk) if b_view else pl.BlockSpec((bn, bk), lambda i, j, k: (j, k))
    else:
        b_spec = _view_spec(b_view, bk, bn, kk, jj) if b_view else pl.BlockSpec((bk, bn), lambda i, j, k: (k, j))
    e_specs = []
    for e in extras:
        if e.shape == (M, N):
            e_specs.append(pl.BlockSpec((bm, bn), lambda i, j, k: (i, j)))
        else:
            assert e.shape == (1, N), e.shape
            e_specs.append(pl.BlockSpec((1, bn), lambda i, j, k: (0, j)))
    ne, no = len(extras), len(out_dtypes)
    if epilogue is None:
        epilogue = lambda acc: (acc,)
    into = out_into is not None

    def body(a_ref, b_ref, *rest):
        e_refs, rest = rest[:ne], rest[ne + (1 if into else 0) + len(after):]
        o_refs = rest[:no]
        part = lax.dot_general(a_ref[...].astype(BF16), b_ref[...].astype(BF16), dn, preferred_element_type=F32)

        def finish(total):
            outs = epilogue(total, *[e[...] for e in e_refs])
            for o, v in zip(o_refs, outs):
                o[...] = v.astype(o.dtype)

        if nk == 1:
            finish(part)
            return
        acc, k = rest[no], pl.program_id(2)

        @pl.when(k == 0)
        def _():
            acc[...] = part

        @pl.when((k > 0) & (k < nk - 1))
        def _():
            acc[...] += part

        @pl.when(k == nk - 1)
        def _():
            finish(acc[...] + part)

    if into:
        assert no == 1 and _view_shape(out_view) == (M, N), (out_view, M, N)
        out_specs = [_view_spec(out_view, bm, bn, ii, jj)]
        out_shape = [jax.ShapeDtypeStruct(out_into.shape, out_into.dtype)]
        extra_in, extra_specs, alias = [out_into], [pl.BlockSpec(memory_space=pl.ANY)], {2 + ne: 0}
    else:
        out_specs = [pl.BlockSpec((bm, bn), lambda i, j, k: (i, j)) for _ in out_dtypes]
        out_shape = [jax.ShapeDtypeStruct((M, N), dt) for dt in out_dtypes]
        extra_in, extra_specs, alias = [], [], {}
    outs = pl.pallas_call(
        body, name=name,
        grid=(M // bm, N // bn, nk),
        in_specs=[a_spec, b_spec] + e_specs + extra_specs + [pl.BlockSpec(memory_space=pl.ANY)] * len(after),
        out_specs=out_specs, out_shape=out_shape,
        scratch_shapes=[pltpu.VMEM((bm, bn), F32)] if nk > 1 else [],
        input_output_aliases=alias,
        compiler_params=_cparams(("parallel", "parallel", "arbitrary")),
    )(a, b, *extras, *extra_in, *after)
    return outs[0] if no == 1 else tuple(outs)


def _rowwise(fn, tiled, bcast, outs, sums=(), *, rows=256, name):
    S = tiled[0].shape[0]
    rows = min(rows, S)
    assert S % rows == 0
    nt, nb, no, ns = len(tiled), len(bcast), len(outs), len(sums)

    def body(*refs):
        t_refs, b_refs = refs[:nt], refs[nt:nt + nb]
        o_refs, s_refs = refs[nt + nb:nt + nb + no], refs[nt + nb + no:]
        res = fn(*[r[...] for r in t_refs], *[r[...] for r in b_refs])
        res = res if isinstance(res, (tuple, list)) else (res,)
        for o, v in zip(o_refs, res[:no]):
            o[...] = v.astype(o.dtype)
        if ns:
            @pl.when(pl.program_id(0) == 0)
            def _():
                for s in s_refs:
                    s[...] = jnp.zeros_like(s)
            for s, v in zip(s_refs, res[no:]):
                s[...] += v

    res = pl.pallas_call(
        body, name=name,
        grid=(S // rows,),
        in_specs=[pl.BlockSpec((rows, t.shape[1]), lambda i: (i, 0)) for t in tiled]
        + [pl.BlockSpec(b.shape, lambda i, nd=b.ndim: (0,) * nd) for b in bcast],
        out_specs=[pl.BlockSpec((rows, c), lambda i: (i, 0)) for c, _ in outs]
        + [pl.BlockSpec((1, c), lambda i: (0, 0)) for c in sums],
        out_shape=[jax.ShapeDtypeStruct((S, c), dt) for c, dt in outs]
        + [jax.ShapeDtypeStruct((1, c), F32) for c in sums],
        compiler_params=_cparams(("arbitrary",)),
    )(*tiled, *bcast)
    return res[0] if len(res) == 1 else tuple(res)


def _rms(x, g):
    return x * lax.rsqrt(jnp.mean(x * x, axis=-1, keepdims=True) + NORM_EPS) * g


def _mxu(a, b, dn):
    if a.ndim == 3:
        ((ca,), (cb,)), _ = dn
        dn = (((ca + 1,), (cb + 1,)), ((0,), (0,)))
    return lax.dot_general(a, b, dn, preferred_element_type=F32)


def _split(a):
    hi = a.astype(BF16)
    return hi, (a - hi.astype(F32)).astype(BF16)


def _passes(a, b, dn, three):
    if not three:
        return _mxu(a.astype(BF16), b.astype(BF16), dn)
    (ah, al), (bh, bl) = _split(a), _split(b)
    return _mxu(ah, bh, dn) + (_mxu(ah, bl, dn) + _mxu(al, bh, dn))


def _dot_grads(a, b, g, dn, three):
    if dn == _NN:
        return _passes(g, b, _NT, three), _passes(a, g, _TN, three)
    if dn == _NT:
        return _passes(g, b, _NN, three), _passes(g, a, _TN, three)
    assert dn == _TN
    return _passes(b, g, _NT, three), _passes(a, g, _NN, three)


@functools.partial(jax.custom_vjp, nondiff_argnums=(2,))
def _dot(a, b, dn=_NN):
    return _passes(a, b, dn, False)


_dot.defvjp(lambda a, b, dn: (_passes(a, b, dn, False), (a, b)),
            lambda dn, res, g: _dot_grads(res[0], res[1], g, dn, False))


@functools.partial(jax.custom_vjp, nondiff_argnums=(2,))
def _dot3(a, b, dn=_NN):
    return _passes(a, b, dn, True)


_dot3.defvjp(lambda a, b, dn: (_passes(a, b, dn, True), (a, b)),
             lambda dn, res, g: _dot_grads(res[0], res[1], g, dn, True))


def _tri_times(x, dn):
    tri = _tril(x.shape[-2]).astype(BF16)
    if x.ndim == 3:
        tri = jnp.broadcast_to(tri, (x.shape[0],) + tri.shape)
    hi, lo = _split(x)
    lo2 = (x - hi.astype(F32) - lo.astype(F32)).astype(BF16)
    return _mxu(tri, hi, dn) + (_mxu(tri, lo, dn) + _mxu(tri, lo2, dn))


@jax.custom_vjp
def _cumsum_rows(x):
    return _tri_times(x, _NN)


_cumsum_rows.defvjp(lambda x: (_tri_times(x, _NN), None), lambda _, g: (_tri_times(g, _TN),))


def _iota(shape, dim):
    return lax.broadcasted_iota(jnp.int32, shape, dim)


def _tril(n, strict=False):
    r, c = _iota((n, n), 0), _iota((n, n), 1)
    return (r > c) if strict else (r >= c)


@functools.partial(jax.custom_vjp, nondiff_argnums=(1,))
def _roll_rows(x, r):
    return pltpu.roll(x, r, 0)


def _roll_rows_fwd(x, r):
    return pltpu.roll(x, r, 0), None


def _roll_rows_bwd(r, _, g):
    return (pltpu.roll(g, (g.shape[0] - r) % g.shape[0], 0),)


_roll_rows.defvjp(_roll_rows_fwd, _roll_rows_bwd)


def _head_norm_gate(o, gn, gate):
    return _rms(o, gn) * jax.nn.silu(gate)


_SUB = 16
_HPS = 8


def _hgrn_chunk(q, fpre, v, gate, lb, gna, st):
    c = q.shape[0]
    forget = lb + (1.0 - lb) * jax.nn.sigmoid(fpre)
    k = 1.0 - forget
    logf = jnp.log(forget)
    cum = _cumsum_rows(logf)
    cum_end = jnp.sum(logf, axis=0, keepdims=True)
    o = _dot(q * jnp.exp(cum), st, _NT)
    st_new = st * jnp.exp(cum_end) + _dot(v, k * jnp.exp(cum_end - cum), _TN)
    t = _iota((c, 1), 0)
    s_off = jnp.zeros((c, c), F32)
    for i in range(1, c // _SUB):
        before = t < i * _SUB
        c_i = jnp.sum(jnp.where(before, logf, 0.0), axis=0, keepdims=True)
        in_blk = (t >= i * _SUB) & (t < (i + 1) * _SUB)
        qi = jnp.where(in_blk, q * jnp.exp(jnp.minimum(cum - c_i, 0.0)), 0.0)
        ki = jnp.where(before, k * jnp.exp(jnp.minimum(c_i - cum, 0.0)), 0.0)
        s_off = s_off + _dot(qi, ki, _NT)
    o = o + _dot(s_off, v)
    tmod = t % _SUB
    for r in range(_SUB):
        kr, cr, vr = (k, cum, v) if r == 0 else (_roll_rows(k, r), _roll_rows(cum, r), _roll_rows(v, r))
        w = q * kr * jnp.exp(jnp.where(tmod >= r, cum - cr, _NEG))
        o = o + jnp.sum(w, axis=1, keepdims=True) * vr
    return _head_norm_gate(o, gna, gate), st_new


def _hgrn_fwd(proj, lb, gna):
    S = proj.shape[0]
    n_chunks, heads = S // CHUNK, 8

    def body(q_ref, f_ref, v_ref, g_ref, lb_ref, gna_ref, o_ref, st_out, st):
        n, hb = pl.program_id(0), pl.program_id(1)

        @pl.when(n == 0)
        def _():
            st[hb] = jnp.zeros((_HPS, HEAD, HEAD), F32)

        sls = [slice(hh * HEAD, (hh + 1) * HEAD) for hh in range(_HPS)]
        st0 = [st[hb, hh] for hh in range(_HPS)]
        ins = [(q_ref[:, sl], f_ref[:, sl], v_ref[:, sl], g_ref[:, sl], lb_ref[:, sl]) for sl in sls]
        gna = gna_ref[...]
        res = [_hgrn_chunk(*ins[hh], gna, st0[hh]) for hh in range(_HPS)]
        for hh, sl in enumerate(sls):
            st_out[hh] = st0[hh]
            o_ref[:, sl] = res[hh][0].astype(o_ref.dtype)
            st[hb, hh] = res[hh][1]

    wide = _HPS * HEAD
    sec = lambda s: pl.BlockSpec((CHUNK, wide), lambda n, h, s=s: (n, (8 // _HPS) * s + h))
    return pl.pallas_call(
        body, name="hgrn_fwd", grid=(n_chunks, heads // _HPS),
        in_specs=[sec(0), sec(1), sec(2), sec(3),
                  pl.BlockSpec((1, wide), lambda n, h: (0, h)), pl.BlockSpec((1, HEAD), lambda n, h: (0, 0))],
        out_specs=[pl.BlockSpec((CHUNK, wide), lambda n, h: (n, h)),
                   pl.BlockSpec((None, _HPS, HEAD, HEAD), lambda n, h: (n, h, 0, 0))],
        out_shape=[jax.ShapeDtypeStruct((S, 1024), BF16),
                   jax.ShapeDtypeStruct((n_chunks, heads, HEAD, HEAD), F32)],
        scratch_shapes=[pltpu.VMEM((heads // _HPS, _HPS, HEAD, HEAD), F32)],
        compiler_params=_cparams(("arbitrary", "arbitrary")),
    )(proj, proj, proj, proj, lb, gna)


def _hgrn_bwd(proj, lb, gna, states, d_o, after=()):
    S = proj.shape[0]
    n_chunks, heads = S // CHUNK, 8

    def body(q_ref, f_ref, v_ref, g_ref, lb_ref, gna_ref, st_ref, do_ref, *rest):
        dq_ref, df_ref, dv_ref, dg_ref, dlb_ref, dgna_ref, dst = rest[len(after):]
        n, hb = pl.program_id(0), pl.program_id(1)

        @pl.when(n == 0)
        def _():
            dst[hb] = jnp.zeros((_HPS, HEAD, HEAD), F32)
            dlb_ref[hb] = jnp.zeros((_HPS, 1, HEAD), F32)

        @pl.when((n == 0) & (hb == 0))
        def _():
            dgna_ref[...] = jnp.zeros_like(dgna_ref)

        sls = [slice(hh * HEAD, (hh + 1) * HEAD) for hh in range(_HPS)]
        gna = gna_ref[...]
        ins = [(q_ref[:, sl], f_ref[:, sl], v_ref[:, sl], g_ref[:, sl], lb_ref[:, sl], gna, st_ref[hh])
               for hh, sl in enumerate(sls)]
        cts = [(do_ref[:, sl], dst[hb, hh]) for hh, sl in enumerate(sls)]
        grads = [jax.vjp(_hgrn_chunk, *ins[hh])[1](cts[hh]) for hh in range(_HPS)]
        dgna_sum = jnp.zeros((1, HEAD), F32)
        for hh, sl in enumerate(sls):
            dq, df, dv, dg, dlb, dgna, dst0 = grads[hh]
            dq_ref[:, sl] = dq.astype(dq_ref.dtype)
            df_ref[:, sl] = df.astype(df_ref.dtype)
            dv_ref[:, sl] = dv.astype(dv_ref.dtype)
            dg_ref[:, sl] = dg.astype(dg_ref.dtype)
            dlb_ref[hb, hh] += dlb
            dgna_sum = dgna_sum + dgna
            dst[hb, hh] = dst0
        dgna_ref[...] += dgna_sum

    rev = lambda n: n_chunks - 1 - n
    wide = _HPS * HEAD
    sec = lambda s: pl.BlockSpec((CHUNK, wide), lambda n, h, s=s: (rev(n), (8 // _HPS) * s + h))
    out = pl.BlockSpec((CHUNK, wide), lambda n, h: (rev(n), h))
    return pl.pallas_call(
        body, name="hgrn_bwd", grid=(n_chunks, heads // _HPS),
        in_specs=[sec(0), sec(1), sec(2), sec(3),
                  pl.BlockSpec((1, wide), lambda n, h: (0, h)), pl.BlockSpec((1, HEAD), lambda n, h: (0, 0)),
                  pl.BlockSpec((None, _HPS, HEAD, HEAD), lambda n, h: (rev(n), h, 0, 0)),
                  pl.BlockSpec((CHUNK, wide), lambda n, h: (rev(n), h))] + [pl.BlockSpec(memory_space=pl.ANY)] * len(after),
        out_specs=[out, out, out, out,
                   pl.BlockSpec((heads // _HPS, _HPS, 1, HEAD), lambda n, h: (0, 0, 0, 0)),
                   pl.BlockSpec((1, HEAD), lambda n, h: (0, 0))],
        out_shape=[jax.ShapeDtypeStruct((S, 1024), BF16)] * 4
        + [jax.ShapeDtypeStruct((heads // _HPS, _HPS, 1, HEAD), F32), jax.ShapeDtypeStruct((1, HEAD), F32)],
        scratch_shapes=[pltpu.VMEM((heads // _HPS, _HPS, HEAD, HEAD), F32)],
        compiler_params=_cparams(("arbitrary", "arbitrary")),
    )(proj, proj, proj, proj, lb, gna, states, d_o, *after)


def _l2n(t):
    return t * lax.rsqrt(jnp.sum(t * t, axis=-1, keepdims=True) + NORM_EPS)


def _unit_lower_inverse(low):
    c = low.shape[-1]
    inv = (_iota((c, c), 0) == _iota((c, c), 1)).astype(F32) - low
    p = low
    span = 2
    while span < c:
        p = _dot3(p, p)
        inv = inv + _dot3(inv, p)
        span *= 2
    return inv


def _gdn_chunk(qc, kc, v, z, tail, h0, arow, dtrow, gnb, st):
    nb, c = qc.shape[0], qc.shape[1]
    head = h0 + _iota((nb, c, HEAD), 0)
    lane = _iota((nb, c, HEAD), 2)
    la_all = arow * jax.nn.softplus(tail + dtrow)
    la = jnp.sum(jnp.where(lane == head, la_all[None], 0.0), axis=2, keepdims=True)
    beta = jnp.sum(jnp.where(lane == head + 8, jax.nn.sigmoid(tail)[None], 0.0), axis=2, keepdims=True)
    la_b = jnp.broadcast_to(la, (nb, c, HEAD))
    cum = _cumsum_rows(la_b)
    cmat = _cumsum_rows(jnp.broadcast_to(la, (nb, c, c)))
    cum_end = jnp.sum(la_b, axis=1, keepdims=True)
    decay = jnp.exp(jnp.where(_tril(c), cmat - jnp.swapaxes(cmat, 1, 2), _NEG))
    q = _l2n(qc) * (HEAD ** -0.5)
    k = _l2n(kc)
    k_beta = k * beta
    low = jnp.where(_tril(c, strict=True), _dot(k_beta, k, _NT) * decay, 0.0)
    inv = _unit_lower_inverse(low)
    u, w = _dot(inv, v * beta), _dot(inv, k_beta * jnp.exp(cum))
    intra = _dot(q, k, _NT) * decay
    v_new = u - _dot(w, st, _NT)
    o = _dot(q * jnp.exp(cum), st, _NT) + _dot(intra, v_new)
    st_new = st * jnp.exp(cum_end) + _dot(v_new, k * jnp.exp(cum_end - cum), _TN)
    return _head_norm_gate(o, gnb, z), st_new


def _gdn_specs(n_of):
    wide = _HPS * HEAD
    qkv = lambda s: pl.BlockSpec((CHUNK, wide), lambda n, h, s=s: (n_of(n), (8 // _HPS) * s + h))
    return [qkv(0), qkv(1), qkv(2),
            pl.BlockSpec((CHUNK, wide), lambda n, h: (n_of(n), 56 // _HPS + h)),
            pl.BlockSpec((CHUNK, HEAD), lambda n, h: (n_of(n), 64)),
            pl.BlockSpec((1, HEAD), lambda n, h: (0, 0)), pl.BlockSpec((1, HEAD), lambda n, h: (0, 0)),
            pl.BlockSpec((1, HEAD), lambda n, h: (0, 0))]


def _gdn_fwd(act, proj, arow, dtrow, gnb):
    S = act.shape[0]
    n_chunks, heads = S // CHUNK, 8

    def body(q_ref, k_ref, v_ref, z_ref, t_ref, a_ref, dt_ref, gnb_ref, o_ref, st_out, st):
        n, hb = pl.program_id(0), pl.program_id(1)

        @pl.when(n == 0)
        def _():
            st[hb] = jnp.zeros((_HPS, HEAD, HEAD), F32)

        sls = [slice(hh * HEAD, (hh + 1) * HEAD) for hh in range(_HPS)]
        heads_of = lambda ref: jnp.stack([ref[:, sl] for sl in sls])
        st0 = st[hb]
        o, st1 = _gdn_chunk(heads_of(q_ref), heads_of(k_ref), heads_of(v_ref), heads_of(z_ref), t_ref[...], hb * _HPS,
                            a_ref[...], dt_ref[...], gnb_ref[...], st0)
        st_out[...] = st0
        st[hb] = st1
        for hh, sl in enumerate(sls):
            o_ref[:, sl] = o[hh].astype(o_ref.dtype)

    return pl.pallas_call(
        body, name="gdn_fwd", grid=(n_chunks, heads // _HPS),
        in_specs=_gdn_specs(lambda n: n),
        out_specs=[pl.BlockSpec((CHUNK, _HPS * HEAD), lambda n, h: (n, h)),
                   pl.BlockSpec((None, _HPS, HEAD, HEAD), lambda n, h: (n, h, 0, 0))],
        out_shape=[jax.ShapeDtypeStruct((S, 1024), BF16),
                   jax.ShapeDtypeStruct((n_chunks, heads, HEAD, HEAD), F32)],
        scratch_shapes=[pltpu.VMEM((heads // _HPS, _HPS, HEAD, HEAD), F32)],
        compiler_params=_cparams(("arbitrary", "arbitrary")),
    )(act, act, act, proj, proj, arow, dtrow, gnb)


def _gdn_bwd(act, proj, arow, dtrow, gnb, states, d_o):
    S = act.shape[0]
    n_chunks, heads = S // CHUNK, 8

    def body(q_ref, k_ref, v_ref, z_ref, t_ref, a_ref, dt_ref, gnb_ref, st_ref, do_ref,
             dq_ref, dk_ref, dv_ref, dz_ref, dt_out, da_ref, ddt_ref, dgnb_ref, dst, dtail_acc):
        n, hb = pl.program_id(0), pl.program_id(1)
        n_hb = heads // _HPS

        @pl.when(n == 0)
        def _():
            dst[hb] = jnp.zeros((_HPS, HEAD, HEAD), F32)

        @pl.when((n == 0) & (hb == 0))
        def _():
            da_ref[...] = jnp.zeros_like(da_ref)
            ddt_ref[...] = jnp.zeros_like(ddt_ref)
            dgnb_ref[...] = jnp.zeros_like(dgnb_ref)

        sls = [slice(hh * HEAD, (hh + 1) * HEAD) for hh in range(_HPS)]
        heads_of = lambda ref: jnp.stack([ref[:, sl] for sl in sls])
        fn = lambda qc, kc, v, z, tail, arow, dtrow, gnb, st: _gdn_chunk(qc, kc, v, z, tail, hb * _HPS, arow, dtrow,
                                                                        gnb, st)
        _, vjp = jax.vjp(fn, heads_of(q_ref), heads_of(k_ref), heads_of(v_ref), heads_of(z_ref), t_ref[...],
                         a_ref[...], dt_ref[...], gnb_ref[...], st_ref[...])
        dq, dk, dv, dz, dtail_sum, da_sum, ddt_sum, dgnb_sum, dst0 = vjp((heads_of(do_ref), dst[hb]))
        dst[hb] = dst0
        for hh, sl in enumerate(sls):
            dq_ref[:, sl] = dq[hh]
            dk_ref[:, sl] = dk[hh]
            dv_ref[:, sl] = dv[hh]
            dz_ref[:, sl] = dz[hh].astype(dz_ref.dtype)

        @pl.when(hb == 0)
        def _():
            dtail_acc[...] = dtail_sum

        @pl.when(hb > 0)
        def _():
            dtail_acc[...] += dtail_sum

        @pl.when(hb == n_hb - 1)
        def _():
            dt_out[...] = dtail_acc[...].astype(dt_out.dtype)

        da_ref[...] += da_sum
        ddt_ref[...] += ddt_sum
        dgnb_ref[...] += dgnb_sum

    rev = lambda n: n_chunks - 1 - n
    out = pl.BlockSpec((CHUNK, _HPS * HEAD), lambda n, h: (rev(n), h))
    row = pl.BlockSpec((1, HEAD), lambda n, h: (0, 0))
    return pl.pallas_call(
        body, name="gdn_bwd", grid=(n_chunks, heads // _HPS),
        in_specs=_gdn_specs(rev)
        + [pl.BlockSpec((None, _HPS, HEAD, HEAD), lambda n, h: (rev(n), h, 0, 0)),
           pl.BlockSpec((CHUNK, _HPS * HEAD), lambda n, h: (rev(n), 8 // _HPS + h))],
        out_specs=[out, out, out, out, pl.BlockSpec((CHUNK, HEAD), lambda n, h: (rev(n), 0)), row, row, row],
        out_shape=[jax.ShapeDtypeStruct((S, 1024), F32)] * 3
        + [jax.ShapeDtypeStruct((S, 1024), BF16), jax.ShapeDtypeStruct((S, HEAD), BF16)]
        + [jax.ShapeDtypeStruct((1, HEAD), F32)] * 3,
        scratch_shapes=[pltpu.VMEM((heads // _HPS, _HPS, HEAD, HEAD), F32), pltpu.VMEM((CHUNK, HEAD), F32)],
        compiler_params=_cparams(("arbitrary", "arbitrary")),
    )(act, act, act, proj, proj, arow, dtrow, gnb, states, d_o)


def _conv_silu(x, w):
    t = _iota((x.shape[0], 1), 0)
    tap = _iota(w.shape, 0)
    y = jnp.zeros_like(x)
    for r in range(4):
        w_r = jnp.sum(jnp.where(tap == 3 - r, w, 0.0), axis=0, keepdims=True)
        y = y + (x if r == 0 else jnp.where(t >= r, _roll_rows(x, r), 0.0)) * w_r
    return jax.nn.silu(y)


_CONV_COLS = 128


def _conv_fwd(proj, conv_w):
    S = proj.shape[0]
    nb = 3072 // _CONV_COLS
    off = 4096 // _CONV_COLS

    def body(x_ref, w_ref, o_ref):
        o_ref[...] = _conv_silu(x_ref[...], w_ref[...])

    return pl.pallas_call(
        body, name="conv_fwd", grid=(nb,),
        in_specs=[pl.BlockSpec((S, _CONV_COLS), lambda j: (0, off + j)), pl.BlockSpec((4, _CONV_COLS), lambda j: (0, j))],
        out_specs=pl.BlockSpec((S, _CONV_COLS), lambda j: (0, j)),
        out_shape=jax.ShapeDtypeStruct((S, 3072), F32),
        compiler_params=_cparams(("parallel",)),
    )(proj, conv_w)


def _conv_bwd(proj, conv_w, dq, dk, dv):
    S = proj.shape[0]
    nb = 3072 // _CONV_COLS
    off = 4096 // _CONV_COLS
    per = 1024 // _CONV_COLS

    def body(x_ref, w_ref, dq_ref, dk_ref, dv_ref, dx_ref, dw_ref):
        j = pl.program_id(0)
        _, vjp = jax.vjp(_conv_silu, x_ref[...], w_ref[...])
        d = jnp.where(j < per, dq_ref[...], jnp.where(j < 2 * per, dk_ref[...], dv_ref[...]))
        dx, dw = vjp(d)
        dx_ref[...] = dx.astype(dx_ref.dtype)
        dw_ref[...] = dw

    dsp = lambda s: pl.BlockSpec((S, _CONV_COLS), lambda j, s=s: (0, jnp.clip(j - s * per, 0, per - 1)))
    return pl.pallas_call(
        body, name="conv_bwd", grid=(nb,),
        in_specs=[pl.BlockSpec((S, _CONV_COLS), lambda j: (0, off + j)), pl.BlockSpec((4, _CONV_COLS), lambda j: (0, j)),
                  dsp(0), dsp(1), dsp(2)],
        out_specs=[pl.BlockSpec((S, _CONV_COLS), lambda j: (0, j)), pl.BlockSpec((4, _CONV_COLS), lambda j: (0, j))],
        out_shape=[jax.ShapeDtypeStruct((S, 3072), BF16), jax.ShapeDtypeStruct((4, 3072), F32)],
        compiler_params=_cparams(("parallel",)),
    )(proj, conv_w, dq, dk, dv)


_S5_T = 512
_S5_L = 512
_S5_NB = 16
_S5_U = 4
_S5_UB = 3


def _s5_tile(xr, xi, pw_r, pw_i, cr, ci, reverse):
    t8 = _iota((8, 1), 0)
    for sh in (1, 2, 4):
        row = (8 - sh) if reverse else (sh - 1)
        ar, ai = pw_r[row:row + 1, :], pw_i[row:row + 1, :]
        if reverse:
            keep, amt = t8 < 8 - sh, 8 - sh
        else:
            keep, amt = t8 >= sh, sh
        sr = jnp.where(keep, pltpu.roll(xr, amt, 0), 0.0)
        si = jnp.where(keep, pltpu.roll(xi, amt, 0), 0.0)
        xr, xi = xr + ar * sr - ai * si, xi + ar * si + ai * sr
    xr, xi = xr + pw_r * cr - pw_i * ci, xi + pw_r * ci + pw_i * cr
    return xr, xi


def _s5_fwd(u, bre, bim, cre, cim, pw_r, pw_i, dskip):
    S = u.shape[0]
    T = min(_S5_T, S)
    nt = S // T

    def body(u_ref, bre_ref, bim_ref, cre_ref, cim_ref, pr_ref, pi_ref, d_ref, y_ref, xr_ref, xi_ref,
             bu_r, bu_i, car_r, car_i):
        @pl.when(pl.program_id(1) == 0)
        def _():
            car_r[...] = jnp.zeros_like(car_r)
            car_i[...] = jnp.zeros_like(car_i)

        uu = u_ref[...]
        bu_r[...] = _dot(uu, bre_ref[...])
        bu_i[...] = _dot(uu, bim_ref[...])
        pw_r, pw_i = pr_ref[...], pi_ref[...]

        def tiles(i, carry):
            ins = []
            for k in range(_S5_U):
                r0 = pl.multiple_of((i * _S5_U + k) * 8, 8)
                ins.append((r0, bu_r[pl.ds(r0, 8), :], bu_i[pl.ds(r0, 8), :]))
            outs = []
            for r0, br, bi in ins:
                xr, xi = _s5_tile(br, bi, pw_r, pw_i, carry[0], carry[1], False)
                carry = (xr[7:8, :], xi[7:8, :])
                outs.append((r0, xr, xi))
            for r0, xr, xi in outs:
                xr_ref[pl.ds(r0, 8), :] = xr
                xi_ref[pl.ds(r0, 8), :] = xi
            return carry

        cr, ci = lax.fori_loop(0, T // (8 * _S5_U), tiles, (car_r[...], car_i[...]))
        car_r[...] = cr
        car_i[...] = ci
        y_ref[...] = _dot(xr_ref[...], cre_ref[...]) - _dot(xi_ref[...], cim_ref[...]) + d_ref[...] * uu

    blk3 = lambda a, b: pl.BlockSpec((None, a, b), lambda j, t: (j, 0, 0))
    return pl.pallas_call(
        body, name="s5_fwd", grid=(_S5_NB, nt),
        in_specs=[pl.BlockSpec((T, HEAD), lambda j, t: (t, j)),
                  blk3(HEAD, _S5_L), blk3(HEAD, _S5_L), blk3(_S5_L, HEAD), blk3(_S5_L, HEAD),
                  blk3(8, _S5_L), blk3(8, _S5_L), pl.BlockSpec((1, HEAD), lambda j, t: (0, j))],
        out_specs=[pl.BlockSpec((T, HEAD), lambda j, t: (t, j)),
                   pl.BlockSpec((T, _S5_L), lambda j, t: (t, j)), pl.BlockSpec((T, _S5_L), lambda j, t: (t, j))],
        out_shape=[jax.ShapeDtypeStruct((S, D_MODEL), F32),
                   jax.ShapeDtypeStruct((S, _S5_NB * _S5_L), F32), jax.ShapeDtypeStruct((S, _S5_NB * _S5_L), F32)],
        scratch_shapes=[pltpu.VMEM((T, _S5_L), F32), pltpu.VMEM((T, _S5_L), F32),
                        pltpu.VMEM((1, _S5_L), F32), pltpu.VMEM((1, _S5_L), F32)],
        compiler_params=_cparams(("parallel", "arbitrary")),
    )(u, bre, bim, cre, cim, pw_r, pw_i, dskip)


def _s5_bwd(dy, u, xre, xim, bre, bim, cre, cim, qw_r, qw_i, dskip):
    S = u.shape[0]
    T = min(_S5_T, S)
    nt = S // T
    nt8 = T // 8

    def body(dy_ref, u_ref, xr_ref, xi_ref, xpr_ref, xpi_ref, bre_ref, bim_ref, cre_ref, cim_ref, qr_ref, qi_ref,
             d_ref, du_ref, dbr_ref, dbi_ref, dcr_ref, dci_ref, dlr_ref, dli_ref, dd_ref,
             g_r, g_i, car_r, car_i):
        t = pl.program_id(1)

        @pl.when(t == 0)
        def _():
            car_r[...] = jnp.zeros_like(car_r)
            car_i[...] = jnp.zeros_like(car_i)
            for r in (dbr_ref, dbi_ref, dcr_ref, dci_ref, dlr_ref, dli_ref, dd_ref):
                r[...] = jnp.zeros_like(r)

        dyy, uu = dy_ref[...], u_ref[...]
        g_r[...] = _dot(dyy, cre_ref[...], _NT)
        g_i[...] = -_dot(dyy, cim_ref[...], _NT)
        qw_r, qw_i = qr_ref[...], qi_ref[...]
        t8 = _iota((8, 1), 0)
        first = t == nt - 1

        def load(r0, prev_r, prev_i):
            rows = pl.ds(r0, 8)
            return r0, g_r[rows, :], g_i[rows, :], xr_ref[rows, :], xi_ref[rows, :], prev_r, prev_i

        def run(loaded, carry, acc):
            done = []
            for r0, dr, di, xr, xi, prev_r, prev_i in loaded:
                gr, gi = _s5_tile(dr, di, qw_r, qw_i, carry[0], carry[1], True)
                carry = (gr[0:1, :], gi[0:1, :])
                xpr = jnp.where(t8 >= 1, pltpu.roll(xr, 1, 0), prev_r)
                xpi = jnp.where(t8 >= 1, pltpu.roll(xi, 1, 0), prev_i)
                acc = (acc[0] + gr * xpr + gi * xpi, acc[1] + gi * xpr - gr * xpi)
                done.append((r0, gr, gi))
            for r0, gr, gi in done:
                g_r[pl.ds(r0, 8), :] = gr
                g_i[pl.ds(r0, 8), :] = gi
            return carry, acc

        def step(ii, state):
            loaded = []
            for k in range(_S5_UB):
                idx = nt8 - 1 - (ii * _S5_UB + k)
                r0 = pl.multiple_of(idx * 8, 8)
                p0 = pl.multiple_of((idx - 1) * 8, 8)
                loaded.append(load(r0, xr_ref[pl.ds(p0, 8), :][7:8, :], xi_ref[pl.ds(p0, 8), :][7:8, :]))
            return run(loaded, *state)

        zero = jnp.zeros((8, _S5_L), F32)
        assert (nt8 - 1) % _S5_UB == 0
        state = lax.fori_loop(0, (nt8 - 1) // _S5_UB, step, ((car_r[...], car_i[...]), (zero, zero)))
        prev_r = jnp.where(first, 0.0, xpr_ref[...][7:8, :])
        prev_i = jnp.where(first, 0.0, xpi_ref[...][7:8, :])
        (cr, ci), (ar, ai) = run([load(0, prev_r, prev_i)], *state)
        car_r[...] = cr
        car_i[...] = ci
        dlr_ref[...] += ar
        dli_ref[...] += ai
        gr, gi = g_r[...], g_i[...]
        du_ref[...] = _dot(gr, bre_ref[...], _NT) + _dot(gi, bim_ref[...], _NT) + d_ref[...] * dyy
        dbr_ref[...] += _dot(uu, gr, _TN)
        dbi_ref[...] += _dot(uu, gi, _TN)
        dcr_ref[...] += _dot(xr_ref[...], dyy, _TN)
        dci_ref[...] -= _dot(xi_ref[...], dyy, _TN)
        dd_ref[...] += jnp.sum(dyy * uu, axis=0, keepdims=True)

    rev = lambda t: nt - 1 - t
    blk3 = lambda a, b: pl.BlockSpec((None, a, b), lambda j, t: (j, 0, 0))
    tl = pl.BlockSpec((T, HEAD), lambda j, t: (rev(t), j))
    xs = pl.BlockSpec((T, _S5_L), lambda j, t: (rev(t), j))
    xp = pl.BlockSpec((8, _S5_L), lambda j, t: (jnp.maximum(rev(t) * nt8 - 1, 0), j))
    return pl.pallas_call(
        body, name="s5_bwd", grid=(_S5_NB, nt),
        in_specs=[tl, tl, xs, xs, xp, xp, blk3(HEAD, _S5_L), blk3(HEAD, _S5_L), blk3(_S5_L, HEAD), blk3(_S5_L, HEAD),
                  blk3(8, _S5_L), blk3(8, _S5_L), pl.BlockSpec((1, HEAD), lambda j, t: (0, j))],
        out_specs=[tl, blk3(HEAD, _S5_L), blk3(HEAD, _S5_L), blk3(_S5_L, HEAD), blk3(_S5_L, HEAD),
                   blk3(8, _S5_L), blk3(8, _S5_L), pl.BlockSpec((1, HEAD), lambda j, t: (0, j))],
        out_shape=[jax.ShapeDtypeStruct((S, D_MODEL), F32),
                   jax.ShapeDtypeStruct((_S5_NB, HEAD, _S5_L), F32), jax.ShapeDtypeStruct((_S5_NB, HEAD, _S5_L), F32),
                   jax.ShapeDtypeStruct((_S5_NB, _S5_L, HEAD), F32), jax.ShapeDtypeStruct((_S5_NB, _S5_L, HEAD), F32),
                   jax.ShapeDtypeStruct((_S5_NB, 8, _S5_L), F32), jax.ShapeDtypeStruct((_S5_NB, 8, _S5_L), F32),
                   jax.ShapeDtypeStruct((1, D_MODEL), F32)],
        scratch_shapes=[pltpu.VMEM((T, _S5_L), F32), pltpu.VMEM((T, _S5_L), F32),
                        pltpu.VMEM((1, _S5_L), F32), pltpu.VMEM((1, _S5_L), F32)],
        compiler_params=_cparams(("parallel", "arbitrary")),
    )(dy, u, xre, xim, xre, xim, bre, bim, cre, cim, qw_r, qw_i, dskip)


def _s5_params(a_re, a_im, log_dt, b_re, b_im, c_re, c_im):
    step = jnp.exp(log_dt)[:, None]
    mag = jnp.exp(a_re * step)
    lr, li = mag * jnp.cos(a_im * step), mag * jnp.sin(a_im * step)
    den = a_re * a_re + a_im * a_im
    nr, ni = lr - 1.0, li
    kr, ki = (nr * a_re + ni * a_im) / den, (ni * a_re - nr * a_im) / den
    bbr = kr[..., None] * b_re - ki[..., None] * b_im
    bbi = kr[..., None] * b_im + ki[..., None] * b_re
    eye = jnp.eye(8, dtype=F32)

    def blk_b(bb):
        t = bb.reshape(_S5_NB, 8, 64, 16).transpose(0, 1, 3, 2)
        return (t[:, :, :, None, :] * eye[None, :, None, :, None]).reshape(_S5_NB, HEAD, _S5_L)

    def blk_c(cc):
        t = cc.reshape(_S5_NB, 8, 16, 64).transpose(0, 1, 3, 2)
        return (t[:, :, :, None, :] * eye[None, :, None, :, None]).reshape(_S5_NB, _S5_L, HEAD)

    return (blk_b(bbr), blk_b(bbi), blk_c(c_re), blk_c(c_im),
            lr.reshape(_S5_NB, _S5_L), li.reshape(_S5_NB, _S5_L))


def _s5_powers(lr, li):
    pr, pi = [lr], [li]
    for _ in range(7):
        pr, pi = pr + [pr[-1] * lr - pi[-1] * li], pi + [pr[-1] * li + pi[-1] * lr]
    return jnp.stack(pr, axis=1), jnp.stack(pi, axis=1)


_MESH = pl.DeviceIdType.MESH
_ANY = pl.BlockSpec(memory_space=pl.ANY)


def _place():
    x, y, c = lax.axis_index("x"), lax.axis_index("y"), lax.axis_index("c")
    return x, y, c, [(1 - x, y), (x, 1 - y), (1 - x, 1 - y)]


def _comm_call(body, arrs, out_shapes, n_remote, name):
    n = len(arrs)
    return pl.pallas_call(
        body, name=name,
        in_specs=[_ANY] * n, out_specs=[_ANY] * n, out_shape=out_shapes,
        scratch_shapes=[pltpu.SemaphoreType.DMA((n * n_remote,)), pltpu.SemaphoreType.DMA((n * n_remote,)),
                        pltpu.SemaphoreType.DMA((n,))],
    )(*arrs)


def _half(ref, slot, h):
    if len(ref.shape) == 4:
        return ref.at[slot, h]
    cols = ref.shape[2] // 2
    return ref.at[slot, :, pl.ds(pl.multiple_of(h * cols, 128), cols)]


def _own_slot(shard, me_idx, name, rows=256, cols=512):
    def body(me_ref, in_ref, o_ref):
        o_ref[...] = in_ref[...]

    if shard.ndim == 2:
        R, C = shard.shape
        return pl.pallas_call(
            body, name=name,
            grid_spec=pltpu.PrefetchScalarGridSpec(
                num_scalar_prefetch=1, grid=(C // cols,),
                in_specs=[pl.BlockSpec((R, cols), lambda j, me: (0, j))],
                out_specs=pl.BlockSpec((None, R, cols), lambda j, me: (me[0], 0, j))),
            out_shape=jax.ShapeDtypeStruct((4,) + shard.shape, shard.dtype),
            compiler_params=_cparams(("parallel",)),
        )(me_idx, shard)
    _, half, C = shard.shape
    rows = min(rows, half)
    assert half % rows == 0
    return pl.pallas_call(
        body, name=name,
        grid_spec=pltpu.PrefetchScalarGridSpec(
            num_scalar_prefetch=1, grid=(2, half // rows),
            in_specs=[pl.BlockSpec((None, rows, C), lambda h, i, me: (h, i, 0))],
            out_specs=pl.BlockSpec((None, None, rows, C), lambda h, i, me: (me[0], h, i, 0))),
        out_shape=jax.ShapeDtypeStruct((4,) + shard.shape, shard.dtype),
        compiler_params=_cparams(("parallel", "parallel")),
    )(me_idx, shard)


def _pair_split(arrs, name):
    n = len(arrs)

    def body(*refs):
        ins, outs, (ssem, rsem, _) = refs[:n], refs[n:2 * n], refs[2 * n:]
        x, y, c, _ = _place()
        copies = []
        for a in range(n):
            for s in range(4):
                cp = pltpu.make_async_remote_copy(src_ref=_half(ins[a], s, 1 - c), dst_ref=outs[a].at[s],
                                                  send_sem=ssem.at[4 * a + s], recv_sem=rsem.at[4 * a + s],
                                                  device_id=(x, y, 1 - c), device_id_type=_MESH)
                cp.start()
                copies.append(cp)
        for cp in copies:
            cp.wait()

    shapes = [jax.ShapeDtypeStruct((4,) + (a.shape[2:] if a.ndim == 4 else (a.shape[1], a.shape[2] // 2)), a.dtype)
              for a in arrs]
    return _comm_call(body, arrs, shapes, 4, name)


def _pair_swap(arrs, name):
    n = len(arrs)

    def body(*refs):
        ins, outs, (ssem, rsem, _) = refs[:n], refs[n:2 * n], refs[2 * n:]
        x, y, c, _ = _place()
        copies = []
        for a in range(n):
            cp = pltpu.make_async_remote_copy(src_ref=ins[a], dst_ref=outs[a], send_sem=ssem.at[a], recv_sem=rsem.at[a],
                                              device_id=(x, y, 1 - c), device_id_type=_MESH)
            cp.start()
            copies.append(cp)
        for cp in copies:
            cp.wait()

    return _comm_call(body, arrs, [jax.ShapeDtypeStruct(a.shape, a.dtype) for a in arrs], 1, name)


def _pair_sum(full, recv, c_idx, name, rows=128):
    def add(c_ref, a_ref, b_ref, o_ref):
        o_ref[...] = (a_ref[...].astype(F32) + b_ref[...].astype(F32)).astype(o_ref.dtype)

    if full.ndim == 3:
        _, R, cols = recv.shape
        blk = lambda col: pl.BlockSpec((None, R, cols), lambda s, c: (s, 0, col(c)))
        return pl.pallas_call(
            add, name=name,
            grid_spec=pltpu.PrefetchScalarGridSpec(
                num_scalar_prefetch=1, grid=(4,),
                in_specs=[blk(lambda c: c[0]), blk(lambda c: 0)], out_specs=blk(lambda c: 0)),
            out_shape=jax.ShapeDtypeStruct(recv.shape, recv.dtype),
            compiler_params=_cparams(("parallel",)),
        )(c_idx, full, recv)
    _, _, half, C = full.shape
    rows = min(rows, half)
    nb = half // rows
    assert half % rows == 0

    def body(c_ref, a_ref, b_ref, o_ref):
        o_ref[...] = (a_ref[...].astype(F32) + b_ref[...].astype(F32)).astype(o_ref.dtype)

    return pl.pallas_call(
        body, name=name,
        grid_spec=pltpu.PrefetchScalarGridSpec(
            num_scalar_prefetch=1, grid=(4, nb),
            in_specs=[pl.BlockSpec((None, None, rows, C), lambda s, i, c: (s, c[0], i, 0)),
                      pl.BlockSpec((None, rows, C), lambda s, i, c: (s, i, 0))],
            out_specs=pl.BlockSpec((None, rows, C), lambda s, i, c: (s, i, 0))),
        out_shape=jax.ShapeDtypeStruct(recv.shape, recv.dtype),
        compiler_params=_cparams(("parallel", "parallel")),
    )(c_idx, full, recv)


_HBM = pl.BlockSpec(memory_space=pltpu.HBM)
_SEM = pl.BlockSpec(memory_space=pltpu.SEMAPHORE)
_SPLIT = dict(has_side_effects=pltpu.SideEffectType.DATAFLOW_SIDE_EFFECTING)


def _hbm(t):
    return pltpu.with_memory_space_constraint(t, pltpu.HBM)


def _ag_copy(buf_ref, ssem, rsem, j, chip, c, me):
    px, py = chip
    mine = _half(buf_ref, me, c)
    return pltpu.make_async_remote_copy(src_ref=mine, dst_ref=mine, send_sem=ssem.at[j], recv_sem=rsem.at[j],
                                        device_id=(px, py, c), device_id_type=_MESH)


def _ag_start(bufs, after, name):
    n = len(bufs)

    def body(*refs):
        buf_refs, (ssem, rsem), token = refs[:n], refs[n + 1:n + 3], refs[-1]
        x, y, c, chips = _place()
        for a in range(n):
            for j, chip in enumerate(chips):
                _ag_copy(buf_refs[a], ssem, rsem, 3 * a + j, chip, c, 2 * x + y).start()
        token[...] = jnp.zeros_like(token)

    out = pl.pallas_call(
        body, name=name,
        out_shape=(pltpu.SemaphoreType.DMA((3 * n,)), pltpu.SemaphoreType.DMA((3 * n,)),
                   *[pltpu.HBM(b.shape, b.dtype) for b in bufs], jax.ShapeDtypeStruct((8, HEAD), F32)),
        in_specs=(*[_HBM] * n, _ANY), out_specs=(_SEM, _SEM, *[_HBM] * n, pl.BlockSpec(memory_space=pltpu.VMEM)),
        input_output_aliases={a: 2 + a for a in range(n)}, compiler_params=pltpu.CompilerParams(**_SPLIT),
    )(*[_hbm(b) for b in bufs], after)
    return out[0], out[1], list(out[2:2 + n]), out[-1]


def _ag_wait(ssem, rsem, bufs, after, name):
    n = len(bufs)

    def body(*refs):
        buf_refs, ssem, rsem = refs[:n], refs[n], refs[n + 1]
        x, y, c, chips = _place()
        for a in range(n):
            for j, chip in enumerate(chips):
                cp = _ag_copy(buf_refs[a], ssem, rsem, 3 * a + j, chip, c, 2 * x + y)
                cp.wait_send()
                cp.wait_recv()

    return list(pl.pallas_call(
        body, name=name, out_shape=tuple(pltpu.HBM(b.shape, b.dtype) for b in bufs),
        in_specs=(*[_HBM] * n, _SEM, _SEM, _ANY), out_specs=tuple([_HBM] * n),
        input_output_aliases={a: a for a in range(n)}, compiler_params=pltpu.CompilerParams(**_SPLIT),
    )(*bufs, ssem, rsem, after))


def _pair_forward(bufs, name):
    n = len(bufs)

    def body(*refs):
        outs, (ssem, rsem) = refs[n:2 * n], refs[2 * n:]
        x, y, c, chips = _place()
        copies = []
        for a in range(n):
            for j, (px, py) in enumerate(chips):
                landed = _half(outs[a], 2 * px + py, c)
                cp = pltpu.make_async_remote_copy(src_ref=landed, dst_ref=landed, send_sem=ssem.at[3 * a + j],
                                                  recv_sem=rsem.at[3 * a + j], device_id=(x, y, 1 - c),
                                                  device_id_type=_MESH)
                cp.start()
                copies.append(cp)
        for cp in copies:
            cp.wait()

    return pl.pallas_call(
        body, name=name, in_specs=[_ANY] * n, out_specs=[_ANY] * n,
        out_shape=[jax.ShapeDtypeStruct(b.shape, b.dtype) for b in bufs],
        input_output_aliases={a: a for a in range(n)},
        scratch_shapes=[pltpu.SemaphoreType.DMA((3 * n,)), pltpu.SemaphoreType.DMA((3 * n,))],
    )(*bufs)


def _rs_copy(src_ref, land_ref, ssem, rsem, j, chip, c):
    px, py = chip
    return pltpu.make_async_remote_copy(src_ref=src_ref.at[2 * px + py], dst_ref=land_ref.at[j], send_sem=ssem.at[j],
                                        recv_sem=rsem.at[j], device_id=(px, py, c), device_id_type=_MESH)


def _rs_start(part, after, name):
    def body(part_ref, land_ref, after_ref, ssem, rsem, part_thru, land_thru, token):
        x, y, c, chips = _place()
        for j, chip in enumerate(chips):
            _rs_copy(part_ref, land_ref, ssem, rsem, j, chip, c).start()
        token[...] = jnp.zeros_like(token)

    land = jax.ShapeDtypeStruct((3,) + part.shape[1:], part.dtype)
    return pl.pallas_call(
        body, name=name,
        out_shape=(pltpu.SemaphoreType.DMA((3,)), pltpu.SemaphoreType.DMA((3,)), pltpu.HBM(part.shape, part.dtype),
                   pltpu.HBM(land.shape, land.dtype), jax.ShapeDtypeStruct((8, HEAD), F32)),
        in_specs=(_HBM, _HBM, _ANY), out_specs=(_SEM, _SEM, _HBM, _HBM, pl.BlockSpec(memory_space=pltpu.VMEM)),
        input_output_aliases={0: 2, 1: 3}, compiler_params=pltpu.CompilerParams(**_SPLIT),
    )(_hbm(part), _hbm(lax.empty(land.shape, land.dtype)), after)


def _rs_wait(ssem, rsem, part, land, after, name):
    def body(part_ref, land_ref, ssem, rsem, after_ref, part_out, land_out):
        x, y, c, chips = _place()
        for j, chip in enumerate(chips):
            cp = _rs_copy(part_ref, land_ref, ssem, rsem, j, chip, c)
            cp.wait_send()
            cp.wait_recv()

    return pl.pallas_call(
        body, name=name, out_shape=(pltpu.HBM(part.shape, part.dtype), pltpu.HBM(land.shape, land.dtype)),
        in_specs=(_HBM, _HBM, _SEM, _SEM, _ANY), out_specs=(_HBM, _HBM),
        input_output_aliases={0: 0, 1: 1}, compiler_params=pltpu.CompilerParams(**_SPLIT),
    )(part, land, ssem, rsem, after)


def _sum_own_recv(part, land, me_idx, name, rows=128):
    _, half, C = part.shape

    def body(me_ref, own_ref, land_ref, o_ref):
        acc = own_ref[...].astype(F32)
        for s in range(3):
            acc = acc + land_ref[s].astype(F32)
        o_ref[...] = acc

    if half % 8:
        cols = 256
        return pl.pallas_call(
            body, name=name,
            grid_spec=pltpu.PrefetchScalarGridSpec(
                num_scalar_prefetch=1, grid=(C // cols,),
                in_specs=[pl.BlockSpec((None, half, cols), lambda j, me: (me[0], 0, j)),
                          pl.BlockSpec((3, half, cols), lambda j, me: (0, 0, j))],
                out_specs=pl.BlockSpec((half, cols), lambda j, me: (0, j))),
            out_shape=jax.ShapeDtypeStruct((half, C), F32),
            compiler_params=_cparams(("parallel",)),
        )(me_idx, part, land)
    rows = min(rows, half)
    assert half % rows == 0
    return pl.pallas_call(
        body, name=name,
        grid_spec=pltpu.PrefetchScalarGridSpec(
            num_scalar_prefetch=1, grid=(half // rows,),
            in_specs=[pl.BlockSpec((None, rows, C), lambda i, me: (me[0], i, 0)),
                      pl.BlockSpec((3, rows, C), lambda i, me: (0, i, 0))],
            out_specs=pl.BlockSpec((rows, C), lambda i, me: (i, 0))),
        out_shape=jax.ShapeDtypeStruct((half, C), F32),
        compiler_params=_cparams(("parallel",)),
    )(me_idx, part, land)


def _sum_slots(arr, name, rows=128):
    k, R, C = arr.shape
    rows = min(rows, R)
    assert R % rows == 0

    def body(in_ref, o_ref):
        acc = in_ref[0].astype(F32)
        for s in range(1, k):
            acc = acc + in_ref[s].astype(F32)
        o_ref[...] = acc

    return pl.pallas_call(
        body, name=name, grid=(R // rows,),
        in_specs=[pl.BlockSpec((k, rows, C), lambda i: (0, i, 0))],
        out_specs=pl.BlockSpec((rows, C), lambda i: (i, 0)),
        out_shape=jax.ShapeDtypeStruct((R, C), F32),
        compiler_params=_cparams(("parallel",)),
    )(arr)


ADAM_LR, ADAM_B1, ADAM_B2, ADAM_EPS, ADAM_WD, ADAM_STEP = 0.001, 0.9, 0.999, 1e-08, 0.01, 10


def _adam_math(w, g, m, v):
    m = ADAM_B1 * m + (1.0 - ADAM_B1) * g
    v = ADAM_B2 * v + (1.0 - ADAM_B2) * jnp.square(g)
    m_hat = m / (1.0 - ADAM_B1 ** ADAM_STEP)
    v_hat = v / (1.0 - ADAM_B2 ** ADAM_STEP)
    delta = -ADAM_LR * (m_hat / (jnp.sqrt(v_hat) + ADAM_EPS) + ADAM_WD * w)
    return delta, m, v


def _adamw(w, m, v, layer, mine, other, c_idx, name, g_off=0, prev=None, after=(), rows=128):
    _, R, C = w.shape
    half = mine.shape[0]
    rows = min(rows, R)
    assert R % rows == 0 and g_off % rows == 0 and half % rows == 0
    nbh, b0 = half // rows, g_off // rows

    def body(c_ref, w_ref, m_ref, v_ref, mine_ref, other_ref, *rest):
        go, do, mo, vo = rest[-4:]
        in_my_half = (b0 + pl.program_id(0)) // nbh == c_ref[0]
        g = jnp.where(in_my_half, mine_ref[...], other_ref[...])
        delta, m1, v1 = _adam_math(w_ref[...], g, m_ref[...], v_ref[...])
        go[...] = g
        do[...] = delta
        mo[...] = m1
        vo[...] = v1

    blk = pl.BlockSpec((None, rows, C), lambda i, c: (layer, i, 0))
    gblk = pl.BlockSpec((rows, C), lambda i, c: ((b0 + i) % nbh, 0))
    carried = list(prev) if prev is not None else []
    return pl.pallas_call(
        body, name=name,
        grid_spec=pltpu.PrefetchScalarGridSpec(
            num_scalar_prefetch=1, grid=(R // rows,),
            in_specs=[blk] * 3 + [gblk] * 2 + [_ANY] * (len(carried) + len(after)), out_specs=[blk] * 4),
        out_shape=[jax.ShapeDtypeStruct(w.shape, F32)] * 4,
        input_output_aliases={6 + k: k for k in range(len(carried))},
        compiler_params=_cparams(("parallel",)),
    )(c_idx, w, m, v, mine, other, *carried, *after)


def _adamw_t(w, m, v, mine, other, c_idx, name, rows=513, cols=128):
    R, _, C = w.shape
    nbh = (C // 2) // cols
    assert R % rows == 0 and (C // 2) % cols == 0

    def body(c_ref, w_ref, m_ref, v_ref, mine_ref, other_ref, go, do, mo, vo):
        in_my_half = pl.program_id(1) // nbh == c_ref[0]
        g = jnp.where(in_my_half, mine_ref[...], other_ref[...])
        delta, m1, v1 = _adam_math(w_ref[...], g, m_ref[...], v_ref[...])
        go[...] = g
        do[...] = delta
        mo[...] = m1
        vo[...] = v1

    blk = pl.BlockSpec((rows, 1, cols), lambda i, j, c: (i, 0, j))
    gblk = pl.BlockSpec((rows, 1, cols), lambda i, j, c: (i, 0, j % nbh))
    return pl.pallas_call(
        body, name=name,
        grid_spec=pltpu.PrefetchScalarGridSpec(
            num_scalar_prefetch=1, grid=(R // rows, C // cols), in_specs=[blk] * 3 + [gblk] * 2,
            out_specs=[blk] * 4),
        out_shape=[jax.ShapeDtypeStruct(w.shape, F32)] * 4,
        compiler_params=_cparams(("parallel", "parallel")),
    )(c_idx, w, m, v, mine, other)


_WEIGHTS = ['norm_mix', 'norm_mlp', 'norm_ple', 'w_in_e', 'w_out_e', 'hgrn_lb', 'g_norm_a', 'conv_w', 'a_log',
            'dt_bias', 'g_norm_b', 's5_a_re', 's5_a_im', 's5_b_re', 's5_b_im', 's5_c_re', 's5_c_im', 's5_d',
            's5_log_dt', 'w_glu', 'b_glu', 'w_out_o', 'w_up', 'w_down', 'w_ple_gate', 'w_ple_proj', 'final_norm']
_INPUTS = ['x', 'p'] + _WEIGHTS + ['loss_target'] + ['m_' + n for n in _WEIGHTS] + ['v_' + n for n in _WEIGHTS]
_FAMILY = [('w_up', 0, 'col', 2048), ('w_down', 0, 'row', 2048), ('w_ple_gate', 0, 'row', 512), ('w_out_e', 0, 'row', 512),
           ('w_up', 1, 'col', 2048), ('w_down', 1, 'row', 2048), ('w_ple_gate', 1, 'row', 512), ('w_glu', 0, 'row', 512),
           ('w_out_o', 0, 'row', 512)]
_PACK = {('w_up', 0): 0, ('w_down', 0): 0, ('w_ple_gate', 0): 0, ('w_out_e', 0): 0,
         ('w_up', 1): 1, ('w_down', 1): 1, ('w_ple_gate', 1): 1, ('w_glu', 0): 1, ('w_out_o', 0): 1}
_IN_PAD = 8320
_IN_BLK = 1664


def _rms_bwd_fn(x, d_hn, d_res, g):
    _, vjp = jax.vjp(_rms, x, g)
    dx, dg = vjp(d_hn.astype(F32))
    return dx + d_res, dg


def _rms_bwd_both(x, d_hn, d_res, g):
    dx, dg = _rms_bwd_fn(x, d_hn, d_res, g)
    return dx, dx, dg


def _add_res(acc, h):
    return (acc + h,)


def _pack_rows(parts, lanes=128, mult=256):
    flat = jnp.concatenate([q.reshape(-1).astype(F32) for q in parts])
    n = flat.shape[0]
    rows = -(-n // (lanes * mult)) * mult
    return jnp.pad(flat, (0, rows * lanes - n)).reshape(rows, lanes)


def _unpack_rows(buf, shapes):
    flat, out, off = buf.reshape(-1), [], 0
    for s in shapes:
        n = math.prod(s)
        out.append(flat[off:off + n].reshape(s))
        off += n
    return out


def _step(a):
    S = a['x'].shape[1]
    x, tgt = a['x'][0], a['loss_target'][0]
    xi, yi = lax.axis_index("x"), lax.axis_index("y")
    me = 2 * xi + yi
    row = lambda t: t.reshape(1, -1)

    views, pack_of, packs, rt = {}, {}, [[], []], [0, 0]
    for nm, l, kind, r in _FAMILY:
        views[nm, l], pack_of[nm, l] = (kind, rt[_PACK[nm, l]], r), _PACK[nm, l]
        packs[_PACK[nm, l]].append(a[nm][l].astype(BF16))
        rt[_PACK[nm, l]] += r
    small_sh = jnp.concatenate([a['conv_w'][0].reshape(-1), a['s5_d'][0], a['b_glu'][0]]).reshape(16, 256)
    rows_first = lambda t: jnp.transpose(t, (2, 0, 1))
    shards = [jnp.concatenate(packs[0], axis=0), jnp.concatenate(packs[1], axis=0),
              rows_first(a['w_in_e']).reshape(2052, D_MODEL).astype(BF16),
              a['w_ple_proj'].reshape(512, 512).astype(BF16), small_sh]
    me_idx = me.astype(jnp.int32).reshape(1)
    c_idx = lax.axis_index("c").astype(jnp.int32).reshape(1)
    own = lambda i: _own_slot(shards[i] if i == 2 else shards[i].reshape(2, shards[i].shape[0] // 2, shards[i].shape[1]),
                              me_idx, name=f"own_slot{i}")
    whole = lambda g, t: g.reshape((4,) + t.shape)
    first = _ag_start([own(2), own(3), own(4)], me_idx, name="gather_first_start")
    slots = [own(0), own(1)]
    landed = _pair_forward(_ag_wait(first[0], first[1], first[2], slots[1], name="gather_first_wait"),
                           name="gather_first_pass")
    g_in, g_pp, g_small = [whole(g, t) for g, t in zip(landed, shards[2:])]
    gw = {}
    W = lambda nm, l: dict(b=gw[pack_of[nm, l]], b_view=views[nm, l])
    sems0 = _ag_start([slots[0]], g_in, name="gather0_start")
    win_t = jnp.pad(g_in.reshape(8208, D_MODEL), ((0, _IN_PAD - 8208), (0, 0)))
    w_pp = g_pp.reshape(4, 2, 256, 512).transpose(1, 2, 0, 3).reshape(2, 256, D_MODEL)
    g_small = g_small.reshape(4, 4096)
    conv_full = g_small[:, :3072].reshape(4, 4, 768).transpose(1, 0, 2).reshape(4, 3072)
    s5_d_full = g_small[:, 3072:3584].reshape(1, D_MODEL)
    b_glu_full = g_small[:, 3584:].reshape(1, D_MODEL)

    lb, lb_vjp = jax.vjp(lambda hl: jax.nn.softmax(hl, axis=0)[0:1], a['hgrn_lb'])
    pad_h = lambda t: jnp.pad(t, ((0, 0), (0, HEAD - t.shape[1])))
    (arow, dtrow), gdn_vjp = jax.vjp(lambda al, dt: (pad_h(-jnp.exp(al)), pad_h(dt)), a['a_log'], a['dt_bias'])
    s5p, s5_vjp = jax.vjp(_s5_params, a['s5_a_re'][0], a['s5_a_im'][0], a['s5_log_dt'][0], a['s5_b_re'][0],
                          a['s5_b_im'][0], a['s5_c_re'][0], a['s5_c_im'][0])
    bre, bim, cre, cim = [t.astype(BF16) for t in s5p[:4]]
    pw_r, pw_i = _s5_powers(s5p[4], s5p[5])
    gna, gnb = a['g_norm_a'], a['g_norm_b']

    def norm_cast(h, g, dt, name):
        return _rowwise(_rms, [h], [row(g)], [(D_MODEL, dt)], name=name)

    def mlp_ple_fwd(h, l):
        hn = norm_cast(h, a['norm_mlp'][l], BF16, f"rms_mlp{l}")
        up, act = _mm(hn, **W('w_up', l), out_dtypes=(BF16, BF16), name=f"up{l}",
                      epilogue=lambda acc: (acc, jnp.square(jnp.maximum(acc, 0.0))))
        h2 = _mm(act, **W('w_down', l), epilogue=_add_res, extras=(h,), name=f"down{l}")
        hnp = norm_cast(h2, a['norm_ple'][l], BF16, f"rms_ple{l}")
        pp = _mm(a['p'][l, 0], w_pp[l], name=f"ple_proj{l}")
        h3, gpre = _mm(hnp, **W('w_ple_gate', l), extras=(h2, pp), out_dtypes=(F32, F32), name=f"ple_gate{l}",
                       bm=512, epilogue=lambda acc, h2, pp: (h2 + jax.nn.sigmoid(acc) * pp, acc))
        return h3, (h, hn, up, act, h2, hnp, pp, gpre)

    hn0 = norm_cast(x, a['norm_mix'][0], BF16, "rms_mix0")
    proj = _mm(hn0, win_t, dims="nt", bn=_IN_BLK, name="in_proj", after=(sems0[3],))
    o_a, st_a = _hgrn_fwd(proj, lb, gna)
    act_b = _conv_fwd(proj, conv_full)
    o_b, st_b = _gdn_fwd(act_b, proj, arow, dtrow, gnb)
    merged = jnp.concatenate([o_a, o_b], axis=1)
    landed0 = _ag_wait(sems0[0], sems0[1], sems0[2], merged, name="gather0_wait")
    gw[0] = whole(_pair_forward(landed0, name="gather0_pass")[0], shards[0])
    sems1 = _ag_start([slots[1]], gw[0], name="gather1_start")
    h1 = _mm(merged, **W('w_out_e', 0), epilogue=_add_res, extras=(x,), name="out_e", after=(sems1[3],))
    h3, saved0 = mlp_ple_fwd(h1, 0)

    u = norm_cast(h3, a['norm_mix'][1], F32, "rms_mix1")
    y, xre, xim = _s5_fwd(u, bre, bim, cre, cim, pw_r, pw_i, s5_d_full)
    actg = _rowwise(jax.nn.gelu, [y], [], [(D_MODEL, BF16)], name="gelu")
    landed1 = _ag_wait(sems1[0], sems1[1], sems1[2], actg, name="gather1_wait")
    gw[1] = whole(_pair_forward(landed1, name="gather1_pass")[0], shards[1])
    glu, z = _mm(actg, **W('w_glu', 0), extras=(actg, b_glu_full), out_dtypes=(BF16, F32), name="glu",
                 epilogue=lambda acc, act, b: (act.astype(F32) * jax.nn.sigmoid(acc + b), acc + b))
    h4 = _mm(glu, **W('w_out_o', 0), epilogue=_add_res, extras=(h3,), name="out_o")
    h6, saved1 = mlp_ple_fwd(h4, 1)

    def head(h, t, g):
        def f(h, g):
            e = _rms(h, g) - t
            return 0.5 * jnp.sum(jnp.mean(e * e, axis=-1))
        val, vjp = jax.vjp(f, h, g)
        dh, dg = vjp(jnp.ones((), F32))
        return dh, dg, jnp.full((1, HEAD), val)

    dh, d_final, loss_part = _rowwise(head, [h6, tgt], [row(a['final_norm'])], [(D_MODEL, F32)], sums=(D_MODEL, HEAD),
                                      name="loss_head")
    loss = lax.psum(loss_part[0, 0], ("x", "y", "c"))

    gb = {k: lax.empty((4, rt[k], PACK_COLS), BF16) for k in (0, 1)}

    def into(lhs, rhs, key, name, after=()):
        k = pack_of[key]
        gb[k] = _mm(lhs, rhs, dims="tn", out_into=gb[k], out_view=views[key], out_dtypes=(BF16,), name=name, after=after)

    def mlp_ple_bwd(dh, l, saved, after=()):
        h, hn, up, act, h2, hnp, pp, gpre = saved

        def gate_bwd(d, gpre, pp):
            s = jax.nn.sigmoid(gpre)
            return d * s, d * pp * s * (1.0 - s)

        d_pp, d_gp = _rowwise(gate_bwd, [dh, gpre, pp], [], [(D_MODEL, BF16), (D_MODEL, BF16)], name=f"ple_bwd{l}")
        d_wpp = _mm(a['p'][l, 0], d_pp, dims="tn", name=f"d_ple_proj{l}", after=after)
        into(hnp, d_gp, ('w_ple_gate', l), f"d_ple_gate{l}")
        d_hnp = _mm(d_gp, **W('w_ple_gate', l), dims="nt", name=f"ple_gate_t{l}")
        dh2, dh2_b, d_nple = _rowwise(_rms_bwd_both, [h2, d_hnp, dh], [row(a['norm_ple'][l])],
                                      [(D_MODEL, F32), (D_MODEL, BF16)], sums=(D_MODEL,), name=f"rms_ple_bwd{l}")
        d_up = _mm(dh2_b, **W('w_down', l), dims="nt", extras=(up,), out_dtypes=(BF16,), name=f"down_t{l}",
                   epilogue=lambda acc, up: (acc * 2.0 * jnp.maximum(up.astype(F32), 0.0),))
        into(act, dh2_b, ('w_down', l), f"d_down{l}")
        into(hn, d_up, ('w_up', l), f"d_up{l}")
        d_hn = _mm(d_up, **W('w_up', l), dims="nt", name=f"up_t{l}")
        dh1, dh1_b, d_nmlp = _rowwise(_rms_bwd_both, [h, d_hn, dh2], [row(a['norm_mlp'][l])],
                                      [(D_MODEL, F32), (D_MODEL, BF16)], sums=(D_MODEL,), name=f"rms_mlp_bwd{l}")
        return dh1, dh1_b, d_wpp, d_nple, d_nmlp

    dh4, dh4_b, d_wpp1, d_nple1, d_nmlp1 = mlp_ple_bwd(dh, 1, saved1)

    d_glu = _mm(dh4_b, **W('w_out_o', 0), dims="nt", name="out_o_t")
    into(glu, dh4_b, ('w_out_o', 0), "d_out_o")

    def glu_bwd(d, z, act):
        s, act = jax.nn.sigmoid(z), act.astype(F32)
        dz = d * act * s * (1.0 - s)
        return dz, d * s, jnp.sum(dz, axis=0, keepdims=True)

    d_z, d_actp, d_bglu = _rowwise(glu_bwd, [d_glu, z, actg], [], [(D_MODEL, BF16), (D_MODEL, F32)], sums=(D_MODEL,),
                                   name="glu_bwd")
    into(actg, d_z, ('w_glu', 0), "d_glu")

    def reduce_start(bufs, tag):
        parts = [t if t.shape[1] % 32 else t.reshape(4, 2, t.shape[1] // 2, t.shape[2]) for t in bufs]
        recv = _pair_split(parts, name=f"pair_split{tag}")
        sums = [_pair_sum(f, r, c_idx, name=f"pair_sum{tag}_{i}") for i, (f, r) in enumerate(zip(parts, recv))]
        started, prev = [], c_idx
        for i, s in enumerate(sums):
            started.append(_rs_start(s, prev, name=f"scatter{tag}_{i}_start"))
            prev = started[-1][4]
        return started

    def reduce_finish(started, after, tag):
        halves = []
        for i, (ssem, rsem, part, land, _) in enumerate(started):
            part, land = _rs_wait(ssem, rsem, part, land, after, name=f"scatter{tag}_{i}_wait")
            halves.append(_sum_own_recv(part, land, me_idx, name=f"sum_partials{tag}_{i}"))
        return list(zip(halves, _pair_swap(halves, name=f"swap_halves{tag}")))

    started1 = reduce_start([gb[1]], 1)

    def gelu_bwd(acc, dap, y):
        _, vjp = jax.vjp(jax.nn.gelu, y)
        return vjp(acc + dap)

    dy = _mm(d_z, **W('w_glu', 0), dims="nt", extras=(d_actp, y), epilogue=gelu_bwd, name="glu_t",
             after=(started1[0][4],))
    du, d_bre, d_bim, d_cre, d_cim, d_lr, d_li, d_s5d = _s5_bwd(dy, u, xre, xim, bre, bim, cre, cim,
                                                                 pw_r[:, ::-1], -pw_i[:, ::-1], s5_d_full)
    dh3, d_nmix1 = _rowwise(_rms_bwd_fn, [h3, du, dh4], [row(a['norm_mix'][1])], [(D_MODEL, F32)], sums=(D_MODEL,),
                            name="rms_mix1_bwd")
    d_are, d_aim, d_logdt, d_sbre, d_sbim, d_scre, d_scim = s5_vjp(
        (d_bre, d_bim, d_cre, d_cim, d_lr.sum(axis=1), d_li.sum(axis=1)))

    dh1, dh1_b, d_wpp0, d_nple0, d_nmlp0 = mlp_ple_bwd(dh3, 0, saved0)

    d_merged = _mm(dh1_b, **W('w_out_e', 0), dims="nt", name="out_e_t")
    into(merged, dh1_b, ('w_out_e', 0), "d_out_e")
    started0 = reduce_start([gb[0]], 0)
    dq, df, dv, dg, d_lb, d_gna = _hgrn_bwd(proj, lb, gna, st_a, d_merged, after=(started0[0][4],))
    dqb, dkb, dvb, dzb, d_tail, d_arow, d_dtrow, d_gnb = _gdn_bwd(act_b, proj, arow, dtrow, gnb, st_b, d_merged)
    d_pre, d_conv = _conv_bwd(proj, conv_full, dqb, dkb, dvb)
    d_proj = jnp.concatenate([dq, df, dv, dg, d_pre, dzb, d_tail], axis=1)
    d_win_t = _mm(d_proj, hn0, dims="tn", bm=_IN_BLK, out_dtypes=(BF16,), name="d_in_proj")
    d_hn0 = _mm(d_proj, win_t, bk=_IN_BLK, name="in_proj_t")
    grad_x, d_nmix0 = _rowwise(_rms_bwd_fn, [x, d_hn0, dh1], [row(a['norm_mix'][0])], [(D_MODEL, F32)], sums=(D_MODEL,),
                               name="rms_mix0_bwd")
    (d_hlb,) = lb_vjp(d_lb.reshape(1, 1024))
    d_alog, d_dtb = gdn_vjp((d_arow, d_dtrow))

    d_win_sh = d_win_t[:8208].reshape(4, 2052, D_MODEL)
    d_wpp_sh = jnp.stack([d_wpp0, d_wpp1]).astype(BF16).reshape(2, 256, 4, 512).transpose(2, 0, 1, 3).reshape(4, 512, 512)
    started2 = reduce_start([d_win_sh, d_wpp_sh], 2)
    small = {
        'norm_mix': jnp.concatenate([d_nmix0, d_nmix1]), 'norm_mlp': jnp.concatenate([d_nmlp0, d_nmlp1]),
        'norm_ple': jnp.concatenate([d_nple0, d_nple1]), 'hgrn_lb': d_hlb, 'g_norm_a': d_gna, 'conv_w': d_conv,
        'a_log': d_alog, 'dt_bias': d_dtb, 'g_norm_b': d_gnb, 's5_a_re': d_are, 's5_a_im': d_aim, 's5_b_re': d_sbre,
        's5_b_im': d_sbim, 's5_c_re': d_scre, 's5_c_im': d_scim, 's5_d': d_s5d, 's5_log_dt': d_logdt,
        'b_glu': d_bglu, 'final_norm': d_final}
    s_names = list(small)
    s_shapes = [tuple(small[n].shape) for n in s_names]
    mine_small = _pack_rows([small[n] for n in s_names])
    chip_small = _rowwise(lambda p, o: p + o, [mine_small, _pair_swap([mine_small], name="swap_small")[0]], [],
                          [(128, F32)], name="add_small")
    n_small = chip_small.shape[0]
    small_go = _ag_start([_own_slot(chip_small.reshape(2, n_small // 2, 128), me_idx, name="own_slot_small")],
                         started2[-1][4], name="gather_small_start")
    red = {1: reduce_finish(started1, small_go[3], 1)[0]}
    red[0] = reduce_finish(started0, small_go[3], 0)[0]
    res, order = {}, []

    def adam_layer(nm, layer, key, g_off=None):
        g_off = views[nm, layer][1] if g_off is None else g_off
        prev = res.get(nm)
        follows = () if not order or (prev is not None and order[-1] is prev[0]) else (order[-1],)
        res[nm] = tuple(_adamw(a[nm], a['m_' + nm], a['v_' + nm], layer, *red[key], c_idx, name=f"adam_{nm}{layer}",
                               g_off=g_off, prev=prev, after=follows))
        order.append(res[nm][0])

    for nm, layer in (('w_glu', 0), ('w_out_o', 0), ('w_up', 1), ('w_down', 1), ('w_ple_gate', 1)):
        adam_layer(nm, layer, 1)
    for nm in ('w_up', 'w_down', 'w_ple_gate', 'w_out_e'):
        adam_layer(nm, 0, 0)

    small_all = _pair_forward(_ag_wait(small_go[0], small_go[1], small_go[2], order[-1], name="gather_small_wait"),
                              name="gather_small_pass")[0]
    reduced = _sum_slots(small_all.reshape(4, n_small, 128), name="sum_small")
    sg = dict(zip(s_names, _unpack_rows(reduced, s_shapes)))
    sg['conv_w'] = lax.dynamic_slice_in_dim(sg['conv_w'], me * 768, 768, axis=1)
    sg['s5_d'] = lax.dynamic_slice_in_dim(sg['s5_d'], me * 512, 512, axis=1)
    sg['b_glu'] = lax.dynamic_slice_in_dim(sg['b_glu'], me * 512, 512, axis=1)
    sg = {n: sg[n].reshape(a[n].shape) for n in s_names}
    w_pack, m_pack, v_pack, g_pack = [_pack_rows([src[pre + n] for n in s_names])
                                      for src, pre in ((a, ''), (a, 'm_'), (a, 'v_'), (sg, ''))]
    sd, sm, sv = _rowwise(_adam_math, [w_pack, g_pack, m_pack, v_pack], [], [(128, F32)] * 3, name="adam_small")
    w_shapes = [tuple(a[n].shape) for n in s_names]
    res.update({n: (sg[n], d_, m_, v_) for n, d_, m_, v_ in zip(s_names, _unpack_rows(sd, w_shapes),
                                                               _unpack_rows(sm, w_shapes), _unpack_rows(sv, w_shapes))})

    red['in'], red['pp'] = reduce_finish(started2, sd, 2)
    res['w_in_e'] = tuple(jnp.transpose(o, (1, 2, 0)) for o in _adamw_t(
        rows_first(a['w_in_e']), rows_first(a['m_w_in_e']), rows_first(a['v_w_in_e']),
        *[h.reshape(2052, 1, D_MODEL // 2) for h in red['in']], c_idx, name="adam_w_in_e"))
    adam_layer('w_ple_proj', 0, 'pp', 0)
    adam_layer('w_ple_proj', 1, 'pp', 256)

    return (loss, grad_x[None], *[res[n][0] for n in _WEIGHTS], *[res[n][1] for n in _WEIGHTS],
            *[res[n][2] for n in _WEIGHTS], *[res[n][3] for n in _WEIGHTS])


def kernel(x, p, norm_mix, norm_mlp, norm_ple, w_in_e, w_out_e, hgrn_lb, g_norm_a, conv_w, a_log, dt_bias, g_norm_b, s5_a_re, s5_a_im, s5_b_re, s5_b_im, s5_c_re, s5_c_im, s5_d, s5_log_dt, w_glu, b_glu, w_out_o, w_up, w_down, w_ple_gate, w_ple_proj, final_norm, loss_target, m_norm_mix, m_norm_mlp, m_norm_ple, m_w_in_e, m_w_out_e, m_hgrn_lb, m_g_norm_a, m_conv_w, m_a_log, m_dt_bias, m_g_norm_b, m_s5_a_re, m_s5_a_im, m_s5_b_re, m_s5_b_im, m_s5_c_re, m_s5_c_im, m_s5_d, m_s5_log_dt, m_w_glu, m_b_glu, m_w_out_o, m_w_up, m_w_down, m_w_ple_gate, m_w_ple_proj, m_final_norm, v_norm_mix, v_norm_mlp, v_norm_ple, v_w_in_e, v_w_out_e, v_hgrn_lb, v_g_norm_a, v_conv_w, v_a_log, v_dt_bias, v_g_norm_b, v_s5_a_re, v_s5_a_im, v_s5_b_re, v_s5_b_im, v_s5_c_re, v_s5_c_im, v_s5_d, v_s5_log_dt, v_w_glu, v_b_glu, v_w_out_o, v_w_up, v_w_down, v_w_ple_gate, v_w_ple_proj, v_final_norm):
    args = (x, p, norm_mix, norm_mlp, norm_ple, w_in_e, w_out_e, hgrn_lb, g_norm_a, conv_w, a_log, dt_bias, g_norm_b, s5_a_re, s5_a_im, s5_b_re, s5_b_im, s5_c_re, s5_c_im, s5_d, s5_log_dt, w_glu, b_glu, w_out_o, w_up, w_down, w_ple_gate, w_ple_proj, final_norm, loss_target, m_norm_mix, m_norm_mlp, m_norm_ple, m_w_in_e, m_w_out_e, m_hgrn_lb, m_g_norm_a, m_conv_w, m_a_log, m_dt_bias, m_g_norm_b, m_s5_a_re, m_s5_a_im, m_s5_b_re, m_s5_b_im, m_s5_c_re, m_s5_c_im, m_s5_d, m_s5_log_dt, m_w_glu, m_b_glu, m_w_out_o, m_w_up, m_w_down, m_w_ple_gate, m_w_ple_proj, m_final_norm, v_norm_mix, v_norm_mlp, v_norm_ple, v_w_in_e, v_w_out_e, v_hgrn_lb, v_g_norm_a, v_conv_w, v_a_log, v_dt_bias, v_g_norm_b, v_s5_a_re, v_s5_a_im, v_s5_b_re, v_s5_b_im, v_s5_c_re, v_s5_c_im, v_s5_d, v_s5_log_dt, v_w_glu, v_b_glu, v_w_out_o, v_w_up, v_w_down, v_w_ple_gate, v_w_ple_proj, v_final_norm)
    return _step(dict(zip(_INPUTS, args)))
```

```python
import functools
import math

import jax
import jax.numpy as jnp
from jax import lax
from jax.experimental import pallas as pl
from jax.experimental.pallas import tpu as pltpu

F32 = jnp.float32
BF16 = jnp.bfloat16

D_MODEL = 2048
SEQ = 4096
NORM_EPS = 1e-6
CHUNK = 64
HEAD = 128
VMEM_LIMIT = 56 * 1024 * 1024


_NN = (((1,), (0,)), ((), ()))
_NT = (((1,), (1,)), ((), ()))
_TN = (((0,), (0,)), ((), ()))
_HI = lax.Precision.HIGHEST
_NEG = -1e30


def _cparams(sem, **kw):
    return pltpu.CompilerParams(dimension_semantics=sem, vmem_limit_bytes=VMEM_LIMIT, **kw)


PACK_COLS = 2048


def _view_shape(view):
    kind, _, r = view
    return (4 * r, PACK_COLS) if kind == "row" else (r, 4 * PACK_COLS)


def _view_spec(view, rb, cb, row_of, col_of):
    kind, off, r = view
    assert off % rb == 0 and r % rb == 0 and PACK_COLS % cb == 0, (view, rb, cb)
    if kind == "row":
        nrb = r // rb
        return pl.BlockSpec((None, rb, cb), lambda i, j, k: (row_of(i, j, k) // nrb,
                                                             off // rb + row_of(i, j, k) % nrb, col_of(i, j, k)))
    ncb = PACK_COLS // cb
    return pl.BlockSpec((None, rb, cb), lambda i, j, k: (col_of(i, j, k) // ncb,
                                                         off // rb + row_of(i, j, k), col_of(i, j, k) % ncb))


def _mm(a, b, *, dims="nn", epilogue=None, extras=(), out_dtypes=(F32,), bm=1024, bn=1024, bk=2048, name,
        b_view=None, out_into=None, out_view=None, after=()):
    b_shape = _view_shape(b_view) if b_view is not None else b.shape
    if dims == "tn":
        (K, M), (K2, N) = a.shape, b_shape
    elif dims == "nt":
        (M, K), (N, K2) = a.shape, b_shape
    else:
        (M, K), (K2, N) = a.shape, b_shape
    assert K == K2, (a.shape, b_shape, dims)
    if b_view is not None and dims == "nt":
        bn = min(bn, b_view[2])
    if b_view is not None and dims != "nt":
        bk = min(bk, b_view[2])
    if out_view is not None:
        bm = min(bm, out_view[2])
    bm, bn, bk = min(bm, M), min(bn, N), min(bk, K)
    assert M % bm == 0 and N % bn == 0 and K % bk == 0, (M, N, K, bm, bn, bk)
    nk = K // bk
    ii, jj, kk = (lambda i, j, k: i), (lambda i, j, k: j), (lambda i, j, k: k)
    if dims == "tn":
        a_spec = pl.BlockSpec((bk, bm), lambda i, j, k: (k, i))
        dn = _TN
    else:
        a_spec = pl.BlockSpec((bm, bk), lambda i, j, k: (i, k))
        dn = _NT if dims == "nt" else _NN
    if dims == "nt":
        b_spec = _view_spec(b_view, bn, bk, jj, kk) if b_view else pl.BlockSpec((bn, bk), lambda i, j, k: (j, k))
    else:
        b_spec = _view_spec(b_view, bk, bn, kk, jj) if b_view else pl.BlockSpec((bk, bn), lambda i, j, k: (k, j))
    e_specs = []
    for e in extras:
        if e.shape == (M, N):
            e_specs.append(pl.BlockSpec((bm, bn), lambda i, j, k: (i, j)))
        else:
            assert e.shape == (1, N), e.shape
            e_specs.append(pl.BlockSpec((1, bn), lambda i, j, k: (0, j)))
    ne, no = len(extras), len(out_dtypes)
    if epilogue is None:
        epilogue = lambda acc: (acc,)
    into = out_into is not None

    def body(a_ref, b_ref, *rest):
        e_refs, rest = rest[:ne], rest[ne + (1 if into else 0) + len(after):]
        o_refs = rest[:no]
        part = lax.dot_general(a_ref[...].astype(BF16), b_ref[...].astype(BF16), dn, preferred_element_type=F32)

        def finish(total):
            outs = epilogue(total, *[e[...] for e in e_refs])
            for o, v in zip(o_refs, outs):
                o[...] = v.astype(o.dtype)

        if nk == 1:
            finish(part)
            return
        acc, k = rest[no], pl.program_id(2)

        @pl.when(k == 0)
        def _():
            acc[...] = part

        @pl.when((k > 0) & (k < nk - 1))
        def _():
            acc[...] += part

        @pl.when(k == nk - 1)
        def _():
            finish(acc[...] + part)

    if into:
        assert no == 1 and _view_shape(out_view) == (M, N), (out_view, M, N)
        out_specs = [_view_spec(out_view, bm, bn, ii, jj)]
        out_shape = [jax.ShapeDtypeStruct(out_into.shape, out_into.dtype)]
        extra_in, extra_specs, alias = [out_into], [pl.BlockSpec(memory_space=pl.ANY)], {2 + ne: 0}
    else:
        out_specs = [pl.BlockSpec((bm, bn), lambda i, j, k: (i, j)) for _ in out_dtypes]
        out_shape = [jax.ShapeDtypeStruct((M, N), dt) for dt in out_dtypes]
        extra_in, extra_specs, alias = [], [], {}
    outs = pl.pallas_call(
        body, name=name,
        grid=(M // bm, N // bn, nk),
        in_specs=[a_spec, b_spec] + e_specs + extra_specs + [pl.BlockSpec(memory_space=pl.ANY)] * len(after),
        out_specs=out_specs, out_shape=out_shape,
        scratch_shapes=[pltpu.VMEM((bm, bn), F32)] if nk > 1 else [],
        input_output_aliases=alias,
        compiler_params=_cparams(("parallel", "parallel", "arbitrary")),
    )(a, b, *extras, *extra_in, *after)
    return outs[0] if no == 1 else tuple(outs)


def _rowwise(fn, tiled, bcast, outs, sums=(), *, rows=256, name):
    S = tiled[0].shape[0]
    rows = min(rows, S)
    assert S % rows == 0
    nt, nb, no, ns = len(tiled), len(bcast), len(outs), len(sums)

    def body(*refs):
        t_refs, b_refs = refs[:nt], refs[nt:nt + nb]
        o_refs, s_refs = refs[nt + nb:nt + nb + no], refs[nt + nb + no:]
        res = fn(*[r[...] for r in t_refs], *[r[...] for r in b_refs])
        res = res if isinstance(res, (tuple, list)) else (res,)
        for o, v in zip(o_refs, res[:no]):
            o[...] = v.astype(o.dtype)
        if ns:
            @pl.when(pl.program_id(0) == 0)
            def _():
                for s in s_refs:
                    s[...] = jnp.zeros_like(s)
            for s, v in zip(s_refs, res[no:]):
                s[...] += v

    res = pl.pallas_call(
        body, name=name,
        grid=(S // rows,),
        in_specs=[pl.BlockSpec((rows, t.shape[1]), lambda i: (i, 0)) for t in tiled]
        + [pl.BlockSpec(b.shape, lambda i, nd=b.ndim: (0,) * nd) for b in bcast],
        out_specs=[pl.BlockSpec((rows, c), lambda i: (i, 0)) for c, _ in outs]
        + [pl.BlockSpec((1, c), lambda i: (0, 0)) for c in sums],
        out_shape=[jax.ShapeDtypeStruct((S, c), dt) for c, dt in outs]
        + [jax.ShapeDtypeStruct((1, c), F32) for c in sums],
        compiler_params=_cparams(("arbitrary",)),
    )(*tiled, *bcast)
    return res[0] if len(res) == 1 else tuple(res)


def _rms(x, g):
    return x * lax.rsqrt(jnp.mean(x * x, axis=-1, keepdims=True) + NORM_EPS) * g


def _mxu(a, b, dn):
    if a.ndim == 3:
        ((ca,), (cb,)), _ = dn
        dn = (((ca + 1,), (cb + 1,)), ((0,), (0,)))
    return lax.dot_general(a, b, dn, preferred_element_type=F32)


def _split(a):
    hi = a.astype(BF16)
    return hi, (a - hi.astype(F32)).astype(BF16)


def _passes(a, b, dn, three):
    if not three:
        return _mxu(a.astype(BF16), b.astype(BF16), dn)
    (ah, al), (bh, bl) = _split(a), _split(b)
    return _mxu(ah, bh, dn) + (_mxu(ah, bl, dn) + _mxu(al, bh, dn))


def _dot_grads(a, b, g, dn, three):
    if dn == _NN:
        return _passes(g, b, _NT, three), _passes(a, g, _TN, three)
    if dn == _NT:
        return _passes(g, b, _NN, three), _passes(g, a, _TN, three)
    assert dn == _TN
    return _passes(b, g, _NT, three), _passes(a, g, _NN, three)


@functools.partial(jax.custom_vjp, nondiff_argnums=(2,))
def _dot(a, b, dn=_NN):
    return _passes(a, b, dn, False)


_dot.defvjp(lambda a, b, dn: (_passes(a, b, dn, False), (a, b)),
            lambda dn, res, g: _dot_grads(res[0], res[1], g, dn, False))


@functools.partial(jax.custom_vjp, nondiff_argnums=(2,))
def _dot3(a, b, dn=_NN):
    return _passes(a, b, dn, True)


_dot3.defvjp(lambda a, b, dn: (_passes(a, b, dn, True), (a, b)),
             lambda dn, res, g: _dot_grads(res[0], res[1], g, dn, True))


def _tri_times(x, dn):
    tri = _tril(x.shape[-2]).astype(BF16)
    if x.ndim == 3:
        tri = jnp.broadcast_to(tri, (x.shape[0],) + tri.shape)
    hi, lo = _split(x)
    lo2 = (x - hi.astype(F32) - lo.astype(F32)).astype(BF16)
    return _mxu(tri, hi, dn) + (_mxu(tri, lo, dn) + _mxu(tri, lo2, dn))


@jax.custom_vjp
def _cumsum_rows(x):
    return _tri_times(x, _NN)


_cumsum_rows.defvjp(lambda x: (_tri_times(x, _NN), None), lambda _, g: (_tri_times(g, _TN),))


def _iota(shape, dim):
    return lax.broadcasted_iota(jnp.int32, shape, dim)


def _tril(n, strict=False):
    r, c = _iota((n, n), 0), _iota((n, n), 1)
    return (r > c) if strict else (r >= c)


@functools.partial(jax.custom_vjp, nondiff_argnums=(1,))
def _roll_rows(x, r):
    return pltpu.roll(x, r, 0)


def _roll_rows_fwd(x, r):
    return pltpu.roll(x, r, 0), None


def _roll_rows_bwd(r, _, g):
    return (pltpu.roll(g, (g.shape[0] - r) % g.shape[0], 0),)


_roll_rows.defvjp(_roll_rows_fwd, _roll_rows_bwd)


def _head_norm_gate(o, gn, gate):
    return _rms(o, gn) * jax.nn.silu(gate)


_SUB = 16
_HPS = 8


def _hgrn_chunk(q, fpre, v, gate, lb, gna, st):
    c = q.shape[0]
    forget = lb + (1.0 - lb) * jax.nn.sigmoid(fpre)
    k = 1.0 - forget
    logf = jnp.log(forget)
    cum = _cumsum_rows(logf)
    cum_end = jnp.sum(logf, axis=0, keepdims=True)
    o = _dot(q * jnp.exp(cum), st, _NT)
    st_new = st * jnp.exp(cum_end) + _dot(v, k * jnp.exp(cum_end - cum), _TN)
    t = _iota((c, 1), 0)
    s_off = jnp.zeros((c, c), F32)
    for i in range(1, c // _SUB):
        before = t < i * _SUB
        c_i = jnp.sum(jnp.where(before, logf, 0.0), axis=0, keepdims=True)
        in_blk = (t >= i * _SUB) & (t < (i + 1) * _SUB)
        qi = jnp.where(in_blk, q * jnp.exp(jnp.minimum(cum - c_i, 0.0)), 0.0)
        ki = jnp.where(before, k * jnp.exp(jnp.minimum(c_i - cum, 0.0)), 0.0)
        s_off = s_off + _dot(qi, ki, _NT)
    o = o + _dot(s_off, v)
    tmod = t % _SUB
    for r in range(_SUB):
        kr, cr, vr = (k, cum, v) if r == 0 else (_roll_rows(k, r), _roll_rows(cum, r), _roll_rows(v, r))
        w = q * kr * jnp.exp(jnp.where(tmod >= r, cum - cr, _NEG))
        o = o + jnp.sum(w, axis=1, keepdims=True) * vr
    return _head_norm_gate(o, gna, gate), st_new


def _hgrn_fwd(proj, lb, gna):
    S = proj.shape[0]
    n_chunks, heads = S // CHUNK, 8

    def body(q_ref, f_ref, v_ref, g_ref, lb_ref, gna_ref, o_ref, st_out, st):
        n, hb = pl.program_id(0), pl.program_id(1)

        @pl.when(n == 0)
        def _():
            st[hb] = jnp.zeros((_HPS, HEAD, HEAD), F32)

        sls = [slice(hh * HEAD, (hh + 1) * HEAD) for hh in range(_HPS)]
        st0 = [st[hb, hh] for hh in range(_HPS)]
        ins = [(q_ref[:, sl], f_ref[:, sl], v_ref[:, sl], g_ref[:, sl], lb_ref[:, sl]) for sl in sls]
        gna = gna_ref[...]
        res = [_hgrn_chunk(*ins[hh], gna, st0[hh]) for hh in range(_HPS)]
        for hh, sl in enumerate(sls):
            st_out[hh] = st0[hh]
            o_ref[:, sl] = res[hh][0].astype(o_ref.dtype)
            st[hb, hh] = res[hh][1]

    wide = _HPS * HEAD
    sec = lambda s: pl.BlockSpec((CHUNK, wide), lambda n, h, s=s: (n, (8 // _HPS) * s + h))
    return pl.pallas_call(
        body, name="hgrn_fwd", grid=(n_chunks, heads // _HPS),
        in_specs=[sec(0), sec(1), sec(2), sec(3),
                  pl.BlockSpec((1, wide), lambda n, h: (0, h)), pl.BlockSpec((1, HEAD), lambda n, h: (0, 0))],
        out_specs=[pl.BlockSpec((CHUNK, wide), lambda n, h: (n, h)),
                   pl.BlockSpec((None, _HPS, HEAD, HEAD), lambda n, h: (n, h, 0, 0))],
        out_shape=[jax.ShapeDtypeStruct((S, 1024), BF16),
                   jax.ShapeDtypeStruct((n_chunks, heads, HEAD, HEAD), F32)],
        scratch_shapes=[pltpu.VMEM((heads // _HPS, _HPS, HEAD, HEAD), F32)],
        compiler_params=_cparams(("arbitrary", "arbitrary")),
    )(proj, proj, proj, proj, lb, gna)


def _hgrn_bwd(proj, lb, gna, states, d_o, after=()):
    S = proj.shape[0]
    n_chunks, heads = S // CHUNK, 8

    def body(q_ref, f_ref, v_ref, g_ref, lb_ref, gna_ref, st_ref, do_ref, *rest):
        dq_ref, df_ref, dv_ref, dg_ref, dlb_ref, dgna_ref, dst = rest[len(after):]
        n, hb = pl.program_id(0), pl.program_id(1)

        @pl.when(n == 0)
        def _():
            dst[hb] = jnp.zeros((_HPS, HEAD, HEAD), F32)
            dlb_ref[hb] = jnp.zeros((_HPS, 1, HEAD), F32)

        @pl.when((n == 0) & (hb == 0))
        def _():
            dgna_ref[...] = jnp.zeros_like(dgna_ref)

        sls = [slice(hh * HEAD, (hh + 1) * HEAD) for hh in range(_HPS)]
        gna = gna_ref[...]
        ins = [(q_ref[:, sl], f_ref[:, sl], v_ref[:, sl], g_ref[:, sl], lb_ref[:, sl], gna, st_ref[hh])
               for hh, sl in enumerate(sls)]
        cts = [(do_ref[:, sl], dst[hb, hh]) for hh, sl in enumerate(sls)]
        grads = [jax.vjp(_hgrn_chunk, *ins[hh])[1](cts[hh]) for hh in range(_HPS)]
        dgna_sum = jnp.zeros((1, HEAD), F32)
        for hh, sl in enumerate(sls):
            dq, df, dv, dg, dlb, dgna, dst0 = grads[hh]
            dq_ref[:, sl] = dq.astype(dq_ref.dtype)
            df_ref[:, sl] = df.astype(df_ref.dtype)
            dv_ref[:, sl] = dv.astype(dv_ref.dtype)
            dg_ref[:, sl] = dg.astype(dg_ref.dtype)
            dlb_ref[hb, hh] += dlb
            dgna_sum = dgna_sum + dgna
            dst[hb, hh] = dst0
        dgna_ref[...] += dgna_sum

    rev = lambda n: n_chunks - 1 - n
    wide = _HPS * HEAD
    sec = lambda s: pl.BlockSpec((CHUNK, wide), lambda n, h, s=s: (rev(n), (8 // _HPS) * s + h))
    out = pl.BlockSpec((CHUNK, wide), lambda n, h: (rev(n), h))
    return pl.pallas_call(
        body, name="hgrn_bwd", grid=(n_chunks, heads // _HPS),
        in_specs=[sec(0), sec(1), sec(2), sec(3),
                  pl.BlockSpec((1, wide), lambda n, h: (0, h)), pl.BlockSpec((1, HEAD), lambda n, h: (0, 0)),
                  pl.BlockSpec((None, _HPS, HEAD, HEAD), lambda n, h: (rev(n), h, 0, 0)),
                  pl.BlockSpec((CHUNK, wide), lambda n, h: (rev(n), h))] + [pl.BlockSpec(memory_space=pl.ANY)] * len(after),
        out_specs=[out, out, out, out,
                   pl.BlockSpec((heads // _HPS, _HPS, 1, HEAD), lambda n, h: (0, 0, 0, 0)),
                   pl.BlockSpec((1, HEAD), lambda n, h: (0, 0))],
        out_shape=[jax.ShapeDtypeStruct((S, 1024), BF16)] * 4
        + [jax.ShapeDtypeStruct((heads // _HPS, _HPS, 1, HEAD), F32), jax.ShapeDtypeStruct((1, HEAD), F32)],
        scratch_shapes=[pltpu.VMEM((heads // _HPS, _HPS, HEAD, HEAD), F32)],
        compiler_params=_cparams(("arbitrary", "arbitrary")),
    )(proj, proj, proj, proj, lb, gna, states, d_o, *after)


def _l2n(t):
    return t * lax.rsqrt(jnp.sum(t * t, axis=-1, keepdims=True) + NORM_EPS)


def _unit_lower_inverse(low):
    c = low.shape[-1]
    inv = (_iota((c, c), 0) == _iota((c, c), 1)).astype(F32) - low
    p = low
    span = 2
    while span < c:
        p = _dot3(p, p)
        inv = inv + _dot3(inv, p)
        span *= 2
    return inv


def _gdn_chunk(qc, kc, v, z, tail, h0, arow, dtrow, gnb, st):
    nb, c = qc.shape[0], qc.shape[1]
    head = h0 + _iota((nb, c, HEAD), 0)
    lane = _iota((nb, c, HEAD), 2)
    la_all = arow * jax.nn.softplus(tail + dtrow)
    la = jnp.sum(jnp.where(lane == head, la_all[None], 0.0), axis=2, keepdims=True)
    beta = jnp.sum(jnp.where(lane == head + 8, jax.nn.sigmoid(tail)[None], 0.0), axis=2, keepdims=True)
    la_b = jnp.broadcast_to(la, (nb, c, HEAD))
    cum = _cumsum_rows(la_b)
    cmat = _cumsum_rows(jnp.broadcast_to(la, (nb, c, c)))
    cum_end = jnp.sum(la_b, axis=1, keepdims=True)
    decay = jnp.exp(jnp.where(_tril(c), cmat - jnp.swapaxes(cmat, 1, 2), _NEG))
    q = _l2n(qc) * (HEAD ** -0.5)
    k = _l2n(kc)
    k_beta = k * beta
    low = jnp.where(_tril(c, strict=True), _dot(k_beta, k, _NT) * decay, 0.0)
    inv = _unit_lower_inverse(low)
    u, w = _dot(inv, v * beta), _dot(inv, k_beta * jnp.exp(cum))
    intra = _dot(q, k, _NT) * decay
    v_new = u - _dot(w, st, _NT)
    o = _dot(q * jnp.exp(cum), st, _NT) + _dot(intra, v_new)
    st_new = st * jnp.exp(cum_end) + _dot(v_new, k * jnp.exp(cum_end - cum), _TN)
    return _head_norm_gate(o, gnb, z), st_new


def _gdn_specs(n_of):
    wide = _HPS * HEAD
    qkv = lambda s: pl.BlockSpec((CHUNK, wide), lambda n, h, s=s: (n_of(n), (8 // _HPS) * s + h))
    return [qkv(0), qkv(1), qkv(2),
            pl.BlockSpec((CHUNK, wide), lambda n, h: (n_of(n), 56 // _HPS + h)),
            pl.BlockSpec((CHUNK, HEAD), lambda n, h: (n_of(n), 64)),
            pl.BlockSpec((1, HEAD), lambda n, h: (0, 0)), pl.BlockSpec((1, HEAD), lambda n, h: (0, 0)),
            pl.BlockSpec((1, HEAD), lambda n, h: (0, 0))]


def _gdn_fwd(act, proj, arow, dtrow, gnb):
    S = act.shape[0]
    n_chunks, heads = S // CHUNK, 8

    def body(q_ref, k_ref, v_ref, z_ref, t_ref, a_ref, dt_ref, gnb_ref, o_ref, st_out, st):
        n, hb = pl.program_id(0), pl.program_id(1)

        @pl.when(n == 0)
        def _():
            st[hb] = jnp.zeros((_HPS, HEAD, HEAD), F32)

        sls = [slice(hh * HEAD, (hh + 1) * HEAD) for hh in range(_HPS)]
        heads_of = lambda ref: jnp.stack([ref[:, sl] for sl in sls])
        st0 = st[hb]
        o, st1 = _gdn_chunk(heads_of(q_ref), heads_of(k_ref), heads_of(v_ref), heads_of(z_ref), t_ref[...], hb * _HPS,
                            a_ref[...], dt_ref[...], gnb_ref[...], st0)
        st_out[...] = st0
        st[hb] = st1
        for hh, sl in enumerate(sls):
            o_ref[:, sl] = o[hh].astype(o_ref.dtype)

    return pl.pallas_call(
        body, name="gdn_fwd", grid=(n_chunks, heads // _HPS),
        in_specs=_gdn_specs(lambda n: n),
        out_specs=[pl.BlockSpec((CHUNK, _HPS * HEAD), lambda n, h: (n, h)),
                   pl.BlockSpec((None, _HPS, HEAD, HEAD), lambda n, h: (n, h, 0, 0))],
        out_shape=[jax.ShapeDtypeStruct((S, 1024), BF16),
                   jax.ShapeDtypeStruct((n_chunks, heads, HEAD, HEAD), F32)],
        scratch_shapes=[pltpu.VMEM((heads // _HPS, _HPS, HEAD, HEAD), F32)],
        compiler_params=_cparams(("arbitrary", "arbitrary")),
    )(act, act, act, proj, proj, arow, dtrow, gnb)


def _gdn_bwd(act, proj, arow, dtrow, gnb, states, d_o):
    S = act.shape[0]
    n_chunks, heads = S // CHUNK, 8

    def body(q_ref, k_ref, v_ref, z_ref, t_ref, a_ref, dt_ref, gnb_ref, st_ref, do_ref,
             dq_ref, dk_ref, dv_ref, dz_ref, dt_out, da_ref, ddt_ref, dgnb_ref, dst, dtail_acc):
        n, hb = pl.program_id(0), pl.program_id(1)
        n_hb = heads // _HPS

        @pl.when(n == 0)
        def _():
            dst[hb] = jnp.zeros((_HPS, HEAD, HEAD), F32)

        @pl.when((n == 0) & (hb == 0))
        def _():
            da_ref[...] = jnp.zeros_like(da_ref)
            ddt_ref[...] = jnp.zeros_like(ddt_ref)
            dgnb_ref[...] = jnp.zeros_like(dgnb_ref)

        sls = [slice(hh * HEAD, (hh + 1) * HEAD) for hh in range(_HPS)]
        heads_of = lambda ref: jnp.stack([ref[:, sl] for sl in sls])
        fn = lambda qc, kc, v, z, tail, arow, dtrow, gnb, st: _gdn_chunk(qc, kc, v, z, tail, hb * _HPS, arow, dtrow,
                                                                        gnb, st)
        _, vjp = jax.vjp(fn, heads_of(q_ref), heads_of(k_ref), heads_of(v_ref), heads_of(z_ref), t_ref[...],
                         a_ref[...], dt_ref[...], gnb_ref[...], st_ref[...])
        dq, dk, dv, dz, dtail_sum, da_sum, ddt_sum, dgnb_sum, dst0 = vjp((heads_of(do_ref), dst[hb]))
        dst[hb] = dst0
        for hh, sl in enumerate(sls):
            dq_ref[:, sl] = dq[hh]
            dk_ref[:, sl] = dk[hh]
            dv_ref[:, sl] = dv[hh]
            dz_ref[:, sl] = dz[hh].astype(dz_ref.dtype)

        @pl.when(hb == 0)
        def _():
            dtail_acc[...] = dtail_sum

        @pl.when(hb > 0)
        def _():
            dtail_acc[...] += dtail_sum

        @pl.when(hb == n_hb - 1)
        def _():
            dt_out[...] = dtail_acc[...].astype(dt_out.dtype)

        da_ref[...] += da_sum
        ddt_ref[...] += ddt_sum
        dgnb_ref[...] += dgnb_sum

    rev = lambda n: n_chunks - 1 - n
    out = pl.BlockSpec((CHUNK, _HPS * HEAD), lambda n, h: (rev(n), h))
    row = pl.BlockSpec((1, HEAD), lambda n, h: (0, 0))
    return pl.pallas_call(
        body, name="gdn_bwd", grid=(n_chunks, heads // _HPS),
        in_specs=_gdn_specs(rev)
        + [pl.BlockSpec((None, _HPS, HEAD, HEAD), lambda n, h: (rev(n), h, 0, 0)),
           pl.BlockSpec((CHUNK, _HPS * HEAD), lambda n, h: (rev(n), 8 // _HPS + h))],
        out_specs=[out, out, out, out, pl.BlockSpec((CHUNK, HEAD), lambda n, h: (rev(n), 0)), row, row, row],
        out_shape=[jax.ShapeDtypeStruct((S, 1024), F32)] * 3
        + [jax.ShapeDtypeStruct((S, 1024), BF16), jax.ShapeDtypeStruct((S, HEAD), BF16)]
        + [jax.ShapeDtypeStruct((1, HEAD), F32)] * 3,
        scratch_shapes=[pltpu.VMEM((heads // _HPS, _HPS, HEAD, HEAD), F32), pltpu.VMEM((CHUNK, HEAD), F32)],
        compiler_params=_cparams(("arbitrary", "arbitrary")),
    )(act, act, act, proj, proj, arow, dtrow, gnb, states, d_o)


def _conv_silu(x, w):
    t = _iota((x.shape[0], 1), 0)
    tap = _iota(w.shape, 0)
    y = jnp.zeros_like(x)
    for r in range(4):
        w_r = jnp.sum(jnp.where(tap == 3 - r, w, 0.0), axis=0, keepdims=True)
        y = y + (x if r == 0 else jnp.where(t >= r, _roll_rows(x, r), 0.0)) * w_r
    return jax.nn.silu(y)


_CONV_COLS = 128


def _conv_fwd(proj, conv_w):
    S = proj.shape[0]
    nb = 3072 // _CONV_COLS
    off = 4096 // _CONV_COLS

    def body(x_ref, w_ref, o_ref):
        o_ref[...] = _conv_silu(x_ref[...], w_ref[...])

    return pl.pallas_call(
        body, name="conv_fwd", grid=(nb,),
        in_specs=[pl.BlockSpec((S, _CONV_COLS), lambda j: (0, off + j)), pl.BlockSpec((4, _CONV_COLS), lambda j: (0, j))],
        out_specs=pl.BlockSpec((S, _CONV_COLS), lambda j: (0, j)),
        out_shape=jax.ShapeDtypeStruct((S, 3072), F32),
        compiler_params=_cparams(("parallel",)),
    )(proj, conv_w)


def _conv_bwd(proj, conv_w, dq, dk, dv):
    S = proj.shape[0]
    nb = 3072 // _CONV_COLS
    off = 4096 // _CONV_COLS
    per = 1024 // _CONV_COLS

    def body(x_ref, w_ref, dq_ref, dk_ref, dv_ref, dx_ref, dw_ref):
        j = pl.program_id(0)
        _, vjp = jax.vjp(_conv_silu, x_ref[...], w_ref[...])
        d = jnp.where(j < per, dq_ref[...], jnp.where(j < 2 * per, dk_ref[...], dv_ref[...]))
        dx, dw = vjp(d)
        dx_ref[...] = dx.astype(dx_ref.dtype)
        dw_ref[...] = dw

    dsp = lambda s: pl.BlockSpec((S, _CONV_COLS), lambda j, s=s: (0, jnp.clip(j - s * per, 0, per - 1)))
    return pl.pallas_call(
        body, name="conv_bwd", grid=(nb,),
        in_specs=[pl.BlockSpec((S, _CONV_COLS), lambda j: (0, off + j)), pl.BlockSpec((4, _CONV_COLS), lambda j: (0, j)),
                  dsp(0), dsp(1), dsp(2)],
        out_specs=[pl.BlockSpec((S, _CONV_COLS), lambda j: (0, j)), pl.BlockSpec((4, _CONV_COLS), lambda j: (0, j))],
        out_shape=[jax.ShapeDtypeStruct((S, 3072), BF16), jax.ShapeDtypeStruct((4, 3072), F32)],
        compiler_params=_cparams(("parallel",)),
    )(proj, conv_w, dq, dk, dv)


_S5_T = 512
_S5_L = 512
_S5_NB = 16
_S5_U = 4
_S5_UB = 3


def _s5_tile(xr, xi, pw_r, pw_i, cr, ci, reverse):
    t8 = _iota((8, 1), 0)
    for sh in (1, 2, 4):
        row = (8 - sh) if reverse else (sh - 1)
        ar, ai = pw_r[row:row + 1, :], pw_i[row:row + 1, :]
        if reverse:
            keep, amt = t8 < 8 - sh, 8 - sh
        else:
            keep, amt = t8 >= sh, sh
        sr = jnp.where(keep, pltpu.roll(xr, amt, 0), 0.0)
        si = jnp.where(keep, pltpu.roll(xi, amt, 0), 0.0)
        xr, xi = xr + ar * sr - ai * si, xi + ar * si + ai * sr
    xr, xi = xr + pw_r * cr - pw_i * ci, xi + pw_r * ci + pw_i * cr
    return xr, xi


def _s5_fwd(u, bre, bim, cre, cim, pw_r, pw_i, dskip):
    S = u.shape[0]
    T = min(_S5_T, S)
    nt = S // T

    def body(u_ref, bre_ref, bim_ref, cre_ref, cim_ref, pr_ref, pi_ref, d_ref, y_ref, xr_ref, xi_ref,
             bu_r, bu_i, car_r, car_i):
        @pl.when(pl.program_id(1) == 0)
        def _():
            car_r[...] = jnp.zeros_like(car_r)
            car_i[...] = jnp.zeros_like(car_i)

        uu = u_ref[...]
        bu_r[...] = _dot(uu, bre_ref[...])
        bu_i[...] = _dot(uu, bim_ref[...])
        pw_r, pw_i = pr_ref[...], pi_ref[...]

        def tiles(i, carry):
            ins = []
            for k in range(_S5_U):
                r0 = pl.multiple_of((i * _S5_U + k) * 8, 8)
                ins.append((r0, bu_r[pl.ds(r0, 8), :], bu_i[pl.ds(r0, 8), :]))
            outs = []
            for r0, br, bi in ins:
                xr, xi = _s5_tile(br, bi, pw_r, pw_i, carry[0], carry[1], False)
                carry = (xr[7:8, :], xi[7:8, :])
                outs.append((r0, xr, xi))
            for r0, xr, xi in outs:
                xr_ref[pl.ds(r0, 8), :] = xr
                xi_ref[pl.ds(r0, 8), :] = xi
            return carry

        cr, ci = lax.fori_loop(0, T // (8 * _S5_U), tiles, (car_r[...], car_i[...]))
        car_r[...] = cr
        car_i[...] = ci
        y_ref[...] = _dot(xr_ref[...], cre_ref[...]) - _dot(xi_ref[...], cim_ref[...]) + d_ref[...] * uu

    blk3 = lambda a, b: pl.BlockSpec((None, a, b), lambda j, t: (j, 0, 0))
    return pl.pallas_call(
        body, name="s5_fwd", grid=(_S5_NB, nt),
        in_specs=[pl.BlockSpec((T, HEAD), lambda j, t: (t, j)),
                  blk3(HEAD, _S5_L), blk3(HEAD, _S5_L), blk3(_S5_L, HEAD), blk3(_S5_L, HEAD),
                  blk3(8, _S5_L), blk3(8, _S5_L), pl.BlockSpec((1, HEAD), lambda j, t: (0, j))],
        out_specs=[pl.BlockSpec((T, HEAD), lambda j, t: (t, j)),
                   pl.BlockSpec((T, _S5_L), lambda j, t: (t, j)), pl.BlockSpec((T, _S5_L), lambda j, t: (t, j))],
        out_shape=[jax.ShapeDtypeStruct((S, D_MODEL), F32),
                   jax.ShapeDtypeStruct((S, _S5_NB * _S5_L), F32), jax.ShapeDtypeStruct((S, _S5_NB * _S5_L), F32)],
        scratch_shapes=[pltpu.VMEM((T, _S5_L), F32), pltpu.VMEM((T, _S5_L), F32),
                        pltpu.VMEM((1, _S5_L), F32), pltpu.VMEM((1, _S5_L), F32)],
        compiler_params=_cparams(("parallel", "arbitrary")),
    )(u, bre, bim, cre, cim, pw_r, pw_i, dskip)


def _s5_bwd(dy, u, xre, xim, bre, bim, cre, cim, qw_r, qw_i, dskip):
    S = u.shape[0]
    T = min(_S5_T, S)
    nt = S // T
    nt8 = T // 8

    def body(dy_ref, u_ref, xr_ref, xi_ref, xpr_ref, xpi_ref, bre_ref, bim_ref, cre_ref, cim_ref, qr_ref, qi_ref,
             d_ref, du_ref, dbr_ref, dbi_ref, dcr_ref, dci_ref, dlr_ref, dli_ref, dd_ref,
             g_r, g_i, car_r, car_i):
        t = pl.program_id(1)

        @pl.when(t == 0)
        def _():
            car_r[...] = jnp.zeros_like(car_r)
            car_i[...] = jnp.zeros_like(car_i)
            for r in (dbr_ref, dbi_ref, dcr_ref, dci_ref, dlr_ref, dli_ref, dd_ref):
                r[...] = jnp.zeros_like(r)

        dyy, uu = dy_ref[...], u_ref[...]
        g_r[...] = _dot(dyy, cre_ref[...], _NT)
        g_i[...] = -_dot(dyy, cim_ref[...], _NT)
        qw_r, qw_i = qr_ref[...], qi_ref[...]
        t8 = _iota((8, 1), 0)
        first = t == nt - 1

        def load(r0, prev_r, prev_i):
            rows = pl.ds(r0, 8)
            return r0, g_r[rows, :], g_i[rows, :], xr_ref[rows, :], xi_ref[rows, :], prev_r, prev_i

        def run(loaded, carry, acc):
            done = []
            for r0, dr, di, xr, xi, prev_r, prev_i in loaded:
                gr, gi = _s5_tile(dr, di, qw_r, qw_i, carry[0], carry[1], True)
                carry = (gr[0:1, :], gi[0:1, :])
                xpr = jnp.where(t8 >= 1, pltpu.roll(xr, 1, 0), prev_r)
                xpi = jnp.where(t8 >= 1, pltpu.roll(xi, 1, 0), prev_i)
                acc = (acc[0] + gr * xpr + gi * xpi, acc[1] + gi * xpr - gr * xpi)
                done.append((r0, gr, gi))
            for r0, gr, gi in done:
                g_r[pl.ds(r0, 8), :] = gr
                g_i[pl.ds(r0, 8), :] = gi
            return carry, acc

        def step(ii, state):
            loaded = []
            for k in range(_S5_UB):
                idx = nt8 - 1 - (ii * _S5_UB + k)
                r0 = pl.multiple_of(idx * 8, 8)
                p0 = pl.multiple_of((idx - 1) * 8, 8)
                loaded.append(load(r0, xr_ref[pl.ds(p0, 8), :][7:8, :], xi_ref[pl.ds(p0, 8), :][7:8, :]))
            return run(loaded, *state)

        zero = jnp.zeros((8, _S5_L), F32)
        assert (nt8 - 1) % _S5_UB == 0
        state = lax.fori_loop(0, (nt8 - 1) // _S5_UB, step, ((car_r[...], car_i[...]), (zero, zero)))
        prev_r = jnp.where(first, 0.0, xpr_ref[...][7:8, :])
        prev_i = jnp.where(first, 0.0, xpi_ref[...][7:8, :])
        (cr, ci), (ar, ai) = run([load(0, prev_r, prev_i)], *state)
        car_r[...] = cr
        car_i[...] = ci
        dlr_ref[...] += ar
        dli_ref[...] += ai
        gr, gi = g_r[...], g_i[...]
        du_ref[...] = _dot(gr, bre_ref[...], _NT) + _dot(gi, bim_ref[...], _NT) + d_ref[...] * dyy
        dbr_ref[...] += _dot(uu, gr, _TN)
        dbi_ref[...] += _dot(uu, gi, _TN)
        dcr_ref[...] += _dot(xr_ref[...], dyy, _TN)
        dci_ref[...] -= _dot(xi_ref[...], dyy, _TN)
        dd_ref[...] += jnp.sum(dyy * uu, axis=0, keepdims=True)

    rev = lambda t: nt - 1 - t
    blk3 = lambda a, b: pl.BlockSpec((None, a, b), lambda j, t: (j, 0, 0))
    tl = pl.BlockSpec((T, HEAD), lambda j, t: (rev(t), j))
    xs = pl.BlockSpec((T, _S5_L), lambda j, t: (rev(t), j))
    xp = pl.BlockSpec((8, _S5_L), lambda j, t: (jnp.maximum(rev(t) * nt8 - 1, 0), j))
    return pl.pallas_call(
        body, name="s5_bwd", grid=(_S5_NB, nt),
        in_specs=[tl, tl, xs, xs, xp, xp, blk3(HEAD, _S5_L), blk3(HEAD, _S5_L), blk3(_S5_L, HEAD), blk3(_S5_L, HEAD),
                  blk3(8, _S5_L), blk3(8, _S5_L), pl.BlockSpec((1, HEAD), lambda j, t: (0, j))],
        out_specs=[tl, blk3(HEAD, _S5_L), blk3(HEAD, _S5_L), blk3(_S5_L, HEAD), blk3(_S5_L, HEAD),
                   blk3(8, _S5_L), blk3(8, _S5_L), pl.BlockSpec((1, HEAD), lambda j, t: (0, j))],
        out_shape=[jax.ShapeDtypeStruct((S, D_MODEL), F32),
                   jax.ShapeDtypeStruct((_S5_NB, HEAD, _S5_L), F32), jax.ShapeDtypeStruct((_S5_NB, HEAD, _S5_L), F32),
                   jax.ShapeDtypeStruct((_S5_NB, _S5_L, HEAD), F32), jax.ShapeDtypeStruct((_S5_NB, _S5_L, HEAD), F32),
                   jax.ShapeDtypeStruct((_S5_NB, 8, _S5_L), F32), jax.ShapeDtypeStruct((_S5_NB, 8, _S5_L), F32),
                   jax.ShapeDtypeStruct((1, D_MODEL), F32)],
        scratch_shapes=[pltpu.VMEM((T, _S5_L), F32), pltpu.VMEM((T, _S5_L), F32),
                        pltpu.VMEM((1, _S5_L), F32), pltpu.VMEM((1, _S5_L), F32)],
        compiler_params=_cparams(("parallel", "arbitrary")),
    )(dy, u, xre, xim, xre, xim, bre, bim, cre, cim, qw_r, qw_i, dskip)


def _s5_params(a_re, a_im, log_dt, b_re, b_im, c_re, c_im):
    step = jnp.exp(log_dt)[:, None]
    mag = jnp.exp(a_re * step)
    lr, li = mag * jnp.cos(a_im * step), mag * jnp.sin(a_im * step)
    den = a_re * a_re + a_im * a_im
    nr, ni = lr - 1.0, li
    kr, ki = (nr * a_re + ni * a_im) / den, (ni * a_re - nr * a_im) / den
    bbr = kr[..., None] * b_re - ki[..., None] * b_im
    bbi = kr[..., None] * b_im + ki[..., None] * b_re
    eye = jnp.eye(8, dtype=F32)

    def blk_b(bb):
        t = bb.reshape(_S5_NB, 8, 64, 16).transpose(0, 1, 3, 2)
        return (t[:, :, :, None, :] * eye[None, :, None, :, None]).reshape(_S5_NB, HEAD, _S5_L)

    def blk_c(cc):
        t = cc.reshape(_S5_NB, 8, 16, 64).transpose(0, 1, 3, 2)
        return (t[:, :, :, None, :] * eye[None, :, None, :, None]).reshape(_S5_NB, _S5_L, HEAD)

    return (blk_b(bbr), blk_b(bbi), blk_c(c_re), blk_c(c_im),
            lr.reshape(_S5_NB, _S5_L), li.reshape(_S5_NB, _S5_L))


def _s5_powers(lr, li):
    pr, pi = [lr], [li]
    for _ in range(7):
        pr, pi = pr + [pr[-1] * lr - pi[-1] * li], pi + [pr[-1] * li + pi[-1] * lr]
    return jnp.stack(pr, axis=1), jnp.stack(pi, axis=1)


_MESH = pl.DeviceIdType.MESH
_ANY = pl.BlockSpec(memory_space=pl.ANY)


def _place():
    x, y, c = lax.axis_index("x"), lax.axis_index("y"), lax.axis_index("c")
    return x, y, c, [(1 - x, y), (x, 1 - y), (1 - x, 1 - y)]


def _comm_call(body, arrs, out_shapes, n_remote, name):
    n = len(arrs)
    return pl.pallas_call(
        body, name=name,
        in_specs=[_ANY] * n, out_specs=[_ANY] * n, out_shape=out_shapes,
        scratch_shapes=[pltpu.SemaphoreType.DMA((n * n_remote,)), pltpu.SemaphoreType.DMA((n * n_remote,)),
                        pltpu.SemaphoreType.DMA((n,))],
    )(*arrs)


def _half(ref, slot, h):
    if len(ref.shape) == 4:
        return ref.at[slot, h]
    cols = ref.shape[2] // 2
    return ref.at[slot, :, pl.ds(pl.multiple_of(h * cols, 128), cols)]


def _own_slot(shard, me_idx, name, rows=256, cols=512):
    def body(me_ref, in_ref, o_ref):
        o_ref[...] = in_ref[...]

    if shard.ndim == 2:
        R, C = shard.shape
        return pl.pallas_call(
            body, name=name,
            grid_spec=pltpu.PrefetchScalarGridSpec(
                num_scalar_prefetch=1, grid=(C // cols,),
                in_specs=[pl.BlockSpec((R, cols), lambda j, me: (0, j))],
                out_specs=pl.BlockSpec((None, R, cols), lambda j, me: (me[0], 0, j))),
            out_shape=jax.ShapeDtypeStruct((4,) + shard.shape, shard.dtype),
            compiler_params=_cparams(("parallel",)),
        )(me_idx, shard)
    _, half, C = shard.shape
    rows = min(rows, half)
    assert half % rows == 0
    return pl.pallas_call(
        body, name=name,
        grid_spec=pltpu.PrefetchScalarGridSpec(
            num_scalar_prefetch=1, grid=(2, half // rows),
            in_specs=[pl.BlockSpec((None, rows, C), lambda h, i, me: (h, i, 0))],
            out_specs=pl.BlockSpec((None, None, rows, C), lambda h, i, me: (me[0], h, i, 0))),
        out_shape=jax.ShapeDtypeStruct((4,) + shard.shape, shard.dtype),
        compiler_params=_cparams(("parallel", "parallel")),
    )(me_idx, shard)


def _pair_split(arrs, name):
    n = len(arrs)

    def body(*refs):
        ins, outs, (ssem, rsem, _) = refs[:n], refs[n:2 * n], refs[2 * n:]
        x, y, c, _ = _place()
        copies = []
        for a in range(n):
            for s in range(4):
                cp = pltpu.make_async_remote_copy(src_ref=_half(ins[a], s, 1 - c), dst_ref=outs[a].at[s],
                                                  send_sem=ssem.at[4 * a + s], recv_sem=rsem.at[4 * a + s],
                                                  device_id=(x, y, 1 - c), device_id_type=_MESH)
                cp.start()
                copies.append(cp)
        for cp in copies:
            cp.wait()

    shapes = [jax.ShapeDtypeStruct((4,) + (a.shape[2:] if a.ndim == 4 else (a.shape[1], a.shape[2] // 2)), a.dtype)
              for a in arrs]
    return _comm_call(body, arrs, shapes, 4, name)


def _pair_swap(arrs, name):
    n = len(arrs)

    def body(*refs):
        ins, outs, (ssem, rsem, _) = refs[:n], refs[n:2 * n], refs[2 * n:]
        x, y, c, _ = _place()
        copies = []
        for a in range(n):
            cp = pltpu.make_async_remote_copy(src_ref=ins[a], dst_ref=outs[a], send_sem=ssem.at[a], recv_sem=rsem.at[a],
                                              device_id=(x, y, 1 - c), device_id_type=_MESH)
            cp.start()
            copies.append(cp)
        for cp in copies:
            cp.wait()

    return _comm_call(body, arrs, [jax.ShapeDtypeStruct(a.shape, a.dtype) for a in arrs], 1, name)


def _pair_sum(full, recv, c_idx, name, rows=128):
    def add(c_ref, a_ref, b_ref, o_ref):
        o_ref[...] = (a_ref[...].astype(F32) + b_ref[...].astype(F32)).astype(o_ref.dtype)

    if full.ndim == 3:
        _, R, cols = recv.shape
        blk = lambda col: pl.BlockSpec((None, R, cols), lambda s, c: (s, 0, col(c)))
        return pl.pallas_call(
            add, name=name,
            grid_spec=pltpu.PrefetchScalarGridSpec(
                num_scalar_prefetch=1, grid=(4,),
                in_specs=[blk(lambda c: c[0]), blk(lambda c: 0)], out_specs=blk(lambda c: 0)),
            out_shape=jax.ShapeDtypeStruct(recv.shape, recv.dtype),
            compiler_params=_cparams(("parallel",)),
        )(c_idx, full, recv)
    _, _, half, C = full.shape
    rows = min(rows, half)
    nb = half // rows
    assert half % rows == 0

    def body(c_ref, a_ref, b_ref, o_ref):
        o_ref[...] = (a_ref[...].astype(F32) + b_ref[...].astype(F32)).astype(o_ref.dtype)

    return pl.pallas_call(
        body, name=name,
        grid_spec=pltpu.PrefetchScalarGridSpec(
            num_scalar_prefetch=1, grid=(4, nb),
            in_specs=[pl.BlockSpec((None, None, rows, C), lambda s, i, c: (s, c[0], i, 0)),
                      pl.BlockSpec((None, rows, C), lambda s, i, c: (s, i, 0))],
            out_specs=pl.BlockSpec((None, rows, C), lambda s, i, c: (s, i, 0))),
        out_shape=jax.ShapeDtypeStruct(recv.shape, recv.dtype),
        compiler_params=_cparams(("parallel", "parallel")),
    )(c_idx, full, recv)


_HBM = pl.BlockSpec(memory_space=pltpu.HBM)
_SEM = pl.BlockSpec(memory_space=pltpu.SEMAPHORE)
_SPLIT = dict(has_side_effects=pltpu.SideEffectType.DATAFLOW_SIDE_EFFECTING)


def _hbm(t):
    return pltpu.with_memory_space_constraint(t, pltpu.HBM)


def _ag_copy(buf_ref, ssem, rsem, j, chip, c, me):
    px, py = chip
    mine = _half(buf_ref, me, c)
    return pltpu.make_async_remote_copy(src_ref=mine, dst_ref=mine, send_sem=ssem.at[j], recv_sem=rsem.at[j],
                                        device_id=(px, py, c), device_id_type=_MESH)


def _ag_start(bufs, after, name):
    n = len(bufs)

    def body(*refs):
        buf_refs, (ssem, rsem), token = refs[:n], refs[n + 1:n + 3], refs[-1]
        x, y, c, chips = _place()
        for a in range(n):
            for j, chip in enumerate(chips):
                _ag_copy(buf_refs[a], ssem, rsem, 3 * a + j, chip, c, 2 * x + y).start()
        token[...] = jnp.zeros_like(token)

    out = pl.pallas_call(
        body, name=name,
        out_shape=(pltpu.SemaphoreType.DMA((3 * n,)), pltpu.SemaphoreType.DMA((3 * n,)),
                   *[pltpu.HBM(b.shape, b.dtype) for b in bufs], jax.ShapeDtypeStruct((8, HEAD), F32)),
        in_specs=(*[_HBM] * n, _ANY), out_specs=(_SEM, _SEM, *[_HBM] * n, pl.BlockSpec(memory_space=pltpu.VMEM)),
        input_output_aliases={a: 2 + a for a in range(n)}, compiler_params=pltpu.CompilerParams(**_SPLIT),
    )(*[_hbm(b) for b in bufs], after)
    return out[0], out[1], list(out[2:2 + n]), out[-1]


def _ag_wait(ssem, rsem, bufs, after, name):
    n = len(bufs)

    def body(*refs):
        buf_refs, ssem, rsem = refs[:n], refs[n], refs[n + 1]
        x, y, c, chips = _place()
        for a in range(n):
            for j, chip in enumerate(chips):
                cp = _ag_copy(buf_refs[a], ssem, rsem, 3 * a + j, chip, c, 2 * x + y)
                cp.wait_send()
                cp.wait_recv()

    return list(pl.pallas_call(
        body, name=name, out_shape=tuple(pltpu.HBM(b.shape, b.dtype) for b in bufs),
        in_specs=(*[_HBM] * n, _SEM, _SEM, _ANY), out_specs=tuple([_HBM] * n),
        input_output_aliases={a: a for a in range(n)}, compiler_params=pltpu.CompilerParams(**_SPLIT),
    )(*bufs, ssem, rsem, after))


def _pair_forward(bufs, name):
    n = len(bufs)

    def body(*refs):
        outs, (ssem, rsem) = refs[n:2 * n], refs[2 * n:]
        x, y, c, chips = _place()
        copies = []
        for a in range(n):
            for j, (px, py) in enumerate(chips):
                landed = _half(outs[a], 2 * px + py, c)
                cp = pltpu.make_async_remote_copy(src_ref=landed, dst_ref=landed, send_sem=ssem.at[3 * a + j],
                                                  recv_sem=rsem.at[3 * a + j], device_id=(x, y, 1 - c),
                                                  device_id_type=_MESH)
                cp.start()
                copies.append(cp)
        for cp in copies:
            cp.wait()

    return pl.pallas_call(
        body, name=name, in_specs=[_ANY] * n, out_specs=[_ANY] * n,
        out_shape=[jax.ShapeDtypeStruct(b.shape, b.dtype) for b in bufs],
        input_output_aliases={a: a for a in range(n)},
        scratch_shapes=[pltpu.SemaphoreType.DMA((3 * n,)), pltpu.SemaphoreType.DMA((3 * n,))],
    )(*bufs)


def _rs_copy(src_ref, land_ref, ssem, rsem, j, chip, c):
    px, py = chip
    return pltpu.make_async_remote_copy(src_ref=src_ref.at[2 * px + py], dst_ref=land_ref.at[j], send_sem=ssem.at[j],
                                        recv_sem=rsem.at[j], device_id=(px, py, c), device_id_type=_MESH)


def _rs_start(part, after, name):
    def body(part_ref, land_ref, after_ref, ssem, rsem, part_thru, land_thru, token):
        x, y, c, chips = _place()
        for j, chip in enumerate(chips):
            _rs_copy(part_ref, land_ref, ssem, rsem, j, chip, c).start()
        token[...] = jnp.zeros_like(token)

    land = jax.ShapeDtypeStruct((3,) + part.shape[1:], part.dtype)
    return pl.pallas_call(
        body, name=name,
        out_shape=(pltpu.SemaphoreType.DMA((3,)), pltpu.SemaphoreType.DMA((3,)), pltpu.HBM(part.shape, part.dtype),
                   pltpu.HBM(land.shape, land.dtype), jax.ShapeDtypeStruct((8, HEAD), F32)),
        in_specs=(_HBM, _HBM, _ANY), out_specs=(_SEM, _SEM, _HBM, _HBM, pl.BlockSpec(memory_space=pltpu.VMEM)),
        input_output_aliases={0: 2, 1: 3}, compiler_params=pltpu.CompilerParams(**_SPLIT),
    )(_hbm(part), _hbm(lax.empty(land.shape, land.dtype)), after)


def _rs_wait(ssem, rsem, part, land, after, name):
    def body(part_ref, land_ref, ssem, rsem, after_ref, part_out, land_out):
        x, y, c, chips = _place()
        for j, chip in enumerate(chips):
            cp = _rs_copy(part_ref, land_ref, ssem, rsem, j, chip, c)
            cp.wait_send()
            cp.wait_recv()

    return pl.pallas_call(
        body, name=name, out_shape=(pltpu.HBM(part.shape, part.dtype), pltpu.HBM(land.shape, land.dtype)),
        in_specs=(_HBM, _HBM, _SEM, _SEM, _ANY), out_specs=(_HBM, _HBM),
        input_output_aliases={0: 0, 1: 1}, compiler_params=pltpu.CompilerParams(**_SPLIT),
    )(part, land, ssem, rsem, after)


def _sum_own_recv(part, land, me_idx, name, rows=128):
    _, half, C = part.shape

    def body(me_ref, own_ref, land_ref, o_ref):
        acc = own_ref[...].astype(F32)
        for s in range(3):
            acc = acc + land_ref[s].astype(F32)
        o_ref[...] = acc

    if half % 8:
        cols = 256
        return pl.pallas_call(
            body, name=name,
            grid_spec=pltpu.PrefetchScalarGridSpec(
                num_scalar_prefetch=1, grid=(C // cols,),
                in_specs=[pl.BlockSpec((None, half, cols), lambda j, me: (me[0], 0, j)),
                          pl.BlockSpec((3, half, cols), lambda j, me: (0, 0, j))],
                out_specs=pl.BlockSpec((half, cols), lambda j, me: (0, j))),
            out_shape=jax.ShapeDtypeStruct((half, C), F32),
            compiler_params=_cparams(("parallel",)),
        )(me_idx, part, land)
    rows = min(rows, half)
    assert half % rows == 0
    return pl.pallas_call(
        body, name=name,
        grid_spec=pltpu.PrefetchScalarGridSpec(
            num_scalar_prefetch=1, grid=(half // rows,),
            in_specs=[pl.BlockSpec((None, rows, C), lambda i, me: (me[0], i, 0)),
                      pl.BlockSpec((3, rows, C), lambda i, me: (0, i, 0))],
            out_specs=pl.BlockSpec((rows, C), lambda i, me: (i, 0))),
        out_shape=jax.ShapeDtypeStruct((half, C), F32),
        compiler_params=_cparams(("parallel",)),
    )(me_idx, part, land)


def _sum_slots(arr, name, rows=128):
    k, R, C = arr.shape
    rows = min(rows, R)
    assert R % rows == 0

    def body(in_ref, o_ref):
        acc = in_ref[0].astype(F32)
        for s in range(1, k):
            acc = acc + in_ref[s].astype(F32)
        o_ref[...] = acc

    return pl.pallas_call(
        body, name=name, grid=(R // rows,),
        in_specs=[pl.BlockSpec((k, rows, C), lambda i: (0, i, 0))],
        out_specs=pl.BlockSpec((rows, C), lambda i: (i, 0)),
        out_shape=jax.ShapeDtypeStruct((R, C), F32),
        compiler_params=_cparams(("parallel",)),
    )(arr)


ADAM_LR, ADAM_B1, ADAM_B2, ADAM_EPS, ADAM_WD, ADAM_STEP = 0.001, 0.9, 0.999, 1e-08, 0.01, 10


def _adam_math(w, g, m, v):
    m = ADAM_B1 * m + (1.0 - ADAM_B1) * g
    v = ADAM_B2 * v + (1.0 - ADAM_B2) * jnp.square(g)
    m_hat = m / (1.0 - ADAM_B1 ** ADAM_STEP)
    v_hat = v / (1.0 - ADAM_B2 ** ADAM_STEP)
    delta = -ADAM_LR * (m_hat / (jnp.sqrt(v_hat) + ADAM_EPS) + ADAM_WD * w)
    return delta, m, v


def _adamw(w, m, v, layer, mine, other, c_idx, name, g_off=0, prev=None, after=(), rows=128):
    _, R, C = w.shape
    half = mine.shape[0]
    rows = min(rows, R)
    assert R % rows == 0 and g_off % rows == 0 and half % rows == 0
    nbh, b0 = half // rows, g_off // rows

    def body(c_ref, w_ref, m_ref, v_ref, mine_ref, other_ref, *rest):
        go, do, mo, vo = rest[-4:]
        in_my_half = (b0 + pl.program_id(0)) // nbh == c_ref[0]
        g = jnp.where(in_my_half, mine_ref[...], other_ref[...])
        delta, m1, v1 = _adam_math(w_ref[...], g, m_ref[...], v_ref[...])
        go[...] = g
        do[...] = delta
        mo[...] = m1
        vo[...] = v1

    blk = pl.BlockSpec((None, rows, C), lambda i, c: (layer, i, 0))
    gblk = pl.BlockSpec((rows, C), lambda i, c: ((b0 + i) % nbh, 0))
    carried = list(prev) if prev is not None else []
    return pl.pallas_call(
        body, name=name,
        grid_spec=pltpu.PrefetchScalarGridSpec(
            num_scalar_prefetch=1, grid=(R // rows,),
            in_specs=[blk] * 3 + [gblk] * 2 + [_ANY] * (len(carried) + len(after)), out_specs=[blk] * 4),
        out_shape=[jax.ShapeDtypeStruct(w.shape, F32)] * 4,
        input_output_aliases={6 + k: k for k in range(len(carried))},
        compiler_params=_cparams(("parallel",)),
    )(c_idx, w, m, v, mine, other, *carried, *after)


def _adamw_t(w, m, v, mine, other, c_idx, name, rows=513, cols=128):
    R, _, C = w.shape
    nbh = (C // 2) // cols
    assert R % rows == 0 and (C // 2) % cols == 0

    def body(c_ref, w_ref, m_ref, v_ref, mine_ref, other_ref, go, do, mo, vo):
        in_my_half = pl.program_id(1) // nbh == c_ref[0]
        g = jnp.where(in_my_half, mine_ref[...], other_ref[...])
        delta, m1, v1 = _adam_math(w_ref[...], g, m_ref[...], v_ref[...])
        go[...] = g
        do[...] = delta
        mo[...] = m1
        vo[...] = v1

    blk = pl.BlockSpec((rows, 1, cols), lambda i, j, c: (i, 0, j))
    gblk = pl.BlockSpec((rows, 1, cols), lambda i, j, c: (i, 0, j % nbh))
    return pl.pallas_call(
        body, name=name,
        grid_spec=pltpu.PrefetchScalarGridSpec(
            num_scalar_prefetch=1, grid=(R // rows, C // cols), in_specs=[blk] * 3 + [gblk] * 2,
            out_specs=[blk] * 4),
        out_shape=[jax.ShapeDtypeStruct(w.shape, F32)] * 4,
        compiler_params=_cparams(("parallel", "parallel")),
    )(c_idx, w, m, v, mine, other)


_WEIGHTS = ['norm_mix', 'norm_mlp', 'norm_ple', 'w_in_e', 'w_out_e', 'hgrn_lb', 'g_norm_a', 'conv_w', 'a_log',
            'dt_bias', 'g_norm_b', 's5_a_re', 's5_a_im', 's5_b_re', 's5_b_im', 's5_c_re', 's5_c_im', 's5_d',
            's5_log_dt', 'w_glu', 'b_glu', 'w_out_o', 'w_up', 'w_down', 'w_ple_gate', 'w_ple_proj', 'final_norm']
_INPUTS = ['x', 'p'] + _WEIGHTS + ['loss_target'] + ['m_' + n for n in _WEIGHTS] + ['v_' + n for n in _WEIGHTS]
_FAMILY = [('w_up', 0, 'col', 2048), ('w_down', 0, 'row', 2048), ('w_ple_gate', 0, 'row', 512), ('w_out_e', 0, 'row', 512),
           ('w_up', 1, 'col', 2048), ('w_down', 1, 'row', 2048), ('w_ple_gate', 1, 'row', 512), ('w_glu', 0, 'row', 512),
           ('w_out_o', 0, 'row', 512)]
_PACK = {('w_up', 0): 0, ('w_down', 0): 0, ('w_ple_gate', 0): 0, ('w_out_e', 0): 0,
         ('w_up', 1): 1, ('w_down', 1): 1, ('w_ple_gate', 1): 1, ('w_glu', 0): 1, ('w_out_o', 0): 1}
_IN_PAD = 8320
_IN_BLK = 1664


def _rms_bwd_fn(x, d_hn, d_res, g):
    _, vjp = jax.vjp(_rms, x, g)
    dx, dg = vjp(d_hn.astype(F32))
    return dx + d_res, dg


def _rms_bwd_both(x, d_hn, d_res, g):
    dx, dg = _rms_bwd_fn(x, d_hn, d_res, g)
    return dx, dx, dg


def _add_res(acc, h):
    return (acc + h,)


def _pack_rows(parts, lanes=128, mult=256):
    flat = jnp.concatenate([q.reshape(-1).astype(F32) for q in parts])
    n = flat.shape[0]
    rows = -(-n // (lanes * mult)) * mult
    return jnp.pad(flat, (0, rows * lanes - n)).reshape(rows, lanes)


def _unpack_rows(buf, shapes):
    flat, out, off = buf.reshape(-1), [], 0
    for s in shapes:
        n = math.prod(s)
        out.append(flat[off:off + n].reshape(s))
        off += n
    return out


def _step(a):
    S = a['x'].shape[1]
    x, tgt = a['x'][0], a['loss_target'][0]
    xi, yi = lax.axis_index("x"), lax.axis_index("y")
    me = 2 * xi + yi
    row = lambda t: t.reshape(1, -1)

    me_idx = me.astype(jnp.int32).reshape(1)
    c_idx = lax.axis_index("c").astype(jnp.int32).reshape(1)
    small_sh = jnp.concatenate([a['conv_w'][0].reshape(-1), a['s5_d'][0], a['b_glu'][0]]).reshape(16, 256)
    rows_first = lambda t: jnp.transpose(t, (2, 0, 1))
    shards = [None, None, rows_first(a['w_in_e']).reshape(2052, D_MODEL).astype(BF16),
              a['w_ple_proj'].reshape(512, 512).astype(BF16), small_sh]
    own = lambda i: _own_slot(shards[i] if i == 2 else shards[i].reshape(2, shards[i].shape[0] // 2, shards[i].shape[1]),
                              me_idx, name=f"own_slot{i}")
    whole = lambda g, t: g.reshape((4,) + t.shape)
    first = _ag_start([own(2), own(3), own(4)], me_idx, name="gather_first_start")
    held = [n for n in _WEIGHTS if n not in ('w_in_e', 'w_ple_proj')] + ['x']
    _, late = lax.optimization_barrier((first[3], {n: a[n] for n in held}))
    a = {**a, **late}
    x = a['x'][0]
    views, pack_of, packs, rt = {}, {}, [[], []], [0, 0]
    for nm, l, kind, r in _FAMILY:
        views[nm, l], pack_of[nm, l] = (kind, rt[_PACK[nm, l]], r), _PACK[nm, l]
        packs[_PACK[nm, l]].append(a[nm][l].astype(BF16))
        rt[_PACK[nm, l]] += r
    shards[0], shards[1] = jnp.concatenate(packs[0], axis=0), jnp.concatenate(packs[1], axis=0)
    slots = [own(0), own(1)]
    hn0 = _rowwise(_rms, [x], [row(a['norm_mix'][0])], [(D_MODEL, BF16)], name="rms_mix0")
    landed = _pair_forward(_ag_wait(first[0], first[1], first[2], hn0, name="gather_first_wait"),
                           name="gather_first_pass")
    g_in, g_pp, g_small = [whole(g, t) for g, t in zip(landed, shards[2:])]
    gw = {}
    W = lambda nm, l: dict(b=gw[pack_of[nm, l]], b_view=views[nm, l])
    sems0 = _ag_start([slots[0]], g_in, name="gather0_start")
    win_t = jnp.pad(g_in.reshape(8208, D_MODEL), ((0, _IN_PAD - 8208), (0, 0)))
    w_pp = g_pp.reshape(4, 2, 256, 512).transpose(1, 2, 0, 3).reshape(2, 256, D_MODEL)
    g_small = g_small.reshape(4, 4096)
    conv_full = g_small[:, :3072].reshape(4, 4, 768).transpose(1, 0, 2).reshape(4, 3072)
    s5_d_full = g_small[:, 3072:3584].reshape(1, D_MODEL)
    b_glu_full = g_small[:, 3584:].reshape(1, D_MODEL)

    lb, lb_vjp = jax.vjp(lambda hl: jax.nn.softmax(hl, axis=0)[0:1], a['hgrn_lb'])
    pad_h = lambda t: jnp.pad(t, ((0, 0), (0, HEAD - t.shape[1])))
    (arow, dtrow), gdn_vjp = jax.vjp(lambda al, dt: (pad_h(-jnp.exp(al)), pad_h(dt)), a['a_log'], a['dt_bias'])
    s5p, s5_vjp = jax.vjp(_s5_params, a['s5_a_re'][0], a['s5_a_im'][0], a['s5_log_dt'][0], a['s5_b_re'][0],
                          a['s5_b_im'][0], a['s5_c_re'][0], a['s5_c_im'][0])
    bre, bim, cre, cim = [t.astype(BF16) for t in s5p[:4]]
    pw_r, pw_i = _s5_powers(s5p[4], s5p[5])
    gna, gnb = a['g_norm_a'], a['g_norm_b']

    def norm_cast(h, g, dt, name):
        return _rowwise(_rms, [h], [row(g)], [(D_MODEL, dt)], name=name)

    def mlp_ple_fwd(h, l):
        hn = norm_cast(h, a['norm_mlp'][l], BF16, f"rms_mlp{l}")
        up, act = _mm(hn, **W('w_up', l), out_dtypes=(BF16, BF16), name=f"up{l}",
                      epilogue=lambda acc: (acc, jnp.square(jnp.maximum(acc, 0.0))))
        h2 = _mm(act, **W('w_down', l), epilogue=_add_res, extras=(h,), name=f"down{l}")
        hnp = norm_cast(h2, a['norm_ple'][l], BF16, f"rms_ple{l}")
        pp = _mm(a['p'][l, 0], w_pp[l], name=f"ple_proj{l}")
        h3, gpre = _mm(hnp, **W('w_ple_gate', l), extras=(h2, pp), out_dtypes=(F32, F32), name=f"ple_gate{l}",
                       bm=512, epilogue=lambda acc, h2, pp: (h2 + jax.nn.sigmoid(acc) * pp, acc))
        return h3, (h, hn, up, act, h2, hnp, pp, gpre)

    proj = _mm(hn0, win_t, dims="nt", bn=_IN_BLK, name="in_proj", after=(sems0[3],))
    o_a, st_a = _hgrn_fwd(proj, lb, gna)
    act_b = _conv_fwd(proj, conv_full)
    o_b, st_b = _gdn_fwd(act_b, proj, arow, dtrow, gnb)
    merged = jnp.concatenate([o_a, o_b], axis=1)
    landed0 = _ag_wait(sems0[0], sems0[1], sems0[2], merged, name="gather0_wait")
    gw[0] = whole(_pair_forward(landed0, name="gather0_pass")[0], shards[0])
    sems1 = _ag_start([slots[1]], gw[0], name="gather1_start")
    h1 = _mm(merged, **W('w_out_e', 0), epilogue=_add_res, extras=(x,), name="out_e", after=(sems1[3],))
    h3, saved0 = mlp_ple_fwd(h1, 0)

    u = norm_cast(h3, a['norm_mix'][1], F32, "rms_mix1")
    y, xre, xim = _s5_fwd(u, bre, bim, cre, cim, pw_r, pw_i, s5_d_full)
    actg = _rowwise(jax.nn.gelu, [y], [], [(D_MODEL, BF16)], name="gelu")
    landed1 = _ag_wait(sems1[0], sems1[1], sems1[2], actg, name="gather1_wait")
    gw[1] = whole(_pair_forward(landed1, name="gather1_pass")[0], shards[1])
    glu, z = _mm(actg, **W('w_glu', 0), extras=(actg, b_glu_full), out_dtypes=(BF16, F32), name="glu",
                 epilogue=lambda acc, act, b: (act.astype(F32) * jax.nn.sigmoid(acc + b), acc + b))
    h4 = _mm(glu, **W('w_out_o', 0), epilogue=_add_res, extras=(h3,), name="out_o")
    h6, saved1 = mlp_ple_fwd(h4, 1)

    def head(h, t, g):
        def f(h, g):
            e = _rms(h, g) - t
            return 0.5 * jnp.sum(jnp.mean(e * e, axis=-1))
        val, vjp = jax.vjp(f, h, g)
        dh, dg = vjp(jnp.ones((), F32))
        return dh, dg, jnp.full((1, HEAD), val)

    dh, d_final, loss_part = _rowwise(head, [h6, tgt], [row(a['final_norm'])], [(D_MODEL, F32)], sums=(D_MODEL, HEAD),
                                      name="loss_head")
    loss = lax.psum(loss_part[0, 0], ("x", "y", "c"))

    gb = {k: lax.empty((4, rt[k], PACK_COLS), BF16) for k in (0, 1)}

    def into(lhs, rhs, key, name, after=()):
        k = pack_of[key]
        gb[k] = _mm(lhs, rhs, dims="tn", out_into=gb[k], out_view=views[key], out_dtypes=(BF16,), name=name, after=after)

    def mlp_ple_bwd(dh, l, saved, after=()):
        h, hn, up, act, h2, hnp, pp, gpre = saved

        def gate_bwd(d, gpre, pp):
            s = jax.nn.sigmoid(gpre)
            return d * s, d * pp * s * (1.0 - s)

        d_pp, d_gp = _rowwise(gate_bwd, [dh, gpre, pp], [], [(D_MODEL, BF16), (D_MODEL, BF16)], name=f"ple_bwd{l}")
        d_wpp = _mm(a['p'][l, 0], d_pp, dims="tn", name=f"d_ple_proj{l}", after=after)
        into(hnp, d_gp, ('w_ple_gate', l), f"d_ple_gate{l}")
        d_hnp = _mm(d_gp, **W('w_ple_gate', l), dims="nt", name=f"ple_gate_t{l}")
        dh2, dh2_b, d_nple = _rowwise(_rms_bwd_both, [h2, d_hnp, dh], [row(a['norm_ple'][l])],
                                      [(D_MODEL, F32), (D_MODEL, BF16)], sums=(D_MODEL,), name=f"rms_ple_bwd{l}")
        d_up = _mm(dh2_b, **W('w_down', l), dims="nt", extras=(up,), out_dtypes=(BF16,), name=f"down_t{l}",
                   epilogue=lambda acc, up: (acc * 2.0 * jnp.maximum(up.astype(F32), 0.0),))
        into(act, dh2_b, ('w_down', l), f"d_down{l}")
        into(hn, d_up, ('w_up', l), f"d_up{l}")
        d_hn = _mm(d_up, **W('w_up', l), dims="nt", name=f"up_t{l}")
        dh1, dh1_b, d_nmlp = _rowwise(_rms_bwd_both, [h, d_hn, dh2], [row(a['norm_mlp'][l])],
                                      [(D_MODEL, F32), (D_MODEL, BF16)], sums=(D_MODEL,), name=f"rms_mlp_bwd{l}")
        return dh1, dh1_b, d_wpp, d_nple, d_nmlp

    dh4, dh4_b, d_wpp1, d_nple1, d_nmlp1 = mlp_ple_bwd(dh, 1, saved1)

    d_glu = _mm(dh4_b, **W('w_out_o', 0), dims="nt", name="out_o_t")
    into(glu, dh4_b, ('w_out_o', 0), "d_out_o")

    def glu_bwd(d, z, act):
        s, act = jax.nn.sigmoid(z), act.astype(F32)
        dz = d * act * s * (1.0 - s)
        return dz, d * s, jnp.sum(dz, axis=0, keepdims=True)

    d_z, d_actp, d_bglu = _rowwise(glu_bwd, [d_glu, z, actg], [], [(D_MODEL, BF16), (D_MODEL, F32)], sums=(D_MODEL,),
                                   name="glu_bwd")
    into(actg, d_z, ('w_glu', 0), "d_glu")

    def reduce_start(bufs, tag):
        parts = [t if t.shape[1] % 32 else t.reshape(4, 2, t.shape[1] // 2, t.shape[2]) for t in bufs]
        recv = _pair_split(parts, name=f"pair_split{tag}")
        sums = [_pair_sum(f, r, c_idx, name=f"pair_sum{tag}_{i}") for i, (f, r) in enumerate(zip(parts, recv))]
        started, prev = [], c_idx
        for i, s in enumerate(sums):
            started.append(_rs_start(s, prev, name=f"scatter{tag}_{i}_start"))
            prev = started[-1][4]
        return started

    def reduce_finish(started, after, tag):
        halves = []
        for i, (ssem, rsem, part, land, _) in enumerate(started):
            part, land = _rs_wait(ssem, rsem, part, land, after, name=f"scatter{tag}_{i}_wait")
            halves.append(_sum_own_recv(part, land, me_idx, name=f"sum_partials{tag}_{i}"))
        return list(zip(halves, _pair_swap(halves, name=f"swap_halves{tag}")))

    started1 = reduce_start([gb[1]], 1)

    def gelu_bwd(acc, dap, y):
        _, vjp = jax.vjp(jax.nn.gelu, y)
        return vjp(acc + dap)

    dy = _mm(d_z, **W('w_glu', 0), dims="nt", extras=(d_actp, y), epilogue=gelu_bwd, name="glu_t",
             after=(started1[0][4],))
    du, d_bre, d_bim, d_cre, d_cim, d_lr, d_li, d_s5d = _s5_bwd(dy, u, xre, xim, bre, bim, cre, cim,
                                                                 pw_r[:, ::-1], -pw_i[:, ::-1], s5_d_full)
    dh3, d_nmix1 = _rowwise(_rms_bwd_fn, [h3, du, dh4], [row(a['norm_mix'][1])], [(D_MODEL, F32)], sums=(D_MODEL,),
                            name="rms_mix1_bwd")
    d_are, d_aim, d_logdt, d_sbre, d_sbim, d_scre, d_scim = s5_vjp(
        (d_bre, d_bim, d_cre, d_cim, d_lr.sum(axis=1), d_li.sum(axis=1)))

    dh1, dh1_b, d_wpp0, d_nple0, d_nmlp0 = mlp_ple_bwd(dh3, 0, saved0)

    d_merged = _mm(dh1_b, **W('w_out_e', 0), dims="nt", name="out_e_t")
    into(merged, dh1_b, ('w_out_e', 0), "d_out_e")
    started0 = reduce_start([gb[0]], 0)
    dq, df, dv, dg, d_lb, d_gna = _hgrn_bwd(proj, lb, gna, st_a, d_merged, after=(started0[0][4],))
    dqb, dkb, dvb, dzb, d_tail, d_arow, d_dtrow, d_gnb = _gdn_bwd(act_b, proj, arow, dtrow, gnb, st_b, d_merged)
    d_pre, d_conv = _conv_bwd(proj, conv_full, dqb, dkb, dvb)
    d_proj = jnp.concatenate([dq, df, dv, dg, d_pre, dzb, d_tail], axis=1)
    d_win_t = _mm(d_proj, hn0, dims="tn", bm=_IN_BLK, out_dtypes=(BF16,), name="d_in_proj")
    d_hn0 = _mm(d_proj, win_t, bk=_IN_BLK, name="in_proj_t")
    grad_x, d_nmix0 = _rowwise(_rms_bwd_fn, [x, d_hn0, dh1], [row(a['norm_mix'][0])], [(D_MODEL, F32)], sums=(D_MODEL,),
                               name="rms_mix0_bwd")
    (d_hlb,) = lb_vjp(d_lb.reshape(1, 1024))
    d_alog, d_dtb = gdn_vjp((d_arow, d_dtrow))

    d_win_sh = d_win_t[:8208].reshape(4, 2052, D_MODEL)
    d_wpp_sh = jnp.stack([d_wpp0, d_wpp1]).astype(BF16).reshape(2, 256, 4, 512).transpose(2, 0, 1, 3).reshape(4, 512, 512)
    started2 = reduce_start([d_win_sh, d_wpp_sh], 2)
    small = {
        'norm_mix': jnp.concatenate([d_nmix0, d_nmix1]), 'norm_mlp': jnp.concatenate([d_nmlp0, d_nmlp1]),
        'norm_ple': jnp.concatenate([d_nple0, d_nple1]), 'hgrn_lb': d_hlb, 'g_norm_a': d_gna, 'conv_w': d_conv,
        'a_log': d_alog, 'dt_bias': d_dtb, 'g_norm_b': d_gnb, 's5_a_re': d_are, 's5_a_im': d_aim, 's5_b_re': d_sbre,
        's5_b_im': d_sbim, 's5_c_re': d_scre, 's5_c_im': d_scim, 's5_d': d_s5d, 's5_log_dt': d_logdt,
        'b_glu': d_bglu, 'final_norm': d_final}
    s_names = list(small)
    s_shapes = [tuple(small[n].shape) for n in s_names]
    mine_small = _pack_rows([small[n] for n in s_names])
    chip_small = _rowwise(lambda p, o: p + o, [mine_small, _pair_swap([mine_small], name="swap_small")[0]], [],
                          [(128, F32)], name="add_small")
    n_small = chip_small.shape[0]
    small_go = _ag_start([_own_slot(chip_small.reshape(2, n_small // 2, 128), me_idx, name="own_slot_small")],
                         started2[-1][4], name="gather_small_start")
    red = {1: reduce_finish(started1, small_go[3], 1)[0]}
    red[0] = reduce_finish(started0, small_go[3], 0)[0]
    res, order = {}, []

    def adam_layer(nm, layer, key, g_off=None):
        g_off = views[nm, layer][1] if g_off is None else g_off
        prev = res.get(nm)
        follows = () if not order or (prev is not None and order[-1] is prev[0]) else (order[-1],)
        res[nm] = tuple(_adamw(a[nm], a['m_' + nm], a['v_' + nm], layer, *red[key], c_idx, name=f"adam_{nm}{layer}",
                               g_off=g_off, prev=prev, after=follows))
        order.append(res[nm][0])

    for nm, layer in (('w_glu', 0), ('w_out_o', 0), ('w_up', 1), ('w_down', 1), ('w_ple_gate', 1)):
        adam_layer(nm, layer, 1)
    for nm in ('w_up', 'w_down', 'w_ple_gate', 'w_out_e'):
        adam_layer(nm, 0, 0)

    small_all = _pair_forward(_ag_wait(small_go[0], small_go[1], small_go[2], order[-1], name="gather_small_wait"),
                              name="gather_small_pass")[0]
    reduced = _sum_slots(small_all.reshape(4, n_small, 128), name="sum_small")
    sg = dict(zip(s_names, _unpack_rows(reduced, s_shapes)))
    sg['conv_w'] = lax.dynamic_slice_in_dim(sg['conv_w'], me * 768, 768, axis=1)
    sg['s5_d'] = lax.dynamic_slice_in_dim(sg['s5_d'], me * 512, 512, axis=1)
    sg['b_glu'] = lax.dynamic_slice_in_dim(sg['b_glu'], me * 512, 512, axis=1)
    sg = {n: sg[n].reshape(a[n].shape) for n in s_names}
    w_pack, m_pack, v_pack, g_pack = [_pack_rows([src[pre + n] for n in s_names])
                                      for src, pre in ((a, ''), (a, 'm_'), (a, 'v_'), (sg, ''))]
    sd, sm, sv = _rowwise(_adam_math, [w_pack, g_pack, m_pack, v_pack], [], [(128, F32)] * 3, name="adam_small")
    w_shapes = [tuple(a[n].shape) for n in s_names]
    res.update({n: (sg[n], d_, m_, v_) for n, d_, m_, v_ in zip(s_names, _unpack_rows(sd, w_shapes),
                                                               _unpack_rows(sm, w_shapes), _unpack_rows(sv, w_shapes))})

    red['in'], red['pp'] = reduce_finish(started2, sd, 2)
    res['w_in_e'] = tuple(jnp.transpose(o, (1, 2, 0)) for o in _adamw_t(
        rows_first(a['w_in_e']), rows_first(a['m_w_in_e']), rows_first(a['v_w_in_e']),
        *[h.reshape(2052, 1, D_MODEL // 2) for h in red['in']], c_idx, name="adam_w_in_e"))
    adam_layer('w_ple_proj', 0, 'pp', 0)
    adam_layer('w_ple_proj', 1, 'pp', 256)

    return (loss, grad_x[None], *[res[n][0] for n in _WEIGHTS], *[res[n][1] for n in _WEIGHTS],
            *[res[n][2] for n in _WEIGHTS], *[res[n][3] for n in _WEIGHTS])


def kernel(x, p, norm_mix, norm_mlp, norm_ple, w_in_e, w_out_e, hgrn_lb, g_norm_a, conv_w, a_log, dt_bias, g_norm_b, s5_a_re, s5_a_im, s5_b_re, s5_b_im, s5_c_re, s5_c_im, s5_d, s5_log_dt, w_glu, b_glu, w_out_o, w_up, w_down, w_ple_gate, w_ple_proj, final_norm, loss_target, m_norm_mix, m_norm_mlp, m_norm_ple, m_w_in_e, m_w_out_e, m_hgrn_lb, m_g_norm_a, m_conv_w, m_a_log, m_dt_bias, m_g_norm_b, m_s5_a_re, m_s5_a_im, m_s5_b_re, m_s5_b_im, m_s5_c_re, m_s5_c_im, m_s5_d, m_s5_log_dt, m_w_glu, m_b_glu, m_w_out_o, m_w_up, m_w_down, m_w_ple_gate, m_w_ple_proj, m_final_norm, v_norm_mix, v_norm_mlp, v_norm_ple, v_w_in_e, v_w_out_e, v_hgrn_lb, v_g_norm_a, v_conv_w, v_a_log, v_dt_bias, v_g_norm_b, v_s5_a_re, v_s5_a_im, v_s5_b_re, v_s5_b_im, v_s5_c_re, v_s5_c_im, v_s5_d, v_s5_log_dt, v_w_glu, v_b_glu, v_w_out_o, v_w_up, v_w_down, v_w_ple_gate, v_w_ple_proj, v_final_norm):
    args = (x, p, norm_mix, norm_mlp, norm_ple, w_in_e, w_out_e, hgrn_lb, g_norm_a, conv_w, a_log, dt_bias, g_norm_b, s5_a_re, s5_a_im, s5_b_re, s5_b_im, s5_c_re, s5_c_im, s5_d, s5_log_dt, w_glu, b_glu, w_out_o, w_up, w_down, w_ple_gate, w_ple_proj, final_norm, loss_target, m_norm_mix, m_norm_mlp, m_norm_ple, m_w_in_e, m_w_out_e, m_hgrn_lb, m_g_norm_a, m_conv_w, m_a_log, m_dt_bias, m_g_norm_b, m_s5_a_re, m_s5_a_im, m_s5_b_re, m_s5_b_im, m_s5_c_re, m_s5_c_im, m_s5_d, m_s5_log_dt, m_w_glu, m_b_glu, m_w_out_o, m_w_up, m_w_down, m_w_ple_gate, m_w_ple_proj, m_final_norm, v_norm_mix, v_norm_mlp, v_norm_ple, v_w_in_e, v_w_out_e, v_hgrn_lb, v_g_norm_a, v_conv_w, v_a_log, v_dt_bias, v_g_norm_b, v_s5_a_re, v_s5_a_im, v_s5_b_re, v_s5_b_im, v_s5_c_re, v_s5_c_im, v_s5_d, v_s5_log_dt, v_w_glu, v_b_glu, v_w_out_o, v_w_up, v_w_down, v_w_ple_gate, v_w_ple_proj, v_final_norm)
    return _step(dict(zip(_INPUTS, args)))
```

```python
import functools
import math

import jax
import jax.numpy as jnp
from jax import lax
from jax.experimental import pallas as pl
from jax.experimental.pallas import tpu as pltpu

F32 = jnp.float32
BF16 = jnp.bfloat16

D_MODEL = 2048
SEQ = 4096
NORM_EPS = 1e-6
CHUNK = 64
HEAD = 128
VMEM_LIMIT = 56 * 1024 * 1024


_NN = (((1,), (0,)), ((), ()))
_NT = (((1,), (1,)), ((), ()))
_TN = (((0,), (0,)), ((), ()))
_HI = lax.Precision.HIGHEST
_NEG = -1e30


def _cparams(sem, **kw):
    return pltpu.CompilerParams(dimension_semantics=sem, vmem_limit_bytes=VMEM_LIMIT, **kw)


PACK_COLS = 2048


def _view_shape(view):
    kind, _, r = view
    return (4 * r, PACK_COLS) if kind == "row" else (r, 4 * PACK_COLS)


def _view_spec(view, rb, cb, row_of, col_of):
    kind, off, r = view
    assert off % rb == 0 and r % rb == 0 and PACK_COLS % cb == 0, (view, rb, cb)
    if kind == "row":
        nrb = r // rb
        return pl.BlockSpec((None, rb, cb), lambda i, j, k: (row_of(i, j, k) // nrb,
                                                             off // rb + row_of(i, j, k) % nrb, col_of(i, j, k)))
    ncb = PACK_COLS // cb
    return pl.BlockSpec((None, rb, cb), lambda i, j, k: (col_of(i, j, k) // ncb,
                                                         off // rb + row_of(i, j, k), col_of(i, j, k) % ncb))


def _mm(a, b, *, dims="nn", epilogue=None, extras=(), out_dtypes=(F32,), bm=1024, bn=1024, bk=2048, name,
        b_view=None, out_into=None, out_view=None, after=()):
    b_shape = _view_shape(b_view) if b_view is not None else b.shape
    if dims == "tn":
        (K, M), (K2, N) = a.shape, b_shape
    elif dims == "nt":
        (M, K), (N, K2) = a.shape, b_shape
    else:
        (M, K), (K2, N) = a.shape, b_shape
    assert K == K2, (a.shape, b_shape, dims)
    if b_view is not None and dims == "nt":
        bn = min(bn, b_view[2])
    if b_view is not None and dims != "nt":
        bk = min(bk, b_view[2])
    if out_view is not None:
        bm = min(bm, out_view[2])
    bm, bn, bk = min(bm, M), min(bn, N), min(bk, K)
    assert M % bm == 0 and N % bn == 0 and K % bk == 0, (M, N, K, bm, bn, bk)
    nk = K // bk
    ii, jj, kk = (lambda i, j, k: i), (lambda i, j, k: j), (lambda i, j, k: k)
    if dims == "tn":
        a_spec = pl.BlockSpec((bk, bm), lambda i, j, k: (k, i))
        dn = _TN
    else:
        a_spec = pl.BlockSpec((bm, bk), lambda i, j, k: (i, k))
        dn = _NT if dims == "nt" else _NN
    if dims == "nt":
        b_spec = _view_spec(b_view, bn, bk, jj, kk) if b_view else pl.BlockSpec((bn, bk), lambda i, j, k: (j, k))
    else:
        b_spec = _view_spec(b_view, bk, bn, kk, jj) if b_view else pl.BlockSpec((bk, bn), lambda i, j, k: (k, j))
    e_specs = []
    for e in extras:
        if e.shape == (M, N):
            e_specs.append(pl.BlockSpec((bm, bn), lambda i, j, k: (i, j)))
        else:
            assert e.shape == (1, N), e.shape
            e_specs.append(pl.BlockSpec((1, bn), lambda i, j, k: (0, j)))
    ne, no = len(extras), len(out_dtypes)
    if epilogue is None:
        epilogue = lambda acc: (acc,)
    into = out_into is not None

    def body(a_ref, b_ref, *rest):
        e_refs, rest = rest[:ne], rest[ne + (1 if into else 0) + len(after):]
        o_refs = rest[:no]
        part = lax.dot_general(a_ref[...].astype(BF16), b_ref[...].astype(BF16), dn, preferred_element_type=F32)

        def finish(total):
            outs = epilogue(total, *[e[...] for e in e_refs])
            for o, v in zip(o_refs, outs):
                o[...] = v.astype(o.dtype)

        if nk == 1:
            finish(part)
            return
        acc, k = rest[no], pl.program_id(2)

        @pl.when(k == 0)
        def _():
            acc[...] = part

        @pl.when((k > 0) & (k < nk - 1))
        def _():
            acc[...] += part

        @pl.when(k == nk - 1)
        def _():
            finish(acc[...] + part)

    if into:
        assert no == 1 and _view_shape(out_view) == (M, N), (out_view, M, N)
        out_specs = [_view_spec(out_view, bm, bn, ii, jj)]
        out_shape = [jax.ShapeDtypeStruct(out_into.shape, out_into.dtype)]
        extra_in, extra_specs, alias = [out_into], [pl.BlockSpec(memory_space=pl.ANY)], {2 + ne: 0}
    else:
        out_specs = [pl.BlockSpec((bm, bn), lambda i, j, k: (i, j)) for _ in out_dtypes]
        out_shape = [jax.ShapeDtypeStruct((M, N), dt) for dt in out_dtypes]
        extra_in, extra_specs, alias = [], [], {}
    outs = pl.pallas_call(
        body, name=name,
        grid=(M // bm, N // bn, nk),
        in_specs=[a_spec, b_spec] + e_specs + extra_specs + [pl.BlockSpec(memory_space=pl.ANY)] * len(after),
        out_specs=out_specs, out_shape=out_shape,
        scratch_shapes=[pltpu.VMEM((bm, bn), F32)] if nk > 1 else [],
        input_output_aliases=alias,
        compiler_params=_cparams(("parallel", "parallel", "arbitrary")),
    )(a, b, *extras, *extra_in, *after)
    return outs[0] if no == 1 else tuple(outs)


def _rowwise(fn, tiled, bcast, outs, sums=(), *, rows=256, name):
    S = tiled[0].shape[0]
    rows = min(rows, S)
    assert S % rows == 0
    nt, nb, no, ns = len(tiled), len(bcast), len(outs), len(sums)

    def body(*refs):
        t_refs, b_refs = refs[:nt], refs[nt:nt + nb]
        o_refs, s_refs = refs[nt + nb:nt + nb + no], refs[nt + nb + no:]
        res = fn(*[r[...] for r in t_refs], *[r[...] for r in b_refs])
        res = res if isinstance(res, (tuple, list)) else (res,)
        for o, v in zip(o_refs, res[:no]):
            o[...] = v.astype(o.dtype)
        if ns:
            @pl.when(pl.program_id(0) == 0)
            def _():
                for s in s_refs:
                    s[...] = jnp.zeros_like(s)
            for s, v in zip(s_refs, res[no:]):
                s[...] += v

    res = pl.pallas_call(
        body, name=name,
        grid=(S // rows,),
        in_specs=[pl.BlockSpec((rows, t.shape[1]), lambda i: (i, 0)) for t in tiled]
        + [pl.BlockSpec(b.shape, lambda i, nd=b.ndim: (0,) * nd) for b in bcast],
        out_specs=[pl.BlockSpec((rows, c), lambda i: (i, 0)) for c, _ in outs]
        + [pl.BlockSpec((1, c), lambda i: (0, 0)) for c in sums],
        out_shape=[jax.ShapeDtypeStruct((S, c), dt) for c, dt in outs]
        + [jax.ShapeDtypeStruct((1, c), F32) for c in sums],
        compiler_params=_cparams(("arbitrary",)),
    )(*tiled, *bcast)
    return res[0] if len(res) == 1 else tuple(res)


def _rms(x, g):
    return x * lax.rsqrt(jnp.mean(x * x, axis=-1, keepdims=True) + NORM_EPS) * g


def _mxu(a, b, dn):
    if a.ndim == 3:
        ((ca,), (cb,)), _ = dn
        dn = (((ca + 1,), (cb + 1,)), ((0,), (0,)))
    return lax.dot_general(a, b, dn, preferred_element_type=F32)


def _split(a):
    hi = a.astype(BF16)
    return hi, (a - hi.astype(F32)).astype(BF16)


def _passes(a, b, dn, three):
    if not three:
        return _mxu(a.astype(BF16), b.astype(BF16), dn)
    (ah, al), (bh, bl) = _split(a), _split(b)
    return _mxu(ah, bh, dn) + (_mxu(ah, bl, dn) + _mxu(al, bh, dn))


def _dot_grads(a, b, g, dn, three):
    if dn == _NN:
        return _passes(g, b, _NT, three), _passes(a, g, _TN, three)
    if dn == _NT:
        return _passes(g, b, _NN, three), _passes(g, a, _TN, three)
    assert dn == _TN
    return _passes(b, g, _NT, three), _passes(a, g, _NN, three)


@functools.partial(jax.custom_vjp, nondiff_argnums=(2,))
def _dot(a, b, dn=_NN):
    return _passes(a, b, dn, False)


_dot.defvjp(lambda a, b, dn: (_passes(a, b, dn, False), (a, b)),
            lambda dn, res, g: _dot_grads(res[0], res[1], g, dn, False))


@functools.partial(jax.custom_vjp, nondiff_argnums=(2,))
def _dot3(a, b, dn=_NN):
    return _passes(a, b, dn, True)


_dot3.defvjp(lambda a, b, dn: (_passes(a, b, dn, True), (a, b)),
             lambda dn, res, g: _dot_grads(res[0], res[1], g, dn, True))


def _tri_times(x, dn):
    tri = _tril(x.shape[-2]).astype(BF16)
    if x.ndim == 3:
        tri = jnp.broadcast_to(tri, (x.shape[0],) + tri.shape)
    hi, lo = _split(x)
    lo2 = (x - hi.astype(F32) - lo.astype(F32)).astype(BF16)
    return _mxu(tri, hi, dn) + (_mxu(tri, lo, dn) + _mxu(tri, lo2, dn))


@jax.custom_vjp
def _cumsum_rows(x):
    return _tri_times(x, _NN)


_cumsum_rows.defvjp(lambda x: (_tri_times(x, _NN), None), lambda _, g: (_tri_times(g, _TN),))


def _iota(shape, dim):
    return lax.broadcasted_iota(jnp.int32, shape, dim)


def _tril(n, strict=False):
    r, c = _iota((n, n), 0), _iota((n, n), 1)
    return (r > c) if strict else (r >= c)


@functools.partial(jax.custom_vjp, nondiff_argnums=(1,))
def _roll_rows(x, r):
    return pltpu.roll(x, r, 0)


def _roll_rows_fwd(x, r):
    return pltpu.roll(x, r, 0), None


def _roll_rows_bwd(r, _, g):
    return (pltpu.roll(g, (g.shape[0] - r) % g.shape[0], 0),)


_roll_rows.defvjp(_roll_rows_fwd, _roll_rows_bwd)


def _head_norm_gate(o, gn, gate):
    return _rms(o, gn) * jax.nn.silu(gate)


_SUB = 16
_HPS = 8


def _hgrn_chunk(q, fpre, v, gate, lb, gna, st):
    c = q.shape[0]
    forget = lb + (1.0 - lb) * jax.nn.sigmoid(fpre)
    k = 1.0 - forget
    logf = jnp.log(forget)
    cum = _cumsum_rows(logf)
    cum_end = jnp.sum(logf, axis=0, keepdims=True)
    o = _dot(q * jnp.exp(cum), st, _NT)
    st_new = st * jnp.exp(cum_end) + _dot(v, k * jnp.exp(cum_end - cum), _TN)
    t = _iota((c, 1), 0)
    s_off = jnp.zeros((c, c), F32)
    for i in range(1, c // _SUB):
        before = t < i * _SUB
        c_i = jnp.sum(jnp.where(before, logf, 0.0), axis=0, keepdims=True)
        in_blk = (t >= i * _SUB) & (t < (i + 1) * _SUB)
        qi = jnp.where(in_blk, q * jnp.exp(jnp.minimum(cum - c_i, 0.0)), 0.0)
        ki = jnp.where(before, k * jnp.exp(jnp.minimum(c_i - cum, 0.0)), 0.0)
        s_off = s_off + _dot(qi, ki, _NT)
    o = o + _dot(s_off, v)
    tmod = t % _SUB
    for r in range(_SUB):
        kr, cr, vr = (k, cum, v) if r == 0 else (_roll_rows(k, r), _roll_rows(cum, r), _roll_rows(v, r))
        w = q * kr * jnp.exp(jnp.where(tmod >= r, cum - cr, _NEG))
        o = o + jnp.sum(w, axis=1, keepdims=True) * vr
    return _head_norm_gate(o, gna, gate), st_new


def _hgrn_fwd(proj, lb, gna):
    S = proj.shape[0]
    n_chunks, heads = S // CHUNK, 8

    def body(q_ref, f_ref, v_ref, g_ref, lb_ref, gna_ref, o_ref, st_out, st):
        n, hb = pl.program_id(0), pl.program_id(1)

        @pl.when(n == 0)
        def _():
            st[hb] = jnp.zeros((_HPS, HEAD, HEAD), F32)

        sls = [slice(hh * HEAD, (hh + 1) * HEAD) for hh in range(_HPS)]
        st0 = [st[hb, hh] for hh in range(_HPS)]
        ins = [(q_ref[:, sl], f_ref[:, sl], v_ref[:, sl], g_ref[:, sl], lb_ref[:, sl]) for sl in sls]
        gna = gna_ref[...]
        res = [_hgrn_chunk(*ins[hh], gna, st0[hh]) for hh in range(_HPS)]
        for hh, sl in enumerate(sls):
            st_out[hh] = st0[hh]
            o_ref[:, sl] = res[hh][0].astype(o_ref.dtype)
            st[hb, hh] = res[hh][1]

    wide = _HPS * HEAD
    sec = lambda s: pl.BlockSpec((CHUNK, wide), lambda n, h, s=s: (n, (8 // _HPS) * s + h))
    return pl.pallas_call(
        body, name="hgrn_fwd", grid=(n_chunks, heads // _HPS),
        in_specs=[sec(0), sec(1), sec(2), sec(3),
                  pl.BlockSpec((1, wide), lambda n, h: (0, h)), pl.BlockSpec((1, HEAD), lambda n, h: (0, 0))],
        out_specs=[pl.BlockSpec((CHUNK, wide), lambda n, h: (n, h)),
                   pl.BlockSpec((None, _HPS, HEAD, HEAD), lambda n, h: (n, h, 0, 0))],
        out_shape=[jax.ShapeDtypeStruct((S, 1024), BF16),
                   jax.ShapeDtypeStruct((n_chunks, heads, HEAD, HEAD), F32)],
        scratch_shapes=[pltpu.VMEM((heads // _HPS, _HPS, HEAD, HEAD), F32)],
        compiler_params=_cparams(("arbitrary", "arbitrary")),
    )(proj, proj, proj, proj, lb, gna)


def _hgrn_bwd(proj, lb, gna, states, d_o, after=()):
    S = proj.shape[0]
    n_chunks, heads = S // CHUNK, 8

    def body(q_ref, f_ref, v_ref, g_ref, lb_ref, gna_ref, st_ref, do_ref, *rest):
        dq_ref, df_ref, dv_ref, dg_ref, dlb_ref, dgna_ref, dst = rest[len(after):]
        n, hb = pl.program_id(0), pl.program_id(1)

        @pl.when(n == 0)
        def _():
            dst[hb] = jnp.zeros((_HPS, HEAD, HEAD), F32)
            dlb_ref[hb] = jnp.zeros((_HPS, 1, HEAD), F32)

        @pl.when((n == 0) & (hb == 0))
        def _():
            dgna_ref[...] = jnp.zeros_like(dgna_ref)

        sls = [slice(hh * HEAD, (hh + 1) * HEAD) for hh in range(_HPS)]
        gna = gna_ref[...]
        ins = [(q_ref[:, sl], f_ref[:, sl], v_ref[:, sl], g_ref[:, sl], lb_ref[:, sl], gna, st_ref[hh])
               for hh, sl in enumerate(sls)]
        cts = [(do_ref[:, sl], dst[hb, hh]) for hh, sl in enumerate(sls)]
        grads = [jax.vjp(_hgrn_chunk, *ins[hh])[1](cts[hh]) for hh in range(_HPS)]
        dgna_sum = jnp.zeros((1, HEAD), F32)
        for hh, sl in enumerate(sls):
            dq, df, dv, dg, dlb, dgna, dst0 = grads[hh]
            dq_ref[:, sl] = dq.astype(dq_ref.dtype)
            df_ref[:, sl] = df.astype(df_ref.dtype)
            dv_ref[:, sl] = dv.astype(dv_ref.dtype)
            dg_ref[:, sl] = dg.astype(dg_ref.dtype)
            dlb_ref[hb, hh] += dlb
            dgna_sum = dgna_sum + dgna
            dst[hb, hh] = dst0
        dgna_ref[...] += dgna_sum

    rev = lambda n: n_chunks - 1 - n
    wide = _HPS * HEAD
    sec = lambda s: pl.BlockSpec((CHUNK, wide), lambda n, h, s=s: (rev(n), (8 // _HPS) * s + h))
    out = pl.BlockSpec((CHUNK, wide), lambda n, h: (rev(n), h))
    return pl.pallas_call(
        body, name="hgrn_bwd", grid=(n_chunks, heads // _HPS),
        in_specs=[sec(0), sec(1), sec(2), sec(3),
                  pl.BlockSpec((1, wide), lambda n, h: (0, h)), pl.BlockSpec((1, HEAD), lambda n, h: (0, 0)),
                  pl.BlockSpec((None, _HPS, HEAD, HEAD), lambda n, h: (rev(n), h, 0, 0)),
                  pl.BlockSpec((CHUNK, wide), lambda n, h: (rev(n), h))] + [pl.BlockSpec(memory_space=pl.ANY)] * len(after),
        out_specs=[out, out, out, out,
                   pl.BlockSpec((heads // _HPS, _HPS, 1, HEAD), lambda n, h: (0, 0, 0, 0)),
                   pl.BlockSpec((1, HEAD), lambda n, h: (0, 0))],
        out_shape=[jax.ShapeDtypeStruct((S, 1024), BF16)] * 4
        + [jax.ShapeDtypeStruct((heads // _HPS, _HPS, 1, HEAD), F32), jax.ShapeDtypeStruct((1, HEAD), F32)],
        scratch_shapes=[pltpu.VMEM((heads // _HPS, _HPS, HEAD, HEAD), F32)],
        compiler_params=_cparams(("arbitrary", "arbitrary")),
    )(proj, proj, proj, proj, lb, gna, states, d_o, *after)


def _l2n(t):
    return t * lax.rsqrt(jnp.sum(t * t, axis=-1, keepdims=True) + NORM_EPS)


def _unit_lower_inverse(low):
    c = low.shape[-1]
    inv = (_iota((c, c), 0) == _iota((c, c), 1)).astype(F32) - low
    p = low
    span = 2
    while span < c:
        p = _dot3(p, p)
        inv = inv + _dot3(inv, p)
        span *= 2
    return inv


def _gdn_chunk(qc, kc, v, z, tail, h0, arow, dtrow, gnb, st):
    nb, c = qc.shape[0], qc.shape[1]
    head = h0 + _iota((nb, c, HEAD), 0)
    lane = _iota((nb, c, HEAD), 2)
    la_all = arow * jax.nn.softplus(tail + dtrow)
    la = jnp.sum(jnp.where(lane == head, la_all[None], 0.0), axis=2, keepdims=True)
    beta = jnp.sum(jnp.where(lane == head + 8, jax.nn.sigmoid(tail)[None], 0.0), axis=2, keepdims=True)
    la_b = jnp.broadcast_to(la, (nb, c, HEAD))
    cum = _cumsum_rows(la_b)
    cmat = _cumsum_rows(jnp.broadcast_to(la, (nb, c, c)))
    cum_end = jnp.sum(la_b, axis=1, keepdims=True)
    decay = jnp.exp(jnp.where(_tril(c), cmat - jnp.swapaxes(cmat, 1, 2), _NEG))
    q = _l2n(qc) * (HEAD ** -0.5)
    k = _l2n(kc)
    k_beta = k * beta
    low = jnp.where(_tril(c, strict=True), _dot(k_beta, k, _NT) * decay, 0.0)
    inv = _unit_lower_inverse(low)
    u, w = _dot(inv, v * beta), _dot(inv, k_beta * jnp.exp(cum))
    intra = _dot(q, k, _NT) * decay
    v_new = u - _dot(w, st, _NT)
    o = _dot(q * jnp.exp(cum), st, _NT) + _dot(intra, v_new)
    st_new = st * jnp.exp(cum_end) + _dot(v_new, k * jnp.exp(cum_end - cum), _TN)
    return _head_norm_gate(o, gnb, z), st_new


def _gdn_specs(n_of):
    wide = _HPS * HEAD
    qkv = lambda s: pl.BlockSpec((CHUNK, wide), lambda n, h, s=s: (n_of(n), (8 // _HPS) * s + h))
    return [qkv(0), qkv(1), qkv(2),
            pl.BlockSpec((CHUNK, wide), lambda n, h: (n_of(n), 56 // _HPS + h)),
            pl.BlockSpec((CHUNK, HEAD), lambda n, h: (n_of(n), 64)),
            pl.BlockSpec((1, HEAD), lambda n, h: (0, 0)), pl.BlockSpec((1, HEAD), lambda n, h: (0, 0)),
            pl.BlockSpec((1, HEAD), lambda n, h: (0, 0))]


def _gdn_fwd(act, proj, arow, dtrow, gnb):
    S = act.shape[0]
    n_chunks, heads = S // CHUNK, 8

    def body(q_ref, k_ref, v_ref, z_ref, t_ref, a_ref, dt_ref, gnb_ref, o_ref, st_out, st):
        n, hb = pl.program_id(0), pl.program_id(1)

        @pl.when(n == 0)
        def _():
            st[hb] = jnp.zeros((_HPS, HEAD, HEAD), F32)

        sls = [slice(hh * HEAD, (hh + 1) * HEAD) for hh in range(_HPS)]
        heads_of = lambda ref: jnp.stack([ref[:, sl] for sl in sls])
        st0 = st[hb]
        o, st1 = _gdn_chunk(heads_of(q_ref), heads_of(k_ref), heads_of(v_ref), heads_of(z_ref), t_ref[...], hb * _HPS,
                            a_ref[...], dt_ref[...], gnb_ref[...], st0)
        st_out[...] = st0
        st[hb] = st1
        for hh, sl in enumerate(sls):
            o_ref[:, sl] = o[hh].astype(o_ref.dtype)

    return pl.pallas_call(
        body, name="gdn_fwd", grid=(n_chunks, heads // _HPS),
        in_specs=_gdn_specs(lambda n: n),
        out_specs=[pl.BlockSpec((CHUNK, _HPS * HEAD), lambda n, h: (n, h)),
                   pl.BlockSpec((None, _HPS, HEAD, HEAD), lambda n, h: (n, h, 0, 0))],
        out_shape=[jax.ShapeDtypeStruct((S, 1024), BF16),
                   jax.ShapeDtypeStruct((n_chunks, heads, HEAD, HEAD), F32)],
        scratch_shapes=[pltpu.VMEM((heads // _HPS, _HPS, HEAD, HEAD), F32)],
        compiler_params=_cparams(("arbitrary", "arbitrary")),
    )(act, act, act, proj, proj, arow, dtrow, gnb)


def _gdn_bwd(act, proj, arow, dtrow, gnb, states, d_o):
    S = act.shape[0]
    n_chunks, heads = S // CHUNK, 8

    def body(q_ref, k_ref, v_ref, z_ref, t_ref, a_ref, dt_ref, gnb_ref, st_ref, do_ref,
             dq_ref, dk_ref, dv_ref, dz_ref, dt_out, da_ref, ddt_ref, dgnb_ref, dst, dtail_acc):
        n, hb = pl.program_id(0), pl.program_id(1)
        n_hb = heads // _HPS

        @pl.when(n == 0)
        def _():
            dst[hb] = jnp.zeros((_HPS, HEAD, HEAD), F32)

        @pl.when((n == 0) & (hb == 0))
        def _():
            da_ref[...] = jnp.zeros_like(da_ref)
            ddt_ref[...] = jnp.zeros_like(ddt_ref)
            dgnb_ref[...] = jnp.zeros_like(dgnb_ref)

        sls = [slice(hh * HEAD, (hh + 1) * HEAD) for hh in range(_HPS)]
        heads_of = lambda ref: jnp.stack([ref[:, sl] for sl in sls])
        fn = lambda qc, kc, v, z, tail, arow, dtrow, gnb, st: _gdn_chunk(qc, kc, v, z, tail, hb * _HPS, arow, dtrow,
                                                                        gnb, st)
        _, vjp = jax.vjp(fn, heads_of(q_ref), heads_of(k_ref), heads_of(v_ref), heads_of(z_ref), t_ref[...],
                         a_ref[...], dt_ref[...], gnb_ref[...], st_ref[...])
        dq, dk, dv, dz, dtail_sum, da_sum, ddt_sum, dgnb_sum, dst0 = vjp((heads_of(do_ref), dst[hb]))
        dst[hb] = dst0
        for hh, sl in enumerate(sls):
            dq_ref[:, sl] = dq[hh]
            dk_ref[:, sl] = dk[hh]
            dv_ref[:, sl] = dv[hh]
            dz_ref[:, sl] = dz[hh].astype(dz_ref.dtype)

        @pl.when(hb == 0)
        def _():
            dtail_acc[...] = dtail_sum

        @pl.when(hb > 0)
        def _():
            dtail_acc[...] += dtail_sum

        @pl.when(hb == n_hb - 1)
        def _():
            dt_out[...] = dtail_acc[...].astype(dt_out.dtype)

        da_ref[...] += da_sum
        ddt_ref[...] += ddt_sum
        dgnb_ref[...] += dgnb_sum

    rev = lambda n: n_chunks - 1 - n
    out = pl.BlockSpec((CHUNK, _HPS * HEAD), lambda n, h: (rev(n), h))
    row = pl.BlockSpec((1, HEAD), lambda n, h: (0, 0))
    return pl.pallas_call(
        body, name="gdn_bwd", grid=(n_chunks, heads // _HPS),
        in_specs=_gdn_specs(rev)
        + [pl.BlockSpec((None, _HPS, HEAD, HEAD), lambda n, h: (rev(n), h, 0, 0)),
           pl.BlockSpec((CHUNK, _HPS * HEAD), lambda n, h: (rev(n), 8 // _HPS + h))],
        out_specs=[out, out, out, out, pl.BlockSpec((CHUNK, HEAD), lambda n, h: (rev(n), 0)), row, row, row],
        out_shape=[jax.ShapeDtypeStruct((S, 1024), F32)] * 3
        + [jax.ShapeDtypeStruct((S, 1024), BF16), jax.ShapeDtypeStruct((S, HEAD), BF16)]
        + [jax.ShapeDtypeStruct((1, HEAD), F32)] * 3,
        scratch_shapes=[pltpu.VMEM((heads // _HPS, _HPS, HEAD, HEAD), F32), pltpu.VMEM((CHUNK, HEAD), F32)],
        compiler_params=_cparams(("arbitrary", "arbitrary")),
    )(act, act, act, proj, proj, arow, dtrow, gnb, states, d_o)


def _conv_silu(x, w):
    t = _iota((x.shape[0], 1), 0)
    tap = _iota(w.shape, 0)
    y = jnp.zeros_like(x)
    for r in range(4):
        w_r = jnp.sum(jnp.where(tap == 3 - r, w, 0.0), axis=0, keepdims=True)
        y = y + (x if r == 0 else jnp.where(t >= r, _roll_rows(x, r), 0.0)) * w_r
    return jax.nn.silu(y)


_CONV_COLS = 128


def _conv_fwd(proj, conv_w):
    S = proj.shape[0]
    nb = 3072 // _CONV_COLS
    off = 4096 // _CONV_COLS

    def body(x_ref, w_ref, o_ref):
        o_ref[...] = _conv_silu(x_ref[...], w_ref[...])

    return pl.pallas_call(
        body, name="conv_fwd", grid=(nb,),
        in_specs=[pl.BlockSpec((S, _CONV_COLS), lambda j: (0, off + j)), pl.BlockSpec((4, _CONV_COLS), lambda j: (0, j))],
        out_specs=pl.BlockSpec((S, _CONV_COLS), lambda j: (0, j)),
        out_shape=jax.ShapeDtypeStruct((S, 3072), F32),
        compiler_params=_cparams(("parallel",)),
    )(proj, conv_w)


def _conv_bwd(proj, conv_w, dq, dk, dv):
    S = proj.shape[0]
    nb = 3072 // _CONV_COLS
    off = 4096 // _CONV_COLS
    per = 1024 // _CONV_COLS

    def body(x_ref, w_ref, dq_ref, dk_ref, dv_ref, dx_ref, dw_ref):
        j = pl.program_id(0)
        _, vjp = jax.vjp(_conv_silu, x_ref[...], w_ref[...])
        d = jnp.where(j < per, dq_ref[...], jnp.where(j < 2 * per, dk_ref[...], dv_ref[...]))
        dx, dw = vjp(d)
        dx_ref[...] = dx.astype(dx_ref.dtype)
        dw_ref[...] = dw

    dsp = lambda s: pl.BlockSpec((S, _CONV_COLS), lambda j, s=s: (0, jnp.clip(j - s * per, 0, per - 1)))
    return pl.pallas_call(
        body, name="conv_bwd", grid=(nb,),
        in_specs=[pl.BlockSpec((S, _CONV_COLS), lambda j: (0, off + j)), pl.BlockSpec((4, _CONV_COLS), lambda j: (0, j)),
                  dsp(0), dsp(1), dsp(2)],
        out_specs=[pl.BlockSpec((S, _CONV_COLS), lambda j: (0, j)), pl.BlockSpec((4, _CONV_COLS), lambda j: (0, j))],
        out_shape=[jax.ShapeDtypeStruct((S, 3072), BF16), jax.ShapeDtypeStruct((4, 3072), F32)],
        compiler_params=_cparams(("parallel",)),
    )(proj, conv_w, dq, dk, dv)


_S5_T = 512
_S5_L = 512
_S5_NB = 16
_S5_U = 4
_S5_UB = 3


def _s5_tile(xr, xi, pw_r, pw_i, cr, ci, reverse):
    t8 = _iota((8, 1), 0)
    for sh in (1, 2, 4):
        row = (8 - sh) if reverse else (sh - 1)
        ar, ai = pw_r[row:row + 1, :], pw_i[row:row + 1, :]
        if reverse:
            keep, amt = t8 < 8 - sh, 8 - sh
        else:
            keep, amt = t8 >= sh, sh
        sr = jnp.where(keep, pltpu.roll(xr, amt, 0), 0.0)
        si = jnp.where(keep, pltpu.roll(xi, amt, 0), 0.0)
        xr, xi = xr + ar * sr - ai * si, xi + ar * si + ai * sr
    xr, xi = xr + pw_r * cr - pw_i * ci, xi + pw_r * ci + pw_i * cr
    return xr, xi


def _s5_fwd(u, bre, bim, cre, cim, pw_r, pw_i, dskip):
    S = u.shape[0]
    T = min(_S5_T, S)
    nt = S // T

    def body(u_ref, bre_ref, bim_ref, cre_ref, cim_ref, pr_ref, pi_ref, d_ref, y_ref, xr_ref, xi_ref,
             bu_r, bu_i, car_r, car_i):
        @pl.when(pl.program_id(1) == 0)
        def _():
            car_r[...] = jnp.zeros_like(car_r)
            car_i[...] = jnp.zeros_like(car_i)

        uu = u_ref[...]
        bu_r[...] = _dot(uu, bre_ref[...])
        bu_i[...] = _dot(uu, bim_ref[...])
        pw_r, pw_i = pr_ref[...], pi_ref[...]

        def tiles(i, carry):
            ins = []
            for k in range(_S5_U):
                r0 = pl.multiple_of((i * _S5_U + k) * 8, 8)
                ins.append((r0, bu_r[pl.ds(r0, 8), :], bu_i[pl.ds(r0, 8), :]))
            outs = []
            for r0, br, bi in ins:
                xr, xi = _s5_tile(br, bi, pw_r, pw_i, carry[0], carry[1], False)
                carry = (xr[7:8, :], xi[7:8, :])
                outs.append((r0, xr, xi))
            for r0, xr, xi in outs:
                xr_ref[pl.ds(r0, 8), :] = xr
                xi_ref[pl.ds(r0, 8), :] = xi
            return carry

        cr, ci = lax.fori_loop(0, T // (8 * _S5_U), tiles, (car_r[...], car_i[...]))
        car_r[...] = cr
        car_i[...] = ci
        y_ref[...] = _dot(xr_ref[...], cre_ref[...]) - _dot(xi_ref[...], cim_ref[...]) + d_ref[...] * uu

    blk3 = lambda a, b: pl.BlockSpec((None, a, b), lambda j, t: (j, 0, 0))
    return pl.pallas_call(
        body, name="s5_fwd", grid=(_S5_NB, nt),
        in_specs=[pl.BlockSpec((T, HEAD), lambda j, t: (t, j)),
                  blk3(HEAD, _S5_L), blk3(HEAD, _S5_L), blk3(_S5_L, HEAD), blk3(_S5_L, HEAD),
                  blk3(8, _S5_L), blk3(8, _S5_L), pl.BlockSpec((1, HEAD), lambda j, t: (0, j))],
        out_specs=[pl.BlockSpec((T, HEAD), lambda j, t: (t, j)),
                   pl.BlockSpec((T, _S5_L), lambda j, t: (t, j)), pl.BlockSpec((T, _S5_L), lambda j, t: (t, j))],
        out_shape=[jax.ShapeDtypeStruct((S, D_MODEL), F32),
                   jax.ShapeDtypeStruct((S, _S5_NB * _S5_L), F32), jax.ShapeDtypeStruct((S, _S5_NB * _S5_L), F32)],
        scratch_shapes=[pltpu.VMEM((T, _S5_L), F32), pltpu.VMEM((T, _S5_L), F32),
                        pltpu.VMEM((1, _S5_L), F32), pltpu.VMEM((1, _S5_L), F32)],
        compiler_params=_cparams(("parallel", "arbitrary")),
    )(u, bre, bim, cre, cim, pw_r, pw_i, dskip)


def _s5_bwd(dy, u, xre, xim, bre, bim, cre, cim, qw_r, qw_i, dskip):
    S = u.shape[0]
    T = min(_S5_T, S)
    nt = S // T
    nt8 = T // 8

    def body(dy_ref, u_ref, xr_ref, xi_ref, xpr_ref, xpi_ref, bre_ref, bim_ref, cre_ref, cim_ref, qr_ref, qi_ref,
             d_ref, du_ref, dbr_ref, dbi_ref, dcr_ref, dci_ref, dlr_ref, dli_ref, dd_ref,
             g_r, g_i, car_r, car_i):
        t = pl.program_id(1)

        @pl.when(t == 0)
        def _():
            car_r[...] = jnp.zeros_like(car_r)
            car_i[...] = jnp.zeros_like(car_i)
            for r in (dbr_ref, dbi_ref, dcr_ref, dci_ref, dlr_ref, dli_ref, dd_ref):
                r[...] = jnp.zeros_like(r)

        dyy, uu = dy_ref[...], u_ref[...]
        g_r[...] = _dot(dyy, cre_ref[...], _NT)
        g_i[...] = -_dot(dyy, cim_ref[...], _NT)
        qw_r, qw_i = qr_ref[...], qi_ref[...]
        t8 = _iota((8, 1), 0)
        first = t == nt - 1

        def load(r0, prev_r, prev_i):
            rows = pl.ds(r0, 8)
            return r0, g_r[rows, :], g_i[rows, :], xr_ref[rows, :], xi_ref[rows, :], prev_r, prev_i

        def run(loaded, carry, acc):
            done = []
            for r0, dr, di, xr, xi, prev_r, prev_i in loaded:
                gr, gi = _s5_tile(dr, di, qw_r, qw_i, carry[0], carry[1], True)
                carry = (gr[0:1, :], gi[0:1, :])
                xpr = jnp.where(t8 >= 1, pltpu.roll(xr, 1, 0), prev_r)
                xpi = jnp.where(t8 >= 1, pltpu.roll(xi, 1, 0), prev_i)
                acc = (acc[0] + gr * xpr + gi * xpi, acc[1] + gi * xpr - gr * xpi)
                done.append((r0, gr, gi))
            for r0, gr, gi in done:
                g_r[pl.ds(r0, 8), :] = gr
                g_i[pl.ds(r0, 8), :] = gi
            return carry, acc

        def step(ii, state):
            loaded = []
            for k in range(_S5_UB):
                idx = nt8 - 1 - (ii * _S5_UB + k)
                r0 = pl.multiple_of(idx * 8, 8)
                p0 = pl.multiple_of((idx - 1) * 8, 8)
                loaded.append(load(r0, xr_ref[pl.ds(p0, 8), :][7:8, :], xi_ref[pl.ds(p0, 8), :][7:8, :]))
            return run(loaded, *state)

        zero = jnp.zeros((8, _S5_L), F32)
        assert (nt8 - 1) % _S5_UB == 0
        state = lax.fori_loop(0, (nt8 - 1) // _S5_UB, step, ((car_r[...], car_i[...]), (zero, zero)))
        prev_r = jnp.where(first, 0.0, xpr_ref[...][7:8, :])
        prev_i = jnp.where(first, 0.0, xpi_ref[...][7:8, :])
        (cr, ci), (ar, ai) = run([load(0, prev_r, prev_i)], *state)
        car_r[...] = cr
        car_i[...] = ci
        dlr_ref[...] += ar
        dli_ref[...] += ai
        gr, gi = g_r[...], g_i[...]
        du_ref[...] = _dot(gr, bre_ref[...], _NT) + _dot(gi, bim_ref[...], _NT) + d_ref[...] * dyy
        dbr_ref[...] += _dot(uu, gr, _TN)
        dbi_ref[...] += _dot(uu, gi, _TN)
        dcr_ref[...] += _dot(xr_ref[...], dyy, _TN)
        dci_ref[...] -= _dot(xi_ref[...], dyy, _TN)
        dd_ref[...] += jnp.sum(dyy * uu, axis=0, keepdims=True)

    rev = lambda t: nt - 1 - t
    blk3 = lambda a, b: pl.BlockSpec((None, a, b), lambda j, t: (j, 0, 0))
    tl = pl.BlockSpec((T, HEAD), lambda j, t: (rev(t), j))
    xs = pl.BlockSpec((T, _S5_L), lambda j, t: (rev(t), j))
    xp = pl.BlockSpec((8, _S5_L), lambda j, t: (jnp.maximum(rev(t) * nt8 - 1, 0), j))
    return pl.pallas_call(
        body, name="s5_bwd", grid=(_S5_NB, nt),
        in_specs=[tl, tl, xs, xs, xp, xp, blk3(HEAD, _S5_L), blk3(HEAD, _S5_L), blk3(_S5_L, HEAD), blk3(_S5_L, HEAD),
                  blk3(8, _S5_L), blk3(8, _S5_L), pl.BlockSpec((1, HEAD), lambda j, t: (0, j))],
        out_specs=[tl, blk3(HEAD, _S5_L), blk3(HEAD, _S5_L), blk3(_S5_L, HEAD), blk3(_S5_L, HEAD),
                   blk3(8, _S5_L), blk3(8, _S5_L), pl.BlockSpec((1, HEAD), lambda j, t: (0, j))],
        out_shape=[jax.ShapeDtypeStruct((S, D_MODEL), F32),
                   jax.ShapeDtypeStruct((_S5_NB, HEAD, _S5_L), F32), jax.ShapeDtypeStruct((_S5_NB, HEAD, _S5_L), F32),
                   jax.ShapeDtypeStruct((_S5_NB, _S5_L, HEAD), F32), jax.ShapeDtypeStruct((_S5_NB, _S5_L, HEAD), F32),
                   jax.ShapeDtypeStruct((_S5_NB, 8, _S5_L), F32), jax.ShapeDtypeStruct((_S5_NB, 8, _S5_L), F32),
                   jax.ShapeDtypeStruct((1, D_MODEL), F32)],
        scratch_shapes=[pltpu.VMEM((T, _S5_L), F32), pltpu.VMEM((T, _S5_L), F32),
                        pltpu.VMEM((1, _S5_L), F32), pltpu.VMEM((1, _S5_L), F32)],
        compiler_params=_cparams(("parallel", "arbitrary")),
    )(dy, u, xre, xim, xre, xim, bre, bim, cre, cim, qw_r, qw_i, dskip)


def _s5_params(a_re, a_im, log_dt, b_re, b_im, c_re, c_im):
    step = jnp.exp(log_dt)[:, None]
    mag = jnp.exp(a_re * step)
    lr, li = mag * jnp.cos(a_im * step), mag * jnp.sin(a_im * step)
    den = a_re * a_re + a_im * a_im
    nr, ni = lr - 1.0, li
    kr, ki = (nr * a_re + ni * a_im) / den, (ni * a_re - nr * a_im) / den
    bbr = kr[..., None] * b_re - ki[..., None] * b_im
    bbi = kr[..., None] * b_im + ki[..., None] * b_re
    eye = jnp.eye(8, dtype=F32)

    def blk_b(bb):
        t = bb.reshape(_S5_NB, 8, 64, 16).transpose(0, 1, 3, 2)
        return (t[:, :, :, None, :] * eye[None, :, None, :, None]).reshape(_S5_NB, HEAD, _S5_L)

    def blk_c(cc):
        t = cc.reshape(_S5_NB, 8, 16, 64).transpose(0, 1, 3, 2)
        return (t[:, :, :, None, :] * eye[None, :, None, :, None]).reshape(_S5_NB, _S5_L, HEAD)

    return (blk_b(bbr), blk_b(bbi), blk_c(c_re), blk_c(c_im),
            lr.reshape(_S5_NB, _S5_L), li.reshape(_S5_NB, _S5_L))


def _s5_powers(lr, li):
    pr, pi = [lr], [li]
    for _ in range(7):
        pr, pi = pr + [pr[-1] * lr - pi[-1] * li], pi + [pr[-1] * li + pi[-1] * lr]
    return jnp.stack(pr, axis=1), jnp.stack(pi, axis=1)


_MESH = pl.DeviceIdType.MESH
_ANY = pl.BlockSpec(memory_space=pl.ANY)


def _place():
    x, y, c = lax.axis_index("x"), lax.axis_index("y"), lax.axis_index("c")
    return x, y, c, [(1 - x, y), (x, 1 - y), (1 - x, 1 - y)]


def _comm_call(body, arrs, out_shapes, n_remote, name):
    n = len(arrs)
    return pl.pallas_call(
        body, name=name,
        in_specs=[_ANY] * n, out_specs=[_ANY] * n, out_shape=out_shapes,
        scratch_shapes=[pltpu.SemaphoreType.DMA((n * n_remote,)), pltpu.SemaphoreType.DMA((n * n_remote,)),
                        pltpu.SemaphoreType.DMA((n,))],
    )(*arrs)


def _half(ref, slot, h):
    if len(ref.shape) == 4:
        return ref.at[slot, h]
    cols = ref.shape[2] // 2
    return ref.at[slot, :, pl.ds(pl.multiple_of(h * cols, 128), cols)]


def _own_slot(shard, me_idx, name, rows=256, cols=512):
    def body(me_ref, in_ref, o_ref):
        o_ref[...] = in_ref[...]

    if shard.ndim == 2:
        R, C = shard.shape
        return pl.pallas_call(
            body, name=name,
            grid_spec=pltpu.PrefetchScalarGridSpec(
                num_scalar_prefetch=1, grid=(C // cols,),
                in_specs=[pl.BlockSpec((R, cols), lambda j, me: (0, j))],
                out_specs=pl.BlockSpec((None, R, cols), lambda j, me: (me[0], 0, j))),
            out_shape=jax.ShapeDtypeStruct((4,) + shard.shape, shard.dtype),
            compiler_params=_cparams(("parallel",)),
        )(me_idx, shard)
    _, half, C = shard.shape
    rows = min(rows, half)
    assert half % rows == 0
    return pl.pallas_call(
        body, name=name,
        grid_spec=pltpu.PrefetchScalarGridSpec(
            num_scalar_prefetch=1, grid=(2, half // rows),
            in_specs=[pl.BlockSpec((None, rows, C), lambda h, i, me: (h, i, 0))],
            out_specs=pl.BlockSpec((None, None, rows, C), lambda h, i, me: (me[0], h, i, 0))),
        out_shape=jax.ShapeDtypeStruct((4,) + shard.shape, shard.dtype),
        compiler_params=_cparams(("parallel", "parallel")),
    )(me_idx, shard)


def _own_rows(w, layer, buf, me_idx, row_off, after, name, rows=256):
    _, r, C = w.shape
    carried = [] if isinstance(buf, tuple) else [buf]
    shape = buf if isinstance(buf, tuple) else buf.shape
    assert r % rows == 0 and row_off % rows == 0

    def body(me_ref, in_ref, *rest):
        rest[-1][...] = in_ref[...].astype(BF16)

    return pl.pallas_call(
        body, name=name,
        grid_spec=pltpu.PrefetchScalarGridSpec(
            num_scalar_prefetch=1, grid=(r // rows,),
            in_specs=[pl.BlockSpec((None, rows, C), lambda i, me: (layer, i, 0))] + [_ANY] * (len(carried) + len(after)),
            out_specs=pl.BlockSpec((None, rows, C), lambda i, me: (me[0], row_off // rows + i, 0))),
        out_shape=jax.ShapeDtypeStruct(shape, BF16),
        input_output_aliases={2: 0} if carried else {},
        compiler_params=_cparams(("parallel",)),
    )(me_idx, w, *carried, *after)


def _pair_split(arrs, name):
    n = len(arrs)

    def body(*refs):
        ins, outs, (ssem, rsem, _) = refs[:n], refs[n:2 * n], refs[2 * n:]
        x, y, c, _ = _place()
        copies = []
        for a in range(n):
            for s in range(4):
                cp = pltpu.make_async_remote_copy(src_ref=_half(ins[a], s, 1 - c), dst_ref=outs[a].at[s],
                                                  send_sem=ssem.at[4 * a + s], recv_sem=rsem.at[4 * a + s],
                                                  device_id=(x, y, 1 - c), device_id_type=_MESH)
                cp.start()
                copies.append(cp)
        for cp in copies:
            cp.wait()

    shapes = [jax.ShapeDtypeStruct((4,) + (a.shape[2:] if a.ndim == 4 else (a.shape[1], a.shape[2] // 2)), a.dtype)
              for a in arrs]
    return _comm_call(body, arrs, shapes, 4, name)


def _pair_swap(arrs, name):
    n = len(arrs)

    def body(*refs):
        ins, outs, (ssem, rsem, _) = refs[:n], refs[n:2 * n], refs[2 * n:]
        x, y, c, _ = _place()
        copies = []
        for a in range(n):
            cp = pltpu.make_async_remote_copy(src_ref=ins[a], dst_ref=outs[a], send_sem=ssem.at[a], recv_sem=rsem.at[a],
                                              device_id=(x, y, 1 - c), device_id_type=_MESH)
            cp.start()
            copies.append(cp)
        for cp in copies:
            cp.wait()

    return _comm_call(body, arrs, [jax.ShapeDtypeStruct(a.shape, a.dtype) for a in arrs], 1, name)


def _pair_sum(full, recv, c_idx, name, rows=128):
    def add(c_ref, a_ref, b_ref, o_ref):
        o_ref[...] = (a_ref[...].astype(F32) + b_ref[...].astype(F32)).astype(o_ref.dtype)

    if full.ndim == 3:
        _, R, cols = recv.shape
        blk = lambda col: pl.BlockSpec((None, R, cols), lambda s, c: (s, 0, col(c)))
        return pl.pallas_call(
            add, name=name,
            grid_spec=pltpu.PrefetchScalarGridSpec(
                num_scalar_prefetch=1, grid=(4,),
                in_specs=[blk(lambda c: c[0]), blk(lambda c: 0)], out_specs=blk(lambda c: 0)),
            out_shape=jax.ShapeDtypeStruct(recv.shape, recv.dtype),
            compiler_params=_cparams(("parallel",)),
        )(c_idx, full, recv)
    _, _, half, C = full.shape
    rows = min(rows, half)
    nb = half // rows
    assert half % rows == 0

    def body(c_ref, a_ref, b_ref, o_ref):
        o_ref[...] = (a_ref[...].astype(F32) + b_ref[...].astype(F32)).astype(o_ref.dtype)

    return pl.pallas_call(
        body, name=name,
        grid_spec=pltpu.PrefetchScalarGridSpec(
            num_scalar_prefetch=1, grid=(4, nb),
            in_specs=[pl.BlockSpec((None, None, rows, C), lambda s, i, c: (s, c[0], i, 0)),
                      pl.BlockSpec((None, rows, C), lambda s, i, c: (s, i, 0))],
            out_specs=pl.BlockSpec((None, rows, C), lambda s, i, c: (s, i, 0))),
        out_shape=jax.ShapeDtypeStruct(recv.shape, recv.dtype),
        compiler_params=_cparams(("parallel", "parallel")),
    )(c_idx, full, recv)


_HBM = pl.BlockSpec(memory_space=pltpu.HBM)
_SEM = pl.BlockSpec(memory_space=pltpu.SEMAPHORE)
_SPLIT = dict(has_side_effects=pltpu.SideEffectType.DATAFLOW_SIDE_EFFECTING)


def _hbm(t):
    return pltpu.with_memory_space_constraint(t, pltpu.HBM)


def _ag_copy(buf_ref, ssem, rsem, j, chip, c, me):
    px, py = chip
    mine = _half(buf_ref, me, c)
    return pltpu.make_async_remote_copy(src_ref=mine, dst_ref=mine, send_sem=ssem.at[j], recv_sem=rsem.at[j],
                                        device_id=(px, py, c), device_id_type=_MESH)


def _ag_start(bufs, after, name):
    n = len(bufs)

    def body(*refs):
        buf_refs, (ssem, rsem), token = refs[:n], refs[n + 1:n + 3], refs[-1]
        x, y, c, chips = _place()
        for a in range(n):
            for j, chip in enumerate(chips):
                _ag_copy(buf_refs[a], ssem, rsem, 3 * a + j, chip, c, 2 * x + y).start()
        token[...] = jnp.zeros_like(token)

    out = pl.pallas_call(
        body, name=name,
        out_shape=(pltpu.SemaphoreType.DMA((3 * n,)), pltpu.SemaphoreType.DMA((3 * n,)),
                   *[pltpu.HBM(b.shape, b.dtype) for b in bufs], jax.ShapeDtypeStruct((8, HEAD), F32)),
        in_specs=(*[_HBM] * n, _ANY), out_specs=(_SEM, _SEM, *[_HBM] * n, pl.BlockSpec(memory_space=pltpu.VMEM)),
        input_output_aliases={a: 2 + a for a in range(n)}, compiler_params=pltpu.CompilerParams(**_SPLIT),
    )(*[_hbm(b) for b in bufs], after)
    return out[0], out[1], list(out[2:2 + n]), out[-1]


def _ag_wait(ssem, rsem, bufs, after, name):
    n = len(bufs)

    def body(*refs):
        buf_refs, ssem, rsem = refs[:n], refs[n], refs[n + 1]
        x, y, c, chips = _place()
        for a in range(n):
            for j, chip in enumerate(chips):
                cp = _ag_copy(buf_refs[a], ssem, rsem, 3 * a + j, chip, c, 2 * x + y)
                cp.wait_send()
                cp.wait_recv()

    return list(pl.pallas_call(
        body, name=name, out_shape=tuple(pltpu.HBM(b.shape, b.dtype) for b in bufs),
        in_specs=(*[_HBM] * n, _SEM, _SEM, _ANY), out_specs=tuple([_HBM] * n),
        input_output_aliases={a: a for a in range(n)}, compiler_params=pltpu.CompilerParams(**_SPLIT),
    )(*bufs, ssem, rsem, after))


def _pair_forward(bufs, name):
    n = len(bufs)

    def body(*refs):
        outs, (ssem, rsem) = refs[n:2 * n], refs[2 * n:]
        x, y, c, chips = _place()
        copies = []
        for a in range(n):
            for j, (px, py) in enumerate(chips):
                landed = _half(outs[a], 2 * px + py, c)
                cp = pltpu.make_async_remote_copy(src_ref=landed, dst_ref=landed, send_sem=ssem.at[3 * a + j],
                                                  recv_sem=rsem.at[3 * a + j], device_id=(x, y, 1 - c),
                                                  device_id_type=_MESH)
                cp.start()
                copies.append(cp)
        for cp in copies:
            cp.wait()

    return pl.pallas_call(
        body, name=name, in_specs=[_ANY] * n, out_specs=[_ANY] * n,
        out_shape=[jax.ShapeDtypeStruct(b.shape, b.dtype) for b in bufs],
        input_output_aliases={a: a for a in range(n)},
        scratch_shapes=[pltpu.SemaphoreType.DMA((3 * n,)), pltpu.SemaphoreType.DMA((3 * n,))],
    )(*bufs)


def _rs_copy(src_ref, land_ref, ssem, rsem, j, chip, c):
    px, py = chip
    return pltpu.make_async_remote_copy(src_ref=src_ref.at[2 * px + py], dst_ref=land_ref.at[j], send_sem=ssem.at[j],
                                        recv_sem=rsem.at[j], device_id=(px, py, c), device_id_type=_MESH)


def _rs_start(part, after, name):
    def body(part_ref, land_ref, after_ref, ssem, rsem, part_thru, land_thru, token):
        x, y, c, chips = _place()
        for j, chip in enumerate(chips):
            _rs_copy(part_ref, land_ref, ssem, rsem, j, chip, c).start()
        token[...] = jnp.zeros_like(token)

    land = jax.ShapeDtypeStruct((3,) + part.shape[1:], part.dtype)
    return pl.pallas_call(
        body, name=name,
        out_shape=(pltpu.SemaphoreType.DMA((3,)), pltpu.SemaphoreType.DMA((3,)), pltpu.HBM(part.shape, part.dtype),
                   pltpu.HBM(land.shape, land.dtype), jax.ShapeDtypeStruct((8, HEAD), F32)),
        in_specs=(_HBM, _HBM, _ANY), out_specs=(_SEM, _SEM, _HBM, _HBM, pl.BlockSpec(memory_space=pltpu.VMEM)),
        input_output_aliases={0: 2, 1: 3}, compiler_params=pltpu.CompilerParams(**_SPLIT),
    )(_hbm(part), _hbm(lax.empty(land.shape, land.dtype)), after)


def _rs_wait(ssem, rsem, part, land, after, name):
    def body(part_ref, land_ref, ssem, rsem, after_ref, part_out, land_out):
        x, y, c, chips = _place()
        for j, chip in enumerate(chips):
            cp = _rs_copy(part_ref, land_ref, ssem, rsem, j, chip, c)
            cp.wait_send()
            cp.wait_recv()

    return pl.pallas_call(
        body, name=name, out_shape=(pltpu.HBM(part.shape, part.dtype), pltpu.HBM(land.shape, land.dtype)),
        in_specs=(_HBM, _HBM, _SEM, _SEM, _ANY), out_specs=(_HBM, _HBM),
        input_output_aliases={0: 0, 1: 1}, compiler_params=pltpu.CompilerParams(**_SPLIT),
    )(part, land, ssem, rsem, after)


def _sum_own_recv(part, land, me_idx, name, rows=128):
    _, half, C = part.shape

    def body(me_ref, own_ref, land_ref, o_ref):
        acc = own_ref[...].astype(F32)
        for s in range(3):
            acc = acc + land_ref[s].astype(F32)
        o_ref[...] = acc

    if half % 8:
        cols = 256
        return pl.pallas_call(
            body, name=name,
            grid_spec=pltpu.PrefetchScalarGridSpec(
                num_scalar_prefetch=1, grid=(C // cols,),
                in_specs=[pl.BlockSpec((None, half, cols), lambda j, me: (me[0], 0, j)),
                          pl.BlockSpec((3, half, cols), lambda j, me: (0, 0, j))],
                out_specs=pl.BlockSpec((half, cols), lambda j, me: (0, j))),
            out_shape=jax.ShapeDtypeStruct((half, C), F32),
            compiler_params=_cparams(("parallel",)),
        )(me_idx, part, land)
    rows = min(rows, half)
    assert half % rows == 0
    return pl.pallas_call(
        body, name=name,
        grid_spec=pltpu.PrefetchScalarGridSpec(
            num_scalar_prefetch=1, grid=(half // rows,),
            in_specs=[pl.BlockSpec((None, rows, C), lambda i, me: (me[0], i, 0)),
                      pl.BlockSpec((3, rows, C), lambda i, me: (0, i, 0))],
            out_specs=pl.BlockSpec((rows, C), lambda i, me: (i, 0))),
        out_shape=jax.ShapeDtypeStruct((half, C), F32),
        compiler_params=_cparams(("parallel",)),
    )(me_idx, part, land)


def _sum_slots(arr, name, rows=128):
    k, R, C = arr.shape
    rows = min(rows, R)
    assert R % rows == 0

    def body(in_ref, o_ref):
        acc = in_ref[0].astype(F32)
        for s in range(1, k):
            acc = acc + in_ref[s].astype(F32)
        o_ref[...] = acc

    return pl.pallas_call(
        body, name=name, grid=(R // rows,),
        in_specs=[pl.BlockSpec((k, rows, C), lambda i: (0, i, 0))],
        out_specs=pl.BlockSpec((rows, C), lambda i: (i, 0)),
        out_shape=jax.ShapeDtypeStruct((R, C), F32),
        compiler_params=_cparams(("parallel",)),
    )(arr)


ADAM_LR, ADAM_B1, ADAM_B2, ADAM_EPS, ADAM_WD, ADAM_STEP = 0.001, 0.9, 0.999, 1e-08, 0.01, 10


def _adam_math(w, g, m, v):
    m = ADAM_B1 * m + (1.0 - ADAM_B1) * g
    v = ADAM_B2 * v + (1.0 - ADAM_B2) * jnp.square(g)
    m_hat = m / (1.0 - ADAM_B1 ** ADAM_STEP)
    v_hat = v / (1.0 - ADAM_B2 ** ADAM_STEP)
    delta = -ADAM_LR * (m_hat / (jnp.sqrt(v_hat) + ADAM_EPS) + ADAM_WD * w)
    return delta, m, v


def _adamw(w, m, v, layer, mine, other, c_idx, name, g_off=0, prev=None, after=(), rows=128):
    _, R, C = w.shape
    half = mine.shape[0]
    rows = min(rows, R)
    assert R % rows == 0 and g_off % rows == 0 and half % rows == 0
    nbh, b0 = half // rows, g_off // rows

    def body(c_ref, w_ref, m_ref, v_ref, mine_ref, other_ref, *rest):
        go, do, mo, vo = rest[-4:]
        in_my_half = (b0 + pl.program_id(0)) // nbh == c_ref[0]
        g = jnp.where(in_my_half, mine_ref[...], other_ref[...])
        delta, m1, v1 = _adam_math(w_ref[...], g, m_ref[...], v_ref[...])
        go[...] = g
        do[...] = delta
        mo[...] = m1
        vo[...] = v1

    blk = pl.BlockSpec((None, rows, C), lambda i, c: (layer, i, 0))
    gblk = pl.BlockSpec((rows, C), lambda i, c: ((b0 + i) % nbh, 0))
    carried = list(prev) if prev is not None else []
    return pl.pallas_call(
        body, name=name,
        grid_spec=pltpu.PrefetchScalarGridSpec(
            num_scalar_prefetch=1, grid=(R // rows,),
            in_specs=[blk] * 3 + [gblk] * 2 + [_ANY] * (len(carried) + len(after)), out_specs=[blk] * 4),
        out_shape=[jax.ShapeDtypeStruct(w.shape, F32)] * 4,
        input_output_aliases={6 + k: k for k in range(len(carried))},
        compiler_params=_cparams(("parallel",)),
    )(c_idx, w, m, v, mine, other, *carried, *after)


def _adamw_t(w, m, v, mine, other, c_idx, name, rows=513, cols=128):
    R, _, C = w.shape
    nbh = (C // 2) // cols
    assert R % rows == 0 and (C // 2) % cols == 0

    def body(c_ref, w_ref, m_ref, v_ref, mine_ref, other_ref, go, do, mo, vo):
        in_my_half = pl.program_id(1) // nbh == c_ref[0]
        g = jnp.where(in_my_half, mine_ref[...], other_ref[...])
        delta, m1, v1 = _adam_math(w_ref[...], g, m_ref[...], v_ref[...])
        go[...] = g
        do[...] = delta
        mo[...] = m1
        vo[...] = v1

    blk = pl.BlockSpec((rows, 1, cols), lambda i, j, c: (i, 0, j))
    gblk = pl.BlockSpec((rows, 1, cols), lambda i, j, c: (i, 0, j % nbh))
    return pl.pallas_call(
        body, name=name,
        grid_spec=pltpu.PrefetchScalarGridSpec(
            num_scalar_prefetch=1, grid=(R // rows, C // cols), in_specs=[blk] * 3 + [gblk] * 2,
            out_specs=[blk] * 4),
        out_shape=[jax.ShapeDtypeStruct(w.shape, F32)] * 4,
        compiler_params=_cparams(("parallel", "parallel")),
    )(c_idx, w, m, v, mine, other)


_WEIGHTS = ['norm_mix', 'norm_mlp', 'norm_ple', 'w_in_e', 'w_out_e', 'hgrn_lb', 'g_norm_a', 'conv_w', 'a_log',
            'dt_bias', 'g_norm_b', 's5_a_re', 's5_a_im', 's5_b_re', 's5_b_im', 's5_c_re', 's5_c_im', 's5_d',
            's5_log_dt', 'w_glu', 'b_glu', 'w_out_o', 'w_up', 'w_down', 'w_ple_gate', 'w_ple_proj', 'final_norm']
_INPUTS = ['x', 'p'] + _WEIGHTS + ['loss_target'] + ['m_' + n for n in _WEIGHTS] + ['v_' + n for n in _WEIGHTS]
_FAMILY = [('w_up', 0, 'col', 2048), ('w_down', 0, 'row', 2048), ('w_ple_gate', 0, 'row', 512), ('w_out_e', 0, 'row', 512),
           ('w_up', 1, 'col', 2048), ('w_down', 1, 'row', 2048), ('w_ple_gate', 1, 'row', 512), ('w_glu', 0, 'row', 512),
           ('w_out_o', 0, 'row', 512)]
_PACK = {('w_up', 0): 0, ('w_down', 0): 0, ('w_ple_gate', 0): 0, ('w_out_e', 0): 0,
         ('w_up', 1): 1, ('w_down', 1): 1, ('w_ple_gate', 1): 1, ('w_glu', 0): 1, ('w_out_o', 0): 1}
_IN_PAD = 8320
_IN_BLK = 1664


def _rms_bwd_fn(x, d_hn, d_res, g):
    _, vjp = jax.vjp(_rms, x, g)
    dx, dg = vjp(d_hn.astype(F32))
    return dx + d_res, dg


def _rms_bwd_both(x, d_hn, d_res, g):
    dx, dg = _rms_bwd_fn(x, d_hn, d_res, g)
    return dx, dx, dg


def _add_res(acc, h):
    return (acc + h,)


def _pack_rows(parts, lanes=128, mult=256):
    flat = jnp.concatenate([q.reshape(-1).astype(F32) for q in parts])
    n = flat.shape[0]
    rows = -(-n // (lanes * mult)) * mult
    return jnp.pad(flat, (0, rows * lanes - n)).reshape(rows, lanes)


def _unpack_rows(buf, shapes):
    flat, out, off = buf.reshape(-1), [], 0
    for s in shapes:
        n = math.prod(s)
        out.append(flat[off:off + n].reshape(s))
        off += n
    return out


def _step(a):
    S = a['x'].shape[1]
    x, tgt = a['x'][0], a['loss_target'][0]
    xi, yi = lax.axis_index("x"), lax.axis_index("y")
    me = 2 * xi + yi
    row = lambda t: t.reshape(1, -1)

    me_idx = me.astype(jnp.int32).reshape(1)
    c_idx = lax.axis_index("c").astype(jnp.int32).reshape(1)
    small_sh = jnp.concatenate([a['conv_w'][0].reshape(-1), a['s5_d'][0], a['b_glu'][0]]).reshape(16, 256)
    rows_first = lambda t: jnp.transpose(t, (2, 0, 1))
    shards = [None, None, rows_first(a['w_in_e']).reshape(2052, D_MODEL).astype(BF16),
              a['w_ple_proj'].reshape(512, 512).astype(BF16), small_sh]
    own = lambda i: _own_slot(shards[i] if i == 2 else shards[i].reshape(2, shards[i].shape[0] // 2, shards[i].shape[1]),
                              me_idx, name=f"own_slot{i}")
    whole = lambda g, t: g.reshape((4,) + t.shape)
    first = _ag_start([own(2), own(3), own(4)], me_idx, name="gather_first_start")
    views, pack_of, rt = {}, {}, [0, 0]
    for nm, l, kind, r in _FAMILY:
        views[nm, l], pack_of[nm, l] = (kind, rt[_PACK[nm, l]], r), _PACK[nm, l]
        rt[_PACK[nm, l]] += r
    bufs = [(4, rt[0], PACK_COLS), (4, rt[1], PACK_COLS)]
    for nm, l, kind, r in _FAMILY:
        k = pack_of[nm, l]
        bufs[k] = _own_rows(a[nm], l, bufs[k], me_idx, views[nm, l][1], (first[3],), name=f"own_{nm}{l}")
    shards[0], shards[1] = [jax.ShapeDtypeStruct((rt[k], PACK_COLS), BF16) for k in (0, 1)]
    slots = [bufs[k].reshape(4, 2, rt[k] // 2, PACK_COLS) for k in (0, 1)]
    hn0 = _rowwise(_rms, [x], [row(a['norm_mix'][0])], [(D_MODEL, BF16)], name="rms_mix0")
    landed = _pair_forward(_ag_wait(first[0], first[1], first[2], hn0, name="gather_first_wait"),
                           name="gather_first_pass")
    g_in, g_pp, g_small = [whole(g, t) for g, t in zip(landed, shards[2:])]
    gw = {}
    W = lambda nm, l: dict(b=gw[pack_of[nm, l]], b_view=views[nm, l])
    sems0 = _ag_start([slots[0]], g_in, name="gather0_start")
    win_t = jnp.pad(g_in.reshape(8208, D_MODEL), ((0, _IN_PAD - 8208), (0, 0)))
    w_pp = g_pp.reshape(4, 2, 256, 512).transpose(1, 2, 0, 3).reshape(2, 256, D_MODEL)
    g_small = g_small.reshape(4, 4096)
    conv_full = g_small[:, :3072].reshape(4, 4, 768).transpose(1, 0, 2).reshape(4, 3072)
    s5_d_full = g_small[:, 3072:3584].reshape(1, D_MODEL)
    b_glu_full = g_small[:, 3584:].reshape(1, D_MODEL)

    lb, lb_vjp = jax.vjp(lambda hl: jax.nn.softmax(hl, axis=0)[0:1], a['hgrn_lb'])
    pad_h = lambda t: jnp.pad(t, ((0, 0), (0, HEAD - t.shape[1])))
    (arow, dtrow), gdn_vjp = jax.vjp(lambda al, dt: (pad_h(-jnp.exp(al)), pad_h(dt)), a['a_log'], a['dt_bias'])
    s5p, s5_vjp = jax.vjp(_s5_params, a['s5_a_re'][0], a['s5_a_im'][0], a['s5_log_dt'][0], a['s5_b_re'][0],
                          a['s5_b_im'][0], a['s5_c_re'][0], a['s5_c_im'][0])
    bre, bim, cre, cim = [t.astype(BF16) for t in s5p[:4]]
    pw_r, pw_i = _s5_powers(s5p[4], s5p[5])
    gna, gnb = a['g_norm_a'], a['g_norm_b']

    def norm_cast(h, g, dt, name):
        return _rowwise(_rms, [h], [row(g)], [(D_MODEL, dt)], name=name)

    def mlp_ple_fwd(h, l):
        hn = norm_cast(h, a['norm_mlp'][l], BF16, f"rms_mlp{l}")
        up, act = _mm(hn, **W('w_up', l), out_dtypes=(BF16, BF16), name=f"up{l}",
                      epilogue=lambda acc: (acc, jnp.square(jnp.maximum(acc, 0.0))))
        h2 = _mm(act, **W('w_down', l), epilogue=_add_res, extras=(h,), name=f"down{l}")
        hnp = norm_cast(h2, a['norm_ple'][l], BF16, f"rms_ple{l}")
        pp = _mm(a['p'][l, 0], w_pp[l], name=f"ple_proj{l}")
        h3, gpre = _mm(hnp, **W('w_ple_gate', l), extras=(h2, pp), out_dtypes=(F32, F32), name=f"ple_gate{l}",
                       bm=512, epilogue=lambda acc, h2, pp: (h2 + jax.nn.sigmoid(acc) * pp, acc))
        return h3, (h, hn, up, act, h2, hnp, pp, gpre)

    proj = _mm(hn0, win_t, dims="nt", bn=_IN_BLK, name="in_proj", after=(sems0[3],))
    o_a, st_a = _hgrn_fwd(proj, lb, gna)
    act_b = _conv_fwd(proj, conv_full)
    o_b, st_b = _gdn_fwd(act_b, proj, arow, dtrow, gnb)
    merged = jnp.concatenate([o_a, o_b], axis=1)
    landed0 = _ag_wait(sems0[0], sems0[1], sems0[2], merged, name="gather0_wait")
    gw[0] = whole(_pair_forward(landed0, name="gather0_pass")[0], shards[0])
    sems1 = _ag_start([slots[1]], gw[0], name="gather1_start")
    h1 = _mm(merged, **W('w_out_e', 0), epilogue=_add_res, extras=(x,), name="out_e", after=(sems1[3],))
    h3, saved0 = mlp_ple_fwd(h1, 0)

    u = norm_cast(h3, a['norm_mix'][1], F32, "rms_mix1")
    y, xre, xim = _s5_fwd(u, bre, bim, cre, cim, pw_r, pw_i, s5_d_full)
    actg = _rowwise(jax.nn.gelu, [y], [], [(D_MODEL, BF16)], name="gelu")
    landed1 = _ag_wait(sems1[0], sems1[1], sems1[2], actg, name="gather1_wait")
    gw[1] = whole(_pair_forward(landed1, name="gather1_pass")[0], shards[1])
    glu, z = _mm(actg, **W('w_glu', 0), extras=(actg, b_glu_full), out_dtypes=(BF16, F32), name="glu",
                 epilogue=lambda acc, act, b: (act.astype(F32) * jax.nn.sigmoid(acc + b), acc + b))
    h4 = _mm(glu, **W('w_out_o', 0), epilogue=_add_res, extras=(h3,), name="out_o")
    h6, saved1 = mlp_ple_fwd(h4, 1)

    def head(h, t, g):
        def f(h, g):
            e = _rms(h, g) - t
            return 0.5 * jnp.sum(jnp.mean(e * e, axis=-1))
        val, vjp = jax.vjp(f, h, g)
        dh, dg = vjp(jnp.ones((), F32))
        return dh, dg, jnp.full((1, HEAD), val)

    dh, d_final, loss_part = _rowwise(head, [h6, tgt], [row(a['final_norm'])], [(D_MODEL, F32)], sums=(D_MODEL, HEAD),
                                      name="loss_head")
    loss = lax.psum(loss_part[0, 0], ("x", "y", "c"))

    gb = {k: lax.empty((4, rt[k], PACK_COLS), BF16) for k in (0, 1)}

    def into(lhs, rhs, key, name, after=()):
        k = pack_of[key]
        gb[k] = _mm(lhs, rhs, dims="tn", out_into=gb[k], out_view=views[key], out_dtypes=(BF16,), name=name, after=after)

    def mlp_ple_bwd(dh, l, saved, after=()):
        h, hn, up, act, h2, hnp, pp, gpre = saved

        def gate_bwd(d, gpre, pp):
            s = jax.nn.sigmoid(gpre)
            return d * s, d * pp * s * (1.0 - s)

        d_pp, d_gp = _rowwise(gate_bwd, [dh, gpre, pp], [], [(D_MODEL, BF16), (D_MODEL, BF16)], name=f"ple_bwd{l}")
        d_wpp = _mm(a['p'][l, 0], d_pp, dims="tn", name=f"d_ple_proj{l}", after=after)
        into(hnp, d_gp, ('w_ple_gate', l), f"d_ple_gate{l}")
        d_hnp = _mm(d_gp, **W('w_ple_gate', l), dims="nt", name=f"ple_gate_t{l}")
        dh2, dh2_b, d_nple = _rowwise(_rms_bwd_both, [h2, d_hnp, dh], [row(a['norm_ple'][l])],
                                      [(D_MODEL, F32), (D_MODEL, BF16)], sums=(D_MODEL,), name=f"rms_ple_bwd{l}")
        d_up = _mm(dh2_b, **W('w_down', l), dims="nt", extras=(up,), out_dtypes=(BF16,), name=f"down_t{l}",
                   epilogue=lambda acc, up: (acc * 2.0 * jnp.maximum(up.astype(F32), 0.0),))
        into(act, dh2_b, ('w_down', l), f"d_down{l}")
        into(hn, d_up, ('w_up', l), f"d_up{l}")
        d_hn = _mm(d_up, **W('w_up', l), dims="nt", name=f"up_t{l}")
        dh1, dh1_b, d_nmlp = _rowwise(_rms_bwd_both, [h, d_hn, dh2], [row(a['norm_mlp'][l])],
                                      [(D_MODEL, F32), (D_MODEL, BF16)], sums=(D_MODEL,), name=f"rms_mlp_bwd{l}")
        return dh1, dh1_b, d_wpp, d_nple, d_nmlp

    dh4, dh4_b, d_wpp1, d_nple1, d_nmlp1 = mlp_ple_bwd(dh, 1, saved1)

    d_glu = _mm(dh4_b, **W('w_out_o', 0), dims="nt", name="out_o_t")
    into(glu, dh4_b, ('w_out_o', 0), "d_out_o")

    def glu_bwd(d, z, act):
        s, act = jax.nn.sigmoid(z), act.astype(F32)
        dz = d * act * s * (1.0 - s)
        return dz, d * s, jnp.sum(dz, axis=0, keepdims=True)

    d_z, d_actp, d_bglu = _rowwise(glu_bwd, [d_glu, z, actg], [], [(D_MODEL, BF16), (D_MODEL, F32)], sums=(D_MODEL,),
                                   name="glu_bwd")
    into(actg, d_z, ('w_glu', 0), "d_glu")

    def reduce_start(bufs, tag):
        parts = [t if t.shape[1] % 32 else t.reshape(4, 2, t.shape[1] // 2, t.shape[2]) for t in bufs]
        recv = _pair_split(parts, name=f"pair_split{tag}")
        sums = [_pair_sum(f, r, c_idx, name=f"pair_sum{tag}_{i}") for i, (f, r) in enumerate(zip(parts, recv))]
        started, prev = [], c_idx
        for i, s in enumerate(sums):
            started.append(_rs_start(s, prev, name=f"scatter{tag}_{i}_start"))
            prev = started[-1][4]
        return started

    def reduce_finish(started, after, tag):
        halves = []
        for i, (ssem, rsem, part, land, _) in enumerate(started):
            part, land = _rs_wait(ssem, rsem, part, land, after, name=f"scatter{tag}_{i}_wait")
            halves.append(_sum_own_recv(part, land, me_idx, name=f"sum_partials{tag}_{i}"))
        return list(zip(halves, _pair_swap(halves, name=f"swap_halves{tag}")))

    started1 = reduce_start([gb[1]], 1)

    def gelu_bwd(acc, dap, y):
        _, vjp = jax.vjp(jax.nn.gelu, y)
        return vjp(acc + dap)

    dy = _mm(d_z, **W('w_glu', 0), dims="nt", extras=(d_actp, y), epilogue=gelu_bwd, name="glu_t",
             after=(started1[0][4],))
    du, d_bre, d_bim, d_cre, d_cim, d_lr, d_li, d_s5d = _s5_bwd(dy, u, xre, xim, bre, bim, cre, cim,
                                                                 pw_r[:, ::-1], -pw_i[:, ::-1], s5_d_full)
    dh3, d_nmix1 = _rowwise(_rms_bwd_fn, [h3, du, dh4], [row(a['norm_mix'][1])], [(D_MODEL, F32)], sums=(D_MODEL,),
                            name="rms_mix1_bwd")
    d_are, d_aim, d_logdt, d_sbre, d_sbim, d_scre, d_scim = s5_vjp(
        (d_bre, d_bim, d_cre, d_cim, d_lr.sum(axis=1), d_li.sum(axis=1)))

    dh1, dh1_b, d_wpp0, d_nple0, d_nmlp0 = mlp_ple_bwd(dh3, 0, saved0)

    d_merged = _mm(dh1_b, **W('w_out_e', 0), dims="nt", name="out_e_t")
    into(merged, dh1_b, ('w_out_e', 0), "d_out_e")
    started0 = reduce_start([gb[0]], 0)
    dq, df, dv, dg, d_lb, d_gna = _hgrn_bwd(proj, lb, gna, st_a, d_merged, after=(started0[0][4],))
    dqb, dkb, dvb, dzb, d_tail, d_arow, d_dtrow, d_gnb = _gdn_bwd(act_b, proj, arow, dtrow, gnb, st_b, d_merged)
    d_pre, d_conv = _conv_bwd(proj, conv_full, dqb, dkb, dvb)
    d_proj = jnp.concatenate([dq, df, dv, dg, d_pre, dzb, d_tail], axis=1)
    d_win_t = _mm(d_proj, hn0, dims="tn", bm=_IN_BLK, out_dtypes=(BF16,), name="d_in_proj")
    d_hn0 = _mm(d_proj, win_t, bk=_IN_BLK, name="in_proj_t")
    grad_x, d_nmix0 = _rowwise(_rms_bwd_fn, [x, d_hn0, dh1], [row(a['norm_mix'][0])], [(D_MODEL, F32)], sums=(D_MODEL,),
                               name="rms_mix0_bwd")
    (d_hlb,) = lb_vjp(d_lb.reshape(1, 1024))
    d_alog, d_dtb = gdn_vjp((d_arow, d_dtrow))

    d_win_sh = d_win_t[:8208].reshape(4, 2052, D_MODEL)
    d_wpp_sh = jnp.stack([d_wpp0, d_wpp1]).astype(BF16).reshape(2, 256, 4, 512).transpose(2, 0, 1, 3).reshape(4, 512, 512)
    started2 = reduce_start([d_win_sh, d_wpp_sh], 2)
    small = {
        'norm_mix': jnp.concatenate([d_nmix0, d_nmix1]), 'norm_mlp': jnp.concatenate([d_nmlp0, d_nmlp1]),
        'norm_ple': jnp.concatenate([d_nple0, d_nple1]), 'hgrn_lb': d_hlb, 'g_norm_a': d_gna, 'conv_w': d_conv,
        'a_log': d_alog, 'dt_bias': d_dtb, 'g_norm_b': d_gnb, 's5_a_re': d_are, 's5_a_im': d_aim, 's5_b_re': d_sbre,
        's5_b_im': d_sbim, 's5_c_re': d_scre, 's5_c_im': d_scim, 's5_d': d_s5d, 's5_log_dt': d_logdt,
        'b_glu': d_bglu, 'final_norm': d_final}
    s_names = list(small)
    s_shapes = [tuple(small[n].shape) for n in s_names]
    mine_small = _pack_rows([small[n] for n in s_names])
    chip_small = _rowwise(lambda p, o: p + o, [mine_small, _pair_swap([mine_small], name="swap_small")[0]], [],
                          [(128, F32)], name="add_small")
    n_small = chip_small.shape[0]
    small_go = _ag_start([_own_slot(chip_small.reshape(2, n_small // 2, 128), me_idx, name="own_slot_small")],
                         started2[-1][4], name="gather_small_start")
    red = {1: reduce_finish(started1, small_go[3], 1)[0]}
    red[0] = reduce_finish(started0, small_go[3], 0)[0]
    res, order = {}, []

    def adam_layer(nm, layer, key, g_off=None):
        g_off = views[nm, layer][1] if g_off is None else g_off
        prev = res.get(nm)
        follows = () if not order or (prev is not None and order[-1] is prev[0]) else (order[-1],)
        res[nm] = tuple(_adamw(a[nm], a['m_' + nm], a['v_' + nm], layer, *red[key], c_idx, name=f"adam_{nm}{layer}",
                               g_off=g_off, prev=prev, after=follows))
        order.append(res[nm][0])

    for nm, layer in (('w_glu', 0), ('w_out_o', 0), ('w_up', 1), ('w_down', 1), ('w_ple_gate', 1)):
        adam_layer(nm, layer, 1)
    for nm in ('w_up', 'w_down', 'w_ple_gate', 'w_out_e'):
        adam_layer(nm, 0, 0)

    small_all = _pair_forward(_ag_wait(small_go[0], small_go[1], small_go[2], order[-1], name="gather_small_wait"),
                              name="gather_small_pass")[0]
    reduced = _sum_slots(small_all.reshape(4, n_small, 128), name="sum_small")
    sg = dict(zip(s_names, _unpack_rows(reduced, s_shapes)))
    sg['conv_w'] = lax.dynamic_slice_in_dim(sg['conv_w'], me * 768, 768, axis=1)
    sg['s5_d'] = lax.dynamic_slice_in_dim(sg['s5_d'], me * 512, 512, axis=1)
    sg['b_glu'] = lax.dynamic_slice_in_dim(sg['b_glu'], me * 512, 512, axis=1)
    sg = {n: sg[n].reshape(a[n].shape) for n in s_names}
    w_pack, m_pack, v_pack, g_pack = [_pack_rows([src[pre + n] for n in s_names])
                                      for src, pre in ((a, ''), (a, 'm_'), (a, 'v_'), (sg, ''))]
    sd, sm, sv = _rowwise(_adam_math, [w_pack, g_pack, m_pack, v_pack], [], [(128, F32)] * 3, name="adam_small")
    w_shapes = [tuple(a[n].shape) for n in s_names]
    res.update({n: (sg[n], d_, m_, v_) for n, d_, m_, v_ in zip(s_names, _unpack_rows(sd, w_shapes),
                                                               _unpack_rows(sm, w_shapes), _unpack_rows(sv, w_shapes))})

    red['in'], red['pp'] = reduce_finish(started2, sd, 2)
    res['w_in_e'] = tuple(jnp.transpose(o, (1, 2, 0)) for o in _adamw_t(
        rows_first(a['w_in_e']), rows_first(a['m_w_in_e']), rows_first(a['v_w_in_e']),
        *[h.reshape(2052, 1, D_MODEL // 2) for h in red['in']], c_idx, name="adam_w_in_e"))
    adam_layer('w_ple_proj', 0, 'pp', 0)
    adam_layer('w_ple_proj', 1, 'pp', 256)

    return (loss, grad_x[None], *[res[n][0] for n in _WEIGHTS], *[res[n][1] for n in _WEIGHTS],
            *[res[n][2] for n in _WEIGHTS], *[res[n][3] for n in _WEIGHTS])


def kernel(x, p, norm_mix, norm_mlp, norm_ple, w_in_e, w_out_e, hgrn_lb, g_norm_a, conv_w, a_log, dt_bias, g_norm_b, s5_a_re, s5_a_im, s5_b_re, s5_b_im, s5_c_re, s5_c_im, s5_d, s5_log_dt, w_glu, b_glu, w_out_o, w_up, w_down, w_ple_gate, w_ple_proj, final_norm, loss_target, m_norm_mix, m_norm_mlp, m_norm_ple, m_w_in_e, m_w_out_e, m_hgrn_lb, m_g_norm_a, m_conv_w, m_a_log, m_dt_bias, m_g_norm_b, m_s5_a_re, m_s5_a_im, m_s5_b_re, m_s5_b_im, m_s5_c_re, m_s5_c_im, m_s5_d, m_s5_log_dt, m_w_glu, m_b_glu, m_w_out_o, m_w_up, m_w_down, m_w_ple_gate, m_w_ple_proj, m_final_norm, v_norm_mix, v_norm_mlp, v_norm_ple, v_w_in_e, v_w_out_e, v_hgrn_lb, v_g_norm_a, v_conv_w, v_a_log, v_dt_bias, v_g_norm_b, v_s5_a_re, v_s5_a_im, v_s5_b_re, v_s5_b_im, v_s5_c_re, v_s5_c_im, v_s5_d, v_s5_log_dt, v_w_glu, v_b_glu, v_w_out_o, v_w_up, v_w_down, v_w_ple_gate, v_w_ple_proj, v_final_norm):
    args = (x, p, norm_mix, norm_mlp, norm_ple, w_in_e, w_out_e, hgrn_lb, g_norm_a, conv_w, a_log, dt_bias, g_norm_b, s5_a_re, s5_a_im, s5_b_re, s5_b_im, s5_c_re, s5_c_im, s5_d, s5_log_dt, w_glu, b_glu, w_out_o, w_up, w_down, w_ple_gate, w_ple_proj, final_norm, loss_target, m_norm_mix, m_norm_mlp, m_norm_ple, m_w_in_e, m_w_out_e, m_hgrn_lb, m_g_norm_a, m_conv_w, m_a_log, m_dt_bias, m_g_norm_b, m_s5_a_re, m_s5_a_im, m_s5_b_re, m_s5_b_im, m_s5_c_re, m_s5_c_im, m_s5_d, m_s5_log_dt, m_w_glu, m_b_glu, m_w_out_o, m_w_up, m_w_down, m_w_ple_gate, m_w_ple_proj, m_final_norm, v_norm_mix, v_norm_mlp, v_norm_ple, v_w_in_e, v_w_out_e, v_hgrn_lb, v_g_norm_a, v_conv_w, v_a_log, v_dt_bias, v_g_norm_b, v_s5_a_re, v_s5_a_im, v_s5_b_re, v_s5_b_im, v_s5_c_re, v_s5_c_im, v_s5_d, v_s5_log_dt, v_w_glu, v_b_glu, v_w_out_o, v_w_up, v_w_down, v_w_ple_gate, v_w_ple_proj, v_final_norm)
    return _step(dict(zip(_INPUTS, args)))
```

```python
import functools
import math

import jax
import jax.numpy as jnp
from jax import lax
from jax.experimental import pallas as pl
from jax.experimental.pallas import tpu as pltpu

F32 = jnp.float32
BF16 = jnp.bfloat16

D_MODEL = 2048
SEQ = 4096
NORM_EPS = 1e-6
CHUNK = 64
HEAD = 128
VMEM_LIMIT = 56 * 1024 * 1024


_NN = (((1,), (0,)), ((), ()))
_NT = (((1,), (1,)), ((), ()))
_TN = (((0,), (0,)), ((), ()))
_HI = lax.Precision.HIGHEST
_NEG = -1e30


def _cparams(sem, **kw):
    return pltpu.CompilerParams(dimension_semantics=sem, vmem_limit_bytes=VMEM_LIMIT, **kw)


PACK_COLS = 2048


def _view_shape(view):
    kind, _, r = view
    return (4 * r, PACK_COLS) if kind == "row" else (r, 4 * PACK_COLS)


def _view_spec(view, rb, cb, row_of, col_of):
    kind, off, r = view
    assert off % rb == 0 and r % rb == 0 and PACK_COLS % cb == 0, (view, rb, cb)
    if kind == "row":
        nrb = r // rb
        return pl.BlockSpec((None, rb, cb), lambda i, j, k: (row_of(i, j, k) // nrb,
                                                             off // rb + row_of(i, j, k) % nrb, col_of(i, j, k)))
    ncb = PACK_COLS // cb
    return pl.BlockSpec((None, rb, cb), lambda i, j, k: (col_of(i, j, k) // ncb,
                                                         off // rb + row_of(i, j, k), col_of(i, j, k) % ncb))


def _mm(a, b, *, dims="nn", epilogue=None, extras=(), out_dtypes=(F32,), bm=1024, bn=1024, bk=2048, name,
        b_view=None, out_into=None, out_view=None, after=()):
    b_shape = _view_shape(b_view) if b_view is not None else b.shape
    if dims == "tn":
        (K, M), (K2, N) = a.shape, b_shape
    elif dims == "nt":
        (M, K), (N, K2) = a.shape, b_shape
    else:
        (M, K), (K2, N) = a.shape, b_shape
    assert K == K2, (a.shape, b_shape, dims)
    if b_view is not None and dims == "nt":
        bn = min(bn, b_view[2])
    if b_view is not None and dims != "nt":
        bk = min(bk, b_view[2])
    if out_view is not None:
        bm = min(bm, out_view[2])
    bm, bn, bk = min(bm, M), min(bn, N), min(bk, K)
    assert M % bm == 0 and N % bn == 0 and K % bk == 0, (M, N, K, bm, bn, bk)
    nk = K // bk
    ii, jj, kk = (lambda i, j, k: i), (lambda i, j, k: j), (lambda i, j, k: k)
    if dims == "tn":
        a_spec = pl.BlockSpec((bk, bm), lambda i, j, k: (k, i))
        dn = _TN
    else:
        a_spec = pl.BlockSpec((bm, bk), lambda i, j, k: (i, k))
        dn = _NT if dims == "nt" else _NN
    if dims == "nt":
        b_spec = _view_spec(b_view, bn, bk, jj, kk) if b_view else pl.BlockSpec((bn, bk), lambda i, j, k: (j, k))
    else:
        b_spec = _view_spec(b_view, bk, bn, kk, jj) if b_view else pl.BlockSpec((bk, bn), lambda i, j, k: (k, j))
    e_specs = []
    for e in extras:
        if e.shape == (M, N):
            e_specs.append(pl.BlockSpec((bm, bn), lambda i, j, k: (i, j)))
        else:
            assert e.shape == (1, N), e.shape
            e_specs.append(pl.BlockSpec((1, bn), lambda i, j, k: (0, j)))
    ne, no = len(extras), len(out_dtypes)
    if epilogue is None:
        epilogue = lambda acc: (acc,)
    into = out_into is not None

    def body(a_ref, b_ref, *rest):
        e_refs, rest = rest[:ne], rest[ne + (1 if into else 0) + len(after):]
        o_refs = rest[:no]
        part = lax.dot_general(a_ref[...].astype(BF16), b_ref[...].astype(BF16), dn, preferred_element_type=F32)

        def finish(total):
            outs = epilogue(total, *[e[...] for e in e_refs])
            for o, v in zip(o_refs, outs):
                o[...] = v.astype(o.dtype)

        if nk == 1:
            finish(part)
            return
        acc, k = rest[no], pl.program_id(2)

        @pl.when(k == 0)
        def _():
            acc[...] = part

        @pl.when((k > 0) & (k < nk - 1))
        def _():
            acc[...] += part

        @pl.when(k == nk - 1)
        def _():
            finish(acc[...] + part)

    if into:
        assert no == 1 and _view_shape(out_view) == (M, N), (out_view, M, N)
        out_specs = [_view_spec(out_view, bm, bn, ii, jj)]
        out_shape = [jax.ShapeDtypeStruct(out_into.shape, out_into.dtype)]
        extra_in, extra_specs, alias = [out_into], [pl.BlockSpec(memory_space=pl.ANY)], {2 + ne: 0}
    else:
        out_specs = [pl.BlockSpec((bm, bn), lambda i, j, k: (i, j)) for _ in out_dtypes]
        out_shape = [jax.ShapeDtypeStruct((M, N), dt) for dt in out_dtypes]
        extra_in, extra_specs, alias = [], [], {}
    outs = pl.pallas_call(
        body, name=name,
        grid=(M // bm, N // bn, nk),
        in_specs=[a_spec, b_spec] + e_specs + extra_specs + [pl.BlockSpec(memory_space=pl.ANY)] * len(after),
        out_specs=out_specs, out_shape=out_shape,
        scratch_shapes=[pltpu.VMEM((bm, bn), F32)] if nk > 1 else [],
        input_output_aliases=alias,
        compiler_params=_cparams(("parallel", "parallel", "arbitrary")),
    )(a, b, *extras, *extra_in, *after)
    return outs[0] if no == 1 else tuple(outs)


def _rowwise(fn, tiled, bcast, outs, sums=(), *, rows=256, name):
    S = tiled[0].shape[0]
    rows = min(rows, S)
    assert S % rows == 0
    nt, nb, no, ns = len(tiled), len(bcast), len(outs), len(sums)

    def body(*refs):
        t_refs, b_refs = refs[:nt], refs[nt:nt + nb]
        o_refs, s_refs = refs[nt + nb:nt + nb + no], refs[nt + nb + no:]
        res = fn(*[r[...] for r in t_refs], *[r[...] for r in b_refs])
        res = res if isinstance(res, (tuple, list)) else (res,)
        for o, v in zip(o_refs, res[:no]):
            o[...] = v.astype(o.dtype)
        if ns:
            @pl.when(pl.program_id(0) == 0)
            def _():
                for s in s_refs:
                    s[...] = jnp.zeros_like(s)
            for s, v in zip(s_refs, res[no:]):
                s[...] += v

    res = pl.pallas_call(
        body, name=name,
        grid=(S // rows,),
        in_specs=[pl.BlockSpec((rows, t.shape[1]), lambda i: (i, 0)) for t in tiled]
        + [pl.BlockSpec(b.shape, lambda i, nd=b.ndim: (0,) * nd) for b in bcast],
        out_specs=[pl.BlockSpec((rows, c), lambda i: (i, 0)) for c, _ in outs]
        + [pl.BlockSpec((1, c), lambda i: (0, 0)) for c in sums],
        out_shape=[jax.ShapeDtypeStruct((S, c), dt) for c, dt in outs]
        + [jax.ShapeDtypeStruct((1, c), F32) for c in sums],
        compiler_params=_cparams(("arbitrary",)),
    )(*tiled, *bcast)
    return res[0] if len(res) == 1 else tuple(res)


def _rms(x, g):
    return x * lax.rsqrt(jnp.mean(x * x, axis=-1, keepdims=True) + NORM_EPS) * g


def _mxu(a, b, dn):
    if a.ndim == 3:
        ((ca,), (cb,)), _ = dn
        dn = (((ca + 1,), (cb + 1,)), ((0,), (0,)))
    return lax.dot_general(a, b, dn, preferred_element_type=F32)


def _split(a):
    hi = a.astype(BF16)
    return hi, (a - hi.astype(F32)).astype(BF16)


def _passes(a, b, dn, three):
    if not three:
        return _mxu(a.astype(BF16), b.astype(BF16), dn)
    (ah, al), (bh, bl) = _split(a), _split(b)
    return _mxu(ah, bh, dn) + (_mxu(ah, bl, dn) + _mxu(al, bh, dn))


def _dot_grads(a, b, g, dn, three):
    if dn == _NN:
        return _passes(g, b, _NT, three), _passes(a, g, _TN, three)
    if dn == _NT:
        return _passes(g, b, _NN, three), _passes(g, a, _TN, three)
    assert dn == _TN
    return _passes(b, g, _NT, three), _passes(a, g, _NN, three)


@functools.partial(jax.custom_vjp, nondiff_argnums=(2,))
def _dot(a, b, dn=_NN):
    return _passes(a, b, dn, False)


_dot.defvjp(lambda a, b, dn: (_passes(a, b, dn, False), (a, b)),
            lambda dn, res, g: _dot_grads(res[0], res[1], g, dn, False))


@functools.partial(jax.custom_vjp, nondiff_argnums=(2,))
def _dot3(a, b, dn=_NN):
    return _passes(a, b, dn, True)


_dot3.defvjp(lambda a, b, dn: (_passes(a, b, dn, True), (a, b)),
             lambda dn, res, g: _dot_grads(res[0], res[1], g, dn, True))


def _tri_times(x, dn):
    tri = _tril(x.shape[-2]).astype(BF16)
    if x.ndim == 3:
        tri = jnp.broadcast_to(tri, (x.shape[0],) + tri.shape)
    hi, lo = _split(x)
    lo2 = (x - hi.astype(F32) - lo.astype(F32)).astype(BF16)
    return _mxu(tri, hi, dn) + (_mxu(tri, lo, dn) + _mxu(tri, lo2, dn))


@jax.custom_vjp
def _cumsum_rows(x):
    return _tri_times(x, _NN)


_cumsum_rows.defvjp(lambda x: (_tri_times(x, _NN), None), lambda _, g: (_tri_times(g, _TN),))


def _iota(shape, dim):
    return lax.broadcasted_iota(jnp.int32, shape, dim)


def _tril(n, strict=False):
    r, c = _iota((n, n), 0), _iota((n, n), 1)
    return (r > c) if strict else (r >= c)


@functools.partial(jax.custom_vjp, nondiff_argnums=(1,))
def _roll_rows(x, r):
    return pltpu.roll(x, r, 0)


def _roll_rows_fwd(x, r):
    return pltpu.roll(x, r, 0), None


def _roll_rows_bwd(r, _, g):
    return (pltpu.roll(g, (g.shape[0] - r) % g.shape[0], 0),)


_roll_rows.defvjp(_roll_rows_fwd, _roll_rows_bwd)


def _head_norm_gate(o, gn, gate):
    return _rms(o, gn) * jax.nn.silu(gate)


_SUB = 16
_HPS = 8


def _hgrn_chunk(q, fpre, v, gate, lb, gna, st):
    c = q.shape[0]
    forget = lb + (1.0 - lb) * jax.nn.sigmoid(fpre)
    k = 1.0 - forget
    logf = jnp.log(forget)
    cum = _cumsum_rows(logf)
    cum_end = jnp.sum(logf, axis=0, keepdims=True)
    o = _dot(q * jnp.exp(cum), st, _NT)
    st_new = st * jnp.exp(cum_end) + _dot(v, k * jnp.exp(cum_end - cum), _TN)
    t = _iota((c, 1), 0)
    s_off = jnp.zeros((c, c), F32)
    for i in range(1, c // _SUB):
        before = t < i * _SUB
        c_i = jnp.sum(jnp.where(before, logf, 0.0), axis=0, keepdims=True)
        in_blk = (t >= i * _SUB) & (t < (i + 1) * _SUB)
        qi = jnp.where(in_blk, q * jnp.exp(jnp.minimum(cum - c_i, 0.0)), 0.0)
        ki = jnp.where(before, k * jnp.exp(jnp.minimum(c_i - cum, 0.0)), 0.0)
        s_off = s_off + _dot(qi, ki, _NT)
    o = o + _dot(s_off, v)
    tmod = t % _SUB
    for r in range(_SUB):
        kr, cr, vr = (k, cum, v) if r == 0 else (_roll_rows(k, r), _roll_rows(cum, r), _roll_rows(v, r))
        w = q * kr * jnp.exp(jnp.where(tmod >= r, cum - cr, _NEG))
        o = o + jnp.sum(w, axis=1, keepdims=True) * vr
    return _head_norm_gate(o, gna, gate), st_new


def _hgrn_fwd(proj, lb, gna):
    S = proj.shape[0]
    n_chunks, heads = S // CHUNK, 8

    def body(q_ref, f_ref, v_ref, g_ref, lb_ref, gna_ref, o_ref, st_out, st):
        n, hb = pl.program_id(0), pl.program_id(1)

        @pl.when(n == 0)
        def _():
            st[hb] = jnp.zeros((_HPS, HEAD, HEAD), F32)

        sls = [slice(hh * HEAD, (hh + 1) * HEAD) for hh in range(_HPS)]
        st0 = [st[hb, hh] for hh in range(_HPS)]
        ins = [(q_ref[:, sl], f_ref[:, sl], v_ref[:, sl], g_ref[:, sl], lb_ref[:, sl]) for sl in sls]
        gna = gna_ref[...]
        res = [_hgrn_chunk(*ins[hh], gna, st0[hh]) for hh in range(_HPS)]
        for hh, sl in enumerate(sls):
            st_out[hh] = st0[hh]
            o_ref[:, sl] = res[hh][0].astype(o_ref.dtype)
            st[hb, hh] = res[hh][1]

    wide = _HPS * HEAD
    sec = lambda s: pl.BlockSpec((CHUNK, wide), lambda n, h, s=s: (n, (8 // _HPS) * s + h))
    return pl.pallas_call(
        body, name="hgrn_fwd", grid=(n_chunks, heads // _HPS),
        in_specs=[sec(0), sec(1), sec(2), sec(3),
                  pl.BlockSpec((1, wide), lambda n, h: (0, h)), pl.BlockSpec((1, HEAD), lambda n, h: (0, 0))],
        out_specs=[pl.BlockSpec((CHUNK, wide), lambda n, h: (n, h)),
                   pl.BlockSpec((None, _HPS, HEAD, HEAD), lambda n, h: (n, h, 0, 0))],
        out_shape=[jax.ShapeDtypeStruct((S, 1024), BF16),
                   jax.ShapeDtypeStruct((n_chunks, heads, HEAD, HEAD), F32)],
        scratch_shapes=[pltpu.VMEM((heads // _HPS, _HPS, HEAD, HEAD), F32)],
        compiler_params=_cparams(("arbitrary", "arbitrary")),
    )(proj, proj, proj, proj, lb, gna)


def _hgrn_bwd(proj, lb, gna, states, d_o, after=()):
    S = proj.shape[0]
    n_chunks, heads = S // CHUNK, 8

    def body(q_ref, f_ref, v_ref, g_ref, lb_ref, gna_ref, st_ref, do_ref, *rest):
        dq_ref, df_ref, dv_ref, dg_ref, dlb_ref, dgna_ref, dst = rest[len(after):]
        n, hb = pl.program_id(0), pl.program_id(1)

        @pl.when(n == 0)
        def _():
            dst[hb] = jnp.zeros((_HPS, HEAD, HEAD), F32)
            dlb_ref[hb] = jnp.zeros((_HPS, 1, HEAD), F32)

        @pl.when((n == 0) & (hb == 0))
        def _():
            dgna_ref[...] = jnp.zeros_like(dgna_ref)

        sls = [slice(hh * HEAD, (hh + 1) * HEAD) for hh in range(_HPS)]
        gna = gna_ref[...]
        ins = [(q_ref[:, sl], f_ref[:, sl], v_ref[:, sl], g_ref[:, sl], lb_ref[:, sl], gna, st_ref[hh])
               for hh, sl in enumerate(sls)]
        cts = [(do_ref[:, sl], dst[hb, hh]) for hh, sl in enumerate(sls)]
        grads = [jax.vjp(_hgrn_chunk, *ins[hh])[1](cts[hh]) for hh in range(_HPS)]
        dgna_sum = jnp.zeros((1, HEAD), F32)
        for hh, sl in enumerate(sls):
            dq, df, dv, dg, dlb, dgna, dst0 = grads[hh]
            dq_ref[:, sl] = dq.astype(dq_ref.dtype)
            df_ref[:, sl] = df.astype(df_ref.dtype)
            dv_ref[:, sl] = dv.astype(dv_ref.dtype)
            dg_ref[:, sl] = dg.astype(dg_ref.dtype)
            dlb_ref[hb, hh] += dlb
            dgna_sum = dgna_sum + dgna
            dst[hb, hh] = dst0
        dgna_ref[...] += dgna_sum

    rev = lambda n: n_chunks - 1 - n
    wide = _HPS * HEAD
    sec = lambda s: pl.BlockSpec((CHUNK, wide), lambda n, h, s=s: (rev(n), (8 // _HPS) * s + h))
    out = pl.BlockSpec((CHUNK, wide), lambda n, h: (rev(n), h))
    return pl.pallas_call(
        body, name="hgrn_bwd", grid=(n_chunks, heads // _HPS),
        in_specs=[sec(0), sec(1), sec(2), sec(3),
                  pl.BlockSpec((1, wide), lambda n, h: (0, h)), pl.BlockSpec((1, HEAD), lambda n, h: (0, 0)),
                  pl.BlockSpec((None, _HPS, HEAD, HEAD), lambda n, h: (rev(n), h, 0, 0)),
                  pl.BlockSpec((CHUNK, wide), lambda n, h: (rev(n), h))] + [pl.BlockSpec(memory_space=pl.ANY)] * len(after),
        out_specs=[out, out, out, out,
                   pl.BlockSpec((heads // _HPS, _HPS, 1, HEAD), lambda n, h: (0, 0, 0, 0)),
                   pl.BlockSpec((1, HEAD), lambda n, h: (0, 0))],
        out_shape=[jax.ShapeDtypeStruct((S, 1024), BF16)] * 4
        + [jax.ShapeDtypeStruct((heads // _HPS, _HPS, 1, HEAD), F32), jax.ShapeDtypeStruct((1, HEAD), F32)],
        scratch_shapes=[pltpu.VMEM((heads // _HPS, _HPS, HEAD, HEAD), F32)],
        compiler_params=_cparams(("arbitrary", "arbitrary")),
    )(proj, proj, proj, proj, lb, gna, states, d_o, *after)


def _l2n(t):
    return t * lax.rsqrt(jnp.sum(t * t, axis=-1, keepdims=True) + NORM_EPS)


def _unit_lower_inverse(low):
    c = low.shape[-1]
    inv = (_iota((c, c), 0) == _iota((c, c), 1)).astype(F32) - low
    p = low
    span = 2
    while span < c:
        p = _dot3(p, p)
        inv = inv + _dot3(inv, p)
        span *= 2
    return inv


def _gdn_chunk(qc, kc, v, z, tail, h0, arow, dtrow, gnb, st):
    nb, c = qc.shape[0], qc.shape[1]
    head = h0 + _iota((nb, c, HEAD), 0)
    lane = _iota((nb, c, HEAD), 2)
    la_all = arow * jax.nn.softplus(tail + dtrow)
    la = jnp.sum(jnp.where(lane == head, la_all[None], 0.0), axis=2, keepdims=True)
    beta = jnp.sum(jnp.where(lane == head + 8, jax.nn.sigmoid(tail)[None], 0.0), axis=2, keepdims=True)
    la_b = jnp.broadcast_to(la, (nb, c, HEAD))
    cum = _cumsum_rows(la_b)
    cmat = _cumsum_rows(jnp.broadcast_to(la, (nb, c, c)))
    cum_end = jnp.sum(la_b, axis=1, keepdims=True)
    decay = jnp.exp(jnp.where(_tril(c), cmat - jnp.swapaxes(cmat, 1, 2), _NEG))
    q = _l2n(qc) * (HEAD ** -0.5)
    k = _l2n(kc)
    k_beta = k * beta
    low = jnp.where(_tril(c, strict=True), _dot(k_beta, k, _NT) * decay, 0.0)
    inv = _unit_lower_inverse(low)
    u, w = _dot(inv, v * beta), _dot(inv, k_beta * jnp.exp(cum))
    intra = _dot(q, k, _NT) * decay
    v_new = u - _dot(w, st, _NT)
    o = _dot(q * jnp.exp(cum), st, _NT) + _dot(intra, v_new)
    st_new = st * jnp.exp(cum_end) + _dot(v_new, k * jnp.exp(cum_end - cum), _TN)
    return _head_norm_gate(o, gnb, z), st_new


def _gdn_specs(n_of):
    wide = _HPS * HEAD
    qkv = lambda s: pl.BlockSpec((CHUNK, wide), lambda n, h, s=s: (n_of(n), (8 // _HPS) * s + h))
    return [qkv(0), qkv(1), qkv(2),
            pl.BlockSpec((CHUNK, wide), lambda n, h: (n_of(n), 56 // _HPS + h)),
            pl.BlockSpec((CHUNK, HEAD), lambda n, h: (n_of(n), 64)),
            pl.BlockSpec((1, HEAD), lambda n, h: (0, 0)), pl.BlockSpec((1, HEAD), lambda n, h: (0, 0)),
            pl.BlockSpec((1, HEAD), lambda n, h: (0, 0))]


def _gdn_fwd(act, proj, arow, dtrow, gnb):
    S = act.shape[0]
    n_chunks, heads = S // CHUNK, 8

    def body(q_ref, k_ref, v_ref, z_ref, t_ref, a_ref, dt_ref, gnb_ref, o_ref, st_out, st):
        n, hb = pl.program_id(0), pl.program_id(1)

        @pl.when(n == 0)
        def _():
            st[hb] = jnp.zeros((_HPS, HEAD, HEAD), F32)

        sls = [slice(hh * HEAD, (hh + 1) * HEAD) for hh in range(_HPS)]
        heads_of = lambda ref: jnp.stack([ref[:, sl] for sl in sls])
        st0 = st[hb]
        o, st1 = _gdn_chunk(heads_of(q_ref), heads_of(k_ref), heads_of(v_ref), heads_of(z_ref), t_ref[...], hb * _HPS,
                            a_ref[...], dt_ref[...], gnb_ref[...], st0)
        st_out[...] = st0
        st[hb] = st1
        for hh, sl in enumerate(sls):
            o_ref[:, sl] = o[hh].astype(o_ref.dtype)

    return pl.pallas_call(
        body, name="gdn_fwd", grid=(n_chunks, heads // _HPS),
        in_specs=_gdn_specs(lambda n: n),
        out_specs=[pl.BlockSpec((CHUNK, _HPS * HEAD), lambda n, h: (n, h)),
                   pl.BlockSpec((None, _HPS, HEAD, HEAD), lambda n, h: (n, h, 0, 0))],
        out_shape=[jax.ShapeDtypeStruct((S, 1024), BF16),
                   jax.ShapeDtypeStruct((n_chunks, heads, HEAD, HEAD), F32)],
        scratch_shapes=[pltpu.VMEM((heads // _HPS, _HPS, HEAD, HEAD), F32)],
        compiler_params=_cparams(("arbitrary", "arbitrary")),
    )(act, act, act, proj, proj, arow, dtrow, gnb)


def _gdn_bwd(act, proj, arow, dtrow, gnb, states, d_o):
    S = act.shape[0]
    n_chunks, heads = S // CHUNK, 8

    def body(q_ref, k_ref, v_ref, z_ref, t_ref, a_ref, dt_ref, gnb_ref, st_ref, do_ref,
             dq_ref, dk_ref, dv_ref, dz_ref, dt_out, da_ref, ddt_ref, dgnb_ref, dst, dtail_acc):
        n, hb = pl.program_id(0), pl.program_id(1)
        n_hb = heads // _HPS

        @pl.when(n == 0)
        def _():
            dst[hb] = jnp.zeros((_HPS, HEAD, HEAD), F32)

        @pl.when((n == 0) & (hb == 0))
        def _():
            da_ref[...] = jnp.zeros_like(da_ref)
            ddt_ref[...] = jnp.zeros_like(ddt_ref)
            dgnb_ref[...] = jnp.zeros_like(dgnb_ref)

        sls = [slice(hh * HEAD, (hh + 1) * HEAD) for hh in range(_HPS)]
        heads_of = lambda ref: jnp.stack([ref[:, sl] for sl in sls])
        fn = lambda qc, kc, v, z, tail, arow, dtrow, gnb, st: _gdn_chunk(qc, kc, v, z, tail, hb * _HPS, arow, dtrow,
                                                                        gnb, st)
        _, vjp = jax.vjp(fn, heads_of(q_ref), heads_of(k_ref), heads_of(v_ref), heads_of(z_ref), t_ref[...],
                         a_ref[...], dt_ref[...], gnb_ref[...], st_ref[...])
        dq, dk, dv, dz, dtail_sum, da_sum, ddt_sum, dgnb_sum, dst0 = vjp((heads_of(do_ref), dst[hb]))
        dst[hb] = dst0
        for hh, sl in enumerate(sls):
            dq_ref[:, sl] = dq[hh]
            dk_ref[:, sl] = dk[hh]
            dv_ref[:, sl] = dv[hh]
            dz_ref[:, sl] = dz[hh].astype(dz_ref.dtype)

        @pl.when(hb == 0)
        def _():
            dtail_acc[...] = dtail_sum

        @pl.when(hb > 0)
        def _():
            dtail_acc[...] += dtail_sum

        @pl.when(hb == n_hb - 1)
        def _():
            dt_out[...] = dtail_acc[...].astype(dt_out.dtype)

        da_ref[...] += da_sum
        ddt_ref[...] += ddt_sum
        dgnb_ref[...] += dgnb_sum

    rev = lambda n: n_chunks - 1 - n
    out = pl.BlockSpec((CHUNK, _HPS * HEAD), lambda n, h: (rev(n), h))
    row = pl.BlockSpec((1, HEAD), lambda n, h: (0, 0))
    return pl.pallas_call(
        body, name="gdn_bwd", grid=(n_chunks, heads // _HPS),
        in_specs=_gdn_specs(rev)
        + [pl.BlockSpec((None, _HPS, HEAD, HEAD), lambda n, h: (rev(n), h, 0, 0)),
           pl.BlockSpec((CHUNK, _HPS * HEAD), lambda n, h: (rev(n), 8 // _HPS + h))],
        out_specs=[out, out, out, out, pl.BlockSpec((CHUNK, HEAD), lambda n, h: (rev(n), 0)), row, row, row],
        out_shape=[jax.ShapeDtypeStruct((S, 1024), F32)] * 3
        + [jax.ShapeDtypeStruct((S, 1024), BF16), jax.ShapeDtypeStruct((S, HEAD), BF16)]
        + [jax.ShapeDtypeStruct((1, HEAD), F32)] * 3,
        scratch_shapes=[pltpu.VMEM((heads // _HPS, _HPS, HEAD, HEAD), F32), pltpu.VMEM((CHUNK, HEAD), F32)],
        compiler_params=_cparams(("arbitrary", "arbitrary")),
    )(act, act, act, proj, proj, arow, dtrow, gnb, states, d_o)


def _conv_silu(x, w):
    t = _iota((x.shape[0], 1), 0)
    tap = _iota(w.shape, 0)
    y = jnp.zeros_like(x)
    for r in range(4):
        w_r = jnp.sum(jnp.where(tap == 3 - r, w, 0.0), axis=0, keepdims=True)
        y = y + (x if r == 0 else jnp.where(t >= r, _roll_rows(x, r), 0.0)) * w_r
    return jax.nn.silu(y)


_CONV_COLS = 128


def _conv_fwd(proj, conv_w):
    S = proj.shape[0]
    nb = 3072 // _CONV_COLS
    off = 4096 // _CONV_COLS

    def body(x_ref, w_ref, o_ref):
        o_ref[...] = _conv_silu(x_ref[...], w_ref[...])

    return pl.pallas_call(
        body, name="conv_fwd", grid=(nb,),
        in_specs=[pl.BlockSpec((S, _CONV_COLS), lambda j: (0, off + j)), pl.BlockSpec((4, _CONV_COLS), lambda j: (0, j))],
        out_specs=pl.BlockSpec((S, _CONV_COLS), lambda j: (0, j)),
        out_shape=jax.ShapeDtypeStruct((S, 3072), F32),
        compiler_params=_cparams(("parallel",)),
    )(proj, conv_w)


def _conv_bwd(proj, conv_w, dq, dk, dv):
    S = proj.shape[0]
    nb = 3072 // _CONV_COLS
    off = 4096 // _CONV_COLS
    per = 1024 // _CONV_COLS

    def body(x_ref, w_ref, dq_ref, dk_ref, dv_ref, dx_ref, dw_ref):
        j = pl.program_id(0)
        _, vjp = jax.vjp(_conv_silu, x_ref[...], w_ref[...])
        d = jnp.where(j < per, dq_ref[...], jnp.where(j < 2 * per, dk_ref[...], dv_ref[...]))
        dx, dw = vjp(d)
        dx_ref[...] = dx.astype(dx_ref.dtype)
        dw_ref[...] = dw

    dsp = lambda s: pl.BlockSpec((S, _CONV_COLS), lambda j, s=s: (0, jnp.clip(j - s * per, 0, per - 1)))
    return pl.pallas_call(
        body, name="conv_bwd", grid=(nb,),
        in_specs=[pl.BlockSpec((S, _CONV_COLS), lambda j: (0, off + j)), pl.BlockSpec((4, _CONV_COLS), lambda j: (0, j)),
                  dsp(0), dsp(1), dsp(2)],
        out_specs=[pl.BlockSpec((S, _CONV_COLS), lambda j: (0, j)), pl.BlockSpec((4, _CONV_COLS), lambda j: (0, j))],
        out_shape=[jax.ShapeDtypeStruct((S, 3072), BF16), jax.ShapeDtypeStruct((4, 3072), F32)],
        compiler_params=_cparams(("parallel",)),
    )(proj, conv_w, dq, dk, dv)


_S5_T = 512
_S5_L = 512
_S5_NB = 16
_S5_U = 4
_S5_UB = 3


def _s5_tile(xr, xi, pw_r, pw_i, cr, ci, reverse):
    t8 = _iota((8, 1), 0)
    for sh in (1, 2, 4):
        row = (8 - sh) if reverse else (sh - 1)
        ar, ai = pw_r[row:row + 1, :], pw_i[row:row + 1, :]
        if reverse:
            keep, amt = t8 < 8 - sh, 8 - sh
        else:
            keep, amt = t8 >= sh, sh
        sr = jnp.where(keep, pltpu.roll(xr, amt, 0), 0.0)
        si = jnp.where(keep, pltpu.roll(xi, amt, 0), 0.0)
        xr, xi = xr + ar * sr - ai * si, xi + ar * si + ai * sr
    xr, xi = xr + pw_r * cr - pw_i * ci, xi + pw_r * ci + pw_i * cr
    return xr, xi


def _s5_fwd(u, bre, bim, cre, cim, pw_r, pw_i, dskip):
    S = u.shape[0]
    T = min(_S5_T, S)
    nt = S // T

    def body(u_ref, bre_ref, bim_ref, cre_ref, cim_ref, pr_ref, pi_ref, d_ref, y_ref, xr_ref, xi_ref,
             bu_r, bu_i, car_r, car_i):
        @pl.when(pl.program_id(1) == 0)
        def _():
            car_r[...] = jnp.zeros_like(car_r)
            car_i[...] = jnp.zeros_like(car_i)

        uu = u_ref[...]
        bu_r[...] = _dot(uu, bre_ref[...])
        bu_i[...] = _dot(uu, bim_ref[...])
        pw_r, pw_i = pr_ref[...], pi_ref[...]

        def tiles(i, carry):
            ins = []
            for k in range(_S5_U):
                r0 = pl.multiple_of((i * _S5_U + k) * 8, 8)
                ins.append((r0, bu_r[pl.ds(r0, 8), :], bu_i[pl.ds(r0, 8), :]))
            outs = []
            for r0, br, bi in ins:
                xr, xi = _s5_tile(br, bi, pw_r, pw_i, carry[0], carry[1], False)
                carry = (xr[7:8, :], xi[7:8, :])
                outs.append((r0, xr, xi))
            for r0, xr, xi in outs:
                xr_ref[pl.ds(r0, 8), :] = xr
                xi_ref[pl.ds(r0, 8), :] = xi
            return carry

        cr, ci = lax.fori_loop(0, T // (8 * _S5_U), tiles, (car_r[...], car_i[...]))
        car_r[...] = cr
        car_i[...] = ci
        y_ref[...] = _dot(xr_ref[...], cre_ref[...]) - _dot(xi_ref[...], cim_ref[...]) + d_ref[...] * uu

    blk3 = lambda a, b: pl.BlockSpec((None, a, b), lambda j, t: (j, 0, 0))
    return pl.pallas_call(
        body, name="s5_fwd", grid=(_S5_NB, nt),
        in_specs=[pl.BlockSpec((T, HEAD), lambda j, t: (t, j)),
                  blk3(HEAD, _S5_L), blk3(HEAD, _S5_L), blk3(_S5_L, HEAD), blk3(_S5_L, HEAD),
                  blk3(8, _S5_L), blk3(8, _S5_L), pl.BlockSpec((1, HEAD), lambda j, t: (0, j))],
        out_specs=[pl.BlockSpec((T, HEAD), lambda j, t: (t, j)),
                   pl.BlockSpec((T, _S5_L), lambda j, t: (t, j)), pl.BlockSpec((T, _S5_L), lambda j, t: (t, j))],
        out_shape=[jax.ShapeDtypeStruct((S, D_MODEL), F32),
                   jax.ShapeDtypeStruct((S, _S5_NB * _S5_L), F32), jax.ShapeDtypeStruct((S, _S5_NB * _S5_L), F32)],
        scratch_shapes=[pltpu.VMEM((T, _S5_L), F32), pltpu.VMEM((T, _S5_L), F32),
                        pltpu.VMEM((1, _S5_L), F32), pltpu.VMEM((1, _S5_L), F32)],
        compiler_params=_cparams(("parallel", "arbitrary")),
    )(u, bre, bim, cre, cim, pw_r, pw_i, dskip)


def _s5_bwd(dy, u, xre, xim, bre, bim, cre, cim, qw_r, qw_i, dskip):
    S = u.shape[0]
    T = min(_S5_T, S)
    nt = S // T
    nt8 = T // 8

    def body(dy_ref, u_ref, xr_ref, xi_ref, xpr_ref, xpi_ref, bre_ref, bim_ref, cre_ref, cim_ref, qr_ref, qi_ref,
             d_ref, du_ref, dbr_ref, dbi_ref, dcr_ref, dci_ref, dlr_ref, dli_ref, dd_ref,
             g_r, g_i, car_r, car_i):
        t = pl.program_id(1)

        @pl.when(t == 0)
        def _():
            car_r[...] = jnp.zeros_like(car_r)
            car_i[...] = jnp.zeros_like(car_i)
            for r in (dbr_ref, dbi_ref, dcr_ref, dci_ref, dlr_ref, dli_ref, dd_ref):
                r[...] = jnp.zeros_like(r)

        dyy, uu = dy_ref[...], u_ref[...]
        g_r[...] = _dot(dyy, cre_ref[...], _NT)
        g_i[...] = -_dot(dyy, cim_ref[...], _NT)
        qw_r, qw_i = qr_ref[...], qi_ref[...]
        t8 = _iota((8, 1), 0)
        first = t == nt - 1

        def load(r0, prev_r, prev_i):
            rows = pl.ds(r0, 8)
            return r0, g_r[rows, :], g_i[rows, :], xr_ref[rows, :], xi_ref[rows, :], prev_r, prev_i

        def run(loaded, carry, acc):
            done = []
            for r0, dr, di, xr, xi, prev_r, prev_i in loaded:
                gr, gi = _s5_tile(dr, di, qw_r, qw_i, carry[0], carry[1], True)
                carry = (gr[0:1, :], gi[0:1, :])
                xpr = jnp.where(t8 >= 1, pltpu.roll(xr, 1, 0), prev_r)
                xpi = jnp.where(t8 >= 1, pltpu.roll(xi, 1, 0), prev_i)
                acc = (acc[0] + gr * xpr + gi * xpi, acc[1] + gi * xpr - gr * xpi)
                done.append((r0, gr, gi))
            for r0, gr, gi in done:
                g_r[pl.ds(r0, 8), :] = gr
                g_i[pl.ds(r0, 8), :] = gi
            return carry, acc

        def step(ii, state):
            loaded = []
            for k in range(_S5_UB):
                idx = nt8 - 1 - (ii * _S5_UB + k)
                r0 = pl.multiple_of(idx * 8, 8)
                p0 = pl.multiple_of((idx - 1) * 8, 8)
                loaded.append(load(r0, xr_ref[pl.ds(p0, 8), :][7:8, :], xi_ref[pl.ds(p0, 8), :][7:8, :]))
            return run(loaded, *state)

        zero = jnp.zeros((8, _S5_L), F32)
        assert (nt8 - 1) % _S5_UB == 0
        state = lax.fori_loop(0, (nt8 - 1) // _S5_UB, step, ((car_r[...], car_i[...]), (zero, zero)))
        prev_r = jnp.where(first, 0.0, xpr_ref[...][7:8, :])
        prev_i = jnp.where(first, 0.0, xpi_ref[...][7:8, :])
        (cr, ci), (ar, ai) = run([load(0, prev_r, prev_i)], *state)
        car_r[...] = cr
        car_i[...] = ci
        dlr_ref[...] += ar
        dli_ref[...] += ai
        gr, gi = g_r[...], g_i[...]
        du_ref[...] = _dot(gr, bre_ref[...], _NT) + _dot(gi, bim_ref[...], _NT) + d_ref[...] * dyy
        dbr_ref[...] += _dot(uu, gr, _TN)
        dbi_ref[...] += _dot(uu, gi, _TN)
        dcr_ref[...] += _dot(xr_ref[...], dyy, _TN)
        dci_ref[...] -= _dot(xi_ref[...], dyy, _TN)
        dd_ref[...] += jnp.sum(dyy * uu, axis=0, keepdims=True)

    rev = lambda t: nt - 1 - t
    blk3 = lambda a, b: pl.BlockSpec((None, a, b), lambda j, t: (j, 0, 0))
    tl = pl.BlockSpec((T, HEAD), lambda j, t: (rev(t), j))
    xs = pl.BlockSpec((T, _S5_L), lambda j, t: (rev(t), j))
    xp = pl.BlockSpec((8, _S5_L), lambda j, t: (jnp.maximum(rev(t) * nt8 - 1, 0), j))
    return pl.pallas_call(
        body, name="s5_bwd", grid=(_S5_NB, nt),
        in_specs=[tl, tl, xs, xs, xp, xp, blk3(HEAD, _S5_L), blk3(HEAD, _S5_L), blk3(_S5_L, HEAD), blk3(_S5_L, HEAD),
                  blk3(8, _S5_L), blk3(8, _S5_L), pl.BlockSpec((1, HEAD), lambda j, t: (0, j))],
        out_specs=[tl, blk3(HEAD, _S5_L), blk3(HEAD, _S5_L), blk3(_S5_L, HEAD), blk3(_S5_L, HEAD),
                   blk3(8, _S5_L), blk3(8, _S5_L), pl.BlockSpec((1, HEAD), lambda j, t: (0, j))],
        out_shape=[jax.ShapeDtypeStruct((S, D_MODEL), F32),
                   jax.ShapeDtypeStruct((_S5_NB, HEAD, _S5_L), F32), jax.ShapeDtypeStruct((_S5_NB, HEAD, _S5_L), F32),
                   jax.ShapeDtypeStruct((_S5_NB, _S5_L, HEAD), F32), jax.ShapeDtypeStruct((_S5_NB, _S5_L, HEAD), F32),
                   jax.ShapeDtypeStruct((_S5_NB, 8, _S5_L), F32), jax.ShapeDtypeStruct((_S5_NB, 8, _S5_L), F32),
                   jax.ShapeDtypeStruct((1, D_MODEL), F32)],
        scratch_shapes=[pltpu.VMEM((T, _S5_L), F32), pltpu.VMEM((T, _S5_L), F32),
                        pltpu.VMEM((1, _S5_L), F32), pltpu.VMEM((1, _S5_L), F32)],
        compiler_params=_cparams(("parallel", "arbitrary")),
    )(dy, u, xre, xim, xre, xim, bre, bim, cre, cim, qw_r, qw_i, dskip)


def _s5_params(a_re, a_im, log_dt, b_re, b_im, c_re, c_im):
    step = jnp.exp(log_dt)[:, None]
    mag = jnp.exp(a_re * step)
    lr, li = mag * jnp.cos(a_im * step), mag * jnp.sin(a_im * step)
    den = a_re * a_re + a_im * a_im
    nr, ni = lr - 1.0, li
    kr, ki = (nr * a_re + ni * a_im) / den, (ni * a_re - nr * a_im) / den
    bbr = kr[..., None] * b_re - ki[..., None] * b_im
    bbi = kr[..., None] * b_im + ki[..., None] * b_re
    eye = jnp.eye(8, dtype=F32)

    def blk_b(bb):
        t = bb.reshape(_S5_NB, 8, 64, 16).transpose(0, 1, 3, 2)
        return (t[:, :, :, None, :] * eye[None, :, None, :, None]).reshape(_S5_NB, HEAD, _S5_L)

    def blk_c(cc):
        t = cc.reshape(_S5_NB, 8, 16, 64).transpose(0, 1, 3, 2)
        return (t[:, :, :, None, :] * eye[None, :, None, :, None]).reshape(_S5_NB, _S5_L, HEAD)

    return (blk_b(bbr), blk_b(bbi), blk_c(c_re), blk_c(c_im),
            lr.reshape(_S5_NB, _S5_L), li.reshape(_S5_NB, _S5_L))


def _s5_powers(lr, li):
    pr, pi = [lr], [li]
    for _ in range(7):
        pr, pi = pr + [pr[-1] * lr - pi[-1] * li], pi + [pr[-1] * li + pi[-1] * lr]
    return jnp.stack(pr, axis=1), jnp.stack(pi, axis=1)


_MESH = pl.DeviceIdType.MESH
_ANY = pl.BlockSpec(memory_space=pl.ANY)


def _place():
    x, y, c = lax.axis_index("x"), lax.axis_index("y"), lax.axis_index("c")
    return x, y, c, [(1 - x, y), (x, 1 - y), (1 - x, 1 - y)]


def _comm_call(body, arrs, out_shapes, n_remote, name):
    n = len(arrs)
    return pl.pallas_call(
        body, name=name,
        in_specs=[_ANY] * n, out_specs=[_ANY] * n, out_shape=out_shapes,
        scratch_shapes=[pltpu.SemaphoreType.DMA((n * n_remote,)), pltpu.SemaphoreType.DMA((n * n_remote,)),
                        pltpu.SemaphoreType.DMA((n,))],
    )(*arrs)


def _half(ref, slot, h):
    if len(ref.shape) == 4:
        return ref.at[slot, h]
    cols = ref.shape[2] // 2
    return ref.at[slot, :, pl.ds(pl.multiple_of(h * cols, 128), cols)]


def _own_slot(shard, me_idx, name, rows=256, cols=512):
    def body(me_ref, in_ref, o_ref):
        o_ref[...] = in_ref[...]

    if shard.ndim == 2:
        R, C = shard.shape
        return pl.pallas_call(
            body, name=name,
            grid_spec=pltpu.PrefetchScalarGridSpec(
                num_scalar_prefetch=1, grid=(C // cols,),
                in_specs=[pl.BlockSpec((R, cols), lambda j, me: (0, j))],
                out_specs=pl.BlockSpec((None, R, cols), lambda j, me: (me[0], 0, j))),
            out_shape=jax.ShapeDtypeStruct((4,) + shard.shape, shard.dtype),
            compiler_params=_cparams(("parallel",)),
        )(me_idx, shard)
    _, half, C = shard.shape
    rows = min(rows, half)
    assert half % rows == 0
    return pl.pallas_call(
        body, name=name,
        grid_spec=pltpu.PrefetchScalarGridSpec(
            num_scalar_prefetch=1, grid=(2, half // rows),
            in_specs=[pl.BlockSpec((None, rows, C), lambda h, i, me: (h, i, 0))],
            out_specs=pl.BlockSpec((None, None, rows, C), lambda h, i, me: (me[0], h, i, 0))),
        out_shape=jax.ShapeDtypeStruct((4,) + shard.shape, shard.dtype),
        compiler_params=_cparams(("parallel", "parallel")),
    )(me_idx, shard)


def _own_rows(w, layer, buf, me_idx, row_off, after, name, rows=256):
    _, r, C = w.shape
    carried = [] if isinstance(buf, tuple) else [buf]
    shape = buf if isinstance(buf, tuple) else buf.shape
    assert r % rows == 0 and row_off % rows == 0

    def body(me_ref, in_ref, *rest):
        rest[-1][...] = in_ref[...].astype(BF16)

    return pl.pallas_call(
        body, name=name,
        grid_spec=pltpu.PrefetchScalarGridSpec(
            num_scalar_prefetch=1, grid=(r // rows,),
            in_specs=[pl.BlockSpec((None, rows, C), lambda i, me: (layer, i, 0))] + [_ANY] * (len(carried) + len(after)),
            out_specs=pl.BlockSpec((None, rows, C), lambda i, me: (me[0], row_off // rows + i, 0))),
        out_shape=jax.ShapeDtypeStruct(shape, BF16),
        input_output_aliases={2: 0} if carried else {},
        compiler_params=_cparams(("parallel",)),
    )(me_idx, w, *carried, *after)


def _pair_split(arrs, name):
    n = len(arrs)

    def body(*refs):
        ins, outs, (ssem, rsem, _) = refs[:n], refs[n:2 * n], refs[2 * n:]
        x, y, c, _ = _place()
        copies = []
        for a in range(n):
            for s in range(4):
                cp = pltpu.make_async_remote_copy(src_ref=_half(ins[a], s, 1 - c), dst_ref=outs[a].at[s],
                                                  send_sem=ssem.at[4 * a + s], recv_sem=rsem.at[4 * a + s],
                                                  device_id=(x, y, 1 - c), device_id_type=_MESH)
                cp.start()
                copies.append(cp)
        for cp in copies:
            cp.wait()

    shapes = [jax.ShapeDtypeStruct((4,) + (a.shape[2:] if a.ndim == 4 else (a.shape[1], a.shape[2] // 2)), a.dtype)
              for a in arrs]
    return _comm_call(body, arrs, shapes, 4, name)


def _pair_swap(arrs, name):
    n = len(arrs)

    def body(*refs):
        ins, outs, (ssem, rsem, _) = refs[:n], refs[n:2 * n], refs[2 * n:]
        x, y, c, _ = _place()
        copies = []
        for a in range(n):
            cp = pltpu.make_async_remote_copy(src_ref=ins[a], dst_ref=outs[a], send_sem=ssem.at[a], recv_sem=rsem.at[a],
                                              device_id=(x, y, 1 - c), device_id_type=_MESH)
            cp.start()
            copies.append(cp)
        for cp in copies:
            cp.wait()

    return _comm_call(body, arrs, [jax.ShapeDtypeStruct(a.shape, a.dtype) for a in arrs], 1, name)


def _pair_sum(full, recv, c_idx, name, rows=128):
    def add(c_ref, a_ref, b_ref, o_ref):
        o_ref[...] = (a_ref[...].astype(F32) + b_ref[...].astype(F32)).astype(o_ref.dtype)

    if full.ndim == 3:
        _, R, cols = recv.shape
        blk = lambda col: pl.BlockSpec((None, R, cols), lambda s, c: (s, 0, col(c)))
        return pl.pallas_call(
            add, name=name,
            grid_spec=pltpu.PrefetchScalarGridSpec(
                num_scalar_prefetch=1, grid=(4,),
                in_specs=[blk(lambda c: c[0]), blk(lambda c: 0)], out_specs=blk(lambda c: 0)),
            out_shape=jax.ShapeDtypeStruct(recv.shape, recv.dtype),
            compiler_params=_cparams(("parallel",)),
        )(c_idx, full, recv)
    _, _, half, C = full.shape
    rows = min(rows, half)
    nb = half // rows
    assert half % rows == 0

    def body(c_ref, a_ref, b_ref, o_ref):
        o_ref[...] = (a_ref[...].astype(F32) + b_ref[...].astype(F32)).astype(o_ref.dtype)

    return pl.pallas_call(
        body, name=name,
        grid_spec=pltpu.PrefetchScalarGridSpec(
            num_scalar_prefetch=1, grid=(4, nb),
            in_specs=[pl.BlockSpec((None, None, rows, C), lambda s, i, c: (s, c[0], i, 0)),
                      pl.BlockSpec((None, rows, C), lambda s, i, c: (s, i, 0))],
            out_specs=pl.BlockSpec((None, rows, C), lambda s, i, c: (s, i, 0))),
        out_shape=jax.ShapeDtypeStruct(recv.shape, recv.dtype),
        compiler_params=_cparams(("parallel", "parallel")),
    )(c_idx, full, recv)


_HBM = pl.BlockSpec(memory_space=pltpu.HBM)
_SEM = pl.BlockSpec(memory_space=pltpu.SEMAPHORE)
_SPLIT = dict(has_side_effects=pltpu.SideEffectType.DATAFLOW_SIDE_EFFECTING)


def _hbm(t):
    return pltpu.with_memory_space_constraint(t, pltpu.HBM)


def _ag_copy(buf_ref, ssem, rsem, j, chip, c, me):
    px, py = chip
    mine = _half(buf_ref, me, c)
    return pltpu.make_async_remote_copy(src_ref=mine, dst_ref=mine, send_sem=ssem.at[j], recv_sem=rsem.at[j],
                                        device_id=(px, py, c), device_id_type=_MESH)


def _ag_start(bufs, after, name):
    n = len(bufs)

    def body(*refs):
        buf_refs, (ssem, rsem), token = refs[:n], refs[n + 1:n + 3], refs[-1]
        x, y, c, chips = _place()
        for a in range(n):
            for j, chip in enumerate(chips):
                _ag_copy(buf_refs[a], ssem, rsem, 3 * a + j, chip, c, 2 * x + y).start()
        token[...] = jnp.zeros_like(token)

    out = pl.pallas_call(
        body, name=name,
        out_shape=(pltpu.SemaphoreType.DMA((3 * n,)), pltpu.SemaphoreType.DMA((3 * n,)),
                   *[pltpu.HBM(b.shape, b.dtype) for b in bufs], jax.ShapeDtypeStruct((8, HEAD), F32)),
        in_specs=(*[_HBM] * n, _ANY), out_specs=(_SEM, _SEM, *[_HBM] * n, pl.BlockSpec(memory_space=pltpu.VMEM)),
        input_output_aliases={a: 2 + a for a in range(n)}, compiler_params=pltpu.CompilerParams(**_SPLIT),
    )(*[_hbm(b) for b in bufs], after)
    return out[0], out[1], list(out[2:2 + n]), out[-1]


def _ag_wait(ssem, rsem, bufs, after, name):
    n = len(bufs)

    def body(*refs):
        buf_refs, ssem, rsem = refs[:n], refs[n], refs[n + 1]
        x, y, c, chips = _place()
        for a in range(n):
            for j, chip in enumerate(chips):
                cp = _ag_copy(buf_refs[a], ssem, rsem, 3 * a + j, chip, c, 2 * x + y)
                cp.wait_send()
                cp.wait_recv()

    return list(pl.pallas_call(
        body, name=name, out_shape=tuple(pltpu.HBM(b.shape, b.dtype) for b in bufs),
        in_specs=(*[_HBM] * n, _SEM, _SEM, _ANY), out_specs=tuple([_HBM] * n),
        input_output_aliases={a: a for a in range(n)}, compiler_params=pltpu.CompilerParams(**_SPLIT),
    )(*bufs, ssem, rsem, after))


def _pair_forward(bufs, name):
    n = len(bufs)

    def body(*refs):
        outs, (ssem, rsem) = refs[n:2 * n], refs[2 * n:]
        x, y, c, chips = _place()
        copies = []
        for a in range(n):
            for j, (px, py) in enumerate(chips):
                landed = _half(outs[a], 2 * px + py, c)
                cp = pltpu.make_async_remote_copy(src_ref=landed, dst_ref=landed, send_sem=ssem.at[3 * a + j],
                                                  recv_sem=rsem.at[3 * a + j], device_id=(x, y, 1 - c),
                                                  device_id_type=_MESH)
                cp.start()
                copies.append(cp)
        for cp in copies:
            cp.wait()

    return pl.pallas_call(
        body, name=name, in_specs=[_ANY] * n, out_specs=[_ANY] * n,
        out_shape=[jax.ShapeDtypeStruct(b.shape, b.dtype) for b in bufs],
        input_output_aliases={a: a for a in range(n)},
        scratch_shapes=[pltpu.SemaphoreType.DMA((3 * n,)), pltpu.SemaphoreType.DMA((3 * n,))],
    )(*bufs)


def _rs_copy(src_ref, land_ref, ssem, rsem, j, chip, c):
    px, py = chip
    return pltpu.make_async_remote_copy(src_ref=src_ref.at[2 * px + py], dst_ref=land_ref.at[j], send_sem=ssem.at[j],
                                        recv_sem=rsem.at[j], device_id=(px, py, c), device_id_type=_MESH)


def _rs_start(part, after, name):
    def body(part_ref, land_ref, after_ref, ssem, rsem, part_thru, land_thru, token):
        x, y, c, chips = _place()
        for j, chip in enumerate(chips):
            _rs_copy(part_ref, land_ref, ssem, rsem, j, chip, c).start()
        token[...] = jnp.zeros_like(token)

    land = jax.ShapeDtypeStruct((3,) + part.shape[1:], part.dtype)
    return pl.pallas_call(
        body, name=name,
        out_shape=(pltpu.SemaphoreType.DMA((3,)), pltpu.SemaphoreType.DMA((3,)), pltpu.HBM(part.shape, part.dtype),
                   pltpu.HBM(land.shape, land.dtype), jax.ShapeDtypeStruct((8, HEAD), F32)),
        in_specs=(_HBM, _HBM, _ANY), out_specs=(_SEM, _SEM, _HBM, _HBM, pl.BlockSpec(memory_space=pltpu.VMEM)),
        input_output_aliases={0: 2, 1: 3}, compiler_params=pltpu.CompilerParams(**_SPLIT),
    )(_hbm(part), _hbm(lax.empty(land.shape, land.dtype)), after)


def _rs_wait(ssem, rsem, part, land, after, name):
    def body(part_ref, land_ref, ssem, rsem, after_ref, part_out, land_out):
        x, y, c, chips = _place()
        for j, chip in enumerate(chips):
            cp = _rs_copy(part_ref, land_ref, ssem, rsem, j, chip, c)
            cp.wait_send()
            cp.wait_recv()

    return pl.pallas_call(
        body, name=name, out_shape=(pltpu.HBM(part.shape, part.dtype), pltpu.HBM(land.shape, land.dtype)),
        in_specs=(_HBM, _HBM, _SEM, _SEM, _ANY), out_specs=(_HBM, _HBM),
        input_output_aliases={0: 0, 1: 1}, compiler_params=pltpu.CompilerParams(**_SPLIT),
    )(part, land, ssem, rsem, after)


def _sum_own_recv(part, land, me_idx, name, rows=128):
    _, half, C = part.shape

    def body(me_ref, own_ref, land_ref, o_ref):
        acc = own_ref[...].astype(F32)
        for s in range(3):
            acc = acc + land_ref[s].astype(F32)
        o_ref[...] = acc

    if half % 8:
        cols = 256
        return pl.pallas_call(
            body, name=name,
            grid_spec=pltpu.PrefetchScalarGridSpec(
                num_scalar_prefetch=1, grid=(C // cols,),
                in_specs=[pl.BlockSpec((None, half, cols), lambda j, me: (me[0], 0, j)),
                          pl.BlockSpec((3, half, cols), lambda j, me: (0, 0, j))],
                out_specs=pl.BlockSpec((half, cols), lambda j, me: (0, j))),
            out_shape=jax.ShapeDtypeStruct((half, C), F32),
            compiler_params=_cparams(("parallel",)),
        )(me_idx, part, land)
    rows = min(rows, half)
    assert half % rows == 0
    return pl.pallas_call(
        body, name=name,
        grid_spec=pltpu.PrefetchScalarGridSpec(
            num_scalar_prefetch=1, grid=(half // rows,),
            in_specs=[pl.BlockSpec((None, rows, C), lambda i, me: (me[0], i, 0)),
                      pl.BlockSpec((3, rows, C), lambda i, me: (0, i, 0))],
            out_specs=pl.BlockSpec((rows, C), lambda i, me: (i, 0))),
        out_shape=jax.ShapeDtypeStruct((half, C), F32),
        compiler_params=_cparams(("parallel",)),
    )(me_idx, part, land)


def _sum_slots(arr, name, rows=128):
    k, R, C = arr.shape
    rows = min(rows, R)
    assert R % rows == 0

    def body(in_ref, o_ref):
        acc = in_ref[0].astype(F32)
        for s in range(1, k):
            acc = acc + in_ref[s].astype(F32)
        o_ref[...] = acc

    return pl.pallas_call(
        body, name=name, grid=(R // rows,),
        in_specs=[pl.BlockSpec((k, rows, C), lambda i: (0, i, 0))],
        out_specs=pl.BlockSpec((rows, C), lambda i: (i, 0)),
        out_shape=jax.ShapeDtypeStruct((R, C), F32),
        compiler_params=_cparams(("parallel",)),
    )(arr)


ADAM_LR, ADAM_B1, ADAM_B2, ADAM_EPS, ADAM_WD, ADAM_STEP = 0.001, 0.9, 0.999, 1e-08, 0.01, 10


def _adam_math(w, g, m, v):
    m = ADAM_B1 * m + (1.0 - ADAM_B1) * g
    v = ADAM_B2 * v + (1.0 - ADAM_B2) * jnp.square(g)
    m_hat = m / (1.0 - ADAM_B1 ** ADAM_STEP)
    v_hat = v / (1.0 - ADAM_B2 ** ADAM_STEP)
    delta = -ADAM_LR * (m_hat / (jnp.sqrt(v_hat) + ADAM_EPS) + ADAM_WD * w)
    return delta, m, v


def _adamw(w, m, v, layer, mine, other, c_idx, name, g_off=0, prev=None, after=(), rows=128):
    _, R, C = w.shape
    half = mine.shape[0]
    rows = min(rows, R)
    assert R % rows == 0 and g_off % rows == 0 and half % rows == 0
    nbh, b0 = half // rows, g_off // rows

    def body(c_ref, w_ref, m_ref, v_ref, mine_ref, other_ref, *rest):
        go, do, mo, vo = rest[-4:]
        in_my_half = (b0 + pl.program_id(0)) // nbh == c_ref[0]
        g = jnp.where(in_my_half, mine_ref[...], other_ref[...])
        delta, m1, v1 = _adam_math(w_ref[...], g, m_ref[...], v_ref[...])
        go[...] = g
        do[...] = delta
        mo[...] = m1
        vo[...] = v1

    blk = pl.BlockSpec((None, rows, C), lambda i, c: (layer, i, 0))

    def gblk(of_mine):
        return pl.BlockSpec((rows, C), lambda i, c: (
            jnp.where(((b0 + i) // nbh == c[0]) == of_mine, (b0 + i) % nbh, 0), 0))

    carried = list(prev) if prev is not None else []
    return pl.pallas_call(
        body, name=name,
        grid_spec=pltpu.PrefetchScalarGridSpec(
            num_scalar_prefetch=1, grid=(R // rows,),
            in_specs=[blk] * 3 + [gblk(True), gblk(False)] + [_ANY] * (len(carried) + len(after)),
            out_specs=[blk] * 4),
        out_shape=[jax.ShapeDtypeStruct(w.shape, F32)] * 4,
        input_output_aliases={6 + k: k for k in range(len(carried))},
        compiler_params=_cparams(("parallel",)),
    )(c_idx, w, m, v, mine, other, *carried, *after)


def _adamw_t(w, m, v, mine, other, c_idx, name, rows=513, cols=128):
    R, _, C = w.shape
    nbh = (C // 2) // cols
    assert R % rows == 0 and (C // 2) % cols == 0

    def body(c_ref, w_ref, m_ref, v_ref, mine_ref, other_ref, go, do, mo, vo):
        in_my_half = pl.program_id(1) // nbh == c_ref[0]
        g = jnp.where(in_my_half, mine_ref[...], other_ref[...])
        delta, m1, v1 = _adam_math(w_ref[...], g, m_ref[...], v_ref[...])
        go[...] = g
        do[...] = delta
        mo[...] = m1
        vo[...] = v1

    blk = pl.BlockSpec((rows, 1, cols), lambda i, j, c: (i, 0, j))
    gblk = pl.BlockSpec((rows, 1, cols), lambda i, j, c: (i, 0, j % nbh))
    return pl.pallas_call(
        body, name=name,
        grid_spec=pltpu.PrefetchScalarGridSpec(
            num_scalar_prefetch=1, grid=(R // rows, C // cols), in_specs=[blk] * 3 + [gblk] * 2,
            out_specs=[blk] * 4),
        out_shape=[jax.ShapeDtypeStruct(w.shape, F32)] * 4,
        compiler_params=_cparams(("parallel", "parallel")),
    )(c_idx, w, m, v, mine, other)


_WEIGHTS = ['norm_mix', 'norm_mlp', 'norm_ple', 'w_in_e', 'w_out_e', 'hgrn_lb', 'g_norm_a', 'conv_w', 'a_log',
            'dt_bias', 'g_norm_b', 's5_a_re', 's5_a_im', 's5_b_re', 's5_b_im', 's5_c_re', 's5_c_im', 's5_d',
            's5_log_dt', 'w_glu', 'b_glu', 'w_out_o', 'w_up', 'w_down', 'w_ple_gate', 'w_ple_proj', 'final_norm']
_INPUTS = ['x', 'p'] + _WEIGHTS + ['loss_target'] + ['m_' + n for n in _WEIGHTS] + ['v_' + n for n in _WEIGHTS]
_FAMILY = [('w_up', 0, 'col', 2048), ('w_down', 0, 'row', 2048), ('w_ple_gate', 0, 'row', 512), ('w_out_e', 0, 'row', 512),
           ('w_up', 1, 'col', 2048), ('w_down', 1, 'row', 2048), ('w_ple_gate', 1, 'row', 512), ('w_glu', 0, 'row', 512),
           ('w_out_o', 0, 'row', 512)]
_PACK = {('w_up', 0): 0, ('w_down', 0): 0, ('w_ple_gate', 0): 0, ('w_out_e', 0): 0,
         ('w_up', 1): 1, ('w_down', 1): 1, ('w_ple_gate', 1): 1, ('w_glu', 0): 1, ('w_out_o', 0): 1}
_IN_PAD = 8320
_IN_BLK = 1664


def _rms_bwd_fn(x, d_hn, d_res, g):
    _, vjp = jax.vjp(_rms, x, g)
    dx, dg = vjp(d_hn.astype(F32))
    return dx + d_res, dg


def _rms_bwd_both(x, d_hn, d_res, g):
    dx, dg = _rms_bwd_fn(x, d_hn, d_res, g)
    return dx, dx, dg


def _add_res(acc, h):
    return (acc + h,)


def _pack_rows(parts, lanes=128, mult=256):
    flat = jnp.concatenate([q.reshape(-1).astype(F32) for q in parts])
    n = flat.shape[0]
    rows = -(-n // (lanes * mult)) * mult
    return jnp.pad(flat, (0, rows * lanes - n)).reshape(rows, lanes)


def _unpack_rows(buf, shapes):
    flat, out, off = buf.reshape(-1), [], 0
    for s in shapes:
        n = math.prod(s)
        out.append(flat[off:off + n].reshape(s))
        off += n
    return out


def _step(a):
    S = a['x'].shape[1]
    x, tgt = a['x'][0], a['loss_target'][0]
    xi, yi = lax.axis_index("x"), lax.axis_index("y")
    me = 2 * xi + yi
    row = lambda t: t.reshape(1, -1)

    me_idx = me.astype(jnp.int32).reshape(1)
    c_idx = lax.axis_index("c").astype(jnp.int32).reshape(1)
    small_sh = jnp.concatenate([a['conv_w'][0].reshape(-1), a['s5_d'][0], a['b_glu'][0]]).reshape(16, 256)
    rows_first = lambda t: jnp.transpose(t, (2, 0, 1))
    shards = [None, None, rows_first(a['w_in_e']).reshape(2052, D_MODEL).astype(BF16),
              a['w_ple_proj'].reshape(512, 512).astype(BF16), small_sh]
    own = lambda i: _own_slot(shards[i] if i == 2 else shards[i].reshape(2, shards[i].shape[0] // 2, shards[i].shape[1]),
                              me_idx, name=f"own_slot{i}")
    whole = lambda g, t: g.reshape((4,) + t.shape)
    first = _ag_start([own(2), own(3), own(4)], me_idx, name="gather_first_start")
    big = {nm for nm, _, _, _ in _FAMILY} | {'w_in_e', 'w_ple_proj'}
    held = ['x'] + [pre + n for n in _WEIGHTS if n not in big for pre in ('', 'm_', 'v_')]
    _, late = lax.optimization_barrier((first[3], {n: a[n] for n in held}))
    a = {**a, **late}
    x = a['x'][0]
    views, pack_of, rt = {}, {}, [0, 0]
    for nm, l, kind, r in _FAMILY:
        views[nm, l], pack_of[nm, l] = (kind, rt[_PACK[nm, l]], r), _PACK[nm, l]
        rt[_PACK[nm, l]] += r
    bufs = [(4, rt[0], PACK_COLS), (4, rt[1], PACK_COLS)]
    for nm, l, kind, r in _FAMILY:
        k = pack_of[nm, l]
        bufs[k] = _own_rows(a[nm], l, bufs[k], me_idx, views[nm, l][1], (first[3],), name=f"own_{nm}{l}")
    shards[0], shards[1] = [jax.ShapeDtypeStruct((rt[k], PACK_COLS), BF16) for k in (0, 1)]
    slots = [bufs[k].reshape(4, 2, rt[k] // 2, PACK_COLS) for k in (0, 1)]
    hn0 = _rowwise(_rms, [x], [row(a['norm_mix'][0])], [(D_MODEL, BF16)], name="rms_mix0")
    landed = _pair_forward(_ag_wait(first[0], first[1], first[2], hn0, name="gather_first_wait"),
                           name="gather_first_pass")
    g_in, g_pp, g_small = [whole(g, t) for g, t in zip(landed, shards[2:])]
    gw = {}
    W = lambda nm, l: dict(b=gw[pack_of[nm, l]], b_view=views[nm, l])
    sems0 = _ag_start([slots[0]], g_in, name="gather0_start")
    win_t = jnp.pad(g_in.reshape(8208, D_MODEL), ((0, _IN_PAD - 8208), (0, 0)))
    w_pp = g_pp.reshape(4, 2, 256, 512).transpose(1, 2, 0, 3).reshape(2, 256, D_MODEL)
    g_small = g_small.reshape(4, 4096)
    conv_full = g_small[:, :3072].reshape(4, 4, 768).transpose(1, 0, 2).reshape(4, 3072)
    s5_d_full = g_small[:, 3072:3584].reshape(1, D_MODEL)
    b_glu_full = g_small[:, 3584:].reshape(1, D_MODEL)

    lb, lb_vjp = jax.vjp(lambda hl: jax.nn.softmax(hl, axis=0)[0:1], a['hgrn_lb'])
    pad_h = lambda t: jnp.pad(t, ((0, 0), (0, HEAD - t.shape[1])))
    (arow, dtrow), gdn_vjp = jax.vjp(lambda al, dt: (pad_h(-jnp.exp(al)), pad_h(dt)), a['a_log'], a['dt_bias'])
    s5p, s5_vjp = jax.vjp(_s5_params, a['s5_a_re'][0], a['s5_a_im'][0], a['s5_log_dt'][0], a['s5_b_re'][0],
                          a['s5_b_im'][0], a['s5_c_re'][0], a['s5_c_im'][0])
    bre, bim, cre, cim = [t.astype(BF16) for t in s5p[:4]]
    pw_r, pw_i = _s5_powers(s5p[4], s5p[5])
    gna, gnb = a['g_norm_a'], a['g_norm_b']

    def norm_cast(h, g, dt, name):
        return _rowwise(_rms, [h], [row(g)], [(D_MODEL, dt)], name=name)

    def mlp_ple_fwd(h, l):
        hn = norm_cast(h, a['norm_mlp'][l], BF16, f"rms_mlp{l}")
        up, act = _mm(hn, **W('w_up', l), out_dtypes=(BF16, BF16), name=f"up{l}",
                      epilogue=lambda acc: (acc, jnp.square(jnp.maximum(acc, 0.0))))
        h2 = _mm(act, **W('w_down', l), epilogue=_add_res, extras=(h,), name=f"down{l}")
        hnp = norm_cast(h2, a['norm_ple'][l], BF16, f"rms_ple{l}")
        pp = _mm(a['p'][l, 0], w_pp[l], name=f"ple_proj{l}")
        h3, gpre = _mm(hnp, **W('w_ple_gate', l), extras=(h2, pp), out_dtypes=(F32, F32), name=f"ple_gate{l}",
                       bm=512, epilogue=lambda acc, h2, pp: (h2 + jax.nn.sigmoid(acc) * pp, acc))
        return h3, (h, hn, up, act, h2, hnp, pp, gpre)

    proj = _mm(hn0, win_t, dims="nt", bn=_IN_BLK, name="in_proj", after=(sems0[3],))
    o_a, st_a = _hgrn_fwd(proj, lb, gna)
    act_b = _conv_fwd(proj, conv_full)
    o_b, st_b = _gdn_fwd(act_b, proj, arow, dtrow, gnb)
    merged = jnp.concatenate([o_a, o_b], axis=1)
    landed0 = _ag_wait(sems0[0], sems0[1], sems0[2], merged, name="gather0_wait")
    gw[0] = whole(_pair_forward(landed0, name="gather0_pass")[0], shards[0])
    sems1 = _ag_start([slots[1]], gw[0], name="gather1_start")
    h1 = _mm(merged, **W('w_out_e', 0), epilogue=_add_res, extras=(x,), name="out_e", after=(sems1[3],))
    h3, saved0 = mlp_ple_fwd(h1, 0)

    u = norm_cast(h3, a['norm_mix'][1], F32, "rms_mix1")
    y, xre, xim = _s5_fwd(u, bre, bim, cre, cim, pw_r, pw_i, s5_d_full)
    actg = _rowwise(jax.nn.gelu, [y], [], [(D_MODEL, BF16)], name="gelu")
    landed1 = _ag_wait(sems1[0], sems1[1], sems1[2], actg, name="gather1_wait")
    gw[1] = whole(_pair_forward(landed1, name="gather1_pass")[0], shards[1])
    glu, z = _mm(actg, **W('w_glu', 0), extras=(actg, b_glu_full), out_dtypes=(BF16, F32), name="glu",
                 epilogue=lambda acc, act, b: (act.astype(F32) * jax.nn.sigmoid(acc + b), acc + b))
    h4 = _mm(glu, **W('w_out_o', 0), epilogue=_add_res, extras=(h3,), name="out_o")
    h6, saved1 = mlp_ple_fwd(h4, 1)

    def head(h, t, g):
        def f(h, g):
            e = _rms(h, g) - t
            return 0.5 * jnp.sum(jnp.mean(e * e, axis=-1))
        val, vjp = jax.vjp(f, h, g)
        dh, dg = vjp(jnp.ones((), F32))
        return dh, dg, jnp.full((1, HEAD), val)

    dh, d_final, loss_part = _rowwise(head, [h6, tgt], [row(a['final_norm'])], [(D_MODEL, F32)], sums=(D_MODEL, HEAD),
                                      name="loss_head")
    loss = lax.psum(loss_part[0, 0], ("x", "y", "c"))

    gb = {k: lax.empty((4, rt[k], PACK_COLS), BF16) for k in (0, 1)}

    def into(lhs, rhs, key, name, after=()):
        k = pack_of[key]
        gb[k] = _mm(lhs, rhs, dims="tn", out_into=gb[k], out_view=views[key], out_dtypes=(BF16,), name=name, after=after)

    def mlp_ple_bwd(dh, l, saved, after=()):
        h, hn, up, act, h2, hnp, pp, gpre = saved

        def gate_bwd(d, gpre, pp):
            s = jax.nn.sigmoid(gpre)
            return d * s, d * pp * s * (1.0 - s)

        d_pp, d_gp = _rowwise(gate_bwd, [dh, gpre, pp], [], [(D_MODEL, BF16), (D_MODEL, BF16)], name=f"ple_bwd{l}")
        d_wpp = _mm(a['p'][l, 0], d_pp, dims="tn", name=f"d_ple_proj{l}", after=after)
        into(hnp, d_gp, ('w_ple_gate', l), f"d_ple_gate{l}")
        d_hnp = _mm(d_gp, **W('w_ple_gate', l), dims="nt", name=f"ple_gate_t{l}")
        dh2, dh2_b, d_nple = _rowwise(_rms_bwd_both, [h2, d_hnp, dh], [row(a['norm_ple'][l])],
                                      [(D_MODEL, F32), (D_MODEL, BF16)], sums=(D_MODEL,), name=f"rms_ple_bwd{l}")
        d_up = _mm(dh2_b, **W('w_down', l), dims="nt", extras=(up,), out_dtypes=(BF16,), name=f"down_t{l}",
                   epilogue=lambda acc, up: (acc * 2.0 * jnp.maximum(up.astype(F32), 0.0),))
        into(act, dh2_b, ('w_down', l), f"d_down{l}")
        into(hn, d_up, ('w_up', l), f"d_up{l}")
        d_hn = _mm(d_up, **W('w_up', l), dims="nt", name=f"up_t{l}")
        dh1, dh1_b, d_nmlp = _rowwise(_rms_bwd_both, [h, d_hn, dh2], [row(a['norm_mlp'][l])],
                                      [(D_MODEL, F32), (D_MODEL, BF16)], sums=(D_MODEL,), name=f"rms_mlp_bwd{l}")
        return dh1, dh1_b, d_wpp, d_nple, d_nmlp

    dh4, dh4_b, d_wpp1, d_nple1, d_nmlp1 = mlp_ple_bwd(dh, 1, saved1)

    d_glu = _mm(dh4_b, **W('w_out_o', 0), dims="nt", name="out_o_t")
    into(glu, dh4_b, ('w_out_o', 0), "d_out_o")

    def glu_bwd(d, z, act):
        s, act = jax.nn.sigmoid(z), act.astype(F32)
        dz = d * act * s * (1.0 - s)
        return dz, d * s, jnp.sum(dz, axis=0, keepdims=True)

    d_z, d_actp, d_bglu = _rowwise(glu_bwd, [d_glu, z, actg], [], [(D_MODEL, BF16), (D_MODEL, F32)], sums=(D_MODEL,),
                                   name="glu_bwd")
    into(actg, d_z, ('w_glu', 0), "d_glu")

    def reduce_start(bufs, tag):
        parts = [t if t.shape[1] % 32 else t.reshape(4, 2, t.shape[1] // 2, t.shape[2]) for t in bufs]
        recv = _pair_split(parts, name=f"pair_split{tag}")
        sums = [_pair_sum(f, r, c_idx, name=f"pair_sum{tag}_{i}") for i, (f, r) in enumerate(zip(parts, recv))]
        started, prev = [], c_idx
        for i, s in enumerate(sums):
            started.append(_rs_start(s, prev, name=f"scatter{tag}_{i}_start"))
            prev = started[-1][4]
        return started

    def reduce_finish(started, after, tag):
        halves = []
        for i, (ssem, rsem, part, land, _) in enumerate(started):
            part, land = _rs_wait(ssem, rsem, part, land, after, name=f"scatter{tag}_{i}_wait")
            halves.append(_sum_own_recv(part, land, me_idx, name=f"sum_partials{tag}_{i}"))
        return list(zip(halves, _pair_swap(halves, name=f"swap_halves{tag}")))

    started1 = reduce_start([gb[1]], 1)

    def gelu_bwd(acc, dap, y):
        _, vjp = jax.vjp(jax.nn.gelu, y)
        return vjp(acc + dap)

    dy = _mm(d_z, **W('w_glu', 0), dims="nt", extras=(d_actp, y), epilogue=gelu_bwd, name="glu_t",
             after=(started1[0][4],))
    du, d_bre, d_bim, d_cre, d_cim, d_lr, d_li, d_s5d = _s5_bwd(dy, u, xre, xim, bre, bim, cre, cim,
                                                                 pw_r[:, ::-1], -pw_i[:, ::-1], s5_d_full)
    dh3, d_nmix1 = _rowwise(_rms_bwd_fn, [h3, du, dh4], [row(a['norm_mix'][1])], [(D_MODEL, F32)], sums=(D_MODEL,),
                            name="rms_mix1_bwd")
    d_are, d_aim, d_logdt, d_sbre, d_sbim, d_scre, d_scim = s5_vjp(
        (d_bre, d_bim, d_cre, d_cim, d_lr.sum(axis=1), d_li.sum(axis=1)))

    dh1, dh1_b, d_wpp0, d_nple0, d_nmlp0 = mlp_ple_bwd(dh3, 0, saved0)

    d_merged = _mm(dh1_b, **W('w_out_e', 0), dims="nt", name="out_e_t")
    into(merged, dh1_b, ('w_out_e', 0), "d_out_e")
    started0 = reduce_start([gb[0]], 0)
    dq, df, dv, dg, d_lb, d_gna = _hgrn_bwd(proj, lb, gna, st_a, d_merged, after=(started0[0][4],))
    dqb, dkb, dvb, dzb, d_tail, d_arow, d_dtrow, d_gnb = _gdn_bwd(act_b, proj, arow, dtrow, gnb, st_b, d_merged)
    d_pre, d_conv = _conv_bwd(proj, conv_full, dqb, dkb, dvb)
    d_proj = jnp.concatenate([dq, df, dv, dg, d_pre, dzb, d_tail], axis=1)
    d_win_t = _mm(d_proj, hn0, dims="tn", bm=_IN_BLK, out_dtypes=(BF16,), name="d_in_proj")
    d_hn0 = _mm(d_proj, win_t, bk=_IN_BLK, name="in_proj_t")
    grad_x, d_nmix0 = _rowwise(_rms_bwd_fn, [x, d_hn0, dh1], [row(a['norm_mix'][0])], [(D_MODEL, F32)], sums=(D_MODEL,),
                               name="rms_mix0_bwd")
    (d_hlb,) = lb_vjp(d_lb.reshape(1, 1024))
    d_alog, d_dtb = gdn_vjp((d_arow, d_dtrow))

    d_win_sh = d_win_t[:8208].reshape(4, 2052, D_MODEL)
    d_wpp_sh = jnp.stack([d_wpp0, d_wpp1]).astype(BF16).reshape(2, 256, 4, 512).transpose(2, 0, 1, 3).reshape(4, 512, 512)
    started2 = reduce_start([d_win_sh, d_wpp_sh], 2)
    small = {
        'norm_mix': jnp.concatenate([d_nmix0, d_nmix1]), 'norm_mlp': jnp.concatenate([d_nmlp0, d_nmlp1]),
        'norm_ple': jnp.concatenate([d_nple0, d_nple1]), 'hgrn_lb': d_hlb, 'g_norm_a': d_gna, 'conv_w': d_conv,
        'a_log': d_alog, 'dt_bias': d_dtb, 'g_norm_b': d_gnb, 's5_a_re': d_are, 's5_a_im': d_aim, 's5_b_re': d_sbre,
        's5_b_im': d_sbim, 's5_c_re': d_scre, 's5_c_im': d_scim, 's5_d': d_s5d, 's5_log_dt': d_logdt,
        'b_glu': d_bglu, 'final_norm': d_final}
    s_names = list(small)
    s_shapes = [tuple(small[n].shape) for n in s_names]
    mine_small = _pack_rows([small[n] for n in s_names])
    chip_small = _rowwise(lambda p, o: p + o, [mine_small, _pair_swap([mine_small], name="swap_small")[0]], [],
                          [(128, F32)], name="add_small")
    n_small = chip_small.shape[0]
    small_go = _ag_start([_own_slot(chip_small.reshape(2, n_small // 2, 128), me_idx, name="own_slot_small")],
                         started2[-1][4], name="gather_small_start")
    red = {1: reduce_finish(started1, small_go[3], 1)[0]}
    red[0] = reduce_finish(started0, small_go[3], 0)[0]
    res, order = {}, []

    def adam_layer(nm, layer, key, g_off=None):
        g_off = views[nm, layer][1] if g_off is None else g_off
        prev = res.get(nm)
        follows = () if not order or (prev is not None and order[-1] is prev[0]) else (order[-1],)
        res[nm] = tuple(_adamw(a[nm], a['m_' + nm], a['v_' + nm], layer, *red[key], c_idx, name=f"adam_{nm}{layer}",
                               g_off=g_off, prev=prev, after=follows))
        order.append(res[nm][0])

    for nm, layer in (('w_glu', 0), ('w_out_o', 0), ('w_up', 1), ('w_down', 1), ('w_ple_gate', 1)):
        adam_layer(nm, layer, 1)
    for nm in ('w_up', 'w_down', 'w_ple_gate', 'w_out_e'):
        adam_layer(nm, 0, 0)

    small_all = _pair_forward(_ag_wait(small_go[0], small_go[1], small_go[2], order[-1], name="gather_small_wait"),
                              name="gather_small_pass")[0]
    reduced = _sum_slots(small_all.reshape(4, n_small, 128), name="sum_small")
    sg = dict(zip(s_names, _unpack_rows(reduced, s_shapes)))
    sg['conv_w'] = lax.dynamic_slice_in_dim(sg['conv_w'], me * 768, 768, axis=1)
    sg['s5_d'] = lax.dynamic_slice_in_dim(sg['s5_d'], me * 512, 512, axis=1)
    sg['b_glu'] = lax.dynamic_slice_in_dim(sg['b_glu'], me * 512, 512, axis=1)
    sg = {n: sg[n].reshape(a[n].shape) for n in s_names}
    w_pack, m_pack, v_pack, g_pack = [_pack_rows([src[pre + n] for n in s_names])
                                      for src, pre in ((a, ''), (a, 'm_'), (a, 'v_'), (sg, ''))]
    sd, sm, sv = _rowwise(_adam_math, [w_pack, g_pack, m_pack, v_pack], [], [(128, F32)] * 3, name="adam_small")
    w_shapes = [tuple(a[n].shape) for n in s_names]
    res.update({n: (sg[n], d_, m_, v_) for n, d_, m_, v_ in zip(s_names, _unpack_rows(sd, w_shapes),
                                                               _unpack_rows(sm, w_shapes), _unpack_rows(sv, w_shapes))})

    red['in'], red['pp'] = reduce_finish(started2, sd, 2)
    res['w_in_e'] = tuple(jnp.transpose(o, (1, 2, 0)) for o in _adamw_t(
        rows_first(a['w_in_e']), rows_first(a['m_w_in_e']), rows_first(a['v_w_in_e']),
        *[h.reshape(2052, 1, D_MODEL // 2) for h in red['in']], c_idx, name="adam_w_in_e"))
    adam_layer('w_ple_proj', 0, 'pp', 0)
    adam_layer('w_ple_proj', 1, 'pp', 256)

    return (loss, grad_x[None], *[res[n][0] for n in _WEIGHTS], *[res[n][1] for n in _WEIGHTS],
            *[res[n][2] for n in _WEIGHTS], *[res[n][3] for n in _WEIGHTS])


def kernel(x, p, norm_mix, norm_mlp, norm_ple, w_in_e, w_out_e, hgrn_lb, g_norm_a, conv_w, a_log, dt_bias, g_norm_b, s5_a_re, s5_a_im, s5_b_re, s5_b_im, s5_c_re, s5_c_im, s5_d, s5_log_dt, w_glu, b_glu, w_out_o, w_up, w_down, w_ple_gate, w_ple_proj, final_norm, loss_target, m_norm_mix, m_norm_mlp, m_norm_ple, m_w_in_e, m_w_out_e, m_hgrn_lb, m_g_norm_a, m_conv_w, m_a_log, m_dt_bias, m_g_norm_b, m_s5_a_re, m_s5_a_im, m_s5_b_re, m_s5_b_im, m_s5_c_re, m_s5_c_im, m_s5_d, m_s5_log_dt, m_w_glu, m_b_glu, m_w_out_o, m_w_up, m_w_down, m_w_ple_gate, m_w_ple_proj, m_final_norm, v_norm_mix, v_norm_mlp, v_norm_ple, v_w_in_e, v_w_out_e, v_hgrn_lb, v_g_norm_a, v_conv_w, v_a_log, v_dt_bias, v_g_norm_b, v_s5_a_re, v_s5_a_im, v_s5_b_re, v_s5_b_im, v_s5_c_re, v_s5_c_im, v_s5_d, v_s5_log_dt, v_w_glu, v_b_glu, v_w_out_o, v_w_up, v_w_down, v_w_ple_gate, v_w_ple_proj, v_final_norm):
    args = (x, p, norm_mix, norm_mlp, norm_ple, w_in_e, w_out_e, hgrn_lb, g_norm_a, conv_w, a_log, dt_bias, g_norm_b, s5_a_re, s5_a_im, s5_b_re, s5_b_im, s5_c_re, s5_c_im, s5_d, s5_log_dt, w_glu, b_glu, w_out_o, w_up, w_down, w_ple_gate, w_ple_proj, final_norm, loss_target, m_norm_mix, m_norm_mlp, m_norm_ple, m_w_in_e, m_w_out_e, m_hgrn_lb, m_g_norm_a, m_conv_w, m_a_log, m_dt_bias, m_g_norm_b, m_s5_a_re, m_s5_a_im, m_s5_b_re, m_s5_b_im, m_s5_c_re, m_s5_c_im, m_s5_d, m_s5_log_dt, m_w_glu, m_b_glu, m_w_out_o, m_w_up, m_w_down, m_w_ple_gate, m_w_ple_proj, m_final_norm, v_norm_mix, v_norm_mlp, v_norm_ple, v_w_in_e, v_w_out_e, v_hgrn_lb, v_g_norm_a, v_conv_w, v_a_log, v_dt_bias, v_g_norm_b, v_s5_a_re, v_s5_a_im, v_s5_b_re, v_s5_b_im, v_s5_c_re, v_s5_c_im, v_s5_d, v_s5_log_dt, v_w_glu, v_b_glu, v_w_out_o, v_w_up, v_w_down, v_w_ple_gate, v_w_ple_proj, v_final_norm)
    return _step(dict(zip(_INPUTS, args)))
```

```python
import functools
import math

import jax
import jax.numpy as jnp
from jax import lax
from jax.experimental import pallas as pl
from jax.experimental.pallas import tpu as pltpu

F32 = jnp.float32
BF16 = jnp.bfloat16

D_MODEL = 2048
SEQ = 4096
NORM_EPS = 1e-6
CHUNK = 64
HEAD = 128
VMEM_LIMIT = 56 * 1024 * 1024


_NN = (((1,), (0,)), ((), ()))
_NT = (((1,), (1,)), ((), ()))
_TN = (((0,), (0,)), ((), ()))
_HI = lax.Precision.HIGHEST
_NEG = -1e30


def _cparams(sem, **kw):
    return pltpu.CompilerParams(dimension_semantics=sem, vmem_limit_bytes=VMEM_LIMIT, **kw)


PACK_COLS = 2048


def _view_shape(view):
    kind, _, r = view
    return (4 * r, PACK_COLS) if kind == "row" else (r, 4 * PACK_COLS)


def _view_spec(view, rb, cb, row_of, col_of):
    kind, off, r = view
    assert off % rb == 0 and r % rb == 0 and PACK_COLS % cb == 0, (view, rb, cb)
    if kind == "row":
        nrb = r // rb
        return pl.BlockSpec((None, rb, cb), lambda i, j, k: (row_of(i, j, k) // nrb,
                                                             off // rb + row_of(i, j, k) % nrb, col_of(i, j, k)))
    ncb = PACK_COLS // cb
    return pl.BlockSpec((None, rb, cb), lambda i, j, k: (col_of(i, j, k) // ncb,
                                                         off // rb + row_of(i, j, k), col_of(i, j, k) % ncb))


def _mm(a, b, *, dims="nn", epilogue=None, extras=(), out_dtypes=(F32,), bm=1024, bn=1024, bk=2048, name,
        b_view=None, out_into=None, out_view=None, after=()):
    b_shape = _view_shape(b_view) if b_view is not None else b.shape
    if dims == "tn":
        (K, M), (K2, N) = a.shape, b_shape
    elif dims == "nt":
        (M, K), (N, K2) = a.shape, b_shape
    else:
        (M, K), (K2, N) = a.shape, b_shape
    assert K == K2, (a.shape, b_shape, dims)
    if b_view is not None and dims == "nt":
        bn = min(bn, b_view[2])
    if b_view is not None and dims != "nt":
        bk = min(bk, b_view[2])
    if out_view is not None:
        bm = min(bm, out_view[2])
    bm, bn, bk = min(bm, M), min(bn, N), min(bk, K)
    assert M % bm == 0 and N % bn == 0 and K % bk == 0, (M, N, K, bm, bn, bk)
    nk = K // bk
    ii, jj, kk = (lambda i, j, k: i), (lambda i, j, k: j), (lambda i, j, k: k)
    if dims == "tn":
        a_spec = pl.BlockSpec((bk, bm), lambda i, j, k: (k, i))
        dn = _TN
    else:
        a_spec = pl.BlockSpec((bm, bk), lambda i, j, k: (i, k))
        dn = _NT if dims == "nt" else _NN
    if dims == "nt":
        b_spec = _view_spec(b_view, bn, bk, jj, kk) if b_view else pl.BlockSpec((bn, bk), lambda i, j, k: (j, k))
    else:
        b_spec = _view_spec(b_view, bk, bn, kk, jj) if b_view else pl.BlockSpec((bk, bn), lambda i, j, k: (k, j))
    e_specs = []
    for e in extras:
        if e.shape == (M, N):
            e_specs.append(pl.BlockSpec((bm, bn), lambda i, j, k: (i, j)))
        else:
            assert e.shape == (1, N), e.shape
            e_specs.append(pl.BlockSpec((1, bn), lambda i, j, k: (0, j)))
    ne, no = len(extras), len(out_dtypes)
    if epilogue is None:
        epilogue = lambda acc: (acc,)
    into = out_into is not None

    def body(a_ref, b_ref, *rest):
        e_refs, rest = rest[:ne], rest[ne + (1 if into else 0) + len(after):]
        o_refs = rest[:no]
        part = lax.dot_general(a_ref[...].astype(BF16), b_ref[...].astype(BF16), dn, preferred_element_type=F32)

        def finish(total):
            outs = epilogue(total, *[e[...] for e in e_refs])
            for o, v in zip(o_refs, outs):
                o[...] = v.astype(o.dtype)

        if nk == 1:
            finish(part)
            return
        acc, k = rest[no], pl.program_id(2)

        @pl.when(k == 0)
        def _():
            acc[...] = part

        @pl.when((k > 0) & (k < nk - 1))
        def _():
            acc[...] += part

        @pl.when(k == nk - 1)
        def _():
            finish(acc[...] + part)

    if into:
        assert no == 1 and _view_shape(out_view) == (M, N), (out_view, M, N)
        out_specs = [_view_spec(out_view, bm, bn, ii, jj)]
        out_shape = [jax.ShapeDtypeStruct(out_into.shape, out_into.dtype)]
        extra_in, extra_specs, alias = [out_into], [pl.BlockSpec(memory_space=pl.ANY)], {2 + ne: 0}
    else:
        out_specs = [pl.BlockSpec((bm, bn), lambda i, j, k: (i, j)) for _ in out_dtypes]
        out_shape = [jax.ShapeDtypeStruct((M, N), dt) for dt in out_dtypes]
        extra_in, extra_specs, alias = [], [], {}
    outs = pl.pallas_call(
        body, name=name,
        grid=(M // bm, N // bn, nk),
        in_specs=[a_spec, b_spec] + e_specs + extra_specs + [pl.BlockSpec(memory_space=pl.ANY)] * len(after),
        out_specs=out_specs, out_shape=out_shape,
        scratch_shapes=[pltpu.VMEM((bm, bn), F32)] if nk > 1 else [],
        input_output_aliases=alias,
        compiler_params=_cparams(("parallel", "parallel", "arbitrary")),
    )(a, b, *extras, *extra_in, *after)
    return outs[0] if no == 1 else tuple(outs)


def _rowwise(fn, tiled, bcast, outs, sums=(), *, rows=256, name):
    S = tiled[0].shape[0]
    rows = min(rows, S)
    assert S % rows == 0
    nt, nb, no, ns = len(tiled), len(bcast), len(outs), len(sums)

    def body(*refs):
        t_refs, b_refs = refs[:nt], refs[nt:nt + nb]
        o_refs, s_refs = refs[nt + nb:nt + nb + no], refs[nt + nb + no:]
        res = fn(*[r[...] for r in t_refs], *[r[...] for r in b_refs])
        res = res if isinstance(res, (tuple, list)) else (res,)
        for o, v in zip(o_refs, res[:no]):
            o[...] = v.astype(o.dtype)
        if ns:
            @pl.when(pl.program_id(0) == 0)
            def _():
                for s in s_refs:
                    s[...] = jnp.zeros_like(s)
            for s, v in zip(s_refs, res[no:]):
                s[...] += v

    res = pl.pallas_call(
        body, name=name,
        grid=(S // rows,),
        in_specs=[pl.BlockSpec((rows, t.shape[1]), lambda i: (i, 0)) for t in tiled]
        + [pl.BlockSpec(b.shape, lambda i, nd=b.ndim: (0,) * nd) for b in bcast],
        out_specs=[pl.BlockSpec((rows, c), lambda i: (i, 0)) for c, _ in outs]
        + [pl.BlockSpec((1, c), lambda i: (0, 0)) for c in sums],
        out_shape=[jax.ShapeDtypeStruct((S, c), dt) for c, dt in outs]
        + [jax.ShapeDtypeStruct((1, c), F32) for c in sums],
        compiler_params=_cparams(("arbitrary",)),
    )(*tiled, *bcast)
    return res[0] if len(res) == 1 else tuple(res)


def _rms(x, g):
    return x * lax.rsqrt(jnp.mean(x * x, axis=-1, keepdims=True) + NORM_EPS) * g


def _mxu(a, b, dn):
    if a.ndim == 3:
        ((ca,), (cb,)), _ = dn
        dn = (((ca + 1,), (cb + 1,)), ((0,), (0,)))
    return lax.dot_general(a, b, dn, preferred_element_type=F32)


def _split(a):
    hi = a.astype(BF16)
    return hi, (a - hi.astype(F32)).astype(BF16)


def _passes(a, b, dn, three):
    if not three:
        return _mxu(a.astype(BF16), b.astype(BF16), dn)
    (ah, al), (bh, bl) = _split(a), _split(b)
    return _mxu(ah, bh, dn) + (_mxu(ah, bl, dn) + _mxu(al, bh, dn))


def _dot_grads(a, b, g, dn, three):
    if dn == _NN:
        return _passes(g, b, _NT, three), _passes(a, g, _TN, three)
    if dn == _NT:
        return _passes(g, b, _NN, three), _passes(g, a, _TN, three)
    assert dn == _TN
    return _passes(b, g, _NT, three), _passes(a, g, _NN, three)


@functools.partial(jax.custom_vjp, nondiff_argnums=(2,))
def _dot(a, b, dn=_NN):
    return _passes(a, b, dn, False)


_dot.defvjp(lambda a, b, dn: (_passes(a, b, dn, False), (a, b)),
            lambda dn, res, g: _dot_grads(res[0], res[1], g, dn, False))


@functools.partial(jax.custom_vjp, nondiff_argnums=(2,))
def _dot3(a, b, dn=_NN):
    return _passes(a, b, dn, True)


_dot3.defvjp(lambda a, b, dn: (_passes(a, b, dn, True), (a, b)),
             lambda dn, res, g: _dot_grads(res[0], res[1], g, dn, True))


def _tri_times(x, dn):
    tri = _tril(x.shape[-2]).astype(BF16)
    if x.ndim == 3:
        tri = jnp.broadcast_to(tri, (x.shape[0],) + tri.shape)
    hi, lo = _split(x)
    lo2 = (x - hi.astype(F32) - lo.astype(F32)).astype(BF16)
    return _mxu(tri, hi, dn) + (_mxu(tri, lo, dn) + _mxu(tri, lo2, dn))


@jax.custom_vjp
def _cumsum_rows(x):
    return _tri_times(x, _NN)


_cumsum_rows.defvjp(lambda x: (_tri_times(x, _NN), None), lambda _, g: (_tri_times(g, _TN),))


def _iota(shape, dim):
    return lax.broadcasted_iota(jnp.int32, shape, dim)


def _tril(n, strict=False):
    r, c = _iota((n, n), 0), _iota((n, n), 1)
    return (r > c) if strict else (r >= c)


@functools.partial(jax.custom_vjp, nondiff_argnums=(1,))
def _roll_rows(x, r):
    return pltpu.roll(x, r, 0)


def _roll_rows_fwd(x, r):
    return pltpu.roll(x, r, 0), None


def _roll_rows_bwd(r, _, g):
    return (pltpu.roll(g, (g.shape[0] - r) % g.shape[0], 0),)


_roll_rows.defvjp(_roll_rows_fwd, _roll_rows_bwd)


def _head_norm_gate(o, gn, gate):
    return _rms(o, gn) * jax.nn.silu(gate)


_SUB = 16
_HPS = 8


def _hgrn_chunk(q, fpre, v, gate, lb, gna, st):
    c = q.shape[0]
    forget = lb + (1.0 - lb) * jax.nn.sigmoid(fpre)
    k = 1.0 - forget
    logf = jnp.log(forget)
    cum = _cumsum_rows(logf)
    cum_end = jnp.sum(logf, axis=0, keepdims=True)
    o = _dot(q * jnp.exp(cum), st, _NT)
    st_new = st * jnp.exp(cum_end) + _dot(v, k * jnp.exp(cum_end - cum), _TN)
    t = _iota((c, 1), 0)
    s_off = jnp.zeros((c, c), F32)
    for i in range(1, c // _SUB):
        before = t < i * _SUB
        c_i = jnp.sum(jnp.where(before, logf, 0.0), axis=0, keepdims=True)
        in_blk = (t >= i * _SUB) & (t < (i + 1) * _SUB)
        qi = jnp.where(in_blk, q * jnp.exp(jnp.minimum(cum - c_i, 0.0)), 0.0)
        ki = jnp.where(before, k * jnp.exp(jnp.minimum(c_i - cum, 0.0)), 0.0)
        s_off = s_off + _dot(qi, ki, _NT)
    o = o + _dot(s_off, v)
    tmod = t % _SUB
    for r in range(_SUB):
        kr, cr, vr = (k, cum, v) if r == 0 else (_roll_rows(k, r), _roll_rows(cum, r), _roll_rows(v, r))
        w = q * kr * jnp.exp(jnp.where(tmod >= r, cum - cr, _NEG))
        o = o + jnp.sum(w, axis=1, keepdims=True) * vr
    return _head_norm_gate(o, gna, gate), st_new


def _hgrn_fwd(proj, lb, gna):
    S = proj.shape[0]
    n_chunks, heads = S // CHUNK, 8

    def body(q_ref, f_ref, v_ref, g_ref, lb_ref, gna_ref, o_ref, st_out, st):
        n, hb = pl.program_id(0), pl.program_id(1)

        @pl.when(n == 0)
        def _():
            st[hb] = jnp.zeros((_HPS, HEAD, HEAD), F32)

        sls = [slice(hh * HEAD, (hh + 1) * HEAD) for hh in range(_HPS)]
        st0 = [st[hb, hh] for hh in range(_HPS)]
        ins = [(q_ref[:, sl], f_ref[:, sl], v_ref[:, sl], g_ref[:, sl], lb_ref[:, sl]) for sl in sls]
        gna = gna_ref[...]
        res = [_hgrn_chunk(*ins[hh], gna, st0[hh]) for hh in range(_HPS)]
        for hh, sl in enumerate(sls):
            st_out[hh] = st0[hh]
            o_ref[:, sl] = res[hh][0].astype(o_ref.dtype)
            st[hb, hh] = res[hh][1]

    wide = _HPS * HEAD
    sec = lambda s: pl.BlockSpec((CHUNK, wide), lambda n, h, s=s: (n, (8 // _HPS) * s + h))
    return pl.pallas_call(
        body, name="hgrn_fwd", grid=(n_chunks, heads // _HPS),
        in_specs=[sec(0), sec(1), sec(2), sec(3),
                  pl.BlockSpec((1, wide), lambda n, h: (0, h)), pl.BlockSpec((1, HEAD), lambda n, h: (0, 0))],
        out_specs=[pl.BlockSpec((CHUNK, wide), lambda n, h: (n, h)),
                   pl.BlockSpec((None, _HPS, HEAD, HEAD), lambda n, h: (n, h, 0, 0))],
        out_shape=[jax.ShapeDtypeStruct((S, 1024), BF16),
                   jax.ShapeDtypeStruct((n_chunks, heads, HEAD, HEAD), F32)],
        scratch_shapes=[pltpu.VMEM((heads // _HPS, _HPS, HEAD, HEAD), F32)],
        compiler_params=_cparams(("arbitrary", "arbitrary")),
    )(proj, proj, proj, proj, lb, gna)


def _hgrn_bwd(proj, lb, gna, states, d_o, after=()):
    S = proj.shape[0]
    n_chunks, heads = S // CHUNK, 8

    def body(q_ref, f_ref, v_ref, g_ref, lb_ref, gna_ref, st_ref, do_ref, *rest):
        dq_ref, df_ref, dv_ref, dg_ref, dlb_ref, dgna_ref, dst = rest[len(after):]
        n, hb = pl.program_id(0), pl.program_id(1)

        @pl.when(n == 0)
        def _():
            dst[hb] = jnp.zeros((_HPS, HEAD, HEAD), F32)
            dlb_ref[hb] = jnp.zeros((_HPS, 1, HEAD), F32)

        @pl.when((n == 0) & (hb == 0))
        def _():
            dgna_ref[...] = jnp.zeros_like(dgna_ref)

        sls = [slice(hh * HEAD, (hh + 1) * HEAD) for hh in range(_HPS)]
        gna = gna_ref[...]
        ins = [(q_ref[:, sl], f_ref[:, sl], v_ref[:, sl], g_ref[:, sl], lb_ref[:, sl], gna, st_ref[hh])
               for hh, sl in enumerate(sls)]
        cts = [(do_ref[:, sl], dst[hb, hh]) for hh, sl in enumerate(sls)]
        grads = [jax.vjp(_hgrn_chunk, *ins[hh])[1](cts[hh]) for hh in range(_HPS)]
        dgna_sum = jnp.zeros((1, HEAD), F32)
        for hh, sl in enumerate(sls):
            dq, df, dv, dg, dlb, dgna, dst0 = grads[hh]
            dq_ref[:, sl] = dq.astype(dq_ref.dtype)
            df_ref[:, sl] = df.astype(df_ref.dtype)
            dv_ref[:, sl] = dv.astype(dv_ref.dtype)
            dg_ref[:, sl] = dg.astype(dg_ref.dtype)
            dlb_ref[hb, hh] += dlb
            dgna_sum = dgna_sum + dgna
            dst[hb, hh] = dst0
        dgna_ref[...] += dgna_sum

    rev = lambda n: n_chunks - 1 - n
    wide = _HPS * HEAD
    sec = lambda s: pl.BlockSpec((CHUNK, wide), lambda n, h, s=s: (rev(n), (8 // _HPS) * s + h))
    out = pl.BlockSpec((CHUNK, wide), lambda n, h: (rev(n), h))
    return pl.pallas_call(
        body, name="hgrn_bwd", grid=(n_chunks, heads // _HPS),
        in_specs=[sec(0), sec(1), sec(2), sec(3),
                  pl.BlockSpec((1, wide), lambda n, h: (0, h)), pl.BlockSpec((1, HEAD), lambda n, h: (0, 0)),
                  pl.BlockSpec((None, _HPS, HEAD, HEAD), lambda n, h: (rev(n), h, 0, 0)),
                  pl.BlockSpec((CHUNK, wide), lambda n, h: (rev(n), h))] + [pl.BlockSpec(memory_space=pl.ANY)] * len(after),
        out_specs=[out, out, out, out,
                   pl.BlockSpec((heads // _HPS, _HPS, 1, HEAD), lambda n, h: (0, 0, 0, 0)),
                   pl.BlockSpec((1, HEAD), lambda n, h: (0, 0))],
        out_shape=[jax.ShapeDtypeStruct((S, 1024), BF16)] * 4
        + [jax.ShapeDtypeStruct((heads // _HPS, _HPS, 1, HEAD), F32), jax.ShapeDtypeStruct((1, HEAD), F32)],
        scratch_shapes=[pltpu.VMEM((heads // _HPS, _HPS, HEAD, HEAD), F32)],
        compiler_params=_cparams(("arbitrary", "arbitrary")),
    )(proj, proj, proj, proj, lb, gna, states, d_o, *after)


def _l2n(t):
    return t * lax.rsqrt(jnp.sum(t * t, axis=-1, keepdims=True) + NORM_EPS)


def _unit_lower_inverse(low):
    c = low.shape[-1]
    inv = (_iota((c, c), 0) == _iota((c, c), 1)).astype(F32) - low
    p = low
    span = 2
    while span < c:
        p = _dot3(p, p)
        inv = inv + _dot3(inv, p)
        span *= 2
    return inv


def _gdn_chunk(qc, kc, v, z, tail, h0, arow, dtrow, gnb, st):
    nb, c = qc.shape[0], qc.shape[1]
    head = h0 + _iota((nb, c, HEAD), 0)
    lane = _iota((nb, c, HEAD), 2)
    la_all = arow * jax.nn.softplus(tail + dtrow)
    la = jnp.sum(jnp.where(lane == head, la_all[None], 0.0), axis=2, keepdims=True)
    beta = jnp.sum(jnp.where(lane == head + 8, jax.nn.sigmoid(tail)[None], 0.0), axis=2, keepdims=True)
    la_b = jnp.broadcast_to(la, (nb, c, HEAD))
    cum = _cumsum_rows(la_b)
    cmat = _cumsum_rows(jnp.broadcast_to(la, (nb, c, c)))
    cum_end = jnp.sum(la_b, axis=1, keepdims=True)
    decay = jnp.exp(jnp.where(_tril(c), cmat - jnp.swapaxes(cmat, 1, 2), _NEG))
    q = _l2n(qc) * (HEAD ** -0.5)
    k = _l2n(kc)
    k_beta = k * beta
    low = jnp.where(_tril(c, strict=True), _dot(k_beta, k, _NT) * decay, 0.0)
    inv = _unit_lower_inverse(low)
    u, w = _dot(inv, v * beta), _dot(inv, k_beta * jnp.exp(cum))
    intra = _dot(q, k, _NT) * decay
    v_new = u - _dot(w, st, _NT)
    o = _dot(q * jnp.exp(cum), st, _NT) + _dot(intra, v_new)
    st_new = st * jnp.exp(cum_end) + _dot(v_new, k * jnp.exp(cum_end - cum), _TN)
    return _head_norm_gate(o, gnb, z), st_new


def _gdn_specs(n_of):
    wide = _HPS * HEAD
    qkv = lambda s: pl.BlockSpec((CHUNK, wide), lambda n, h, s=s: (n_of(n), (8 // _HPS) * s + h))
    return [qkv(0), qkv(1), qkv(2),
            pl.BlockSpec((CHUNK, wide), lambda n, h: (n_of(n), 56 // _HPS + h)),
            pl.BlockSpec((CHUNK, HEAD), lambda n, h: (n_of(n), 64)),
            pl.BlockSpec((1, HEAD), lambda n, h: (0, 0)), pl.BlockSpec((1, HEAD), lambda n, h: (0, 0)),
            pl.BlockSpec((1, HEAD), lambda n, h: (0, 0))]


def _gdn_fwd(act, proj, arow, dtrow, gnb):
    S = act.shape[0]
    n_chunks, heads = S // CHUNK, 8

    def body(q_ref, k_ref, v_ref, z_ref, t_ref, a_ref, dt_ref, gnb_ref, o_ref, st_out, st):
        n, hb = pl.program_id(0), pl.program_id(1)

        @pl.when(n == 0)
        def _():
            st[hb] = jnp.zeros((_HPS, HEAD, HEAD), F32)

        sls = [slice(hh * HEAD, (hh + 1) * HEAD) for hh in range(_HPS)]
        heads_of = lambda ref: jnp.stack([ref[:, sl] for sl in sls])
        st0 = st[hb]
        o, st1 = _gdn_chunk(heads_of(q_ref), heads_of(k_ref), heads_of(v_ref), heads_of(z_ref), t_ref[...], hb * _HPS,
                            a_ref[...], dt_ref[...], gnb_ref[...], st0)
        st_out[...] = st0
        st[hb] = st1
        for hh, sl in enumerate(sls):
            o_ref[:, sl] = o[hh].astype(o_ref.dtype)

    return pl.pallas_call(
        body, name="gdn_fwd", grid=(n_chunks, heads // _HPS),
        in_specs=_gdn_specs(lambda n: n),
        out_specs=[pl.BlockSpec((CHUNK, _HPS * HEAD), lambda n, h: (n, h)),
                   pl.BlockSpec((None, _HPS, HEAD, HEAD), lambda n, h: (n, h, 0, 0))],
        out_shape=[jax.ShapeDtypeStruct((S, 1024), BF16),
                   jax.ShapeDtypeStruct((n_chunks, heads, HEAD, HEAD), F32)],
        scratch_shapes=[pltpu.VMEM((heads // _HPS, _HPS, HEAD, HEAD), F32)],
        compiler_params=_cparams(("arbitrary", "arbitrary")),
    )(act, act, act, proj, proj, arow, dtrow, gnb)


def _gdn_bwd(act, proj, arow, dtrow, gnb, states, d_o):
    S = act.shape[0]
    n_chunks, heads = S // CHUNK, 8

    def body(q_ref, k_ref, v_ref, z_ref, t_ref, a_ref, dt_ref, gnb_ref, st_ref, do_ref,
             dq_ref, dk_ref, dv_ref, dz_ref, dt_out, da_ref, ddt_ref, dgnb_ref, dst, dtail_acc):
        n, hb = pl.program_id(0), pl.program_id(1)
        n_hb = heads // _HPS

        @pl.when(n == 0)
        def _():
            dst[hb] = jnp.zeros((_HPS, HEAD, HEAD), F32)

        @pl.when((n == 0) & (hb == 0))
        def _():
            da_ref[...] = jnp.zeros_like(da_ref)
            ddt_ref[...] = jnp.zeros_like(ddt_ref)
            dgnb_ref[...] = jnp.zeros_like(dgnb_ref)

        sls = [slice(hh * HEAD, (hh + 1) * HEAD) for hh in range(_HPS)]
        heads_of = lambda ref: jnp.stack([ref[:, sl] for sl in sls])
        fn = lambda qc, kc, v, z, tail, arow, dtrow, gnb, st: _gdn_chunk(qc, kc, v, z, tail, hb * _HPS, arow, dtrow,
                                                                        gnb, st)
        _, vjp = jax.vjp(fn, heads_of(q_ref), heads_of(k_ref), heads_of(v_ref), heads_of(z_ref), t_ref[...],
                         a_ref[...], dt_ref[...], gnb_ref[...], st_ref[...])
        dq, dk, dv, dz, dtail_sum, da_sum, ddt_sum, dgnb_sum, dst0 = vjp((heads_of(do_ref), dst[hb]))
        dst[hb] = dst0
        for hh, sl in enumerate(sls):
            dq_ref[:, sl] = dq[hh]
            dk_ref[:, sl] = dk[hh]
            dv_ref[:, sl] = dv[hh]
            dz_ref[:, sl] = dz[hh].astype(dz_ref.dtype)

        @pl.when(hb == 0)
        def _():
            dtail_acc[...] = dtail_sum

        @pl.when(hb > 0)
        def _():
            dtail_acc[...] += dtail_sum

        @pl.when(hb == n_hb - 1)
        def _():
            dt_out[...] = dtail_acc[...].astype(dt_out.dtype)

        da_ref[...] += da_sum
        ddt_ref[...] += ddt_sum
        dgnb_ref[...] += dgnb_sum

    rev = lambda n: n_chunks - 1 - n
    out = pl.BlockSpec((CHUNK, _HPS * HEAD), lambda n, h: (rev(n), h))
    row = pl.BlockSpec((1, HEAD), lambda n, h: (0, 0))
    return pl.pallas_call(
        body, name="gdn_bwd", grid=(n_chunks, heads // _HPS),
        in_specs=_gdn_specs(rev)
        + [pl.BlockSpec((None, _HPS, HEAD, HEAD), lambda n, h: (rev(n), h, 0, 0)),
           pl.BlockSpec((CHUNK, _HPS * HEAD), lambda n, h: (rev(n), 8 // _HPS + h))],
        out_specs=[out, out, out, out, pl.BlockSpec((CHUNK, HEAD), lambda n, h: (rev(n), 0)), row, row, row],
        out_shape=[jax.ShapeDtypeStruct((S, 1024), F32)] * 3
        + [jax.ShapeDtypeStruct((S, 1024), BF16), jax.ShapeDtypeStruct((S, HEAD), BF16)]
        + [jax.ShapeDtypeStruct((1, HEAD), F32)] * 3,
        scratch_shapes=[pltpu.VMEM((heads // _HPS, _HPS, HEAD, HEAD), F32), pltpu.VMEM((CHUNK, HEAD), F32)],
        compiler_params=_cparams(("arbitrary", "arbitrary")),
    )(act, act, act, proj, proj, arow, dtrow, gnb, states, d_o)


def _conv_silu(x, w):
    t = _iota((x.shape[0], 1), 0)
    tap = _iota(w.shape, 0)
    y = jnp.zeros_like(x)
    for r in range(4):
        w_r = jnp.sum(jnp.where(tap == 3 - r, w, 0.0), axis=0, keepdims=True)
        y = y + (x if r == 0 else jnp.where(t >= r, _roll_rows(x, r), 0.0)) * w_r
    return jax.nn.silu(y)


_CONV_COLS = 128


def _conv_fwd(proj, conv_w):
    S = proj.shape[0]
    nb = 3072 // _CONV_COLS
    off = 4096 // _CONV_COLS

    def body(x_ref, w_ref, o_ref):
        o_ref[...] = _conv_silu(x_ref[...], w_ref[...])

    return pl.pallas_call(
        body, name="conv_fwd", grid=(nb,),
        in_specs=[pl.BlockSpec((S, _CONV_COLS), lambda j: (0, off + j)), pl.BlockSpec((4, _CONV_COLS), lambda j: (0, j))],
        out_specs=pl.BlockSpec((S, _CONV_COLS), lambda j: (0, j)),
        out_shape=jax.ShapeDtypeStruct((S, 3072), F32),
        compiler_params=_cparams(("parallel",)),
    )(proj, conv_w)


def _conv_bwd(proj, conv_w, dq, dk, dv):
    S = proj.shape[0]
    nb = 3072 // _CONV_COLS
    off = 4096 // _CONV_COLS
    per = 1024 // _CONV_COLS

    def body(x_ref, w_ref, dq_ref, dk_ref, dv_ref, dx_ref, dw_ref):
        j = pl.program_id(0)
        _, vjp = jax.vjp(_conv_silu, x_ref[...], w_ref[...])
        d = jnp.where(j < per, dq_ref[...], jnp.where(j < 2 * per, dk_ref[...], dv_ref[...]))
        dx, dw = vjp(d)
        dx_ref[...] = dx.astype(dx_ref.dtype)
        dw_ref[...] = dw

    dsp = lambda s: pl.BlockSpec((S, _CONV_COLS), lambda j, s=s: (0, jnp.clip(j - s * per, 0, per - 1)))
    return pl.pallas_call(
        body, name="conv_bwd", grid=(nb,),
        in_specs=[pl.BlockSpec((S, _CONV_COLS), lambda j: (0, off + j)), pl.BlockSpec((4, _CONV_COLS), lambda j: (0, j)),
                  dsp(0), dsp(1), dsp(2)],
        out_specs=[pl.BlockSpec((S, _CONV_COLS), lambda j: (0, j)), pl.BlockSpec((4, _CONV_COLS), lambda j: (0, j))],
        out_shape=[jax.ShapeDtypeStruct((S, 3072), BF16), jax.ShapeDtypeStruct((4, 3072), F32)],
        compiler_params=_cparams(("parallel",)),
    )(proj, conv_w, dq, dk, dv)


_S5_T = 512
_S5_L = 512
_S5_NB = 16
_S5_U = 4
_S5_UB = 3


def _s5_tile(xr, xi, pw_r, pw_i, cr, ci, reverse):
    t8 = _iota((8, 1), 0)
    for sh in (1, 2, 4):
        row = (8 - sh) if reverse else (sh - 1)
        ar, ai = pw_r[row:row + 1, :], pw_i[row:row + 1, :]
        if reverse:
            keep, amt = t8 < 8 - sh, 8 - sh
        else:
            keep, amt = t8 >= sh, sh
        sr = jnp.where(keep, pltpu.roll(xr, amt, 0), 0.0)
        si = jnp.where(keep, pltpu.roll(xi, amt, 0), 0.0)
        xr, xi = xr + ar * sr - ai * si, xi + ar * si + ai * sr
    xr, xi = xr + pw_r * cr - pw_i * ci, xi + pw_r * ci + pw_i * cr
    return xr, xi


def _s5_fwd(u, bre, bim, cre, cim, pw_r, pw_i, dskip):
    S = u.shape[0]
    T = min(_S5_T, S)
    nt = S // T

    def body(u_ref, bre_ref, bim_ref, cre_ref, cim_ref, pr_ref, pi_ref, d_ref, y_ref, xr_ref, xi_ref,
             bu_r, bu_i, car_r, car_i):
        @pl.when(pl.program_id(1) == 0)
        def _():
            car_r[...] = jnp.zeros_like(car_r)
            car_i[...] = jnp.zeros_like(car_i)

        uu = u_ref[...]
        bu_r[...] = _dot(uu, bre_ref[...])
        bu_i[...] = _dot(uu, bim_ref[...])
        pw_r, pw_i = pr_ref[...], pi_ref[...]

        def tiles(i, carry):
            ins = []
            for k in range(_S5_U):
                r0 = pl.multiple_of((i * _S5_U + k) * 8, 8)
                ins.append((r0, bu_r[pl.ds(r0, 8), :], bu_i[pl.ds(r0, 8), :]))
            outs = []
            for r0, br, bi in ins:
                xr, xi = _s5_tile(br, bi, pw_r, pw_i, carry[0], carry[1], False)
                carry = (xr[7:8, :], xi[7:8, :])
                outs.append((r0, xr, xi))
            for r0, xr, xi in outs:
                xr_ref[pl.ds(r0, 8), :] = xr
                xi_ref[pl.ds(r0, 8), :] = xi
            return carry

        cr, ci = lax.fori_loop(0, T // (8 * _S5_U), tiles, (car_r[...], car_i[...]))
        car_r[...] = cr
        car_i[...] = ci
        y_ref[...] = _dot(xr_ref[...], cre_ref[...]) - _dot(xi_ref[...], cim_ref[...]) + d_ref[...] * uu

    blk3 = lambda a, b: pl.BlockSpec((None, a, b), lambda j, t: (j, 0, 0))
    return pl.pallas_call(
        body, name="s5_fwd", grid=(_S5_NB, nt),
        in_specs=[pl.BlockSpec((T, HEAD), lambda j, t: (t, j)),
                  blk3(HEAD, _S5_L), blk3(HEAD, _S5_L), blk3(_S5_L, HEAD), blk3(_S5_L, HEAD),
                  blk3(8, _S5_L), blk3(8, _S5_L), pl.BlockSpec((1, HEAD), lambda j, t: (0, j))],
        out_specs=[pl.BlockSpec((T, HEAD), lambda j, t: (t, j)),
                   pl.BlockSpec((T, _S5_L), lambda j, t: (t, j)), pl.BlockSpec((T, _S5_L), lambda j, t: (t, j))],
        out_shape=[jax.ShapeDtypeStruct((S, D_MODEL), F32),
                   jax.ShapeDtypeStruct((S, _S5_NB * _S5_L), F32), jax.ShapeDtypeStruct((S, _S5_NB * _S5_L), F32)],
        scratch_shapes=[pltpu.VMEM((T, _S5_L), F32), pltpu.VMEM((T, _S5_L), F32),
                        pltpu.VMEM((1, _S5_L), F32), pltpu.VMEM((1, _S5_L), F32)],
        compiler_params=_cparams(("parallel", "arbitrary")),
    )(u, bre, bim, cre, cim, pw_r, pw_i, dskip)


def _s5_bwd(dy, u, xre, xim, bre, bim, cre, cim, qw_r, qw_i, dskip):
    S = u.shape[0]
    T = min(_S5_T, S)
    nt = S // T
    nt8 = T // 8

    def body(dy_ref, u_ref, xr_ref, xi_ref, xpr_ref, xpi_ref, bre_ref, bim_ref, cre_ref, cim_ref, qr_ref, qi_ref,
             d_ref, du_ref, dbr_ref, dbi_ref, dcr_ref, dci_ref, dlr_ref, dli_ref, dd_ref,
             g_r, g_i, car_r, car_i):
        t = pl.program_id(1)

        @pl.when(t == 0)
        def _():
            car_r[...] = jnp.zeros_like(car_r)
            car_i[...] = jnp.zeros_like(car_i)
            for r in (dbr_ref, dbi_ref, dcr_ref, dci_ref, dlr_ref, dli_ref, dd_ref):
                r[...] = jnp.zeros_like(r)

        dyy, uu = dy_ref[...], u_ref[...]
        g_r[...] = _dot(dyy, cre_ref[...], _NT)
        g_i[...] = -_dot(dyy, cim_ref[...], _NT)
        qw_r, qw_i = qr_ref[...], qi_ref[...]
        t8 = _iota((8, 1), 0)
        first = t == nt - 1

        def load(r0, prev_r, prev_i):
            rows = pl.ds(r0, 8)
            return r0, g_r[rows, :], g_i[rows, :], xr_ref[rows, :], xi_ref[rows, :], prev_r, prev_i

        def run(loaded, carry, acc):
            done = []
            for r0, dr, di, xr, xi, prev_r, prev_i in loaded:
                gr, gi = _s5_tile(dr, di, qw_r, qw_i, carry[0], carry[1], True)
                carry = (gr[0:1, :], gi[0:1, :])
                xpr = jnp.where(t8 >= 1, pltpu.roll(xr, 1, 0), prev_r)
                xpi = jnp.where(t8 >= 1, pltpu.roll(xi, 1, 0), prev_i)
                acc = (acc[0] + gr * xpr + gi * xpi, acc[1] + gi * xpr - gr * xpi)
                done.append((r0, gr, gi))
            for r0, gr, gi in done:
                g_r[pl.ds(r0, 8), :] = gr
                g_i[pl.ds(r0, 8), :] = gi
            return carry, acc

        def step(ii, state):
            loaded = []
            for k in range(_S5_UB):
                idx = nt8 - 1 - (ii * _S5_UB + k)
                r0 = pl.multiple_of(idx * 8, 8)
                p0 = pl.multiple_of((idx - 1) * 8, 8)
                loaded.append(load(r0, xr_ref[pl.ds(p0, 8), :][7:8, :], xi_ref[pl.ds(p0, 8), :][7:8, :]))
            return run(loaded, *state)

        zero = jnp.zeros((8, _S5_L), F32)
        assert (nt8 - 1) % _S5_UB == 0
        state = lax.fori_loop(0, (nt8 - 1) // _S5_UB, step, ((car_r[...], car_i[...]), (zero, zero)))
        prev_r = jnp.where(first, 0.0, xpr_ref[...][7:8, :])
        prev_i = jnp.where(first, 0.0, xpi_ref[...][7:8, :])
        (cr, ci), (ar, ai) = run([load(0, prev_r, prev_i)], *state)
        car_r[...] = cr
        car_i[...] = ci
        dlr_ref[...] += ar
        dli_ref[...] += ai
        gr, gi = g_r[...], g_i[...]
        du_ref[...] = _dot(gr, bre_ref[...], _NT) + _dot(gi, bim_ref[...], _NT) + d_ref[...] * dyy
        dbr_ref[...] += _dot(uu, gr, _TN)
        dbi_ref[...] += _dot(uu, gi, _TN)
        dcr_ref[...] += _dot(xr_ref[...], dyy, _TN)
        dci_ref[...] -= _dot(xi_ref[...], dyy, _TN)
        dd_ref[...] += jnp.sum(dyy * uu, axis=0, keepdims=True)

    rev = lambda t: nt - 1 - t
    blk3 = lambda a, b: pl.BlockSpec((None, a, b), lambda j, t: (j, 0, 0))
    tl = pl.BlockSpec((T, HEAD), lambda j, t: (rev(t), j))
    xs = pl.BlockSpec((T, _S5_L), lambda j, t: (rev(t), j))
    xp = pl.BlockSpec((8, _S5_L), lambda j, t: (jnp.maximum(rev(t) * nt8 - 1, 0), j))
    return pl.pallas_call(
        body, name="s5_bwd", grid=(_S5_NB, nt),
        in_specs=[tl, tl, xs, xs, xp, xp, blk3(HEAD, _S5_L), blk3(HEAD, _S5_L), blk3(_S5_L, HEAD), blk3(_S5_L, HEAD),
                  blk3(8, _S5_L), blk3(8, _S5_L), pl.BlockSpec((1, HEAD), lambda j, t: (0, j))],
        out_specs=[tl, blk3(HEAD, _S5_L), blk3(HEAD, _S5_L), blk3(_S5_L, HEAD), blk3(_S5_L, HEAD),
                   blk3(8, _S5_L), blk3(8, _S5_L), pl.BlockSpec((1, HEAD), lambda j, t: (0, j))],
        out_shape=[jax.ShapeDtypeStruct((S, D_MODEL), F32),
                   jax.ShapeDtypeStruct((_S5_NB, HEAD, _S5_L), F32), jax.ShapeDtypeStruct((_S5_NB, HEAD, _S5_L), F32),
                   jax.ShapeDtypeStruct((_S5_NB, _S5_L, HEAD), F32), jax.ShapeDtypeStruct((_S5_NB, _S5_L, HEAD), F32),
                   jax.ShapeDtypeStruct((_S5_NB, 8, _S5_L), F32), jax.ShapeDtypeStruct((_S5_NB, 8, _S5_L), F32),
                   jax.ShapeDtypeStruct((1, D_MODEL), F32)],
        scratch_shapes=[pltpu.VMEM((T, _S5_L), F32), pltpu.VMEM((T, _S5_L), F32),
                        pltpu.VMEM((1, _S5_L), F32), pltpu.VMEM((1, _S5_L), F32)],
        compiler_params=_cparams(("parallel", "arbitrary")),
    )(dy, u, xre, xim, xre, xim, bre, bim, cre, cim, qw_r, qw_i, dskip)


def _s5_params(a_re, a_im, log_dt, b_re, b_im, c_re, c_im):
    step = jnp.exp(log_dt)[:, None]
    mag = jnp.exp(a_re * step)
    lr, li = mag * jnp.cos(a_im * step), mag * jnp.sin(a_im * step)
    den = a_re * a_re + a_im * a_im
    nr, ni = lr - 1.0, li
    kr, ki = (nr * a_re + ni * a_im) / den, (ni * a_re - nr * a_im) / den
    bbr = kr[..., None] * b_re - ki[..., None] * b_im
    bbi = kr[..., None] * b_im + ki[..., None] * b_re
    eye = jnp.eye(8, dtype=F32)

    def blk_b(bb):
        t = bb.reshape(_S5_NB, 8, 64, 16).transpose(0, 1, 3, 2)
        return (t[:, :, :, None, :] * eye[None, :, None, :, None]).reshape(_S5_NB, HEAD, _S5_L)

    def blk_c(cc):
        t = cc.reshape(_S5_NB, 8, 16, 64).transpose(0, 1, 3, 2)
        return (t[:, :, :, None, :] * eye[None, :, None, :, None]).reshape(_S5_NB, _S5_L, HEAD)

    return (blk_b(bbr), blk_b(bbi), blk_c(c_re), blk_c(c_im),
            lr.reshape(_S5_NB, _S5_L), li.reshape(_S5_NB, _S5_L))


def _s5_powers(lr, li):
    pr, pi = [lr], [li]
    for _ in range(7):
        pr, pi = pr + [pr[-1] * lr - pi[-1] * li], pi + [pr[-1] * li + pi[-1] * lr]
    return jnp.stack(pr, axis=1), jnp.stack(pi, axis=1)


_MESH = pl.DeviceIdType.MESH
_ANY = pl.BlockSpec(memory_space=pl.ANY)


def _place():
    x, y, c = lax.axis_index("x"), lax.axis_index("y"), lax.axis_index("c")
    return x, y, c, [(1 - x, y), (x, 1 - y), (1 - x, 1 - y)]


def _comm_call(body, arrs, out_shapes, n_remote, name):
    n = len(arrs)
    return pl.pallas_call(
        body, name=name,
        in_specs=[_ANY] * n, out_specs=[_ANY] * n, out_shape=out_shapes,
        scratch_shapes=[pltpu.SemaphoreType.DMA((n * n_remote,)), pltpu.SemaphoreType.DMA((n * n_remote,)),
                        pltpu.SemaphoreType.DMA((n,))],
    )(*arrs)


def _half(ref, slot, h):
    if len(ref.shape) == 4:
        return ref.at[slot, h]
    cols = ref.shape[2] // 2
    return ref.at[slot, :, pl.ds(pl.multiple_of(h * cols, 128), cols)]


def _own_slot(shard, me_idx, name, rows=256, cols=512):
    def body(me_ref, in_ref, o_ref):
        o_ref[...] = in_ref[...]

    if shard.ndim == 2:
        R, C = shard.shape
        return pl.pallas_call(
            body, name=name,
            grid_spec=pltpu.PrefetchScalarGridSpec(
                num_scalar_prefetch=1, grid=(C // cols,),
                in_specs=[pl.BlockSpec((R, cols), lambda j, me: (0, j))],
                out_specs=pl.BlockSpec((None, R, cols), lambda j, me: (me[0], 0, j))),
            out_shape=jax.ShapeDtypeStruct((4,) + shard.shape, shard.dtype),
            compiler_params=_cparams(("parallel",)),
        )(me_idx, shard)
    _, half, C = shard.shape
    rows = min(rows, half)
    assert half % rows == 0
    return pl.pallas_call(
        body, name=name,
        grid_spec=pltpu.PrefetchScalarGridSpec(
            num_scalar_prefetch=1, grid=(2, half // rows),
            in_specs=[pl.BlockSpec((None, rows, C), lambda h, i, me: (h, i, 0))],
            out_specs=pl.BlockSpec((None, None, rows, C), lambda h, i, me: (me[0], h, i, 0))),
        out_shape=jax.ShapeDtypeStruct((4,) + shard.shape, shard.dtype),
        compiler_params=_cparams(("parallel", "parallel")),
    )(me_idx, shard)


def _own_rows(w, layer, buf, me_idx, row_off, after, name, rows=256):
    _, r, C = w.shape
    carried = [] if isinstance(buf, tuple) else [buf]
    shape = buf if isinstance(buf, tuple) else buf.shape
    assert r % rows == 0 and row_off % rows == 0

    def body(me_ref, in_ref, *rest):
        rest[-1][...] = in_ref[...].astype(BF16)

    return pl.pallas_call(
        body, name=name,
        grid_spec=pltpu.PrefetchScalarGridSpec(
            num_scalar_prefetch=1, grid=(r // rows,),
            in_specs=[pl.BlockSpec((None, rows, C), lambda i, me: (layer, i, 0))] + [_ANY] * (len(carried) + len(after)),
            out_specs=pl.BlockSpec((None, rows, C), lambda i, me: (me[0], row_off // rows + i, 0))),
        out_shape=jax.ShapeDtypeStruct(shape, BF16),
        input_output_aliases={2: 0} if carried else {},
        compiler_params=_cparams(("parallel",)),
    )(me_idx, w, *carried, *after)


def _pair_split(arrs, name):
    n = len(arrs)

    def body(*refs):
        ins, outs, (ssem, rsem, _) = refs[:n], refs[n:2 * n], refs[2 * n:]
        x, y, c, _ = _place()
        copies = []
        for a in range(n):
            for s in range(4):
                cp = pltpu.make_async_remote_copy(src_ref=_half(ins[a], s, 1 - c), dst_ref=outs[a].at[s],
                                                  send_sem=ssem.at[4 * a + s], recv_sem=rsem.at[4 * a + s],
                                                  device_id=(x, y, 1 - c), device_id_type=_MESH)
                cp.start()
                copies.append(cp)
        for cp in copies:
            cp.wait()

    shapes = [jax.ShapeDtypeStruct((4,) + (a.shape[2:] if a.ndim == 4 else (a.shape[1], a.shape[2] // 2)), a.dtype)
              for a in arrs]
    return _comm_call(body, arrs, shapes, 4, name)


def _pair_swap(arrs, name):
    n = len(arrs)

    def body(*refs):
        ins, outs, (ssem, rsem, _) = refs[:n], refs[n:2 * n], refs[2 * n:]
        x, y, c, _ = _place()
        copies = []
        for a in range(n):
            cp = pltpu.make_async_remote_copy(src_ref=ins[a], dst_ref=outs[a], send_sem=ssem.at[a], recv_sem=rsem.at[a],
                                              device_id=(x, y, 1 - c), device_id_type=_MESH)
            cp.start()
            copies.append(cp)
        for cp in copies:
            cp.wait()

    return _comm_call(body, arrs, [jax.ShapeDtypeStruct(a.shape, a.dtype) for a in arrs], 1, name)


def _pair_sum(full, recv, c_idx, name, rows=128):
    def add(c_ref, a_ref, b_ref, o_ref):
        o_ref[...] = (a_ref[...].astype(F32) + b_ref[...].astype(F32)).astype(o_ref.dtype)

    if full.ndim == 3:
        _, R, cols = recv.shape
        blk = lambda col: pl.BlockSpec((None, R, cols), lambda s, c: (s, 0, col(c)))
        return pl.pallas_call(
            add, name=name,
            grid_spec=pltpu.PrefetchScalarGridSpec(
                num_scalar_prefetch=1, grid=(4,),
                in_specs=[blk(lambda c: c[0]), blk(lambda c: 0)], out_specs=blk(lambda c: 0)),
            out_shape=jax.ShapeDtypeStruct(recv.shape, recv.dtype),
            compiler_params=_cparams(("parallel",)),
        )(c_idx, full, recv)
    _, _, half, C = full.shape
    rows = min(rows, half)
    nb = half // rows
    assert half % rows == 0

    def body(c_ref, a_ref, b_ref, o_ref):
        o_ref[...] = (a_ref[...].astype(F32) + b_ref[...].astype(F32)).astype(o_ref.dtype)

    return pl.pallas_call(
        body, name=name,
        grid_spec=pltpu.PrefetchScalarGridSpec(
            num_scalar_prefetch=1, grid=(4, nb),
            in_specs=[pl.BlockSpec((None, None, rows, C), lambda s, i, c: (s, c[0], i, 0)),
                      pl.BlockSpec((None, rows, C), lambda s, i, c: (s, i, 0))],
            out_specs=pl.BlockSpec((None, rows, C), lambda s, i, c: (s, i, 0))),
        out_shape=jax.ShapeDtypeStruct(recv.shape, recv.dtype),
        compiler_params=_cparams(("parallel", "parallel")),
    )(c_idx, full, recv)


_HBM = pl.BlockSpec(memory_space=pltpu.HBM)
_SEM = pl.BlockSpec(memory_space=pltpu.SEMAPHORE)
_SPLIT = dict(has_side_effects=pltpu.SideEffectType.DATAFLOW_SIDE_EFFECTING)


def _hbm(t):
    return pltpu.with_memory_space_constraint(t, pltpu.HBM)


def _ag_copy(buf_ref, ssem, rsem, j, chip, c, me):
    px, py = chip
    mine = _half(buf_ref, me, c)
    return pltpu.make_async_remote_copy(src_ref=mine, dst_ref=mine, send_sem=ssem.at[j], recv_sem=rsem.at[j],
                                        device_id=(px, py, c), device_id_type=_MESH)


def _ag_start(bufs, after, name):
    n = len(bufs)

    def body(*refs):
        buf_refs, (ssem, rsem), token = refs[:n], refs[n + 1:n + 3], refs[-1]
        x, y, c, chips = _place()
        for a in range(n):
            for j, chip in enumerate(chips):
                _ag_copy(buf_refs[a], ssem, rsem, 3 * a + j, chip, c, 2 * x + y).start()
        token[...] = jnp.zeros_like(token)

    out = pl.pallas_call(
        body, name=name,
        out_shape=(pltpu.SemaphoreType.DMA((3 * n,)), pltpu.SemaphoreType.DMA((3 * n,)),
                   *[pltpu.HBM(b.shape, b.dtype) for b in bufs], jax.ShapeDtypeStruct((8, HEAD), F32)),
        in_specs=(*[_HBM] * n, _ANY), out_specs=(_SEM, _SEM, *[_HBM] * n, pl.BlockSpec(memory_space=pltpu.VMEM)),
        input_output_aliases={a: 2 + a for a in range(n)}, compiler_params=pltpu.CompilerParams(**_SPLIT),
    )(*[_hbm(b) for b in bufs], after)
    return out[0], out[1], list(out[2:2 + n]), out[-1]


def _ag_wait(ssem, rsem, bufs, after, name):
    n = len(bufs)

    def body(*refs):
        buf_refs, ssem, rsem = refs[:n], refs[n], refs[n + 1]
        x, y, c, chips = _place()
        for a in range(n):
            for j, chip in enumerate(chips):
                cp = _ag_copy(buf_refs[a], ssem, rsem, 3 * a + j, chip, c, 2 * x + y)
                cp.wait_send()
                cp.wait_recv()

    after = after if isinstance(after, (tuple, list)) else (after,)
    return list(pl.pallas_call(
        body, name=name, out_shape=tuple(pltpu.HBM(b.shape, b.dtype) for b in bufs),
        in_specs=(*[_HBM] * n, _SEM, _SEM, *[_ANY] * len(after)), out_specs=tuple([_HBM] * n),
        input_output_aliases={a: a for a in range(n)}, compiler_params=pltpu.CompilerParams(**_SPLIT),
    )(*bufs, ssem, rsem, *after))


def _pair_forward(bufs, name):
    n = len(bufs)

    def body(*refs):
        outs, (ssem, rsem) = refs[n:2 * n], refs[2 * n:]
        x, y, c, chips = _place()
        copies = []
        for a in range(n):
            for j, (px, py) in enumerate(chips):
                landed = _half(outs[a], 2 * px + py, c)
                cp = pltpu.make_async_remote_copy(src_ref=landed, dst_ref=landed, send_sem=ssem.at[3 * a + j],
                                                  recv_sem=rsem.at[3 * a + j], device_id=(x, y, 1 - c),
                                                  device_id_type=_MESH)
                cp.start()
                copies.append(cp)
        for cp in copies:
            cp.wait()

    return pl.pallas_call(
        body, name=name, in_specs=[_ANY] * n, out_specs=[_ANY] * n,
        out_shape=[jax.ShapeDtypeStruct(b.shape, b.dtype) for b in bufs],
        input_output_aliases={a: a for a in range(n)},
        scratch_shapes=[pltpu.SemaphoreType.DMA((3 * n,)), pltpu.SemaphoreType.DMA((3 * n,))],
    )(*bufs)


def _rs_copy(src_ref, land_ref, ssem, rsem, j, chip, c):
    px, py = chip
    return pltpu.make_async_remote_copy(src_ref=src_ref.at[2 * px + py], dst_ref=land_ref.at[j], send_sem=ssem.at[j],
                                        recv_sem=rsem.at[j], device_id=(px, py, c), device_id_type=_MESH)


def _rs_start(part, after, name):
    def body(part_ref, land_ref, after_ref, ssem, rsem, part_thru, land_thru, token):
        x, y, c, chips = _place()
        for j, chip in enumerate(chips):
            _rs_copy(part_ref, land_ref, ssem, rsem, j, chip, c).start()
        token[...] = jnp.zeros_like(token)

    land = jax.ShapeDtypeStruct((3,) + part.shape[1:], part.dtype)
    return pl.pallas_call(
        body, name=name,
        out_shape=(pltpu.SemaphoreType.DMA((3,)), pltpu.SemaphoreType.DMA((3,)), pltpu.HBM(part.shape, part.dtype),
                   pltpu.HBM(land.shape, land.dtype), jax.ShapeDtypeStruct((8, HEAD), F32)),
        in_specs=(_HBM, _HBM, _ANY), out_specs=(_SEM, _SEM, _HBM, _HBM, pl.BlockSpec(memory_space=pltpu.VMEM)),
        input_output_aliases={0: 2, 1: 3}, compiler_params=pltpu.CompilerParams(**_SPLIT),
    )(_hbm(part), _hbm(lax.empty(land.shape, land.dtype)), after)


def _rs_wait(ssem, rsem, part, land, after, name):
    def body(part_ref, land_ref, ssem, rsem, after_ref, part_out, land_out):
        x, y, c, chips = _place()
        for j, chip in enumerate(chips):
            cp = _rs_copy(part_ref, land_ref, ssem, rsem, j, chip, c)
            cp.wait_send()
            cp.wait_recv()

    return pl.pallas_call(
        body, name=name, out_shape=(pltpu.HBM(part.shape, part.dtype), pltpu.HBM(land.shape, land.dtype)),
        in_specs=(_HBM, _HBM, _SEM, _SEM, _ANY), out_specs=(_HBM, _HBM),
        input_output_aliases={0: 0, 1: 1}, compiler_params=pltpu.CompilerParams(**_SPLIT),
    )(part, land, ssem, rsem, after)


def _sum_own_recv(part, land, me_idx, name, rows=128):
    _, half, C = part.shape

    def body(me_ref, own_ref, land_ref, o_ref):
        acc = own_ref[...].astype(F32)
        for s in range(3):
            acc = acc + land_ref[s].astype(F32)
        o_ref[...] = acc

    if half % 8:
        cols = 256
        return pl.pallas_call(
            body, name=name,
            grid_spec=pltpu.PrefetchScalarGridSpec(
                num_scalar_prefetch=1, grid=(C // cols,),
                in_specs=[pl.BlockSpec((None, half, cols), lambda j, me: (me[0], 0, j)),
                          pl.BlockSpec((3, half, cols), lambda j, me: (0, 0, j))],
                out_specs=pl.BlockSpec((half, cols), lambda j, me: (0, j))),
            out_shape=jax.ShapeDtypeStruct((half, C), F32),
            compiler_params=_cparams(("parallel",)),
        )(me_idx, part, land)
    rows = min(rows, half)
    assert half % rows == 0
    return pl.pallas_call(
        body, name=name,
        grid_spec=pltpu.PrefetchScalarGridSpec(
            num_scalar_prefetch=1, grid=(half // rows,),
            in_specs=[pl.BlockSpec((None, rows, C), lambda i, me: (me[0], i, 0)),
                      pl.BlockSpec((3, rows, C), lambda i, me: (0, i, 0))],
            out_specs=pl.BlockSpec((rows, C), lambda i, me: (i, 0))),
        out_shape=jax.ShapeDtypeStruct((half, C), F32),
        compiler_params=_cparams(("parallel",)),
    )(me_idx, part, land)


def _sum_slots(arr, name, rows=128):
    k, R, C = arr.shape
    rows = min(rows, R)
    assert R % rows == 0

    def body(in_ref, o_ref):
        acc = in_ref[0].astype(F32)
        for s in range(1, k):
            acc = acc + in_ref[s].astype(F32)
        o_ref[...] = acc

    return pl.pallas_call(
        body, name=name, grid=(R // rows,),
        in_specs=[pl.BlockSpec((k, rows, C), lambda i: (0, i, 0))],
        out_specs=pl.BlockSpec((rows, C), lambda i: (i, 0)),
        out_shape=jax.ShapeDtypeStruct((R, C), F32),
        compiler_params=_cparams(("parallel",)),
    )(arr)


ADAM_LR, ADAM_B1, ADAM_B2, ADAM_EPS, ADAM_WD, ADAM_STEP = 0.001, 0.9, 0.999, 1e-08, 0.01, 10


def _adam_math(w, g, m, v):
    m = ADAM_B1 * m + (1.0 - ADAM_B1) * g
    v = ADAM_B2 * v + (1.0 - ADAM_B2) * jnp.square(g)
    m_hat = m / (1.0 - ADAM_B1 ** ADAM_STEP)
    v_hat = v / (1.0 - ADAM_B2 ** ADAM_STEP)
    delta = -ADAM_LR * (m_hat / (jnp.sqrt(v_hat) + ADAM_EPS) + ADAM_WD * w)
    return delta, m, v


def _adamw(w, m, v, layer, mine, other, c_idx, name, g_off=0, prev=None, after=(), rows=128):
    _, R, C = w.shape
    half = mine.shape[0]
    rows = min(rows, R)
    assert R % rows == 0 and g_off % rows == 0 and half % rows == 0
    nbh, b0 = half // rows, g_off // rows

    def body(c_ref, w_ref, m_ref, v_ref, mine_ref, other_ref, *rest):
        go, do, mo, vo = rest[-4:]
        in_my_half = (b0 + pl.program_id(0)) // nbh == c_ref[0]
        g = jnp.where(in_my_half, mine_ref[...], other_ref[...])
        delta, m1, v1 = _adam_math(w_ref[...], g, m_ref[...], v_ref[...])
        go[...] = g
        do[...] = delta
        mo[...] = m1
        vo[...] = v1

    blk = pl.BlockSpec((None, rows, C), lambda i, c: (layer, i, 0))

    def gblk(of_mine):
        return pl.BlockSpec((rows, C), lambda i, c: (
            jnp.where(((b0 + i) // nbh == c[0]) == of_mine, (b0 + i) % nbh, 0), 0))

    carried = list(prev) if prev is not None else []
    return pl.pallas_call(
        body, name=name,
        grid_spec=pltpu.PrefetchScalarGridSpec(
            num_scalar_prefetch=1, grid=(R // rows,),
            in_specs=[blk] * 3 + [gblk(True), gblk(False)] + [_ANY] * (len(carried) + len(after)),
            out_specs=[blk] * 4),
        out_shape=[jax.ShapeDtypeStruct(w.shape, F32)] * 4,
        input_output_aliases={6 + k: k for k in range(len(carried))},
        compiler_params=_cparams(("parallel",)),
    )(c_idx, w, m, v, mine, other, *carried, *after)


def _adamw_t(w, m, v, mine, other, c_idx, name, rows=513, cols=128):
    R, _, C = w.shape
    nbh = (C // 2) // cols
    assert R % rows == 0 and (C // 2) % cols == 0

    def body(c_ref, w_ref, m_ref, v_ref, mine_ref, other_ref, go, do, mo, vo):
        in_my_half = pl.program_id(1) // nbh == c_ref[0]
        g = jnp.where(in_my_half, mine_ref[...], other_ref[...])
        delta, m1, v1 = _adam_math(w_ref[...], g, m_ref[...], v_ref[...])
        go[...] = g
        do[...] = delta
        mo[...] = m1
        vo[...] = v1

    blk = pl.BlockSpec((rows, 1, cols), lambda i, j, c: (i, 0, j))
    gblk = pl.BlockSpec((rows, 1, cols), lambda i, j, c: (i, 0, j % nbh))
    return pl.pallas_call(
        body, name=name,
        grid_spec=pltpu.PrefetchScalarGridSpec(
            num_scalar_prefetch=1, grid=(R // rows, C // cols), in_specs=[blk] * 3 + [gblk] * 2,
            out_specs=[blk] * 4),
        out_shape=[jax.ShapeDtypeStruct(w.shape, F32)] * 4,
        compiler_params=_cparams(("parallel", "parallel")),
    )(c_idx, w, m, v, mine, other)


_WEIGHTS = ['norm_mix', 'norm_mlp', 'norm_ple', 'w_in_e', 'w_out_e', 'hgrn_lb', 'g_norm_a', 'conv_w', 'a_log',
            'dt_bias', 'g_norm_b', 's5_a_re', 's5_a_im', 's5_b_re', 's5_b_im', 's5_c_re', 's5_c_im', 's5_d',
            's5_log_dt', 'w_glu', 'b_glu', 'w_out_o', 'w_up', 'w_down', 'w_ple_gate', 'w_ple_proj', 'final_norm']
_INPUTS = ['x', 'p'] + _WEIGHTS + ['loss_target'] + ['m_' + n for n in _WEIGHTS] + ['v_' + n for n in _WEIGHTS]
_FAMILY = [('w_up', 0, 'col', 2048), ('w_down', 0, 'row', 2048), ('w_ple_gate', 0, 'row', 512), ('w_out_e', 0, 'row', 512),
           ('w_up', 1, 'col', 2048), ('w_down', 1, 'row', 2048), ('w_ple_gate', 1, 'row', 512), ('w_glu', 0, 'row', 512),
           ('w_out_o', 0, 'row', 512)]
_PACK = {('w_up', 0): 0, ('w_down', 0): 0, ('w_ple_gate', 0): 0, ('w_out_e', 0): 0,
         ('w_up', 1): 1, ('w_down', 1): 1, ('w_ple_gate', 1): 1, ('w_glu', 0): 1, ('w_out_o', 0): 1}
_IN_PAD = 8320
_IN_BLK = 1664


def _rms_bwd_fn(x, d_hn, d_res, g):
    _, vjp = jax.vjp(_rms, x, g)
    dx, dg = vjp(d_hn.astype(F32))
    return dx + d_res, dg


def _rms_bwd_both(x, d_hn, d_res, g):
    dx, dg = _rms_bwd_fn(x, d_hn, d_res, g)
    return dx, dx, dg


def _add_res(acc, h):
    return (acc + h,)


def _pack_rows(parts, lanes=128, mult=256):
    flat = jnp.concatenate([q.reshape(-1).astype(F32) for q in parts])
    n = flat.shape[0]
    rows = -(-n // (lanes * mult)) * mult
    return jnp.pad(flat, (0, rows * lanes - n)).reshape(rows, lanes)


def _unpack_rows(buf, shapes):
    flat, out, off = buf.reshape(-1), [], 0
    for s in shapes:
        n = math.prod(s)
        out.append(flat[off:off + n].reshape(s))
        off += n
    return out


def _step(a):
    S = a['x'].shape[1]
    x, tgt = a['x'][0], a['loss_target'][0]
    xi, yi = lax.axis_index("x"), lax.axis_index("y")
    me = 2 * xi + yi
    row = lambda t: t.reshape(1, -1)

    me_idx = me.astype(jnp.int32).reshape(1)
    c_idx = lax.axis_index("c").astype(jnp.int32).reshape(1)
    small_sh = jnp.concatenate([a['conv_w'][0].reshape(-1), a['s5_d'][0], a['b_glu'][0]]).reshape(16, 256)
    rows_first = lambda t: jnp.transpose(t, (2, 0, 1))
    shards = [None, None, rows_first(a['w_in_e']).reshape(2052, D_MODEL).astype(BF16),
              a['w_ple_proj'].reshape(512, 512).astype(BF16), small_sh]
    own = lambda i: _own_slot(shards[i] if i == 2 else shards[i].reshape(2, shards[i].shape[0] // 2, shards[i].shape[1]),
                              me_idx, name=f"own_slot{i}")
    whole = lambda g, t: g.reshape((4,) + t.shape)
    first = _ag_start([own(2), own(3), own(4)], me_idx, name="gather_first_start")
    big = {nm for nm, _, _, _ in _FAMILY} | {'w_in_e', 'w_ple_proj'}
    held = ['x'] + [pre + n for n in _WEIGHTS if n not in big for pre in ('', 'm_', 'v_')]
    _, late = lax.optimization_barrier((first[3], {n: a[n] for n in held}))
    a = {**a, **late}
    x = a['x'][0]
    views, pack_of, rt = {}, {}, [0, 0]
    for nm, l, kind, r in _FAMILY:
        views[nm, l], pack_of[nm, l] = (kind, rt[_PACK[nm, l]], r), _PACK[nm, l]
        rt[_PACK[nm, l]] += r
    bufs = [(4, rt[0], PACK_COLS), (4, rt[1], PACK_COLS)]
    for nm, l, kind, r in _FAMILY:
        k = pack_of[nm, l]
        bufs[k] = _own_rows(a[nm], l, bufs[k], me_idx, views[nm, l][1], (first[3],), name=f"own_{nm}{l}")
    shards[0], shards[1] = [jax.ShapeDtypeStruct((rt[k], PACK_COLS), BF16) for k in (0, 1)]
    slots = [bufs[k].reshape(4, 2, rt[k] // 2, PACK_COLS) for k in (0, 1)]
    hn0 = _rowwise(_rms, [x], [row(a['norm_mix'][0])], [(D_MODEL, BF16)], name="rms_mix0")
    s_names = [n for n in _WEIGHTS if n not in big]
    w_pack, m_pack, v_pack = [_pack_rows([a[pre + n] for n in s_names]) for pre in ('', 'm_', 'v_')]
    landed = _pair_forward(_ag_wait(first[0], first[1], first[2], (hn0, bufs[0], bufs[1], w_pack, m_pack, v_pack),
                                    name="gather_first_wait"), name="gather_first_pass")
    g_in, g_pp, g_small = [whole(g, t) for g, t in zip(landed, shards[2:])]
    gw = {}
    W = lambda nm, l: dict(b=gw[pack_of[nm, l]], b_view=views[nm, l])
    sems0 = _ag_start([slots[0]], g_in, name="gather0_start")
    win_t = jnp.pad(g_in.reshape(8208, D_MODEL), ((0, _IN_PAD - 8208), (0, 0)))
    w_pp = g_pp.reshape(4, 2, 256, 512).transpose(1, 2, 0, 3).reshape(2, 256, D_MODEL)
    g_small = g_small.reshape(4, 4096)
    conv_full = g_small[:, :3072].reshape(4, 4, 768).transpose(1, 0, 2).reshape(4, 3072)
    s5_d_full = g_small[:, 3072:3584].reshape(1, D_MODEL)
    b_glu_full = g_small[:, 3584:].reshape(1, D_MODEL)

    lb, lb_vjp = jax.vjp(lambda hl: jax.nn.softmax(hl, axis=0)[0:1], a['hgrn_lb'])
    pad_h = lambda t: jnp.pad(t, ((0, 0), (0, HEAD - t.shape[1])))
    (arow, dtrow), gdn_vjp = jax.vjp(lambda al, dt: (pad_h(-jnp.exp(al)), pad_h(dt)), a['a_log'], a['dt_bias'])
    s5p, s5_vjp = jax.vjp(_s5_params, a['s5_a_re'][0], a['s5_a_im'][0], a['s5_log_dt'][0], a['s5_b_re'][0],
                          a['s5_b_im'][0], a['s5_c_re'][0], a['s5_c_im'][0])
    bre, bim, cre, cim = [t.astype(BF16) for t in s5p[:4]]
    pw_r, pw_i = _s5_powers(s5p[4], s5p[5])
    gna, gnb = a['g_norm_a'], a['g_norm_b']

    def norm_cast(h, g, dt, name):
        return _rowwise(_rms, [h], [row(g)], [(D_MODEL, dt)], name=name)

    def mlp_ple_fwd(h, l):
        hn = norm_cast(h, a['norm_mlp'][l], BF16, f"rms_mlp{l}")
        up, act = _mm(hn, **W('w_up', l), out_dtypes=(BF16, BF16), name=f"up{l}",
                      epilogue=lambda acc: (acc, jnp.square(jnp.maximum(acc, 0.0))))
        h2 = _mm(act, **W('w_down', l), epilogue=_add_res, extras=(h,), name=f"down{l}")
        hnp = norm_cast(h2, a['norm_ple'][l], BF16, f"rms_ple{l}")
        pp = _mm(a['p'][l, 0], w_pp[l], name=f"ple_proj{l}")
        h3, gpre = _mm(hnp, **W('w_ple_gate', l), extras=(h2, pp), out_dtypes=(F32, F32), name=f"ple_gate{l}",
                       bm=512, epilogue=lambda acc, h2, pp: (h2 + jax.nn.sigmoid(acc) * pp, acc))
        return h3, (h, hn, up, act, h2, hnp, pp, gpre)

    proj = _mm(hn0, win_t, dims="nt", bn=_IN_BLK, name="in_proj", after=(sems0[3],))
    o_a, st_a = _hgrn_fwd(proj, lb, gna)
    act_b = _conv_fwd(proj, conv_full)
    o_b, st_b = _gdn_fwd(act_b, proj, arow, dtrow, gnb)
    merged = jnp.concatenate([o_a, o_b], axis=1)
    landed0 = _ag_wait(sems0[0], sems0[1], sems0[2], merged, name="gather0_wait")
    gw[0] = whole(_pair_forward(landed0, name="gather0_pass")[0], shards[0])
    sems1 = _ag_start([slots[1]], gw[0], name="gather1_start")
    h1 = _mm(merged, **W('w_out_e', 0), epilogue=_add_res, extras=(x,), name="out_e", after=(sems1[3],))
    h3, saved0 = mlp_ple_fwd(h1, 0)

    u = norm_cast(h3, a['norm_mix'][1], F32, "rms_mix1")
    y, xre, xim = _s5_fwd(u, bre, bim, cre, cim, pw_r, pw_i, s5_d_full)
    actg = _rowwise(jax.nn.gelu, [y], [], [(D_MODEL, BF16)], name="gelu")
    landed1 = _ag_wait(sems1[0], sems1[1], sems1[2], actg, name="gather1_wait")
    gw[1] = whole(_pair_forward(landed1, name="gather1_pass")[0], shards[1])
    glu, z = _mm(actg, **W('w_glu', 0), extras=(actg, b_glu_full), out_dtypes=(BF16, F32), name="glu",
                 epilogue=lambda acc, act, b: (act.astype(F32) * jax.nn.sigmoid(acc + b), acc + b))
    h4 = _mm(glu, **W('w_out_o', 0), epilogue=_add_res, extras=(h3,), name="out_o")
    h6, saved1 = mlp_ple_fwd(h4, 1)

    def head(h, t, g):
        def f(h, g):
            e = _rms(h, g) - t
            return 0.5 * jnp.sum(jnp.mean(e * e, axis=-1))
        val, vjp = jax.vjp(f, h, g)
        dh, dg = vjp(jnp.ones((), F32))
        return dh, dg, jnp.full((1, HEAD), val)

    dh, d_final, loss_part = _rowwise(head, [h6, tgt], [row(a['final_norm'])], [(D_MODEL, F32)], sums=(D_MODEL, HEAD),
                                      name="loss_head")
    loss = lax.psum(loss_part[0, 0], ("x", "y", "c"))

    gb = {k: lax.empty((4, rt[k], PACK_COLS), BF16) for k in (0, 1)}

    def into(lhs, rhs, key, name, after=()):
        k = pack_of[key]
        gb[k] = _mm(lhs, rhs, dims="tn", out_into=gb[k], out_view=views[key], out_dtypes=(BF16,), name=name, after=after)

    def mlp_ple_bwd(dh, l, saved, after=()):
        h, hn, up, act, h2, hnp, pp, gpre = saved

        def gate_bwd(d, gpre, pp):
            s = jax.nn.sigmoid(gpre)
            return d * s, d * pp * s * (1.0 - s)

        d_pp, d_gp = _rowwise(gate_bwd, [dh, gpre, pp], [], [(D_MODEL, BF16), (D_MODEL, BF16)], name=f"ple_bwd{l}")
        d_wpp = _mm(a['p'][l, 0], d_pp, dims="tn", name=f"d_ple_proj{l}", after=after)
        into(hnp, d_gp, ('w_ple_gate', l), f"d_ple_gate{l}")
        d_hnp = _mm(d_gp, **W('w_ple_gate', l), dims="nt", name=f"ple_gate_t{l}")
        dh2, dh2_b, d_nple = _rowwise(_rms_bwd_both, [h2, d_hnp, dh], [row(a['norm_ple'][l])],
                                      [(D_MODEL, F32), (D_MODEL, BF16)], sums=(D_MODEL,), name=f"rms_ple_bwd{l}")
        d_up = _mm(dh2_b, **W('w_down', l), dims="nt", extras=(up,), out_dtypes=(BF16,), name=f"down_t{l}",
                   epilogue=lambda acc, up: (acc * 2.0 * jnp.maximum(up.astype(F32), 0.0),))
        into(act, dh2_b, ('w_down', l), f"d_down{l}")
        into(hn, d_up, ('w_up', l), f"d_up{l}")
        d_hn = _mm(d_up, **W('w_up', l), dims="nt", name=f"up_t{l}")
        dh1, dh1_b, d_nmlp = _rowwise(_rms_bwd_both, [h, d_hn, dh2], [row(a['norm_mlp'][l])],
                                      [(D_MODEL, F32), (D_MODEL, BF16)], sums=(D_MODEL,), name=f"rms_mlp_bwd{l}")
        return dh1, dh1_b, d_wpp, d_nple, d_nmlp

    dh4, dh4_b, d_wpp1, d_nple1, d_nmlp1 = mlp_ple_bwd(dh, 1, saved1)

    d_glu = _mm(dh4_b, **W('w_out_o', 0), dims="nt", name="out_o_t")
    into(glu, dh4_b, ('w_out_o', 0), "d_out_o")

    def glu_bwd(d, z, act):
        s, act = jax.nn.sigmoid(z), act.astype(F32)
        dz = d * act * s * (1.0 - s)
        return dz, d * s, jnp.sum(dz, axis=0, keepdims=True)

    d_z, d_actp, d_bglu = _rowwise(glu_bwd, [d_glu, z, actg], [], [(D_MODEL, BF16), (D_MODEL, F32)], sums=(D_MODEL,),
                                   name="glu_bwd")
    into(actg, d_z, ('w_glu', 0), "d_glu")

    def reduce_start(bufs, tag):
        parts = [t if t.shape[1] % 32 else t.reshape(4, 2, t.shape[1] // 2, t.shape[2]) for t in bufs]
        recv = _pair_split(parts, name=f"pair_split{tag}")
        sums = [_pair_sum(f, r, c_idx, name=f"pair_sum{tag}_{i}") for i, (f, r) in enumerate(zip(parts, recv))]
        started, prev = [], c_idx
        for i, s in enumerate(sums):
            started.append(_rs_start(s, prev, name=f"scatter{tag}_{i}_start"))
            prev = started[-1][4]
        return started

    def reduce_finish(started, after, tag):
        halves = []
        for i, (ssem, rsem, part, land, _) in enumerate(started):
            part, land = _rs_wait(ssem, rsem, part, land, after, name=f"scatter{tag}_{i}_wait")
            halves.append(_sum_own_recv(part, land, me_idx, name=f"sum_partials{tag}_{i}"))
        return list(zip(halves, _pair_swap(halves, name=f"swap_halves{tag}")))

    started1 = reduce_start([gb[1]], 1)

    def gelu_bwd(acc, dap, y):
        _, vjp = jax.vjp(jax.nn.gelu, y)
        return vjp(acc + dap)

    dy = _mm(d_z, **W('w_glu', 0), dims="nt", extras=(d_actp, y), epilogue=gelu_bwd, name="glu_t",
             after=(started1[0][4],))
    du, d_bre, d_bim, d_cre, d_cim, d_lr, d_li, d_s5d = _s5_bwd(dy, u, xre, xim, bre, bim, cre, cim,
                                                                 pw_r[:, ::-1], -pw_i[:, ::-1], s5_d_full)
    dh3, d_nmix1 = _rowwise(_rms_bwd_fn, [h3, du, dh4], [row(a['norm_mix'][1])], [(D_MODEL, F32)], sums=(D_MODEL,),
                            name="rms_mix1_bwd")
    d_are, d_aim, d_logdt, d_sbre, d_sbim, d_scre, d_scim = s5_vjp(
        (d_bre, d_bim, d_cre, d_cim, d_lr.sum(axis=1), d_li.sum(axis=1)))

    dh1, dh1_b, d_wpp0, d_nple0, d_nmlp0 = mlp_ple_bwd(dh3, 0, saved0)

    d_merged = _mm(dh1_b, **W('w_out_e', 0), dims="nt", name="out_e_t")
    into(merged, dh1_b, ('w_out_e', 0), "d_out_e")
    started0 = reduce_start([gb[0]], 0)
    dq, df, dv, dg, d_lb, d_gna = _hgrn_bwd(proj, lb, gna, st_a, d_merged, after=(started0[0][4],))
    dqb, dkb, dvb, dzb, d_tail, d_arow, d_dtrow, d_gnb = _gdn_bwd(act_b, proj, arow, dtrow, gnb, st_b, d_merged)
    d_pre, d_conv = _conv_bwd(proj, conv_full, dqb, dkb, dvb)
    d_proj = jnp.concatenate([dq, df, dv, dg, d_pre, dzb, d_tail], axis=1)
    d_win_t = _mm(d_proj, hn0, dims="tn", bm=_IN_BLK, out_dtypes=(BF16,), name="d_in_proj")
    d_hn0 = _mm(d_proj, win_t, bk=_IN_BLK, name="in_proj_t")
    grad_x, d_nmix0 = _rowwise(_rms_bwd_fn, [x, d_hn0, dh1], [row(a['norm_mix'][0])], [(D_MODEL, F32)], sums=(D_MODEL,),
                               name="rms_mix0_bwd")
    (d_hlb,) = lb_vjp(d_lb.reshape(1, 1024))
    d_alog, d_dtb = gdn_vjp((d_arow, d_dtrow))

    d_win_sh = d_win_t[:8208].reshape(4, 2052, D_MODEL)
    d_wpp_sh = jnp.stack([d_wpp0, d_wpp1]).astype(BF16).reshape(2, 256, 4, 512).transpose(2, 0, 1, 3).reshape(4, 512, 512)
    started2 = reduce_start([d_win_sh, d_wpp_sh], 2)
    small = {
        'norm_mix': jnp.concatenate([d_nmix0, d_nmix1]), 'norm_mlp': jnp.concatenate([d_nmlp0, d_nmlp1]),
        'norm_ple': jnp.concatenate([d_nple0, d_nple1]), 'hgrn_lb': d_hlb, 'g_norm_a': d_gna, 'conv_w': d_conv,
        'a_log': d_alog, 'dt_bias': d_dtb, 'g_norm_b': d_gnb, 's5_a_re': d_are, 's5_a_im': d_aim, 's5_b_re': d_sbre,
        's5_b_im': d_sbim, 's5_c_re': d_scre, 's5_c_im': d_scim, 's5_d': d_s5d, 's5_log_dt': d_logdt,
        'b_glu': d_bglu, 'final_norm': d_final}
    s_names = list(small)
    s_shapes = [tuple(small[n].shape) for n in s_names]
    mine_small = _pack_rows([small[n] for n in s_names])
    chip_small = _rowwise(lambda p, o: p + o, [mine_small, _pair_swap([mine_small], name="swap_small")[0]], [],
                          [(128, F32)], name="add_small")
    n_small = chip_small.shape[0]
    small_go = _ag_start([_own_slot(chip_small.reshape(2, n_small // 2, 128), me_idx, name="own_slot_small")],
                         started2[-1][4], name="gather_small_start")
    red = {1: reduce_finish(started1, small_go[3], 1)[0]}
    red[0] = reduce_finish(started0, small_go[3], 0)[0]
    res, order = {}, []

    def adam_layer(nm, layer, key, g_off=None):
        g_off = views[nm, layer][1] if g_off is None else g_off
        prev = res.get(nm)
        follows = () if not order or (prev is not None and order[-1] is prev[0]) else (order[-1],)
        res[nm] = tuple(_adamw(a[nm], a['m_' + nm], a['v_' + nm], layer, *red[key], c_idx, name=f"adam_{nm}{layer}",
                               g_off=g_off, prev=prev, after=follows))
        order.append(res[nm][0])

    for nm, layer in (('w_glu', 0), ('w_out_o', 0), ('w_up', 1), ('w_down', 1), ('w_ple_gate', 1)):
        adam_layer(nm, layer, 1)
    for nm in ('w_up', 'w_down', 'w_ple_gate', 'w_out_e'):
        adam_layer(nm, 0, 0)

    small_all = _pair_forward(_ag_wait(small_go[0], small_go[1], small_go[2], order[-1], name="gather_small_wait"),
                              name="gather_small_pass")[0]
    reduced = _sum_slots(small_all.reshape(4, n_small, 128), name="sum_small")
    sg = dict(zip(s_names, _unpack_rows(reduced, s_shapes)))
    sg['conv_w'] = lax.dynamic_slice_in_dim(sg['conv_w'], me * 768, 768, axis=1)
    sg['s5_d'] = lax.dynamic_slice_in_dim(sg['s5_d'], me * 512, 512, axis=1)
    sg['b_glu'] = lax.dynamic_slice_in_dim(sg['b_glu'], me * 512, 512, axis=1)
    sg = {n: sg[n].reshape(a[n].shape) for n in s_names}
    g_pack = _pack_rows([sg[n] for n in s_names])
    sd, sm, sv = _rowwise(_adam_math, [w_pack, g_pack, m_pack, v_pack], [], [(128, F32)] * 3, name="adam_small")
    w_shapes = [tuple(a[n].shape) for n in s_names]
    res.update({n: (sg[n], d_, m_, v_) for n, d_, m_, v_ in zip(s_names, _unpack_rows(sd, w_shapes),
                                                               _unpack_rows(sm, w_shapes), _unpack_rows(sv, w_shapes))})

    red['in'], red['pp'] = reduce_finish(started2, sd, 2)
    res['w_in_e'] = tuple(jnp.transpose(o, (1, 2, 0)) for o in _adamw_t(
        rows_first(a['w_in_e']), rows_first(a['m_w_in_e']), rows_first(a['v_w_in_e']),
        *[h.reshape(2052, 1, D_MODEL // 2) for h in red['in']], c_idx, name="adam_w_in_e"))
    adam_layer('w_ple_proj', 0, 'pp', 0)
    adam_layer('w_ple_proj', 1, 'pp', 256)

    return (loss, grad_x[None], *[res[n][0] for n in _WEIGHTS], *[res[n][1] for n in _WEIGHTS],
            *[res[n][2] for n in _WEIGHTS], *[res[n][3] for n in _WEIGHTS])


def kernel(x, p, norm_mix, norm_mlp, norm_ple, w_in_e, w_out_e, hgrn_lb, g_norm_a, conv_w, a_log, dt_bias, g_norm_b, s5_a_re, s5_a_im, s5_b_re, s5_b_im, s5_c_re, s5_c_im, s5_d, s5_log_dt, w_glu, b_glu, w_out_o, w_up, w_down, w_ple_gate, w_ple_proj, final_norm, loss_target, m_norm_mix, m_norm_mlp, m_norm_ple, m_w_in_e, m_w_out_e, m_hgrn_lb, m_g_norm_a, m_conv_w, m_a_log, m_dt_bias, m_g_norm_b, m_s5_a_re, m_s5_a_im, m_s5_b_re, m_s5_b_im, m_s5_c_re, m_s5_c_im, m_s5_d, m_s5_log_dt, m_w_glu, m_b_glu, m_w_out_o, m_w_up, m_w_down, m_w_ple_gate, m_w_ple_proj, m_final_norm, v_norm_mix, v_norm_mlp, v_norm_ple, v_w_in_e, v_w_out_e, v_hgrn_lb, v_g_norm_a, v_conv_w, v_a_log, v_dt_bias, v_g_norm_b, v_s5_a_re, v_s5_a_im, v_s5_b_re, v_s5_b_im, v_s5_c_re, v_s5_c_im, v_s5_d, v_s5_log_dt, v_w_glu, v_b_glu, v_w_out_o, v_w_up, v_w_down, v_w_ple_gate, v_w_ple_proj, v_final_norm):
    args = (x, p, norm_mix, norm_mlp, norm_ple, w_in_e, w_out_e, hgrn_lb, g_norm_a, conv_w, a_log, dt_bias, g_norm_b, s5_a_re, s5_a_im, s5_b_re, s5_b_im, s5_c_re, s5_c_im, s5_d, s5_log_dt, w_glu, b_glu, w_out_o, w_up, w_down, w_ple_gate, w_ple_proj, final_norm, loss_target, m_norm_mix, m_norm_mlp, m_norm_ple, m_w_in_e, m_w_out_e, m_hgrn_lb, m_g_norm_a, m_conv_w, m_a_log, m_dt_bias, m_g_norm_b, m_s5_a_re, m_s5_a_im, m_s5_b_re, m_s5_b_im, m_s5_c_re, m_s5_c_im, m_s5_d, m_s5_log_dt, m_w_glu, m_b_glu, m_w_out_o, m_w_up, m_w_down, m_w_ple_gate, m_w_ple_proj, m_final_norm, v_norm_mix, v_norm_mlp, v_norm_ple, v_w_in_e, v_w_out_e, v_hgrn_lb, v_g_norm_a, v_conv_w, v_a_log, v_dt_bias, v_g_norm_b, v_s5_a_re, v_s5_a_im, v_s5_b_re, v_s5_b_im, v_s5_c_re, v_s5_c_im, v_s5_d, v_s5_log_dt, v_w_glu, v_b_glu, v_w_out_o, v_w_up, v_w_down, v_w_ple_gate, v_w_ple_proj, v_final_norm)
    return _step(dict(zip(_INPUTS, args)))
```

```python
import functools
import math

import jax
import jax.numpy as jnp
from jax import lax
from jax.experimental import pallas as pl
from jax.experimental.pallas import tpu as pltpu

F32 = jnp.float32
BF16 = jnp.bfloat16

D_MODEL = 2048
SEQ = 4096
NORM_EPS = 1e-6
CHUNK = 64
HEAD = 128
VMEM_LIMIT = 56 * 1024 * 1024


_NN = (((1,), (0,)), ((), ()))
_NT = (((1,), (1,)), ((), ()))
_TN = (((0,), (0,)), ((), ()))
_HI = lax.Precision.HIGHEST
_NEG = -1e30


def _cparams(sem, **kw):
    return pltpu.CompilerParams(dimension_semantics=sem, vmem_limit_bytes=VMEM_LIMIT, **kw)


PACK_COLS = 2048


def _view_shape(view):
    kind, _, r = view
    return (4 * r, PACK_COLS) if kind == "row" else (r, 4 * PACK_COLS)


def _view_spec(view, rb, cb, row_of, col_of):
    kind, off, r = view
    assert off % rb == 0 and r % rb == 0 and PACK_COLS % cb == 0, (view, rb, cb)
    if kind == "row":
        nrb = r // rb
        return pl.BlockSpec((None, rb, cb), lambda i, j, k: (row_of(i, j, k) // nrb,
                                                             off // rb + row_of(i, j, k) % nrb, col_of(i, j, k)))
    ncb = PACK_COLS // cb
    return pl.BlockSpec((None, rb, cb), lambda i, j, k: (col_of(i, j, k) // ncb,
                                                         off // rb + row_of(i, j, k), col_of(i, j, k) % ncb))


def _mm(a, b, *, dims="nn", epilogue=None, extras=(), out_dtypes=(F32,), bm=1024, bn=1024, bk=2048, name,
        b_view=None, out_into=None, out_view=None, after=()):
    b_shape = _view_shape(b_view) if b_view is not None else b.shape
    if dims == "tn":
        (K, M), (K2, N) = a.shape, b_shape
    elif dims == "nt":
        (M, K), (N, K2) = a.shape, b_shape
    else:
        (M, K), (K2, N) = a.shape, b_shape
    assert K == K2, (a.shape, b_shape, dims)
    if b_view is not None and dims == "nt":
        bn = min(bn, b_view[2])
    if b_view is not None and dims != "nt":
        bk = min(bk, b_view[2])
    if out_view is not None:
        bm = min(bm, out_view[2])
    bm, bn, bk = min(bm, M), min(bn, N), min(bk, K)
    assert M % bm == 0 and N % bn == 0 and K % bk == 0, (M, N, K, bm, bn, bk)
    nk = K // bk
    ii, jj, kk = (lambda i, j, k: i), (lambda i, j, k: j), (lambda i, j, k: k)
    if dims == "tn":
        a_spec = pl.BlockSpec((bk, bm), lambda i, j, k: (k, i))
        dn = _TN
    else:
        a_spec = pl.BlockSpec((bm, bk), lambda i, j, k: (i, k))
        dn = _NT if dims == "nt" else _NN
    if dims == "nt":
        b_spec = _view_spec(b_view, bn, bk, jj, kk) if b_view else pl.BlockSpec((bn, bk), lambda i, j, k: (j, k))
    else:
        b_spec = _view_spec(b_view, bk, bn, kk, jj) if b_view else pl.BlockSpec((bk, bn), lambda i, j, k: (k, j))
    e_specs = []
    for e in extras:
        if e.shape == (M, N):
            e_specs.append(pl.BlockSpec((bm, bn), lambda i, j, k: (i, j)))
        else:
            assert e.shape == (1, N), e.shape
            e_specs.append(pl.BlockSpec((1, bn), lambda i, j, k: (0, j)))
    ne, no = len(extras), len(out_dtypes)
    if epilogue is None:
        epilogue = lambda acc: (acc,)
    into = out_into is not None

    def body(a_ref, b_ref, *rest):
        e_refs, rest = rest[:ne], rest[ne + (1 if into else 0) + len(after):]
        o_refs = rest[:no]
        part = lax.dot_general(a_ref[...].astype(BF16), b_ref[...].astype(BF16), dn, preferred_element_type=F32)

        def finish(total):
            outs = epilogue(total, *[e[...] for e in e_refs])
            for o, v in zip(o_refs, outs):
                o[...] = v.astype(o.dtype)

        if nk == 1:
            finish(part)
            return
        acc, k = rest[no], pl.program_id(2)

        @pl.when(k == 0)
        def _():
            acc[...] = part

        @pl.when((k > 0) & (k < nk - 1))
        def _():
            acc[...] += part

        @pl.when(k == nk - 1)
        def _():
            finish(acc[...] + part)

    if into:
        assert no == 1 and _view_shape(out_view) == (M, N), (out_view, M, N)
        out_specs = [_view_spec(out_view, bm, bn, ii, jj)]
        out_shape = [jax.ShapeDtypeStruct(out_into.shape, out_into.dtype)]
        extra_in, extra_specs, alias = [out_into], [pl.BlockSpec(memory_space=pl.ANY)], {2 + ne: 0}
    else:
        out_specs = [pl.BlockSpec((bm, bn), lambda i, j, k: (i, j)) for _ in out_dtypes]
        out_shape = [jax.ShapeDtypeStruct((M, N), dt) for dt in out_dtypes]
        extra_in, extra_specs, alias = [], [], {}
    outs = pl.pallas_call(
        body, name=name,
        grid=(M // bm, N // bn, nk),
        in_specs=[a_spec, b_spec] + e_specs + extra_specs + [pl.BlockSpec(memory_space=pl.ANY)] * len(after),
        out_specs=out_specs, out_shape=out_shape,
        scratch_shapes=[pltpu.VMEM((bm, bn), F32)] if nk > 1 else [],
        input_output_aliases=alias,
        compiler_params=_cparams(("parallel", "parallel", "arbitrary")),
    )(a, b, *extras, *extra_in, *after)
    return outs[0] if no == 1 else tuple(outs)


def _rowwise(fn, tiled, bcast, outs, sums=(), *, rows=256, name):
    S = tiled[0].shape[0]
    rows = min(rows, S)
    assert S % rows == 0
    nt, nb, no, ns = len(tiled), len(bcast), len(outs), len(sums)

    def body(*refs):
        t_refs, b_refs = refs[:nt], refs[nt:nt + nb]
        o_refs, s_refs = refs[nt + nb:nt + nb + no], refs[nt + nb + no:]
        res = fn(*[r[...] for r in t_refs], *[r[...] for r in b_refs])
        res = res if isinstance(res, (tuple, list)) else (res,)
        for o, v in zip(o_refs, res[:no]):
            o[...] = v.astype(o.dtype)
        if ns:
            @pl.when(pl.program_id(0) == 0)
            def _():
                for s in s_refs:
                    s[...] = jnp.zeros_like(s)
            for s, v in zip(s_refs, res[no:]):
                s[...] += v

    res = pl.pallas_call(
        body, name=name,
        grid=(S // rows,),
        in_specs=[pl.BlockSpec((rows, t.shape[1]), lambda i: (i, 0)) for t in tiled]
        + [pl.BlockSpec(b.shape, lambda i, nd=b.ndim: (0,) * nd) for b in bcast],
        out_specs=[pl.BlockSpec((rows, c), lambda i: (i, 0)) for c, _ in outs]
        + [pl.BlockSpec((1, c), lambda i: (0, 0)) for c in sums],
        out_shape=[jax.ShapeDtypeStruct((S, c), dt) for c, dt in outs]
        + [jax.ShapeDtypeStruct((1, c), F32) for c in sums],
        compiler_params=_cparams(("arbitrary",)),
    )(*tiled, *bcast)
    return res[0] if len(res) == 1 else tuple(res)


def _rms(x, g):
    return x * lax.rsqrt(jnp.mean(x * x, axis=-1, keepdims=True) + NORM_EPS) * g


def _mxu(a, b, dn):
    if a.ndim == 3:
        ((ca,), (cb,)), _ = dn
        dn = (((ca + 1,), (cb + 1,)), ((0,), (0,)))
    return lax.dot_general(a, b, dn, preferred_element_type=F32)


def _split(a):
    hi = a.astype(BF16)
    return hi, (a - hi.astype(F32)).astype(BF16)


def _passes(a, b, dn, three):
    if not three:
        return _mxu(a.astype(BF16), b.astype(BF16), dn)
    (ah, al), (bh, bl) = _split(a), _split(b)
    return _mxu(ah, bh, dn) + (_mxu(ah, bl, dn) + _mxu(al, bh, dn))


def _dot_grads(a, b, g, dn, three):
    if dn == _NN:
        return _passes(g, b, _NT, three), _passes(a, g, _TN, three)
    if dn == _NT:
        return _passes(g, b, _NN, three), _passes(g, a, _TN, three)
    assert dn == _TN
    return _passes(b, g, _NT, three), _passes(a, g, _NN, three)


@functools.partial(jax.custom_vjp, nondiff_argnums=(2,))
def _dot(a, b, dn=_NN):
    return _passes(a, b, dn, False)


_dot.defvjp(lambda a, b, dn: (_passes(a, b, dn, False), (a, b)),
            lambda dn, res, g: _dot_grads(res[0], res[1], g, dn, False))


@functools.partial(jax.custom_vjp, nondiff_argnums=(2,))
def _dot3(a, b, dn=_NN):
    return _passes(a, b, dn, True)


_dot3.defvjp(lambda a, b, dn: (_passes(a, b, dn, True), (a, b)),
             lambda dn, res, g: _dot_grads(res[0], res[1], g, dn, True))


def _tri_times(x, dn):
    tri = _tril(x.shape[-2]).astype(BF16)
    if x.ndim == 3:
        tri = jnp.broadcast_to(tri, (x.shape[0],) + tri.shape)
    hi, lo = _split(x)
    lo2 = (x - hi.astype(F32) - lo.astype(F32)).astype(BF16)
    return _mxu(tri, hi, dn) + (_mxu(tri, lo, dn) + _mxu(tri, lo2, dn))


@jax.custom_vjp
def _cumsum_rows(x):
    return _tri_times(x, _NN)


_cumsum_rows.defvjp(lambda x: (_tri_times(x, _NN), None), lambda _, g: (_tri_times(g, _TN),))


def _iota(shape, dim):
    return lax.broadcasted_iota(jnp.int32, shape, dim)


def _tril(n, strict=False):
    r, c = _iota((n, n), 0), _iota((n, n), 1)
    return (r > c) if strict else (r >= c)


@functools.partial(jax.custom_vjp, nondiff_argnums=(1,))
def _roll_rows(x, r):
    return pltpu.roll(x, r, 0)


def _roll_rows_fwd(x, r):
    return pltpu.roll(x, r, 0), None


def _roll_rows_bwd(r, _, g):
    return (pltpu.roll(g, (g.shape[0] - r) % g.shape[0], 0),)


_roll_rows.defvjp(_roll_rows_fwd, _roll_rows_bwd)


def _head_norm_gate(o, gn, gate):
    return _rms(o, gn) * jax.nn.silu(gate)


_SUB = 16
_HPS = 8


def _hgrn_chunk(q, fpre, v, gate, lb, gna, st):
    c = q.shape[0]
    forget = lb + (1.0 - lb) * jax.nn.sigmoid(fpre)
    k = 1.0 - forget
    logf = jnp.log(forget)
    cum = _cumsum_rows(logf)
    cum_end = jnp.sum(logf, axis=0, keepdims=True)
    o = _dot(q * jnp.exp(cum), st, _NT)
    st_new = st * jnp.exp(cum_end) + _dot(v, k * jnp.exp(cum_end - cum), _TN)
    t = _iota((c, 1), 0)
    s_off = jnp.zeros((c, c), F32)
    for i in range(1, c // _SUB):
        before = t < i * _SUB
        c_i = jnp.sum(jnp.where(before, logf, 0.0), axis=0, keepdims=True)
        in_blk = (t >= i * _SUB) & (t < (i + 1) * _SUB)
        qi = jnp.where(in_blk, q * jnp.exp(jnp.minimum(cum - c_i, 0.0)), 0.0)
        ki = jnp.where(before, k * jnp.exp(jnp.minimum(c_i - cum, 0.0)), 0.0)
        s_off = s_off + _dot(qi, ki, _NT)
    o = o + _dot(s_off, v)
    tmod = t % _SUB
    for r in range(_SUB):
        kr, cr, vr = (k, cum, v) if r == 0 else (_roll_rows(k, r), _roll_rows(cum, r), _roll_rows(v, r))
        w = q * kr * jnp.exp(jnp.where(tmod >= r, cum - cr, _NEG))
        o = o + jnp.sum(w, axis=1, keepdims=True) * vr
    return _head_norm_gate(o, gna, gate), st_new


def _hgrn_fwd(proj, lb, gna):
    S = proj.shape[0]
    n_chunks, heads = S // CHUNK, 8

    def body(q_ref, f_ref, v_ref, g_ref, lb_ref, gna_ref, o_ref, st_out, st):
        n, hb = pl.program_id(0), pl.program_id(1)

        @pl.when(n == 0)
        def _():
            st[hb] = jnp.zeros((_HPS, HEAD, HEAD), F32)

        sls = [slice(hh * HEAD, (hh + 1) * HEAD) for hh in range(_HPS)]
        st0 = [st[hb, hh] for hh in range(_HPS)]
        ins = [(q_ref[:, sl], f_ref[:, sl], v_ref[:, sl], g_ref[:, sl], lb_ref[:, sl]) for sl in sls]
        gna = gna_ref[...]
        res = [_hgrn_chunk(*ins[hh], gna, st0[hh]) for hh in range(_HPS)]
        for hh, sl in enumerate(sls):
            st_out[hh] = st0[hh]
            o_ref[:, sl] = res[hh][0].astype(o_ref.dtype)
            st[hb, hh] = res[hh][1]

    wide = _HPS * HEAD
    sec = lambda s: pl.BlockSpec((CHUNK, wide), lambda n, h, s=s: (n, (8 // _HPS) * s + h))
    return pl.pallas_call(
        body, name="hgrn_fwd", grid=(n_chunks, heads // _HPS),
        in_specs=[sec(0), sec(1), sec(2), sec(3),
                  pl.BlockSpec((1, wide), lambda n, h: (0, h)), pl.BlockSpec((1, HEAD), lambda n, h: (0, 0))],
        out_specs=[pl.BlockSpec((CHUNK, wide), lambda n, h: (n, h)),
                   pl.BlockSpec((None, _HPS, HEAD, HEAD), lambda n, h: (n, h, 0, 0))],
        out_shape=[jax.ShapeDtypeStruct((S, 1024), BF16),
                   jax.ShapeDtypeStruct((n_chunks, heads, HEAD, HEAD), F32)],
        scratch_shapes=[pltpu.VMEM((heads // _HPS, _HPS, HEAD, HEAD), F32)],
        compiler_params=_cparams(("arbitrary", "arbitrary")),
    )(proj, proj, proj, proj, lb, gna)


def _hgrn_bwd(proj, lb, gna, states, d_o, after=()):
    S = proj.shape[0]
    n_chunks, heads = S // CHUNK, 8

    def body(q_ref, f_ref, v_ref, g_ref, lb_ref, gna_ref, st_ref, do_ref, *rest):
        dq_ref, df_ref, dv_ref, dg_ref, dlb_ref, dgna_ref, dst = rest[len(after):]
        n, hb = pl.program_id(0), pl.program_id(1)

        @pl.when(n == 0)
        def _():
            dst[hb] = jnp.zeros((_HPS, HEAD, HEAD), F32)
            dlb_ref[hb] = jnp.zeros((_HPS, 1, HEAD), F32)

        @pl.when((n == 0) & (hb == 0))
        def _():
            dgna_ref[...] = jnp.zeros_like(dgna_ref)

        sls = [slice(hh * HEAD, (hh + 1) * HEAD) for hh in range(_HPS)]
        gna = gna_ref[...]
        ins = [(q_ref[:, sl], f_ref[:, sl], v_ref[:, sl], g_ref[:, sl], lb_ref[:, sl], gna, st_ref[hh])
               for hh, sl in enumerate(sls)]
        cts = [(do_ref[:, sl], dst[hb, hh]) for hh, sl in enumerate(sls)]
        grads = [jax.vjp(_hgrn_chunk, *ins[hh])[1](cts[hh]) for hh in range(_HPS)]
        dgna_sum = jnp.zeros((1, HEAD), F32)
        for hh, sl in enumerate(sls):
            dq, df, dv, dg, dlb, dgna, dst0 = grads[hh]
            dq_ref[:, sl] = dq.astype(dq_ref.dtype)
            df_ref[:, sl] = df.astype(df_ref.dtype)
            dv_ref[:, sl] = dv.astype(dv_ref.dtype)
            dg_ref[:, sl] = dg.astype(dg_ref.dtype)
            dlb_ref[hb, hh] += dlb
            dgna_sum = dgna_sum + dgna
            dst[hb, hh] = dst0
        dgna_ref[...] += dgna_sum

    rev = lambda n: n_chunks - 1 - n
    wide = _HPS * HEAD
    sec = lambda s: pl.BlockSpec((CHUNK, wide), lambda n, h, s=s: (rev(n), (8 // _HPS) * s + h))
    out = pl.BlockSpec((CHUNK, wide), lambda n, h: (rev(n), h))
    return pl.pallas_call(
        body, name="hgrn_bwd", grid=(n_chunks, heads // _HPS),
        in_specs=[sec(0), sec(1), sec(2), sec(3),
                  pl.BlockSpec((1, wide), lambda n, h: (0, h)), pl.BlockSpec((1, HEAD), lambda n, h: (0, 0)),
                  pl.BlockSpec((None, _HPS, HEAD, HEAD), lambda n, h: (rev(n), h, 0, 0)),
                  pl.BlockSpec((CHUNK, wide), lambda n, h: (rev(n), h))] + [pl.BlockSpec(memory_space=pl.ANY)] * len(after),
        out_specs=[out, out, out, out,
                   pl.BlockSpec((heads // _HPS, _HPS, 1, HEAD), lambda n, h: (0, 0, 0, 0)),
                   pl.BlockSpec((1, HEAD), lambda n, h: (0, 0))],
        out_shape=[jax.ShapeDtypeStruct((S, 1024), BF16)] * 4
        + [jax.ShapeDtypeStruct((heads // _HPS, _HPS, 1, HEAD), F32), jax.ShapeDtypeStruct((1, HEAD), F32)],
        scratch_shapes=[pltpu.VMEM((heads // _HPS, _HPS, HEAD, HEAD), F32)],
        compiler_params=_cparams(("arbitrary", "arbitrary")),
    )(proj, proj, proj, proj, lb, gna, states, d_o, *after)


def _l2n(t):
    return t * lax.rsqrt(jnp.sum(t * t, axis=-1, keepdims=True) + NORM_EPS)


def _unit_lower_inverse(low):
    c = low.shape[-1]
    inv = (_iota((c, c), 0) == _iota((c, c), 1)).astype(F32) - low
    p = low
    span = 2
    while span < c:
        p = _dot3(p, p)
        inv = inv + _dot3(inv, p)
        span *= 2
    return inv


def _gdn_chunk(qc, kc, v, z, tail, h0, arow, dtrow, gnb, st):
    nb, c = qc.shape[0], qc.shape[1]
    head = h0 + _iota((nb, c, HEAD), 0)
    lane = _iota((nb, c, HEAD), 2)
    la_all = arow * jax.nn.softplus(tail + dtrow)
    la = jnp.sum(jnp.where(lane == head, la_all[None], 0.0), axis=2, keepdims=True)
    beta = jnp.sum(jnp.where(lane == head + 8, jax.nn.sigmoid(tail)[None], 0.0), axis=2, keepdims=True)
    la_b = jnp.broadcast_to(la, (nb, c, HEAD))
    cum = _cumsum_rows(la_b)
    cmat = _cumsum_rows(jnp.broadcast_to(la, (nb, c, c)))
    cum_end = jnp.sum(la_b, axis=1, keepdims=True)
    decay = jnp.exp(jnp.where(_tril(c), cmat - jnp.swapaxes(cmat, 1, 2), _NEG))
    q = _l2n(qc) * (HEAD ** -0.5)
    k = _l2n(kc)
    k_beta = k * beta
    low = jnp.where(_tril(c, strict=True), _dot(k_beta, k, _NT) * decay, 0.0)
    inv = _unit_lower_inverse(low)
    u, w = _dot(inv, v * beta), _dot(inv, k_beta * jnp.exp(cum))
    intra = _dot(q, k, _NT) * decay
    v_new = u - _dot(w, st, _NT)
    o = _dot(q * jnp.exp(cum), st, _NT) + _dot(intra, v_new)
    st_new = st * jnp.exp(cum_end) + _dot(v_new, k * jnp.exp(cum_end - cum), _TN)
    return _head_norm_gate(o, gnb, z), st_new


def _gdn_specs(n_of):
    wide = _HPS * HEAD
    qkv = lambda s: pl.BlockSpec((CHUNK, wide), lambda n, h, s=s: (n_of(n), (8 // _HPS) * s + h))
    return [qkv(0), qkv(1), qkv(2),
            pl.BlockSpec((CHUNK, wide), lambda n, h: (n_of(n), 56 // _HPS + h)),
            pl.BlockSpec((CHUNK, HEAD), lambda n, h: (n_of(n), 64)),
            pl.BlockSpec((1, HEAD), lambda n, h: (0, 0)), pl.BlockSpec((1, HEAD), lambda n, h: (0, 0)),
            pl.BlockSpec((1, HEAD), lambda n, h: (0, 0))]


def _gdn_fwd(act, proj, arow, dtrow, gnb):
    S = act.shape[0]
    n_chunks, heads = S // CHUNK, 8

    def body(q_ref, k_ref, v_ref, z_ref, t_ref, a_ref, dt_ref, gnb_ref, o_ref, st_out, st):
        n, hb = pl.program_id(0), pl.program_id(1)

        @pl.when(n == 0)
        def _():
            st[hb] = jnp.zeros((_HPS, HEAD, HEAD), F32)

        sls = [slice(hh * HEAD, (hh + 1) * HEAD) for hh in range(_HPS)]
        heads_of = lambda ref: jnp.stack([ref[:, sl] for sl in sls])
        st0 = st[hb]
        o, st1 = _gdn_chunk(heads_of(q_ref), heads_of(k_ref), heads_of(v_ref), heads_of(z_ref), t_ref[...], hb * _HPS,
                            a_ref[...], dt_ref[...], gnb_ref[...], st0)
        st_out[...] = st0
        st[hb] = st1
        for hh, sl in enumerate(sls):
            o_ref[:, sl] = o[hh].astype(o_ref.dtype)

    return pl.pallas_call(
        body, name="gdn_fwd", grid=(n_chunks, heads // _HPS),
        in_specs=_gdn_specs(lambda n: n),
        out_specs=[pl.BlockSpec((CHUNK, _HPS * HEAD), lambda n, h: (n, h)),
                   pl.BlockSpec((None, _HPS, HEAD, HEAD), lambda n, h: (n, h, 0, 0))],
        out_shape=[jax.ShapeDtypeStruct((S, 1024), BF16),
                   jax.ShapeDtypeStruct((n_chunks, heads, HEAD, HEAD), F32)],
        scratch_shapes=[pltpu.VMEM((heads // _HPS, _HPS, HEAD, HEAD), F32)],
        compiler_params=_cparams(("arbitrary", "arbitrary")),
    )(act, act, act, proj, proj, arow, dtrow, gnb)


def _gdn_bwd(act, proj, arow, dtrow, gnb, states, d_o):
    S = act.shape[0]
    n_chunks, heads = S // CHUNK, 8

    def body(q_ref, k_ref, v_ref, z_ref, t_ref, a_ref, dt_ref, gnb_ref, st_ref, do_ref,
             dq_ref, dk_ref, dv_ref, dz_ref, dt_out, da_ref, ddt_ref, dgnb_ref, dst, dtail_acc):
        n, hb = pl.program_id(0), pl.program_id(1)
        n_hb = heads // _HPS

        @pl.when(n == 0)
        def _():
            dst[hb] = jnp.zeros((_HPS, HEAD, HEAD), F32)

        @pl.when((n == 0) & (hb == 0))
        def _():
            da_ref[...] = jnp.zeros_like(da_ref)
            ddt_ref[...] = jnp.zeros_like(ddt_ref)
            dgnb_ref[...] = jnp.zeros_like(dgnb_ref)

        sls = [slice(hh * HEAD, (hh + 1) * HEAD) for hh in range(_HPS)]
        heads_of = lambda ref: jnp.stack([ref[:, sl] for sl in sls])
        fn = lambda qc, kc, v, z, tail, arow, dtrow, gnb, st: _gdn_chunk(qc, kc, v, z, tail, hb * _HPS, arow, dtrow,
                                                                        gnb, st)
        _, vjp = jax.vjp(fn, heads_of(q_ref), heads_of(k_ref), heads_of(v_ref), heads_of(z_ref), t_ref[...],
                         a_ref[...], dt_ref[...], gnb_ref[...], st_ref[...])
        dq, dk, dv, dz, dtail_sum, da_sum, ddt_sum, dgnb_sum, dst0 = vjp((heads_of(do_ref), dst[hb]))
        dst[hb] = dst0
        for hh, sl in enumerate(sls):
            dq_ref[:, sl] = dq[hh]
            dk_ref[:, sl] = dk[hh]
            dv_ref[:, sl] = dv[hh]
            dz_ref[:, sl] = dz[hh].astype(dz_ref.dtype)

        @pl.when(hb == 0)
        def _():
            dtail_acc[...] = dtail_sum

        @pl.when(hb > 0)
        def _():
            dtail_acc[...] += dtail_sum

        @pl.when(hb == n_hb - 1)
        def _():
            dt_out[...] = dtail_acc[...].astype(dt_out.dtype)

        da_ref[...] += da_sum
        ddt_ref[...] += ddt_sum
        dgnb_ref[...] += dgnb_sum

    rev = lambda n: n_chunks - 1 - n
    out = pl.BlockSpec((CHUNK, _HPS * HEAD), lambda n, h: (rev(n), h))
    row = pl.BlockSpec((1, HEAD), lambda n, h: (0, 0))
    return pl.pallas_call(
        body, name="gdn_bwd", grid=(n_chunks, heads // _HPS),
        in_specs=_gdn_specs(rev)
        + [pl.BlockSpec((None, _HPS, HEAD, HEAD), lambda n, h: (rev(n), h, 0, 0)),
           pl.BlockSpec((CHUNK, _HPS * HEAD), lambda n, h: (rev(n), 8 // _HPS + h))],
        out_specs=[out, out, out, out, pl.BlockSpec((CHUNK, HEAD), lambda n, h: (rev(n), 0)), row, row, row],
        out_shape=[jax.ShapeDtypeStruct((S, 1024), F32)] * 3
        + [jax.ShapeDtypeStruct((S, 1024), BF16), jax.ShapeDtypeStruct((S, HEAD), BF16)]
        + [jax.ShapeDtypeStruct((1, HEAD), F32)] * 3,
        scratch_shapes=[pltpu.VMEM((heads // _HPS, _HPS, HEAD, HEAD), F32), pltpu.VMEM((CHUNK, HEAD), F32)],
        compiler_params=_cparams(("arbitrary", "arbitrary")),
    )(act, act, act, proj, proj, arow, dtrow, gnb, states, d_o)


def _conv_silu(x, w):
    t = _iota((x.shape[0], 1), 0)
    tap = _iota(w.shape, 0)
    y = jnp.zeros_like(x)
    for r in range(4):
        w_r = jnp.sum(jnp.where(tap == 3 - r, w, 0.0), axis=0, keepdims=True)
        y = y + (x if r == 0 else jnp.where(t >= r, _roll_rows(x, r), 0.0)) * w_r
    return jax.nn.silu(y)


_CONV_COLS = 128


def _conv_fwd(proj, conv_w):
    S = proj.shape[0]
    nb = 3072 // _CONV_COLS
    off = 4096 // _CONV_COLS

    def body(x_ref, w_ref, o_ref):
        o_ref[...] = _conv_silu(x_ref[...], w_ref[...])

    return pl.pallas_call(
        body, name="conv_fwd", grid=(nb,),
        in_specs=[pl.BlockSpec((S, _CONV_COLS), lambda j: (0, off + j)), pl.BlockSpec((4, _CONV_COLS), lambda j: (0, j))],
        out_specs=pl.BlockSpec((S, _CONV_COLS), lambda j: (0, j)),
        out_shape=jax.ShapeDtypeStruct((S, 3072), F32),
        compiler_params=_cparams(("parallel",)),
    )(proj, conv_w)


def _conv_bwd(proj, conv_w, dq, dk, dv):
    S = proj.shape[0]
    nb = 3072 // _CONV_COLS
    off = 4096 // _CONV_COLS
    per = 1024 // _CONV_COLS

    def body(x_ref, w_ref, dq_ref, dk_ref, dv_ref, dx_ref, dw_ref):
        j = pl.program_id(0)
        _, vjp = jax.vjp(_conv_silu, x_ref[...], w_ref[...])
        d = jnp.where(j < per, dq_ref[...], jnp.where(j < 2 * per, dk_ref[...], dv_ref[...]))
        dx, dw = vjp(d)
        dx_ref[...] = dx.astype(dx_ref.dtype)
        dw_ref[...] = dw

    dsp = lambda s: pl.BlockSpec((S, _CONV_COLS), lambda j, s=s: (0, jnp.clip(j - s * per, 0, per - 1)))
    return pl.pallas_call(
        body, name="conv_bwd", grid=(nb,),
        in_specs=[pl.BlockSpec((S, _CONV_COLS), lambda j: (0, off + j)), pl.BlockSpec((4, _CONV_COLS), lambda j: (0, j)),
                  dsp(0), dsp(1), dsp(2)],
        out_specs=[pl.BlockSpec((S, _CONV_COLS), lambda j: (0, j)), pl.BlockSpec((4, _CONV_COLS), lambda j: (0, j))],
        out_shape=[jax.ShapeDtypeStruct((S, 3072), BF16), jax.ShapeDtypeStruct((4, 3072), F32)],
        compiler_params=_cparams(("parallel",)),
    )(proj, conv_w, dq, dk, dv)


_S5_T = 512
_S5_L = 512
_S5_NB = 16
_S5_U = 4
_S5_UB = 3


def _s5_tile(xr, xi, pw_r, pw_i, cr, ci, reverse):
    t8 = _iota((8, 1), 0)
    for sh in (1, 2, 4):
        row = (8 - sh) if reverse else (sh - 1)
        ar, ai = pw_r[row:row + 1, :], pw_i[row:row + 1, :]
        if reverse:
            keep, amt = t8 < 8 - sh, 8 - sh
        else:
            keep, amt = t8 >= sh, sh
        sr = jnp.where(keep, pltpu.roll(xr, amt, 0), 0.0)
        si = jnp.where(keep, pltpu.roll(xi, amt, 0), 0.0)
        xr, xi = xr + ar * sr - ai * si, xi + ar * si + ai * sr
    xr, xi = xr + pw_r * cr - pw_i * ci, xi + pw_r * ci + pw_i * cr
    return xr, xi


def _s5_fwd(u, bre, bim, cre, cim, pw_r, pw_i, dskip):
    S = u.shape[0]
    T = min(_S5_T, S)
    nt = S // T

    def body(u_ref, bre_ref, bim_ref, cre_ref, cim_ref, pr_ref, pi_ref, d_ref, y_ref, xr_ref, xi_ref,
             bu_r, bu_i, car_r, car_i):
        @pl.when(pl.program_id(1) == 0)
        def _():
            car_r[...] = jnp.zeros_like(car_r)
            car_i[...] = jnp.zeros_like(car_i)

        uu = u_ref[...]
        bu_r[...] = _dot(uu, bre_ref[...])
        bu_i[...] = _dot(uu, bim_ref[...])
        pw_r, pw_i = pr_ref[...], pi_ref[...]

        def tiles(i, carry):
            ins = []
            for k in range(_S5_U):
                r0 = pl.multiple_of((i * _S5_U + k) * 8, 8)
                ins.append((r0, bu_r[pl.ds(r0, 8), :], bu_i[pl.ds(r0, 8), :]))
            outs = []
            for r0, br, bi in ins:
                xr, xi = _s5_tile(br, bi, pw_r, pw_i, carry[0], carry[1], False)
                carry = (xr[7:8, :], xi[7:8, :])
                outs.append((r0, xr, xi))
            for r0, xr, xi in outs:
                xr_ref[pl.ds(r0, 8), :] = xr
                xi_ref[pl.ds(r0, 8), :] = xi
            return carry

        cr, ci = lax.fori_loop(0, T // (8 * _S5_U), tiles, (car_r[...], car_i[...]))
        car_r[...] = cr
        car_i[...] = ci
        y_ref[...] = _dot(xr_ref[...], cre_ref[...]) - _dot(xi_ref[...], cim_ref[...]) + d_ref[...] * uu

    blk3 = lambda a, b: pl.BlockSpec((None, a, b), lambda j, t: (j, 0, 0))
    return pl.pallas_call(
        body, name="s5_fwd", grid=(_S5_NB, nt),
        in_specs=[pl.BlockSpec((T, HEAD), lambda j, t: (t, j)),
                  blk3(HEAD, _S5_L), blk3(HEAD, _S5_L), blk3(_S5_L, HEAD), blk3(_S5_L, HEAD),
                  blk3(8, _S5_L), blk3(8, _S5_L), pl.BlockSpec((1, HEAD), lambda j, t: (0, j))],
        out_specs=[pl.BlockSpec((T, HEAD), lambda j, t: (t, j)),
                   pl.BlockSpec((T, _S5_L), lambda j, t: (t, j)), pl.BlockSpec((T, _S5_L), lambda j, t: (t, j))],
        out_shape=[jax.ShapeDtypeStruct((S, D_MODEL), F32),
                   jax.ShapeDtypeStruct((S, _S5_NB * _S5_L), F32), jax.ShapeDtypeStruct((S, _S5_NB * _S5_L), F32)],
        scratch_shapes=[pltpu.VMEM((T, _S5_L), F32), pltpu.VMEM((T, _S5_L), F32),
                        pltpu.VMEM((1, _S5_L), F32), pltpu.VMEM((1, _S5_L), F32)],
        compiler_params=_cparams(("parallel", "arbitrary")),
    )(u, bre, bim, cre, cim, pw_r, pw_i, dskip)


def _s5_bwd(dy, u, xre, xim, bre, bim, cre, cim, qw_r, qw_i, dskip):
    S = u.shape[0]
    T = min(_S5_T, S)
    nt = S // T
    nt8 = T // 8

    def body(dy_ref, u_ref, xr_ref, xi_ref, xpr_ref, xpi_ref, bre_ref, bim_ref, cre_ref, cim_ref, qr_ref, qi_ref,
             d_ref, du_ref, dbr_ref, dbi_ref, dcr_ref, dci_ref, dlr_ref, dli_ref, dd_ref,
             g_r, g_i, car_r, car_i):
        t = pl.program_id(1)

        @pl.when(t == 0)
        def _():
            car_r[...] = jnp.zeros_like(car_r)
            car_i[...] = jnp.zeros_like(car_i)
            for r in (dbr_ref, dbi_ref, dcr_ref, dci_ref, dlr_ref, dli_ref, dd_ref):
                r[...] = jnp.zeros_like(r)

        dyy, uu = dy_ref[...], u_ref[...]
        g_r[...] = _dot(dyy, cre_ref[...], _NT)
        g_i[...] = -_dot(dyy, cim_ref[...], _NT)
        qw_r, qw_i = qr_ref[...], qi_ref[...]
        t8 = _iota((8, 1), 0)
        first = t == nt - 1

        def load(r0, prev_r, prev_i):
            rows = pl.ds(r0, 8)
            return r0, g_r[rows, :], g_i[rows, :], xr_ref[rows, :], xi_ref[rows, :], prev_r, prev_i

        def run(loaded, carry, acc):
            done = []
            for r0, dr, di, xr, xi, prev_r, prev_i in loaded:
                gr, gi = _s5_tile(dr, di, qw_r, qw_i, carry[0], carry[1], True)
                carry = (gr[0:1, :], gi[0:1, :])
                xpr = jnp.where(t8 >= 1, pltpu.roll(xr, 1, 0), prev_r)
                xpi = jnp.where(t8 >= 1, pltpu.roll(xi, 1, 0), prev_i)
                acc = (acc[0] + gr * xpr + gi * xpi, acc[1] + gi * xpr - gr * xpi)
                done.append((r0, gr, gi))
            for r0, gr, gi in done:
                g_r[pl.ds(r0, 8), :] = gr
                g_i[pl.ds(r0, 8), :] = gi
            return carry, acc

        def step(ii, state):
            loaded = []
            for k in range(_S5_UB):
                idx = nt8 - 1 - (ii * _S5_UB + k)
                r0 = pl.multiple_of(idx * 8, 8)
                p0 = pl.multiple_of((idx - 1) * 8, 8)
                loaded.append(load(r0, xr_ref[pl.ds(p0, 8), :][7:8, :], xi_ref[pl.ds(p0, 8), :][7:8, :]))
            return run(loaded, *state)

        zero = jnp.zeros((8, _S5_L), F32)
        assert (nt8 - 1) % _S5_UB == 0
        state = lax.fori_loop(0, (nt8 - 1) // _S5_UB, step, ((car_r[...], car_i[...]), (zero, zero)))
        prev_r = jnp.where(first, 0.0, xpr_ref[...][7:8, :])
        prev_i = jnp.where(first, 0.0, xpi_ref[...][7:8, :])
        (cr, ci), (ar, ai) = run([load(0, prev_r, prev_i)], *state)
        car_r[...] = cr
        car_i[...] = ci
        dlr_ref[...] += ar
        dli_ref[...] += ai
        gr, gi = g_r[...], g_i[...]
        du_ref[...] = _dot(gr, bre_ref[...], _NT) + _dot(gi, bim_ref[...], _NT) + d_ref[...] * dyy
        dbr_ref[...] += _dot(uu, gr, _TN)
        dbi_ref[...] += _dot(uu, gi, _TN)
        dcr_ref[...] += _dot(xr_ref[...], dyy, _TN)
        dci_ref[...] -= _dot(xi_ref[...], dyy, _TN)
        dd_ref[...] += jnp.sum(dyy * uu, axis=0, keepdims=True)

    rev = lambda t: nt - 1 - t
    blk3 = lambda a, b: pl.BlockSpec((None, a, b), lambda j, t: (j, 0, 0))
    tl = pl.BlockSpec((T, HEAD), lambda j, t: (rev(t), j))
    xs = pl.BlockSpec((T, _S5_L), lambda j, t: (rev(t), j))
    xp = pl.BlockSpec((8, _S5_L), lambda j, t: (jnp.maximum(rev(t) * nt8 - 1, 0), j))
    return pl.pallas_call(
        body, name="s5_bwd", grid=(_S5_NB, nt),
        in_specs=[tl, tl, xs, xs, xp, xp, blk3(HEAD, _S5_L), blk3(HEAD, _S5_L), blk3(_S5_L, HEAD), blk3(_S5_L, HEAD),
                  blk3(8, _S5_L), blk3(8, _S5_L), pl.BlockSpec((1, HEAD), lambda j, t: (0, j))],
        out_specs=[tl, blk3(HEAD, _S5_L), blk3(HEAD, _S5_L), blk3(_S5_L, HEAD), blk3(_S5_L, HEAD),
                   blk3(8, _S5_L), blk3(8, _S5_L), pl.BlockSpec((1, HEAD), lambda j, t: (0, j))],
        out_shape=[jax.ShapeDtypeStruct((S, D_MODEL), F32),
                   jax.ShapeDtypeStruct((_S5_NB, HEAD, _S5_L), F32), jax.ShapeDtypeStruct((_S5_NB, HEAD, _S5_L), F32),
                   jax.ShapeDtypeStruct((_S5_NB, _S5_L, HEAD), F32), jax.ShapeDtypeStruct((_S5_NB, _S5_L, HEAD), F32),
                   jax.ShapeDtypeStruct((_S5_NB, 8, _S5_L), F32), jax.ShapeDtypeStruct((_S5_NB, 8, _S5_L), F32),
                   jax.ShapeDtypeStruct((1, D_MODEL), F32)],
        scratch_shapes=[pltpu.VMEM((T, _S5_L), F32), pltpu.VMEM((T, _S5_L), F32),
                        pltpu.VMEM((1, _S5_L), F32), pltpu.VMEM((1, _S5_L), F32)],
        compiler_params=_cparams(("parallel", "arbitrary")),
    )(dy, u, xre, xim, xre, xim, bre, bim, cre, cim, qw_r, qw_i, dskip)


def _s5_params(a_re, a_im, log_dt, b_re, b_im, c_re, c_im):
    step = jnp.exp(log_dt)[:, None]
    mag = jnp.exp(a_re * step)
    lr, li = mag * jnp.cos(a_im * step), mag * jnp.sin(a_im * step)
    den = a_re * a_re + a_im * a_im
    nr, ni = lr - 1.0, li
    kr, ki = (nr * a_re + ni * a_im) / den, (ni * a_re - nr * a_im) / den
    bbr = kr[..., None] * b_re - ki[..., None] * b_im
    bbi = kr[..., None] * b_im + ki[..., None] * b_re
    eye = jnp.eye(8, dtype=F32)

    def blk_b(bb):
        t = bb.reshape(_S5_NB, 8, 64, 16).transpose(0, 1, 3, 2)
        return (t[:, :, :, None, :] * eye[None, :, None, :, None]).reshape(_S5_NB, HEAD, _S5_L)

    def blk_c(cc):
        t = cc.reshape(_S5_NB, 8, 16, 64).transpose(0, 1, 3, 2)
        return (t[:, :, :, None, :] * eye[None, :, None, :, None]).reshape(_S5_NB, _S5_L, HEAD)

    return (blk_b(bbr), blk_b(bbi), blk_c(c_re), blk_c(c_im),
            lr.reshape(_S5_NB, _S5_L), li.reshape(_S5_NB, _S5_L))


def _s5_powers(lr, li):
    pr, pi = [lr], [li]
    for _ in range(7):
        pr, pi = pr + [pr[-1] * lr - pi[-1] * li], pi + [pr[-1] * li + pi[-1] * lr]
    return jnp.stack(pr, axis=1), jnp.stack(pi, axis=1)


_MESH = pl.DeviceIdType.MESH
_ANY = pl.BlockSpec(memory_space=pl.ANY)


def _place():
    x, y, c = lax.axis_index("x"), lax.axis_index("y"), lax.axis_index("c")
    return x, y, c, [(1 - x, y), (x, 1 - y), (1 - x, 1 - y)]


def _comm_call(body, arrs, out_shapes, n_remote, name):
    n = len(arrs)
    return pl.pallas_call(
        body, name=name,
        in_specs=[_ANY] * n, out_specs=[_ANY] * n, out_shape=out_shapes,
        scratch_shapes=[pltpu.SemaphoreType.DMA((n * n_remote,)), pltpu.SemaphoreType.DMA((n * n_remote,)),
                        pltpu.SemaphoreType.DMA((n,))],
    )(*arrs)


def _half(ref, slot, h):
    if len(ref.shape) == 4:
        return ref.at[slot, h]
    cols = ref.shape[2] // 2
    return ref.at[slot, :, pl.ds(pl.multiple_of(h * cols, 128), cols)]


def _own_slot(shard, me_idx, name, rows=256, cols=512):
    def body(me_ref, in_ref, o_ref):
        o_ref[...] = in_ref[...]

    if shard.ndim == 2:
        R, C = shard.shape
        return pl.pallas_call(
            body, name=name,
            grid_spec=pltpu.PrefetchScalarGridSpec(
                num_scalar_prefetch=1, grid=(C // cols,),
                in_specs=[pl.BlockSpec((R, cols), lambda j, me: (0, j))],
                out_specs=pl.BlockSpec((None, R, cols), lambda j, me: (me[0], 0, j))),
            out_shape=jax.ShapeDtypeStruct((4,) + shard.shape, shard.dtype),
            compiler_params=_cparams(("parallel",)),
        )(me_idx, shard)
    _, half, C = shard.shape
    rows = min(rows, half)
    assert half % rows == 0
    return pl.pallas_call(
        body, name=name,
        grid_spec=pltpu.PrefetchScalarGridSpec(
            num_scalar_prefetch=1, grid=(2, half // rows),
            in_specs=[pl.BlockSpec((None, rows, C), lambda h, i, me: (h, i, 0))],
            out_specs=pl.BlockSpec((None, None, rows, C), lambda h, i, me: (me[0], h, i, 0))),
        out_shape=jax.ShapeDtypeStruct((4,) + shard.shape, shard.dtype),
        compiler_params=_cparams(("parallel", "parallel")),
    )(me_idx, shard)


def _own_rows(w, layer, buf, me_idx, row_off, after, name, rows=256):
    _, r, C = w.shape
    carried = [] if isinstance(buf, tuple) else [buf]
    shape = buf if isinstance(buf, tuple) else buf.shape
    assert r % rows == 0 and row_off % rows == 0

    def body(me_ref, in_ref, *rest):
        rest[-1][...] = in_ref[...].astype(BF16)

    return pl.pallas_call(
        body, name=name,
        grid_spec=pltpu.PrefetchScalarGridSpec(
            num_scalar_prefetch=1, grid=(r // rows,),
            in_specs=[pl.BlockSpec((None, rows, C), lambda i, me: (layer, i, 0))] + [_ANY] * (len(carried) + len(after)),
            out_specs=pl.BlockSpec((None, rows, C), lambda i, me: (me[0], row_off // rows + i, 0))),
        out_shape=jax.ShapeDtypeStruct(shape, BF16),
        input_output_aliases={2: 0} if carried else {},
        compiler_params=_cparams(("parallel",)),
    )(me_idx, w, *carried, *after)


def _pair_split(arrs, name):
    n = len(arrs)

    def body(*refs):
        ins, outs, (ssem, rsem, _) = refs[:n], refs[n:2 * n], refs[2 * n:]
        x, y, c, _ = _place()
        copies = []
        for a in range(n):
            for s in range(4):
                cp = pltpu.make_async_remote_copy(src_ref=_half(ins[a], s, 1 - c), dst_ref=outs[a].at[s],
                                                  send_sem=ssem.at[4 * a + s], recv_sem=rsem.at[4 * a + s],
                                                  device_id=(x, y, 1 - c), device_id_type=_MESH)
                cp.start()
                copies.append(cp)
        for cp in copies:
            cp.wait()

    shapes = [jax.ShapeDtypeStruct((4,) + (a.shape[2:] if a.ndim == 4 else (a.shape[1], a.shape[2] // 2)), a.dtype)
              for a in arrs]
    return _comm_call(body, arrs, shapes, 4, name)


def _pair_swap(arrs, name):
    n = len(arrs)

    def body(*refs):
        ins, outs, (ssem, rsem, _) = refs[:n], refs[n:2 * n], refs[2 * n:]
        x, y, c, _ = _place()
        copies = []
        for a in range(n):
            cp = pltpu.make_async_remote_copy(src_ref=ins[a], dst_ref=outs[a], send_sem=ssem.at[a], recv_sem=rsem.at[a],
                                              device_id=(x, y, 1 - c), device_id_type=_MESH)
            cp.start()
            copies.append(cp)
        for cp in copies:
            cp.wait()

    return _comm_call(body, arrs, [jax.ShapeDtypeStruct(a.shape, a.dtype) for a in arrs], 1, name)


def _pair_sum(full, recv, c_idx, name, rows=128):
    def add(c_ref, a_ref, b_ref, o_ref):
        o_ref[...] = (a_ref[...].astype(F32) + b_ref[...].astype(F32)).astype(o_ref.dtype)

    if full.ndim == 3:
        _, R, cols = recv.shape
        blk = lambda col: pl.BlockSpec((None, R, cols), lambda s, c: (s, 0, col(c)))
        return pl.pallas_call(
            add, name=name,
            grid_spec=pltpu.PrefetchScalarGridSpec(
                num_scalar_prefetch=1, grid=(4,),
                in_specs=[blk(lambda c: c[0]), blk(lambda c: 0)], out_specs=blk(lambda c: 0)),
            out_shape=jax.ShapeDtypeStruct(recv.shape, recv.dtype),
            compiler_params=_cparams(("parallel",)),
        )(c_idx, full, recv)
    _, _, half, C = full.shape
    rows = min(rows, half)
    nb = half // rows
    assert half % rows == 0

    def body(c_ref, a_ref, b_ref, o_ref):
        o_ref[...] = (a_ref[...].astype(F32) + b_ref[...].astype(F32)).astype(o_ref.dtype)

    return pl.pallas_call(
        body, name=name,
        grid_spec=pltpu.PrefetchScalarGridSpec(
            num_scalar_prefetch=1, grid=(4, nb),
            in_specs=[pl.BlockSpec((None, None, rows, C), lambda s, i, c: (s, c[0], i, 0)),
                      pl.BlockSpec((None, rows, C), lambda s, i, c: (s, i, 0))],
            out_specs=pl.BlockSpec((None, rows, C), lambda s, i, c: (s, i, 0))),
        out_shape=jax.ShapeDtypeStruct(recv.shape, recv.dtype),
        compiler_params=_cparams(("parallel", "parallel")),
    )(c_idx, full, recv)


_HBM = pl.BlockSpec(memory_space=pltpu.HBM)
_SEM = pl.BlockSpec(memory_space=pltpu.SEMAPHORE)
_SPLIT = dict(has_side_effects=pltpu.SideEffectType.DATAFLOW_SIDE_EFFECTING)


def _hbm(t):
    return pltpu.with_memory_space_constraint(t, pltpu.HBM)


def _ag_copy(buf_ref, ssem, rsem, j, chip, c, me):
    px, py = chip
    mine = _half(buf_ref, me, c)
    return pltpu.make_async_remote_copy(src_ref=mine, dst_ref=mine, send_sem=ssem.at[j], recv_sem=rsem.at[j],
                                        device_id=(px, py, c), device_id_type=_MESH)


def _ag_start(bufs, after, name):
    n = len(bufs)

    def body(*refs):
        buf_refs, (ssem, rsem), token = refs[:n], refs[n + 1:n + 3], refs[-1]
        x, y, c, chips = _place()
        for a in range(n):
            for j, chip in enumerate(chips):
                _ag_copy(buf_refs[a], ssem, rsem, 3 * a + j, chip, c, 2 * x + y).start()
        token[...] = jnp.zeros_like(token)

    out = pl.pallas_call(
        body, name=name,
        out_shape=(pltpu.SemaphoreType.DMA((3 * n,)), pltpu.SemaphoreType.DMA((3 * n,)),
                   *[pltpu.HBM(b.shape, b.dtype) for b in bufs], jax.ShapeDtypeStruct((8, HEAD), F32)),
        in_specs=(*[_HBM] * n, _ANY), out_specs=(_SEM, _SEM, *[_HBM] * n, pl.BlockSpec(memory_space=pltpu.VMEM)),
        input_output_aliases={a: 2 + a for a in range(n)}, compiler_params=pltpu.CompilerParams(**_SPLIT),
    )(*[_hbm(b) for b in bufs], after)
    return out[0], out[1], list(out[2:2 + n]), out[-1]


def _ag_wait(ssem, rsem, bufs, after, name):
    n = len(bufs)

    def body(*refs):
        buf_refs, ssem, rsem = refs[:n], refs[n], refs[n + 1]
        x, y, c, chips = _place()
        for a in range(n):
            for j, chip in enumerate(chips):
                cp = _ag_copy(buf_refs[a], ssem, rsem, 3 * a + j, chip, c, 2 * x + y)
                cp.wait_send()
                cp.wait_recv()

    after = after if isinstance(after, (tuple, list)) else (after,)
    return list(pl.pallas_call(
        body, name=name, out_shape=tuple(pltpu.HBM(b.shape, b.dtype) for b in bufs),
        in_specs=(*[_HBM] * n, _SEM, _SEM, *[_ANY] * len(after)), out_specs=tuple([_HBM] * n),
        input_output_aliases={a: a for a in range(n)}, compiler_params=pltpu.CompilerParams(**_SPLIT),
    )(*bufs, ssem, rsem, *after))


def _pair_forward(bufs, name):
    n = len(bufs)

    def body(*refs):
        outs, (ssem, rsem) = refs[n:2 * n], refs[2 * n:]
        x, y, c, chips = _place()
        copies = []
        for a in range(n):
            for j, (px, py) in enumerate(chips):
                landed = _half(outs[a], 2 * px + py, c)
                cp = pltpu.make_async_remote_copy(src_ref=landed, dst_ref=landed, send_sem=ssem.at[3 * a + j],
                                                  recv_sem=rsem.at[3 * a + j], device_id=(x, y, 1 - c),
                                                  device_id_type=_MESH)
                cp.start()
                copies.append(cp)
        for cp in copies:
            cp.wait()

    return pl.pallas_call(
        body, name=name, in_specs=[_ANY] * n, out_specs=[_ANY] * n,
        out_shape=[jax.ShapeDtypeStruct(b.shape, b.dtype) for b in bufs],
        input_output_aliases={a: a for a in range(n)},
        scratch_shapes=[pltpu.SemaphoreType.DMA((3 * n,)), pltpu.SemaphoreType.DMA((3 * n,))],
    )(*bufs)


def _split_copy(part_ref, land_ref, ssem, rsem, s, x, y, c):
    return pltpu.make_async_remote_copy(src_ref=_half(part_ref, s, 1 - c), dst_ref=land_ref.at[s], send_sem=ssem.at[s],
                                        recv_sem=rsem.at[s], device_id=(x, y, 1 - c), device_id_type=_MESH)


def _pair_split_start(part, after, name):
    land = jax.ShapeDtypeStruct((4,) + part.shape[2:], part.dtype)

    def body(part_ref, land_ref, after_ref, ssem, rsem, part_thru, land_thru, token):
        x, y, c, _ = _place()
        for s in range(4):
            _split_copy(part_ref, land_ref, ssem, rsem, s, x, y, c).start()
        token[...] = jnp.zeros_like(token)

    return pl.pallas_call(
        body, name=name,
        out_shape=(pltpu.SemaphoreType.DMA((4,)), pltpu.SemaphoreType.DMA((4,)), pltpu.HBM(part.shape, part.dtype),
                   pltpu.HBM(land.shape, land.dtype), jax.ShapeDtypeStruct((8, HEAD), F32)),
        in_specs=(_HBM, _HBM, _ANY), out_specs=(_SEM, _SEM, _HBM, _HBM, pl.BlockSpec(memory_space=pltpu.VMEM)),
        input_output_aliases={0: 2, 1: 3}, compiler_params=pltpu.CompilerParams(**_SPLIT),
    )(_hbm(part), _hbm(lax.empty(land.shape, land.dtype)), after)


def _pair_split_wait(ssem, rsem, part, land, after, name):
    def body(part_ref, land_ref, ssem, rsem, after_ref, part_out, land_out):
        x, y, c, _ = _place()
        for s in range(4):
            cp = _split_copy(part_ref, land_ref, ssem, rsem, s, x, y, c)
            cp.wait_send()
            cp.wait_recv()

    return pl.pallas_call(
        body, name=name, out_shape=(pltpu.HBM(part.shape, part.dtype), pltpu.HBM(land.shape, land.dtype)),
        in_specs=(_HBM, _HBM, _SEM, _SEM, _ANY), out_specs=(_HBM, _HBM),
        input_output_aliases={0: 0, 1: 1}, compiler_params=pltpu.CompilerParams(**_SPLIT),
    )(part, land, ssem, rsem, after)


def _rs_copy(src_ref, land_ref, ssem, rsem, j, chip, c):
    px, py = chip
    return pltpu.make_async_remote_copy(src_ref=src_ref.at[2 * px + py], dst_ref=land_ref.at[j], send_sem=ssem.at[j],
                                        recv_sem=rsem.at[j], device_id=(px, py, c), device_id_type=_MESH)


def _rs_start(part, after, name):
    def body(part_ref, land_ref, after_ref, ssem, rsem, part_thru, land_thru, token):
        x, y, c, chips = _place()
        for j, chip in enumerate(chips):
            _rs_copy(part_ref, land_ref, ssem, rsem, j, chip, c).start()
        token[...] = jnp.zeros_like(token)

    land = jax.ShapeDtypeStruct((3,) + part.shape[1:], part.dtype)
    return pl.pallas_call(
        body, name=name,
        out_shape=(pltpu.SemaphoreType.DMA((3,)), pltpu.SemaphoreType.DMA((3,)), pltpu.HBM(part.shape, part.dtype),
                   pltpu.HBM(land.shape, land.dtype), jax.ShapeDtypeStruct((8, HEAD), F32)),
        in_specs=(_HBM, _HBM, _ANY), out_specs=(_SEM, _SEM, _HBM, _HBM, pl.BlockSpec(memory_space=pltpu.VMEM)),
        input_output_aliases={0: 2, 1: 3}, compiler_params=pltpu.CompilerParams(**_SPLIT),
    )(_hbm(part), _hbm(lax.empty(land.shape, land.dtype)), after)


def _rs_wait(ssem, rsem, part, land, after, name):
    def body(part_ref, land_ref, ssem, rsem, after_ref, part_out, land_out):
        x, y, c, chips = _place()
        for j, chip in enumerate(chips):
            cp = _rs_copy(part_ref, land_ref, ssem, rsem, j, chip, c)
            cp.wait_send()
            cp.wait_recv()

    return pl.pallas_call(
        body, name=name, out_shape=(pltpu.HBM(part.shape, part.dtype), pltpu.HBM(land.shape, land.dtype)),
        in_specs=(_HBM, _HBM, _SEM, _SEM, _ANY), out_specs=(_HBM, _HBM),
        input_output_aliases={0: 0, 1: 1}, compiler_params=pltpu.CompilerParams(**_SPLIT),
    )(part, land, ssem, rsem, after)


def _sum_own_recv(part, land, me_idx, name, rows=128):
    _, half, C = part.shape

    def body(me_ref, own_ref, land_ref, o_ref):
        acc = own_ref[...].astype(F32)
        for s in range(3):
            acc = acc + land_ref[s].astype(F32)
        o_ref[...] = acc

    if half % 8:
        cols = 256
        return pl.pallas_call(
            body, name=name,
            grid_spec=pltpu.PrefetchScalarGridSpec(
                num_scalar_prefetch=1, grid=(C // cols,),
                in_specs=[pl.BlockSpec((None, half, cols), lambda j, me: (me[0], 0, j)),
                          pl.BlockSpec((3, half, cols), lambda j, me: (0, 0, j))],
                out_specs=pl.BlockSpec((half, cols), lambda j, me: (0, j))),
            out_shape=jax.ShapeDtypeStruct((half, C), F32),
            compiler_params=_cparams(("parallel",)),
        )(me_idx, part, land)
    rows = min(rows, half)
    assert half % rows == 0
    return pl.pallas_call(
        body, name=name,
        grid_spec=pltpu.PrefetchScalarGridSpec(
            num_scalar_prefetch=1, grid=(half // rows,),
            in_specs=[pl.BlockSpec((None, rows, C), lambda i, me: (me[0], i, 0)),
                      pl.BlockSpec((3, rows, C), lambda i, me: (0, i, 0))],
            out_specs=pl.BlockSpec((rows, C), lambda i, me: (i, 0))),
        out_shape=jax.ShapeDtypeStruct((half, C), F32),
        compiler_params=_cparams(("parallel",)),
    )(me_idx, part, land)


def _sum_slots(arr, name, rows=128):
    k, R, C = arr.shape
    rows = min(rows, R)
    assert R % rows == 0

    def body(in_ref, o_ref):
        acc = in_ref[0].astype(F32)
        for s in range(1, k):
            acc = acc + in_ref[s].astype(F32)
        o_ref[...] = acc

    return pl.pallas_call(
        body, name=name, grid=(R // rows,),
        in_specs=[pl.BlockSpec((k, rows, C), lambda i: (0, i, 0))],
        out_specs=pl.BlockSpec((rows, C), lambda i: (i, 0)),
        out_shape=jax.ShapeDtypeStruct((R, C), F32),
        compiler_params=_cparams(("parallel",)),
    )(arr)


ADAM_LR, ADAM_B1, ADAM_B2, ADAM_EPS, ADAM_WD, ADAM_STEP = 0.001, 0.9, 0.999, 1e-08, 0.01, 10


def _adam_math(w, g, m, v):
    m = ADAM_B1 * m + (1.0 - ADAM_B1) * g
    v = ADAM_B2 * v + (1.0 - ADAM_B2) * jnp.square(g)
    m_hat = m / (1.0 - ADAM_B1 ** ADAM_STEP)
    v_hat = v / (1.0 - ADAM_B2 ** ADAM_STEP)
    delta = -ADAM_LR * (m_hat / (jnp.sqrt(v_hat) + ADAM_EPS) + ADAM_WD * w)
    return delta, m, v


def _adamw(w, m, v, layer, mine, other, c_idx, name, g_off=0, prev=None, after=(), rows=128):
    _, R, C = w.shape
    half = mine.shape[0]
    rows = min(rows, R)
    assert R % rows == 0 and g_off % rows == 0 and half % rows == 0
    nbh, b0 = half // rows, g_off // rows

    def body(c_ref, w_ref, m_ref, v_ref, mine_ref, other_ref, *rest):
        go, do, mo, vo = rest[-4:]
        in_my_half = (b0 + pl.program_id(0)) // nbh == c_ref[0]
        g = jnp.where(in_my_half, mine_ref[...], other_ref[...])
        delta, m1, v1 = _adam_math(w_ref[...], g, m_ref[...], v_ref[...])
        go[...] = g
        do[...] = delta
        mo[...] = m1
        vo[...] = v1

    blk = pl.BlockSpec((None, rows, C), lambda i, c: (layer, i, 0))

    def gblk(of_mine):
        return pl.BlockSpec((rows, C), lambda i, c: (
            jnp.where(((b0 + i) // nbh == c[0]) == of_mine, (b0 + i) % nbh, 0), 0))

    carried = list(prev) if prev is not None else []
    return pl.pallas_call(
        body, name=name,
        grid_spec=pltpu.PrefetchScalarGridSpec(
            num_scalar_prefetch=1, grid=(R // rows,),
            in_specs=[blk] * 3 + [gblk(True), gblk(False)] + [_ANY] * (len(carried) + len(after)),
            out_specs=[blk] * 4),
        out_shape=[jax.ShapeDtypeStruct(w.shape, F32)] * 4,
        input_output_aliases={6 + k: k for k in range(len(carried))},
        compiler_params=_cparams(("parallel",)),
    )(c_idx, w, m, v, mine, other, *carried, *after)


def _adamw_t(w, m, v, mine, other, c_idx, name, rows=513, cols=128):
    R, _, C = w.shape
    nbh = (C // 2) // cols
    assert R % rows == 0 and (C // 2) % cols == 0

    def body(c_ref, w_ref, m_ref, v_ref, mine_ref, other_ref, go, do, mo, vo):
        in_my_half = pl.program_id(1) // nbh == c_ref[0]
        g = jnp.where(in_my_half, mine_ref[...], other_ref[...])
        delta, m1, v1 = _adam_math(w_ref[...], g, m_ref[...], v_ref[...])
        go[...] = g
        do[...] = delta
        mo[...] = m1
        vo[...] = v1

    blk = pl.BlockSpec((rows, 1, cols), lambda i, j, c: (i, 0, j))
    gblk = pl.BlockSpec((rows, 1, cols), lambda i, j, c: (i, 0, j % nbh))
    return pl.pallas_call(
        body, name=name,
        grid_spec=pltpu.PrefetchScalarGridSpec(
            num_scalar_prefetch=1, grid=(R // rows, C // cols), in_specs=[blk] * 3 + [gblk] * 2,
            out_specs=[blk] * 4),
        out_shape=[jax.ShapeDtypeStruct(w.shape, F32)] * 4,
        compiler_params=_cparams(("parallel", "parallel")),
    )(c_idx, w, m, v, mine, other)


_WEIGHTS = ['norm_mix', 'norm_mlp', 'norm_ple', 'w_in_e', 'w_out_e', 'hgrn_lb', 'g_norm_a', 'conv_w', 'a_log',
            'dt_bias', 'g_norm_b', 's5_a_re', 's5_a_im', 's5_b_re', 's5_b_im', 's5_c_re', 's5_c_im', 's5_d',
            's5_log_dt', 'w_glu', 'b_glu', 'w_out_o', 'w_up', 'w_down', 'w_ple_gate', 'w_ple_proj', 'final_norm']
_INPUTS = ['x', 'p'] + _WEIGHTS + ['loss_target'] + ['m_' + n for n in _WEIGHTS] + ['v_' + n for n in _WEIGHTS]
_FAMILY = [('w_up', 0, 'col', 2048), ('w_down', 0, 'row', 2048), ('w_ple_gate', 0, 'row', 512), ('w_out_e', 0, 'row', 512),
           ('w_up', 1, 'col', 2048), ('w_down', 1, 'row', 2048), ('w_ple_gate', 1, 'row', 512), ('w_glu', 0, 'row', 512),
           ('w_out_o', 0, 'row', 512)]
_PACK = {('w_up', 0): 0, ('w_down', 0): 0, ('w_ple_gate', 0): 0, ('w_out_e', 0): 0,
         ('w_up', 1): 1, ('w_down', 1): 1, ('w_ple_gate', 1): 1, ('w_glu', 0): 1, ('w_out_o', 0): 1}
_IN_PAD = 8320
_IN_BLK = 1664


def _rms_bwd_fn(x, d_hn, d_res, g):
    _, vjp = jax.vjp(_rms, x, g)
    dx, dg = vjp(d_hn.astype(F32))
    return dx + d_res, dg


def _rms_bwd_both(x, d_hn, d_res, g):
    dx, dg = _rms_bwd_fn(x, d_hn, d_res, g)
    return dx, dx, dg


def _add_res(acc, h):
    return (acc + h,)


def _pack_rows(parts, lanes=128, mult=256):
    flat = jnp.concatenate([q.reshape(-1).astype(F32) for q in parts])
    n = flat.shape[0]
    rows = -(-n // (lanes * mult)) * mult
    return jnp.pad(flat, (0, rows * lanes - n)).reshape(rows, lanes)


def _unpack_rows(buf, shapes):
    flat, out, off = buf.reshape(-1), [], 0
    for s in shapes:
        n = math.prod(s)
        out.append(flat[off:off + n].reshape(s))
        off += n
    return out


def _step(a):
    S = a['x'].shape[1]
    x, tgt = a['x'][0], a['loss_target'][0]
    xi, yi = lax.axis_index("x"), lax.axis_index("y")
    me = 2 * xi + yi
    row = lambda t: t.reshape(1, -1)

    me_idx = me.astype(jnp.int32).reshape(1)
    c_idx = lax.axis_index("c").astype(jnp.int32).reshape(1)
    small_sh = jnp.concatenate([a['conv_w'][0].reshape(-1), a['s5_d'][0], a['b_glu'][0]]).reshape(16, 256)
    rows_first = lambda t: jnp.transpose(t, (2, 0, 1))
    shards = [None, None, rows_first(a['w_in_e']).reshape(2052, D_MODEL).astype(BF16),
              a['w_ple_proj'].reshape(512, 512).astype(BF16), small_sh]
    own = lambda i: _own_slot(shards[i] if i == 2 else shards[i].reshape(2, shards[i].shape[0] // 2, shards[i].shape[1]),
                              me_idx, name=f"own_slot{i}")
    whole = lambda g, t: g.reshape((4,) + t.shape)
    first = _ag_start([own(2), own(3), own(4)], me_idx, name="gather_first_start")
    big = {nm for nm, _, _, _ in _FAMILY} | {'w_in_e', 'w_ple_proj'}
    held = ['x'] + [pre + n for n in _WEIGHTS if n not in big for pre in ('', 'm_', 'v_')]
    _, late = lax.optimization_barrier((first[3], {n: a[n] for n in held}))
    a = {**a, **late}
    x = a['x'][0]
    views, pack_of, rt = {}, {}, [0, 0]
    for nm, l, kind, r in _FAMILY:
        views[nm, l], pack_of[nm, l] = (kind, rt[_PACK[nm, l]], r), _PACK[nm, l]
        rt[_PACK[nm, l]] += r
    bufs = [(4, rt[0], PACK_COLS), (4, rt[1], PACK_COLS)]
    for nm, l, kind, r in _FAMILY:
        k = pack_of[nm, l]
        bufs[k] = _own_rows(a[nm], l, bufs[k], me_idx, views[nm, l][1], (first[3],), name=f"own_{nm}{l}")
    shards[0], shards[1] = [jax.ShapeDtypeStruct((rt[k], PACK_COLS), BF16) for k in (0, 1)]
    slots = [bufs[k].reshape(4, 2, rt[k] // 2, PACK_COLS) for k in (0, 1)]
    hn0 = _rowwise(_rms, [x], [row(a['norm_mix'][0])], [(D_MODEL, BF16)], name="rms_mix0")
    s_names = [n for n in _WEIGHTS if n not in big]
    w_pack, m_pack, v_pack = [_pack_rows([a[pre + n] for n in s_names]) for pre in ('', 'm_', 'v_')]
    landed = _pair_forward(_ag_wait(first[0], first[1], first[2], (hn0, bufs[0], bufs[1], w_pack, m_pack, v_pack),
                                    name="gather_first_wait"), name="gather_first_pass")
    g_in, g_pp, g_small = [whole(g, t) for g, t in zip(landed, shards[2:])]
    gw = {}
    W = lambda nm, l: dict(b=gw[pack_of[nm, l]], b_view=views[nm, l])
    sems0 = _ag_start([slots[0]], g_in, name="gather0_start")
    win_t = jnp.pad(g_in.reshape(8208, D_MODEL), ((0, _IN_PAD - 8208), (0, 0)))
    w_pp = g_pp.reshape(4, 2, 256, 512).transpose(1, 2, 0, 3).reshape(2, 256, D_MODEL)
    g_small = g_small.reshape(4, 4096)
    conv_full = g_small[:, :3072].reshape(4, 4, 768).transpose(1, 0, 2).reshape(4, 3072)
    s5_d_full = g_small[:, 3072:3584].reshape(1, D_MODEL)
    b_glu_full = g_small[:, 3584:].reshape(1, D_MODEL)

    lb, lb_vjp = jax.vjp(lambda hl: jax.nn.softmax(hl, axis=0)[0:1], a['hgrn_lb'])
    pad_h = lambda t: jnp.pad(t, ((0, 0), (0, HEAD - t.shape[1])))
    (arow, dtrow), gdn_vjp = jax.vjp(lambda al, dt: (pad_h(-jnp.exp(al)), pad_h(dt)), a['a_log'], a['dt_bias'])
    s5p, s5_vjp = jax.vjp(_s5_params, a['s5_a_re'][0], a['s5_a_im'][0], a['s5_log_dt'][0], a['s5_b_re'][0],
                          a['s5_b_im'][0], a['s5_c_re'][0], a['s5_c_im'][0])
    bre, bim, cre, cim = [t.astype(BF16) for t in s5p[:4]]
    pw_r, pw_i = _s5_powers(s5p[4], s5p[5])
    gna, gnb = a['g_norm_a'], a['g_norm_b']

    def norm_cast(h, g, dt, name):
        return _rowwise(_rms, [h], [row(g)], [(D_MODEL, dt)], name=name)

    def mlp_ple_fwd(h, l):
        hn = norm_cast(h, a['norm_mlp'][l], BF16, f"rms_mlp{l}")
        up, act = _mm(hn, **W('w_up', l), out_dtypes=(BF16, BF16), name=f"up{l}",
                      epilogue=lambda acc: (acc, jnp.square(jnp.maximum(acc, 0.0))))
        h2 = _mm(act, **W('w_down', l), epilogue=_add_res, extras=(h,), name=f"down{l}")
        hnp = norm_cast(h2, a['norm_ple'][l], BF16, f"rms_ple{l}")
        pp = _mm(a['p'][l, 0], w_pp[l], name=f"ple_proj{l}")
        h3, gpre = _mm(hnp, **W('w_ple_gate', l), extras=(h2, pp), out_dtypes=(F32, F32), name=f"ple_gate{l}",
                       bm=512, epilogue=lambda acc, h2, pp: (h2 + jax.nn.sigmoid(acc) * pp, acc))
        return h3, (h, hn, up, act, h2, hnp, pp, gpre)

    proj = _mm(hn0, win_t, dims="nt", bn=_IN_BLK, name="in_proj", after=(sems0[3],))
    o_a, st_a = _hgrn_fwd(proj, lb, gna)
    act_b = _conv_fwd(proj, conv_full)
    o_b, st_b = _gdn_fwd(act_b, proj, arow, dtrow, gnb)
    merged = jnp.concatenate([o_a, o_b], axis=1)
    landed0 = _ag_wait(sems0[0], sems0[1], sems0[2], merged, name="gather0_wait")
    gw[0] = whole(_pair_forward(landed0, name="gather0_pass")[0], shards[0])
    sems1 = _ag_start([slots[1]], gw[0], name="gather1_start")
    h1 = _mm(merged, **W('w_out_e', 0), epilogue=_add_res, extras=(x,), name="out_e", after=(sems1[3],))
    h3, saved0 = mlp_ple_fwd(h1, 0)

    u = norm_cast(h3, a['norm_mix'][1], F32, "rms_mix1")
    y, xre, xim = _s5_fwd(u, bre, bim, cre, cim, pw_r, pw_i, s5_d_full)
    actg = _rowwise(jax.nn.gelu, [y], [], [(D_MODEL, BF16)], name="gelu")
    landed1 = _ag_wait(sems1[0], sems1[1], sems1[2], actg, name="gather1_wait")
    gw[1] = whole(_pair_forward(landed1, name="gather1_pass")[0], shards[1])
    glu, z = _mm(actg, **W('w_glu', 0), extras=(actg, b_glu_full), out_dtypes=(BF16, F32), name="glu",
                 epilogue=lambda acc, act, b: (act.astype(F32) * jax.nn.sigmoid(acc + b), acc + b))
    h4 = _mm(glu, **W('w_out_o', 0), epilogue=_add_res, extras=(h3,), name="out_o")
    h6, saved1 = mlp_ple_fwd(h4, 1)

    def head(h, t, g):
        def f(h, g):
            e = _rms(h, g) - t
            return 0.5 * jnp.sum(jnp.mean(e * e, axis=-1))
        val, vjp = jax.vjp(f, h, g)
        dh, dg = vjp(jnp.ones((), F32))
        return dh, dg, jnp.full((1, HEAD), val)

    dh, d_final, loss_part = _rowwise(head, [h6, tgt], [row(a['final_norm'])], [(D_MODEL, F32)], sums=(D_MODEL, HEAD),
                                      name="loss_head")
    loss = lax.psum(loss_part[0, 0], ("x", "y", "c"))

    gb = {k: lax.empty((4, rt[k], PACK_COLS), BF16) for k in (0, 1)}

    def into(lhs, rhs, key, name, after=()):
        k = pack_of[key]
        gb[k] = _mm(lhs, rhs, dims="tn", out_into=gb[k], out_view=views[key], out_dtypes=(BF16,), name=name, after=after)

    def mlp_ple_bwd(dh, l, saved, after=()):
        h, hn, up, act, h2, hnp, pp, gpre = saved

        def gate_bwd(d, gpre, pp):
            s = jax.nn.sigmoid(gpre)
            return d * s, d * pp * s * (1.0 - s)

        d_pp, d_gp = _rowwise(gate_bwd, [dh, gpre, pp], [], [(D_MODEL, BF16), (D_MODEL, BF16)], name=f"ple_bwd{l}")
        d_wpp = _mm(a['p'][l, 0], d_pp, dims="tn", name=f"d_ple_proj{l}", after=after)
        into(hnp, d_gp, ('w_ple_gate', l), f"d_ple_gate{l}")
        d_hnp = _mm(d_gp, **W('w_ple_gate', l), dims="nt", name=f"ple_gate_t{l}")
        dh2, dh2_b, d_nple = _rowwise(_rms_bwd_both, [h2, d_hnp, dh], [row(a['norm_ple'][l])],
                                      [(D_MODEL, F32), (D_MODEL, BF16)], sums=(D_MODEL,), name=f"rms_ple_bwd{l}")
        d_up = _mm(dh2_b, **W('w_down', l), dims="nt", extras=(up,), out_dtypes=(BF16,), name=f"down_t{l}",
                   epilogue=lambda acc, up: (acc * 2.0 * jnp.maximum(up.astype(F32), 0.0),))
        into(act, dh2_b, ('w_down', l), f"d_down{l}")
        into(hn, d_up, ('w_up', l), f"d_up{l}")
        d_hn = _mm(d_up, **W('w_up', l), dims="nt", name=f"up_t{l}")
        dh1, dh1_b, d_nmlp = _rowwise(_rms_bwd_both, [h, d_hn, dh2], [row(a['norm_mlp'][l])],
                                      [(D_MODEL, F32), (D_MODEL, BF16)], sums=(D_MODEL,), name=f"rms_mlp_bwd{l}")
        return dh1, dh1_b, d_wpp, d_nple, d_nmlp

    dh4, dh4_b, d_wpp1, d_nple1, d_nmlp1 = mlp_ple_bwd(dh, 1, saved1)

    d_glu = _mm(dh4_b, **W('w_out_o', 0), dims="nt", name="out_o_t")
    into(glu, dh4_b, ('w_out_o', 0), "d_out_o")

    def glu_bwd(d, z, act):
        s, act = jax.nn.sigmoid(z), act.astype(F32)
        dz = d * act * s * (1.0 - s)
        return dz, d * s, jnp.sum(dz, axis=0, keepdims=True)

    d_z, d_actp, d_bglu = _rowwise(glu_bwd, [d_glu, z, actg], [], [(D_MODEL, BF16), (D_MODEL, F32)], sums=(D_MODEL,),
                                   name="glu_bwd")
    into(actg, d_z, ('w_glu', 0), "d_glu")

    def reduce_start(bufs, tag):
        parts = [t if t.shape[1] % 32 else t.reshape(4, 2, t.shape[1] // 2, t.shape[2]) for t in bufs]
        recv = _pair_split(parts, name=f"pair_split{tag}")
        sums = [_pair_sum(f, r, c_idx, name=f"pair_sum{tag}_{i}") for i, (f, r) in enumerate(zip(parts, recv))]
        started, prev = [], c_idx
        for i, s in enumerate(sums):
            started.append(_rs_start(s, prev, name=f"scatter{tag}_{i}_start"))
            prev = started[-1][4]
        return started

    def pack_split(k):
        part = gb[k].reshape(4, 2, rt[k] // 2, PACK_COLS)
        return _pair_split_start(part, c_idx, name=f"pair_split{k}_start")

    def pack_scatter(k, split, after):
        part, recv = _pair_split_wait(split[0], split[1], split[2], split[3], after, name=f"pair_split{k}_wait")
        return [_rs_start(_pair_sum(part, recv, c_idx, name=f"pair_sum{k}_0"), c_idx, name=f"scatter{k}_0_start")]

    def reduce_finish(started, after, tag):
        halves = []
        for i, (ssem, rsem, part, land, _) in enumerate(started):
            part, land = _rs_wait(ssem, rsem, part, land, after, name=f"scatter{tag}_{i}_wait")
            halves.append(_sum_own_recv(part, land, me_idx, name=f"sum_partials{tag}_{i}"))
        return list(zip(halves, _pair_swap(halves, name=f"swap_halves{tag}")))

    split1 = pack_split(1)

    def gelu_bwd(acc, dap, y):
        _, vjp = jax.vjp(jax.nn.gelu, y)
        return vjp(acc + dap)

    dy = _mm(d_z, **W('w_glu', 0), dims="nt", extras=(d_actp, y), epilogue=gelu_bwd, name="glu_t",
             after=(split1[4],))
    du, d_bre, d_bim, d_cre, d_cim, d_lr, d_li, d_s5d = _s5_bwd(dy, u, xre, xim, bre, bim, cre, cim,
                                                                 pw_r[:, ::-1], -pw_i[:, ::-1], s5_d_full)
    dh3, d_nmix1 = _rowwise(_rms_bwd_fn, [h3, du, dh4], [row(a['norm_mix'][1])], [(D_MODEL, F32)], sums=(D_MODEL,),
                            name="rms_mix1_bwd")
    d_are, d_aim, d_logdt, d_sbre, d_sbim, d_scre, d_scim = s5_vjp(
        (d_bre, d_bim, d_cre, d_cim, d_lr.sum(axis=1), d_li.sum(axis=1)))

    started1 = pack_scatter(1, split1, dh3)
    dh1, dh1_b, d_wpp0, d_nple0, d_nmlp0 = mlp_ple_bwd(dh3, 0, saved0, after=(started1[0][4],))

    d_merged = _mm(dh1_b, **W('w_out_e', 0), dims="nt", name="out_e_t")
    into(merged, dh1_b, ('w_out_e', 0), "d_out_e")
    split0 = pack_split(0)
    dq, df, dv, dg, d_lb, d_gna = _hgrn_bwd(proj, lb, gna, st_a, d_merged, after=(split0[4],))
    started0 = pack_scatter(0, split0, dq)
    dqb, dkb, dvb, dzb, d_tail, d_arow, d_dtrow, d_gnb = _gdn_bwd(act_b, proj, arow, dtrow, gnb, st_b, d_merged)
    d_pre, d_conv = _conv_bwd(proj, conv_full, dqb, dkb, dvb)
    d_proj = jnp.concatenate([dq, df, dv, dg, d_pre, dzb, d_tail], axis=1)
    d_win_t = _mm(d_proj, hn0, dims="tn", bm=_IN_BLK, out_dtypes=(BF16,), name="d_in_proj",
                  after=(started0[0][4],))
    d_hn0 = _mm(d_proj, win_t, bk=_IN_BLK, name="in_proj_t")
    grad_x, d_nmix0 = _rowwise(_rms_bwd_fn, [x, d_hn0, dh1], [row(a['norm_mix'][0])], [(D_MODEL, F32)], sums=(D_MODEL,),
                               name="rms_mix0_bwd")
    (d_hlb,) = lb_vjp(d_lb.reshape(1, 1024))
    d_alog, d_dtb = gdn_vjp((d_arow, d_dtrow))

    d_win_sh = d_win_t[:8208].reshape(4, 2052, D_MODEL)
    d_wpp_sh = jnp.stack([d_wpp0, d_wpp1]).astype(BF16).reshape(2, 256, 4, 512).transpose(2, 0, 1, 3).reshape(4, 512, 512)
    started2 = reduce_start([d_win_sh, d_wpp_sh], 2)
    small = {
        'norm_mix': jnp.concatenate([d_nmix0, d_nmix1]), 'norm_mlp': jnp.concatenate([d_nmlp0, d_nmlp1]),
        'norm_ple': jnp.concatenate([d_nple0, d_nple1]), 'hgrn_lb': d_hlb, 'g_norm_a': d_gna, 'conv_w': d_conv,
        'a_log': d_alog, 'dt_bias': d_dtb, 'g_norm_b': d_gnb, 's5_a_re': d_are, 's5_a_im': d_aim, 's5_b_re': d_sbre,
        's5_b_im': d_sbim, 's5_c_re': d_scre, 's5_c_im': d_scim, 's5_d': d_s5d, 's5_log_dt': d_logdt,
        'b_glu': d_bglu, 'final_norm': d_final}
    s_names = list(small)
    s_shapes = [tuple(small[n].shape) for n in s_names]
    mine_small = _pack_rows([small[n] for n in s_names])
    chip_small = _rowwise(lambda p, o: p + o, [mine_small, _pair_swap([mine_small], name="swap_small")[0]], [],
                          [(128, F32)], name="add_small")
    n_small = chip_small.shape[0]
    small_go = _ag_start([_own_slot(chip_small.reshape(2, n_small // 2, 128), me_idx, name="own_slot_small")],
                         started2[-1][4], name="gather_small_start")
    red = {1: reduce_finish(started1, small_go[3], 1)[0]}
    red[0] = reduce_finish(started0, small_go[3], 0)[0]
    res, order = {}, []

    def adam_layer(nm, layer, key, g_off=None):
        g_off = views[nm, layer][1] if g_off is None else g_off
        prev = res.get(nm)
        follows = () if not order or (prev is not None and order[-1] is prev[0]) else (order[-1],)
        res[nm] = tuple(_adamw(a[nm], a['m_' + nm], a['v_' + nm], layer, *red[key], c_idx, name=f"adam_{nm}{layer}",
                               g_off=g_off, prev=prev, after=follows))
        order.append(res[nm][0])

    for nm, layer in (('w_glu', 0), ('w_out_o', 0), ('w_up', 1), ('w_down', 1), ('w_ple_gate', 1)):
        adam_layer(nm, layer, 1)
    for nm in ('w_up', 'w_down', 'w_ple_gate', 'w_out_e'):
        adam_layer(nm, 0, 0)

    small_all = _pair_forward(_ag_wait(small_go[0], small_go[1], small_go[2], order[-1], name="gather_small_wait"),
                              name="gather_small_pass")[0]
    reduced = _sum_slots(small_all.reshape(4, n_small, 128), name="sum_small")
    sg = dict(zip(s_names, _unpack_rows(reduced, s_shapes)))
    sg['conv_w'] = lax.dynamic_slice_in_dim(sg['conv_w'], me * 768, 768, axis=1)
    sg['s5_d'] = lax.dynamic_slice_in_dim(sg['s5_d'], me * 512, 512, axis=1)
    sg['b_glu'] = lax.dynamic_slice_in_dim(sg['b_glu'], me * 512, 512, axis=1)
    sg = {n: sg[n].reshape(a[n].shape) for n in s_names}
    g_pack = _pack_rows([sg[n] for n in s_names])
    sd, sm, sv = _rowwise(_adam_math, [w_pack, g_pack, m_pack, v_pack], [], [(128, F32)] * 3, name="adam_small")
    w_shapes = [tuple(a[n].shape) for n in s_names]
    res.update({n: (sg[n], d_, m_, v_) for n, d_, m_, v_ in zip(s_names, _unpack_rows(sd, w_shapes),
                                                               _unpack_rows(sm, w_shapes), _unpack_rows(sv, w_shapes))})

    red['in'], red['pp'] = reduce_finish(started2, sd, 2)
    res['w_in_e'] = tuple(jnp.transpose(o, (1, 2, 0)) for o in _adamw_t(
        rows_first(a['w_in_e']), rows_first(a['m_w_in_e']), rows_first(a['v_w_in_e']),
        *[h.reshape(2052, 1, D_MODEL // 2) for h in red['in']], c_idx, name="adam_w_in_e"))
    adam_layer('w_ple_proj', 0, 'pp', 0)
    adam_layer('w_ple_proj', 1, 'pp', 256)

    return (loss, grad_x[None], *[res[n][0] for n in _WEIGHTS], *[res[n][1] for n in _WEIGHTS],
            *[res[n][2] for n in _WEIGHTS], *[res[n][3] for n in _WEIGHTS])


def kernel(x, p, norm_mix, norm_mlp, norm_ple, w_in_e, w_out_e, hgrn_lb, g_norm_a, conv_w, a_log, dt_bias, g_norm_b, s5_a_re, s5_a_im, s5_b_re, s5_b_im, s5_c_re, s5_c_im, s5_d, s5_log_dt, w_glu, b_glu, w_out_o, w_up, w_down, w_ple_gate, w_ple_proj, final_norm, loss_target, m_norm_mix, m_norm_mlp, m_norm_ple, m_w_in_e, m_w_out_e, m_hgrn_lb, m_g_norm_a, m_conv_w, m_a_log, m_dt_bias, m_g_norm_b, m_s5_a_re, m_s5_a_im, m_s5_b_re, m_s5_b_im, m_s5_c_re, m_s5_c_im, m_s5_d, m_s5_log_dt, m_w_glu, m_b_glu, m_w_out_o, m_w_up, m_w_down, m_w_ple_gate, m_w_ple_proj, m_final_norm, v_norm_mix, v_norm_mlp, v_norm_ple, v_w_in_e, v_w_out_e, v_hgrn_lb, v_g_norm_a, v_conv_w, v_a_log, v_dt_bias, v_g_norm_b, v_s5_a_re, v_s5_a_im, v_s5_b_re, v_s5_b_im, v_s5_c_re, v_s5_c_im, v_s5_d, v_s5_log_dt, v_w_glu, v_b_glu, v_w_out_o, v_w_up, v_w_down, v_w_ple_gate, v_w_ple_proj, v_final_norm):
    args = (x, p, norm_mix, norm_mlp, norm_ple, w_in_e, w_out_e, hgrn_lb, g_norm_a, conv_w, a_log, dt_bias, g_norm_b, s5_a_re, s5_a_im, s5_b_re, s5_b_im, s5_c_re, s5_c_im, s5_d, s5_log_dt, w_glu, b_glu, w_out_o, w_up, w_down, w_ple_gate, w_ple_proj, final_norm, loss_target, m_norm_mix, m_norm_mlp, m_norm_ple, m_w_in_e, m_w_out_e, m_hgrn_lb, m_g_norm_a, m_conv_w, m_a_log, m_dt_bias, m_g_norm_b, m_s5_a_re, m_s5_a_im, m_s5_b_re, m_s5_b_im, m_s5_c_re, m_s5_c_im, m_s5_d, m_s5_log_dt, m_w_glu, m_b_glu, m_w_out_o, m_w_up, m_w_down, m_w_ple_gate, m_w_ple_proj, m_final_norm, v_norm_mix, v_norm_mlp, v_norm_ple, v_w_in_e, v_w_out_e, v_hgrn_lb, v_g_norm_a, v_conv_w, v_a_log, v_dt_bias, v_g_norm_b, v_s5_a_re, v_s5_a_im, v_s5_b_re, v_s5_b_im, v_s5_c_re, v_s5_c_im, v_s5_d, v_s5_log_dt, v_w_glu, v_b_glu, v_w_out_o, v_w_up, v_w_down, v_w_ple_gate, v_w_ple_proj, v_final_norm)
    return _step(dict(zip(_INPUTS, args)))
```

```python
import functools
import math

import jax
import jax.numpy as jnp
from jax import lax
from jax.experimental import pallas as pl
from jax.experimental.pallas import tpu as pltpu

F32 = jnp.float32
BF16 = jnp.bfloat16

D_MODEL = 2048
SEQ = 4096
NORM_EPS = 1e-6
CHUNK = 64
HEAD = 128
VMEM_LIMIT = 56 * 1024 * 1024


_NN = (((1,), (0,)), ((), ()))
_NT = (((1,), (1,)), ((), ()))
_TN = (((0,), (0,)), ((), ()))
_HI = lax.Precision.HIGHEST
_NEG = -1e30


def _cparams(sem, **kw):
    return pltpu.CompilerParams(dimension_semantics=sem, vmem_limit_bytes=VMEM_LIMIT, **kw)


PACK_COLS = 2048


def _view_shape(view):
    kind, _, r = view
    return (4 * r, PACK_COLS) if kind == "row" else (r, 4 * PACK_COLS)


def _view_spec(view, rb, cb, row_of, col_of):
    kind, off, r = view
    assert off % rb == 0 and r % rb == 0 and PACK_COLS % cb == 0, (view, rb, cb)
    if kind == "row":
        nrb = r // rb
        return pl.BlockSpec((None, rb, cb), lambda i, j, k: (row_of(i, j, k) // nrb,
                                                             off // rb + row_of(i, j, k) % nrb, col_of(i, j, k)))
    ncb = PACK_COLS // cb
    return pl.BlockSpec((None, rb, cb), lambda i, j, k: (col_of(i, j, k) // ncb,
                                                         off // rb + row_of(i, j, k), col_of(i, j, k) % ncb))


def _mm(a, b, *, dims="nn", epilogue=None, extras=(), out_dtypes=(F32,), bm=1024, bn=1024, bk=2048, name,
        b_view=None, out_into=None, out_view=None, after=()):
    b_shape = _view_shape(b_view) if b_view is not None else b.shape
    if dims == "tn":
        (K, M), (K2, N) = a.shape, b_shape
    elif dims == "nt":
        (M, K), (N, K2) = a.shape, b_shape
    else:
        (M, K), (K2, N) = a.shape, b_shape
    assert K == K2, (a.shape, b_shape, dims)
    if b_view is not None and dims == "nt":
        bn = min(bn, b_view[2])
    if b_view is not None and dims != "nt":
        bk = min(bk, b_view[2])
    if out_view is not None:
        bm = min(bm, out_view[2])
    bm, bn, bk = min(bm, M), min(bn, N), min(bk, K)
    assert M % bm == 0 and N % bn == 0 and K % bk == 0, (M, N, K, bm, bn, bk)
    nk = K // bk
    ii, jj, kk = (lambda i, j, k: i), (lambda i, j, k: j), (lambda i, j, k: k)
    if dims == "tn":
        a_spec = pl.BlockSpec((bk, bm), lambda i, j, k: (k, i))
        dn = _TN
    else:
        a_spec = pl.BlockSpec((bm, bk), lambda i, j, k: (i, k))
        dn = _NT if dims == "nt" else _NN
    if dims == "nt":
        b_spec = _view_spec(b_view, bn, bk, jj, kk) if b_view else pl.BlockSpec((bn, bk), lambda i, j, k: (j, k))
    else:
        b_spec = _view_spec(b_view, bk, bn, kk, jj) if b_view else pl.BlockSpec((bk, bn), lambda i, j, k: (k, j))
    e_specs = []
    for e in extras:
        if e.shape == (M, N):
            e_specs.append(pl.BlockSpec((bm, bn), lambda i, j, k: (i, j)))
        else:
            assert e.shape == (1, N), e.shape
            e_specs.append(pl.BlockSpec((1, bn), lambda i, j, k: (0, j)))
    ne, no = len(extras), len(out_dtypes)
    if epilogue is None:
        epilogue = lambda acc: (acc,)
    into = out_into is not None

    def body(a_ref, b_ref, *rest):
        e_refs, rest = rest[:ne], rest[ne + (1 if into else 0) + len(after):]
        o_refs = rest[:no]
        part = lax.dot_general(a_ref[...].astype(BF16), b_ref[...].astype(BF16), dn, preferred_element_type=F32)

        def finish(total):
            outs = epilogue(total, *[e[...] for e in e_refs])
            for o, v in zip(o_refs, outs):
                o[...] = v.astype(o.dtype)

        if nk == 1:
            finish(part)
            return
        acc, k = rest[no], pl.program_id(2)

        @pl.when(k == 0)
        def _():
            acc[...] = part

        @pl.when((k > 0) & (k < nk - 1))
        def _():
            acc[...] += part

        @pl.when(k == nk - 1)
        def _():
            finish(acc[...] + part)

    if into:
        assert no == 1 and _view_shape(out_view) == (M, N), (out_view, M, N)
        out_specs = [_view_spec(out_view, bm, bn, ii, jj)]
        out_shape = [jax.ShapeDtypeStruct(out_into.shape, out_into.dtype)]
        extra_in, extra_specs, alias = [out_into], [pl.BlockSpec(memory_space=pl.ANY)], {2 + ne: 0}
    else:
        out_specs = [pl.BlockSpec((bm, bn), lambda i, j, k: (i, j)) for _ in out_dtypes]
        out_shape = [jax.ShapeDtypeStruct((M, N), dt) for dt in out_dtypes]
        extra_in, extra_specs, alias = [], [], {}
    outs = pl.pallas_call(
        body, name=name,
        grid=(M // bm, N // bn, nk),
        in_specs=[a_spec, b_spec] + e_specs + extra_specs + [pl.BlockSpec(memory_space=pl.ANY)] * len(after),
        out_specs=out_specs, out_shape=out_shape,
        scratch_shapes=[pltpu.VMEM((bm, bn), F32)] if nk > 1 else [],
        input_output_aliases=alias,
        compiler_params=_cparams(("parallel", "parallel", "arbitrary")),
    )(a, b, *extras, *extra_in, *after)
    return outs[0] if no == 1 else tuple(outs)


def _rowwise(fn, tiled, bcast, outs, sums=(), *, rows=256, name):
    S = tiled[0].shape[0]
    rows = min(rows, S)
    assert S % rows == 0
    nt, nb, no, ns = len(tiled), len(bcast), len(outs), len(sums)

    def body(*refs):
        t_refs, b_refs = refs[:nt], refs[nt:nt + nb]
        o_refs, s_refs = refs[nt + nb:nt + nb + no], refs[nt + nb + no:]
        res = fn(*[r[...] for r in t_refs], *[r[...] for r in b_refs])
        res = res if isinstance(res, (tuple, list)) else (res,)
        for o, v in zip(o_refs, res[:no]):
            o[...] = v.astype(o.dtype)
        if ns:
            @pl.when(pl.program_id(0) == 0)
            def _():
                for s in s_refs:
                    s[...] = jnp.zeros_like(s)
            for s, v in zip(s_refs, res[no:]):
                s[...] += v

    res = pl.pallas_call(
        body, name=name,
        grid=(S // rows,),
        in_specs=[pl.BlockSpec((rows, t.shape[1]), lambda i: (i, 0)) for t in tiled]
        + [pl.BlockSpec(b.shape, lambda i, nd=b.ndim: (0,) * nd) for b in bcast],
        out_specs=[pl.BlockSpec((rows, c), lambda i: (i, 0)) for c, _ in outs]
        + [pl.BlockSpec((1, c), lambda i: (0, 0)) for c in sums],
        out_shape=[jax.ShapeDtypeStruct((S, c), dt) for c, dt in outs]
        + [jax.ShapeDtypeStruct((1, c), F32) for c in sums],
        compiler_params=_cparams(("arbitrary",)),
    )(*tiled, *bcast)
    return res[0] if len(res) == 1 else tuple(res)


def _rms(x, g):
    return x * lax.rsqrt(jnp.mean(x * x, axis=-1, keepdims=True) + NORM_EPS) * g


def _mxu(a, b, dn):
    if a.ndim == 3:
        ((ca,), (cb,)), _ = dn
        dn = (((ca + 1,), (cb + 1,)), ((0,), (0,)))
    return lax.dot_general(a, b, dn, preferred_element_type=F32)


def _split(a):
    hi = a.astype(BF16)
    return hi, (a - hi.astype(F32)).astype(BF16)


def _passes(a, b, dn, three):
    if not three:
        return _mxu(a.astype(BF16), b.astype(BF16), dn)
    (ah, al), (bh, bl) = _split(a), _split(b)
    return _mxu(ah, bh, dn) + (_mxu(ah, bl, dn) + _mxu(al, bh, dn))


def _dot_grads(a, b, g, dn, three):
    if dn == _NN:
        return _passes(g, b, _NT, three), _passes(a, g, _TN, three)
    if dn == _NT:
        return _passes(g, b, _NN, three), _passes(g, a, _TN, three)
    assert dn == _TN
    return _passes(b, g, _NT, three), _passes(a, g, _NN, three)


@functools.partial(jax.custom_vjp, nondiff_argnums=(2,))
def _dot(a, b, dn=_NN):
    return _passes(a, b, dn, False)


_dot.defvjp(lambda a, b, dn: (_passes(a, b, dn, False), (a, b)),
            lambda dn, res, g: _dot_grads(res[0], res[1], g, dn, False))


@functools.partial(jax.custom_vjp, nondiff_argnums=(2,))
def _dot3(a, b, dn=_NN):
    return _passes(a, b, dn, True)


_dot3.defvjp(lambda a, b, dn: (_passes(a, b, dn, True), (a, b)),
             lambda dn, res, g: _dot_grads(res[0], res[1], g, dn, True))


def _tri_times(x, dn):
    tri = _tril(x.shape[-2]).astype(BF16)
    if x.ndim == 3:
        tri = jnp.broadcast_to(tri, (x.shape[0],) + tri.shape)
    hi, lo = _split(x)
    lo2 = (x - hi.astype(F32) - lo.astype(F32)).astype(BF16)
    return _mxu(tri, hi, dn) + (_mxu(tri, lo, dn) + _mxu(tri, lo2, dn))


@jax.custom_vjp
def _cumsum_rows(x):
    return _tri_times(x, _NN)


_cumsum_rows.defvjp(lambda x: (_tri_times(x, _NN), None), lambda _, g: (_tri_times(g, _TN),))


def _iota(shape, dim):
    return lax.broadcasted_iota(jnp.int32, shape, dim)


def _tril(n, strict=False):
    r, c = _iota((n, n), 0), _iota((n, n), 1)
    return (r > c) if strict else (r >= c)


@functools.partial(jax.custom_vjp, nondiff_argnums=(1,))
def _roll_rows(x, r):
    return pltpu.roll(x, r, 0)


def _roll_rows_fwd(x, r):
    return pltpu.roll(x, r, 0), None


def _roll_rows_bwd(r, _, g):
    return (pltpu.roll(g, (g.shape[0] - r) % g.shape[0], 0),)


_roll_rows.defvjp(_roll_rows_fwd, _roll_rows_bwd)


def _head_norm_gate(o, gn, gate):
    return _rms(o, gn) * jax.nn.silu(gate)


_SUB = 16
_HPS = 8


def _hgrn_chunk(q, fpre, v, gate, lb, gna, st):
    c = q.shape[0]
    forget = lb + (1.0 - lb) * jax.nn.sigmoid(fpre)
    k = 1.0 - forget
    logf = jnp.log(forget)
    cum = _cumsum_rows(logf)
    cum_end = jnp.sum(logf, axis=0, keepdims=True)
    o = _dot(q * jnp.exp(cum), st, _NT)
    st_new = st * jnp.exp(cum_end) + _dot(v, k * jnp.exp(cum_end - cum), _TN)
    t = _iota((c, 1), 0)
    s_off = jnp.zeros((c, c), F32)
    for i in range(1, c // _SUB):
        before = t < i * _SUB
        c_i = jnp.sum(jnp.where(before, logf, 0.0), axis=0, keepdims=True)
        in_blk = (t >= i * _SUB) & (t < (i + 1) * _SUB)
        qi = jnp.where(in_blk, q * jnp.exp(jnp.minimum(cum - c_i, 0.0)), 0.0)
        ki = jnp.where(before, k * jnp.exp(jnp.minimum(c_i - cum, 0.0)), 0.0)
        s_off = s_off + _dot(qi, ki, _NT)
    o = o + _dot(s_off, v)
    tmod = t % _SUB
    for r in range(_SUB):
        kr, cr, vr = (k, cum, v) if r == 0 else (_roll_rows(k, r), _roll_rows(cum, r), _roll_rows(v, r))
        w = q * kr * jnp.exp(jnp.where(tmod >= r, cum - cr, _NEG))
        o = o + jnp.sum(w, axis=1, keepdims=True) * vr
    return _head_norm_gate(o, gna, gate), st_new


def _hgrn_fwd(proj, lb, gna):
    S = proj.shape[0]
    n_chunks, heads = S // CHUNK, 8

    def body(q_ref, f_ref, v_ref, g_ref, lb_ref, gna_ref, o_ref, st_out, st):
        n, hb = pl.program_id(0), pl.program_id(1)

        @pl.when(n == 0)
        def _():
            st[hb] = jnp.zeros((_HPS, HEAD, HEAD), F32)

        sls = [slice(hh * HEAD, (hh + 1) * HEAD) for hh in range(_HPS)]
        st0 = [st[hb, hh] for hh in range(_HPS)]
        ins = [(q_ref[:, sl], f_ref[:, sl], v_ref[:, sl], g_ref[:, sl], lb_ref[:, sl]) for sl in sls]
        gna = gna_ref[...]
        res = [_hgrn_chunk(*ins[hh], gna, st0[hh]) for hh in range(_HPS)]
        for hh, sl in enumerate(sls):
            st_out[hh] = st0[hh]
            o_ref[:, sl] = res[hh][0].astype(o_ref.dtype)
            st[hb, hh] = res[hh][1]

    wide = _HPS * HEAD
    sec = lambda s: pl.BlockSpec((CHUNK, wide), lambda n, h, s=s: (n, (8 // _HPS) * s + h))
    return pl.pallas_call(
        body, name="hgrn_fwd", grid=(n_chunks, heads // _HPS),
        in_specs=[sec(0), sec(1), sec(2), sec(3),
                  pl.BlockSpec((1, wide), lambda n, h: (0, h)), pl.BlockSpec((1, HEAD), lambda n, h: (0, 0))],
        out_specs=[pl.BlockSpec((CHUNK, wide), lambda n, h: (n, h)),
                   pl.BlockSpec((None, _HPS, HEAD, HEAD), lambda n, h: (n, h, 0, 0))],
        out_shape=[jax.ShapeDtypeStruct((S, 1024), BF16),
                   jax.ShapeDtypeStruct((n_chunks, heads, HEAD, HEAD), F32)],
        scratch_shapes=[pltpu.VMEM((heads // _HPS, _HPS, HEAD, HEAD), F32)],
        compiler_params=_cparams(("arbitrary", "arbitrary")),
    )(proj, proj, proj, proj, lb, gna)


def _hgrn_bwd(proj, lb, gna, states, d_o, after=()):
    S = proj.shape[0]
    n_chunks, heads = S // CHUNK, 8

    def body(q_ref, f_ref, v_ref, g_ref, lb_ref, gna_ref, st_ref, do_ref, *rest):
        dq_ref, df_ref, dv_ref, dg_ref, dlb_ref, dgna_ref, dst = rest[len(after):]
        n, hb = pl.program_id(0), pl.program_id(1)

        @pl.when(n == 0)
        def _():
            dst[hb] = jnp.zeros((_HPS, HEAD, HEAD), F32)
            dlb_ref[hb] = jnp.zeros((_HPS, 1, HEAD), F32)

        @pl.when((n == 0) & (hb == 0))
        def _():
            dgna_ref[...] = jnp.zeros_like(dgna_ref)

        sls = [slice(hh * HEAD, (hh + 1) * HEAD) for hh in range(_HPS)]
        gna = gna_ref[...]
        ins = [(q_ref[:, sl], f_ref[:, sl], v_ref[:, sl], g_ref[:, sl], lb_ref[:, sl], gna, st_ref[hh])
               for hh, sl in enumerate(sls)]
        cts = [(do_ref[:, sl], dst[hb, hh]) for hh, sl in enumerate(sls)]
        grads = [jax.vjp(_hgrn_chunk, *ins[hh])[1](cts[hh]) for hh in range(_HPS)]
        dgna_sum = jnp.zeros((1, HEAD), F32)
        for hh, sl in enumerate(sls):
            dq, df, dv, dg, dlb, dgna, dst0 = grads[hh]
            dq_ref[:, sl] = dq.astype(dq_ref.dtype)
            df_ref[:, sl] = df.astype(df_ref.dtype)
            dv_ref[:, sl] = dv.astype(dv_ref.dtype)
            dg_ref[:, sl] = dg.astype(dg_ref.dtype)
            dlb_ref[hb, hh] += dlb
            dgna_sum = dgna_sum + dgna
            dst[hb, hh] = dst0
        dgna_ref[...] += dgna_sum

    rev = lambda n: n_chunks - 1 - n
    wide = _HPS * HEAD
    sec = lambda s: pl.BlockSpec((CHUNK, wide), lambda n, h, s=s: (rev(n), (8 // _HPS) * s + h))
    out = pl.BlockSpec((CHUNK, wide), lambda n, h: (rev(n), h))
    return pl.pallas_call(
        body, name="hgrn_bwd", grid=(n_chunks, heads // _HPS),
        in_specs=[sec(0), sec(1), sec(2), sec(3),
                  pl.BlockSpec((1, wide), lambda n, h: (0, h)), pl.BlockSpec((1, HEAD), lambda n, h: (0, 0)),
                  pl.BlockSpec((None, _HPS, HEAD, HEAD), lambda n, h: (rev(n), h, 0, 0)),
                  pl.BlockSpec((CHUNK, wide), lambda n, h: (rev(n), h))] + [pl.BlockSpec(memory_space=pl.ANY)] * len(after),
        out_specs=[out, out, out, out,
                   pl.BlockSpec((heads // _HPS, _HPS, 1, HEAD), lambda n, h: (0, 0, 0, 0)),
                   pl.BlockSpec((1, HEAD), lambda n, h: (0, 0))],
        out_shape=[jax.ShapeDtypeStruct((S, 1024), BF16)] * 4
        + [jax.ShapeDtypeStruct((heads // _HPS, _HPS, 1, HEAD), F32), jax.ShapeDtypeStruct((1, HEAD), F32)],
        scratch_shapes=[pltpu.VMEM((heads // _HPS, _HPS, HEAD, HEAD), F32)],
        compiler_params=_cparams(("arbitrary", "arbitrary")),
    )(proj, proj, proj, proj, lb, gna, states, d_o, *after)


def _l2n(t):
    return t * lax.rsqrt(jnp.sum(t * t, axis=-1, keepdims=True) + NORM_EPS)


def _unit_lower_inverse(low):
    c = low.shape[-1]
    inv = (_iota((c, c), 0) == _iota((c, c), 1)).astype(F32) - low
    p = low
    span = 2
    while span < c:
        p = _dot3(p, p)
        inv = inv + _dot3(inv, p)
        span *= 2
    return inv


def _gdn_chunk(qc, kc, v, z, tail, h0, arow, dtrow, gnb, st):
    nb, c = qc.shape[0], qc.shape[1]
    head = h0 + _iota((nb, c, HEAD), 0)
    lane = _iota((nb, c, HEAD), 2)
    la_all = arow * jax.nn.softplus(tail + dtrow)
    la = jnp.sum(jnp.where(lane == head, la_all[None], 0.0), axis=2, keepdims=True)
    beta = jnp.sum(jnp.where(lane == head + 8, jax.nn.sigmoid(tail)[None], 0.0), axis=2, keepdims=True)
    la_b = jnp.broadcast_to(la, (nb, c, HEAD))
    cum = _cumsum_rows(la_b)
    cmat = _cumsum_rows(jnp.broadcast_to(la, (nb, c, c)))
    cum_end = jnp.sum(la_b, axis=1, keepdims=True)
    decay = jnp.exp(jnp.where(_tril(c), cmat - jnp.swapaxes(cmat, 1, 2), _NEG))
    q = _l2n(qc) * (HEAD ** -0.5)
    k = _l2n(kc)
    k_beta = k * beta
    low = jnp.where(_tril(c, strict=True), _dot(k_beta, k, _NT) * decay, 0.0)
    inv = _unit_lower_inverse(low)
    u, w = _dot(inv, v * beta), _dot(inv, k_beta * jnp.exp(cum))
    intra = _dot(q, k, _NT) * decay
    v_new = u - _dot(w, st, _NT)
    o = _dot(q * jnp.exp(cum), st, _NT) + _dot(intra, v_new)
    st_new = st * jnp.exp(cum_end) + _dot(v_new, k * jnp.exp(cum_end - cum), _TN)
    return _head_norm_gate(o, gnb, z), st_new


def _gdn_specs(n_of):
    wide = _HPS * HEAD
    qkv = lambda s: pl.BlockSpec((CHUNK, wide), lambda n, h, s=s: (n_of(n), (8 // _HPS) * s + h))
    return [qkv(0), qkv(1), qkv(2),
            pl.BlockSpec((CHUNK, wide), lambda n, h: (n_of(n), 56 // _HPS + h)),
            pl.BlockSpec((CHUNK, HEAD), lambda n, h: (n_of(n), 64)),
            pl.BlockSpec((1, HEAD), lambda n, h: (0, 0)), pl.BlockSpec((1, HEAD), lambda n, h: (0, 0)),
            pl.BlockSpec((1, HEAD), lambda n, h: (0, 0))]


def _gdn_fwd(act, proj, arow, dtrow, gnb):
    S = act.shape[0]
    n_chunks, heads = S // CHUNK, 8

    def body(q_ref, k_ref, v_ref, z_ref, t_ref, a_ref, dt_ref, gnb_ref, o_ref, st_out, st):
        n, hb = pl.program_id(0), pl.program_id(1)

        @pl.when(n == 0)
        def _():
            st[hb] = jnp.zeros((_HPS, HEAD, HEAD), F32)

        sls = [slice(hh * HEAD, (hh + 1) * HEAD) for hh in range(_HPS)]
        heads_of = lambda ref: jnp.stack([ref[:, sl] for sl in sls])
        st0 = st[hb]
        o, st1 = _gdn_chunk(heads_of(q_ref), heads_of(k_ref), heads_of(v_ref), heads_of(z_ref), t_ref[...], hb * _HPS,
                            a_ref[...], dt_ref[...], gnb_ref[...], st0)
        st_out[...] = st0
        st[hb] = st1
        for hh, sl in enumerate(sls):
            o_ref[:, sl] = o[hh].astype(o_ref.dtype)

    return pl.pallas_call(
        body, name="gdn_fwd", grid=(n_chunks, heads // _HPS),
        in_specs=_gdn_specs(lambda n: n),
        out_specs=[pl.BlockSpec((CHUNK, _HPS * HEAD), lambda n, h: (n, h)),
                   pl.BlockSpec((None, _HPS, HEAD, HEAD), lambda n, h: (n, h, 0, 0))],
        out_shape=[jax.ShapeDtypeStruct((S, 1024), BF16),
                   jax.ShapeDtypeStruct((n_chunks, heads, HEAD, HEAD), F32)],
        scratch_shapes=[pltpu.VMEM((heads // _HPS, _HPS, HEAD, HEAD), F32)],
        compiler_params=_cparams(("arbitrary", "arbitrary")),
    )(act, act, act, proj, proj, arow, dtrow, gnb)


def _gdn_bwd(act, proj, arow, dtrow, gnb, states, d_o, after=()):
    S = act.shape[0]
    n_chunks, heads = S // CHUNK, 8

    def body(q_ref, k_ref, v_ref, z_ref, t_ref, a_ref, dt_ref, gnb_ref, st_ref, do_ref, *rest):
        dq_ref, dk_ref, dv_ref, dz_ref, dt_out, da_ref, ddt_ref, dgnb_ref, dst, dtail_acc = rest[len(after):]
        n, hb = pl.program_id(0), pl.program_id(1)
        n_hb = heads // _HPS

        @pl.when(n == 0)
        def _():
            dst[hb] = jnp.zeros((_HPS, HEAD, HEAD), F32)

        @pl.when((n == 0) & (hb == 0))
        def _():
            da_ref[...] = jnp.zeros_like(da_ref)
            ddt_ref[...] = jnp.zeros_like(ddt_ref)
            dgnb_ref[...] = jnp.zeros_like(dgnb_ref)

        sls = [slice(hh * HEAD, (hh + 1) * HEAD) for hh in range(_HPS)]
        heads_of = lambda ref: jnp.stack([ref[:, sl] for sl in sls])
        fn = lambda qc, kc, v, z, tail, arow, dtrow, gnb, st: _gdn_chunk(qc, kc, v, z, tail, hb * _HPS, arow, dtrow,
                                                                        gnb, st)
        _, vjp = jax.vjp(fn, heads_of(q_ref), heads_of(k_ref), heads_of(v_ref), heads_of(z_ref), t_ref[...],
                         a_ref[...], dt_ref[...], gnb_ref[...], st_ref[...])
        dq, dk, dv, dz, dtail_sum, da_sum, ddt_sum, dgnb_sum, dst0 = vjp((heads_of(do_ref), dst[hb]))
        dst[hb] = dst0
        for hh, sl in enumerate(sls):
            dq_ref[:, sl] = dq[hh]
            dk_ref[:, sl] = dk[hh]
            dv_ref[:, sl] = dv[hh]
            dz_ref[:, sl] = dz[hh].astype(dz_ref.dtype)

        @pl.when(hb == 0)
        def _():
            dtail_acc[...] = dtail_sum

        @pl.when(hb > 0)
        def _():
            dtail_acc[...] += dtail_sum

        @pl.when(hb == n_hb - 1)
        def _():
            dt_out[...] = dtail_acc[...].astype(dt_out.dtype)

        da_ref[...] += da_sum
        ddt_ref[...] += ddt_sum
        dgnb_ref[...] += dgnb_sum

    rev = lambda n: n_chunks - 1 - n
    out = pl.BlockSpec((CHUNK, _HPS * HEAD), lambda n, h: (rev(n), h))
    row = pl.BlockSpec((1, HEAD), lambda n, h: (0, 0))
    return pl.pallas_call(
        body, name="gdn_bwd", grid=(n_chunks, heads // _HPS),
        in_specs=_gdn_specs(rev)
        + [pl.BlockSpec((None, _HPS, HEAD, HEAD), lambda n, h: (rev(n), h, 0, 0)),
           pl.BlockSpec((CHUNK, _HPS * HEAD), lambda n, h: (rev(n), 8 // _HPS + h))]
        + [pl.BlockSpec(memory_space=pl.ANY)] * len(after),
        out_specs=[out, out, out, out, pl.BlockSpec((CHUNK, HEAD), lambda n, h: (rev(n), 0)), row, row, row],
        out_shape=[jax.ShapeDtypeStruct((S, 1024), F32)] * 3
        + [jax.ShapeDtypeStruct((S, 1024), BF16), jax.ShapeDtypeStruct((S, HEAD), BF16)]
        + [jax.ShapeDtypeStruct((1, HEAD), F32)] * 3,
        scratch_shapes=[pltpu.VMEM((heads // _HPS, _HPS, HEAD, HEAD), F32), pltpu.VMEM((CHUNK, HEAD), F32)],
        compiler_params=_cparams(("arbitrary", "arbitrary")),
    )(act, act, act, proj, proj, arow, dtrow, gnb, states, d_o, *after)


def _conv_silu(x, w):
    t = _iota((x.shape[0], 1), 0)
    tap = _iota(w.shape, 0)
    y = jnp.zeros_like(x)
    for r in range(4):
        w_r = jnp.sum(jnp.where(tap == 3 - r, w, 0.0), axis=0, keepdims=True)
        y = y + (x if r == 0 else jnp.where(t >= r, _roll_rows(x, r), 0.0)) * w_r
    return jax.nn.silu(y)


_CONV_COLS = 128


def _conv_fwd(proj, conv_w):
    S = proj.shape[0]
    nb = 3072 // _CONV_COLS
    off = 4096 // _CONV_COLS

    def body(x_ref, w_ref, o_ref):
        o_ref[...] = _conv_silu(x_ref[...], w_ref[...])

    return pl.pallas_call(
        body, name="conv_fwd", grid=(nb,),
        in_specs=[pl.BlockSpec((S, _CONV_COLS), lambda j: (0, off + j)), pl.BlockSpec((4, _CONV_COLS), lambda j: (0, j))],
        out_specs=pl.BlockSpec((S, _CONV_COLS), lambda j: (0, j)),
        out_shape=jax.ShapeDtypeStruct((S, 3072), F32),
        compiler_params=_cparams(("parallel",)),
    )(proj, conv_w)


def _conv_bwd(proj, conv_w, dq, dk, dv):
    S = proj.shape[0]
    nb = 3072 // _CONV_COLS
    off = 4096 // _CONV_COLS
    per = 1024 // _CONV_COLS

    def body(x_ref, w_ref, dq_ref, dk_ref, dv_ref, dx_ref, dw_ref):
        j = pl.program_id(0)
        _, vjp = jax.vjp(_conv_silu, x_ref[...], w_ref[...])
        d = jnp.where(j < per, dq_ref[...], jnp.where(j < 2 * per, dk_ref[...], dv_ref[...]))
        dx, dw = vjp(d)
        dx_ref[...] = dx.astype(dx_ref.dtype)
        dw_ref[...] = dw

    dsp = lambda s: pl.BlockSpec((S, _CONV_COLS), lambda j, s=s: (0, jnp.clip(j - s * per, 0, per - 1)))
    return pl.pallas_call(
        body, name="conv_bwd", grid=(nb,),
        in_specs=[pl.BlockSpec((S, _CONV_COLS), lambda j: (0, off + j)), pl.BlockSpec((4, _CONV_COLS), lambda j: (0, j)),
                  dsp(0), dsp(1), dsp(2)],
        out_specs=[pl.BlockSpec((S, _CONV_COLS), lambda j: (0, j)), pl.BlockSpec((4, _CONV_COLS), lambda j: (0, j))],
        out_shape=[jax.ShapeDtypeStruct((S, 3072), BF16), jax.ShapeDtypeStruct((4, 3072), F32)],
        compiler_params=_cparams(("parallel",)),
    )(proj, conv_w, dq, dk, dv)


_S5_T = 512
_S5_L = 512
_S5_NB = 16
_S5_U = 4
_S5_UB = 3


def _s5_tile(xr, xi, pw_r, pw_i, cr, ci, reverse):
    t8 = _iota((8, 1), 0)
    for sh in (1, 2, 4):
        row = (8 - sh) if reverse else (sh - 1)
        ar, ai = pw_r[row:row + 1, :], pw_i[row:row + 1, :]
        if reverse:
            keep, amt = t8 < 8 - sh, 8 - sh
        else:
            keep, amt = t8 >= sh, sh
        sr = jnp.where(keep, pltpu.roll(xr, amt, 0), 0.0)
        si = jnp.where(keep, pltpu.roll(xi, amt, 0), 0.0)
        xr, xi = xr + ar * sr - ai * si, xi + ar * si + ai * sr
    xr, xi = xr + pw_r * cr - pw_i * ci, xi + pw_r * ci + pw_i * cr
    return xr, xi


def _s5_fwd(u, bre, bim, cre, cim, pw_r, pw_i, dskip):
    S = u.shape[0]
    T = min(_S5_T, S)
    nt = S // T

    def body(u_ref, bre_ref, bim_ref, cre_ref, cim_ref, pr_ref, pi_ref, d_ref, y_ref, xr_ref, xi_ref,
             bu_r, bu_i, car_r, car_i):
        @pl.when(pl.program_id(1) == 0)
        def _():
            car_r[...] = jnp.zeros_like(car_r)
            car_i[...] = jnp.zeros_like(car_i)

        uu = u_ref[...]
        bu_r[...] = _dot(uu, bre_ref[...])
        bu_i[...] = _dot(uu, bim_ref[...])
        pw_r, pw_i = pr_ref[...], pi_ref[...]

        def tiles(i, carry):
            ins = []
            for k in range(_S5_U):
                r0 = pl.multiple_of((i * _S5_U + k) * 8, 8)
                ins.append((r0, bu_r[pl.ds(r0, 8), :], bu_i[pl.ds(r0, 8), :]))
            outs = []
            for r0, br, bi in ins:
                xr, xi = _s5_tile(br, bi, pw_r, pw_i, carry[0], carry[1], False)
                carry = (xr[7:8, :], xi[7:8, :])
                outs.append((r0, xr, xi))
            for r0, xr, xi in outs:
                xr_ref[pl.ds(r0, 8), :] = xr
                xi_ref[pl.ds(r0, 8), :] = xi
            return carry

        cr, ci = lax.fori_loop(0, T // (8 * _S5_U), tiles, (car_r[...], car_i[...]))
        car_r[...] = cr
        car_i[...] = ci
        y_ref[...] = _dot(xr_ref[...], cre_ref[...]) - _dot(xi_ref[...], cim_ref[...]) + d_ref[...] * uu

    blk3 = lambda a, b: pl.BlockSpec((None, a, b), lambda j, t: (j, 0, 0))
    return pl.pallas_call(
        body, name="s5_fwd", grid=(_S5_NB, nt),
        in_specs=[pl.BlockSpec((T, HEAD), lambda j, t: (t, j)),
                  blk3(HEAD, _S5_L), blk3(HEAD, _S5_L), blk3(_S5_L, HEAD), blk3(_S5_L, HEAD),
                  blk3(8, _S5_L), blk3(8, _S5_L), pl.BlockSpec((1, HEAD), lambda j, t: (0, j))],
        out_specs=[pl.BlockSpec((T, HEAD), lambda j, t: (t, j)),
                   pl.BlockSpec((T, _S5_L), lambda j, t: (t, j)), pl.BlockSpec((T, _S5_L), lambda j, t: (t, j))],
        out_shape=[jax.ShapeDtypeStruct((S, D_MODEL), F32),
                   jax.ShapeDtypeStruct((S, _S5_NB * _S5_L), F32), jax.ShapeDtypeStruct((S, _S5_NB * _S5_L), F32)],
        scratch_shapes=[pltpu.VMEM((T, _S5_L), F32), pltpu.VMEM((T, _S5_L), F32),
                        pltpu.VMEM((1, _S5_L), F32), pltpu.VMEM((1, _S5_L), F32)],
        compiler_params=_cparams(("parallel", "arbitrary")),
    )(u, bre, bim, cre, cim, pw_r, pw_i, dskip)


def _s5_bwd(dy, u, xre, xim, bre, bim, cre, cim, qw_r, qw_i, dskip):
    S = u.shape[0]
    T = min(_S5_T, S)
    nt = S // T
    nt8 = T // 8

    def body(dy_ref, u_ref, xr_ref, xi_ref, xpr_ref, xpi_ref, bre_ref, bim_ref, cre_ref, cim_ref, qr_ref, qi_ref,
             d_ref, du_ref, dbr_ref, dbi_ref, dcr_ref, dci_ref, dlr_ref, dli_ref, dd_ref,
             g_r, g_i, car_r, car_i):
        t = pl.program_id(1)

        @pl.when(t == 0)
        def _():
            car_r[...] = jnp.zeros_like(car_r)
            car_i[...] = jnp.zeros_like(car_i)
            for r in (dbr_ref, dbi_ref, dcr_ref, dci_ref, dlr_ref, dli_ref, dd_ref):
                r[...] = jnp.zeros_like(r)

        dyy, uu = dy_ref[...], u_ref[...]
        g_r[...] = _dot(dyy, cre_ref[...], _NT)
        g_i[...] = -_dot(dyy, cim_ref[...], _NT)
        qw_r, qw_i = qr_ref[...], qi_ref[...]
        t8 = _iota((8, 1), 0)
        first = t == nt - 1

        def load(r0, prev_r, prev_i):
            rows = pl.ds(r0, 8)
            return r0, g_r[rows, :], g_i[rows, :], xr_ref[rows, :], xi_ref[rows, :], prev_r, prev_i

        def run(loaded, carry, acc):
            done = []
            for r0, dr, di, xr, xi, prev_r, prev_i in loaded:
                gr, gi = _s5_tile(dr, di, qw_r, qw_i, carry[0], carry[1], True)
                carry = (gr[0:1, :], gi[0:1, :])
                xpr = jnp.where(t8 >= 1, pltpu.roll(xr, 1, 0), prev_r)
                xpi = jnp.where(t8 >= 1, pltpu.roll(xi, 1, 0), prev_i)
                acc = (acc[0] + gr * xpr + gi * xpi, acc[1] + gi * xpr - gr * xpi)
                done.append((r0, gr, gi))
            for r0, gr, gi in done:
                g_r[pl.ds(r0, 8), :] = gr
                g_i[pl.ds(r0, 8), :] = gi
            return carry, acc

        def step(ii, state):
            loaded = []
            for k in range(_S5_UB):
                idx = nt8 - 1 - (ii * _S5_UB + k)
                r0 = pl.multiple_of(idx * 8, 8)
                p0 = pl.multiple_of((idx - 1) * 8, 8)
                loaded.append(load(r0, xr_ref[pl.ds(p0, 8), :][7:8, :], xi_ref[pl.ds(p0, 8), :][7:8, :]))
            return run(loaded, *state)

        zero = jnp.zeros((8, _S5_L), F32)
        assert (nt8 - 1) % _S5_UB == 0
        state = lax.fori_loop(0, (nt8 - 1) // _S5_UB, step, ((car_r[...], car_i[...]), (zero, zero)))
        prev_r = jnp.where(first, 0.0, xpr_ref[...][7:8, :])
        prev_i = jnp.where(first, 0.0, xpi_ref[...][7:8, :])
        (cr, ci), (ar, ai) = run([load(0, prev_r, prev_i)], *state)
        car_r[...] = cr
        car_i[...] = ci
        dlr_ref[...] += ar
        dli_ref[...] += ai
        gr, gi = g_r[...], g_i[...]
        du_ref[...] = _dot(gr, bre_ref[...], _NT) + _dot(gi, bim_ref[...], _NT) + d_ref[...] * dyy
        dbr_ref[...] += _dot(uu, gr, _TN)
        dbi_ref[...] += _dot(uu, gi, _TN)
        dcr_ref[...] += _dot(xr_ref[...], dyy, _TN)
        dci_ref[...] -= _dot(xi_ref[...], dyy, _TN)
        dd_ref[...] += jnp.sum(dyy * uu, axis=0, keepdims=True)

    rev = lambda t: nt - 1 - t
    blk3 = lambda a, b: pl.BlockSpec((None, a, b), lambda j, t: (j, 0, 0))
    tl = pl.BlockSpec((T, HEAD), lambda j, t: (rev(t), j))
    xs = pl.BlockSpec((T, _S5_L), lambda j, t: (rev(t), j))
    xp = pl.BlockSpec((8, _S5_L), lambda j, t: (jnp.maximum(rev(t) * nt8 - 1, 0), j))
    return pl.pallas_call(
        body, name="s5_bwd", grid=(_S5_NB, nt),
        in_specs=[tl, tl, xs, xs, xp, xp, blk3(HEAD, _S5_L), blk3(HEAD, _S5_L), blk3(_S5_L, HEAD), blk3(_S5_L, HEAD),
                  blk3(8, _S5_L), blk3(8, _S5_L), pl.BlockSpec((1, HEAD), lambda j, t: (0, j))],
        out_specs=[tl, blk3(HEAD, _S5_L), blk3(HEAD, _S5_L), blk3(_S5_L, HEAD), blk3(_S5_L, HEAD),
                   blk3(8, _S5_L), blk3(8, _S5_L), pl.BlockSpec((1, HEAD), lambda j, t: (0, j))],
        out_shape=[jax.ShapeDtypeStruct((S, D_MODEL), F32),
                   jax.ShapeDtypeStruct((_S5_NB, HEAD, _S5_L), F32), jax.ShapeDtypeStruct((_S5_NB, HEAD, _S5_L), F32),
                   jax.ShapeDtypeStruct((_S5_NB, _S5_L, HEAD), F32), jax.ShapeDtypeStruct((_S5_NB, _S5_L, HEAD), F32),
                   jax.ShapeDtypeStruct((_S5_NB, 8, _S5_L), F32), jax.ShapeDtypeStruct((_S5_NB, 8, _S5_L), F32),
                   jax.ShapeDtypeStruct((1, D_MODEL), F32)],
        scratch_shapes=[pltpu.VMEM((T, _S5_L), F32), pltpu.VMEM((T, _S5_L), F32),
                        pltpu.VMEM((1, _S5_L), F32), pltpu.VMEM((1, _S5_L), F32)],
        compiler_params=_cparams(("parallel", "arbitrary")),
    )(dy, u, xre, xim, xre, xim, bre, bim, cre, cim, qw_r, qw_i, dskip)


def _s5_params(a_re, a_im, log_dt, b_re, b_im, c_re, c_im):
    step = jnp.exp(log_dt)[:, None]
    mag = jnp.exp(a_re * step)
    lr, li = mag * jnp.cos(a_im * step), mag * jnp.sin(a_im * step)
    den = a_re * a_re + a_im * a_im
    nr, ni = lr - 1.0, li
    kr, ki = (nr * a_re + ni * a_im) / den, (ni * a_re - nr * a_im) / den
    bbr = kr[..., None] * b_re - ki[..., None] * b_im
    bbi = kr[..., None] * b_im + ki[..., None] * b_re
    eye = jnp.eye(8, dtype=F32)

    def blk_b(bb):
        t = bb.reshape(_S5_NB, 8, 64, 16).transpose(0, 1, 3, 2)
        return (t[:, :, :, None, :] * eye[None, :, None, :, None]).reshape(_S5_NB, HEAD, _S5_L)

    def blk_c(cc):
        t = cc.reshape(_S5_NB, 8, 16, 64).transpose(0, 1, 3, 2)
        return (t[:, :, :, None, :] * eye[None, :, None, :, None]).reshape(_S5_NB, _S5_L, HEAD)

    return (blk_b(bbr), blk_b(bbi), blk_c(c_re), blk_c(c_im),
            lr.reshape(_S5_NB, _S5_L), li.reshape(_S5_NB, _S5_L))


def _s5_powers(lr, li):
    pr, pi = [lr], [li]
    for _ in range(7):
        pr, pi = pr + [pr[-1] * lr - pi[-1] * li], pi + [pr[-1] * li + pi[-1] * lr]
    return jnp.stack(pr, axis=1), jnp.stack(pi, axis=1)


_MESH = pl.DeviceIdType.MESH
_ANY = pl.BlockSpec(memory_space=pl.ANY)


def _place():
    x, y, c = lax.axis_index("x"), lax.axis_index("y"), lax.axis_index("c")
    return x, y, c, [(1 - x, y), (x, 1 - y), (1 - x, 1 - y)]


def _comm_call(body, arrs, out_shapes, n_remote, name):
    n = len(arrs)
    return pl.pallas_call(
        body, name=name,
        in_specs=[_ANY] * n, out_specs=[_ANY] * n, out_shape=out_shapes,
        scratch_shapes=[pltpu.SemaphoreType.DMA((n * n_remote,)), pltpu.SemaphoreType.DMA((n * n_remote,)),
                        pltpu.SemaphoreType.DMA((n,))],
    )(*arrs)


def _half(ref, slot, h):
    if len(ref.shape) == 4:
        return ref.at[slot, h]
    cols = ref.shape[2] // 2
    return ref.at[slot, :, pl.ds(pl.multiple_of(h * cols, 128), cols)]


def _own_slot(shard, me_idx, name, rows=256, cols=512):
    def body(me_ref, in_ref, o_ref):
        o_ref[...] = in_ref[...]

    if shard.ndim == 2:
        R, C = shard.shape
        return pl.pallas_call(
            body, name=name,
            grid_spec=pltpu.PrefetchScalarGridSpec(
                num_scalar_prefetch=1, grid=(C // cols,),
                in_specs=[pl.BlockSpec((R, cols), lambda j, me: (0, j))],
                out_specs=pl.BlockSpec((None, R, cols), lambda j, me: (me[0], 0, j))),
            out_shape=jax.ShapeDtypeStruct((4,) + shard.shape, shard.dtype),
            compiler_params=_cparams(("parallel",)),
        )(me_idx, shard)
    _, half, C = shard.shape
    rows = min(rows, half)
    assert half % rows == 0
    return pl.pallas_call(
        body, name=name,
        grid_spec=pltpu.PrefetchScalarGridSpec(
            num_scalar_prefetch=1, grid=(2, half // rows),
            in_specs=[pl.BlockSpec((None, rows, C), lambda h, i, me: (h, i, 0))],
            out_specs=pl.BlockSpec((None, None, rows, C), lambda h, i, me: (me[0], h, i, 0))),
        out_shape=jax.ShapeDtypeStruct((4,) + shard.shape, shard.dtype),
        compiler_params=_cparams(("parallel", "parallel")),
    )(me_idx, shard)


def _own_rows(w, layer, buf, me_idx, row_off, after, name, rows=256):
    _, r, C = w.shape
    carried = [] if isinstance(buf, tuple) else [buf]
    shape = buf if isinstance(buf, tuple) else buf.shape
    assert r % rows == 0 and row_off % rows == 0

    def body(me_ref, in_ref, *rest):
        rest[-1][...] = in_ref[...].astype(BF16)

    return pl.pallas_call(
        body, name=name,
        grid_spec=pltpu.PrefetchScalarGridSpec(
            num_scalar_prefetch=1, grid=(r // rows,),
            in_specs=[pl.BlockSpec((None, rows, C), lambda i, me: (layer, i, 0))] + [_ANY] * (len(carried) + len(after)),
            out_specs=pl.BlockSpec((None, rows, C), lambda i, me: (me[0], row_off // rows + i, 0))),
        out_shape=jax.ShapeDtypeStruct(shape, BF16),
        input_output_aliases={2: 0} if carried else {},
        compiler_params=_cparams(("parallel",)),
    )(me_idx, w, *carried, *after)


def _pair_split(arrs, name):
    n = len(arrs)

    def body(*refs):
        ins, outs, (ssem, rsem, _) = refs[:n], refs[n:2 * n], refs[2 * n:]
        x, y, c, _ = _place()
        copies = []
        for a in range(n):
            for s in range(4):
                cp = pltpu.make_async_remote_copy(src_ref=_half(ins[a], s, 1 - c), dst_ref=outs[a].at[s],
                                                  send_sem=ssem.at[4 * a + s], recv_sem=rsem.at[4 * a + s],
                                                  device_id=(x, y, 1 - c), device_id_type=_MESH)
                cp.start()
                copies.append(cp)
        for cp in copies:
            cp.wait()

    shapes = [jax.ShapeDtypeStruct((4,) + (a.shape[2:] if a.ndim == 4 else (a.shape[1], a.shape[2] // 2)), a.dtype)
              for a in arrs]
    return _comm_call(body, arrs, shapes, 4, name)


def _pair_swap(arrs, name):
    n = len(arrs)

    def body(*refs):
        ins, outs, (ssem, rsem, _) = refs[:n], refs[n:2 * n], refs[2 * n:]
        x, y, c, _ = _place()
        copies = []
        for a in range(n):
            cp = pltpu.make_async_remote_copy(src_ref=ins[a], dst_ref=outs[a], send_sem=ssem.at[a], recv_sem=rsem.at[a],
                                              device_id=(x, y, 1 - c), device_id_type=_MESH)
            cp.start()
            copies.append(cp)
        for cp in copies:
            cp.wait()

    return _comm_call(body, arrs, [jax.ShapeDtypeStruct(a.shape, a.dtype) for a in arrs], 1, name)


def _pair_sum(full, recv, c_idx, name, rows=128):
    def add(c_ref, a_ref, b_ref, o_ref):
        o_ref[...] = (a_ref[...].astype(F32) + b_ref[...].astype(F32)).astype(o_ref.dtype)

    if full.ndim == 3:
        _, R, cols = recv.shape
        blk = lambda col: pl.BlockSpec((None, R, cols), lambda s, c: (s, 0, col(c)))
        return pl.pallas_call(
            add, name=name,
            grid_spec=pltpu.PrefetchScalarGridSpec(
                num_scalar_prefetch=1, grid=(4,),
                in_specs=[blk(lambda c: c[0]), blk(lambda c: 0)], out_specs=blk(lambda c: 0)),
            out_shape=jax.ShapeDtypeStruct(recv.shape, recv.dtype),
            compiler_params=_cparams(("parallel",)),
        )(c_idx, full, recv)
    _, _, half, C = full.shape
    rows = min(rows, half)
    nb = half // rows
    assert half % rows == 0

    def body(c_ref, a_ref, b_ref, o_ref):
        o_ref[...] = (a_ref[...].astype(F32) + b_ref[...].astype(F32)).astype(o_ref.dtype)

    return pl.pallas_call(
        body, name=name,
        grid_spec=pltpu.PrefetchScalarGridSpec(
            num_scalar_prefetch=1, grid=(4, nb),
            in_specs=[pl.BlockSpec((None, None, rows, C), lambda s, i, c: (s, c[0], i, 0)),
                      pl.BlockSpec((None, rows, C), lambda s, i, c: (s, i, 0))],
            out_specs=pl.BlockSpec((None, rows, C), lambda s, i, c: (s, i, 0))),
        out_shape=jax.ShapeDtypeStruct(recv.shape, recv.dtype),
        compiler_params=_cparams(("parallel", "parallel")),
    )(c_idx, full, recv)


_HBM = pl.BlockSpec(memory_space=pltpu.HBM)
_SEM = pl.BlockSpec(memory_space=pltpu.SEMAPHORE)
_SPLIT = dict(has_side_effects=pltpu.SideEffectType.DATAFLOW_SIDE_EFFECTING)


def _hbm(t):
    return pltpu.with_memory_space_constraint(t, pltpu.HBM)


def _ag_copy(buf_ref, ssem, rsem, j, chip, c, me):
    px, py = chip
    mine = _half(buf_ref, me, c)
    return pltpu.make_async_remote_copy(src_ref=mine, dst_ref=mine, send_sem=ssem.at[j], recv_sem=rsem.at[j],
                                        device_id=(px, py, c), device_id_type=_MESH)


def _ag_start(bufs, after, name):
    n = len(bufs)

    def body(*refs):
        buf_refs, (ssem, rsem), token = refs[:n], refs[n + 1:n + 3], refs[-1]
        x, y, c, chips = _place()
        for a in range(n):
            for j, chip in enumerate(chips):
                _ag_copy(buf_refs[a], ssem, rsem, 3 * a + j, chip, c, 2 * x + y).start()
        token[...] = jnp.zeros_like(token)

    out = pl.pallas_call(
        body, name=name,
        out_shape=(pltpu.SemaphoreType.DMA((3 * n,)), pltpu.SemaphoreType.DMA((3 * n,)),
                   *[pltpu.HBM(b.shape, b.dtype) for b in bufs], jax.ShapeDtypeStruct((8, HEAD), F32)),
        in_specs=(*[_HBM] * n, _ANY), out_specs=(_SEM, _SEM, *[_HBM] * n, pl.BlockSpec(memory_space=pltpu.VMEM)),
        input_output_aliases={a: 2 + a for a in range(n)}, compiler_params=pltpu.CompilerParams(**_SPLIT),
    )(*[_hbm(b) for b in bufs], after)
    return out[0], out[1], list(out[2:2 + n]), out[-1]


def _ag_wait(ssem, rsem, bufs, after, name):
    n = len(bufs)

    def body(*refs):
        buf_refs, ssem, rsem = refs[:n], refs[n], refs[n + 1]
        x, y, c, chips = _place()
        for a in range(n):
            for j, chip in enumerate(chips):
                cp = _ag_copy(buf_refs[a], ssem, rsem, 3 * a + j, chip, c, 2 * x + y)
                cp.wait_send()
                cp.wait_recv()

    after = after if isinstance(after, (tuple, list)) else (after,)
    return list(pl.pallas_call(
        body, name=name, out_shape=tuple(pltpu.HBM(b.shape, b.dtype) for b in bufs),
        in_specs=(*[_HBM] * n, _SEM, _SEM, *[_ANY] * len(after)), out_specs=tuple([_HBM] * n),
        input_output_aliases={a: a for a in range(n)}, compiler_params=pltpu.CompilerParams(**_SPLIT),
    )(*bufs, ssem, rsem, *after))


def _pair_forward(bufs, name):
    n = len(bufs)

    def body(*refs):
        outs, (ssem, rsem) = refs[n:2 * n], refs[2 * n:]
        x, y, c, chips = _place()
        copies = []
        for a in range(n):
            for j, (px, py) in enumerate(chips):
                landed = _half(outs[a], 2 * px + py, c)
                cp = pltpu.make_async_remote_copy(src_ref=landed, dst_ref=landed, send_sem=ssem.at[3 * a + j],
                                                  recv_sem=rsem.at[3 * a + j], device_id=(x, y, 1 - c),
                                                  device_id_type=_MESH)
                cp.start()
                copies.append(cp)
        for cp in copies:
            cp.wait()

    return pl.pallas_call(
        body, name=name, in_specs=[_ANY] * n, out_specs=[_ANY] * n,
        out_shape=[jax.ShapeDtypeStruct(b.shape, b.dtype) for b in bufs],
        input_output_aliases={a: a for a in range(n)},
        scratch_shapes=[pltpu.SemaphoreType.DMA((3 * n,)), pltpu.SemaphoreType.DMA((3 * n,))],
    )(*bufs)


def _split_copy(part_ref, land_ref, ssem, rsem, s, x, y, c):
    return pltpu.make_async_remote_copy(src_ref=_half(part_ref, s, 1 - c), dst_ref=land_ref.at[s], send_sem=ssem.at[s],
                                        recv_sem=rsem.at[s], device_id=(x, y, 1 - c), device_id_type=_MESH)


def _pair_split_start(part, after, name):
    land = jax.ShapeDtypeStruct((4,) + part.shape[2:], part.dtype)

    def body(part_ref, land_ref, after_ref, ssem, rsem, part_thru, land_thru, token):
        x, y, c, _ = _place()
        for s in range(4):
            _split_copy(part_ref, land_ref, ssem, rsem, s, x, y, c).start()
        token[...] = jnp.zeros_like(token)

    return pl.pallas_call(
        body, name=name,
        out_shape=(pltpu.SemaphoreType.DMA((4,)), pltpu.SemaphoreType.DMA((4,)), pltpu.HBM(part.shape, part.dtype),
                   pltpu.HBM(land.shape, land.dtype), jax.ShapeDtypeStruct((8, HEAD), F32)),
        in_specs=(_HBM, _HBM, _ANY), out_specs=(_SEM, _SEM, _HBM, _HBM, pl.BlockSpec(memory_space=pltpu.VMEM)),
        input_output_aliases={0: 2, 1: 3}, compiler_params=pltpu.CompilerParams(**_SPLIT),
    )(_hbm(part), _hbm(lax.empty(land.shape, land.dtype)), after)


def _pair_split_wait(ssem, rsem, part, land, after, name):
    def body(part_ref, land_ref, ssem, rsem, after_ref, part_out, land_out):
        x, y, c, _ = _place()
        for s in range(4):
            cp = _split_copy(part_ref, land_ref, ssem, rsem, s, x, y, c)
            cp.wait_send()
            cp.wait_recv()

    return pl.pallas_call(
        body, name=name, out_shape=(pltpu.HBM(part.shape, part.dtype), pltpu.HBM(land.shape, land.dtype)),
        in_specs=(_HBM, _HBM, _SEM, _SEM, _ANY), out_specs=(_HBM, _HBM),
        input_output_aliases={0: 0, 1: 1}, compiler_params=pltpu.CompilerParams(**_SPLIT),
    )(part, land, ssem, rsem, after)


def _rs_copy(src_ref, land_ref, ssem, rsem, j, chip, c):
    px, py = chip
    return pltpu.make_async_remote_copy(src_ref=src_ref.at[2 * px + py], dst_ref=land_ref.at[j], send_sem=ssem.at[j],
                                        recv_sem=rsem.at[j], device_id=(px, py, c), device_id_type=_MESH)


def _rs_start(part, after, name):
    def body(part_ref, land_ref, after_ref, ssem, rsem, part_thru, land_thru, token):
        x, y, c, chips = _place()
        for j, chip in enumerate(chips):
            _rs_copy(part_ref, land_ref, ssem, rsem, j, chip, c).start()
        token[...] = jnp.zeros_like(token)

    land = jax.ShapeDtypeStruct((3,) + part.shape[1:], part.dtype)
    return pl.pallas_call(
        body, name=name,
        out_shape=(pltpu.SemaphoreType.DMA((3,)), pltpu.SemaphoreType.DMA((3,)), pltpu.HBM(part.shape, part.dtype),
                   pltpu.HBM(land.shape, land.dtype), jax.ShapeDtypeStruct((8, HEAD), F32)),
        in_specs=(_HBM, _HBM, _ANY), out_specs=(_SEM, _SEM, _HBM, _HBM, pl.BlockSpec(memory_space=pltpu.VMEM)),
        input_output_aliases={0: 2, 1: 3}, compiler_params=pltpu.CompilerParams(**_SPLIT),
    )(_hbm(part), _hbm(lax.empty(land.shape, land.dtype)), after)


def _rs_wait(ssem, rsem, part, land, after, name):
    def body(part_ref, land_ref, ssem, rsem, after_ref, part_out, land_out):
        x, y, c, chips = _place()
        for j, chip in enumerate(chips):
            cp = _rs_copy(part_ref, land_ref, ssem, rsem, j, chip, c)
            cp.wait_send()
            cp.wait_recv()

    return pl.pallas_call(
        body, name=name, out_shape=(pltpu.HBM(part.shape, part.dtype), pltpu.HBM(land.shape, land.dtype)),
        in_specs=(_HBM, _HBM, _SEM, _SEM, _ANY), out_specs=(_HBM, _HBM),
        input_output_aliases={0: 0, 1: 1}, compiler_params=pltpu.CompilerParams(**_SPLIT),
    )(part, land, ssem, rsem, after)


def _sum_own_recv(part, land, me_idx, name, rows=128):
    _, half, C = part.shape

    def body(me_ref, own_ref, land_ref, o_ref):
        acc = own_ref[...].astype(F32)
        for s in range(3):
            acc = acc + land_ref[s].astype(F32)
        o_ref[...] = acc

    if half % 8:
        cols = 256
        return pl.pallas_call(
            body, name=name,
            grid_spec=pltpu.PrefetchScalarGridSpec(
                num_scalar_prefetch=1, grid=(C // cols,),
                in_specs=[pl.BlockSpec((None, half, cols), lambda j, me: (me[0], 0, j)),
                          pl.BlockSpec((3, half, cols), lambda j, me: (0, 0, j))],
                out_specs=pl.BlockSpec((half, cols), lambda j, me: (0, j))),
            out_shape=jax.ShapeDtypeStruct((half, C), F32),
            compiler_params=_cparams(("parallel",)),
        )(me_idx, part, land)
    rows = min(rows, half)
    assert half % rows == 0
    return pl.pallas_call(
        body, name=name,
        grid_spec=pltpu.PrefetchScalarGridSpec(
            num_scalar_prefetch=1, grid=(half // rows,),
            in_specs=[pl.BlockSpec((None, rows, C), lambda i, me: (me[0], i, 0)),
                      pl.BlockSpec((3, rows, C), lambda i, me: (0, i, 0))],
            out_specs=pl.BlockSpec((rows, C), lambda i, me: (i, 0))),
        out_shape=jax.ShapeDtypeStruct((half, C), F32),
        compiler_params=_cparams(("parallel",)),
    )(me_idx, part, land)


def _sum_slots(arr, name, rows=128):
    k, R, C = arr.shape
    rows = min(rows, R)
    assert R % rows == 0

    def body(in_ref, o_ref):
        acc = in_ref[0].astype(F32)
        for s in range(1, k):
            acc = acc + in_ref[s].astype(F32)
        o_ref[...] = acc

    return pl.pallas_call(
        body, name=name, grid=(R // rows,),
        in_specs=[pl.BlockSpec((k, rows, C), lambda i: (0, i, 0))],
        out_specs=pl.BlockSpec((rows, C), lambda i: (i, 0)),
        out_shape=jax.ShapeDtypeStruct((R, C), F32),
        compiler_params=_cparams(("parallel",)),
    )(arr)


ADAM_LR, ADAM_B1, ADAM_B2, ADAM_EPS, ADAM_WD, ADAM_STEP = 0.001, 0.9, 0.999, 1e-08, 0.01, 10


def _adam_math(w, g, m, v):
    m = ADAM_B1 * m + (1.0 - ADAM_B1) * g
    v = ADAM_B2 * v + (1.0 - ADAM_B2) * jnp.square(g)
    m_hat = m / (1.0 - ADAM_B1 ** ADAM_STEP)
    v_hat = v / (1.0 - ADAM_B2 ** ADAM_STEP)
    delta = -ADAM_LR * (m_hat / (jnp.sqrt(v_hat) + ADAM_EPS) + ADAM_WD * w)
    return delta, m, v


def _adamw(w, m, v, layer, mine, other, c_idx, name, g_off=0, prev=None, after=(), rows=128):
    _, R, C = w.shape
    half = mine.shape[0]
    rows = min(rows, R)
    assert R % rows == 0 and g_off % rows == 0 and half % rows == 0
    nbh, b0 = half // rows, g_off // rows

    def body(c_ref, w_ref, m_ref, v_ref, mine_ref, other_ref, *rest):
        go, do, mo, vo = rest[-4:]
        in_my_half = (b0 + pl.program_id(0)) // nbh == c_ref[0]
        g = jnp.where(in_my_half, mine_ref[...], other_ref[...])
        delta, m1, v1 = _adam_math(w_ref[...], g, m_ref[...], v_ref[...])
        go[...] = g
        do[...] = delta
        mo[...] = m1
        vo[...] = v1

    blk = pl.BlockSpec((None, rows, C), lambda i, c: (layer, i, 0))

    def gblk(of_mine):
        return pl.BlockSpec((rows, C), lambda i, c: (
            jnp.where(((b0 + i) // nbh == c[0]) == of_mine, (b0 + i) % nbh, 0), 0))

    carried = list(prev) if prev is not None else []
    return pl.pallas_call(
        body, name=name,
        grid_spec=pltpu.PrefetchScalarGridSpec(
            num_scalar_prefetch=1, grid=(R // rows,),
            in_specs=[blk] * 3 + [gblk(True), gblk(False)] + [_ANY] * (len(carried) + len(after)),
            out_specs=[blk] * 4),
        out_shape=[jax.ShapeDtypeStruct(w.shape, F32)] * 4,
        input_output_aliases={6 + k: k for k in range(len(carried))},
        compiler_params=_cparams(("parallel",)),
    )(c_idx, w, m, v, mine, other, *carried, *after)


def _adamw_t(w, m, v, mine, other, c_idx, name, rows=513, cols=128):
    R, _, C = w.shape
    nbh = (C // 2) // cols
    assert R % rows == 0 and (C // 2) % cols == 0

    def body(c_ref, w_ref, m_ref, v_ref, mine_ref, other_ref, go, do, mo, vo):
        in_my_half = pl.program_id(1) // nbh == c_ref[0]
        g = jnp.where(in_my_half, mine_ref[...], other_ref[...])
        delta, m1, v1 = _adam_math(w_ref[...], g, m_ref[...], v_ref[...])
        go[...] = g
        do[...] = delta
        mo[...] = m1
        vo[...] = v1

    blk = pl.BlockSpec((rows, 1, cols), lambda i, j, c: (i, 0, j))
    gblk = pl.BlockSpec((rows, 1, cols), lambda i, j, c: (i, 0, j % nbh))
    return pl.pallas_call(
        body, name=name,
        grid_spec=pltpu.PrefetchScalarGridSpec(
            num_scalar_prefetch=1, grid=(R // rows, C // cols), in_specs=[blk] * 3 + [gblk] * 2,
            out_specs=[blk] * 4),
        out_shape=[jax.ShapeDtypeStruct(w.shape, F32)] * 4,
        compiler_params=_cparams(("parallel", "parallel")),
    )(c_idx, w, m, v, mine, other)


_WEIGHTS = ['norm_mix', 'norm_mlp', 'norm_ple', 'w_in_e', 'w_out_e', 'hgrn_lb', 'g_norm_a', 'conv_w', 'a_log',
            'dt_bias', 'g_norm_b', 's5_a_re', 's5_a_im', 's5_b_re', 's5_b_im', 's5_c_re', 's5_c_im', 's5_d',
            's5_log_dt', 'w_glu', 'b_glu', 'w_out_o', 'w_up', 'w_down', 'w_ple_gate', 'w_ple_proj', 'final_norm']
_INPUTS = ['x', 'p'] + _WEIGHTS + ['loss_target'] + ['m_' + n for n in _WEIGHTS] + ['v_' + n for n in _WEIGHTS]
_FAMILY = [('w_up', 0, 'col', 2048), ('w_down', 0, 'row', 2048), ('w_ple_gate', 0, 'row', 512), ('w_out_e', 0, 'row', 512),
           ('w_up', 1, 'col', 2048), ('w_down', 1, 'row', 2048), ('w_ple_gate', 1, 'row', 512), ('w_glu', 0, 'row', 512),
           ('w_out_o', 0, 'row', 512)]
_PACK = {('w_up', 0): 0, ('w_down', 0): 0, ('w_ple_gate', 0): 0, ('w_out_e', 0): 0,
         ('w_up', 1): 1, ('w_down', 1): 1, ('w_ple_gate', 1): 1, ('w_glu', 0): 1, ('w_out_o', 0): 1}
_IN_PAD = 8320
_IN_BLK = 1664


def _rms_bwd_fn(x, d_hn, d_res, g):
    _, vjp = jax.vjp(_rms, x, g)
    dx, dg = vjp(d_hn.astype(F32))
    return dx + d_res, dg


def _rms_bwd_both(x, d_hn, d_res, g):
    dx, dg = _rms_bwd_fn(x, d_hn, d_res, g)
    return dx, dx, dg


def _add_res(acc, h):
    return (acc + h,)


def _pack_rows(parts, lanes=128, mult=256):
    flat = jnp.concatenate([q.reshape(-1).astype(F32) for q in parts])
    n = flat.shape[0]
    rows = -(-n // (lanes * mult)) * mult
    return jnp.pad(flat, (0, rows * lanes - n)).reshape(rows, lanes)


def _unpack_rows(buf, shapes):
    flat, out, off = buf.reshape(-1), [], 0
    for s in shapes:
        n = math.prod(s)
        out.append(flat[off:off + n].reshape(s))
        off += n
    return out


def _step(a):
    S = a['x'].shape[1]
    x, tgt = a['x'][0], a['loss_target'][0]
    xi, yi = lax.axis_index("x"), lax.axis_index("y")
    me = 2 * xi + yi
    row = lambda t: t.reshape(1, -1)

    me_idx = me.astype(jnp.int32).reshape(1)
    c_idx = lax.axis_index("c").astype(jnp.int32).reshape(1)
    small_sh = jnp.concatenate([a['conv_w'][0].reshape(-1), a['s5_d'][0], a['b_glu'][0]]).reshape(16, 256)
    rows_first = lambda t: jnp.transpose(t, (2, 0, 1))
    shards = [None, None, rows_first(a['w_in_e']).reshape(2052, D_MODEL).astype(BF16),
              a['w_ple_proj'].reshape(512, 512).astype(BF16), small_sh]
    own = lambda i: _own_slot(shards[i] if i == 2 else shards[i].reshape(2, shards[i].shape[0] // 2, shards[i].shape[1]),
                              me_idx, name=f"own_slot{i}")
    whole = lambda g, t: g.reshape((4,) + t.shape)
    first = _ag_start([own(2), own(3), own(4)], me_idx, name="gather_first_start")
    big = {nm for nm, _, _, _ in _FAMILY} | {'w_in_e', 'w_ple_proj'}
    held = ['x'] + [pre + n for n in _WEIGHTS if n not in big for pre in ('', 'm_', 'v_')]
    _, late = lax.optimization_barrier((first[3], {n: a[n] for n in held}))
    a = {**a, **late}
    x = a['x'][0]
    views, pack_of, rt = {}, {}, [0, 0]
    for nm, l, kind, r in _FAMILY:
        views[nm, l], pack_of[nm, l] = (kind, rt[_PACK[nm, l]], r), _PACK[nm, l]
        rt[_PACK[nm, l]] += r
    bufs = [(4, rt[0], PACK_COLS), (4, rt[1], PACK_COLS)]
    for nm, l, kind, r in _FAMILY:
        k = pack_of[nm, l]
        bufs[k] = _own_rows(a[nm], l, bufs[k], me_idx, views[nm, l][1], (first[3],), name=f"own_{nm}{l}")
    shards[0], shards[1] = [jax.ShapeDtypeStruct((rt[k], PACK_COLS), BF16) for k in (0, 1)]
    slots = [bufs[k].reshape(4, 2, rt[k] // 2, PACK_COLS) for k in (0, 1)]
    hn0 = _rowwise(_rms, [x], [row(a['norm_mix'][0])], [(D_MODEL, BF16)], name="rms_mix0")
    s_names = [n for n in _WEIGHTS if n not in big]
    w_pack, m_pack, v_pack = [_pack_rows([a[pre + n] for n in s_names]) for pre in ('', 'm_', 'v_')]
    landed = _pair_forward(_ag_wait(first[0], first[1], first[2], (hn0, bufs[0], bufs[1], w_pack, m_pack, v_pack),
                                    name="gather_first_wait"), name="gather_first_pass")
    g_in, g_pp, g_small = [whole(g, t) for g, t in zip(landed, shards[2:])]
    gw = {}
    W = lambda nm, l: dict(b=gw[pack_of[nm, l]], b_view=views[nm, l])
    sems0 = _ag_start([slots[0]], g_in, name="gather0_start")
    win_t = jnp.pad(g_in.reshape(8208, D_MODEL), ((0, _IN_PAD - 8208), (0, 0)))
    w_pp = g_pp.reshape(4, 2, 256, 512).transpose(1, 2, 0, 3).reshape(2, 256, D_MODEL)
    g_small = g_small.reshape(4, 4096)
    conv_full = g_small[:, :3072].reshape(4, 4, 768).transpose(1, 0, 2).reshape(4, 3072)
    s5_d_full = g_small[:, 3072:3584].reshape(1, D_MODEL)
    b_glu_full = g_small[:, 3584:].reshape(1, D_MODEL)

    lb, lb_vjp = jax.vjp(lambda hl: jax.nn.softmax(hl, axis=0)[0:1], a['hgrn_lb'])
    pad_h = lambda t: jnp.pad(t, ((0, 0), (0, HEAD - t.shape[1])))
    (arow, dtrow), gdn_vjp = jax.vjp(lambda al, dt: (pad_h(-jnp.exp(al)), pad_h(dt)), a['a_log'], a['dt_bias'])
    s5p, s5_vjp = jax.vjp(_s5_params, a['s5_a_re'][0], a['s5_a_im'][0], a['s5_log_dt'][0], a['s5_b_re'][0],
                          a['s5_b_im'][0], a['s5_c_re'][0], a['s5_c_im'][0])
    bre, bim, cre, cim = [t.astype(BF16) for t in s5p[:4]]
    pw_r, pw_i = _s5_powers(s5p[4], s5p[5])
    gna, gnb = a['g_norm_a'], a['g_norm_b']

    def norm_cast(h, g, dt, name):
        return _rowwise(_rms, [h], [row(g)], [(D_MODEL, dt)], name=name)

    def mlp_ple_fwd(h, l):
        hn = norm_cast(h, a['norm_mlp'][l], BF16, f"rms_mlp{l}")
        up, act = _mm(hn, **W('w_up', l), out_dtypes=(BF16, BF16), name=f"up{l}",
                      epilogue=lambda acc: (acc, jnp.square(jnp.maximum(acc, 0.0))))
        h2 = _mm(act, **W('w_down', l), epilogue=_add_res, extras=(h,), name=f"down{l}")
        hnp = norm_cast(h2, a['norm_ple'][l], BF16, f"rms_ple{l}")
        pp = _mm(a['p'][l, 0], w_pp[l], name=f"ple_proj{l}")
        h3, gpre = _mm(hnp, **W('w_ple_gate', l), extras=(h2, pp), out_dtypes=(F32, F32), name=f"ple_gate{l}",
                       bm=512, epilogue=lambda acc, h2, pp: (h2 + jax.nn.sigmoid(acc) * pp, acc))
        return h3, (h, hn, up, act, h2, hnp, pp, gpre)

    proj = _mm(hn0, win_t, dims="nt", bn=_IN_BLK, name="in_proj", after=(sems0[3],))
    o_a, st_a = _hgrn_fwd(proj, lb, gna)
    act_b = _conv_fwd(proj, conv_full)
    o_b, st_b = _gdn_fwd(act_b, proj, arow, dtrow, gnb)
    merged = jnp.concatenate([o_a, o_b], axis=1)
    landed0 = _ag_wait(sems0[0], sems0[1], sems0[2], merged, name="gather0_wait")
    gw[0] = whole(_pair_forward(landed0, name="gather0_pass")[0], shards[0])
    sems1 = _ag_start([slots[1]], gw[0], name="gather1_start")
    h1 = _mm(merged, **W('w_out_e', 0), epilogue=_add_res, extras=(x,), name="out_e", after=(sems1[3],))
    h3, saved0 = mlp_ple_fwd(h1, 0)

    u = norm_cast(h3, a['norm_mix'][1], F32, "rms_mix1")
    y, xre, xim = _s5_fwd(u, bre, bim, cre, cim, pw_r, pw_i, s5_d_full)
    actg = _rowwise(jax.nn.gelu, [y], [], [(D_MODEL, BF16)], name="gelu")
    landed1 = _ag_wait(sems1[0], sems1[1], sems1[2], actg, name="gather1_wait")
    gw[1] = whole(_pair_forward(landed1, name="gather1_pass")[0], shards[1])
    glu, z = _mm(actg, **W('w_glu', 0), extras=(actg, b_glu_full), out_dtypes=(BF16, F32), name="glu",
                 epilogue=lambda acc, act, b: (act.astype(F32) * jax.nn.sigmoid(acc + b), acc + b))
    h4 = _mm(glu, **W('w_out_o', 0), epilogue=_add_res, extras=(h3,), name="out_o")
    h6, saved1 = mlp_ple_fwd(h4, 1)

    def head(h, t, g):
        def f(h, g):
            e = _rms(h, g) - t
            return 0.5 * jnp.sum(jnp.mean(e * e, axis=-1))
        val, vjp = jax.vjp(f, h, g)
        dh, dg = vjp(jnp.ones((), F32))
        return dh, dg, jnp.full((1, HEAD), val)

    dh, d_final, loss_part = _rowwise(head, [h6, tgt], [row(a['final_norm'])], [(D_MODEL, F32)], sums=(D_MODEL, HEAD),
                                      name="loss_head")
    loss = lax.psum(loss_part[0, 0], ("x", "y", "c"))

    gb = {k: lax.empty((4, rt[k], PACK_COLS), BF16) for k in (0, 1)}

    def into(lhs, rhs, key, name, after=()):
        k = pack_of[key]
        gb[k] = _mm(lhs, rhs, dims="tn", out_into=gb[k], out_view=views[key], out_dtypes=(BF16,), name=name, after=after)

    def mlp_ple_bwd(dh, l, saved, after=()):
        h, hn, up, act, h2, hnp, pp, gpre = saved

        def gate_bwd(d, gpre, pp):
            s = jax.nn.sigmoid(gpre)
            return d * s, d * pp * s * (1.0 - s)

        d_pp, d_gp = _rowwise(gate_bwd, [dh, gpre, pp], [], [(D_MODEL, BF16), (D_MODEL, BF16)], name=f"ple_bwd{l}")
        d_wpp = _mm(a['p'][l, 0], d_pp, dims="tn", name=f"d_ple_proj{l}")
        into(hnp, d_gp, ('w_ple_gate', l), f"d_ple_gate{l}")
        d_hnp = _mm(d_gp, **W('w_ple_gate', l), dims="nt", name=f"ple_gate_t{l}", after=after)
        dh2, dh2_b, d_nple = _rowwise(_rms_bwd_both, [h2, d_hnp, dh], [row(a['norm_ple'][l])],
                                      [(D_MODEL, F32), (D_MODEL, BF16)], sums=(D_MODEL,), name=f"rms_ple_bwd{l}")
        d_up = _mm(dh2_b, **W('w_down', l), dims="nt", extras=(up,), out_dtypes=(BF16,), name=f"down_t{l}",
                   epilogue=lambda acc, up: (acc * 2.0 * jnp.maximum(up.astype(F32), 0.0),))
        into(act, dh2_b, ('w_down', l), f"d_down{l}")
        into(hn, d_up, ('w_up', l), f"d_up{l}")
        d_hn = _mm(d_up, **W('w_up', l), dims="nt", name=f"up_t{l}")
        dh1, dh1_b, d_nmlp = _rowwise(_rms_bwd_both, [h, d_hn, dh2], [row(a['norm_mlp'][l])],
                                      [(D_MODEL, F32), (D_MODEL, BF16)], sums=(D_MODEL,), name=f"rms_mlp_bwd{l}")
        return dh1, dh1_b, d_wpp, d_nple, d_nmlp

    dh4, dh4_b, d_wpp1, d_nple1, d_nmlp1 = mlp_ple_bwd(dh, 1, saved1)

    d_glu = _mm(dh4_b, **W('w_out_o', 0), dims="nt", name="out_o_t")
    into(glu, dh4_b, ('w_out_o', 0), "d_out_o")

    def glu_bwd(d, z, act):
        s, act = jax.nn.sigmoid(z), act.astype(F32)
        dz = d * act * s * (1.0 - s)
        return dz, d * s, jnp.sum(dz, axis=0, keepdims=True)

    d_z, d_actp, d_bglu = _rowwise(glu_bwd, [d_glu, z, actg], [], [(D_MODEL, BF16), (D_MODEL, F32)], sums=(D_MODEL,),
                                   name="glu_bwd")
    into(actg, d_z, ('w_glu', 0), "d_glu")

    def reduce_start(bufs, tag):
        parts = [t if t.shape[1] % 32 else t.reshape(4, 2, t.shape[1] // 2, t.shape[2]) for t in bufs]
        recv = _pair_split(parts, name=f"pair_split{tag}")
        sums = [_pair_sum(f, r, c_idx, name=f"pair_sum{tag}_{i}") for i, (f, r) in enumerate(zip(parts, recv))]
        started, prev = [], c_idx
        for i, s in enumerate(sums):
            started.append(_rs_start(s, prev, name=f"scatter{tag}_{i}_start"))
            prev = started[-1][4]
        return started

    def pack_split(k):
        part = gb[k].reshape(4, 2, rt[k] // 2, PACK_COLS)
        return _pair_split_start(part, c_idx, name=f"pair_split{k}_start")

    def pack_scatter(k, split, after):
        part, recv = _pair_split_wait(split[0], split[1], split[2], split[3], after, name=f"pair_split{k}_wait")
        return [_rs_start(_pair_sum(part, recv, c_idx, name=f"pair_sum{k}_0"), c_idx, name=f"scatter{k}_0_start")]

    def reduce_finish(started, after, tag):
        halves = []
        for i, (ssem, rsem, part, land, _) in enumerate(started):
            part, land = _rs_wait(ssem, rsem, part, land, after, name=f"scatter{tag}_{i}_wait")
            halves.append(_sum_own_recv(part, land, me_idx, name=f"sum_partials{tag}_{i}"))
        return list(zip(halves, _pair_swap(halves, name=f"swap_halves{tag}")))

    split1 = pack_split(1)

    def gelu_bwd(acc, dap, y):
        _, vjp = jax.vjp(jax.nn.gelu, y)
        return vjp(acc + dap)

    dy = _mm(d_z, **W('w_glu', 0), dims="nt", extras=(d_actp, y), epilogue=gelu_bwd, name="glu_t",
             after=(split1[4],))
    du, d_bre, d_bim, d_cre, d_cim, d_lr, d_li, d_s5d = _s5_bwd(dy, u, xre, xim, bre, bim, cre, cim,
                                                                 pw_r[:, ::-1], -pw_i[:, ::-1], s5_d_full)
    dh3, d_nmix1 = _rowwise(_rms_bwd_fn, [h3, du, dh4], [row(a['norm_mix'][1])], [(D_MODEL, F32)], sums=(D_MODEL,),
                            name="rms_mix1_bwd")
    d_are, d_aim, d_logdt, d_sbre, d_sbim, d_scre, d_scim = s5_vjp(
        (d_bre, d_bim, d_cre, d_cim, d_lr.sum(axis=1), d_li.sum(axis=1)))

    started1 = pack_scatter(1, split1, dh3)
    dh1, dh1_b, d_wpp0, d_nple0, d_nmlp0 = mlp_ple_bwd(dh3, 0, saved0, after=(started1[0][4],))

    d_merged = _mm(dh1_b, **W('w_out_e', 0), dims="nt", name="out_e_t")
    into(merged, dh1_b, ('w_out_e', 0), "d_out_e")
    split0 = pack_split(0)
    dq, df, dv, dg, d_lb, d_gna = _hgrn_bwd(proj, lb, gna, st_a, d_merged, after=(split0[4],))
    started0 = pack_scatter(0, split0, dq)
    dqb, dkb, dvb, dzb, d_tail, d_arow, d_dtrow, d_gnb = _gdn_bwd(act_b, proj, arow, dtrow, gnb, st_b, d_merged,
                                                                  after=(started0[0][4],))
    d_pre, d_conv = _conv_bwd(proj, conv_full, dqb, dkb, dvb)
    d_proj = jnp.concatenate([dq, df, dv, dg, d_pre, dzb, d_tail], axis=1)
    d_win_t = _mm(d_proj, hn0, dims="tn", bm=_IN_BLK, out_dtypes=(BF16,), name="d_in_proj")
    d_hn0 = _mm(d_proj, win_t, bk=_IN_BLK, name="in_proj_t")
    grad_x, d_nmix0 = _rowwise(_rms_bwd_fn, [x, d_hn0, dh1], [row(a['norm_mix'][0])], [(D_MODEL, F32)], sums=(D_MODEL,),
                               name="rms_mix0_bwd")
    (d_hlb,) = lb_vjp(d_lb.reshape(1, 1024))
    d_alog, d_dtb = gdn_vjp((d_arow, d_dtrow))

    d_win_sh = d_win_t[:8208].reshape(4, 2052, D_MODEL)
    d_wpp_sh = jnp.stack([d_wpp0, d_wpp1]).astype(BF16).reshape(2, 256, 4, 512).transpose(2, 0, 1, 3).reshape(4, 512, 512)
    started2 = reduce_start([d_win_sh, d_wpp_sh], 2)
    small = {
        'norm_mix': jnp.concatenate([d_nmix0, d_nmix1]), 'norm_mlp': jnp.concatenate([d_nmlp0, d_nmlp1]),
        'norm_ple': jnp.concatenate([d_nple0, d_nple1]), 'hgrn_lb': d_hlb, 'g_norm_a': d_gna, 'conv_w': d_conv,
        'a_log': d_alog, 'dt_bias': d_dtb, 'g_norm_b': d_gnb, 's5_a_re': d_are, 's5_a_im': d_aim, 's5_b_re': d_sbre,
        's5_b_im': d_sbim, 's5_c_re': d_scre, 's5_c_im': d_scim, 's5_d': d_s5d, 's5_log_dt': d_logdt,
        'b_glu': d_bglu, 'final_norm': d_final}
    s_names = list(small)
    s_shapes = [tuple(small[n].shape) for n in s_names]
    mine_small = _pack_rows([small[n] for n in s_names])
    chip_small = _rowwise(lambda p, o: p + o, [mine_small, _pair_swap([mine_small], name="swap_small")[0]], [],
                          [(128, F32)], name="add_small")
    n_small = chip_small.shape[0]
    small_go = _ag_start([_own_slot(chip_small.reshape(2, n_small // 2, 128), me_idx, name="own_slot_small")],
                         started2[-1][4], name="gather_small_start")
    red = {1: reduce_finish(started1, small_go[3], 1)[0]}
    red[0] = reduce_finish(started0, small_go[3], 0)[0]
    res, order = {}, []

    def adam_layer(nm, layer, key, g_off=None):
        g_off = views[nm, layer][1] if g_off is None else g_off
        prev = res.get(nm)
        follows = () if not order or (prev is not None and order[-1] is prev[0]) else (order[-1],)
        res[nm] = tuple(_adamw(a[nm], a['m_' + nm], a['v_' + nm], layer, *red[key], c_idx, name=f"adam_{nm}{layer}",
                               g_off=g_off, prev=prev, after=follows))
        order.append(res[nm][0])

    for nm, layer in (('w_glu', 0), ('w_out_o', 0), ('w_up', 1), ('w_down', 1), ('w_ple_gate', 1)):
        adam_layer(nm, layer, 1)
    for nm in ('w_up', 'w_down', 'w_ple_gate', 'w_out_e'):
        adam_layer(nm, 0, 0)

    small_all = _pair_forward(_ag_wait(small_go[0], small_go[1], small_go[2], order[-1], name="gather_small_wait"),
                              name="gather_small_pass")[0]
    reduced = _sum_slots(small_all.reshape(4, n_small, 128), name="sum_small")
    sg = dict(zip(s_names, _unpack_rows(reduced, s_shapes)))
    sg['conv_w'] = lax.dynamic_slice_in_dim(sg['conv_w'], me * 768, 768, axis=1)
    sg['s5_d'] = lax.dynamic_slice_in_dim(sg['s5_d'], me * 512, 512, axis=1)
    sg['b_glu'] = lax.dynamic_slice_in_dim(sg['b_glu'], me * 512, 512, axis=1)
    sg = {n: sg[n].reshape(a[n].shape) for n in s_names}
    g_pack = _pack_rows([sg[n] for n in s_names])
    sd, sm, sv = _rowwise(_adam_math, [w_pack, g_pack, m_pack, v_pack], [], [(128, F32)] * 3, name="adam_small")
    w_shapes = [tuple(a[n].shape) for n in s_names]
    res.update({n: (sg[n], d_, m_, v_) for n, d_, m_, v_ in zip(s_names, _unpack_rows(sd, w_shapes),
                                                               _unpack_rows(sm, w_shapes), _unpack_rows(sv, w_shapes))})

    red['in'], red['pp'] = reduce_finish(started2, sd, 2)
    res['w_in_e'] = tuple(jnp.transpose(o, (1, 2, 0)) for o in _adamw_t(
        rows_first(a['w_in_e']), rows_first(a['m_w_in_e']), rows_first(a['v_w_in_e']),
        *[h.reshape(2052, 1, D_MODEL // 2) for h in red['in']], c_idx, name="adam_w_in_e"))
    adam_layer('w_ple_proj', 0, 'pp', 0)
    adam_layer('w_ple_proj', 1, 'pp', 256)

    return (loss, grad_x[None], *[res[n][0] for n in _WEIGHTS], *[res[n][1] for n in _WEIGHTS],
            *[res[n][2] for n in _WEIGHTS], *[res[n][3] for n in _WEIGHTS])


def kernel(x, p, norm_mix, norm_mlp, norm_ple, w_in_e, w_out_e, hgrn_lb, g_norm_a, conv_w, a_log, dt_bias, g_norm_b, s5_a_re, s5_a_im, s5_b_re, s5_b_im, s5_c_re, s5_c_im, s5_d, s5_log_dt, w_glu, b_glu, w_out_o, w_up, w_down, w_ple_gate, w_ple_proj, final_norm, loss_target, m_norm_mix, m_norm_mlp, m_norm_ple, m_w_in_e, m_w_out_e, m_hgrn_lb, m_g_norm_a, m_conv_w, m_a_log, m_dt_bias, m_g_norm_b, m_s5_a_re, m_s5_a_im, m_s5_b_re, m_s5_b_im, m_s5_c_re, m_s5_c_im, m_s5_d, m_s5_log_dt, m_w_glu, m_b_glu, m_w_out_o, m_w_up, m_w_down, m_w_ple_gate, m_w_ple_proj, m_final_norm, v_norm_mix, v_norm_mlp, v_norm_ple, v_w_in_e, v_w_out_e, v_hgrn_lb, v_g_norm_a, v_conv_w, v_a_log, v_dt_bias, v_g_norm_b, v_s5_a_re, v_s5_a_im, v_s5_b_re, v_s5_b_im, v_s5_c_re, v_s5_c_im, v_s5_d, v_s5_log_dt, v_w_glu, v_b_glu, v_w_out_o, v_w_up, v_w_down, v_w_ple_gate, v_w_ple_proj, v_final_norm):
    args = (x, p, norm_mix, norm_mlp, norm_ple, w_in_e, w_out_e, hgrn_lb, g_norm_a, conv_w, a_log, dt_bias, g_norm_b, s5_a_re, s5_a_im, s5_b_re, s5_b_im, s5_c_re, s5_c_im, s5_d, s5_log_dt, w_glu, b_glu, w_out_o, w_up, w_down, w_ple_gate, w_ple_proj, final_norm, loss_target, m_norm_mix, m_norm_mlp, m_norm_ple, m_w_in_e, m_w_out_e, m_hgrn_lb, m_g_norm_a, m_conv_w, m_a_log, m_dt_bias, m_g_norm_b, m_s5_a_re, m_s5_a_im, m_s5_b_re, m_s5_b_im, m_s5_c_re, m_s5_c_im, m_s5_d, m_s5_log_dt, m_w_glu, m_b_glu, m_w_out_o, m_w_up, m_w_down, m_w_ple_gate, m_w_ple_proj, m_final_norm, v_norm_mix, v_norm_mlp, v_norm_ple, v_w_in_e, v_w_out_e, v_hgrn_lb, v_g_norm_a, v_conv_w, v_a_log, v_dt_bias, v_g_norm_b, v_s5_a_re, v_s5_a_im, v_s5_b_re, v_s5_b_im, v_s5_c_re, v_s5_c_im, v_s5_d, v_s5_log_dt, v_w_glu, v_b_glu, v_w_out_o, v_w_up, v_w_down, v_w_ple_gate, v_w_ple_proj, v_final_norm)
    return _step(dict(zip(_INPUTS, args)))
```

```python
import functools
import math

import jax
import jax.numpy as jnp
from jax import lax
from jax.experimental import pallas as pl
from jax.experimental.pallas import tpu as pltpu

F32 = jnp.float32
BF16 = jnp.bfloat16

D_MODEL = 2048
SEQ = 4096
NORM_EPS = 1e-6
CHUNK = 64
HEAD = 128
VMEM_LIMIT = 56 * 1024 * 1024


_NN = (((1,), (0,)), ((), ()))
_NT = (((1,), (1,)), ((), ()))
_TN = (((0,), (0,)), ((), ()))
_HI = lax.Precision.HIGHEST
_NEG = -1e30


def _cparams(sem, **kw):
    return pltpu.CompilerParams(dimension_semantics=sem, vmem_limit_bytes=VMEM_LIMIT, **kw)


PACK_COLS = 2048


def _view_shape(view):
    kind, _, r = view
    return (4 * r, PACK_COLS) if kind == "row" else (r, 4 * PACK_COLS)


def _view_spec(view, rb, cb, row_of, col_of):
    kind, off, r = view
    assert off % rb == 0 and r % rb == 0 and PACK_COLS % cb == 0, (view, rb, cb)
    if kind == "row":
        nrb = r // rb
        return pl.BlockSpec((None, rb, cb), lambda i, j, k: (row_of(i, j, k) // nrb,
                                                             off // rb + row_of(i, j, k) % nrb, col_of(i, j, k)))
    ncb = PACK_COLS // cb
    return pl.BlockSpec((None, rb, cb), lambda i, j, k: (col_of(i, j, k) // ncb,
                                                         off // rb + row_of(i, j, k), col_of(i, j, k) % ncb))


def _mm(a, b, *, dims="nn", epilogue=None, extras=(), out_dtypes=(F32,), bm=1024, bn=1024, bk=2048, name,
        b_view=None, out_into=None, out_view=None, after=()):
    b_shape = _view_shape(b_view) if b_view is not None else b.shape
    if dims == "tn":
        (K, M), (K2, N) = a.shape, b_shape
    elif dims == "nt":
        (M, K), (N, K2) = a.shape, b_shape
    else:
        (M, K), (K2, N) = a.shape, b_shape
    assert K == K2, (a.shape, b_shape, dims)
    if b_view is not None and dims == "nt":
        bn = min(bn, b_view[2])
    if b_view is not None and dims != "nt":
        bk = min(bk, b_view[2])
    if out_view is not None:
        bm = min(bm, out_view[2])
    bm, bn, bk = min(bm, M), min(bn, N), min(bk, K)
    assert M % bm == 0 and N % bn == 0 and K % bk == 0, (M, N, K, bm, bn, bk)
    nk = K // bk
    ii, jj, kk = (lambda i, j, k: i), (lambda i, j, k: j), (lambda i, j, k: k)
    if dims == "tn":
        a_spec = pl.BlockSpec((bk, bm), lambda i, j, k: (k, i))
        dn = _TN
    else:
        a_spec = pl.BlockSpec((bm, bk), lambda i, j, k: (i, k))
        dn = _NT if dims == "nt" else _NN
    if dims == "nt":
        b_spec = _view_spec(b_view, bn, bk, jj, kk) if b_view else pl.BlockSpec((bn, bk), lambda i, j, k: (j, k))
    else:
        b_spec = _view_spec(b_view, bk, bn, kk, jj) if b_view else pl.BlockSpec((bk, bn), lambda i, j, k: (k, j))
    e_specs = []
    for e in extras:
        if e.shape == (M, N):
            e_specs.append(pl.BlockSpec((bm, bn), lambda i, j, k: (i, j)))
        else:
            assert e.shape == (1, N), e.shape
            e_specs.append(pl.BlockSpec((1, bn), lambda i, j, k: (0, j)))
    ne, no = len(extras), len(out_dtypes)
    if epilogue is None:
        epilogue = lambda acc: (acc,)
    into = out_into is not None

    def body(a_ref, b_ref, *rest):
        e_refs, rest = rest[:ne], rest[ne + (1 if into else 0) + len(after):]
        o_refs = rest[:no]
        part = lax.dot_general(a_ref[...].astype(BF16), b_ref[...].astype(BF16), dn, preferred_element_type=F32)

        def finish(total):
            outs = epilogue(total, *[e[...] for e in e_refs])
            for o, v in zip(o_refs, outs):
                o[...] = v.astype(o.dtype)

        if nk == 1:
            finish(part)
            return
        acc, k = rest[no], pl.program_id(2)

        @pl.when(k == 0)
        def _():
            acc[...] = part

        @pl.when((k > 0) & (k < nk - 1))
        def _():
            acc[...] += part

        @pl.when(k == nk - 1)
        def _():
            finish(acc[...] + part)

    if into:
        assert no == 1 and _view_shape(out_view) == (M, N), (out_view, M, N)
        out_specs = [_view_spec(out_view, bm, bn, ii, jj)]
        out_shape = [jax.ShapeDtypeStruct(out_into.shape, out_into.dtype)]
        extra_in, extra_specs, alias = [out_into], [pl.BlockSpec(memory_space=pl.ANY)], {2 + ne: 0}
    else:
        out_specs = [pl.BlockSpec((bm, bn), lambda i, j, k: (i, j)) for _ in out_dtypes]
        out_shape = [jax.ShapeDtypeStruct((M, N), dt) for dt in out_dtypes]
        extra_in, extra_specs, alias = [], [], {}
    outs = pl.pallas_call(
        body, name=name,
        grid=(M // bm, N // bn, nk),
        in_specs=[a_spec, b_spec] + e_specs + extra_specs + [pl.BlockSpec(memory_space=pl.ANY)] * len(after),
        out_specs=out_specs, out_shape=out_shape,
        scratch_shapes=[pltpu.VMEM((bm, bn), F32)] if nk > 1 else [],
        input_output_aliases=alias,
        compiler_params=_cparams(("parallel", "parallel", "arbitrary")),
    )(a, b, *extras, *extra_in, *after)
    return outs[0] if no == 1 else tuple(outs)


def _rowwise(fn, tiled, bcast, outs, sums=(), *, rows=512, name):
    S = tiled[0].shape[0]
    rows = min(rows, S)
    assert S % rows == 0
    nt, nb, no, ns = len(tiled), len(bcast), len(outs), len(sums)

    def body(*refs):
        t_refs, b_refs = refs[:nt], refs[nt:nt + nb]
        o_refs, s_refs = refs[nt + nb:nt + nb + no], refs[nt + nb + no:]
        res = fn(*[r[...] for r in t_refs], *[r[...] for r in b_refs])
        res = res if isinstance(res, (tuple, list)) else (res,)
        for o, v in zip(o_refs, res[:no]):
            o[...] = v.astype(o.dtype)
        if ns:
            @pl.when(pl.program_id(0) == 0)
            def _():
                for s in s_refs:
                    s[...] = jnp.zeros_like(s)
            for s, v in zip(s_refs, res[no:]):
                s[...] += v

    res = pl.pallas_call(
        body, name=name,
        grid=(S // rows,),
        in_specs=[pl.BlockSpec((rows, t.shape[1]), lambda i: (i, 0)) for t in tiled]
        + [pl.BlockSpec(b.shape, lambda i, nd=b.ndim: (0,) * nd) for b in bcast],
        out_specs=[pl.BlockSpec((rows, c), lambda i: (i, 0)) for c, _ in outs]
        + [pl.BlockSpec((1, c), lambda i: (0, 0)) for c in sums],
        out_shape=[jax.ShapeDtypeStruct((S, c), dt) for c, dt in outs]
        + [jax.ShapeDtypeStruct((1, c), F32) for c in sums],
        compiler_params=_cparams(("arbitrary",)),
    )(*tiled, *bcast)
    return res[0] if len(res) == 1 else tuple(res)


def _rms(x, g):
    return x * lax.rsqrt(jnp.mean(x * x, axis=-1, keepdims=True) + NORM_EPS) * g


def _mxu(a, b, dn):
    if a.ndim == 3:
        ((ca,), (cb,)), _ = dn
        dn = (((ca + 1,), (cb + 1,)), ((0,), (0,)))
    return lax.dot_general(a, b, dn, preferred_element_type=F32)


def _split(a):
    hi = a.astype(BF16)
    return hi, (a - hi.astype(F32)).astype(BF16)


def _passes(a, b, dn, three):
    if not three:
        return _mxu(a.astype(BF16), b.astype(BF16), dn)
    (ah, al), (bh, bl) = _split(a), _split(b)
    return _mxu(ah, bh, dn) + (_mxu(ah, bl, dn) + _mxu(al, bh, dn))


def _dot_grads(a, b, g, dn, three):
    if dn == _NN:
        return _passes(g, b, _NT, three), _passes(a, g, _TN, three)
    if dn == _NT:
        return _passes(g, b, _NN, three), _passes(g, a, _TN, three)
    assert dn == _TN
    return _passes(b, g, _NT, three), _passes(a, g, _NN, three)


@functools.partial(jax.custom_vjp, nondiff_argnums=(2,))
def _dot(a, b, dn=_NN):
    return _passes(a, b, dn, False)


_dot.defvjp(lambda a, b, dn: (_passes(a, b, dn, False), (a, b)),
            lambda dn, res, g: _dot_grads(res[0], res[1], g, dn, False))


@functools.partial(jax.custom_vjp, nondiff_argnums=(2,))
def _dot3(a, b, dn=_NN):
    return _passes(a, b, dn, True)


_dot3.defvjp(lambda a, b, dn: (_passes(a, b, dn, True), (a, b)),
             lambda dn, res, g: _dot_grads(res[0], res[1], g, dn, True))


def _tri_times(x, dn):
    tri = _tril(x.shape[-2]).astype(BF16)
    if x.ndim == 3:
        tri = jnp.broadcast_to(tri, (x.shape[0],) + tri.shape)
    hi, lo = _split(x)
    lo2 = (x - hi.astype(F32) - lo.astype(F32)).astype(BF16)
    return _mxu(tri, hi, dn) + (_mxu(tri, lo, dn) + _mxu(tri, lo2, dn))


@jax.custom_vjp
def _cumsum_rows(x):
    return _tri_times(x, _NN)


_cumsum_rows.defvjp(lambda x: (_tri_times(x, _NN), None), lambda _, g: (_tri_times(g, _TN),))


def _iota(shape, dim):
    return lax.broadcasted_iota(jnp.int32, shape, dim)


def _tril(n, strict=False):
    r, c = _iota((n, n), 0), _iota((n, n), 1)
    return (r > c) if strict else (r >= c)


@functools.partial(jax.custom_vjp, nondiff_argnums=(1,))
def _roll_rows(x, r):
    return pltpu.roll(x, r, 0)


def _roll_rows_fwd(x, r):
    return pltpu.roll(x, r, 0), None


def _roll_rows_bwd(r, _, g):
    return (pltpu.roll(g, (g.shape[0] - r) % g.shape[0], 0),)


_roll_rows.defvjp(_roll_rows_fwd, _roll_rows_bwd)


def _head_norm_gate(o, gn, gate):
    return _rms(o, gn) * jax.nn.silu(gate)


_SUB = 16
_HPS = 8


def _hgrn_chunk(q, fpre, v, gate, lb, gna, st):
    c = q.shape[0]
    forget = lb + (1.0 - lb) * jax.nn.sigmoid(fpre)
    k = 1.0 - forget
    logf = jnp.log(forget)
    cum = _cumsum_rows(logf)
    cum_end = jnp.sum(logf, axis=0, keepdims=True)
    o = _dot(q * jnp.exp(cum), st, _NT)
    st_new = st * jnp.exp(cum_end) + _dot(v, k * jnp.exp(cum_end - cum), _TN)
    t = _iota((c, 1), 0)
    s_off = jnp.zeros((c, c), F32)
    for i in range(1, c // _SUB):
        before = t < i * _SUB
        c_i = jnp.sum(jnp.where(before, logf, 0.0), axis=0, keepdims=True)
        in_blk = (t >= i * _SUB) & (t < (i + 1) * _SUB)
        qi = jnp.where(in_blk, q * jnp.exp(jnp.minimum(cum - c_i, 0.0)), 0.0)
        ki = jnp.where(before, k * jnp.exp(jnp.minimum(c_i - cum, 0.0)), 0.0)
        s_off = s_off + _dot(qi, ki, _NT)
    o = o + _dot(s_off, v)
    tmod = t % _SUB
    for r in range(_SUB):
        kr, cr, vr = (k, cum, v) if r == 0 else (_roll_rows(k, r), _roll_rows(cum, r), _roll_rows(v, r))
        w = q * kr * jnp.exp(jnp.where(tmod >= r, cum - cr, _NEG))
        o = o + jnp.sum(w, axis=1, keepdims=True) * vr
    return _head_norm_gate(o, gna, gate), st_new


def _hgrn_fwd(proj, lb, gna):
    S = proj.shape[0]
    n_chunks, heads = S // CHUNK, 8

    def body(q_ref, f_ref, v_ref, g_ref, lb_ref, gna_ref, o_ref, st_out, st):
        n, hb = pl.program_id(0), pl.program_id(1)

        @pl.when(n == 0)
        def _():
            st[hb] = jnp.zeros((_HPS, HEAD, HEAD), F32)

        sls = [slice(hh * HEAD, (hh + 1) * HEAD) for hh in range(_HPS)]
        st0 = [st[hb, hh] for hh in range(_HPS)]
        ins = [(q_ref[:, sl], f_ref[:, sl], v_ref[:, sl], g_ref[:, sl], lb_ref[:, sl]) for sl in sls]
        gna = gna_ref[...]
        res = [_hgrn_chunk(*ins[hh], gna, st0[hh]) for hh in range(_HPS)]
        for hh, sl in enumerate(sls):
            st_out[hh] = st0[hh]
            o_ref[:, sl] = res[hh][0].astype(o_ref.dtype)
            st[hb, hh] = res[hh][1]

    wide = _HPS * HEAD
    sec = lambda s: pl.BlockSpec((CHUNK, wide), lambda n, h, s=s: (n, (8 // _HPS) * s + h))
    return pl.pallas_call(
        body, name="hgrn_fwd", grid=(n_chunks, heads // _HPS),
        in_specs=[sec(0), sec(1), sec(2), sec(3),
                  pl.BlockSpec((1, wide), lambda n, h: (0, h)), pl.BlockSpec((1, HEAD), lambda n, h: (0, 0))],
        out_specs=[pl.BlockSpec((CHUNK, wide), lambda n, h: (n, h)),
                   pl.BlockSpec((None, _HPS, HEAD, HEAD), lambda n, h: (n, h, 0, 0))],
        out_shape=[jax.ShapeDtypeStruct((S, 1024), BF16),
                   jax.ShapeDtypeStruct((n_chunks, heads, HEAD, HEAD), F32)],
        scratch_shapes=[pltpu.VMEM((heads // _HPS, _HPS, HEAD, HEAD), F32)],
        compiler_params=_cparams(("arbitrary", "arbitrary")),
    )(proj, proj, proj, proj, lb, gna)


def _hgrn_bwd(proj, lb, gna, states, d_o, after=()):
    S = proj.shape[0]
    n_chunks, heads = S // CHUNK, 8

    def body(q_ref, f_ref, v_ref, g_ref, lb_ref, gna_ref, st_ref, do_ref, *rest):
        dq_ref, df_ref, dv_ref, dg_ref, dlb_ref, dgna_ref, dst = rest[len(after):]
        n, hb = pl.program_id(0), pl.program_id(1)

        @pl.when(n == 0)
        def _():
            dst[hb] = jnp.zeros((_HPS, HEAD, HEAD), F32)
            dlb_ref[hb] = jnp.zeros((_HPS, 1, HEAD), F32)

        @pl.when((n == 0) & (hb == 0))
        def _():
            dgna_ref[...] = jnp.zeros_like(dgna_ref)

        sls = [slice(hh * HEAD, (hh + 1) * HEAD) for hh in range(_HPS)]
        gna = gna_ref[...]
        ins = [(q_ref[:, sl], f_ref[:, sl], v_ref[:, sl], g_ref[:, sl], lb_ref[:, sl], gna, st_ref[hh])
               for hh, sl in enumerate(sls)]
        cts = [(do_ref[:, sl], dst[hb, hh]) for hh, sl in enumerate(sls)]
        grads = [jax.vjp(_hgrn_chunk, *ins[hh])[1](cts[hh]) for hh in range(_HPS)]
        dgna_sum = jnp.zeros((1, HEAD), F32)
        for hh, sl in enumerate(sls):
            dq, df, dv, dg, dlb, dgna, dst0 = grads[hh]
            dq_ref[:, sl] = dq.astype(dq_ref.dtype)
            df_ref[:, sl] = df.astype(df_ref.dtype)
            dv_ref[:, sl] = dv.astype(dv_ref.dtype)
            dg_ref[:, sl] = dg.astype(dg_ref.dtype)
            dlb_ref[hb, hh] += dlb
            dgna_sum = dgna_sum + dgna
            dst[hb, hh] = dst0
        dgna_ref[...] += dgna_sum

    rev = lambda n: n_chunks - 1 - n
    wide = _HPS * HEAD
    sec = lambda s: pl.BlockSpec((CHUNK, wide), lambda n, h, s=s: (rev(n), (8 // _HPS) * s + h))
    out = pl.BlockSpec((CHUNK, wide), lambda n, h: (rev(n), h))
    return pl.pallas_call(
        body, name="hgrn_bwd", grid=(n_chunks, heads // _HPS),
        in_specs=[sec(0), sec(1), sec(2), sec(3),
                  pl.BlockSpec((1, wide), lambda n, h: (0, h)), pl.BlockSpec((1, HEAD), lambda n, h: (0, 0)),
                  pl.BlockSpec((None, _HPS, HEAD, HEAD), lambda n, h: (rev(n), h, 0, 0)),
                  pl.BlockSpec((CHUNK, wide), lambda n, h: (rev(n), h))] + [pl.BlockSpec(memory_space=pl.ANY)] * len(after),
        out_specs=[out, out, out, out,
                   pl.BlockSpec((heads // _HPS, _HPS, 1, HEAD), lambda n, h: (0, 0, 0, 0)),
                   pl.BlockSpec((1, HEAD), lambda n, h: (0, 0))],
        out_shape=[jax.ShapeDtypeStruct((S, 1024), BF16)] * 4
        + [jax.ShapeDtypeStruct((heads // _HPS, _HPS, 1, HEAD), F32), jax.ShapeDtypeStruct((1, HEAD), F32)],
        scratch_shapes=[pltpu.VMEM((heads // _HPS, _HPS, HEAD, HEAD), F32)],
        compiler_params=_cparams(("arbitrary", "arbitrary")),
    )(proj, proj, proj, proj, lb, gna, states, d_o, *after)


def _l2n(t):
    return t * lax.rsqrt(jnp.sum(t * t, axis=-1, keepdims=True) + NORM_EPS)


def _unit_lower_inverse(low):
    c = low.shape[-1]
    inv = (_iota((c, c), 0) == _iota((c, c), 1)).astype(F32) - low
    p = low
    span = 2
    while span < c:
        p = _dot3(p, p)
        inv = inv + _dot3(inv, p)
        span *= 2
    return inv


def _gdn_chunk(qc, kc, v, z, tail, h0, arow, dtrow, gnb, st):
    nb, c = qc.shape[0], qc.shape[1]
    head = h0 + _iota((nb, c, HEAD), 0)
    lane = _iota((nb, c, HEAD), 2)
    la_all = arow * jax.nn.softplus(tail + dtrow)
    la = jnp.sum(jnp.where(lane == head, la_all[None], 0.0), axis=2, keepdims=True)
    beta = jnp.sum(jnp.where(lane == head + 8, jax.nn.sigmoid(tail)[None], 0.0), axis=2, keepdims=True)
    la_b = jnp.broadcast_to(la, (nb, c, HEAD))
    cum = _cumsum_rows(la_b)
    cmat = _cumsum_rows(jnp.broadcast_to(la, (nb, c, c)))
    cum_end = jnp.sum(la_b, axis=1, keepdims=True)
    decay = jnp.exp(jnp.where(_tril(c), cmat - jnp.swapaxes(cmat, 1, 2), _NEG))
    q = _l2n(qc) * (HEAD ** -0.5)
    k = _l2n(kc)
    k_beta = k * beta
    low = jnp.where(_tril(c, strict=True), _dot(k_beta, k, _NT) * decay, 0.0)
    inv = _unit_lower_inverse(low)
    u, w = _dot(inv, v * beta), _dot(inv, k_beta * jnp.exp(cum))
    intra = _dot(q, k, _NT) * decay
    v_new = u - _dot(w, st, _NT)
    o = _dot(q * jnp.exp(cum), st, _NT) + _dot(intra, v_new)
    st_new = st * jnp.exp(cum_end) + _dot(v_new, k * jnp.exp(cum_end - cum), _TN)
    return _head_norm_gate(o, gnb, z), st_new


def _gdn_specs(n_of):
    wide = _HPS * HEAD
    qkv = lambda s: pl.BlockSpec((CHUNK, wide), lambda n, h, s=s: (n_of(n), (8 // _HPS) * s + h))
    return [qkv(0), qkv(1), qkv(2),
            pl.BlockSpec((CHUNK, wide), lambda n, h: (n_of(n), 56 // _HPS + h)),
            pl.BlockSpec((CHUNK, HEAD), lambda n, h: (n_of(n), 64)),
            pl.BlockSpec((1, HEAD), lambda n, h: (0, 0)), pl.BlockSpec((1, HEAD), lambda n, h: (0, 0)),
            pl.BlockSpec((1, HEAD), lambda n, h: (0, 0))]


def _gdn_fwd(act, proj, arow, dtrow, gnb):
    S = act.shape[0]
    n_chunks, heads = S // CHUNK, 8

    def body(q_ref, k_ref, v_ref, z_ref, t_ref, a_ref, dt_ref, gnb_ref, o_ref, st_out, st):
        n, hb = pl.program_id(0), pl.program_id(1)

        @pl.when(n == 0)
        def _():
            st[hb] = jnp.zeros((_HPS, HEAD, HEAD), F32)

        sls = [slice(hh * HEAD, (hh + 1) * HEAD) for hh in range(_HPS)]
        heads_of = lambda ref: jnp.stack([ref[:, sl] for sl in sls])
        st0 = st[hb]
        o, st1 = _gdn_chunk(heads_of(q_ref), heads_of(k_ref), heads_of(v_ref), heads_of(z_ref), t_ref[...], hb * _HPS,
                            a_ref[...], dt_ref[...], gnb_ref[...], st0)
        st_out[...] = st0
        st[hb] = st1
        for hh, sl in enumerate(sls):
            o_ref[:, sl] = o[hh].astype(o_ref.dtype)

    return pl.pallas_call(
        body, name="gdn_fwd", grid=(n_chunks, heads // _HPS),
        in_specs=_gdn_specs(lambda n: n),
        out_specs=[pl.BlockSpec((CHUNK, _HPS * HEAD), lambda n, h: (n, h)),
                   pl.BlockSpec((None, _HPS, HEAD, HEAD), lambda n, h: (n, h, 0, 0))],
        out_shape=[jax.ShapeDtypeStruct((S, 1024), BF16),
                   jax.ShapeDtypeStruct((n_chunks, heads, HEAD, HEAD), F32)],
        scratch_shapes=[pltpu.VMEM((heads // _HPS, _HPS, HEAD, HEAD), F32)],
        compiler_params=_cparams(("arbitrary", "arbitrary")),
    )(act, act, act, proj, proj, arow, dtrow, gnb)


def _gdn_bwd(act, proj, arow, dtrow, gnb, states, d_o, after=()):
    S = act.shape[0]
    n_chunks, heads = S // CHUNK, 8

    def body(q_ref, k_ref, v_ref, z_ref, t_ref, a_ref, dt_ref, gnb_ref, st_ref, do_ref, *rest):
        dq_ref, dk_ref, dv_ref, dz_ref, dt_out, da_ref, ddt_ref, dgnb_ref, dst, dtail_acc = rest[len(after):]
        n, hb = pl.program_id(0), pl.program_id(1)
        n_hb = heads // _HPS

        @pl.when(n == 0)
        def _():
            dst[hb] = jnp.zeros((_HPS, HEAD, HEAD), F32)

        @pl.when((n == 0) & (hb == 0))
        def _():
            da_ref[...] = jnp.zeros_like(da_ref)
            ddt_ref[...] = jnp.zeros_like(ddt_ref)
            dgnb_ref[...] = jnp.zeros_like(dgnb_ref)

        sls = [slice(hh * HEAD, (hh + 1) * HEAD) for hh in range(_HPS)]
        heads_of = lambda ref: jnp.stack([ref[:, sl] for sl in sls])
        fn = lambda qc, kc, v, z, tail, arow, dtrow, gnb, st: _gdn_chunk(qc, kc, v, z, tail, hb * _HPS, arow, dtrow,
                                                                        gnb, st)
        _, vjp = jax.vjp(fn, heads_of(q_ref), heads_of(k_ref), heads_of(v_ref), heads_of(z_ref), t_ref[...],
                         a_ref[...], dt_ref[...], gnb_ref[...], st_ref[...])
        dq, dk, dv, dz, dtail_sum, da_sum, ddt_sum, dgnb_sum, dst0 = vjp((heads_of(do_ref), dst[hb]))
        dst[hb] = dst0
        for hh, sl in enumerate(sls):
            dq_ref[:, sl] = dq[hh]
            dk_ref[:, sl] = dk[hh]
            dv_ref[:, sl] = dv[hh]
            dz_ref[:, sl] = dz[hh].astype(dz_ref.dtype)

        @pl.when(hb == 0)
        def _():
            dtail_acc[...] = dtail_sum

        @pl.when(hb > 0)
        def _():
            dtail_acc[...] += dtail_sum

        @pl.when(hb == n_hb - 1)
        def _():
            dt_out[...] = dtail_acc[...].astype(dt_out.dtype)

        da_ref[...] += da_sum
        ddt_ref[...] += ddt_sum
        dgnb_ref[...] += dgnb_sum

    rev = lambda n: n_chunks - 1 - n
    out = pl.BlockSpec((CHUNK, _HPS * HEAD), lambda n, h: (rev(n), h))
    row = pl.BlockSpec((1, HEAD), lambda n, h: (0, 0))
    return pl.pallas_call(
        body, name="gdn_bwd", grid=(n_chunks, heads // _HPS),
        in_specs=_gdn_specs(rev)
        + [pl.BlockSpec((None, _HPS, HEAD, HEAD), lambda n, h: (rev(n), h, 0, 0)),
           pl.BlockSpec((CHUNK, _HPS * HEAD), lambda n, h: (rev(n), 8 // _HPS + h))]
        + [pl.BlockSpec(memory_space=pl.ANY)] * len(after),
        out_specs=[out, out, out, out, pl.BlockSpec((CHUNK, HEAD), lambda n, h: (rev(n), 0)), row, row, row],
        out_shape=[jax.ShapeDtypeStruct((S, 1024), F32)] * 3
        + [jax.ShapeDtypeStruct((S, 1024), BF16), jax.ShapeDtypeStruct((S, HEAD), BF16)]
        + [jax.ShapeDtypeStruct((1, HEAD), F32)] * 3,
        scratch_shapes=[pltpu.VMEM((heads // _HPS, _HPS, HEAD, HEAD), F32), pltpu.VMEM((CHUNK, HEAD), F32)],
        compiler_params=_cparams(("arbitrary", "arbitrary")),
    )(act, act, act, proj, proj, arow, dtrow, gnb, states, d_o, *after)


def _conv_silu(x, w):
    t = _iota((x.shape[0], 1), 0)
    tap = _iota(w.shape, 0)
    y = jnp.zeros_like(x)
    for r in range(4):
        w_r = jnp.sum(jnp.where(tap == 3 - r, w, 0.0), axis=0, keepdims=True)
        y = y + (x if r == 0 else jnp.where(t >= r, _roll_rows(x, r), 0.0)) * w_r
    return jax.nn.silu(y)


_CONV_COLS = 128


def _conv_fwd(proj, conv_w):
    S = proj.shape[0]
    nb = 3072 // _CONV_COLS
    off = 4096 // _CONV_COLS

    def body(x_ref, w_ref, o_ref):
        o_ref[...] = _conv_silu(x_ref[...], w_ref[...])

    return pl.pallas_call(
        body, name="conv_fwd", grid=(nb,),
        in_specs=[pl.BlockSpec((S, _CONV_COLS), lambda j: (0, off + j)), pl.BlockSpec((4, _CONV_COLS), lambda j: (0, j))],
        out_specs=pl.BlockSpec((S, _CONV_COLS), lambda j: (0, j)),
        out_shape=jax.ShapeDtypeStruct((S, 3072), F32),
        compiler_params=_cparams(("parallel",)),
    )(proj, conv_w)


def _conv_bwd(proj, conv_w, dq, dk, dv):
    S = proj.shape[0]
    nb = 3072 // _CONV_COLS
    off = 4096 // _CONV_COLS
    per = 1024 // _CONV_COLS

    def body(x_ref, w_ref, dq_ref, dk_ref, dv_ref, dx_ref, dw_ref):
        j = pl.program_id(0)
        _, vjp = jax.vjp(_conv_silu, x_ref[...], w_ref[...])
        d = jnp.where(j < per, dq_ref[...], jnp.where(j < 2 * per, dk_ref[...], dv_ref[...]))
        dx, dw = vjp(d)
        dx_ref[...] = dx.astype(dx_ref.dtype)
        dw_ref[...] = dw

    dsp = lambda s: pl.BlockSpec((S, _CONV_COLS), lambda j, s=s: (0, jnp.clip(j - s * per, 0, per - 1)))
    return pl.pallas_call(
        body, name="conv_bwd", grid=(nb,),
        in_specs=[pl.BlockSpec((S, _CONV_COLS), lambda j: (0, off + j)), pl.BlockSpec((4, _CONV_COLS), lambda j: (0, j)),
                  dsp(0), dsp(1), dsp(2)],
        out_specs=[pl.BlockSpec((S, _CONV_COLS), lambda j: (0, j)), pl.BlockSpec((4, _CONV_COLS), lambda j: (0, j))],
        out_shape=[jax.ShapeDtypeStruct((S, 3072), BF16), jax.ShapeDtypeStruct((4, 3072), F32)],
        compiler_params=_cparams(("parallel",)),
    )(proj, conv_w, dq, dk, dv)


_S5_T = 512
_S5_L = 512
_S5_NB = 16
_S5_U = 4
_S5_UB = 3


def _s5_tile(xr, xi, pw_r, pw_i, cr, ci, reverse):
    t8 = _iota((8, 1), 0)
    for sh in (1, 2, 4):
        row = (8 - sh) if reverse else (sh - 1)
        ar, ai = pw_r[row:row + 1, :], pw_i[row:row + 1, :]
        if reverse:
            keep, amt = t8 < 8 - sh, 8 - sh
        else:
            keep, amt = t8 >= sh, sh
        sr = jnp.where(keep, pltpu.roll(xr, amt, 0), 0.0)
        si = jnp.where(keep, pltpu.roll(xi, amt, 0), 0.0)
        xr, xi = xr + ar * sr - ai * si, xi + ar * si + ai * sr
    xr, xi = xr + pw_r * cr - pw_i * ci, xi + pw_r * ci + pw_i * cr
    return xr, xi


def _s5_fwd(u, bre, bim, cre, cim, pw_r, pw_i, dskip):
    S = u.shape[0]
    T = min(_S5_T, S)
    nt = S // T

    def body(u_ref, bre_ref, bim_ref, cre_ref, cim_ref, pr_ref, pi_ref, d_ref, y_ref, xr_ref, xi_ref,
             bu_r, bu_i, car_r, car_i):
        @pl.when(pl.program_id(1) == 0)
        def _():
            car_r[...] = jnp.zeros_like(car_r)
            car_i[...] = jnp.zeros_like(car_i)

        uu = u_ref[...]
        bu_r[...] = _dot(uu, bre_ref[...])
        bu_i[...] = _dot(uu, bim_ref[...])
        pw_r, pw_i = pr_ref[...], pi_ref[...]

        def tiles(i, carry):
            ins = []
            for k in range(_S5_U):
                r0 = pl.multiple_of((i * _S5_U + k) * 8, 8)
                ins.append((r0, bu_r[pl.ds(r0, 8), :], bu_i[pl.ds(r0, 8), :]))
            outs = []
            for r0, br, bi in ins:
                xr, xi = _s5_tile(br, bi, pw_r, pw_i, carry[0], carry[1], False)
                carry = (xr[7:8, :], xi[7:8, :])
                outs.append((r0, xr, xi))
            for r0, xr, xi in outs:
                xr_ref[pl.ds(r0, 8), :] = xr
                xi_ref[pl.ds(r0, 8), :] = xi
            return carry

        cr, ci = lax.fori_loop(0, T // (8 * _S5_U), tiles, (car_r[...], car_i[...]))
        car_r[...] = cr
        car_i[...] = ci
        y_ref[...] = _dot(xr_ref[...], cre_ref[...]) - _dot(xi_ref[...], cim_ref[...]) + d_ref[...] * uu

    blk3 = lambda a, b: pl.BlockSpec((None, a, b), lambda j, t: (j, 0, 0))
    return pl.pallas_call(
        body, name="s5_fwd", grid=(_S5_NB, nt),
        in_specs=[pl.BlockSpec((T, HEAD), lambda j, t: (t, j)),
                  blk3(HEAD, _S5_L), blk3(HEAD, _S5_L), blk3(_S5_L, HEAD), blk3(_S5_L, HEAD),
                  blk3(8, _S5_L), blk3(8, _S5_L), pl.BlockSpec((1, HEAD), lambda j, t: (0, j))],
        out_specs=[pl.BlockSpec((T, HEAD), lambda j, t: (t, j)),
                   pl.BlockSpec((T, _S5_L), lambda j, t: (t, j)), pl.BlockSpec((T, _S5_L), lambda j, t: (t, j))],
        out_shape=[jax.ShapeDtypeStruct((S, D_MODEL), F32),
                   jax.ShapeDtypeStruct((S, _S5_NB * _S5_L), F32), jax.ShapeDtypeStruct((S, _S5_NB * _S5_L), F32)],
        scratch_shapes=[pltpu.VMEM((T, _S5_L), F32), pltpu.VMEM((T, _S5_L), F32),
                        pltpu.VMEM((1, _S5_L), F32), pltpu.VMEM((1, _S5_L), F32)],
        compiler_params=_cparams(("parallel", "arbitrary")),
    )(u, bre, bim, cre, cim, pw_r, pw_i, dskip)


def _s5_bwd(dy, u, xre, xim, bre, bim, cre, cim, qw_r, qw_i, dskip):
    S = u.shape[0]
    T = min(_S5_T, S)
    nt = S // T
    nt8 = T // 8

    def body(dy_ref, u_ref, xr_ref, xi_ref, xpr_ref, xpi_ref, bre_ref, bim_ref, cre_ref, cim_ref, qr_ref, qi_ref,
             d_ref, du_ref, dbr_ref, dbi_ref, dcr_ref, dci_ref, dlr_ref, dli_ref, dd_ref,
             g_r, g_i, car_r, car_i):
        t = pl.program_id(1)

        @pl.when(t == 0)
        def _():
            car_r[...] = jnp.zeros_like(car_r)
            car_i[...] = jnp.zeros_like(car_i)
            for r in (dbr_ref, dbi_ref, dcr_ref, dci_ref, dlr_ref, dli_ref, dd_ref):
                r[...] = jnp.zeros_like(r)

        dyy, uu = dy_ref[...], u_ref[...]
        g_r[...] = _dot(dyy, cre_ref[...], _NT)
        g_i[...] = -_dot(dyy, cim_ref[...], _NT)
        qw_r, qw_i = qr_ref[...], qi_ref[...]
        t8 = _iota((8, 1), 0)
        first = t == nt - 1

        def load(r0, prev_r, prev_i):
            rows = pl.ds(r0, 8)
            return r0, g_r[rows, :], g_i[rows, :], xr_ref[rows, :], xi_ref[rows, :], prev_r, prev_i

        def run(loaded, carry, acc):
            done = []
            for r0, dr, di, xr, xi, prev_r, prev_i in loaded:
                gr, gi = _s5_tile(dr, di, qw_r, qw_i, carry[0], carry[1], True)
                carry = (gr[0:1, :], gi[0:1, :])
                xpr = jnp.where(t8 >= 1, pltpu.roll(xr, 1, 0), prev_r)
                xpi = jnp.where(t8 >= 1, pltpu.roll(xi, 1, 0), prev_i)
                acc = (acc[0] + gr * xpr + gi * xpi, acc[1] + gi * xpr - gr * xpi)
                done.append((r0, gr, gi))
            for r0, gr, gi in done:
                g_r[pl.ds(r0, 8), :] = gr
                g_i[pl.ds(r0, 8), :] = gi
            return carry, acc

        def step(ii, state):
            loaded = []
            for k in range(_S5_UB):
                idx = nt8 - 1 - (ii * _S5_UB + k)
                r0 = pl.multiple_of(idx * 8, 8)
                p0 = pl.multiple_of((idx - 1) * 8, 8)
                loaded.append(load(r0, xr_ref[pl.ds(p0, 8), :][7:8, :], xi_ref[pl.ds(p0, 8), :][7:8, :]))
            return run(loaded, *state)

        zero = jnp.zeros((8, _S5_L), F32)
        assert (nt8 - 1) % _S5_UB == 0
        state = lax.fori_loop(0, (nt8 - 1) // _S5_UB, step, ((car_r[...], car_i[...]), (zero, zero)))
        prev_r = jnp.where(first, 0.0, xpr_ref[...][7:8, :])
        prev_i = jnp.where(first, 0.0, xpi_ref[...][7:8, :])
        (cr, ci), (ar, ai) = run([load(0, prev_r, prev_i)], *state)
        car_r[...] = cr
        car_i[...] = ci
        dlr_ref[...] += ar
        dli_ref[...] += ai
        gr, gi = g_r[...], g_i[...]
        du_ref[...] = _dot(gr, bre_ref[...], _NT) + _dot(gi, bim_ref[...], _NT) + d_ref[...] * dyy
        dbr_ref[...] += _dot(uu, gr, _TN)
        dbi_ref[...] += _dot(uu, gi, _TN)
        dcr_ref[...] += _dot(xr_ref[...], dyy, _TN)
        dci_ref[...] -= _dot(xi_ref[...], dyy, _TN)
        dd_ref[...] += jnp.sum(dyy * uu, axis=0, keepdims=True)

    rev = lambda t: nt - 1 - t
    blk3 = lambda a, b: pl.BlockSpec((None, a, b), lambda j, t: (j, 0, 0))
    tl = pl.BlockSpec((T, HEAD), lambda j, t: (rev(t), j))
    xs = pl.BlockSpec((T, _S5_L), lambda j, t: (rev(t), j))
    xp = pl.BlockSpec((8, _S5_L), lambda j, t: (jnp.maximum(rev(t) * nt8 - 1, 0), j))
    return pl.pallas_call(
        body, name="s5_bwd", grid=(_S5_NB, nt),
        in_specs=[tl, tl, xs, xs, xp, xp, blk3(HEAD, _S5_L), blk3(HEAD, _S5_L), blk3(_S5_L, HEAD), blk3(_S5_L, HEAD),
                  blk3(8, _S5_L), blk3(8, _S5_L), pl.BlockSpec((1, HEAD), lambda j, t: (0, j))],
        out_specs=[tl, blk3(HEAD, _S5_L), blk3(HEAD, _S5_L), blk3(_S5_L, HEAD), blk3(_S5_L, HEAD),
                   blk3(8, _S5_L), blk3(8, _S5_L), pl.BlockSpec((1, HEAD), lambda j, t: (0, j))],
        out_shape=[jax.ShapeDtypeStruct((S, D_MODEL), F32),
                   jax.ShapeDtypeStruct((_S5_NB, HEAD, _S5_L), F32), jax.ShapeDtypeStruct((_S5_NB, HEAD, _S5_L), F32),
                   jax.ShapeDtypeStruct((_S5_NB, _S5_L, HEAD), F32), jax.ShapeDtypeStruct((_S5_NB, _S5_L, HEAD), F32),
                   jax.ShapeDtypeStruct((_S5_NB, 8, _S5_L), F32), jax.ShapeDtypeStruct((_S5_NB, 8, _S5_L), F32),
                   jax.ShapeDtypeStruct((1, D_MODEL), F32)],
        scratch_shapes=[pltpu.VMEM((T, _S5_L), F32), pltpu.VMEM((T, _S5_L), F32),
                        pltpu.VMEM((1, _S5_L), F32), pltpu.VMEM((1, _S5_L), F32)],
        compiler_params=_cparams(("parallel", "arbitrary")),
    )(dy, u, xre, xim, xre, xim, bre, bim, cre, cim, qw_r, qw_i, dskip)


def _s5_params(a_re, a_im, log_dt, b_re, b_im, c_re, c_im):
    step = jnp.exp(log_dt)[:, None]
    mag = jnp.exp(a_re * step)
    lr, li = mag * jnp.cos(a_im * step), mag * jnp.sin(a_im * step)
    den = a_re * a_re + a_im * a_im
    nr, ni = lr - 1.0, li
    kr, ki = (nr * a_re + ni * a_im) / den, (ni * a_re - nr * a_im) / den
    bbr = kr[..., None] * b_re - ki[..., None] * b_im
    bbi = kr[..., None] * b_im + ki[..., None] * b_re
    eye = jnp.eye(8, dtype=F32)

    def blk_b(bb):
        t = bb.reshape(_S5_NB, 8, 64, 16).transpose(0, 1, 3, 2)
        return (t[:, :, :, None, :] * eye[None, :, None, :, None]).reshape(_S5_NB, HEAD, _S5_L)

    def blk_c(cc):
        t = cc.reshape(_S5_NB, 8, 16, 64).transpose(0, 1, 3, 2)
        return (t[:, :, :, None, :] * eye[None, :, None, :, None]).reshape(_S5_NB, _S5_L, HEAD)

    return (blk_b(bbr), blk_b(bbi), blk_c(c_re), blk_c(c_im),
            lr.reshape(_S5_NB, _S5_L), li.reshape(_S5_NB, _S5_L))


def _s5_powers(lr, li):
    pr, pi = [lr], [li]
    for _ in range(7):
        pr, pi = pr + [pr[-1] * lr - pi[-1] * li], pi + [pr[-1] * li + pi[-1] * lr]
    return jnp.stack(pr, axis=1), jnp.stack(pi, axis=1)


_MESH = pl.DeviceIdType.MESH
_ANY = pl.BlockSpec(memory_space=pl.ANY)


def _place():
    x, y, c = lax.axis_index("x"), lax.axis_index("y"), lax.axis_index("c")
    return x, y, c, [(1 - x, y), (x, 1 - y), (1 - x, 1 - y)]


def _comm_call(body, arrs, out_shapes, n_remote, name):
    n = len(arrs)
    return pl.pallas_call(
        body, name=name,
        in_specs=[_ANY] * n, out_specs=[_ANY] * n, out_shape=out_shapes,
        scratch_shapes=[pltpu.SemaphoreType.DMA((n * n_remote,)), pltpu.SemaphoreType.DMA((n * n_remote,)),
                        pltpu.SemaphoreType.DMA((n,))],
    )(*arrs)


def _half(ref, slot, h):
    if len(ref.shape) == 4:
        return ref.at[slot, h]
    cols = ref.shape[2] // 2
    return ref.at[slot, :, pl.ds(pl.multiple_of(h * cols, 128), cols)]


def _own_slot(shard, me_idx, name, rows=256, cols=512):
    def body(me_ref, in_ref, o_ref):
        o_ref[...] = in_ref[...]

    if shard.ndim == 2:
        R, C = shard.shape
        return pl.pallas_call(
            body, name=name,
            grid_spec=pltpu.PrefetchScalarGridSpec(
                num_scalar_prefetch=1, grid=(C // cols,),
                in_specs=[pl.BlockSpec((R, cols), lambda j, me: (0, j))],
                out_specs=pl.BlockSpec((None, R, cols), lambda j, me: (me[0], 0, j))),
            out_shape=jax.ShapeDtypeStruct((4,) + shard.shape, shard.dtype),
            compiler_params=_cparams(("parallel",)),
        )(me_idx, shard)
    _, half, C = shard.shape
    rows = min(rows, half)
    assert half % rows == 0
    return pl.pallas_call(
        body, name=name,
        grid_spec=pltpu.PrefetchScalarGridSpec(
            num_scalar_prefetch=1, grid=(2, half // rows),
            in_specs=[pl.BlockSpec((None, rows, C), lambda h, i, me: (h, i, 0))],
            out_specs=pl.BlockSpec((None, None, rows, C), lambda h, i, me: (me[0], h, i, 0))),
        out_shape=jax.ShapeDtypeStruct((4,) + shard.shape, shard.dtype),
        compiler_params=_cparams(("parallel", "parallel")),
    )(me_idx, shard)


def _own_rows(w, layer, buf, me_idx, row_off, after, name, rows=256):
    _, r, C = w.shape
    carried = [] if isinstance(buf, tuple) else [buf]
    shape = buf if isinstance(buf, tuple) else buf.shape
    assert r % rows == 0 and row_off % rows == 0

    def body(me_ref, in_ref, *rest):
        rest[-1][...] = in_ref[...].astype(BF16)

    return pl.pallas_call(
        body, name=name,
        grid_spec=pltpu.PrefetchScalarGridSpec(
            num_scalar_prefetch=1, grid=(r // rows,),
            in_specs=[pl.BlockSpec((None, rows, C), lambda i, me: (layer, i, 0))] + [_ANY] * (len(carried) + len(after)),
            out_specs=pl.BlockSpec((None, rows, C), lambda i, me: (me[0], row_off // rows + i, 0))),
        out_shape=jax.ShapeDtypeStruct(shape, BF16),
        input_output_aliases={2: 0} if carried else {},
        compiler_params=_cparams(("parallel",)),
    )(me_idx, w, *carried, *after)


def _pair_split(arrs, name):
    n = len(arrs)

    def body(*refs):
        ins, outs, (ssem, rsem, _) = refs[:n], refs[n:2 * n], refs[2 * n:]
        x, y, c, _ = _place()
        copies = []
        for a in range(n):
            for s in range(4):
                cp = pltpu.make_async_remote_copy(src_ref=_half(ins[a], s, 1 - c), dst_ref=outs[a].at[s],
                                                  send_sem=ssem.at[4 * a + s], recv_sem=rsem.at[4 * a + s],
                                                  device_id=(x, y, 1 - c), device_id_type=_MESH)
                cp.start()
                copies.append(cp)
        for cp in copies:
            cp.wait()

    shapes = [jax.ShapeDtypeStruct((4,) + (a.shape[2:] if a.ndim == 4 else (a.shape[1], a.shape[2] // 2)), a.dtype)
              for a in arrs]
    return _comm_call(body, arrs, shapes, 4, name)


def _pair_swap(arrs, name):
    n = len(arrs)

    def body(*refs):
        ins, outs, (ssem, rsem, _) = refs[:n], refs[n:2 * n], refs[2 * n:]
        x, y, c, _ = _place()
        copies = []
        for a in range(n):
            cp = pltpu.make_async_remote_copy(src_ref=ins[a], dst_ref=outs[a], send_sem=ssem.at[a], recv_sem=rsem.at[a],
                                              device_id=(x, y, 1 - c), device_id_type=_MESH)
            cp.start()
            copies.append(cp)
        for cp in copies:
            cp.wait()

    return _comm_call(body, arrs, [jax.ShapeDtypeStruct(a.shape, a.dtype) for a in arrs], 1, name)


def _pair_sum(full, recv, c_idx, name, rows=128):
    def add(c_ref, a_ref, b_ref, o_ref):
        o_ref[...] = (a_ref[...].astype(F32) + b_ref[...].astype(F32)).astype(o_ref.dtype)

    if full.ndim == 3:
        _, R, cols = recv.shape
        blk = lambda col: pl.BlockSpec((None, R, cols), lambda s, c: (s, 0, col(c)))
        return pl.pallas_call(
            add, name=name,
            grid_spec=pltpu.PrefetchScalarGridSpec(
                num_scalar_prefetch=1, grid=(4,),
                in_specs=[blk(lambda c: c[0]), blk(lambda c: 0)], out_specs=blk(lambda c: 0)),
            out_shape=jax.ShapeDtypeStruct(recv.shape, recv.dtype),
            compiler_params=_cparams(("parallel",)),
        )(c_idx, full, recv)
    _, _, half, C = full.shape
    rows = min(rows, half)
    nb = half // rows
    assert half % rows == 0

    def body(c_ref, a_ref, b_ref, o_ref):
        o_ref[...] = (a_ref[...].astype(F32) + b_ref[...].astype(F32)).astype(o_ref.dtype)

    return pl.pallas_call(
        body, name=name,
        grid_spec=pltpu.PrefetchScalarGridSpec(
            num_scalar_prefetch=1, grid=(4, nb),
            in_specs=[pl.BlockSpec((None, None, rows, C), lambda s, i, c: (s, c[0], i, 0)),
                      pl.BlockSpec((None, rows, C), lambda s, i, c: (s, i, 0))],
            out_specs=pl.BlockSpec((None, rows, C), lambda s, i, c: (s, i, 0))),
        out_shape=jax.ShapeDtypeStruct(recv.shape, recv.dtype),
        compiler_params=_cparams(("parallel", "parallel")),
    )(c_idx, full, recv)


_HBM = pl.BlockSpec(memory_space=pltpu.HBM)
_SEM = pl.BlockSpec(memory_space=pltpu.SEMAPHORE)
_SPLIT = dict(has_side_effects=pltpu.SideEffectType.DATAFLOW_SIDE_EFFECTING)


def _hbm(t):
    return pltpu.with_memory_space_constraint(t, pltpu.HBM)


def _ag_copy(buf_ref, ssem, rsem, j, chip, c, me):
    px, py = chip
    mine = _half(buf_ref, me, c)
    return pltpu.make_async_remote_copy(src_ref=mine, dst_ref=mine, send_sem=ssem.at[j], recv_sem=rsem.at[j],
                                        device_id=(px, py, c), device_id_type=_MESH)


def _ag_start(bufs, after, name):
    n = len(bufs)

    def body(*refs):
        buf_refs, (ssem, rsem), token = refs[:n], refs[n + 1:n + 3], refs[-1]
        x, y, c, chips = _place()
        for a in range(n):
            for j, chip in enumerate(chips):
                _ag_copy(buf_refs[a], ssem, rsem, 3 * a + j, chip, c, 2 * x + y).start()
        token[...] = jnp.zeros_like(token)

    out = pl.pallas_call(
        body, name=name,
        out_shape=(pltpu.SemaphoreType.DMA((3 * n,)), pltpu.SemaphoreType.DMA((3 * n,)),
                   *[pltpu.HBM(b.shape, b.dtype) for b in bufs], jax.ShapeDtypeStruct((8, HEAD), F32)),
        in_specs=(*[_HBM] * n, _ANY), out_specs=(_SEM, _SEM, *[_HBM] * n, pl.BlockSpec(memory_space=pltpu.VMEM)),
        input_output_aliases={a: 2 + a for a in range(n)}, compiler_params=pltpu.CompilerParams(**_SPLIT),
    )(*[_hbm(b) for b in bufs], after)
    return out[0], out[1], list(out[2:2 + n]), out[-1]


def _ag_wait(ssem, rsem, bufs, after, name):
    n = len(bufs)

    def body(*refs):
        buf_refs, ssem, rsem = refs[:n], refs[n], refs[n + 1]
        x, y, c, chips = _place()
        for a in range(n):
            for j, chip in enumerate(chips):
                cp = _ag_copy(buf_refs[a], ssem, rsem, 3 * a + j, chip, c, 2 * x + y)
                cp.wait_send()
                cp.wait_recv()

    after = after if isinstance(after, (tuple, list)) else (after,)
    return list(pl.pallas_call(
        body, name=name, out_shape=tuple(pltpu.HBM(b.shape, b.dtype) for b in bufs),
        in_specs=(*[_HBM] * n, _SEM, _SEM, *[_ANY] * len(after)), out_specs=tuple([_HBM] * n),
        input_output_aliases={a: a for a in range(n)}, compiler_params=pltpu.CompilerParams(**_SPLIT),
    )(*bufs, ssem, rsem, *after))


def _pair_forward(bufs, name):
    n = len(bufs)

    def body(*refs):
        outs, (ssem, rsem) = refs[n:2 * n], refs[2 * n:]
        x, y, c, chips = _place()
        copies = []
        for a in range(n):
            for j, (px, py) in enumerate(chips):
                landed = _half(outs[a], 2 * px + py, c)
                cp = pltpu.make_async_remote_copy(src_ref=landed, dst_ref=landed, send_sem=ssem.at[3 * a + j],
                                                  recv_sem=rsem.at[3 * a + j], device_id=(x, y, 1 - c),
                                                  device_id_type=_MESH)
                cp.start()
                copies.append(cp)
        for cp in copies:
            cp.wait()

    return pl.pallas_call(
        body, name=name, in_specs=[_ANY] * n, out_specs=[_ANY] * n,
        out_shape=[jax.ShapeDtypeStruct(b.shape, b.dtype) for b in bufs],
        input_output_aliases={a: a for a in range(n)},
        scratch_shapes=[pltpu.SemaphoreType.DMA((3 * n,)), pltpu.SemaphoreType.DMA((3 * n,))],
    )(*bufs)


def _split_copy(part_ref, land_ref, ssem, rsem, s, x, y, c):
    return pltpu.make_async_remote_copy(src_ref=_half(part_ref, s, 1 - c), dst_ref=land_ref.at[s], send_sem=ssem.at[s],
                                        recv_sem=rsem.at[s], device_id=(x, y, 1 - c), device_id_type=_MESH)


def _pair_split_start(part, after, name):
    land = jax.ShapeDtypeStruct((4,) + part.shape[2:], part.dtype)

    def body(part_ref, land_ref, after_ref, ssem, rsem, part_thru, land_thru, token):
        x, y, c, _ = _place()
        for s in range(4):
            _split_copy(part_ref, land_ref, ssem, rsem, s, x, y, c).start()
        token[...] = jnp.zeros_like(token)

    return pl.pallas_call(
        body, name=name,
        out_shape=(pltpu.SemaphoreType.DMA((4,)), pltpu.SemaphoreType.DMA((4,)), pltpu.HBM(part.shape, part.dtype),
                   pltpu.HBM(land.shape, land.dtype), jax.ShapeDtypeStruct((8, HEAD), F32)),
        in_specs=(_HBM, _HBM, _ANY), out_specs=(_SEM, _SEM, _HBM, _HBM, pl.BlockSpec(memory_space=pltpu.VMEM)),
        input_output_aliases={0: 2, 1: 3}, compiler_params=pltpu.CompilerParams(**_SPLIT),
    )(_hbm(part), _hbm(lax.empty(land.shape, land.dtype)), after)


def _pair_split_wait(ssem, rsem, part, land, after, name):
    def body(part_ref, land_ref, ssem, rsem, after_ref, part_out, land_out):
        x, y, c, _ = _place()
        for s in range(4):
            cp = _split_copy(part_ref, land_ref, ssem, rsem, s, x, y, c)
            cp.wait_send()
            cp.wait_recv()

    return pl.pallas_call(
        body, name=name, out_shape=(pltpu.HBM(part.shape, part.dtype), pltpu.HBM(land.shape, land.dtype)),
        in_specs=(_HBM, _HBM, _SEM, _SEM, _ANY), out_specs=(_HBM, _HBM),
        input_output_aliases={0: 0, 1: 1}, compiler_params=pltpu.CompilerParams(**_SPLIT),
    )(part, land, ssem, rsem, after)


def _rs_copy(src_ref, land_ref, ssem, rsem, j, chip, c):
    px, py = chip
    return pltpu.make_async_remote_copy(src_ref=src_ref.at[2 * px + py], dst_ref=land_ref.at[j], send_sem=ssem.at[j],
                                        recv_sem=rsem.at[j], device_id=(px, py, c), device_id_type=_MESH)


def _rs_start(part, after, name):
    def body(part_ref, land_ref, after_ref, ssem, rsem, part_thru, land_thru, token):
        x, y, c, chips = _place()
        for j, chip in enumerate(chips):
            _rs_copy(part_ref, land_ref, ssem, rsem, j, chip, c).start()
        token[...] = jnp.zeros_like(token)

    land = jax.ShapeDtypeStruct((3,) + part.shape[1:], part.dtype)
    return pl.pallas_call(
        body, name=name,
        out_shape=(pltpu.SemaphoreType.DMA((3,)), pltpu.SemaphoreType.DMA((3,)), pltpu.HBM(part.shape, part.dtype),
                   pltpu.HBM(land.shape, land.dtype), jax.ShapeDtypeStruct((8, HEAD), F32)),
        in_specs=(_HBM, _HBM, _ANY), out_specs=(_SEM, _SEM, _HBM, _HBM, pl.BlockSpec(memory_space=pltpu.VMEM)),
        input_output_aliases={0: 2, 1: 3}, compiler_params=pltpu.CompilerParams(**_SPLIT),
    )(_hbm(part), _hbm(lax.empty(land.shape, land.dtype)), after)


def _rs_wait(ssem, rsem, part, land, after, name):
    def body(part_ref, land_ref, ssem, rsem, after_ref, part_out, land_out):
        x, y, c, chips = _place()
        for j, chip in enumerate(chips):
            cp = _rs_copy(part_ref, land_ref, ssem, rsem, j, chip, c)
            cp.wait_send()
            cp.wait_recv()

    return pl.pallas_call(
        body, name=name, out_shape=(pltpu.HBM(part.shape, part.dtype), pltpu.HBM(land.shape, land.dtype)),
        in_specs=(_HBM, _HBM, _SEM, _SEM, _ANY), out_specs=(_HBM, _HBM),
        input_output_aliases={0: 0, 1: 1}, compiler_params=pltpu.CompilerParams(**_SPLIT),
    )(part, land, ssem, rsem, after)


def _sum_own_recv(part, land, me_idx, name, rows=128):
    _, half, C = part.shape

    def body(me_ref, own_ref, land_ref, o_ref):
        acc = own_ref[...].astype(F32)
        for s in range(3):
            acc = acc + land_ref[s].astype(F32)
        o_ref[...] = acc

    if half % 8:
        cols = 256
        return pl.pallas_call(
            body, name=name,
            grid_spec=pltpu.PrefetchScalarGridSpec(
                num_scalar_prefetch=1, grid=(C // cols,),
                in_specs=[pl.BlockSpec((None, half, cols), lambda j, me: (me[0], 0, j)),
                          pl.BlockSpec((3, half, cols), lambda j, me: (0, 0, j))],
                out_specs=pl.BlockSpec((half, cols), lambda j, me: (0, j))),
            out_shape=jax.ShapeDtypeStruct((half, C), F32),
            compiler_params=_cparams(("parallel",)),
        )(me_idx, part, land)
    rows = min(rows, half)
    assert half % rows == 0
    return pl.pallas_call(
        body, name=name,
        grid_spec=pltpu.PrefetchScalarGridSpec(
            num_scalar_prefetch=1, grid=(half // rows,),
            in_specs=[pl.BlockSpec((None, rows, C), lambda i, me: (me[0], i, 0)),
                      pl.BlockSpec((3, rows, C), lambda i, me: (0, i, 0))],
            out_specs=pl.BlockSpec((rows, C), lambda i, me: (i, 0))),
        out_shape=jax.ShapeDtypeStruct((half, C), F32),
        compiler_params=_cparams(("parallel",)),
    )(me_idx, part, land)


def _sum_slots(arr, name, rows=128):
    k, R, C = arr.shape
    rows = min(rows, R)
    assert R % rows == 0

    def body(in_ref, o_ref):
        acc = in_ref[0].astype(F32)
        for s in range(1, k):
            acc = acc + in_ref[s].astype(F32)
        o_ref[...] = acc

    return pl.pallas_call(
        body, name=name, grid=(R // rows,),
        in_specs=[pl.BlockSpec((k, rows, C), lambda i: (0, i, 0))],
        out_specs=pl.BlockSpec((rows, C), lambda i: (i, 0)),
        out_shape=jax.ShapeDtypeStruct((R, C), F32),
        compiler_params=_cparams(("parallel",)),
    )(arr)


ADAM_LR, ADAM_B1, ADAM_B2, ADAM_EPS, ADAM_WD, ADAM_STEP = 0.001, 0.9, 0.999, 1e-08, 0.01, 10


def _adam_math(w, g, m, v):
    m = ADAM_B1 * m + (1.0 - ADAM_B1) * g
    v = ADAM_B2 * v + (1.0 - ADAM_B2) * jnp.square(g)
    m_hat = m / (1.0 - ADAM_B1 ** ADAM_STEP)
    v_hat = v / (1.0 - ADAM_B2 ** ADAM_STEP)
    delta = -ADAM_LR * (m_hat / (jnp.sqrt(v_hat) + ADAM_EPS) + ADAM_WD * w)
    return delta, m, v


def _adamw(w, m, v, layer, mine, other, c_idx, name, g_off=0, prev=None, after=(), rows=128):
    _, R, C = w.shape
    half = mine.shape[0]
    rows = min(rows, R)
    assert R % rows == 0 and g_off % rows == 0 and half % rows == 0
    nbh, b0 = half // rows, g_off // rows

    def body(c_ref, w_ref, m_ref, v_ref, mine_ref, other_ref, *rest):
        go, do, mo, vo = rest[-4:]
        in_my_half = (b0 + pl.program_id(0)) // nbh == c_ref[0]
        g = jnp.where(in_my_half, mine_ref[...], other_ref[...])
        delta, m1, v1 = _adam_math(w_ref[...], g, m_ref[...], v_ref[...])
        go[...] = g
        do[...] = delta
        mo[...] = m1
        vo[...] = v1

    blk = pl.BlockSpec((None, rows, C), lambda i, c: (layer, i, 0))

    def gblk(of_mine):
        return pl.BlockSpec((rows, C), lambda i, c: (
            jnp.where(((b0 + i) // nbh == c[0]) == of_mine, (b0 + i) % nbh, 0), 0))

    carried = list(prev) if prev is not None else []
    return pl.pallas_call(
        body, name=name,
        grid_spec=pltpu.PrefetchScalarGridSpec(
            num_scalar_prefetch=1, grid=(R // rows,),
            in_specs=[blk] * 3 + [gblk(True), gblk(False)] + [_ANY] * (len(carried) + len(after)),
            out_specs=[blk] * 4),
        out_shape=[jax.ShapeDtypeStruct(w.shape, F32)] * 4,
        input_output_aliases={6 + k: k for k in range(len(carried))},
        compiler_params=_cparams(("parallel",)),
    )(c_idx, w, m, v, mine, other, *carried, *after)


def _adamw_t(w, m, v, mine, other, c_idx, name, rows=513, cols=128):
    R, _, C = w.shape
    nbh = (C // 2) // cols
    assert R % rows == 0 and (C // 2) % cols == 0

    def body(c_ref, w_ref, m_ref, v_ref, mine_ref, other_ref, go, do, mo, vo):
        in_my_half = pl.program_id(1) // nbh == c_ref[0]
        g = jnp.where(in_my_half, mine_ref[...], other_ref[...])
        delta, m1, v1 = _adam_math(w_ref[...], g, m_ref[...], v_ref[...])
        go[...] = g
        do[...] = delta
        mo[...] = m1
        vo[...] = v1

    blk = pl.BlockSpec((rows, 1, cols), lambda i, j, c: (i, 0, j))
    gblk = pl.BlockSpec((rows, 1, cols), lambda i, j, c: (i, 0, j % nbh))
    return pl.pallas_call(
        body, name=name,
        grid_spec=pltpu.PrefetchScalarGridSpec(
            num_scalar_prefetch=1, grid=(R // rows, C // cols), in_specs=[blk] * 3 + [gblk] * 2,
            out_specs=[blk] * 4),
        out_shape=[jax.ShapeDtypeStruct(w.shape, F32)] * 4,
        compiler_params=_cparams(("parallel", "parallel")),
    )(c_idx, w, m, v, mine, other)


_WEIGHTS = ['norm_mix', 'norm_mlp', 'norm_ple', 'w_in_e', 'w_out_e', 'hgrn_lb', 'g_norm_a', 'conv_w', 'a_log',
            'dt_bias', 'g_norm_b', 's5_a_re', 's5_a_im', 's5_b_re', 's5_b_im', 's5_c_re', 's5_c_im', 's5_d',
            's5_log_dt', 'w_glu', 'b_glu', 'w_out_o', 'w_up', 'w_down', 'w_ple_gate', 'w_ple_proj', 'final_norm']
_INPUTS = ['x', 'p'] + _WEIGHTS + ['loss_target'] + ['m_' + n for n in _WEIGHTS] + ['v_' + n for n in _WEIGHTS]
_FAMILY = [('w_up', 0, 'col', 2048), ('w_down', 0, 'row', 2048), ('w_ple_gate', 0, 'row', 512), ('w_out_e', 0, 'row', 512),
           ('w_up', 1, 'col', 2048), ('w_down', 1, 'row', 2048), ('w_ple_gate', 1, 'row', 512), ('w_glu', 0, 'row', 512),
           ('w_out_o', 0, 'row', 512)]
_PACK = {('w_up', 0): 0, ('w_down', 0): 0, ('w_ple_gate', 0): 0, ('w_out_e', 0): 0,
         ('w_up', 1): 1, ('w_down', 1): 1, ('w_ple_gate', 1): 1, ('w_glu', 0): 1, ('w_out_o', 0): 1}
_IN_PAD = 8320
_IN_BLK = 1664


def _rms_bwd_fn(x, d_hn, d_res, g):
    _, vjp = jax.vjp(_rms, x, g)
    dx, dg = vjp(d_hn.astype(F32))
    return dx + d_res, dg


def _rms_bwd_both(x, d_hn, d_res, g):
    dx, dg = _rms_bwd_fn(x, d_hn, d_res, g)
    return dx, dx, dg


def _add_res(acc, h):
    return (acc + h,)


def _pack_rows(parts, lanes=128, mult=256):
    flat = jnp.concatenate([q.reshape(-1).astype(F32) for q in parts])
    n = flat.shape[0]
    rows = -(-n // (lanes * mult)) * mult
    return jnp.pad(flat, (0, rows * lanes - n)).reshape(rows, lanes)


def _unpack_rows(buf, shapes):
    flat, out, off = buf.reshape(-1), [], 0
    for s in shapes:
        n = math.prod(s)
        out.append(flat[off:off + n].reshape(s))
        off += n
    return out


def _step(a):
    S = a['x'].shape[1]
    x, tgt = a['x'][0], a['loss_target'][0]
    xi, yi = lax.axis_index("x"), lax.axis_index("y")
    me = 2 * xi + yi
    row = lambda t: t.reshape(1, -1)

    me_idx = me.astype(jnp.int32).reshape(1)
    c_idx = lax.axis_index("c").astype(jnp.int32).reshape(1)
    small_sh = jnp.concatenate([a['conv_w'][0].reshape(-1), a['s5_d'][0], a['b_glu'][0]]).reshape(16, 256)
    rows_first = lambda t: jnp.transpose(t, (2, 0, 1))
    shards = [None, None, rows_first(a['w_in_e']).reshape(2052, D_MODEL).astype(BF16),
              a['w_ple_proj'].reshape(512, 512).astype(BF16), small_sh]
    own = lambda i: _own_slot(shards[i] if i == 2 else shards[i].reshape(2, shards[i].shape[0] // 2, shards[i].shape[1]),
                              me_idx, name=f"own_slot{i}")
    whole = lambda g, t: g.reshape((4,) + t.shape)
    first = _ag_start([own(2), own(3), own(4)], me_idx, name="gather_first_start")
    big = {nm for nm, _, _, _ in _FAMILY} | {'w_in_e', 'w_ple_proj'}
    held = ['x'] + [pre + n for n in _WEIGHTS if n not in big for pre in ('', 'm_', 'v_')]
    _, late = lax.optimization_barrier((first[3], {n: a[n] for n in held}))
    a = {**a, **late}
    x = a['x'][0]
    views, pack_of, rt = {}, {}, [0, 0]
    for nm, l, kind, r in _FAMILY:
        views[nm, l], pack_of[nm, l] = (kind, rt[_PACK[nm, l]], r), _PACK[nm, l]
        rt[_PACK[nm, l]] += r
    bufs = [(4, rt[0], PACK_COLS), (4, rt[1], PACK_COLS)]
    for nm, l, kind, r in _FAMILY:
        k = pack_of[nm, l]
        bufs[k] = _own_rows(a[nm], l, bufs[k], me_idx, views[nm, l][1], (first[3],), name=f"own_{nm}{l}")
    shards[0], shards[1] = [jax.ShapeDtypeStruct((rt[k], PACK_COLS), BF16) for k in (0, 1)]
    slots = [bufs[k].reshape(4, 2, rt[k] // 2, PACK_COLS) for k in (0, 1)]
    hn0 = _rowwise(_rms, [x], [row(a['norm_mix'][0])], [(D_MODEL, BF16)], name="rms_mix0")
    s_names = [n for n in _WEIGHTS if n not in big]
    w_pack, m_pack, v_pack = [_pack_rows([a[pre + n] for n in s_names]) for pre in ('', 'm_', 'v_')]
    landed = _pair_forward(_ag_wait(first[0], first[1], first[2], (hn0, bufs[0], bufs[1], w_pack, m_pack, v_pack),
                                    name="gather_first_wait"), name="gather_first_pass")
    g_in, g_pp, g_small = [whole(g, t) for g, t in zip(landed, shards[2:])]
    gw = {}
    W = lambda nm, l: dict(b=gw[pack_of[nm, l]], b_view=views[nm, l])
    sems0 = _ag_start([slots[0]], g_in, name="gather0_start")
    win_t = jnp.pad(g_in.reshape(8208, D_MODEL), ((0, _IN_PAD - 8208), (0, 0)))
    w_pp = g_pp.reshape(4, 2, 256, 512).transpose(1, 2, 0, 3).reshape(2, 256, D_MODEL)
    g_small = g_small.reshape(4, 4096)
    conv_full = g_small[:, :3072].reshape(4, 4, 768).transpose(1, 0, 2).reshape(4, 3072)
    s5_d_full = g_small[:, 3072:3584].reshape(1, D_MODEL)
    b_glu_full = g_small[:, 3584:].reshape(1, D_MODEL)

    lb, lb_vjp = jax.vjp(lambda hl: jax.nn.softmax(hl, axis=0)[0:1], a['hgrn_lb'])
    pad_h = lambda t: jnp.pad(t, ((0, 0), (0, HEAD - t.shape[1])))
    (arow, dtrow), gdn_vjp = jax.vjp(lambda al, dt: (pad_h(-jnp.exp(al)), pad_h(dt)), a['a_log'], a['dt_bias'])
    s5p, s5_vjp = jax.vjp(_s5_params, a['s5_a_re'][0], a['s5_a_im'][0], a['s5_log_dt'][0], a['s5_b_re'][0],
                          a['s5_b_im'][0], a['s5_c_re'][0], a['s5_c_im'][0])
    bre, bim, cre, cim = [t.astype(BF16) for t in s5p[:4]]
    pw_r, pw_i = _s5_powers(s5p[4], s5p[5])
    gna, gnb = a['g_norm_a'], a['g_norm_b']

    def norm_cast(h, g, dt, name):
        return _rowwise(_rms, [h], [row(g)], [(D_MODEL, dt)], name=name)

    def mlp_ple_fwd(h, l):
        hn = norm_cast(h, a['norm_mlp'][l], BF16, f"rms_mlp{l}")
        up, act = _mm(hn, **W('w_up', l), out_dtypes=(BF16, BF16), name=f"up{l}",
                      epilogue=lambda acc: (acc, jnp.square(jnp.maximum(acc, 0.0))))
        h2 = _mm(act, **W('w_down', l), epilogue=_add_res, extras=(h,), name=f"down{l}")
        hnp = norm_cast(h2, a['norm_ple'][l], BF16, f"rms_ple{l}")
        pp = _mm(a['p'][l, 0], w_pp[l], name=f"ple_proj{l}")
        h3, gpre = _mm(hnp, **W('w_ple_gate', l), extras=(h2, pp), out_dtypes=(F32, F32), name=f"ple_gate{l}",
                       bm=512, epilogue=lambda acc, h2, pp: (h2 + jax.nn.sigmoid(acc) * pp, acc))
        return h3, (h, hn, up, act, h2, hnp, pp, gpre)

    proj = _mm(hn0, win_t, dims="nt", bn=_IN_BLK, name="in_proj", after=(sems0[3],))
    o_a, st_a = _hgrn_fwd(proj, lb, gna)
    act_b = _conv_fwd(proj, conv_full)
    o_b, st_b = _gdn_fwd(act_b, proj, arow, dtrow, gnb)
    merged = jnp.concatenate([o_a, o_b], axis=1)
    landed0 = _ag_wait(sems0[0], sems0[1], sems0[2], merged, name="gather0_wait")
    gw[0] = whole(_pair_forward(landed0, name="gather0_pass")[0], shards[0])
    sems1 = _ag_start([slots[1]], gw[0], name="gather1_start")
    h1 = _mm(merged, **W('w_out_e', 0), epilogue=_add_res, extras=(x,), name="out_e", after=(sems1[3],))
    h3, saved0 = mlp_ple_fwd(h1, 0)

    u = norm_cast(h3, a['norm_mix'][1], F32, "rms_mix1")
    y, xre, xim = _s5_fwd(u, bre, bim, cre, cim, pw_r, pw_i, s5_d_full)
    actg = _rowwise(jax.nn.gelu, [y], [], [(D_MODEL, BF16)], name="gelu")
    landed1 = _ag_wait(sems1[0], sems1[1], sems1[2], actg, name="gather1_wait")
    gw[1] = whole(_pair_forward(landed1, name="gather1_pass")[0], shards[1])
    glu, z = _mm(actg, **W('w_glu', 0), extras=(actg, b_glu_full), out_dtypes=(BF16, F32), name="glu",
                 epilogue=lambda acc, act, b: (act.astype(F32) * jax.nn.sigmoid(acc + b), acc + b))
    h4 = _mm(glu, **W('w_out_o', 0), epilogue=_add_res, extras=(h3,), name="out_o")
    h6, saved1 = mlp_ple_fwd(h4, 1)

    def head(h, t, g):
        def f(h, g):
            e = _rms(h, g) - t
            return 0.5 * jnp.sum(jnp.mean(e * e, axis=-1))
        val, vjp = jax.vjp(f, h, g)
        dh, dg = vjp(jnp.ones((), F32))
        return dh, dg, jnp.full((1, HEAD), val)

    dh, d_final, loss_part = _rowwise(head, [h6, tgt], [row(a['final_norm'])], [(D_MODEL, F32)], sums=(D_MODEL, HEAD),
                                      name="loss_head")
    loss = lax.psum(loss_part[0, 0], ("x", "y", "c"))

    gb = {k: lax.empty((4, rt[k], PACK_COLS), BF16) for k in (0, 1)}

    def into(lhs, rhs, key, name, after=()):
        k = pack_of[key]
        gb[k] = _mm(lhs, rhs, dims="tn", out_into=gb[k], out_view=views[key], out_dtypes=(BF16,), name=name, after=after)

    def mlp_ple_bwd(dh, l, saved, after=()):
        h, hn, up, act, h2, hnp, pp, gpre = saved

        def gate_bwd(d, gpre, pp):
            s = jax.nn.sigmoid(gpre)
            return d * s, d * pp * s * (1.0 - s)

        d_pp, d_gp = _rowwise(gate_bwd, [dh, gpre, pp], [], [(D_MODEL, BF16), (D_MODEL, BF16)], name=f"ple_bwd{l}")
        d_wpp = _mm(a['p'][l, 0], d_pp, dims="tn", name=f"d_ple_proj{l}")
        into(hnp, d_gp, ('w_ple_gate', l), f"d_ple_gate{l}")
        d_hnp = _mm(d_gp, **W('w_ple_gate', l), dims="nt", name=f"ple_gate_t{l}", after=after)
        dh2, dh2_b, d_nple = _rowwise(_rms_bwd_both, [h2, d_hnp, dh], [row(a['norm_ple'][l])],
                                      [(D_MODEL, F32), (D_MODEL, BF16)], sums=(D_MODEL,), name=f"rms_ple_bwd{l}")
        d_up = _mm(dh2_b, **W('w_down', l), dims="nt", extras=(up,), out_dtypes=(BF16,), name=f"down_t{l}",
                   epilogue=lambda acc, up: (acc * 2.0 * jnp.maximum(up.astype(F32), 0.0),))
        into(act, dh2_b, ('w_down', l), f"d_down{l}")
        into(hn, d_up, ('w_up', l), f"d_up{l}")
        d_hn = _mm(d_up, **W('w_up', l), dims="nt", name=f"up_t{l}")
        dh1, dh1_b, d_nmlp = _rowwise(_rms_bwd_both, [h, d_hn, dh2], [row(a['norm_mlp'][l])],
                                      [(D_MODEL, F32), (D_MODEL, BF16)], sums=(D_MODEL,), name=f"rms_mlp_bwd{l}")
        return dh1, dh1_b, d_wpp, d_nple, d_nmlp

    dh4, dh4_b, d_wpp1, d_nple1, d_nmlp1 = mlp_ple_bwd(dh, 1, saved1)

    d_glu = _mm(dh4_b, **W('w_out_o', 0), dims="nt", name="out_o_t")
    into(glu, dh4_b, ('w_out_o', 0), "d_out_o")

    def glu_bwd(d, z, act):
        s, act = jax.nn.sigmoid(z), act.astype(F32)
        dz = d * act * s * (1.0 - s)
        return dz, d * s, jnp.sum(dz, axis=0, keepdims=True)

    d_z, d_actp, d_bglu = _rowwise(glu_bwd, [d_glu, z, actg], [], [(D_MODEL, BF16), (D_MODEL, F32)], sums=(D_MODEL,),
                                   name="glu_bwd")
    into(actg, d_z, ('w_glu', 0), "d_glu")

    def reduce_start(bufs, tag):
        parts = [t if t.shape[1] % 32 else t.reshape(4, 2, t.shape[1] // 2, t.shape[2]) for t in bufs]
        recv = _pair_split(parts, name=f"pair_split{tag}")
        sums = [_pair_sum(f, r, c_idx, name=f"pair_sum{tag}_{i}") for i, (f, r) in enumerate(zip(parts, recv))]
        started, prev = [], c_idx
        for i, s in enumerate(sums):
            started.append(_rs_start(s, prev, name=f"scatter{tag}_{i}_start"))
            prev = started[-1][4]
        return started

    def pack_split(k):
        part = gb[k].reshape(4, 2, rt[k] // 2, PACK_COLS)
        return _pair_split_start(part, c_idx, name=f"pair_split{k}_start")

    def pack_scatter(k, split, after):
        part, recv = _pair_split_wait(split[0], split[1], split[2], split[3], after, name=f"pair_split{k}_wait")
        return [_rs_start(_pair_sum(part, recv, c_idx, name=f"pair_sum{k}_0"), c_idx, name=f"scatter{k}_0_start")]

    def reduce_finish(started, after, tag):
        halves = []
        for i, (ssem, rsem, part, land, _) in enumerate(started):
            part, land = _rs_wait(ssem, rsem, part, land, after, name=f"scatter{tag}_{i}_wait")
            halves.append(_sum_own_recv(part, land, me_idx, name=f"sum_partials{tag}_{i}"))
        return list(zip(halves, _pair_swap(halves, name=f"swap_halves{tag}")))

    split1 = pack_split(1)

    def gelu_bwd(acc, dap, y):
        _, vjp = jax.vjp(jax.nn.gelu, y)
        return vjp(acc + dap)

    dy = _mm(d_z, **W('w_glu', 0), dims="nt", extras=(d_actp, y), epilogue=gelu_bwd, name="glu_t",
             after=(split1[4],))
    du, d_bre, d_bim, d_cre, d_cim, d_lr, d_li, d_s5d = _s5_bwd(dy, u, xre, xim, bre, bim, cre, cim,
                                                                 pw_r[:, ::-1], -pw_i[:, ::-1], s5_d_full)
    dh3, d_nmix1 = _rowwise(_rms_bwd_fn, [h3, du, dh4], [row(a['norm_mix'][1])], [(D_MODEL, F32)], sums=(D_MODEL,),
                            name="rms_mix1_bwd")
    d_are, d_aim, d_logdt, d_sbre, d_sbim, d_scre, d_scim = s5_vjp(
        (d_bre, d_bim, d_cre, d_cim, d_lr.sum(axis=1), d_li.sum(axis=1)))

    started1 = pack_scatter(1, split1, dh3)
    dh1, dh1_b, d_wpp0, d_nple0, d_nmlp0 = mlp_ple_bwd(dh3, 0, saved0, after=(started1[0][4],))

    d_merged = _mm(dh1_b, **W('w_out_e', 0), dims="nt", name="out_e_t")
    into(merged, dh1_b, ('w_out_e', 0), "d_out_e")
    split0 = pack_split(0)
    dq, df, dv, dg, d_lb, d_gna = _hgrn_bwd(proj, lb, gna, st_a, d_merged, after=(split0[4],))
    started0 = pack_scatter(0, split0, dq)
    dqb, dkb, dvb, dzb, d_tail, d_arow, d_dtrow, d_gnb = _gdn_bwd(act_b, proj, arow, dtrow, gnb, st_b, d_merged,
                                                                  after=(started0[0][4],))
    d_pre, d_conv = _conv_bwd(proj, conv_full, dqb, dkb, dvb)
    d_proj = jnp.concatenate([dq, df, dv, dg, d_pre, dzb, d_tail], axis=1)
    d_win_t = _mm(d_proj, hn0, dims="tn", bm=_IN_BLK, out_dtypes=(BF16,), name="d_in_proj")
    d_hn0 = _mm(d_proj, win_t, bk=_IN_BLK, name="in_proj_t")
    grad_x, d_nmix0 = _rowwise(_rms_bwd_fn, [x, d_hn0, dh1], [row(a['norm_mix'][0])], [(D_MODEL, F32)], sums=(D_MODEL,),
                               name="rms_mix0_bwd")
    (d_hlb,) = lb_vjp(d_lb.reshape(1, 1024))
    d_alog, d_dtb = gdn_vjp((d_arow, d_dtrow))

    d_win_sh = d_win_t[:8208].reshape(4, 2052, D_MODEL)
    d_wpp_sh = jnp.stack([d_wpp0, d_wpp1]).astype(BF16).reshape(2, 256, 4, 512).transpose(2, 0, 1, 3).reshape(4, 512, 512)
    started2 = reduce_start([d_win_sh, d_wpp_sh], 2)
    small = {
        'norm_mix': jnp.concatenate([d_nmix0, d_nmix1]), 'norm_mlp': jnp.concatenate([d_nmlp0, d_nmlp1]),
        'norm_ple': jnp.concatenate([d_nple0, d_nple1]), 'hgrn_lb': d_hlb, 'g_norm_a': d_gna, 'conv_w': d_conv,
        'a_log': d_alog, 'dt_bias': d_dtb, 'g_norm_b': d_gnb, 's5_a_re': d_are, 's5_a_im': d_aim, 's5_b_re': d_sbre,
        's5_b_im': d_sbim, 's5_c_re': d_scre, 's5_c_im': d_scim, 's5_d': d_s5d, 's5_log_dt': d_logdt,
        'b_glu': d_bglu, 'final_norm': d_final}
    s_names = list(small)
    s_shapes = [tuple(small[n].shape) for n in s_names]
    mine_small = _pack_rows([small[n] for n in s_names])
    chip_small = _rowwise(lambda p, o: p + o, [mine_small, _pair_swap([mine_small], name="swap_small")[0]], [],
                          [(128, F32)], name="add_small")
    n_small = chip_small.shape[0]
    small_go = _ag_start([_own_slot(chip_small.reshape(2, n_small // 2, 128), me_idx, name="own_slot_small")],
                         started2[-1][4], name="gather_small_start")
    red = {1: reduce_finish(started1, small_go[3], 1)[0]}
    red[0] = reduce_finish(started0, small_go[3], 0)[0]
    res, order = {}, []

    def adam_layer(nm, layer, key, g_off=None):
        g_off = views[nm, layer][1] if g_off is None else g_off
        prev = res.get(nm)
        follows = () if not order or (prev is not None and order[-1] is prev[0]) else (order[-1],)
        res[nm] = tuple(_adamw(a[nm], a['m_' + nm], a['v_' + nm], layer, *red[key], c_idx, name=f"adam_{nm}{layer}",
                               g_off=g_off, prev=prev, after=follows))
        order.append(res[nm][0])

    for nm, layer in (('w_glu', 0), ('w_out_o', 0), ('w_up', 1), ('w_down', 1), ('w_ple_gate', 1)):
        adam_layer(nm, layer, 1)
    for nm in ('w_up', 'w_down', 'w_ple_gate', 'w_out_e'):
        adam_layer(nm, 0, 0)

    small_all = _pair_forward(_ag_wait(small_go[0], small_go[1], small_go[2], order[-1], name="gather_small_wait"),
                              name="gather_small_pass")[0]
    reduced = _sum_slots(small_all.reshape(4, n_small, 128), name="sum_small")
    sg = dict(zip(s_names, _unpack_rows(reduced, s_shapes)))
    sg['conv_w'] = lax.dynamic_slice_in_dim(sg['conv_w'], me * 768, 768, axis=1)
    sg['s5_d'] = lax.dynamic_slice_in_dim(sg['s5_d'], me * 512, 512, axis=1)
    sg['b_glu'] = lax.dynamic_slice_in_dim(sg['b_glu'], me * 512, 512, axis=1)
    sg = {n: sg[n].reshape(a[n].shape) for n in s_names}
    g_pack = _pack_rows([sg[n] for n in s_names])
    sd, sm, sv = _rowwise(_adam_math, [w_pack, g_pack, m_pack, v_pack], [], [(128, F32)] * 3, name="adam_small")
    w_shapes = [tuple(a[n].shape) for n in s_names]
    res.update({n: (sg[n], d_, m_, v_) for n, d_, m_, v_ in zip(s_names, _unpack_rows(sd, w_shapes),
                                                               _unpack_rows(sm, w_shapes), _unpack_rows(sv, w_shapes))})

    red['in'], red['pp'] = reduce_finish(started2, sd, 2)
    res['w_in_e'] = tuple(jnp.transpose(o, (1, 2, 0)) for o in _adamw_t(
        rows_first(a['w_in_e']), rows_first(a['m_w_in_e']), rows_first(a['v_w_in_e']),
        *[h.reshape(2052, 1, D_MODEL // 2) for h in red['in']], c_idx, name="adam_w_in_e"))
    adam_layer('w_ple_proj', 0, 'pp', 0)
    adam_layer('w_ple_proj', 1, 'pp', 256)

    return (loss, grad_x[None], *[res[n][0] for n in _WEIGHTS], *[res[n][1] for n in _WEIGHTS],
            *[res[n][2] for n in _WEIGHTS], *[res[n][3] for n in _WEIGHTS])


def kernel(x, p, norm_mix, norm_mlp, norm_ple, w_in_e, w_out_e, hgrn_lb, g_norm_a, conv_w, a_log, dt_bias, g_norm_b, s5_a_re, s5_a_im, s5_b_re, s5_b_im, s5_c_re, s5_c_im, s5_d, s5_log_dt, w_glu, b_glu, w_out_o, w_up, w_down, w_ple_gate, w_ple_proj, final_norm, loss_target, m_norm_mix, m_norm_mlp, m_norm_ple, m_w_in_e, m_w_out_e, m_hgrn_lb, m_g_norm_a, m_conv_w, m_a_log, m_dt_bias, m_g_norm_b, m_s5_a_re, m_s5_a_im, m_s5_b_re, m_s5_b_im, m_s5_c_re, m_s5_c_im, m_s5_d, m_s5_log_dt, m_w_glu, m_b_glu, m_w_out_o, m_w_up, m_w_down, m_w_ple_gate, m_w_ple_proj, m_final_norm, v_norm_mix, v_norm_mlp, v_norm_ple, v_w_in_e, v_w_out_e, v_hgrn_lb, v_g_norm_a, v_conv_w, v_a_log, v_dt_bias, v_g_norm_b, v_s5_a_re, v_s5_a_im, v_s5_b_re, v_s5_b_im, v_s5_c_re, v_s5_c_im, v_s5_d, v_s5_log_dt, v_w_glu, v_b_glu, v_w_out_o, v_w_up, v_w_down, v_w_ple_gate, v_w_ple_proj, v_final_norm):
    args = (x, p, norm_mix, norm_mlp, norm_ple, w_in_e, w_out_e, hgrn_lb, g_norm_a, conv_w, a_log, dt_bias, g_norm_b, s5_a_re, s5_a_im, s5_b_re, s5_b_im, s5_c_re, s5_c_im, s5_d, s5_log_dt, w_glu, b_glu, w_out_o, w_up, w_down, w_ple_gate, w_ple_proj, final_norm, loss_target, m_norm_mix, m_norm_mlp, m_norm_ple, m_w_in_e, m_w_out_e, m_hgrn_lb, m_g_norm_a, m_conv_w, m_a_log, m_dt_bias, m_g_norm_b, m_s5_a_re, m_s5_a_im, m_s5_b_re, m_s5_b_im, m_s5_c_re, m_s5_c_im, m_s5_d, m_s5_log_dt, m_w_glu, m_b_glu, m_w_out_o, m_w_up, m_w_down, m_w_ple_gate, m_w_ple_proj, m_final_norm, v_norm_mix, v_norm_mlp, v_norm_ple, v_w_in_e, v_w_out_e, v_hgrn_lb, v_g_norm_a, v_conv_w, v_a_log, v_dt_bias, v_g_norm_b, v_s5_a_re, v_s5_a_im, v_s5_b_re, v_s5_b_im, v_s5_c_re, v_s5_c_im, v_s5_d, v_s5_log_dt, v_w_glu, v_b_glu, v_w_out_o, v_w_up, v_w_down, v_w_ple_gate, v_w_ple_proj, v_final_norm)
    return _step(dict(zip(_INPUTS, args)))
```

```python
import functools
import math

import jax
import jax.numpy as jnp
from jax import lax
from jax.experimental import pallas as pl
from jax.experimental.pallas import tpu as pltpu

F32 = jnp.float32
BF16 = jnp.bfloat16

D_MODEL = 2048
SEQ = 4096
NORM_EPS = 1e-6
CHUNK = 64
HEAD = 128
VMEM_LIMIT = 56 * 1024 * 1024


_NN = (((1,), (0,)), ((), ()))
_NT = (((1,), (1,)), ((), ()))
_TN = (((0,), (0,)), ((), ()))
_HI = lax.Precision.HIGHEST
_NEG = -1e30


def _cparams(sem, **kw):
    return pltpu.CompilerParams(dimension_semantics=sem, vmem_limit_bytes=VMEM_LIMIT, **kw)


PACK_COLS = 2048


def _view_shape(view):
    kind, _, r = view
    return (4 * r, PACK_COLS) if kind == "row" else (r, 4 * PACK_COLS)


def _view_spec(view, rb, cb, row_of, col_of):
    kind, off, r = view
    assert off % rb == 0 and r % rb == 0 and PACK_COLS % cb == 0, (view, rb, cb)
    if kind == "row":
        nrb = r // rb
        return pl.BlockSpec((None, rb, cb), lambda i, j, k: (row_of(i, j, k) // nrb,
                                                             off // rb + row_of(i, j, k) % nrb, col_of(i, j, k)))
    ncb = PACK_COLS // cb
    return pl.BlockSpec((None, rb, cb), lambda i, j, k: (col_of(i, j, k) // ncb,
                                                         off // rb + row_of(i, j, k), col_of(i, j, k) % ncb))


def _mm(a, b, *, dims="nn", epilogue=None, extras=(), out_dtypes=(F32,), bm=1024, bn=1024, bk=2048, name,
        b_view=None, out_into=None, out_view=None, after=()):
    b_shape = _view_shape(b_view) if b_view is not None else b.shape
    if dims == "tn":
        (K, M), (K2, N) = a.shape, b_shape
    elif dims == "nt":
        (M, K), (N, K2) = a.shape, b_shape
    else:
        (M, K), (K2, N) = a.shape, b_shape
    assert K == K2, (a.shape, b_shape, dims)
    if b_view is not None and dims == "nt":
        bn = min(bn, b_view[2])
    if b_view is not None and dims != "nt":
        bk = min(bk, b_view[2])
    if out_view is not None:
        bm = min(bm, out_view[2])
    bm, bn, bk = min(bm, M), min(bn, N), min(bk, K)
    assert M % bm == 0 and N % bn == 0 and K % bk == 0, (M, N, K, bm, bn, bk)
    nk = K // bk
    ii, jj, kk = (lambda i, j, k: i), (lambda i, j, k: j), (lambda i, j, k: k)
    if dims == "tn":
        a_spec = pl.BlockSpec((bk, bm), lambda i, j, k: (k, i))
        dn = _TN
    else:
        a_spec = pl.BlockSpec((bm, bk), lambda i, j, k: (i, k))
        dn = _NT if dims == "nt" else _NN
    if dims == "nt":
        b_spec = _view_spec(b_view, bn, bk, jj, kk) if b_view else pl.BlockSpec((bn, bk), lambda i, j, k: (j, k))
    else:
        b_spec = _view_spec(b_view, bk, bn, kk, jj) if b_view else pl.BlockSpec((bk, bn), lambda i, j, k: (k, j))
    e_specs = []
    for e in extras:
        if e.shape == (M, N):
            e_specs.append(pl.BlockSpec((bm, bn), lambda i, j, k: (i, j)))
        else:
            assert e.shape == (1, N), e.shape
            e_specs.append(pl.BlockSpec((1, bn), lambda i, j, k: (0, j)))
    ne, no = len(extras), len(out_dtypes)
    if epilogue is None:
        epilogue = lambda acc: (acc,)
    into = out_into is not None

    def body(a_ref, b_ref, *rest):
        e_refs, rest = rest[:ne], rest[ne + (1 if into else 0) + len(after):]
        o_refs = rest[:no]
        part = lax.dot_general(a_ref[...].astype(BF16), b_ref[...].astype(BF16), dn, preferred_element_type=F32)

        def finish(total):
            outs = epilogue(total, *[e[...] for e in e_refs])
            for o, v in zip(o_refs, outs):
                o[...] = v.astype(o.dtype)

        if nk == 1:
            finish(part)
            return
        acc, k = rest[no], pl.program_id(2)

        @pl.when(k == 0)
        def _():
            acc[...] = part

        @pl.when((k > 0) & (k < nk - 1))
        def _():
            acc[...] += part

        @pl.when(k == nk - 1)
        def _():
            finish(acc[...] + part)

    if into:
        assert no == 1 and _view_shape(out_view) == (M, N), (out_view, M, N)
        out_specs = [_view_spec(out_view, bm, bn, ii, jj)]
        out_shape = [jax.ShapeDtypeStruct(out_into.shape, out_into.dtype)]
        extra_in, extra_specs, alias = [out_into], [pl.BlockSpec(memory_space=pl.ANY)], {2 + ne: 0}
    else:
        out_specs = [pl.BlockSpec((bm, bn), lambda i, j, k: (i, j)) for _ in out_dtypes]
        out_shape = [jax.ShapeDtypeStruct((M, N), dt) for dt in out_dtypes]
        extra_in, extra_specs, alias = [], [], {}
    outs = pl.pallas_call(
        body, name=name,
        grid=(M // bm, N // bn, nk),
        in_specs=[a_spec, b_spec] + e_specs + extra_specs + [pl.BlockSpec(memory_space=pl.ANY)] * len(after),
        out_specs=out_specs, out_shape=out_shape,
        scratch_shapes=[pltpu.VMEM((bm, bn), F32)] if nk > 1 else [],
        input_output_aliases=alias,
        compiler_params=_cparams(("parallel", "parallel", "arbitrary")),
    )(a, b, *extras, *extra_in, *after)
    return outs[0] if no == 1 else tuple(outs)


def _rowwise(fn, tiled, bcast, outs, sums=(), *, rows=512, name):
    S = tiled[0].shape[0]
    rows = min(rows, S)
    assert S % rows == 0
    nt, nb, no, ns = len(tiled), len(bcast), len(outs), len(sums)

    def body(*refs):
        t_refs, b_refs = refs[:nt], refs[nt:nt + nb]
        o_refs, s_refs = refs[nt + nb:nt + nb + no], refs[nt + nb + no:]
        res = fn(*[r[...] for r in t_refs], *[r[...] for r in b_refs])
        res = res if isinstance(res, (tuple, list)) else (res,)
        for o, v in zip(o_refs, res[:no]):
            o[...] = v.astype(o.dtype)
        if ns:
            @pl.when(pl.program_id(0) == 0)
            def _():
                for s in s_refs:
                    s[...] = jnp.zeros_like(s)
            for s, v in zip(s_refs, res[no:]):
                s[...] += v

    res = pl.pallas_call(
        body, name=name,
        grid=(S // rows,),
        in_specs=[pl.BlockSpec((rows, t.shape[1]), lambda i: (i, 0)) for t in tiled]
        + [pl.BlockSpec(b.shape, lambda i, nd=b.ndim: (0,) * nd) for b in bcast],
        out_specs=[pl.BlockSpec((rows, c), lambda i: (i, 0)) for c, _ in outs]
        + [pl.BlockSpec((1, c), lambda i: (0, 0)) for c in sums],
        out_shape=[jax.ShapeDtypeStruct((S, c), dt) for c, dt in outs]
        + [jax.ShapeDtypeStruct((1, c), F32) for c in sums],
        compiler_params=_cparams(("arbitrary",)),
    )(*tiled, *bcast)
    return res[0] if len(res) == 1 else tuple(res)


def _rms(x, g):
    return x * lax.rsqrt(jnp.mean(x * x, axis=-1, keepdims=True) + NORM_EPS) * g


def _mxu(a, b, dn):
    if a.ndim == 3:
        ((ca,), (cb,)), _ = dn
        dn = (((ca + 1,), (cb + 1,)), ((0,), (0,)))
    return lax.dot_general(a, b, dn, preferred_element_type=F32)


def _split(a):
    hi = a.astype(BF16)
    return hi, (a - hi.astype(F32)).astype(BF16)


def _passes(a, b, dn, three):
    if not three:
        return _mxu(a.astype(BF16), b.astype(BF16), dn)
    (ah, al), (bh, bl) = _split(a), _split(b)
    return _mxu(ah, bh, dn) + (_mxu(ah, bl, dn) + _mxu(al, bh, dn))


def _dot_grads(a, b, g, dn, three):
    if dn == _NN:
        return _passes(g, b, _NT, three), _passes(a, g, _TN, three)
    if dn == _NT:
        return _passes(g, b, _NN, three), _passes(g, a, _TN, three)
    assert dn == _TN
    return _passes(b, g, _NT, three), _passes(a, g, _NN, three)


@functools.partial(jax.custom_vjp, nondiff_argnums=(2,))
def _dot(a, b, dn=_NN):
    return _passes(a, b, dn, False)


_dot.defvjp(lambda a, b, dn: (_passes(a, b, dn, False), (a, b)),
            lambda dn, res, g: _dot_grads(res[0], res[1], g, dn, False))


@functools.partial(jax.custom_vjp, nondiff_argnums=(2,))
def _dot3(a, b, dn=_NN):
    return _passes(a, b, dn, True)


_dot3.defvjp(lambda a, b, dn: (_passes(a, b, dn, True), (a, b)),
             lambda dn, res, g: _dot_grads(res[0], res[1], g, dn, True))


def _tri_times(x, dn):
    tri = _tril(x.shape[-2]).astype(BF16)
    if x.ndim == 3:
        tri = jnp.broadcast_to(tri, (x.shape[0],) + tri.shape)
    hi, lo = _split(x)
    lo2 = (x - hi.astype(F32) - lo.astype(F32)).astype(BF16)
    return _mxu(tri, hi, dn) + (_mxu(tri, lo, dn) + _mxu(tri, lo2, dn))


@jax.custom_vjp
def _cumsum_rows(x):
    return _tri_times(x, _NN)


_cumsum_rows.defvjp(lambda x: (_tri_times(x, _NN), None), lambda _, g: (_tri_times(g, _TN),))


def _iota(shape, dim):
    return lax.broadcasted_iota(jnp.int32, shape, dim)


def _tril(n, strict=False):
    r, c = _iota((n, n), 0), _iota((n, n), 1)
    return (r > c) if strict else (r >= c)


@functools.partial(jax.custom_vjp, nondiff_argnums=(1,))
def _roll_rows(x, r):
    return pltpu.roll(x, r, 0)


def _roll_rows_fwd(x, r):
    return pltpu.roll(x, r, 0), None


def _roll_rows_bwd(r, _, g):
    return (pltpu.roll(g, (g.shape[0] - r) % g.shape[0], 0),)


_roll_rows.defvjp(_roll_rows_fwd, _roll_rows_bwd)


def _head_norm_gate(o, gn, gate):
    return _rms(o, gn) * jax.nn.silu(gate)


_SUB = 16
_HPS = 8


def _hgrn_chunk(q, fpre, v, gate, lb, gna, st):
    c = q.shape[0]
    forget = lb + (1.0 - lb) * jax.nn.sigmoid(fpre)
    k = 1.0 - forget
    logf = jnp.log(forget)
    cum = _cumsum_rows(logf)
    cum_end = jnp.sum(logf, axis=0, keepdims=True)
    o = _dot(q * jnp.exp(cum), st, _NT)
    st_new = st * jnp.exp(cum_end) + _dot(v, k * jnp.exp(cum_end - cum), _TN)
    t = _iota((c, 1), 0)
    s_off = jnp.zeros((c, c), F32)
    for i in range(1, c // _SUB):
        before = t < i * _SUB
        c_i = jnp.sum(jnp.where(before, logf, 0.0), axis=0, keepdims=True)
        in_blk = (t >= i * _SUB) & (t < (i + 1) * _SUB)
        qi = jnp.where(in_blk, q * jnp.exp(jnp.minimum(cum - c_i, 0.0)), 0.0)
        ki = jnp.where(before, k * jnp.exp(jnp.minimum(c_i - cum, 0.0)), 0.0)
        s_off = s_off + _dot(qi, ki, _NT)
    o = o + _dot(s_off, v)
    tmod = t % _SUB
    for r in range(_SUB):
        kr, cr, vr = (k, cum, v) if r == 0 else (_roll_rows(k, r), _roll_rows(cum, r), _roll_rows(v, r))
        w = q * kr * jnp.exp(jnp.where(tmod >= r, cum - cr, _NEG))
        o = o + jnp.sum(w, axis=1, keepdims=True) * vr
    return _head_norm_gate(o, gna, gate), st_new


def _hgrn_fwd(proj, lb, gna):
    S = proj.shape[0]
    n_chunks, heads = S // CHUNK, 8

    def body(q_ref, f_ref, v_ref, g_ref, lb_ref, gna_ref, o_ref, st_out, st):
        n, hb = pl.program_id(0), pl.program_id(1)

        @pl.when(n == 0)
        def _():
            st[hb] = jnp.zeros((_HPS, HEAD, HEAD), F32)

        sls = [slice(hh * HEAD, (hh + 1) * HEAD) for hh in range(_HPS)]
        st0 = [st[hb, hh] for hh in range(_HPS)]
        ins = [(q_ref[:, sl], f_ref[:, sl], v_ref[:, sl], g_ref[:, sl], lb_ref[:, sl]) for sl in sls]
        gna = gna_ref[...]
        res = [_hgrn_chunk(*ins[hh], gna, st0[hh]) for hh in range(_HPS)]
        for hh, sl in enumerate(sls):
            st_out[hh] = st0[hh]
            o_ref[:, sl] = res[hh][0].astype(o_ref.dtype)
            st[hb, hh] = res[hh][1]

    wide = _HPS * HEAD
    sec = lambda s: pl.BlockSpec((CHUNK, wide), lambda n, h, s=s: (n, (8 // _HPS) * s + h))
    return pl.pallas_call(
        body, name="hgrn_fwd", grid=(n_chunks, heads // _HPS),
        in_specs=[sec(0), sec(1), sec(2), sec(3),
                  pl.BlockSpec((1, wide), lambda n, h: (0, h)), pl.BlockSpec((1, HEAD), lambda n, h: (0, 0))],
        out_specs=[pl.BlockSpec((CHUNK, wide), lambda n, h: (n, h)),
                   pl.BlockSpec((None, _HPS, HEAD, HEAD), lambda n, h: (n, h, 0, 0))],
        out_shape=[jax.ShapeDtypeStruct((S, 1024), BF16),
                   jax.ShapeDtypeStruct((n_chunks, heads, HEAD, HEAD), F32)],
        scratch_shapes=[pltpu.VMEM((heads // _HPS, _HPS, HEAD, HEAD), F32)],
        compiler_params=_cparams(("arbitrary", "arbitrary")),
    )(proj, proj, proj, proj, lb, gna)


def _hgrn_bwd(proj, lb, gna, states, d_o, after=()):
    S = proj.shape[0]
    n_chunks, heads = S // CHUNK, 8

    def body(q_ref, f_ref, v_ref, g_ref, lb_ref, gna_ref, st_ref, do_ref, *rest):
        dq_ref, df_ref, dv_ref, dg_ref, dlb_ref, dgna_ref, dst = rest[len(after):]
        n, hb = pl.program_id(0), pl.program_id(1)

        @pl.when(n == 0)
        def _():
            dst[hb] = jnp.zeros((_HPS, HEAD, HEAD), F32)
            dlb_ref[hb] = jnp.zeros((_HPS, 1, HEAD), F32)

        @pl.when((n == 0) & (hb == 0))
        def _():
            dgna_ref[...] = jnp.zeros_like(dgna_ref)

        sls = [slice(hh * HEAD, (hh + 1) * HEAD) for hh in range(_HPS)]
        gna = gna_ref[...]
        ins = [(q_ref[:, sl], f_ref[:, sl], v_ref[:, sl], g_ref[:, sl], lb_ref[:, sl], gna, st_ref[hh])
               for hh, sl in enumerate(sls)]
        cts = [(do_ref[:, sl], dst[hb, hh]) for hh, sl in enumerate(sls)]
        grads = [jax.vjp(_hgrn_chunk, *ins[hh])[1](cts[hh]) for hh in range(_HPS)]
        dgna_sum = jnp.zeros((1, HEAD), F32)
        for hh, sl in enumerate(sls):
            dq, df, dv, dg, dlb, dgna, dst0 = grads[hh]
            dq_ref[:, sl] = dq.astype(dq_ref.dtype)
            df_ref[:, sl] = df.astype(df_ref.dtype)
            dv_ref[:, sl] = dv.astype(dv_ref.dtype)
            dg_ref[:, sl] = dg.astype(dg_ref.dtype)
            dlb_ref[hb, hh] += dlb
            dgna_sum = dgna_sum + dgna
            dst[hb, hh] = dst0
        dgna_ref[...] += dgna_sum

    rev = lambda n: n_chunks - 1 - n
    wide = _HPS * HEAD
    sec = lambda s: pl.BlockSpec((CHUNK, wide), lambda n, h, s=s: (rev(n), (8 // _HPS) * s + h))
    out = pl.BlockSpec((CHUNK, wide), lambda n, h: (rev(n), h))
    return pl.pallas_call(
        body, name="hgrn_bwd", grid=(n_chunks, heads // _HPS),
        in_specs=[sec(0), sec(1), sec(2), sec(3),
                  pl.BlockSpec((1, wide), lambda n, h: (0, h)), pl.BlockSpec((1, HEAD), lambda n, h: (0, 0)),
                  pl.BlockSpec((None, _HPS, HEAD, HEAD), lambda n, h: (rev(n), h, 0, 0)),
                  pl.BlockSpec((CHUNK, wide), lambda n, h: (rev(n), h))] + [pl.BlockSpec(memory_space=pl.ANY)] * len(after),
        out_specs=[out, out, out, out,
                   pl.BlockSpec((heads // _HPS, _HPS, 1, HEAD), lambda n, h: (0, 0, 0, 0)),
                   pl.BlockSpec((1, HEAD), lambda n, h: (0, 0))],
        out_shape=[jax.ShapeDtypeStruct((S, 1024), BF16)] * 4
        + [jax.ShapeDtypeStruct((heads // _HPS, _HPS, 1, HEAD), F32), jax.ShapeDtypeStruct((1, HEAD), F32)],
        scratch_shapes=[pltpu.VMEM((heads // _HPS, _HPS, HEAD, HEAD), F32)],
        compiler_params=_cparams(("arbitrary", "arbitrary")),
    )(proj, proj, proj, proj, lb, gna, states, d_o, *after)


def _l2n(t):
    return t * lax.rsqrt(jnp.sum(t * t, axis=-1, keepdims=True) + NORM_EPS)


def _unit_lower_inverse(low):
    c = low.shape[-1]
    inv = (_iota((c, c), 0) == _iota((c, c), 1)).astype(F32) - low
    p = low
    span = 2
    while span < c:
        p = _dot3(p, p)
        inv = inv + _dot3(inv, p)
        span *= 2
    return inv


def _gdn_chunk(qc, kc, v, z, tail, h0, arow, dtrow, gnb, st):
    nb, c = qc.shape[0], qc.shape[1]
    head = h0 + _iota((nb, c, HEAD), 0)
    lane = _iota((nb, c, HEAD), 2)
    la_all = arow * jax.nn.softplus(tail + dtrow)
    la = jnp.sum(jnp.where(lane == head, la_all[None], 0.0), axis=2, keepdims=True)
    beta = jnp.sum(jnp.where(lane == head + 8, jax.nn.sigmoid(tail)[None], 0.0), axis=2, keepdims=True)
    la_b = jnp.broadcast_to(la, (nb, c, HEAD))
    cum = _cumsum_rows(la_b)
    cmat = _cumsum_rows(jnp.broadcast_to(la, (nb, c, c)))
    cum_end = jnp.sum(la_b, axis=1, keepdims=True)
    decay = jnp.exp(jnp.where(_tril(c), cmat - jnp.swapaxes(cmat, 1, 2), _NEG))
    q = _l2n(qc) * (HEAD ** -0.5)
    k = _l2n(kc)
    k_beta = k * beta
    low = jnp.where(_tril(c, strict=True), _dot(k_beta, k, _NT) * decay, 0.0)
    inv = _unit_lower_inverse(low)
    u, w = _dot(inv, v * beta), _dot(inv, k_beta * jnp.exp(cum))
    intra = _dot(q, k, _NT) * decay
    v_new = u - _dot(w, st, _NT)
    o = _dot(q * jnp.exp(cum), st, _NT) + _dot(intra, v_new)
    st_new = st * jnp.exp(cum_end) + _dot(v_new, k * jnp.exp(cum_end - cum), _TN)
    return _head_norm_gate(o, gnb, z), st_new


def _gdn_specs(n_of):
    wide = _HPS * HEAD
    qkv = lambda s: pl.BlockSpec((CHUNK, wide), lambda n, h, s=s: (n_of(n), (8 // _HPS) * s + h))
    return [qkv(0), qkv(1), qkv(2),
            pl.BlockSpec((CHUNK, wide), lambda n, h: (n_of(n), 56 // _HPS + h)),
            pl.BlockSpec((CHUNK, HEAD), lambda n, h: (n_of(n), 64)),
            pl.BlockSpec((1, HEAD), lambda n, h: (0, 0)), pl.BlockSpec((1, HEAD), lambda n, h: (0, 0)),
            pl.BlockSpec((1, HEAD), lambda n, h: (0, 0))]


def _gdn_fwd(act, proj, arow, dtrow, gnb):
    S = act.shape[0]
    n_chunks, heads = S // CHUNK, 8

    def body(q_ref, k_ref, v_ref, z_ref, t_ref, a_ref, dt_ref, gnb_ref, o_ref, st_out, st):
        n, hb = pl.program_id(0), pl.program_id(1)

        @pl.when(n == 0)
        def _():
            st[hb] = jnp.zeros((_HPS, HEAD, HEAD), F32)

        sls = [slice(hh * HEAD, (hh + 1) * HEAD) for hh in range(_HPS)]
        heads_of = lambda ref: jnp.stack([ref[:, sl] for sl in sls])
        st0 = st[hb]
        o, st1 = _gdn_chunk(heads_of(q_ref), heads_of(k_ref), heads_of(v_ref), heads_of(z_ref), t_ref[...], hb * _HPS,
                            a_ref[...], dt_ref[...], gnb_ref[...], st0)
        st_out[...] = st0
        st[hb] = st1
        for hh, sl in enumerate(sls):
            o_ref[:, sl] = o[hh].astype(o_ref.dtype)

    return pl.pallas_call(
        body, name="gdn_fwd", grid=(n_chunks, heads // _HPS),
        in_specs=_gdn_specs(lambda n: n),
        out_specs=[pl.BlockSpec((CHUNK, _HPS * HEAD), lambda n, h: (n, h)),
                   pl.BlockSpec((None, _HPS, HEAD, HEAD), lambda n, h: (n, h, 0, 0))],
        out_shape=[jax.ShapeDtypeStruct((S, 1024), BF16),
                   jax.ShapeDtypeStruct((n_chunks, heads, HEAD, HEAD), F32)],
        scratch_shapes=[pltpu.VMEM((heads // _HPS, _HPS, HEAD, HEAD), F32)],
        compiler_params=_cparams(("arbitrary", "arbitrary")),
    )(act, act, act, proj, proj, arow, dtrow, gnb)


def _gdn_bwd(act, proj, arow, dtrow, gnb, states, d_o, after=()):
    S = act.shape[0]
    n_chunks, heads = S // CHUNK, 8

    def body(q_ref, k_ref, v_ref, z_ref, t_ref, a_ref, dt_ref, gnb_ref, st_ref, do_ref, *rest):
        dq_ref, dk_ref, dv_ref, dz_ref, dt_out, da_ref, ddt_ref, dgnb_ref, dst, dtail_acc = rest[len(after):]
        n, hb = pl.program_id(0), pl.program_id(1)
        n_hb = heads // _HPS

        @pl.when(n == 0)
        def _():
            dst[hb] = jnp.zeros((_HPS, HEAD, HEAD), F32)

        @pl.when((n == 0) & (hb == 0))
        def _():
            da_ref[...] = jnp.zeros_like(da_ref)
            ddt_ref[...] = jnp.zeros_like(ddt_ref)
            dgnb_ref[...] = jnp.zeros_like(dgnb_ref)

        sls = [slice(hh * HEAD, (hh + 1) * HEAD) for hh in range(_HPS)]
        heads_of = lambda ref: jnp.stack([ref[:, sl] for sl in sls])
        fn = lambda qc, kc, v, z, tail, arow, dtrow, gnb, st: _gdn_chunk(qc, kc, v, z, tail, hb * _HPS, arow, dtrow,
                                                                        gnb, st)
        _, vjp = jax.vjp(fn, heads_of(q_ref), heads_of(k_ref), heads_of(v_ref), heads_of(z_ref), t_ref[...],
                         a_ref[...], dt_ref[...], gnb_ref[...], st_ref[...])
        dq, dk, dv, dz, dtail_sum, da_sum, ddt_sum, dgnb_sum, dst0 = vjp((heads_of(do_ref), dst[hb]))
        dst[hb] = dst0
        for hh, sl in enumerate(sls):
            dq_ref[:, sl] = dq[hh]
            dk_ref[:, sl] = dk[hh]
            dv_ref[:, sl] = dv[hh]
            dz_ref[:, sl] = dz[hh].astype(dz_ref.dtype)

        @pl.when(hb == 0)
        def _():
            dtail_acc[...] = dtail_sum

        @pl.when(hb > 0)
        def _():
            dtail_acc[...] += dtail_sum

        @pl.when(hb == n_hb - 1)
        def _():
            dt_out[...] = dtail_acc[...].astype(dt_out.dtype)

        da_ref[...] += da_sum
        ddt_ref[...] += ddt_sum
        dgnb_ref[...] += dgnb_sum

    rev = lambda n: n_chunks - 1 - n
    out = pl.BlockSpec((CHUNK, _HPS * HEAD), lambda n, h: (rev(n), h))
    row = pl.BlockSpec((1, HEAD), lambda n, h: (0, 0))
    return pl.pallas_call(
        body, name="gdn_bwd", grid=(n_chunks, heads // _HPS),
        in_specs=_gdn_specs(rev)
        + [pl.BlockSpec((None, _HPS, HEAD, HEAD), lambda n, h: (rev(n), h, 0, 0)),
           pl.BlockSpec((CHUNK, _HPS * HEAD), lambda n, h: (rev(n), 8 // _HPS + h))]
        + [pl.BlockSpec(memory_space=pl.ANY)] * len(after),
        out_specs=[out, out, out, out, pl.BlockSpec((CHUNK, HEAD), lambda n, h: (rev(n), 0)), row, row, row],
        out_shape=[jax.ShapeDtypeStruct((S, 1024), F32)] * 3
        + [jax.ShapeDtypeStruct((S, 1024), BF16), jax.ShapeDtypeStruct((S, HEAD), BF16)]
        + [jax.ShapeDtypeStruct((1, HEAD), F32)] * 3,
        scratch_shapes=[pltpu.VMEM((heads // _HPS, _HPS, HEAD, HEAD), F32), pltpu.VMEM((CHUNK, HEAD), F32)],
        compiler_params=_cparams(("arbitrary", "arbitrary")),
    )(act, act, act, proj, proj, arow, dtrow, gnb, states, d_o, *after)


def _conv_silu(x, w):
    t = _iota((x.shape[0], 1), 0)
    tap = _iota(w.shape, 0)
    y = jnp.zeros_like(x)
    for r in range(4):
        w_r = jnp.sum(jnp.where(tap == 3 - r, w, 0.0), axis=0, keepdims=True)
        y = y + (x if r == 0 else jnp.where(t >= r, _roll_rows(x, r), 0.0)) * w_r
    return jax.nn.silu(y)


_CONV_COLS = 128


def _conv_fwd(proj, conv_w):
    S = proj.shape[0]
    nb = 3072 // _CONV_COLS
    off = 4096 // _CONV_COLS

    def body(x_ref, w_ref, o_ref):
        o_ref[...] = _conv_silu(x_ref[...], w_ref[...])

    return pl.pallas_call(
        body, name="conv_fwd", grid=(nb,),
        in_specs=[pl.BlockSpec((S, _CONV_COLS), lambda j: (0, off + j)), pl.BlockSpec((4, _CONV_COLS), lambda j: (0, j))],
        out_specs=pl.BlockSpec((S, _CONV_COLS), lambda j: (0, j)),
        out_shape=jax.ShapeDtypeStruct((S, 3072), F32),
        compiler_params=_cparams(("parallel",)),
    )(proj, conv_w)


def _conv_bwd(proj, conv_w, dq, dk, dv):
    S = proj.shape[0]
    nb = 3072 // _CONV_COLS
    off = 4096 // _CONV_COLS
    per = 1024 // _CONV_COLS

    def body(x_ref, w_ref, dq_ref, dk_ref, dv_ref, dx_ref, dw_ref):
        j = pl.program_id(0)
        _, vjp = jax.vjp(_conv_silu, x_ref[...], w_ref[...])
        d = jnp.where(j < per, dq_ref[...], jnp.where(j < 2 * per, dk_ref[...], dv_ref[...]))
        dx, dw = vjp(d)
        dx_ref[...] = dx.astype(dx_ref.dtype)
        dw_ref[...] = dw

    dsp = lambda s: pl.BlockSpec((S, _CONV_COLS), lambda j, s=s: (0, jnp.clip(j - s * per, 0, per - 1)))
    return pl.pallas_call(
        body, name="conv_bwd", grid=(nb,),
        in_specs=[pl.BlockSpec((S, _CONV_COLS), lambda j: (0, off + j)), pl.BlockSpec((4, _CONV_COLS), lambda j: (0, j)),
                  dsp(0), dsp(1), dsp(2)],
        out_specs=[pl.BlockSpec((S, _CONV_COLS), lambda j: (0, j)), pl.BlockSpec((4, _CONV_COLS), lambda j: (0, j))],
        out_shape=[jax.ShapeDtypeStruct((S, 3072), BF16), jax.ShapeDtypeStruct((4, 3072), F32)],
        compiler_params=_cparams(("parallel",)),
    )(proj, conv_w, dq, dk, dv)


_S5_T = 512
_S5_L = 512
_S5_NB = 16
_S5_U = 4
_S5_UB = 3


def _s5_tile(xr, xi, pw_r, pw_i, cr, ci, reverse):
    t8 = _iota((8, 1), 0)
    for sh in (1, 2, 4):
        row = (8 - sh) if reverse else (sh - 1)
        ar, ai = pw_r[row:row + 1, :], pw_i[row:row + 1, :]
        if reverse:
            keep, amt = t8 < 8 - sh, 8 - sh
        else:
            keep, amt = t8 >= sh, sh
        sr = jnp.where(keep, pltpu.roll(xr, amt, 0), 0.0)
        si = jnp.where(keep, pltpu.roll(xi, amt, 0), 0.0)
        xr, xi = xr + ar * sr - ai * si, xi + ar * si + ai * sr
    xr, xi = xr + pw_r * cr - pw_i * ci, xi + pw_r * ci + pw_i * cr
    return xr, xi


def _s5_fwd(u, bre, bim, cre, cim, pw_r, pw_i, dskip):
    S = u.shape[0]
    T = min(_S5_T, S)
    nt = S // T

    def body(u_ref, bre_ref, bim_ref, cre_ref, cim_ref, pr_ref, pi_ref, d_ref, y_ref, xr_ref, xi_ref,
             bu_r, bu_i, car_r, car_i):
        @pl.when(pl.program_id(1) == 0)
        def _():
            car_r[...] = jnp.zeros_like(car_r)
            car_i[...] = jnp.zeros_like(car_i)

        uu = u_ref[...]
        bu_r[...] = _dot(uu, bre_ref[...])
        bu_i[...] = _dot(uu, bim_ref[...])
        pw_r, pw_i = pr_ref[...], pi_ref[...]

        def tiles(i, carry):
            ins = []
            for k in range(_S5_U):
                r0 = pl.multiple_of((i * _S5_U + k) * 8, 8)
                ins.append((r0, bu_r[pl.ds(r0, 8), :], bu_i[pl.ds(r0, 8), :]))
            outs = []
            for r0, br, bi in ins:
                xr, xi = _s5_tile(br, bi, pw_r, pw_i, carry[0], carry[1], False)
                carry = (xr[7:8, :], xi[7:8, :])
                outs.append((r0, xr, xi))
            for r0, xr, xi in outs:
                xr_ref[pl.ds(r0, 8), :] = xr
                xi_ref[pl.ds(r0, 8), :] = xi
            return carry

        cr, ci = lax.fori_loop(0, T // (8 * _S5_U), tiles, (car_r[...], car_i[...]))
        car_r[...] = cr
        car_i[...] = ci
        y_ref[...] = _dot(xr_ref[...], cre_ref[...]) - _dot(xi_ref[...], cim_ref[...]) + d_ref[...] * uu

    blk3 = lambda a, b: pl.BlockSpec((None, a, b), lambda j, t: (j, 0, 0))
    return pl.pallas_call(
        body, name="s5_fwd", grid=(_S5_NB, nt),
        in_specs=[pl.BlockSpec((T, HEAD), lambda j, t: (t, j)),
                  blk3(HEAD, _S5_L), blk3(HEAD, _S5_L), blk3(_S5_L, HEAD), blk3(_S5_L, HEAD),
                  blk3(8, _S5_L), blk3(8, _S5_L), pl.BlockSpec((1, HEAD), lambda j, t: (0, j))],
        out_specs=[pl.BlockSpec((T, HEAD), lambda j, t: (t, j)),
                   pl.BlockSpec((T, _S5_L), lambda j, t: (t, j)), pl.BlockSpec((T, _S5_L), lambda j, t: (t, j))],
        out_shape=[jax.ShapeDtypeStruct((S, D_MODEL), F32),
                   jax.ShapeDtypeStruct((S, _S5_NB * _S5_L), F32), jax.ShapeDtypeStruct((S, _S5_NB * _S5_L), F32)],
        scratch_shapes=[pltpu.VMEM((T, _S5_L), F32), pltpu.VMEM((T, _S5_L), F32),
                        pltpu.VMEM((1, _S5_L), F32), pltpu.VMEM((1, _S5_L), F32)],
        compiler_params=_cparams(("parallel", "arbitrary")),
    )(u, bre, bim, cre, cim, pw_r, pw_i, dskip)


def _s5_bwd(dy, u, xre, xim, bre, bim, cre, cim, qw_r, qw_i, dskip):
    S = u.shape[0]
    T = min(_S5_T, S)
    nt = S // T
    nt8 = T // 8

    def body(dy_ref, u_ref, xr_ref, xi_ref, xpr_ref, xpi_ref, bre_ref, bim_ref, cre_ref, cim_ref, qr_ref, qi_ref,
             d_ref, du_ref, dbr_ref, dbi_ref, dcr_ref, dci_ref, dlr_ref, dli_ref, dd_ref,
             g_r, g_i, car_r, car_i):
        t = pl.program_id(1)

        @pl.when(t == 0)
        def _():
            car_r[...] = jnp.zeros_like(car_r)
            car_i[...] = jnp.zeros_like(car_i)
            for r in (dbr_ref, dbi_ref, dcr_ref, dci_ref, dlr_ref, dli_ref, dd_ref):
                r[...] = jnp.zeros_like(r)

        dyy, uu = dy_ref[...], u_ref[...]
        g_r[...] = _dot(dyy, cre_ref[...], _NT)
        g_i[...] = -_dot(dyy, cim_ref[...], _NT)
        qw_r, qw_i = qr_ref[...], qi_ref[...]
        t8 = _iota((8, 1), 0)
        first = t == nt - 1

        def load(r0, prev_r, prev_i):
            rows = pl.ds(r0, 8)
            return r0, g_r[rows, :], g_i[rows, :], xr_ref[rows, :], xi_ref[rows, :], prev_r, prev_i

        def run(loaded, carry, acc):
            done = []
            for r0, dr, di, xr, xi, prev_r, prev_i in loaded:
                gr, gi = _s5_tile(dr, di, qw_r, qw_i, carry[0], carry[1], True)
                carry = (gr[0:1, :], gi[0:1, :])
                xpr = jnp.where(t8 >= 1, pltpu.roll(xr, 1, 0), prev_r)
                xpi = jnp.where(t8 >= 1, pltpu.roll(xi, 1, 0), prev_i)
                acc = (acc[0] + gr * xpr + gi * xpi, acc[1] + gi * xpr - gr * xpi)
                done.append((r0, gr, gi))
            for r0, gr, gi in done:
                g_r[pl.ds(r0, 8), :] = gr
                g_i[pl.ds(r0, 8), :] = gi
            return carry, acc

        def step(ii, state):
            loaded = []
            for k in range(_S5_UB):
                idx = nt8 - 1 - (ii * _S5_UB + k)
                r0 = pl.multiple_of(idx * 8, 8)
                p0 = pl.multiple_of((idx - 1) * 8, 8)
                loaded.append(load(r0, xr_ref[pl.ds(p0, 8), :][7:8, :], xi_ref[pl.ds(p0, 8), :][7:8, :]))
            return run(loaded, *state)

        zero = jnp.zeros((8, _S5_L), F32)
        assert (nt8 - 1) % _S5_UB == 0
        state = lax.fori_loop(0, (nt8 - 1) // _S5_UB, step, ((car_r[...], car_i[...]), (zero, zero)))
        prev_r = jnp.where(first, 0.0, xpr_ref[...][7:8, :])
        prev_i = jnp.where(first, 0.0, xpi_ref[...][7:8, :])
        (cr, ci), (ar, ai) = run([load(0, prev_r, prev_i)], *state)
        car_r[...] = cr
        car_i[...] = ci
        dlr_ref[...] += ar
        dli_ref[...] += ai
        gr, gi = g_r[...], g_i[...]
        du_ref[...] = _dot(gr, bre_ref[...], _NT) + _dot(gi, bim_ref[...], _NT) + d_ref[...] * dyy
        dbr_ref[...] += _dot(uu, gr, _TN)
        dbi_ref[...] += _dot(uu, gi, _TN)
        dcr_ref[...] += _dot(xr_ref[...], dyy, _TN)
        dci_ref[...] -= _dot(xi_ref[...], dyy, _TN)
        dd_ref[...] += jnp.sum(dyy * uu, axis=0, keepdims=True)

    rev = lambda t: nt - 1 - t
    blk3 = lambda a, b: pl.BlockSpec((None, a, b), lambda j, t: (j, 0, 0))
    tl = pl.BlockSpec((T, HEAD), lambda j, t: (rev(t), j))
    xs = pl.BlockSpec((T, _S5_L), lambda j, t: (rev(t), j))
    xp = pl.BlockSpec((8, _S5_L), lambda j, t: (jnp.maximum(rev(t) * nt8 - 1, 0), j))
    return pl.pallas_call(
        body, name="s5_bwd", grid=(_S5_NB, nt),
        in_specs=[tl, tl, xs, xs, xp, xp, blk3(HEAD, _S5_L), blk3(HEAD, _S5_L), blk3(_S5_L, HEAD), blk3(_S5_L, HEAD),
                  blk3(8, _S5_L), blk3(8, _S5_L), pl.BlockSpec((1, HEAD), lambda j, t: (0, j))],
        out_specs=[tl, blk3(HEAD, _S5_L), blk3(HEAD, _S5_L), blk3(_S5_L, HEAD), blk3(_S5_L, HEAD),
                   blk3(8, _S5_L), blk3(8, _S5_L), pl.BlockSpec((1, HEAD), lambda j, t: (0, j))],
        out_shape=[jax.ShapeDtypeStruct((S, D_MODEL), F32),
                   jax.ShapeDtypeStruct((_S5_NB, HEAD, _S5_L), F32), jax.ShapeDtypeStruct((_S5_NB, HEAD, _S5_L), F32),
                   jax.ShapeDtypeStruct((_S5_NB, _S5_L, HEAD), F32), jax.ShapeDtypeStruct((_S5_NB, _S5_L, HEAD), F32),
                   jax.ShapeDtypeStruct((_S5_NB, 8, _S5_L), F32), jax.ShapeDtypeStruct((_S5_NB, 8, _S5_L), F32),
                   jax.ShapeDtypeStruct((1, D_MODEL), F32)],
        scratch_shapes=[pltpu.VMEM((T, _S5_L), F32), pltpu.VMEM((T, _S5_L), F32),
                        pltpu.VMEM((1, _S5_L), F32), pltpu.VMEM((1, _S5_L), F32)],
        compiler_params=_cparams(("parallel", "arbitrary")),
    )(dy, u, xre, xim, xre, xim, bre, bim, cre, cim, qw_r, qw_i, dskip)


def _s5_params(a_re, a_im, log_dt, b_re, b_im, c_re, c_im):
    step = jnp.exp(log_dt)[:, None]
    mag = jnp.exp(a_re * step)
    lr, li = mag * jnp.cos(a_im * step), mag * jnp.sin(a_im * step)
    den = a_re * a_re + a_im * a_im
    nr, ni = lr - 1.0, li
    kr, ki = (nr * a_re + ni * a_im) / den, (ni * a_re - nr * a_im) / den
    bbr = kr[..., None] * b_re - ki[..., None] * b_im
    bbi = kr[..., None] * b_im + ki[..., None] * b_re
    eye = jnp.eye(8, dtype=F32)

    def blk_b(bb):
        t = bb.reshape(_S5_NB, 8, 64, 16).transpose(0, 1, 3, 2)
        return (t[:, :, :, None, :] * eye[None, :, None, :, None]).reshape(_S5_NB, HEAD, _S5_L)

    def blk_c(cc):
        t = cc.reshape(_S5_NB, 8, 16, 64).transpose(0, 1, 3, 2)
        return (t[:, :, :, None, :] * eye[None, :, None, :, None]).reshape(_S5_NB, _S5_L, HEAD)

    return (blk_b(bbr), blk_b(bbi), blk_c(c_re), blk_c(c_im),
            lr.reshape(_S5_NB, _S5_L), li.reshape(_S5_NB, _S5_L))


def _s5_powers(lr, li):
    pr, pi = [lr], [li]
    for _ in range(7):
        pr, pi = pr + [pr[-1] * lr - pi[-1] * li], pi + [pr[-1] * li + pi[-1] * lr]
    return jnp.stack(pr, axis=1), jnp.stack(pi, axis=1)


_MESH = pl.DeviceIdType.MESH
_ANY = pl.BlockSpec(memory_space=pl.ANY)


def _place():
    x, y, c = lax.axis_index("x"), lax.axis_index("y"), lax.axis_index("c")
    return x, y, c, [(1 - x, y), (x, 1 - y), (1 - x, 1 - y)]


def _comm_call(body, arrs, out_shapes, n_remote, name):
    n = len(arrs)
    return pl.pallas_call(
        body, name=name,
        in_specs=[_ANY] * n, out_specs=[_ANY] * n, out_shape=out_shapes,
        scratch_shapes=[pltpu.SemaphoreType.DMA((n * n_remote,)), pltpu.SemaphoreType.DMA((n * n_remote,)),
                        pltpu.SemaphoreType.DMA((n,))],
    )(*arrs)


def _half(ref, slot, h):
    if len(ref.shape) == 4:
        return ref.at[slot, h]
    cols = ref.shape[2] // 2
    return ref.at[slot, :, pl.ds(pl.multiple_of(h * cols, 128), cols)]


def _own_slot(shard, me_idx, name, rows=256, cols=512):
    def body(me_ref, in_ref, o_ref):
        o_ref[...] = in_ref[...]

    if shard.ndim == 2:
        R, C = shard.shape
        return pl.pallas_call(
            body, name=name,
            grid_spec=pltpu.PrefetchScalarGridSpec(
                num_scalar_prefetch=1, grid=(C // cols,),
                in_specs=[pl.BlockSpec((R, cols), lambda j, me: (0, j))],
                out_specs=pl.BlockSpec((None, R, cols), lambda j, me: (me[0], 0, j))),
            out_shape=jax.ShapeDtypeStruct((4,) + shard.shape, shard.dtype),
            compiler_params=_cparams(("parallel",)),
        )(me_idx, shard)
    _, half, C = shard.shape
    rows = min(rows, half)
    assert half % rows == 0
    return pl.pallas_call(
        body, name=name,
        grid_spec=pltpu.PrefetchScalarGridSpec(
            num_scalar_prefetch=1, grid=(2, half // rows),
            in_specs=[pl.BlockSpec((None, rows, C), lambda h, i, me: (h, i, 0))],
            out_specs=pl.BlockSpec((None, None, rows, C), lambda h, i, me: (me[0], h, i, 0))),
        out_shape=jax.ShapeDtypeStruct((4,) + shard.shape, shard.dtype),
        compiler_params=_cparams(("parallel", "parallel")),
    )(me_idx, shard)


def _own_rows(w, layer, buf, me_idx, row_off, after, name, rows=512):
    _, r, C = w.shape
    carried = [] if isinstance(buf, tuple) else [buf]
    shape = buf if isinstance(buf, tuple) else buf.shape
    assert r % rows == 0 and row_off % rows == 0

    def body(me_ref, in_ref, *rest):
        rest[-1][...] = in_ref[...].astype(BF16)

    return pl.pallas_call(
        body, name=name,
        grid_spec=pltpu.PrefetchScalarGridSpec(
            num_scalar_prefetch=1, grid=(r // rows,),
            in_specs=[pl.BlockSpec((None, rows, C), lambda i, me: (layer, i, 0))] + [_ANY] * (len(carried) + len(after)),
            out_specs=pl.BlockSpec((None, rows, C), lambda i, me: (me[0], row_off // rows + i, 0))),
        out_shape=jax.ShapeDtypeStruct(shape, BF16),
        input_output_aliases={2: 0} if carried else {},
        compiler_params=_cparams(("parallel",)),
    )(me_idx, w, *carried, *after)


def _pair_split(arrs, name):
    n = len(arrs)

    def body(*refs):
        ins, outs, (ssem, rsem, _) = refs[:n], refs[n:2 * n], refs[2 * n:]
        x, y, c, _ = _place()
        copies = []
        for a in range(n):
            for s in range(4):
                cp = pltpu.make_async_remote_copy(src_ref=_half(ins[a], s, 1 - c), dst_ref=outs[a].at[s],
                                                  send_sem=ssem.at[4 * a + s], recv_sem=rsem.at[4 * a + s],
                                                  device_id=(x, y, 1 - c), device_id_type=_MESH)
                cp.start()
                copies.append(cp)
        for cp in copies:
            cp.wait()

    shapes = [jax.ShapeDtypeStruct((4,) + (a.shape[2:] if a.ndim == 4 else (a.shape[1], a.shape[2] // 2)), a.dtype)
              for a in arrs]
    return _comm_call(body, arrs, shapes, 4, name)


def _pair_swap(arrs, name):
    n = len(arrs)

    def body(*refs):
        ins, outs, (ssem, rsem, _) = refs[:n], refs[n:2 * n], refs[2 * n:]
        x, y, c, _ = _place()
        copies = []
        for a in range(n):
            cp = pltpu.make_async_remote_copy(src_ref=ins[a], dst_ref=outs[a], send_sem=ssem.at[a], recv_sem=rsem.at[a],
                                              device_id=(x, y, 1 - c), device_id_type=_MESH)
            cp.start()
            copies.append(cp)
        for cp in copies:
            cp.wait()

    return _comm_call(body, arrs, [jax.ShapeDtypeStruct(a.shape, a.dtype) for a in arrs], 1, name)


def _pair_sum(full, recv, c_idx, name, rows=256):
    def add(c_ref, a_ref, b_ref, o_ref):
        o_ref[...] = (a_ref[...].astype(F32) + b_ref[...].astype(F32)).astype(o_ref.dtype)

    if full.ndim == 3:
        _, R, cols = recv.shape
        blk = lambda col: pl.BlockSpec((None, R, cols), lambda s, c: (s, 0, col(c)))
        return pl.pallas_call(
            add, name=name,
            grid_spec=pltpu.PrefetchScalarGridSpec(
                num_scalar_prefetch=1, grid=(4,),
                in_specs=[blk(lambda c: c[0]), blk(lambda c: 0)], out_specs=blk(lambda c: 0)),
            out_shape=jax.ShapeDtypeStruct(recv.shape, recv.dtype),
            compiler_params=_cparams(("parallel",)),
        )(c_idx, full, recv)
    _, _, half, C = full.shape
    rows = min(rows, half)
    nb = half // rows
    assert half % rows == 0

    def body(c_ref, a_ref, b_ref, o_ref):
        o_ref[...] = (a_ref[...].astype(F32) + b_ref[...].astype(F32)).astype(o_ref.dtype)

    return pl.pallas_call(
        body, name=name,
        grid_spec=pltpu.PrefetchScalarGridSpec(
            num_scalar_prefetch=1, grid=(4, nb),
            in_specs=[pl.BlockSpec((None, None, rows, C), lambda s, i, c: (s, c[0], i, 0)),
                      pl.BlockSpec((None, rows, C), lambda s, i, c: (s, i, 0))],
            out_specs=pl.BlockSpec((None, rows, C), lambda s, i, c: (s, i, 0))),
        out_shape=jax.ShapeDtypeStruct(recv.shape, recv.dtype),
        compiler_params=_cparams(("parallel", "parallel")),
    )(c_idx, full, recv)


_HBM = pl.BlockSpec(memory_space=pltpu.HBM)
_SEM = pl.BlockSpec(memory_space=pltpu.SEMAPHORE)
_SPLIT = dict(has_side_effects=pltpu.SideEffectType.DATAFLOW_SIDE_EFFECTING)


def _hbm(t):
    return pltpu.with_memory_space_constraint(t, pltpu.HBM)


def _ag_copy(buf_ref, ssem, rsem, j, chip, c, me):
    px, py = chip
    mine = _half(buf_ref, me, c)
    return pltpu.make_async_remote_copy(src_ref=mine, dst_ref=mine, send_sem=ssem.at[j], recv_sem=rsem.at[j],
                                        device_id=(px, py, c), device_id_type=_MESH)


def _ag_start(bufs, after, name):
    n = len(bufs)

    def body(*refs):
        buf_refs, (ssem, rsem), token = refs[:n], refs[n + 1:n + 3], refs[-1]
        x, y, c, chips = _place()
        for a in range(n):
            for j, chip in enumerate(chips):
                _ag_copy(buf_refs[a], ssem, rsem, 3 * a + j, chip, c, 2 * x + y).start()
        token[...] = jnp.zeros_like(token)

    out = pl.pallas_call(
        body, name=name,
        out_shape=(pltpu.SemaphoreType.DMA((3 * n,)), pltpu.SemaphoreType.DMA((3 * n,)),
                   *[pltpu.HBM(b.shape, b.dtype) for b in bufs], jax.ShapeDtypeStruct((8, HEAD), F32)),
        in_specs=(*[_HBM] * n, _ANY), out_specs=(_SEM, _SEM, *[_HBM] * n, pl.BlockSpec(memory_space=pltpu.VMEM)),
        input_output_aliases={a: 2 + a for a in range(n)}, compiler_params=pltpu.CompilerParams(**_SPLIT),
    )(*[_hbm(b) for b in bufs], after)
    return out[0], out[1], list(out[2:2 + n]), out[-1]


def _ag_wait(ssem, rsem, bufs, after, name):
    n = len(bufs)

    def body(*refs):
        buf_refs, ssem, rsem = refs[:n], refs[n], refs[n + 1]
        x, y, c, chips = _place()
        for a in range(n):
            for j, chip in enumerate(chips):
                cp = _ag_copy(buf_refs[a], ssem, rsem, 3 * a + j, chip, c, 2 * x + y)
                cp.wait_send()
                cp.wait_recv()

    after = after if isinstance(after, (tuple, list)) else (after,)
    return list(pl.pallas_call(
        body, name=name, out_shape=tuple(pltpu.HBM(b.shape, b.dtype) for b in bufs),
        in_specs=(*[_HBM] * n, _SEM, _SEM, *[_ANY] * len(after)), out_specs=tuple([_HBM] * n),
        input_output_aliases={a: a for a in range(n)}, compiler_params=pltpu.CompilerParams(**_SPLIT),
    )(*bufs, ssem, rsem, *after))


def _pair_forward(bufs, name):
    n = len(bufs)

    def body(*refs):
        outs, (ssem, rsem) = refs[n:2 * n], refs[2 * n:]
        x, y, c, chips = _place()
        copies = []
        for a in range(n):
            for j, (px, py) in enumerate(chips):
                landed = _half(outs[a], 2 * px + py, c)
                cp = pltpu.make_async_remote_copy(src_ref=landed, dst_ref=landed, send_sem=ssem.at[3 * a + j],
                                                  recv_sem=rsem.at[3 * a + j], device_id=(x, y, 1 - c),
                                                  device_id_type=_MESH)
                cp.start()
                copies.append(cp)
        for cp in copies:
            cp.wait()

    return pl.pallas_call(
        body, name=name, in_specs=[_ANY] * n, out_specs=[_ANY] * n,
        out_shape=[jax.ShapeDtypeStruct(b.shape, b.dtype) for b in bufs],
        input_output_aliases={a: a for a in range(n)},
        scratch_shapes=[pltpu.SemaphoreType.DMA((3 * n,)), pltpu.SemaphoreType.DMA((3 * n,))],
    )(*bufs)


def _split_copy(part_ref, land_ref, ssem, rsem, s, x, y, c):
    return pltpu.make_async_remote_copy(src_ref=_half(part_ref, s, 1 - c), dst_ref=land_ref.at[s], send_sem=ssem.at[s],
                                        recv_sem=rsem.at[s], device_id=(x, y, 1 - c), device_id_type=_MESH)


def _pair_split_start(part, after, name):
    land = jax.ShapeDtypeStruct((4,) + part.shape[2:], part.dtype)

    def body(part_ref, land_ref, after_ref, ssem, rsem, part_thru, land_thru, token):
        x, y, c, _ = _place()
        for s in range(4):
            _split_copy(part_ref, land_ref, ssem, rsem, s, x, y, c).start()
        token[...] = jnp.zeros_like(token)

    return pl.pallas_call(
        body, name=name,
        out_shape=(pltpu.SemaphoreType.DMA((4,)), pltpu.SemaphoreType.DMA((4,)), pltpu.HBM(part.shape, part.dtype),
                   pltpu.HBM(land.shape, land.dtype), jax.ShapeDtypeStruct((8, HEAD), F32)),
        in_specs=(_HBM, _HBM, _ANY), out_specs=(_SEM, _SEM, _HBM, _HBM, pl.BlockSpec(memory_space=pltpu.VMEM)),
        input_output_aliases={0: 2, 1: 3}, compiler_params=pltpu.CompilerParams(**_SPLIT),
    )(_hbm(part), _hbm(lax.empty(land.shape, land.dtype)), after)


def _pair_split_wait(ssem, rsem, part, land, after, name):
    def body(part_ref, land_ref, ssem, rsem, after_ref, part_out, land_out):
        x, y, c, _ = _place()
        for s in range(4):
            cp = _split_copy(part_ref, land_ref, ssem, rsem, s, x, y, c)
            cp.wait_send()
            cp.wait_recv()

    return pl.pallas_call(
        body, name=name, out_shape=(pltpu.HBM(part.shape, part.dtype), pltpu.HBM(land.shape, land.dtype)),
        in_specs=(_HBM, _HBM, _SEM, _SEM, _ANY), out_specs=(_HBM, _HBM),
        input_output_aliases={0: 0, 1: 1}, compiler_params=pltpu.CompilerParams(**_SPLIT),
    )(part, land, ssem, rsem, after)


def _rs_copy(src_ref, land_ref, ssem, rsem, j, chip, c):
    px, py = chip
    return pltpu.make_async_remote_copy(src_ref=src_ref.at[2 * px + py], dst_ref=land_ref.at[j], send_sem=ssem.at[j],
                                        recv_sem=rsem.at[j], device_id=(px, py, c), device_id_type=_MESH)


def _rs_start(part, after, name):
    def body(part_ref, land_ref, after_ref, ssem, rsem, part_thru, land_thru, token):
        x, y, c, chips = _place()
        for j, chip in enumerate(chips):
            _rs_copy(part_ref, land_ref, ssem, rsem, j, chip, c).start()
        token[...] = jnp.zeros_like(token)

    land = jax.ShapeDtypeStruct((3,) + part.shape[1:], part.dtype)
    return pl.pallas_call(
        body, name=name,
        out_shape=(pltpu.SemaphoreType.DMA((3,)), pltpu.SemaphoreType.DMA((3,)), pltpu.HBM(part.shape, part.dtype),
                   pltpu.HBM(land.shape, land.dtype), jax.ShapeDtypeStruct((8, HEAD), F32)),
        in_specs=(_HBM, _HBM, _ANY), out_specs=(_SEM, _SEM, _HBM, _HBM, pl.BlockSpec(memory_space=pltpu.VMEM)),
        input_output_aliases={0: 2, 1: 3}, compiler_params=pltpu.CompilerParams(**_SPLIT),
    )(_hbm(part), _hbm(lax.empty(land.shape, land.dtype)), after)


def _rs_wait(ssem, rsem, part, land, after, name):
    def body(part_ref, land_ref, ssem, rsem, after_ref, part_out, land_out):
        x, y, c, chips = _place()
        for j, chip in enumerate(chips):
            cp = _rs_copy(part_ref, land_ref, ssem, rsem, j, chip, c)
            cp.wait_send()
            cp.wait_recv()

    return pl.pallas_call(
        body, name=name, out_shape=(pltpu.HBM(part.shape, part.dtype), pltpu.HBM(land.shape, land.dtype)),
        in_specs=(_HBM, _HBM, _SEM, _SEM, _ANY), out_specs=(_HBM, _HBM),
        input_output_aliases={0: 0, 1: 1}, compiler_params=pltpu.CompilerParams(**_SPLIT),
    )(part, land, ssem, rsem, after)


def _sum_own_recv(part, land, me_idx, name, rows=256):
    _, half, C = part.shape

    def body(me_ref, own_ref, land_ref, o_ref):
        acc = own_ref[...].astype(F32)
        for s in range(3):
            acc = acc + land_ref[s].astype(F32)
        o_ref[...] = acc

    if half % 8:
        cols = 256
        return pl.pallas_call(
            body, name=name,
            grid_spec=pltpu.PrefetchScalarGridSpec(
                num_scalar_prefetch=1, grid=(C // cols,),
                in_specs=[pl.BlockSpec((None, half, cols), lambda j, me: (me[0], 0, j)),
                          pl.BlockSpec((3, half, cols), lambda j, me: (0, 0, j))],
                out_specs=pl.BlockSpec((half, cols), lambda j, me: (0, j))),
            out_shape=jax.ShapeDtypeStruct((half, C), F32),
            compiler_params=_cparams(("parallel",)),
        )(me_idx, part, land)
    rows = min(rows, half)
    assert half % rows == 0
    return pl.pallas_call(
        body, name=name,
        grid_spec=pltpu.PrefetchScalarGridSpec(
            num_scalar_prefetch=1, grid=(half // rows,),
            in_specs=[pl.BlockSpec((None, rows, C), lambda i, me: (me[0], i, 0)),
                      pl.BlockSpec((3, rows, C), lambda i, me: (0, i, 0))],
            out_specs=pl.BlockSpec((rows, C), lambda i, me: (i, 0))),
        out_shape=jax.ShapeDtypeStruct((half, C), F32),
        compiler_params=_cparams(("parallel",)),
    )(me_idx, part, land)


def _sum_slots(arr, name, rows=128):
    k, R, C = arr.shape
    rows = min(rows, R)
    assert R % rows == 0

    def body(in_ref, o_ref):
        acc = in_ref[0].astype(F32)
        for s in range(1, k):
            acc = acc + in_ref[s].astype(F32)
        o_ref[...] = acc

    return pl.pallas_call(
        body, name=name, grid=(R // rows,),
        in_specs=[pl.BlockSpec((k, rows, C), lambda i: (0, i, 0))],
        out_specs=pl.BlockSpec((rows, C), lambda i: (i, 0)),
        out_shape=jax.ShapeDtypeStruct((R, C), F32),
        compiler_params=_cparams(("parallel",)),
    )(arr)


ADAM_LR, ADAM_B1, ADAM_B2, ADAM_EPS, ADAM_WD, ADAM_STEP = 0.001, 0.9, 0.999, 1e-08, 0.01, 10


def _adam_math(w, g, m, v):
    m = ADAM_B1 * m + (1.0 - ADAM_B1) * g
    v = ADAM_B2 * v + (1.0 - ADAM_B2) * jnp.square(g)
    m_hat = m / (1.0 - ADAM_B1 ** ADAM_STEP)
    v_hat = v / (1.0 - ADAM_B2 ** ADAM_STEP)
    delta = -ADAM_LR * (m_hat / (jnp.sqrt(v_hat) + ADAM_EPS) + ADAM_WD * w)
    return delta, m, v


def _adamw(w, m, v, layer, mine, other, c_idx, name, g_off=0, prev=None, after=(), rows=256):
    _, R, C = w.shape
    half = mine.shape[0]
    rows = min(rows, R)
    assert R % rows == 0 and g_off % rows == 0 and half % rows == 0
    nbh, b0 = half // rows, g_off // rows

    def body(c_ref, w_ref, m_ref, v_ref, mine_ref, other_ref, *rest):
        go, do, mo, vo = rest[-4:]
        in_my_half = (b0 + pl.program_id(0)) // nbh == c_ref[0]
        g = jnp.where(in_my_half, mine_ref[...], other_ref[...])
        delta, m1, v1 = _adam_math(w_ref[...], g, m_ref[...], v_ref[...])
        go[...] = g
        do[...] = delta
        mo[...] = m1
        vo[...] = v1

    blk = pl.BlockSpec((None, rows, C), lambda i, c: (layer, i, 0))

    def gblk(of_mine):
        return pl.BlockSpec((rows, C), lambda i, c: (
            jnp.where(((b0 + i) // nbh == c[0]) == of_mine, (b0 + i) % nbh, 0), 0))

    carried = list(prev) if prev is not None else []
    return pl.pallas_call(
        body, name=name,
        grid_spec=pltpu.PrefetchScalarGridSpec(
            num_scalar_prefetch=1, grid=(R // rows,),
            in_specs=[blk] * 3 + [gblk(True), gblk(False)] + [_ANY] * (len(carried) + len(after)),
            out_specs=[blk] * 4),
        out_shape=[jax.ShapeDtypeStruct(w.shape, F32)] * 4,
        input_output_aliases={6 + k: k for k in range(len(carried))},
        compiler_params=_cparams(("parallel",)),
    )(c_idx, w, m, v, mine, other, *carried, *after)


def _adamw_t(w, m, v, mine, other, c_idx, name, rows=513, cols=128):
    R, _, C = w.shape
    nbh = (C // 2) // cols
    assert R % rows == 0 and (C // 2) % cols == 0

    def body(c_ref, w_ref, m_ref, v_ref, mine_ref, other_ref, go, do, mo, vo):
        in_my_half = pl.program_id(1) // nbh == c_ref[0]
        g = jnp.where(in_my_half, mine_ref[...], other_ref[...])
        delta, m1, v1 = _adam_math(w_ref[...], g, m_ref[...], v_ref[...])
        go[...] = g
        do[...] = delta
        mo[...] = m1
        vo[...] = v1

    blk = pl.BlockSpec((rows, 1, cols), lambda i, j, c: (i, 0, j))
    gblk = pl.BlockSpec((rows, 1, cols), lambda i, j, c: (i, 0, j % nbh))
    return pl.pallas_call(
        body, name=name,
        grid_spec=pltpu.PrefetchScalarGridSpec(
            num_scalar_prefetch=1, grid=(R // rows, C // cols), in_specs=[blk] * 3 + [gblk] * 2,
            out_specs=[blk] * 4),
        out_shape=[jax.ShapeDtypeStruct(w.shape, F32)] * 4,
        compiler_params=_cparams(("parallel", "parallel")),
    )(c_idx, w, m, v, mine, other)


_WEIGHTS = ['norm_mix', 'norm_mlp', 'norm_ple', 'w_in_e', 'w_out_e', 'hgrn_lb', 'g_norm_a', 'conv_w', 'a_log',
            'dt_bias', 'g_norm_b', 's5_a_re', 's5_a_im', 's5_b_re', 's5_b_im', 's5_c_re', 's5_c_im', 's5_d',
            's5_log_dt', 'w_glu', 'b_glu', 'w_out_o', 'w_up', 'w_down', 'w_ple_gate', 'w_ple_proj', 'final_norm']
_INPUTS = ['x', 'p'] + _WEIGHTS + ['loss_target'] + ['m_' + n for n in _WEIGHTS] + ['v_' + n for n in _WEIGHTS]
_FAMILY = [('w_up', 0, 'col', 2048), ('w_down', 0, 'row', 2048), ('w_ple_gate', 0, 'row', 512), ('w_out_e', 0, 'row', 512),
           ('w_up', 1, 'col', 2048), ('w_down', 1, 'row', 2048), ('w_ple_gate', 1, 'row', 512), ('w_glu', 0, 'row', 512),
           ('w_out_o', 0, 'row', 512)]
_PACK = {('w_up', 0): 0, ('w_down', 0): 0, ('w_ple_gate', 0): 0, ('w_out_e', 0): 0,
         ('w_up', 1): 1, ('w_down', 1): 1, ('w_ple_gate', 1): 1, ('w_glu', 0): 1, ('w_out_o', 0): 1}
_IN_PAD = 8320
_IN_BLK = 1664


def _rms_bwd_fn(x, d_hn, d_res, g):
    _, vjp = jax.vjp(_rms, x, g)
    dx, dg = vjp(d_hn.astype(F32))
    return dx + d_res, dg


def _rms_bwd_both(x, d_hn, d_res, g):
    dx, dg = _rms_bwd_fn(x, d_hn, d_res, g)
    return dx, dx, dg


def _add_res(acc, h):
    return (acc + h,)


def _pack_rows(parts, lanes=128, mult=256):
    flat = jnp.concatenate([q.reshape(-1).astype(F32) for q in parts])
    n = flat.shape[0]
    rows = -(-n // (lanes * mult)) * mult
    return jnp.pad(flat, (0, rows * lanes - n)).reshape(rows, lanes)


def _unpack_rows(buf, shapes):
    flat, out, off = buf.reshape(-1), [], 0
    for s in shapes:
        n = math.prod(s)
        out.append(flat[off:off + n].reshape(s))
        off += n
    return out


def _step(a):
    S = a['x'].shape[1]
    x, tgt = a['x'][0], a['loss_target'][0]
    xi, yi = lax.axis_index("x"), lax.axis_index("y")
    me = 2 * xi + yi
    row = lambda t: t.reshape(1, -1)

    me_idx = me.astype(jnp.int32).reshape(1)
    c_idx = lax.axis_index("c").astype(jnp.int32).reshape(1)
    small_sh = jnp.concatenate([a['conv_w'][0].reshape(-1), a['s5_d'][0], a['b_glu'][0]]).reshape(16, 256)
    rows_first = lambda t: jnp.transpose(t, (2, 0, 1))
    shards = [None, None, rows_first(a['w_in_e']).reshape(2052, D_MODEL).astype(BF16),
              a['w_ple_proj'].reshape(512, 512).astype(BF16), small_sh]
    own = lambda i: _own_slot(shards[i] if i == 2 else shards[i].reshape(2, shards[i].shape[0] // 2, shards[i].shape[1]),
                              me_idx, name=f"own_slot{i}")
    whole = lambda g, t: g.reshape((4,) + t.shape)
    first = _ag_start([own(2), own(3), own(4)], me_idx, name="gather_first_start")
    big = {nm for nm, _, _, _ in _FAMILY} | {'w_in_e', 'w_ple_proj'}
    held = ['x'] + [pre + n for n in _WEIGHTS if n not in big for pre in ('', 'm_', 'v_')]
    _, late = lax.optimization_barrier((first[3], {n: a[n] for n in held}))
    a = {**a, **late}
    x = a['x'][0]
    views, pack_of, rt = {}, {}, [0, 0]
    for nm, l, kind, r in _FAMILY:
        views[nm, l], pack_of[nm, l] = (kind, rt[_PACK[nm, l]], r), _PACK[nm, l]
        rt[_PACK[nm, l]] += r
    bufs = [(4, rt[0], PACK_COLS), (4, rt[1], PACK_COLS)]
    for nm, l, kind, r in _FAMILY:
        k = pack_of[nm, l]
        bufs[k] = _own_rows(a[nm], l, bufs[k], me_idx, views[nm, l][1], (first[3],), name=f"own_{nm}{l}")
    shards[0], shards[1] = [jax.ShapeDtypeStruct((rt[k], PACK_COLS), BF16) for k in (0, 1)]
    slots = [bufs[k].reshape(4, 2, rt[k] // 2, PACK_COLS) for k in (0, 1)]
    hn0 = _rowwise(_rms, [x], [row(a['norm_mix'][0])], [(D_MODEL, BF16)], name="rms_mix0")
    s_names = [n for n in _WEIGHTS if n not in big]
    w_pack, m_pack, v_pack = [_pack_rows([a[pre + n] for n in s_names]) for pre in ('', 'm_', 'v_')]
    landed = _pair_forward(_ag_wait(first[0], first[1], first[2], (hn0, bufs[0], bufs[1], w_pack, m_pack, v_pack),
                                    name="gather_first_wait"), name="gather_first_pass")
    g_in, g_pp, g_small = [whole(g, t) for g, t in zip(landed, shards[2:])]
    gw = {}
    W = lambda nm, l: dict(b=gw[pack_of[nm, l]], b_view=views[nm, l])
    sems0 = _ag_start([slots[0]], g_in, name="gather0_start")
    win_t = jnp.pad(g_in.reshape(8208, D_MODEL), ((0, _IN_PAD - 8208), (0, 0)))
    w_pp = g_pp.reshape(4, 2, 256, 512).transpose(1, 2, 0, 3).reshape(2, 256, D_MODEL)
    g_small = g_small.reshape(4, 4096)
    conv_full = g_small[:, :3072].reshape(4, 4, 768).transpose(1, 0, 2).reshape(4, 3072)
    s5_d_full = g_small[:, 3072:3584].reshape(1, D_MODEL)
    b_glu_full = g_small[:, 3584:].reshape(1, D_MODEL)

    lb, lb_vjp = jax.vjp(lambda hl: jax.nn.softmax(hl, axis=0)[0:1], a['hgrn_lb'])
    pad_h = lambda t: jnp.pad(t, ((0, 0), (0, HEAD - t.shape[1])))
    (arow, dtrow), gdn_vjp = jax.vjp(lambda al, dt: (pad_h(-jnp.exp(al)), pad_h(dt)), a['a_log'], a['dt_bias'])
    s5p, s5_vjp = jax.vjp(_s5_params, a['s5_a_re'][0], a['s5_a_im'][0], a['s5_log_dt'][0], a['s5_b_re'][0],
                          a['s5_b_im'][0], a['s5_c_re'][0], a['s5_c_im'][0])
    bre, bim, cre, cim = [t.astype(BF16) for t in s5p[:4]]
    pw_r, pw_i = _s5_powers(s5p[4], s5p[5])
    gna, gnb = a['g_norm_a'], a['g_norm_b']

    def norm_cast(h, g, dt, name):
        return _rowwise(_rms, [h], [row(g)], [(D_MODEL, dt)], name=name)

    def mlp_ple_fwd(h, l):
        hn = norm_cast(h, a['norm_mlp'][l], BF16, f"rms_mlp{l}")
        up, act = _mm(hn, **W('w_up', l), out_dtypes=(BF16, BF16), name=f"up{l}",
                      epilogue=lambda acc: (acc, jnp.square(jnp.maximum(acc, 0.0))))
        h2 = _mm(act, **W('w_down', l), epilogue=_add_res, extras=(h,), name=f"down{l}")
        hnp = norm_cast(h2, a['norm_ple'][l], BF16, f"rms_ple{l}")
        pp = _mm(a['p'][l, 0], w_pp[l], name=f"ple_proj{l}")
        h3, gpre = _mm(hnp, **W('w_ple_gate', l), extras=(h2, pp), out_dtypes=(F32, F32), name=f"ple_gate{l}",
                       bm=512, epilogue=lambda acc, h2, pp: (h2 + jax.nn.sigmoid(acc) * pp, acc))
        return h3, (h, hn, up, act, h2, hnp, pp, gpre)

    proj = _mm(hn0, win_t, dims="nt", bn=_IN_BLK, name="in_proj", after=(sems0[3],))
    o_a, st_a = _hgrn_fwd(proj, lb, gna)
    act_b = _conv_fwd(proj, conv_full)
    o_b, st_b = _gdn_fwd(act_b, proj, arow, dtrow, gnb)
    merged = jnp.concatenate([o_a, o_b], axis=1)
    landed0 = _ag_wait(sems0[0], sems0[1], sems0[2], merged, name="gather0_wait")
    gw[0] = whole(_pair_forward(landed0, name="gather0_pass")[0], shards[0])
    sems1 = _ag_start([slots[1]], gw[0], name="gather1_start")
    h1 = _mm(merged, **W('w_out_e', 0), epilogue=_add_res, extras=(x,), name="out_e", after=(sems1[3],))
    h3, saved0 = mlp_ple_fwd(h1, 0)

    u = norm_cast(h3, a['norm_mix'][1], F32, "rms_mix1")
    y, xre, xim = _s5_fwd(u, bre, bim, cre, cim, pw_r, pw_i, s5_d_full)
    actg = _rowwise(jax.nn.gelu, [y], [], [(D_MODEL, BF16)], name="gelu")
    landed1 = _ag_wait(sems1[0], sems1[1], sems1[2], actg, name="gather1_wait")
    gw[1] = whole(_pair_forward(landed1, name="gather1_pass")[0], shards[1])
    glu, z = _mm(actg, **W('w_glu', 0), extras=(actg, b_glu_full), out_dtypes=(BF16, F32), name="glu",
                 epilogue=lambda acc, act, b: (act.astype(F32) * jax.nn.sigmoid(acc + b), acc + b))
    h4 = _mm(glu, **W('w_out_o', 0), epilogue=_add_res, extras=(h3,), name="out_o")
    h6, saved1 = mlp_ple_fwd(h4, 1)

    def head(h, t, g):
        def f(h, g):
            e = _rms(h, g) - t
            return 0.5 * jnp.sum(jnp.mean(e * e, axis=-1))
        val, vjp = jax.vjp(f, h, g)
        dh, dg = vjp(jnp.ones((), F32))
        return dh, dg, jnp.full((1, HEAD), val)

    dh, d_final, loss_part = _rowwise(head, [h6, tgt], [row(a['final_norm'])], [(D_MODEL, F32)], sums=(D_MODEL, HEAD),
                                      name="loss_head")
    loss = lax.psum(loss_part[0, 0], ("x", "y", "c"))

    gb = {k: lax.empty((4, rt[k], PACK_COLS), BF16) for k in (0, 1)}

    def into(lhs, rhs, key, name, after=()):
        k = pack_of[key]
        gb[k] = _mm(lhs, rhs, dims="tn", out_into=gb[k], out_view=views[key], out_dtypes=(BF16,), name=name, after=after)

    def mlp_ple_bwd(dh, l, saved, after=()):
        h, hn, up, act, h2, hnp, pp, gpre = saved

        def gate_bwd(d, gpre, pp):
            s = jax.nn.sigmoid(gpre)
            return d * s, d * pp * s * (1.0 - s)

        d_pp, d_gp = _rowwise(gate_bwd, [dh, gpre, pp], [], [(D_MODEL, BF16), (D_MODEL, BF16)], name=f"ple_bwd{l}")
        d_wpp = _mm(a['p'][l, 0], d_pp, dims="tn", name=f"d_ple_proj{l}")
        into(hnp, d_gp, ('w_ple_gate', l), f"d_ple_gate{l}")
        d_hnp = _mm(d_gp, **W('w_ple_gate', l), dims="nt", name=f"ple_gate_t{l}", after=after)
        dh2, dh2_b, d_nple = _rowwise(_rms_bwd_both, [h2, d_hnp, dh], [row(a['norm_ple'][l])],
                                      [(D_MODEL, F32), (D_MODEL, BF16)], sums=(D_MODEL,), name=f"rms_ple_bwd{l}")
        d_up = _mm(dh2_b, **W('w_down', l), dims="nt", extras=(up,), out_dtypes=(BF16,), name=f"down_t{l}",
                   epilogue=lambda acc, up: (acc * 2.0 * jnp.maximum(up.astype(F32), 0.0),))
        into(act, dh2_b, ('w_down', l), f"d_down{l}")
        into(hn, d_up, ('w_up', l), f"d_up{l}")
        d_hn = _mm(d_up, **W('w_up', l), dims="nt", name=f"up_t{l}")
        dh1, dh1_b, d_nmlp = _rowwise(_rms_bwd_both, [h, d_hn, dh2], [row(a['norm_mlp'][l])],
                                      [(D_MODEL, F32), (D_MODEL, BF16)], sums=(D_MODEL,), name=f"rms_mlp_bwd{l}")
        return dh1, dh1_b, d_wpp, d_nple, d_nmlp

    dh4, dh4_b, d_wpp1, d_nple1, d_nmlp1 = mlp_ple_bwd(dh, 1, saved1)

    d_glu = _mm(dh4_b, **W('w_out_o', 0), dims="nt", name="out_o_t")
    into(glu, dh4_b, ('w_out_o', 0), "d_out_o")

    def glu_bwd(d, z, act):
        s, act = jax.nn.sigmoid(z), act.astype(F32)
        dz = d * act * s * (1.0 - s)
        return dz, d * s, jnp.sum(dz, axis=0, keepdims=True)

    d_z, d_actp, d_bglu = _rowwise(glu_bwd, [d_glu, z, actg], [], [(D_MODEL, BF16), (D_MODEL, F32)], sums=(D_MODEL,),
                                   name="glu_bwd")
    into(actg, d_z, ('w_glu', 0), "d_glu")

    def reduce_start(bufs, tag):
        parts = [t if t.shape[1] % 32 else t.reshape(4, 2, t.shape[1] // 2, t.shape[2]) for t in bufs]
        recv = _pair_split(parts, name=f"pair_split{tag}")
        sums = [_pair_sum(f, r, c_idx, name=f"pair_sum{tag}_{i}") for i, (f, r) in enumerate(zip(parts, recv))]
        started, prev = [], c_idx
        for i, s in enumerate(sums):
            started.append(_rs_start(s, prev, name=f"scatter{tag}_{i}_start"))
            prev = started[-1][4]
        return started

    def pack_split(k):
        part = gb[k].reshape(4, 2, rt[k] // 2, PACK_COLS)
        return _pair_split_start(part, c_idx, name=f"pair_split{k}_start")

    def pack_scatter(k, split, after):
        part, recv = _pair_split_wait(split[0], split[1], split[2], split[3], after, name=f"pair_split{k}_wait")
        return [_rs_start(_pair_sum(part, recv, c_idx, name=f"pair_sum{k}_0"), c_idx, name=f"scatter{k}_0_start")]

    def reduce_finish(started, after, tag):
        halves = []
        for i, (ssem, rsem, part, land, _) in enumerate(started):
            part, land = _rs_wait(ssem, rsem, part, land, after, name=f"scatter{tag}_{i}_wait")
            halves.append(_sum_own_recv(part, land, me_idx, name=f"sum_partials{tag}_{i}"))
        return list(zip(halves, _pair_swap(halves, name=f"swap_halves{tag}")))

    split1 = pack_split(1)

    def gelu_bwd(acc, dap, y):
        _, vjp = jax.vjp(jax.nn.gelu, y)
        return vjp(acc + dap)

    dy = _mm(d_z, **W('w_glu', 0), dims="nt", extras=(d_actp, y), epilogue=gelu_bwd, name="glu_t",
             after=(split1[4],))
    du, d_bre, d_bim, d_cre, d_cim, d_lr, d_li, d_s5d = _s5_bwd(dy, u, xre, xim, bre, bim, cre, cim,
                                                                 pw_r[:, ::-1], -pw_i[:, ::-1], s5_d_full)
    dh3, d_nmix1 = _rowwise(_rms_bwd_fn, [h3, du, dh4], [row(a['norm_mix'][1])], [(D_MODEL, F32)], sums=(D_MODEL,),
                            name="rms_mix1_bwd")
    d_are, d_aim, d_logdt, d_sbre, d_sbim, d_scre, d_scim = s5_vjp(
        (d_bre, d_bim, d_cre, d_cim, d_lr.sum(axis=1), d_li.sum(axis=1)))

    started1 = pack_scatter(1, split1, dh3)
    dh1, dh1_b, d_wpp0, d_nple0, d_nmlp0 = mlp_ple_bwd(dh3, 0, saved0, after=(started1[0][4],))

    d_merged = _mm(dh1_b, **W('w_out_e', 0), dims="nt", name="out_e_t")
    into(merged, dh1_b, ('w_out_e', 0), "d_out_e")
    split0 = pack_split(0)
    dq, df, dv, dg, d_lb, d_gna = _hgrn_bwd(proj, lb, gna, st_a, d_merged, after=(split0[4],))
    started0 = pack_scatter(0, split0, dq)
    dqb, dkb, dvb, dzb, d_tail, d_arow, d_dtrow, d_gnb = _gdn_bwd(act_b, proj, arow, dtrow, gnb, st_b, d_merged,
                                                                  after=(started0[0][4],))
    d_pre, d_conv = _conv_bwd(proj, conv_full, dqb, dkb, dvb)
    d_proj = jnp.concatenate([dq, df, dv, dg, d_pre, dzb, d_tail], axis=1)
    d_win_t = _mm(d_proj, hn0, dims="tn", bm=_IN_BLK, out_dtypes=(BF16,), name="d_in_proj")
    d_hn0 = _mm(d_proj, win_t, bk=_IN_BLK, name="in_proj_t")
    grad_x, d_nmix0 = _rowwise(_rms_bwd_fn, [x, d_hn0, dh1], [row(a['norm_mix'][0])], [(D_MODEL, F32)], sums=(D_MODEL,),
                               name="rms_mix0_bwd")
    (d_hlb,) = lb_vjp(d_lb.reshape(1, 1024))
    d_alog, d_dtb = gdn_vjp((d_arow, d_dtrow))

    d_win_sh = d_win_t[:8208].reshape(4, 2052, D_MODEL)
    d_wpp_sh = jnp.stack([d_wpp0, d_wpp1]).astype(BF16).reshape(2, 256, 4, 512).transpose(2, 0, 1, 3).reshape(4, 512, 512)
    started2 = reduce_start([d_win_sh, d_wpp_sh], 2)
    small = {
        'norm_mix': jnp.concatenate([d_nmix0, d_nmix1]), 'norm_mlp': jnp.concatenate([d_nmlp0, d_nmlp1]),
        'norm_ple': jnp.concatenate([d_nple0, d_nple1]), 'hgrn_lb': d_hlb, 'g_norm_a': d_gna, 'conv_w': d_conv,
        'a_log': d_alog, 'dt_bias': d_dtb, 'g_norm_b': d_gnb, 's5_a_re': d_are, 's5_a_im': d_aim, 's5_b_re': d_sbre,
        's5_b_im': d_sbim, 's5_c_re': d_scre, 's5_c_im': d_scim, 's5_d': d_s5d, 's5_log_dt': d_logdt,
        'b_glu': d_bglu, 'final_norm': d_final}
    s_names = list(small)
    s_shapes = [tuple(small[n].shape) for n in s_names]
    mine_small = _pack_rows([small[n] for n in s_names])
    chip_small = _rowwise(lambda p, o: p + o, [mine_small, _pair_swap([mine_small], name="swap_small")[0]], [],
                          [(128, F32)], name="add_small")
    n_small = chip_small.shape[0]
    small_go = _ag_start([_own_slot(chip_small.reshape(2, n_small // 2, 128), me_idx, name="own_slot_small")],
                         started2[-1][4], name="gather_small_start")
    red = {1: reduce_finish(started1, small_go[3], 1)[0]}
    red[0] = reduce_finish(started0, small_go[3], 0)[0]
    res, order = {}, []

    def adam_layer(nm, layer, key, g_off=None):
        g_off = views[nm, layer][1] if g_off is None else g_off
        prev = res.get(nm)
        follows = () if not order or (prev is not None and order[-1] is prev[0]) else (order[-1],)
        res[nm] = tuple(_adamw(a[nm], a['m_' + nm], a['v_' + nm], layer, *red[key], c_idx, name=f"adam_{nm}{layer}",
                               g_off=g_off, prev=prev, after=follows))
        order.append(res[nm][0])

    for nm, layer in (('w_glu', 0), ('w_out_o', 0), ('w_up', 1), ('w_down', 1), ('w_ple_gate', 1)):
        adam_layer(nm, layer, 1)
    for nm in ('w_up', 'w_down', 'w_ple_gate', 'w_out_e'):
        adam_layer(nm, 0, 0)

    small_all = _pair_forward(_ag_wait(small_go[0], small_go[1], small_go[2], order[-1], name="gather_small_wait"),
                              name="gather_small_pass")[0]
    reduced = _sum_slots(small_all.reshape(4, n_small, 128), name="sum_small")
    sg = dict(zip(s_names, _unpack_rows(reduced, s_shapes)))
    sg['conv_w'] = lax.dynamic_slice_in_dim(sg['conv_w'], me * 768, 768, axis=1)
    sg['s5_d'] = lax.dynamic_slice_in_dim(sg['s5_d'], me * 512, 512, axis=1)
    sg['b_glu'] = lax.dynamic_slice_in_dim(sg['b_glu'], me * 512, 512, axis=1)
    sg = {n: sg[n].reshape(a[n].shape) for n in s_names}
    g_pack = _pack_rows([sg[n] for n in s_names])
    sd, sm, sv = _rowwise(_adam_math, [w_pack, g_pack, m_pack, v_pack], [], [(128, F32)] * 3, name="adam_small")
    w_shapes = [tuple(a[n].shape) for n in s_names]
    res.update({n: (sg[n], d_, m_, v_) for n, d_, m_, v_ in zip(s_names, _unpack_rows(sd, w_shapes),
                                                               _unpack_rows(sm, w_shapes), _unpack_rows(sv, w_shapes))})

    red['in'], red['pp'] = reduce_finish(started2, sd, 2)
    res['w_in_e'] = tuple(jnp.transpose(o, (1, 2, 0)) for o in _adamw_t(
        rows_first(a['w_in_e']), rows_first(a['m_w_in_e']), rows_first(a['v_w_in_e']),
        *[h.reshape(2052, 1, D_MODEL // 2) for h in red['in']], c_idx, name="adam_w_in_e"))
    adam_layer('w_ple_proj', 0, 'pp', 0)
    adam_layer('w_ple_proj', 1, 'pp', 256)

    return (loss, grad_x[None], *[res[n][0] for n in _WEIGHTS], *[res[n][1] for n in _WEIGHTS],
            *[res[n][2] for n in _WEIGHTS], *[res[n][3] for n in _WEIGHTS])


def kernel(x, p, norm_mix, norm_mlp, norm_ple, w_in_e, w_out_e, hgrn_lb, g_norm_a, conv_w, a_log, dt_bias, g_norm_b, s5_a_re, s5_a_im, s5_b_re, s5_b_im, s5_c_re, s5_c_im, s5_d, s5_log_dt, w_glu, b_glu, w_out_o, w_up, w_down, w_ple_gate, w_ple_proj, final_norm, loss_target, m_norm_mix, m_norm_mlp, m_norm_ple, m_w_in_e, m_w_out_e, m_hgrn_lb, m_g_norm_a, m_conv_w, m_a_log, m_dt_bias, m_g_norm_b, m_s5_a_re, m_s5_a_im, m_s5_b_re, m_s5_b_im, m_s5_c_re, m_s5_c_im, m_s5_d, m_s5_log_dt, m_w_glu, m_b_glu, m_w_out_o, m_w_up, m_w_down, m_w_ple_gate, m_w_ple_proj, m_final_norm, v_norm_mix, v_norm_mlp, v_norm_ple, v_w_in_e, v_w_out_e, v_hgrn_lb, v_g_norm_a, v_conv_w, v_a_log, v_dt_bias, v_g_norm_b, v_s5_a_re, v_s5_a_im, v_s5_b_re, v_s5_b_im, v_s5_c_re, v_s5_c_im, v_s5_d, v_s5_log_dt, v_w_glu, v_b_glu, v_w_out_o, v_w_up, v_w_down, v_w_ple_gate, v_w_ple_proj, v_final_norm):
    args = (x, p, norm_mix, norm_mlp, norm_ple, w_in_e, w_out_e, hgrn_lb, g_norm_a, conv_w, a_log, dt_bias, g_norm_b, s5_a_re, s5_a_im, s5_b_re, s5_b_im, s5_c_re, s5_c_im, s5_d, s5_log_dt, w_glu, b_glu, w_out_o, w_up, w_down, w_ple_gate, w_ple_proj, final_norm, loss_target, m_norm_mix, m_norm_mlp, m_norm_ple, m_w_in_e, m_w_out_e, m_hgrn_lb, m_g_norm_a, m_conv_w, m_a_log, m_dt_bias, m_g_norm_b, m_s5_a_re, m_s5_a_im, m_s5_b_re, m_s5_b_im, m_s5_c_re, m_s5_c_im, m_s5_d, m_s5_log_dt, m_w_glu, m_b_glu, m_w_out_o, m_w_up, m_w_down, m_w_ple_gate, m_w_ple_proj, m_final_norm, v_norm_mix, v_norm_mlp, v_norm_ple, v_w_in_e, v_w_out_e, v_hgrn_lb, v_g_norm_a, v_conv_w, v_a_log, v_dt_bias, v_g_norm_b, v_s5_a_re, v_s5_a_im, v_s5_b_re, v_s5_b_im, v_s5_c_re, v_s5_c_im, v_s5_d, v_s5_log_dt, v_w_glu, v_b_glu, v_w_out_o, v_w_up, v_w_down, v_w_ple_gate, v_w_ple_proj, v_final_norm)
    return _step(dict(zip(_INPUTS, args)))
```

```python
import functools
import math

import jax
import jax.numpy as jnp
from jax import lax
from jax.experimental import pallas as pl
from jax.experimental.pallas import tpu as pltpu

F32 = jnp.float32
BF16 = jnp.bfloat16

D_MODEL = 2048
SEQ = 4096
NORM_EPS = 1e-6
CHUNK = 64
HEAD = 128
VMEM_LIMIT = 56 * 1024 * 1024


_NN = (((1,), (0,)), ((), ()))
_NT = (((1,), (1,)), ((), ()))
_TN = (((0,), (0,)), ((), ()))
_HI = lax.Precision.HIGHEST
_NEG = -1e30


def _cparams(sem, **kw):
    return pltpu.CompilerParams(dimension_semantics=sem, vmem_limit_bytes=VMEM_LIMIT, **kw)


PACK_COLS = 2048


def _view_shape(view):
    kind, _, r = view
    return (4 * r, PACK_COLS) if kind == "row" else (r, 4 * PACK_COLS)


def _view_spec(view, rb, cb, row_of, col_of):
    kind, off, r = view
    assert off % rb == 0 and r % rb == 0 and PACK_COLS % cb == 0, (view, rb, cb)
    if kind == "row":
        nrb = r // rb
        return pl.BlockSpec((None, rb, cb), lambda i, j, k: (row_of(i, j, k) // nrb,
                                                             off // rb + row_of(i, j, k) % nrb, col_of(i, j, k)))
    ncb = PACK_COLS // cb
    return pl.BlockSpec((None, rb, cb), lambda i, j, k: (col_of(i, j, k) // ncb,
                                                         off // rb + row_of(i, j, k), col_of(i, j, k) % ncb))


def _mm(a, b, *, dims="nn", epilogue=None, extras=(), out_dtypes=(F32,), bm=1024, bn=1024, bk=2048, name,
        b_view=None, out_into=None, out_view=None, after=()):
    b_shape = _view_shape(b_view) if b_view is not None else b.shape
    if dims == "tn":
        (K, M), (K2, N) = a.shape, b_shape
    elif dims == "nt":
        (M, K), (N, K2) = a.shape, b_shape
    else:
        (M, K), (K2, N) = a.shape, b_shape
    assert K == K2, (a.shape, b_shape, dims)
    if b_view is not None and dims == "nt":
        bn = min(bn, b_view[2])
    if b_view is not None and dims != "nt":
        bk = min(bk, b_view[2])
    if out_view is not None:
        bm = min(bm, out_view[2])
    bm, bn, bk = min(bm, M), min(bn, N), min(bk, K)
    assert M % bm == 0 and N % bn == 0 and K % bk == 0, (M, N, K, bm, bn, bk)
    nk = K // bk
    ii, jj, kk = (lambda i, j, k: i), (lambda i, j, k: j), (lambda i, j, k: k)
    if dims == "tn":
        a_spec = pl.BlockSpec((bk, bm), lambda i, j, k: (k, i))
        dn = _TN
    else:
        a_spec = pl.BlockSpec((bm, bk), lambda i, j, k: (i, k))
        dn = _NT if dims == "nt" else _NN
    if dims == "nt":
        b_spec = _view_spec(b_view, bn, bk, jj, kk) if b_view else pl.BlockSpec((bn, bk), lambda i, j, k: (j, k))
    else:
        b_spec = _view_spec(b_view, bk, bn, kk, jj) if b_view else pl.BlockSpec((bk, bn), lambda i, j, k: (k, j))
    e_specs = []
    for e in extras:
        if e.shape == (M, N):
            e_specs.append(pl.BlockSpec((bm, bn), lambda i, j, k: (i, j)))
        else:
            assert e.shape == (1, N), e.shape
            e_specs.append(pl.BlockSpec((1, bn), lambda i, j, k: (0, j)))
    ne, no = len(extras), len(out_dtypes)
    if epilogue is None:
        epilogue = lambda acc: (acc,)
    into = out_into is not None

    def body(a_ref, b_ref, *rest):
        e_refs, rest = rest[:ne], rest[ne + (1 if into else 0) + len(after):]
        o_refs = rest[:no]
        part = lax.dot_general(a_ref[...].astype(BF16), b_ref[...].astype(BF16), dn, preferred_element_type=F32)

        def finish(total):
            outs = epilogue(total, *[e[...] for e in e_refs])
            for o, v in zip(o_refs, outs):
                o[...] = v.astype(o.dtype)

        if nk == 1:
            finish(part)
            return
        acc, k = rest[no], pl.program_id(2)

        @pl.when(k == 0)
        def _():
            acc[...] = part

        @pl.when((k > 0) & (k < nk - 1))
        def _():
            acc[...] += part

        @pl.when(k == nk - 1)
        def _():
            finish(acc[...] + part)

    if into:
        assert no == 1 and _view_shape(out_view) == (M, N), (out_view, M, N)
        out_specs = [_view_spec(out_view, bm, bn, ii, jj)]
        out_shape = [jax.ShapeDtypeStruct(out_into.shape, out_into.dtype)]
        extra_in, extra_specs, alias = [out_into], [pl.BlockSpec(memory_space=pl.ANY)], {2 + ne: 0}
    else:
        out_specs = [pl.BlockSpec((bm, bn), lambda i, j, k: (i, j)) for _ in out_dtypes]
        out_shape = [jax.ShapeDtypeStruct((M, N), dt) for dt in out_dtypes]
        extra_in, extra_specs, alias = [], [], {}
    outs = pl.pallas_call(
        body, name=name,
        grid=(M // bm, N // bn, nk),
        in_specs=[a_spec, b_spec] + e_specs + extra_specs + [pl.BlockSpec(memory_space=pl.ANY)] * len(after),
        out_specs=out_specs, out_shape=out_shape,
        scratch_shapes=[pltpu.VMEM((bm, bn), F32)] if nk > 1 else [],
        input_output_aliases=alias,
        compiler_params=_cparams(("parallel", "parallel", "arbitrary")),
    )(a, b, *extras, *extra_in, *after)
    return outs[0] if no == 1 else tuple(outs)


def _rowwise(fn, tiled, bcast, outs, sums=(), *, rows=512, name):
    S = tiled[0].shape[0]
    rows = min(rows, S)
    assert S % rows == 0
    nt, nb, no, ns = len(tiled), len(bcast), len(outs), len(sums)

    def body(*refs):
        t_refs, b_refs = refs[:nt], refs[nt:nt + nb]
        o_refs, s_refs = refs[nt + nb:nt + nb + no], refs[nt + nb + no:]
        res = fn(*[r[...] for r in t_refs], *[r[...] for r in b_refs])
        res = res if isinstance(res, (tuple, list)) else (res,)
        for o, v in zip(o_refs, res[:no]):
            o[...] = v.astype(o.dtype)
        if ns:
            @pl.when(pl.program_id(0) == 0)
            def _():
                for s in s_refs:
                    s[...] = jnp.zeros_like(s)
            for s, v in zip(s_refs, res[no:]):
                s[...] += v

    res = pl.pallas_call(
        body, name=name,
        grid=(S // rows,),
        in_specs=[pl.BlockSpec((rows, t.shape[1]), lambda i: (i, 0)) for t in tiled]
        + [pl.BlockSpec(b.shape, lambda i, nd=b.ndim: (0,) * nd) for b in bcast],
        out_specs=[pl.BlockSpec((rows, c), lambda i: (i, 0)) for c, _ in outs]
        + [pl.BlockSpec((1, c), lambda i: (0, 0)) for c in sums],
        out_shape=[jax.ShapeDtypeStruct((S, c), dt) for c, dt in outs]
        + [jax.ShapeDtypeStruct((1, c), F32) for c in sums],
        compiler_params=_cparams(("arbitrary",)),
    )(*tiled, *bcast)
    return res[0] if len(res) == 1 else tuple(res)


def _rms(x, g):
    return x * lax.rsqrt(jnp.mean(x * x, axis=-1, keepdims=True) + NORM_EPS) * g


def _mxu(a, b, dn):
    if a.ndim == 3:
        ((ca,), (cb,)), _ = dn
        dn = (((ca + 1,), (cb + 1,)), ((0,), (0,)))
    return lax.dot_general(a, b, dn, preferred_element_type=F32)


def _split(a):
    hi = a.astype(BF16)
    return hi, (a - hi.astype(F32)).astype(BF16)


def _passes(a, b, dn, three):
    if not three:
        return _mxu(a.astype(BF16), b.astype(BF16), dn)
    (ah, al), (bh, bl) = _split(a), _split(b)
    return _mxu(ah, bh, dn) + (_mxu(ah, bl, dn) + _mxu(al, bh, dn))


def _dot_grads(a, b, g, dn, three):
    if dn == _NN:
        return _passes(g, b, _NT, three), _passes(a, g, _TN, three)
    if dn == _NT:
        return _passes(g, b, _NN, three), _passes(g, a, _TN, three)
    assert dn == _TN
    return _passes(b, g, _NT, three), _passes(a, g, _NN, three)


@functools.partial(jax.custom_vjp, nondiff_argnums=(2,))
def _dot(a, b, dn=_NN):
    return _passes(a, b, dn, False)


_dot.defvjp(lambda a, b, dn: (_passes(a, b, dn, False), (a, b)),
            lambda dn, res, g: _dot_grads(res[0], res[1], g, dn, False))


@functools.partial(jax.custom_vjp, nondiff_argnums=(2,))
def _dot3(a, b, dn=_NN):
    return _passes(a, b, dn, True)


_dot3.defvjp(lambda a, b, dn: (_passes(a, b, dn, True), (a, b)),
             lambda dn, res, g: _dot_grads(res[0], res[1], g, dn, True))


def _tri_times(x, dn):
    tri = _tril(x.shape[-2]).astype(BF16)
    if x.ndim == 3:
        tri = jnp.broadcast_to(tri, (x.shape[0],) + tri.shape)
    hi, lo = _split(x)
    lo2 = (x - hi.astype(F32) - lo.astype(F32)).astype(BF16)
    return _mxu(tri, hi, dn) + (_mxu(tri, lo, dn) + _mxu(tri, lo2, dn))


@jax.custom_vjp
def _cumsum_rows(x):
    return _tri_times(x, _NN)


_cumsum_rows.defvjp(lambda x: (_tri_times(x, _NN), None), lambda _, g: (_tri_times(g, _TN),))


def _iota(shape, dim):
    return lax.broadcasted_iota(jnp.int32, shape, dim)


def _tril(n, strict=False):
    r, c = _iota((n, n), 0), _iota((n, n), 1)
    return (r > c) if strict else (r >= c)


@functools.partial(jax.custom_vjp, nondiff_argnums=(1,))
def _roll_rows(x, r):
    return pltpu.roll(x, r, 0)


def _roll_rows_fwd(x, r):
    return pltpu.roll(x, r, 0), None


def _roll_rows_bwd(r, _, g):
    return (pltpu.roll(g, (g.shape[0] - r) % g.shape[0], 0),)


_roll_rows.defvjp(_roll_rows_fwd, _roll_rows_bwd)


def _head_norm_gate(o, gn, gate):
    return _rms(o, gn) * jax.nn.silu(gate)


_SUB = 16
_HPS = 8


def _hgrn_chunk(q, fpre, v, gate, lb, gna, st):
    c = q.shape[0]
    forget = lb + (1.0 - lb) * jax.nn.sigmoid(fpre)
    k = 1.0 - forget
    logf = jnp.log(forget)
    cum = _cumsum_rows(logf)
    cum_end = jnp.sum(logf, axis=0, keepdims=True)
    o = _dot(q * jnp.exp(cum), st, _NT)
    st_new = st * jnp.exp(cum_end) + _dot(v, k * jnp.exp(cum_end - cum), _TN)
    t = _iota((c, 1), 0)
    s_off = jnp.zeros((c, c), F32)
    for i in range(1, c // _SUB):
        before = t < i * _SUB
        c_i = jnp.sum(jnp.where(before, logf, 0.0), axis=0, keepdims=True)
        in_blk = (t >= i * _SUB) & (t < (i + 1) * _SUB)
        qi = jnp.where(in_blk, q * jnp.exp(jnp.minimum(cum - c_i, 0.0)), 0.0)
        ki = jnp.where(before, k * jnp.exp(jnp.minimum(c_i - cum, 0.0)), 0.0)
        s_off = s_off + _dot(qi, ki, _NT)
    o = o + _dot(s_off, v)
    tmod = t % _SUB
    for r in range(_SUB):
        kr, cr, vr = (k, cum, v) if r == 0 else (_roll_rows(k, r), _roll_rows(cum, r), _roll_rows(v, r))
        w = q * kr * jnp.exp(jnp.where(tmod >= r, cum - cr, _NEG))
        o = o + jnp.sum(w, axis=1, keepdims=True) * vr
    return _head_norm_gate(o, gna, gate), st_new


def _hgrn_fwd(proj, lb, gna):
    S = proj.shape[0]
    n_chunks, heads = S // CHUNK, 8

    def body(q_ref, f_ref, v_ref, g_ref, lb_ref, gna_ref, o_ref, st_out, st):
        n, hb = pl.program_id(0), pl.program_id(1)

        @pl.when(n == 0)
        def _():
            st[hb] = jnp.zeros((_HPS, HEAD, HEAD), F32)

        sls = [slice(hh * HEAD, (hh + 1) * HEAD) for hh in range(_HPS)]
        st0 = [st[hb, hh] for hh in range(_HPS)]
        ins = [(q_ref[:, sl], f_ref[:, sl], v_ref[:, sl], g_ref[:, sl], lb_ref[:, sl]) for sl in sls]
        gna = gna_ref[...]
        res = [_hgrn_chunk(*ins[hh], gna, st0[hh]) for hh in range(_HPS)]
        for hh, sl in enumerate(sls):
            st_out[hh] = st0[hh]
            o_ref[:, sl] = res[hh][0].astype(o_ref.dtype)
            st[hb, hh] = res[hh][1]

    wide = _HPS * HEAD
    sec = lambda s: pl.BlockSpec((CHUNK, wide), lambda n, h, s=s: (n, (8 // _HPS) * s + h))
    return pl.pallas_call(
        body, name="hgrn_fwd", grid=(n_chunks, heads // _HPS),
        in_specs=[sec(0), sec(1), sec(2), sec(3),
                  pl.BlockSpec((1, wide), lambda n, h: (0, h)), pl.BlockSpec((1, HEAD), lambda n, h: (0, 0))],
        out_specs=[pl.BlockSpec((CHUNK, wide), lambda n, h: (n, h)),
                   pl.BlockSpec((None, _HPS, HEAD, HEAD), lambda n, h: (n, h, 0, 0))],
        out_shape=[jax.ShapeDtypeStruct((S, 1024), BF16),
                   jax.ShapeDtypeStruct((n_chunks, heads, HEAD, HEAD), F32)],
        scratch_shapes=[pltpu.VMEM((heads // _HPS, _HPS, HEAD, HEAD), F32)],
        compiler_params=_cparams(("arbitrary", "arbitrary")),
    )(proj, proj, proj, proj, lb, gna)


def _hgrn_bwd(proj, lb, gna, states, d_o, after=()):
    S = proj.shape[0]
    n_chunks, heads = S // CHUNK, 8

    def body(q_ref, f_ref, v_ref, g_ref, lb_ref, gna_ref, st_ref, do_ref, *rest):
        dq_ref, df_ref, dv_ref, dg_ref, dlb_ref, dgna_ref, dst = rest[len(after):]
        n, hb = pl.program_id(0), pl.program_id(1)

        @pl.when(n == 0)
        def _():
            dst[hb] = jnp.zeros((_HPS, HEAD, HEAD), F32)
            dlb_ref[hb] = jnp.zeros((_HPS, 1, HEAD), F32)

        @pl.when((n == 0) & (hb == 0))
        def _():
            dgna_ref[...] = jnp.zeros_like(dgna_ref)

        sls = [slice(hh * HEAD, (hh + 1) * HEAD) for hh in range(_HPS)]
        gna = gna_ref[...]
        ins = [(q_ref[:, sl], f_ref[:, sl], v_ref[:, sl], g_ref[:, sl], lb_ref[:, sl], gna, st_ref[hh])
               for hh, sl in enumerate(sls)]
        cts = [(do_ref[:, sl], dst[hb, hh]) for hh, sl in enumerate(sls)]
        grads = [jax.vjp(_hgrn_chunk, *ins[hh])[1](cts[hh]) for hh in range(_HPS)]
        dgna_sum = jnp.zeros((1, HEAD), F32)
        for hh, sl in enumerate(sls):
            dq, df, dv, dg, dlb, dgna, dst0 = grads[hh]
            dq_ref[:, sl] = dq.astype(dq_ref.dtype)
            df_ref[:, sl] = df.astype(df_ref.dtype)
            dv_ref[:, sl] = dv.astype(dv_ref.dtype)
            dg_ref[:, sl] = dg.astype(dg_ref.dtype)
            dlb_ref[hb, hh] += dlb
            dgna_sum = dgna_sum + dgna
            dst[hb, hh] = dst0
        dgna_ref[...] += dgna_sum

    rev = lambda n: n_chunks - 1 - n
    wide = _HPS * HEAD
    sec = lambda s: pl.BlockSpec((CHUNK, wide), lambda n, h, s=s: (rev(n), (8 // _HPS) * s + h))
    out = pl.BlockSpec((CHUNK, wide), lambda n, h: (rev(n), h))
    return pl.pallas_call(
        body, name="hgrn_bwd", grid=(n_chunks, heads // _HPS),
        in_specs=[sec(0), sec(1), sec(2), sec(3),
                  pl.BlockSpec((1, wide), lambda n, h: (0, h)), pl.BlockSpec((1, HEAD), lambda n, h: (0, 0)),
                  pl.BlockSpec((None, _HPS, HEAD, HEAD), lambda n, h: (rev(n), h, 0, 0)),
                  pl.BlockSpec((CHUNK, wide), lambda n, h: (rev(n), h))] + [pl.BlockSpec(memory_space=pl.ANY)] * len(after),
        out_specs=[out, out, out, out,
                   pl.BlockSpec((heads // _HPS, _HPS, 1, HEAD), lambda n, h: (0, 0, 0, 0)),
                   pl.BlockSpec((1, HEAD), lambda n, h: (0, 0))],
        out_shape=[jax.ShapeDtypeStruct((S, 1024), BF16)] * 4
        + [jax.ShapeDtypeStruct((heads // _HPS, _HPS, 1, HEAD), F32), jax.ShapeDtypeStruct((1, HEAD), F32)],
        scratch_shapes=[pltpu.VMEM((heads // _HPS, _HPS, HEAD, HEAD), F32)],
        compiler_params=_cparams(("arbitrary", "arbitrary")),
    )(proj, proj, proj, proj, lb, gna, states, d_o, *after)


def _l2n(t):
    return t * lax.rsqrt(jnp.sum(t * t, axis=-1, keepdims=True) + NORM_EPS)


def _unit_lower_inverse(low):
    c = low.shape[-1]
    inv = (_iota((c, c), 0) == _iota((c, c), 1)).astype(F32) - low
    p = low
    span = 2
    while span < c:
        p = _dot3(p, p)
        inv = inv + _dot3(inv, p)
        span *= 2
    return inv


def _gdn_chunk(qc, kc, v, z, tail, h0, arow, dtrow, gnb, st):
    nb, c = qc.shape[0], qc.shape[1]
    head = h0 + _iota((nb, c, HEAD), 0)
    lane = _iota((nb, c, HEAD), 2)
    la_all = arow * jax.nn.softplus(tail + dtrow)
    la = jnp.sum(jnp.where(lane == head, la_all[None], 0.0), axis=2, keepdims=True)
    beta = jnp.sum(jnp.where(lane == head + 8, jax.nn.sigmoid(tail)[None], 0.0), axis=2, keepdims=True)
    la_b = jnp.broadcast_to(la, (nb, c, HEAD))
    cum = _cumsum_rows(la_b)
    cmat = _cumsum_rows(jnp.broadcast_to(la, (nb, c, c)))
    cum_end = jnp.sum(la_b, axis=1, keepdims=True)
    decay = jnp.exp(jnp.where(_tril(c), cmat - jnp.swapaxes(cmat, 1, 2), _NEG))
    q = _l2n(qc) * (HEAD ** -0.5)
    k = _l2n(kc)
    k_beta = k * beta
    low = jnp.where(_tril(c, strict=True), _dot(k_beta, k, _NT) * decay, 0.0)
    inv = _unit_lower_inverse(low)
    u, w = _dot(inv, v * beta), _dot(inv, k_beta * jnp.exp(cum))
    intra = _dot(q, k, _NT) * decay
    v_new = u - _dot(w, st, _NT)
    o = _dot(q * jnp.exp(cum), st, _NT) + _dot(intra, v_new)
    st_new = st * jnp.exp(cum_end) + _dot(v_new, k * jnp.exp(cum_end - cum), _TN)
    return _head_norm_gate(o, gnb, z), st_new


def _gdn_specs(n_of):
    wide = _HPS * HEAD
    qkv = lambda s: pl.BlockSpec((CHUNK, wide), lambda n, h, s=s: (n_of(n), (8 // _HPS) * s + h))
    return [qkv(0), qkv(1), qkv(2),
            pl.BlockSpec((CHUNK, wide), lambda n, h: (n_of(n), 56 // _HPS + h)),
            pl.BlockSpec((CHUNK, HEAD), lambda n, h: (n_of(n), 64)),
            pl.BlockSpec((1, HEAD), lambda n, h: (0, 0)), pl.BlockSpec((1, HEAD), lambda n, h: (0, 0)),
            pl.BlockSpec((1, HEAD), lambda n, h: (0, 0))]


def _gdn_fwd(act, proj, arow, dtrow, gnb):
    S = act.shape[0]
    n_chunks, heads = S // CHUNK, 8

    def body(q_ref, k_ref, v_ref, z_ref, t_ref, a_ref, dt_ref, gnb_ref, o_ref, st_out, st):
        n, hb = pl.program_id(0), pl.program_id(1)

        @pl.when(n == 0)
        def _():
            st[hb] = jnp.zeros((_HPS, HEAD, HEAD), F32)

        sls = [slice(hh * HEAD, (hh + 1) * HEAD) for hh in range(_HPS)]
        heads_of = lambda ref: jnp.stack([ref[:, sl] for sl in sls])
        st0 = st[hb]
        o, st1 = _gdn_chunk(heads_of(q_ref), heads_of(k_ref), heads_of(v_ref), heads_of(z_ref), t_ref[...], hb * _HPS,
                            a_ref[...], dt_ref[...], gnb_ref[...], st0)
        st_out[...] = st0
        st[hb] = st1
        for hh, sl in enumerate(sls):
            o_ref[:, sl] = o[hh].astype(o_ref.dtype)

    return pl.pallas_call(
        body, name="gdn_fwd", grid=(n_chunks, heads // _HPS),
        in_specs=_gdn_specs(lambda n: n),
        out_specs=[pl.BlockSpec((CHUNK, _HPS * HEAD), lambda n, h: (n, h)),
                   pl.BlockSpec((None, _HPS, HEAD, HEAD), lambda n, h: (n, h, 0, 0))],
        out_shape=[jax.ShapeDtypeStruct((S, 1024), BF16),
                   jax.ShapeDtypeStruct((n_chunks, heads, HEAD, HEAD), F32)],
        scratch_shapes=[pltpu.VMEM((heads // _HPS, _HPS, HEAD, HEAD), F32)],
        compiler_params=_cparams(("arbitrary", "arbitrary")),
    )(act, act, act, proj, proj, arow, dtrow, gnb)


def _gdn_bwd(act, proj, arow, dtrow, gnb, states, d_o, after=()):
    S = act.shape[0]
    n_chunks, heads = S // CHUNK, 8

    def body(q_ref, k_ref, v_ref, z_ref, t_ref, a_ref, dt_ref, gnb_ref, st_ref, do_ref, *rest):
        dq_ref, dk_ref, dv_ref, dz_ref, dt_out, da_ref, ddt_ref, dgnb_ref, dst, dtail_acc = rest[len(after):]
        n, hb = pl.program_id(0), pl.program_id(1)
        n_hb = heads // _HPS

        @pl.when(n == 0)
        def _():
            dst[hb] = jnp.zeros((_HPS, HEAD, HEAD), F32)

        @pl.when((n == 0) & (hb == 0))
        def _():
            da_ref[...] = jnp.zeros_like(da_ref)
            ddt_ref[...] = jnp.zeros_like(ddt_ref)
            dgnb_ref[...] = jnp.zeros_like(dgnb_ref)

        sls = [slice(hh * HEAD, (hh + 1) * HEAD) for hh in range(_HPS)]
        heads_of = lambda ref: jnp.stack([ref[:, sl] for sl in sls])
        fn = lambda qc, kc, v, z, tail, arow, dtrow, gnb, st: _gdn_chunk(qc, kc, v, z, tail, hb * _HPS, arow, dtrow,
                                                                        gnb, st)
        _, vjp = jax.vjp(fn, heads_of(q_ref), heads_of(k_ref), heads_of(v_ref), heads_of(z_ref), t_ref[...],
                         a_ref[...], dt_ref[...], gnb_ref[...], st_ref[...])
        dq, dk, dv, dz, dtail_sum, da_sum, ddt_sum, dgnb_sum, dst0 = vjp((heads_of(do_ref), dst[hb]))
        dst[hb] = dst0
        for hh, sl in enumerate(sls):
            dq_ref[:, sl] = dq[hh]
            dk_ref[:, sl] = dk[hh]
            dv_ref[:, sl] = dv[hh]
            dz_ref[:, sl] = dz[hh].astype(dz_ref.dtype)

        @pl.when(hb == 0)
        def _():
            dtail_acc[...] = dtail_sum

        @pl.when(hb > 0)
        def _():
            dtail_acc[...] += dtail_sum

        @pl.when(hb == n_hb - 1)
        def _():
            dt_out[...] = dtail_acc[...].astype(dt_out.dtype)

        da_ref[...] += da_sum
        ddt_ref[...] += ddt_sum
        dgnb_ref[...] += dgnb_sum

    rev = lambda n: n_chunks - 1 - n
    out = pl.BlockSpec((CHUNK, _HPS * HEAD), lambda n, h: (rev(n), h))
    row = pl.BlockSpec((1, HEAD), lambda n, h: (0, 0))
    return pl.pallas_call(
        body, name="gdn_bwd", grid=(n_chunks, heads // _HPS),
        in_specs=_gdn_specs(rev)
        + [pl.BlockSpec((None, _HPS, HEAD, HEAD), lambda n, h: (rev(n), h, 0, 0)),
           pl.BlockSpec((CHUNK, _HPS * HEAD), lambda n, h: (rev(n), 8 // _HPS + h))]
        + [pl.BlockSpec(memory_space=pl.ANY)] * len(after),
        out_specs=[out, out, out, out, pl.BlockSpec((CHUNK, HEAD), lambda n, h: (rev(n), 0)), row, row, row],
        out_shape=[jax.ShapeDtypeStruct((S, 1024), F32)] * 3
        + [jax.ShapeDtypeStruct((S, 1024), BF16), jax.ShapeDtypeStruct((S, HEAD), BF16)]
        + [jax.ShapeDtypeStruct((1, HEAD), F32)] * 3,
        scratch_shapes=[pltpu.VMEM((heads // _HPS, _HPS, HEAD, HEAD), F32), pltpu.VMEM((CHUNK, HEAD), F32)],
        compiler_params=_cparams(("arbitrary", "arbitrary")),
    )(act, act, act, proj, proj, arow, dtrow, gnb, states, d_o, *after)


def _conv_silu(x, w):
    t = _iota((x.shape[0], 1), 0)
    tap = _iota(w.shape, 0)
    y = jnp.zeros_like(x)
    for r in range(4):
        w_r = jnp.sum(jnp.where(tap == 3 - r, w, 0.0), axis=0, keepdims=True)
        y = y + (x if r == 0 else jnp.where(t >= r, _roll_rows(x, r), 0.0)) * w_r
    return jax.nn.silu(y)


_CONV_COLS = 128


def _conv_fwd(proj, conv_w):
    S = proj.shape[0]
    nb = 3072 // _CONV_COLS
    off = 4096 // _CONV_COLS

    def body(x_ref, w_ref, o_ref):
        o_ref[...] = _conv_silu(x_ref[...], w_ref[...])

    return pl.pallas_call(
        body, name="conv_fwd", grid=(nb,),
        in_specs=[pl.BlockSpec((S, _CONV_COLS), lambda j: (0, off + j)), pl.BlockSpec((4, _CONV_COLS), lambda j: (0, j))],
        out_specs=pl.BlockSpec((S, _CONV_COLS), lambda j: (0, j)),
        out_shape=jax.ShapeDtypeStruct((S, 3072), F32),
        compiler_params=_cparams(("parallel",)),
    )(proj, conv_w)


def _conv_bwd(proj, conv_w, dq, dk, dv):
    S = proj.shape[0]
    nb = 3072 // _CONV_COLS
    off = 4096 // _CONV_COLS
    per = 1024 // _CONV_COLS

    def body(x_ref, w_ref, dq_ref, dk_ref, dv_ref, dx_ref, dw_ref):
        j = pl.program_id(0)
        _, vjp = jax.vjp(_conv_silu, x_ref[...], w_ref[...])
        d = jnp.where(j < per, dq_ref[...], jnp.where(j < 2 * per, dk_ref[...], dv_ref[...]))
        dx, dw = vjp(d)
        dx_ref[...] = dx.astype(dx_ref.dtype)
        dw_ref[...] = dw

    dsp = lambda s: pl.BlockSpec((S, _CONV_COLS), lambda j, s=s: (0, jnp.clip(j - s * per, 0, per - 1)))
    return pl.pallas_call(
        body, name="conv_bwd", grid=(nb,),
        in_specs=[pl.BlockSpec((S, _CONV_COLS), lambda j: (0, off + j)), pl.BlockSpec((4, _CONV_COLS), lambda j: (0, j)),
                  dsp(0), dsp(1), dsp(2)],
        out_specs=[pl.BlockSpec((S, _CONV_COLS), lambda j: (0, j)), pl.BlockSpec((4, _CONV_COLS), lambda j: (0, j))],
        out_shape=[jax.ShapeDtypeStruct((S, 3072), BF16), jax.ShapeDtypeStruct((4, 3072), F32)],
        compiler_params=_cparams(("parallel",)),
    )(proj, conv_w, dq, dk, dv)


_S5_T = 512
_S5_L = 512
_S5_NB = 16
_S5_U = 4
_S5_UB = 3


def _s5_tile(xr, xi, pw_r, pw_i, cr, ci, reverse):
    t8 = _iota((8, 1), 0)
    for sh in (1, 2, 4):
        row = (8 - sh) if reverse else (sh - 1)
        ar, ai = pw_r[row:row + 1, :], pw_i[row:row + 1, :]
        if reverse:
            keep, amt = t8 < 8 - sh, 8 - sh
        else:
            keep, amt = t8 >= sh, sh
        sr = jnp.where(keep, pltpu.roll(xr, amt, 0), 0.0)
        si = jnp.where(keep, pltpu.roll(xi, amt, 0), 0.0)
        xr, xi = xr + ar * sr - ai * si, xi + ar * si + ai * sr
    xr, xi = xr + pw_r * cr - pw_i * ci, xi + pw_r * ci + pw_i * cr
    return xr, xi


def _s5_fwd(u, bre, bim, cre, cim, pw_r, pw_i, dskip):
    S = u.shape[0]
    T = min(_S5_T, S)
    nt = S // T

    def body(u_ref, bre_ref, bim_ref, cre_ref, cim_ref, pr_ref, pi_ref, d_ref, y_ref, xr_ref, xi_ref,
             bu_r, bu_i, car_r, car_i):
        @pl.when(pl.program_id(1) == 0)
        def _():
            car_r[...] = jnp.zeros_like(car_r)
            car_i[...] = jnp.zeros_like(car_i)

        uu = u_ref[...]
        bu_r[...] = _dot(uu, bre_ref[...])
        bu_i[...] = _dot(uu, bim_ref[...])
        pw_r, pw_i = pr_ref[...], pi_ref[...]

        def tiles(i, carry):
            ins = []
            for k in range(_S5_U):
                r0 = pl.multiple_of((i * _S5_U + k) * 8, 8)
                ins.append((r0, bu_r[pl.ds(r0, 8), :], bu_i[pl.ds(r0, 8), :]))
            outs = []
            for r0, br, bi in ins:
                xr, xi = _s5_tile(br, bi, pw_r, pw_i, carry[0], carry[1], False)
                carry = (xr[7:8, :], xi[7:8, :])
                outs.append((r0, xr, xi))
            for r0, xr, xi in outs:
                xr_ref[pl.ds(r0, 8), :] = xr
                xi_ref[pl.ds(r0, 8), :] = xi
            return carry

        cr, ci = lax.fori_loop(0, T // (8 * _S5_U), tiles, (car_r[...], car_i[...]))
        car_r[...] = cr
        car_i[...] = ci
        y_ref[...] = _dot(xr_ref[...], cre_ref[...]) - _dot(xi_ref[...], cim_ref[...]) + d_ref[...] * uu

    blk3 = lambda a, b: pl.BlockSpec((None, a, b), lambda j, t: (j, 0, 0))
    return pl.pallas_call(
        body, name="s5_fwd", grid=(_S5_NB, nt),
        in_specs=[pl.BlockSpec((T, HEAD), lambda j, t: (t, j)),
                  blk3(HEAD, _S5_L), blk3(HEAD, _S5_L), blk3(_S5_L, HEAD), blk3(_S5_L, HEAD),
                  blk3(8, _S5_L), blk3(8, _S5_L), pl.BlockSpec((1, HEAD), lambda j, t: (0, j))],
        out_specs=[pl.BlockSpec((T, HEAD), lambda j, t: (t, j)),
                   pl.BlockSpec((T, _S5_L), lambda j, t: (t, j)), pl.BlockSpec((T, _S5_L), lambda j, t: (t, j))],
        out_shape=[jax.ShapeDtypeStruct((S, D_MODEL), F32),
                   jax.ShapeDtypeStruct((S, _S5_NB * _S5_L), F32), jax.ShapeDtypeStruct((S, _S5_NB * _S5_L), F32)],
        scratch_shapes=[pltpu.VMEM((T, _S5_L), F32), pltpu.VMEM((T, _S5_L), F32),
                        pltpu.VMEM((1, _S5_L), F32), pltpu.VMEM((1, _S5_L), F32)],
        compiler_params=_cparams(("parallel", "arbitrary")),
    )(u, bre, bim, cre, cim, pw_r, pw_i, dskip)


def _s5_bwd(dy, u, xre, xim, bre, bim, cre, cim, qw_r, qw_i, dskip):
    S = u.shape[0]
    T = min(_S5_T, S)
    nt = S // T
    nt8 = T // 8

    def body(dy_ref, u_ref, xr_ref, xi_ref, xpr_ref, xpi_ref, bre_ref, bim_ref, cre_ref, cim_ref, qr_ref, qi_ref,
             d_ref, du_ref, dbr_ref, dbi_ref, dcr_ref, dci_ref, dlr_ref, dli_ref, dd_ref,
             g_r, g_i, car_r, car_i):
        t = pl.program_id(1)

        @pl.when(t == 0)
        def _():
            car_r[...] = jnp.zeros_like(car_r)
            car_i[...] = jnp.zeros_like(car_i)
            for r in (dbr_ref, dbi_ref, dcr_ref, dci_ref, dlr_ref, dli_ref, dd_ref):
                r[...] = jnp.zeros_like(r)

        dyy, uu = dy_ref[...], u_ref[...]
        g_r[...] = _dot(dyy, cre_ref[...], _NT)
        g_i[...] = -_dot(dyy, cim_ref[...], _NT)
        qw_r, qw_i = qr_ref[...], qi_ref[...]
        t8 = _iota((8, 1), 0)
        first = t == nt - 1

        def load(r0, prev_r, prev_i):
            rows = pl.ds(r0, 8)
            return r0, g_r[rows, :], g_i[rows, :], xr_ref[rows, :], xi_ref[rows, :], prev_r, prev_i

        def run(loaded, carry, acc):
            done = []
            for r0, dr, di, xr, xi, prev_r, prev_i in loaded:
                gr, gi = _s5_tile(dr, di, qw_r, qw_i, carry[0], carry[1], True)
                carry = (gr[0:1, :], gi[0:1, :])
                xpr = jnp.where(t8 >= 1, pltpu.roll(xr, 1, 0), prev_r)
                xpi = jnp.where(t8 >= 1, pltpu.roll(xi, 1, 0), prev_i)
                acc = (acc[0] + gr * xpr + gi * xpi, acc[1] + gi * xpr - gr * xpi)
                done.append((r0, gr, gi))
            for r0, gr, gi in done:
                g_r[pl.ds(r0, 8), :] = gr
                g_i[pl.ds(r0, 8), :] = gi
            return carry, acc

        def step(ii, state):
            loaded = []
            for k in range(_S5_UB):
                idx = nt8 - 1 - (ii * _S5_UB + k)
                r0 = pl.multiple_of(idx * 8, 8)
                p0 = pl.multiple_of((idx - 1) * 8, 8)
                loaded.append(load(r0, xr_ref[pl.ds(p0, 8), :][7:8, :], xi_ref[pl.ds(p0, 8), :][7:8, :]))
            return run(loaded, *state)

        zero = jnp.zeros((8, _S5_L), F32)
        assert (nt8 - 1) % _S5_UB == 0
        state = lax.fori_loop(0, (nt8 - 1) // _S5_UB, step, ((car_r[...], car_i[...]), (zero, zero)))
        prev_r = jnp.where(first, 0.0, xpr_ref[...][7:8, :])
        prev_i = jnp.where(first, 0.0, xpi_ref[...][7:8, :])
        (cr, ci), (ar, ai) = run([load(0, prev_r, prev_i)], *state)
        car_r[...] = cr
        car_i[...] = ci
        dlr_ref[...] += ar
        dli_ref[...] += ai
        gr, gi = g_r[...], g_i[...]
        du_ref[...] = _dot(gr, bre_ref[...], _NT) + _dot(gi, bim_ref[...], _NT) + d_ref[...] * dyy
        dbr_ref[...] += _dot(uu, gr, _TN)
        dbi_ref[...] += _dot(uu, gi, _TN)
        dcr_ref[...] += _dot(xr_ref[...], dyy, _TN)
        dci_ref[...] -= _dot(xi_ref[...], dyy, _TN)
        dd_ref[...] += jnp.sum(dyy * uu, axis=0, keepdims=True)

    rev = lambda t: nt - 1 - t
    blk3 = lambda a, b: pl.BlockSpec((None, a, b), lambda j, t: (j, 0, 0))
    tl = pl.BlockSpec((T, HEAD), lambda j, t: (rev(t), j))
    xs = pl.BlockSpec((T, _S5_L), lambda j, t: (rev(t), j))
    xp = pl.BlockSpec((8, _S5_L), lambda j, t: (jnp.maximum(rev(t) * nt8 - 1, 0), j))
    return pl.pallas_call(
        body, name="s5_bwd", grid=(_S5_NB, nt),
        in_specs=[tl, tl, xs, xs, xp, xp, blk3(HEAD, _S5_L), blk3(HEAD, _S5_L), blk3(_S5_L, HEAD), blk3(_S5_L, HEAD),
                  blk3(8, _S5_L), blk3(8, _S5_L), pl.BlockSpec((1, HEAD), lambda j, t: (0, j))],
        out_specs=[tl, blk3(HEAD, _S5_L), blk3(HEAD, _S5_L), blk3(_S5_L, HEAD), blk3(_S5_L, HEAD),
                   blk3(8, _S5_L), blk3(8, _S5_L), pl.BlockSpec((1, HEAD), lambda j, t: (0, j))],
        out_shape=[jax.ShapeDtypeStruct((S, D_MODEL), F32),
                   jax.ShapeDtypeStruct((_S5_NB, HEAD, _S5_L), F32), jax.ShapeDtypeStruct((_S5_NB, HEAD, _S5_L), F32),
                   jax.ShapeDtypeStruct((_S5_NB, _S5_L, HEAD), F32), jax.ShapeDtypeStruct((_S5_NB, _S5_L, HEAD), F32),
                   jax.ShapeDtypeStruct((_S5_NB, 8, _S5_L), F32), jax.ShapeDtypeStruct((_S5_NB, 8, _S5_L), F32),
                   jax.ShapeDtypeStruct((1, D_MODEL), F32)],
        scratch_shapes=[pltpu.VMEM((T, _S5_L), F32), pltpu.VMEM((T, _S5_L), F32),
                        pltpu.VMEM((1, _S5_L), F32), pltpu.VMEM((1, _S5_L), F32)],
        compiler_params=_cparams(("parallel", "arbitrary")),
    )(dy, u, xre, xim, xre, xim, bre, bim, cre, cim, qw_r, qw_i, dskip)


def _s5_params(a_re, a_im, log_dt, b_re, b_im, c_re, c_im):
    step = jnp.exp(log_dt)[:, None]
    mag = jnp.exp(a_re * step)
    lr, li = mag * jnp.cos(a_im * step), mag * jnp.sin(a_im * step)
    den = a_re * a_re + a_im * a_im
    nr, ni = lr - 1.0, li
    kr, ki = (nr * a_re + ni * a_im) / den, (ni * a_re - nr * a_im) / den
    bbr = kr[..., None] * b_re - ki[..., None] * b_im
    bbi = kr[..., None] * b_im + ki[..., None] * b_re
    eye = jnp.eye(8, dtype=F32)

    def blk_b(bb):
        t = bb.reshape(_S5_NB, 8, 64, 16).transpose(0, 1, 3, 2)
        return (t[:, :, :, None, :] * eye[None, :, None, :, None]).reshape(_S5_NB, HEAD, _S5_L)

    def blk_c(cc):
        t = cc.reshape(_S5_NB, 8, 16, 64).transpose(0, 1, 3, 2)
        return (t[:, :, :, None, :] * eye[None, :, None, :, None]).reshape(_S5_NB, _S5_L, HEAD)

    return (blk_b(bbr), blk_b(bbi), blk_c(c_re), blk_c(c_im),
            lr.reshape(_S5_NB, _S5_L), li.reshape(_S5_NB, _S5_L))


def _s5_powers(lr, li):
    pr, pi = [lr], [li]
    for _ in range(7):
        pr, pi = pr + [pr[-1] * lr - pi[-1] * li], pi + [pr[-1] * li + pi[-1] * lr]
    return jnp.stack(pr, axis=1), jnp.stack(pi, axis=1)


_MESH = pl.DeviceIdType.MESH
_ANY = pl.BlockSpec(memory_space=pl.ANY)


def _place():
    x, y, c = lax.axis_index("x"), lax.axis_index("y"), lax.axis_index("c")
    return x, y, c, [(1 - x, y), (x, 1 - y), (1 - x, 1 - y)]


def _comm_call(body, arrs, out_shapes, n_remote, name):
    n = len(arrs)
    return pl.pallas_call(
        body, name=name,
        in_specs=[_ANY] * n, out_specs=[_ANY] * n, out_shape=out_shapes,
        scratch_shapes=[pltpu.SemaphoreType.DMA((n * n_remote,)), pltpu.SemaphoreType.DMA((n * n_remote,)),
                        pltpu.SemaphoreType.DMA((n,))],
    )(*arrs)


def _half(ref, slot, h):
    if len(ref.shape) == 4:
        return ref.at[slot, h]
    cols = ref.shape[2] // 2
    return ref.at[slot, :, pl.ds(pl.multiple_of(h * cols, 128), cols)]


def _own_slot(shard, me_idx, name, rows=256, cols=512):
    def body(me_ref, in_ref, o_ref):
        o_ref[...] = in_ref[...]

    if shard.ndim == 2:
        R, C = shard.shape
        return pl.pallas_call(
            body, name=name,
            grid_spec=pltpu.PrefetchScalarGridSpec(
                num_scalar_prefetch=1, grid=(C // cols,),
                in_specs=[pl.BlockSpec((R, cols), lambda j, me: (0, j))],
                out_specs=pl.BlockSpec((None, R, cols), lambda j, me: (me[0], 0, j))),
            out_shape=jax.ShapeDtypeStruct((4,) + shard.shape, shard.dtype),
            compiler_params=_cparams(("parallel",)),
        )(me_idx, shard)
    _, half, C = shard.shape
    rows = min(rows, half)
    assert half % rows == 0
    return pl.pallas_call(
        body, name=name,
        grid_spec=pltpu.PrefetchScalarGridSpec(
            num_scalar_prefetch=1, grid=(2, half // rows),
            in_specs=[pl.BlockSpec((None, rows, C), lambda h, i, me: (h, i, 0))],
            out_specs=pl.BlockSpec((None, None, rows, C), lambda h, i, me: (me[0], h, i, 0))),
        out_shape=jax.ShapeDtypeStruct((4,) + shard.shape, shard.dtype),
        compiler_params=_cparams(("parallel", "parallel")),
    )(me_idx, shard)


def _own_rows(w, layer, buf, me_idx, row_off, after, name, rows=512):
    _, r, C = w.shape
    carried = [] if isinstance(buf, tuple) else [buf]
    shape = buf if isinstance(buf, tuple) else buf.shape
    assert r % rows == 0 and row_off % rows == 0

    def body(me_ref, in_ref, *rest):
        rest[-1][...] = in_ref[...].astype(BF16)

    return pl.pallas_call(
        body, name=name,
        grid_spec=pltpu.PrefetchScalarGridSpec(
            num_scalar_prefetch=1, grid=(r // rows,),
            in_specs=[pl.BlockSpec((None, rows, C), lambda i, me: (layer, i, 0))] + [_ANY] * (len(carried) + len(after)),
            out_specs=pl.BlockSpec((None, rows, C), lambda i, me: (me[0], row_off // rows + i, 0))),
        out_shape=jax.ShapeDtypeStruct(shape, BF16),
        input_output_aliases={2: 0} if carried else {},
        compiler_params=_cparams(("parallel",)),
    )(me_idx, w, *carried, *after)


def _pair_split(arrs, name):
    n = len(arrs)

    def body(*refs):
        ins, outs, (ssem, rsem, _) = refs[:n], refs[n:2 * n], refs[2 * n:]
        x, y, c, _ = _place()
        copies = []
        for a in range(n):
            for s in range(4):
                cp = pltpu.make_async_remote_copy(src_ref=_half(ins[a], s, 1 - c), dst_ref=outs[a].at[s],
                                                  send_sem=ssem.at[4 * a + s], recv_sem=rsem.at[4 * a + s],
                                                  device_id=(x, y, 1 - c), device_id_type=_MESH)
                cp.start()
                copies.append(cp)
        for cp in copies:
            cp.wait()

    shapes = [jax.ShapeDtypeStruct((4,) + (a.shape[2:] if a.ndim == 4 else (a.shape[1], a.shape[2] // 2)), a.dtype)
              for a in arrs]
    return _comm_call(body, arrs, shapes, 4, name)


def _pair_swap(arrs, name):
    n = len(arrs)

    def body(*refs):
        ins, outs, (ssem, rsem, _) = refs[:n], refs[n:2 * n], refs[2 * n:]
        x, y, c, _ = _place()
        copies = []
        for a in range(n):
            cp = pltpu.make_async_remote_copy(src_ref=ins[a], dst_ref=outs[a], send_sem=ssem.at[a], recv_sem=rsem.at[a],
                                              device_id=(x, y, 1 - c), device_id_type=_MESH)
            cp.start()
            copies.append(cp)
        for cp in copies:
            cp.wait()

    return _comm_call(body, arrs, [jax.ShapeDtypeStruct(a.shape, a.dtype) for a in arrs], 1, name)


def _pair_sum(full, recv, c_idx, name, rows=256):
    def add(c_ref, a_ref, b_ref, o_ref):
        o_ref[...] = (a_ref[...].astype(F32) + b_ref[...].astype(F32)).astype(o_ref.dtype)

    if full.ndim == 3:
        _, R, cols = recv.shape
        blk = lambda col: pl.BlockSpec((None, R, cols), lambda s, c: (s, 0, col(c)))
        return pl.pallas_call(
            add, name=name,
            grid_spec=pltpu.PrefetchScalarGridSpec(
                num_scalar_prefetch=1, grid=(4,),
                in_specs=[blk(lambda c: c[0]), blk(lambda c: 0)], out_specs=blk(lambda c: 0)),
            out_shape=jax.ShapeDtypeStruct(recv.shape, recv.dtype),
            compiler_params=_cparams(("parallel",)),
        )(c_idx, full, recv)
    _, _, half, C = full.shape
    rows = min(rows, half)
    nb = half // rows
    assert half % rows == 0

    def body(c_ref, a_ref, b_ref, o_ref):
        o_ref[...] = (a_ref[...].astype(F32) + b_ref[...].astype(F32)).astype(o_ref.dtype)

    return pl.pallas_call(
        body, name=name,
        grid_spec=pltpu.PrefetchScalarGridSpec(
            num_scalar_prefetch=1, grid=(4, nb),
            in_specs=[pl.BlockSpec((None, None, rows, C), lambda s, i, c: (s, c[0], i, 0)),
                      pl.BlockSpec((None, rows, C), lambda s, i, c: (s, i, 0))],
            out_specs=pl.BlockSpec((None, rows, C), lambda s, i, c: (s, i, 0))),
        out_shape=jax.ShapeDtypeStruct(recv.shape, recv.dtype),
        compiler_params=_cparams(("parallel", "parallel")),
    )(c_idx, full, recv)


_HBM = pl.BlockSpec(memory_space=pltpu.HBM)
_SEM = pl.BlockSpec(memory_space=pltpu.SEMAPHORE)
_SPLIT = dict(has_side_effects=pltpu.SideEffectType.DATAFLOW_SIDE_EFFECTING)


def _hbm(t):
    return pltpu.with_memory_space_constraint(t, pltpu.HBM)


def _ag_copy(buf_ref, ssem, rsem, j, chip, c, me):
    px, py = chip
    mine = _half(buf_ref, me, c)
    return pltpu.make_async_remote_copy(src_ref=mine, dst_ref=mine, send_sem=ssem.at[j], recv_sem=rsem.at[j],
                                        device_id=(px, py, c), device_id_type=_MESH)


def _ag_start(bufs, after, name):
    n = len(bufs)

    def body(*refs):
        buf_refs, (ssem, rsem), token = refs[:n], refs[n + 1:n + 3], refs[-1]
        x, y, c, chips = _place()
        for a in range(n):
            for j, chip in enumerate(chips):
                _ag_copy(buf_refs[a], ssem, rsem, 3 * a + j, chip, c, 2 * x + y).start()
        token[...] = jnp.zeros_like(token)

    out = pl.pallas_call(
        body, name=name,
        out_shape=(pltpu.SemaphoreType.DMA((3 * n,)), pltpu.SemaphoreType.DMA((3 * n,)),
                   *[pltpu.HBM(b.shape, b.dtype) for b in bufs], jax.ShapeDtypeStruct((8, HEAD), F32)),
        in_specs=(*[_HBM] * n, _ANY), out_specs=(_SEM, _SEM, *[_HBM] * n, pl.BlockSpec(memory_space=pltpu.VMEM)),
        input_output_aliases={a: 2 + a for a in range(n)}, compiler_params=pltpu.CompilerParams(**_SPLIT),
    )(*[_hbm(b) for b in bufs], after)
    return out[0], out[1], list(out[2:2 + n]), out[-1]


def _ag_wait(ssem, rsem, bufs, after, name):
    n = len(bufs)

    def body(*refs):
        buf_refs, ssem, rsem = refs[:n], refs[n], refs[n + 1]
        x, y, c, chips = _place()
        for a in range(n):
            for j, chip in enumerate(chips):
                cp = _ag_copy(buf_refs[a], ssem, rsem, 3 * a + j, chip, c, 2 * x + y)
                cp.wait_send()
                cp.wait_recv()

    after = after if isinstance(after, (tuple, list)) else (after,)
    return list(pl.pallas_call(
        body, name=name, out_shape=tuple(pltpu.HBM(b.shape, b.dtype) for b in bufs),
        in_specs=(*[_HBM] * n, _SEM, _SEM, *[_ANY] * len(after)), out_specs=tuple([_HBM] * n),
        input_output_aliases={a: a for a in range(n)}, compiler_params=pltpu.CompilerParams(**_SPLIT),
    )(*bufs, ssem, rsem, *after))


def _pair_forward(bufs, name):
    n = len(bufs)

    def body(*refs):
        outs, (ssem, rsem) = refs[n:2 * n], refs[2 * n:]
        x, y, c, chips = _place()
        copies = []
        for a in range(n):
            for j, (px, py) in enumerate(chips):
                landed = _half(outs[a], 2 * px + py, c)
                cp = pltpu.make_async_remote_copy(src_ref=landed, dst_ref=landed, send_sem=ssem.at[3 * a + j],
                                                  recv_sem=rsem.at[3 * a + j], device_id=(x, y, 1 - c),
                                                  device_id_type=_MESH)
                cp.start()
                copies.append(cp)
        for cp in copies:
            cp.wait()

    return pl.pallas_call(
        body, name=name, in_specs=[_ANY] * n, out_specs=[_ANY] * n,
        out_shape=[jax.ShapeDtypeStruct(b.shape, b.dtype) for b in bufs],
        input_output_aliases={a: a for a in range(n)},
        scratch_shapes=[pltpu.SemaphoreType.DMA((3 * n,)), pltpu.SemaphoreType.DMA((3 * n,))],
    )(*bufs)


def _split_copy(part_ref, land_ref, ssem, rsem, s, x, y, c):
    return pltpu.make_async_remote_copy(src_ref=_half(part_ref, s, 1 - c), dst_ref=land_ref.at[s], send_sem=ssem.at[s],
                                        recv_sem=rsem.at[s], device_id=(x, y, 1 - c), device_id_type=_MESH)


def _pair_split_start(part, after, name):
    land = jax.ShapeDtypeStruct((4,) + part.shape[2:], part.dtype)

    def body(part_ref, land_ref, after_ref, ssem, rsem, part_thru, land_thru, token):
        x, y, c, _ = _place()
        for s in range(4):
            _split_copy(part_ref, land_ref, ssem, rsem, s, x, y, c).start()
        token[...] = jnp.zeros_like(token)

    return pl.pallas_call(
        body, name=name,
        out_shape=(pltpu.SemaphoreType.DMA((4,)), pltpu.SemaphoreType.DMA((4,)), pltpu.HBM(part.shape, part.dtype),
                   pltpu.HBM(land.shape, land.dtype), jax.ShapeDtypeStruct((8, HEAD), F32)),
        in_specs=(_HBM, _HBM, _ANY), out_specs=(_SEM, _SEM, _HBM, _HBM, pl.BlockSpec(memory_space=pltpu.VMEM)),
        input_output_aliases={0: 2, 1: 3}, compiler_params=pltpu.CompilerParams(**_SPLIT),
    )(_hbm(part), _hbm(lax.empty(land.shape, land.dtype)), after)


def _pair_split_wait(ssem, rsem, part, land, after, name):
    def body(part_ref, land_ref, ssem, rsem, after_ref, part_out, land_out):
        x, y, c, _ = _place()
        for s in range(4):
            cp = _split_copy(part_ref, land_ref, ssem, rsem, s, x, y, c)
            cp.wait_send()
            cp.wait_recv()

    return pl.pallas_call(
        body, name=name, out_shape=(pltpu.HBM(part.shape, part.dtype), pltpu.HBM(land.shape, land.dtype)),
        in_specs=(_HBM, _HBM, _SEM, _SEM, _ANY), out_specs=(_HBM, _HBM),
        input_output_aliases={0: 0, 1: 1}, compiler_params=pltpu.CompilerParams(**_SPLIT),
    )(part, land, ssem, rsem, after)


def _rs_copy(src_ref, land_ref, ssem, rsem, j, chip, c):
    px, py = chip
    return pltpu.make_async_remote_copy(src_ref=src_ref.at[2 * px + py], dst_ref=land_ref.at[j], send_sem=ssem.at[j],
                                        recv_sem=rsem.at[j], device_id=(px, py, c), device_id_type=_MESH)


def _rs_start(part, after, name):
    def body(part_ref, land_ref, after_ref, ssem, rsem, part_thru, land_thru, token):
        x, y, c, chips = _place()
        for j, chip in enumerate(chips):
            _rs_copy(part_ref, land_ref, ssem, rsem, j, chip, c).start()
        token[...] = jnp.zeros_like(token)

    land = jax.ShapeDtypeStruct((3,) + part.shape[1:], part.dtype)
    return pl.pallas_call(
        body, name=name,
        out_shape=(pltpu.SemaphoreType.DMA((3,)), pltpu.SemaphoreType.DMA((3,)), pltpu.HBM(part.shape, part.dtype),
                   pltpu.HBM(land.shape, land.dtype), jax.ShapeDtypeStruct((8, HEAD), F32)),
        in_specs=(_HBM, _HBM, _ANY), out_specs=(_SEM, _SEM, _HBM, _HBM, pl.BlockSpec(memory_space=pltpu.VMEM)),
        input_output_aliases={0: 2, 1: 3}, compiler_params=pltpu.CompilerParams(**_SPLIT),
    )(_hbm(part), _hbm(lax.empty(land.shape, land.dtype)), after)


def _rs_wait(ssem, rsem, part, land, after, name):
    def body(part_ref, land_ref, ssem, rsem, after_ref, part_out, land_out):
        x, y, c, chips = _place()
        for j, chip in enumerate(chips):
            cp = _rs_copy(part_ref, land_ref, ssem, rsem, j, chip, c)
            cp.wait_send()
            cp.wait_recv()

    return pl.pallas_call(
        body, name=name, out_shape=(pltpu.HBM(part.shape, part.dtype), pltpu.HBM(land.shape, land.dtype)),
        in_specs=(_HBM, _HBM, _SEM, _SEM, _ANY), out_specs=(_HBM, _HBM),
        input_output_aliases={0: 0, 1: 1}, compiler_params=pltpu.CompilerParams(**_SPLIT),
    )(part, land, ssem, rsem, after)


def _sum_own_recv(part, land, me_idx, name, rows=256):
    _, half, C = part.shape

    def body(me_ref, own_ref, land_ref, o_ref):
        acc = own_ref[...].astype(F32)
        for s in range(3):
            acc = acc + land_ref[s].astype(F32)
        o_ref[...] = acc

    if half % 8:
        cols = 256
        return pl.pallas_call(
            body, name=name,
            grid_spec=pltpu.PrefetchScalarGridSpec(
                num_scalar_prefetch=1, grid=(C // cols,),
                in_specs=[pl.BlockSpec((None, half, cols), lambda j, me: (me[0], 0, j)),
                          pl.BlockSpec((3, half, cols), lambda j, me: (0, 0, j))],
                out_specs=pl.BlockSpec((half, cols), lambda j, me: (0, j))),
            out_shape=jax.ShapeDtypeStruct((half, C), F32),
            compiler_params=_cparams(("parallel",)),
        )(me_idx, part, land)
    rows = min(rows, half)
    assert half % rows == 0
    return pl.pallas_call(
        body, name=name,
        grid_spec=pltpu.PrefetchScalarGridSpec(
            num_scalar_prefetch=1, grid=(half // rows,),
            in_specs=[pl.BlockSpec((None, rows, C), lambda i, me: (me[0], i, 0)),
                      pl.BlockSpec((3, rows, C), lambda i, me: (0, i, 0))],
            out_specs=pl.BlockSpec((rows, C), lambda i, me: (i, 0))),
        out_shape=jax.ShapeDtypeStruct((half, C), F32),
        compiler_params=_cparams(("parallel",)),
    )(me_idx, part, land)


def _sum_slots(arr, name, rows=128):
    k, R, C = arr.shape
    rows = min(rows, R)
    assert R % rows == 0

    def body(in_ref, o_ref):
        acc = in_ref[0].astype(F32)
        for s in range(1, k):
            acc = acc + in_ref[s].astype(F32)
        o_ref[...] = acc

    return pl.pallas_call(
        body, name=name, grid=(R // rows,),
        in_specs=[pl.BlockSpec((k, rows, C), lambda i: (0, i, 0))],
        out_specs=pl.BlockSpec((rows, C), lambda i: (i, 0)),
        out_shape=jax.ShapeDtypeStruct((R, C), F32),
        compiler_params=_cparams(("parallel",)),
    )(arr)


ADAM_LR, ADAM_B1, ADAM_B2, ADAM_EPS, ADAM_WD, ADAM_STEP = 0.001, 0.9, 0.999, 1e-08, 0.01, 10


def _adam_math(w, g, m, v):
    m = ADAM_B1 * m + (1.0 - ADAM_B1) * g
    v = ADAM_B2 * v + (1.0 - ADAM_B2) * jnp.square(g)
    m_hat = m / (1.0 - ADAM_B1 ** ADAM_STEP)
    v_hat = v / (1.0 - ADAM_B2 ** ADAM_STEP)
    delta = -ADAM_LR * (m_hat / (jnp.sqrt(v_hat) + ADAM_EPS) + ADAM_WD * w)
    return delta, m, v


def _adamw(w, m, v, layer, mine, other, c_idx, name, g_off=0, prev=None, after=(), rows=256):
    _, R, C = w.shape
    half = mine.shape[0]
    rows = min(rows, R)
    assert R % rows == 0 and g_off % rows == 0 and half % rows == 0
    nbh, b0 = half // rows, g_off // rows

    def body(c_ref, w_ref, m_ref, v_ref, mine_ref, other_ref, *rest):
        go, do, mo, vo = rest[-4:]
        in_my_half = (b0 + pl.program_id(0)) // nbh == c_ref[0]
        g = jnp.where(in_my_half, mine_ref[...], other_ref[...])
        delta, m1, v1 = _adam_math(w_ref[...], g, m_ref[...], v_ref[...])
        go[...] = g
        do[...] = delta
        mo[...] = m1
        vo[...] = v1

    blk = pl.BlockSpec((None, rows, C), lambda i, c: (layer, i, 0))

    def gblk(of_mine):
        return pl.BlockSpec((rows, C), lambda i, c: (
            jnp.where(((b0 + i) // nbh == c[0]) == of_mine, (b0 + i) % nbh, 0), 0))

    carried = list(prev) if prev is not None else []
    return pl.pallas_call(
        body, name=name,
        grid_spec=pltpu.PrefetchScalarGridSpec(
            num_scalar_prefetch=1, grid=(R // rows,),
            in_specs=[blk] * 3 + [gblk(True), gblk(False)] + [_ANY] * (len(carried) + len(after)),
            out_specs=[blk] * 4),
        out_shape=[jax.ShapeDtypeStruct(w.shape, F32)] * 4,
        input_output_aliases={6 + k: k for k in range(len(carried))},
        compiler_params=_cparams(("parallel",)),
    )(c_idx, w, m, v, mine, other, *carried, *after)


def _adamw_t(w, m, v, mine, other, c_idx, name, rows=513, cols=128):
    R, _, C = w.shape
    nbh = (C // 2) // cols
    assert R % rows == 0 and (C // 2) % cols == 0

    def body(c_ref, w_ref, m_ref, v_ref, mine_ref, other_ref, go, do, mo, vo):
        in_my_half = pl.program_id(1) // nbh == c_ref[0]
        g = jnp.where(in_my_half, mine_ref[...], other_ref[...])
        delta, m1, v1 = _adam_math(w_ref[...], g, m_ref[...], v_ref[...])
        go[...] = g
        do[...] = delta
        mo[...] = m1
        vo[...] = v1

    blk = pl.BlockSpec((rows, 1, cols), lambda i, j, c: (i, 0, j))
    gblk = pl.BlockSpec((rows, 1, cols), lambda i, j, c: (i, 0, j % nbh))
    return pl.pallas_call(
        body, name=name,
        grid_spec=pltpu.PrefetchScalarGridSpec(
            num_scalar_prefetch=1, grid=(R // rows, C // cols), in_specs=[blk] * 3 + [gblk] * 2,
            out_specs=[blk] * 4),
        out_shape=[jax.ShapeDtypeStruct(w.shape, F32)] * 4,
        compiler_params=_cparams(("parallel", "parallel")),
    )(c_idx, w, m, v, mine, other)


_WEIGHTS = ['norm_mix', 'norm_mlp', 'norm_ple', 'w_in_e', 'w_out_e', 'hgrn_lb', 'g_norm_a', 'conv_w', 'a_log',
            'dt_bias', 'g_norm_b', 's5_a_re', 's5_a_im', 's5_b_re', 's5_b_im', 's5_c_re', 's5_c_im', 's5_d',
            's5_log_dt', 'w_glu', 'b_glu', 'w_out_o', 'w_up', 'w_down', 'w_ple_gate', 'w_ple_proj', 'final_norm']
_INPUTS = ['x', 'p'] + _WEIGHTS + ['loss_target'] + ['m_' + n for n in _WEIGHTS] + ['v_' + n for n in _WEIGHTS]
_FAMILY = [('w_up', 0, 'col', 2048), ('w_down', 0, 'row', 2048), ('w_ple_gate', 0, 'row', 512), ('w_out_e', 0, 'row', 512),
           ('w_up', 1, 'col', 2048), ('w_down', 1, 'row', 2048), ('w_ple_gate', 1, 'row', 512), ('w_glu', 0, 'row', 512),
           ('w_out_o', 0, 'row', 512)]
_PACK = {('w_up', 0): 0, ('w_down', 0): 0, ('w_ple_gate', 0): 0, ('w_out_e', 0): 0,
         ('w_up', 1): 1, ('w_down', 1): 1, ('w_ple_gate', 1): 1, ('w_glu', 0): 1, ('w_out_o', 0): 1}
_IN_PAD = 8320
_IN_BLK = 1664


def _rms_bwd_fn(x, d_hn, d_res, g):
    _, vjp = jax.vjp(_rms, x, g)
    dx, dg = vjp(d_hn.astype(F32))
    return dx + d_res, dg


def _rms_bwd_both(x, d_hn, d_res, g):
    dx, dg = _rms_bwd_fn(x, d_hn, d_res, g)
    return dx, dx, dg


def _add_res(acc, h):
    return (acc + h,)


def _pack_rows(parts, lanes=128, mult=256):
    flat = jnp.concatenate([q.reshape(-1).astype(F32) for q in parts])
    n = flat.shape[0]
    rows = -(-n // (lanes * mult)) * mult
    return jnp.pad(flat, (0, rows * lanes - n)).reshape(rows, lanes)


def _unpack_rows(buf, shapes):
    flat, out, off = buf.reshape(-1), [], 0
    for s in shapes:
        n = math.prod(s)
        out.append(flat[off:off + n].reshape(s))
        off += n
    return out


def _step(a):
    S = a['x'].shape[1]
    x, tgt = a['x'][0], a['loss_target'][0]
    xi, yi = lax.axis_index("x"), lax.axis_index("y")
    me = 2 * xi + yi
    row = lambda t: t.reshape(1, -1)

    me_idx = me.astype(jnp.int32).reshape(1)
    c_idx = lax.axis_index("c").astype(jnp.int32).reshape(1)
    small_sh = jnp.concatenate([a['conv_w'][0].reshape(-1), a['s5_d'][0], a['b_glu'][0]]).reshape(16, 256)
    rows_first = lambda t: jnp.transpose(t, (2, 0, 1))
    shards = [None, None, rows_first(a['w_in_e']).reshape(2052, D_MODEL).astype(BF16),
              a['w_ple_proj'].reshape(512, 512).astype(BF16), small_sh]
    own = lambda i: _own_slot(shards[i] if i == 2 else shards[i].reshape(2, shards[i].shape[0] // 2, shards[i].shape[1]),
                              me_idx, name=f"own_slot{i}")
    whole = lambda g, t: g.reshape((4,) + t.shape)
    first = _ag_start([own(2), own(3), own(4)], me_idx, name="gather_first_start")
    big = {nm for nm, _, _, _ in _FAMILY} | {'w_in_e', 'w_ple_proj'}
    held = ['x'] + [pre + n for n in _WEIGHTS if n not in big for pre in ('', 'm_', 'v_')]
    _, late = lax.optimization_barrier((first[3], {n: a[n] for n in held}))
    a = {**a, **late}
    x = a['x'][0]
    views, pack_of, rt = {}, {}, [0, 0]
    for nm, l, kind, r in _FAMILY:
        views[nm, l], pack_of[nm, l] = (kind, rt[_PACK[nm, l]], r), _PACK[nm, l]
        rt[_PACK[nm, l]] += r
    bufs = [(4, rt[0], PACK_COLS), (4, rt[1], PACK_COLS)]
    for nm, l, kind, r in _FAMILY:
        k = pack_of[nm, l]
        bufs[k] = _own_rows(a[nm], l, bufs[k], me_idx, views[nm, l][1], (first[3],), name=f"own_{nm}{l}")
    shards[0], shards[1] = [jax.ShapeDtypeStruct((rt[k], PACK_COLS), BF16) for k in (0, 1)]
    slots = [bufs[k].reshape(4, 2, rt[k] // 2, PACK_COLS) for k in (0, 1)]
    hn0 = _rowwise(_rms, [x], [row(a['norm_mix'][0])], [(D_MODEL, BF16)], name="rms_mix0")
    s_names = [n for n in _WEIGHTS if n not in big]
    w_pack, m_pack, v_pack = [_pack_rows([a[pre + n] for n in s_names]) for pre in ('', 'm_', 'v_')]
    landed = _pair_forward(_ag_wait(first[0], first[1], first[2], (hn0, bufs[0], bufs[1], w_pack, m_pack, v_pack),
                                    name="gather_first_wait"), name="gather_first_pass")
    g_in, g_pp, g_small = [whole(g, t) for g, t in zip(landed, shards[2:])]
    gw = {}
    W = lambda nm, l: dict(b=gw[pack_of[nm, l]], b_view=views[nm, l])
    sems0 = _ag_start([slots[0]], g_in, name="gather0_start")
    win_t = jnp.pad(g_in.reshape(8208, D_MODEL), ((0, _IN_PAD - 8208), (0, 0)))
    w_pp = g_pp.reshape(4, 2, 256, 512).transpose(1, 2, 0, 3).reshape(2, 256, D_MODEL)
    g_small = g_small.reshape(4, 4096)
    conv_full = g_small[:, :3072].reshape(4, 4, 768).transpose(1, 0, 2).reshape(4, 3072)
    s5_d_full = g_small[:, 3072:3584].reshape(1, D_MODEL)
    b_glu_full = g_small[:, 3584:].reshape(1, D_MODEL)

    lb, lb_vjp = jax.vjp(lambda hl: jax.nn.softmax(hl, axis=0)[0:1], a['hgrn_lb'])
    pad_h = lambda t: jnp.pad(t, ((0, 0), (0, HEAD - t.shape[1])))
    (arow, dtrow), gdn_vjp = jax.vjp(lambda al, dt: (pad_h(-jnp.exp(al)), pad_h(dt)), a['a_log'], a['dt_bias'])
    s5p, s5_vjp = jax.vjp(_s5_params, a['s5_a_re'][0], a['s5_a_im'][0], a['s5_log_dt'][0], a['s5_b_re'][0],
                          a['s5_b_im'][0], a['s5_c_re'][0], a['s5_c_im'][0])
    bre, bim, cre, cim = [t.astype(BF16) for t in s5p[:4]]
    pw_r, pw_i = _s5_powers(s5p[4], s5p[5])
    gna, gnb = a['g_norm_a'], a['g_norm_b']

    def norm_cast(h, g, dt, name):
        return _rowwise(_rms, [h], [row(g)], [(D_MODEL, dt)], name=name)

    def mlp_ple_fwd(h, l):
        hn = norm_cast(h, a['norm_mlp'][l], BF16, f"rms_mlp{l}")
        up, act = _mm(hn, **W('w_up', l), out_dtypes=(BF16, BF16), name=f"up{l}",
                      epilogue=lambda acc: (acc, jnp.square(jnp.maximum(acc, 0.0))))
        h2 = _mm(act, **W('w_down', l), epilogue=_add_res, extras=(h,), name=f"down{l}")
        hnp = norm_cast(h2, a['norm_ple'][l], BF16, f"rms_ple{l}")
        pp = _mm(a['p'][l, 0], w_pp[l], name=f"ple_proj{l}")
        h3, gpre = _mm(hnp, **W('w_ple_gate', l), extras=(h2, pp), out_dtypes=(F32, F32), name=f"ple_gate{l}",
                       epilogue=lambda acc, h2, pp: (h2 + jax.nn.sigmoid(acc) * pp, acc))
        return h3, (h, hn, up, act, h2, hnp, pp, gpre)

    proj = _mm(hn0, win_t, dims="nt", bn=_IN_BLK, name="in_proj", after=(sems0[3],))
    o_a, st_a = _hgrn_fwd(proj, lb, gna)
    act_b = _conv_fwd(proj, conv_full)
    o_b, st_b = _gdn_fwd(act_b, proj, arow, dtrow, gnb)
    merged = jnp.concatenate([o_a, o_b], axis=1)
    landed0 = _ag_wait(sems0[0], sems0[1], sems0[2], merged, name="gather0_wait")
    gw[0] = whole(_pair_forward(landed0, name="gather0_pass")[0], shards[0])
    sems1 = _ag_start([slots[1]], gw[0], name="gather1_start")
    h1 = _mm(merged, **W('w_out_e', 0), epilogue=_add_res, extras=(x,), name="out_e", after=(sems1[3],))
    h3, saved0 = mlp_ple_fwd(h1, 0)

    u = norm_cast(h3, a['norm_mix'][1], F32, "rms_mix1")
    y, xre, xim = _s5_fwd(u, bre, bim, cre, cim, pw_r, pw_i, s5_d_full)
    actg = _rowwise(jax.nn.gelu, [y], [], [(D_MODEL, BF16)], name="gelu")
    landed1 = _ag_wait(sems1[0], sems1[1], sems1[2], actg, name="gather1_wait")
    gw[1] = whole(_pair_forward(landed1, name="gather1_pass")[0], shards[1])
    glu, z = _mm(actg, **W('w_glu', 0), extras=(actg, b_glu_full), out_dtypes=(BF16, F32), name="glu",
                 epilogue=lambda acc, act, b: (act.astype(F32) * jax.nn.sigmoid(acc + b), acc + b))
    h4 = _mm(glu, **W('w_out_o', 0), epilogue=_add_res, extras=(h3,), name="out_o")
    h6, saved1 = mlp_ple_fwd(h4, 1)

    def head(h, t, g):
        def f(h, g):
            e = _rms(h, g) - t
            return 0.5 * jnp.sum(jnp.mean(e * e, axis=-1))
        val, vjp = jax.vjp(f, h, g)
        dh, dg = vjp(jnp.ones((), F32))
        return dh, dg, jnp.full((1, HEAD), val)

    dh, d_final, loss_part = _rowwise(head, [h6, tgt], [row(a['final_norm'])], [(D_MODEL, F32)], sums=(D_MODEL, HEAD),
                                      name="loss_head")
    loss = lax.psum(loss_part[0, 0], ("x", "y", "c"))

    gb = {k: lax.empty((4, rt[k], PACK_COLS), BF16) for k in (0, 1)}

    def into(lhs, rhs, key, name, after=()):
        k = pack_of[key]
        gb[k] = _mm(lhs, rhs, dims="tn", out_into=gb[k], out_view=views[key], out_dtypes=(BF16,), name=name, after=after)

    def mlp_ple_bwd(dh, l, saved, after=()):
        h, hn, up, act, h2, hnp, pp, gpre = saved

        def gate_bwd(d, gpre, pp):
            s = jax.nn.sigmoid(gpre)
            return d * s, d * pp * s * (1.0 - s)

        d_pp, d_gp = _rowwise(gate_bwd, [dh, gpre, pp], [], [(D_MODEL, BF16), (D_MODEL, BF16)], name=f"ple_bwd{l}")
        d_wpp = _mm(a['p'][l, 0], d_pp, dims="tn", name=f"d_ple_proj{l}")
        into(hnp, d_gp, ('w_ple_gate', l), f"d_ple_gate{l}")
        d_hnp = _mm(d_gp, **W('w_ple_gate', l), dims="nt", name=f"ple_gate_t{l}", after=after)
        dh2, dh2_b, d_nple = _rowwise(_rms_bwd_both, [h2, d_hnp, dh], [row(a['norm_ple'][l])],
                                      [(D_MODEL, F32), (D_MODEL, BF16)], sums=(D_MODEL,), name=f"rms_ple_bwd{l}")
        d_up = _mm(dh2_b, **W('w_down', l), dims="nt", extras=(up,), out_dtypes=(BF16,), name=f"down_t{l}",
                   epilogue=lambda acc, up: (acc * 2.0 * jnp.maximum(up.astype(F32), 0.0),))
        into(act, dh2_b, ('w_down', l), f"d_down{l}")
        into(hn, d_up, ('w_up', l), f"d_up{l}")
        d_hn = _mm(d_up, **W('w_up', l), dims="nt", name=f"up_t{l}")
        dh1, dh1_b, d_nmlp = _rowwise(_rms_bwd_both, [h, d_hn, dh2], [row(a['norm_mlp'][l])],
                                      [(D_MODEL, F32), (D_MODEL, BF16)], sums=(D_MODEL,), name=f"rms_mlp_bwd{l}")
        return dh1, dh1_b, d_wpp, d_nple, d_nmlp

    dh4, dh4_b, d_wpp1, d_nple1, d_nmlp1 = mlp_ple_bwd(dh, 1, saved1)

    d_glu = _mm(dh4_b, **W('w_out_o', 0), dims="nt", name="out_o_t")
    into(glu, dh4_b, ('w_out_o', 0), "d_out_o")

    def glu_bwd(d, z, act):
        s, act = jax.nn.sigmoid(z), act.astype(F32)
        dz = d * act * s * (1.0 - s)
        return dz, d * s, jnp.sum(dz, axis=0, keepdims=True)

    d_z, d_actp, d_bglu = _rowwise(glu_bwd, [d_glu, z, actg], [], [(D_MODEL, BF16), (D_MODEL, F32)], sums=(D_MODEL,),
                                   name="glu_bwd")
    into(actg, d_z, ('w_glu', 0), "d_glu")

    def reduce_start(bufs, tag):
        parts = [t if t.shape[1] % 32 else t.reshape(4, 2, t.shape[1] // 2, t.shape[2]) for t in bufs]
        recv = _pair_split(parts, name=f"pair_split{tag}")
        sums = [_pair_sum(f, r, c_idx, name=f"pair_sum{tag}_{i}") for i, (f, r) in enumerate(zip(parts, recv))]
        started, prev = [], c_idx
        for i, s in enumerate(sums):
            started.append(_rs_start(s, prev, name=f"scatter{tag}_{i}_start"))
            prev = started[-1][4]
        return started

    def pack_split(k):
        part = gb[k].reshape(4, 2, rt[k] // 2, PACK_COLS)
        return _pair_split_start(part, c_idx, name=f"pair_split{k}_start")

    def pack_scatter(k, split, after):
        part, recv = _pair_split_wait(split[0], split[1], split[2], split[3], after, name=f"pair_split{k}_wait")
        return [_rs_start(_pair_sum(part, recv, c_idx, name=f"pair_sum{k}_0"), c_idx, name=f"scatter{k}_0_start")]

    def reduce_finish(started, after, tag):
        halves = []
        for i, (ssem, rsem, part, land, _) in enumerate(started):
            part, land = _rs_wait(ssem, rsem, part, land, after, name=f"scatter{tag}_{i}_wait")
            halves.append(_sum_own_recv(part, land, me_idx, name=f"sum_partials{tag}_{i}"))
        return list(zip(halves, _pair_swap(halves, name=f"swap_halves{tag}")))

    split1 = pack_split(1)

    def gelu_bwd(acc, dap, y):
        _, vjp = jax.vjp(jax.nn.gelu, y)
        return vjp(acc + dap)

    dy = _mm(d_z, **W('w_glu', 0), dims="nt", extras=(d_actp, y), epilogue=gelu_bwd, name="glu_t",
             after=(split1[4],))
    du, d_bre, d_bim, d_cre, d_cim, d_lr, d_li, d_s5d = _s5_bwd(dy, u, xre, xim, bre, bim, cre, cim,
                                                                 pw_r[:, ::-1], -pw_i[:, ::-1], s5_d_full)
    dh3, d_nmix1 = _rowwise(_rms_bwd_fn, [h3, du, dh4], [row(a['norm_mix'][1])], [(D_MODEL, F32)], sums=(D_MODEL,),
                            name="rms_mix1_bwd")
    d_are, d_aim, d_logdt, d_sbre, d_sbim, d_scre, d_scim = s5_vjp(
        (d_bre, d_bim, d_cre, d_cim, d_lr.sum(axis=1), d_li.sum(axis=1)))

    started1 = pack_scatter(1, split1, dh3)
    dh1, dh1_b, d_wpp0, d_nple0, d_nmlp0 = mlp_ple_bwd(dh3, 0, saved0, after=(started1[0][4],))

    d_merged = _mm(dh1_b, **W('w_out_e', 0), dims="nt", name="out_e_t")
    into(merged, dh1_b, ('w_out_e', 0), "d_out_e")
    split0 = pack_split(0)
    dq, df, dv, dg, d_lb, d_gna = _hgrn_bwd(proj, lb, gna, st_a, d_merged, after=(split0[4],))
    started0 = pack_scatter(0, split0, dq)
    dqb, dkb, dvb, dzb, d_tail, d_arow, d_dtrow, d_gnb = _gdn_bwd(act_b, proj, arow, dtrow, gnb, st_b, d_merged,
                                                                  after=(started0[0][4],))
    d_pre, d_conv = _conv_bwd(proj, conv_full, dqb, dkb, dvb)
    d_proj = jnp.concatenate([dq, df, dv, dg, d_pre, dzb, d_tail], axis=1)
    d_win_t = _mm(d_proj, hn0, dims="tn", bm=_IN_BLK, out_dtypes=(BF16,), name="d_in_proj")
    d_hn0 = _mm(d_proj, win_t, bk=_IN_BLK, name="in_proj_t")
    grad_x, d_nmix0 = _rowwise(_rms_bwd_fn, [x, d_hn0, dh1], [row(a['norm_mix'][0])], [(D_MODEL, F32)], sums=(D_MODEL,),
                               name="rms_mix0_bwd")
    (d_hlb,) = lb_vjp(d_lb.reshape(1, 1024))
    d_alog, d_dtb = gdn_vjp((d_arow, d_dtrow))

    d_win_sh = d_win_t[:8208].reshape(4, 2052, D_MODEL)
    d_wpp_sh = jnp.stack([d_wpp0, d_wpp1]).astype(BF16).reshape(2, 256, 4, 512).transpose(2, 0, 1, 3).reshape(4, 512, 512)
    started2 = reduce_start([d_win_sh, d_wpp_sh], 2)
    small = {
        'norm_mix': jnp.concatenate([d_nmix0, d_nmix1]), 'norm_mlp': jnp.concatenate([d_nmlp0, d_nmlp1]),
        'norm_ple': jnp.concatenate([d_nple0, d_nple1]), 'hgrn_lb': d_hlb, 'g_norm_a': d_gna, 'conv_w': d_conv,
        'a_log': d_alog, 'dt_bias': d_dtb, 'g_norm_b': d_gnb, 's5_a_re': d_are, 's5_a_im': d_aim, 's5_b_re': d_sbre,
        's5_b_im': d_sbim, 's5_c_re': d_scre, 's5_c_im': d_scim, 's5_d': d_s5d, 's5_log_dt': d_logdt,
        'b_glu': d_bglu, 'final_norm': d_final}
    s_names = list(small)
    s_shapes = [tuple(small[n].shape) for n in s_names]
    mine_small = _pack_rows([small[n] for n in s_names])
    chip_small = _rowwise(lambda p, o: p + o, [mine_small, _pair_swap([mine_small], name="swap_small")[0]], [],
                          [(128, F32)], name="add_small")
    n_small = chip_small.shape[0]
    small_go = _ag_start([_own_slot(chip_small.reshape(2, n_small // 2, 128), me_idx, name="own_slot_small")],
                         started2[-1][4], name="gather_small_start")
    red = {1: reduce_finish(started1, small_go[3], 1)[0]}
    red[0] = reduce_finish(started0, small_go[3], 0)[0]
    res, order = {}, []

    def adam_layer(nm, layer, key, g_off=None):
        g_off = views[nm, layer][1] if g_off is None else g_off
        prev = res.get(nm)
        follows = () if not order or (prev is not None and order[-1] is prev[0]) else (order[-1],)
        res[nm] = tuple(_adamw(a[nm], a['m_' + nm], a['v_' + nm], layer, *red[key], c_idx, name=f"adam_{nm}{layer}",
                               g_off=g_off, prev=prev, after=follows))
        order.append(res[nm][0])

    for nm, layer in (('w_glu', 0), ('w_out_o', 0), ('w_up', 1), ('w_down', 1), ('w_ple_gate', 1)):
        adam_layer(nm, layer, 1)
    for nm in ('w_up', 'w_down', 'w_ple_gate', 'w_out_e'):
        adam_layer(nm, 0, 0)

    small_all = _pair_forward(_ag_wait(small_go[0], small_go[1], small_go[2], order[-1], name="gather_small_wait"),
                              name="gather_small_pass")[0]
    reduced = _sum_slots(small_all.reshape(4, n_small, 128), name="sum_small")
    sg = dict(zip(s_names, _unpack_rows(reduced, s_shapes)))
    sg['conv_w'] = lax.dynamic_slice_in_dim(sg['conv_w'], me * 768, 768, axis=1)
    sg['s5_d'] = lax.dynamic_slice_in_dim(sg['s5_d'], me * 512, 512, axis=1)
    sg['b_glu'] = lax.dynamic_slice_in_dim(sg['b_glu'], me * 512, 512, axis=1)
    sg = {n: sg[n].reshape(a[n].shape) for n in s_names}
    g_pack = _pack_rows([sg[n] for n in s_names])
    sd, sm, sv = _rowwise(_adam_math, [w_pack, g_pack, m_pack, v_pack], [], [(128, F32)] * 3, name="adam_small")
    w_shapes = [tuple(a[n].shape) for n in s_names]
    res.update({n: (sg[n], d_, m_, v_) for n, d_, m_, v_ in zip(s_names, _unpack_rows(sd, w_shapes),
                                                               _unpack_rows(sm, w_shapes), _unpack_rows(sv, w_shapes))})

    red['in'], red['pp'] = reduce_finish(started2, sd, 2)
    res['w_in_e'] = tuple(jnp.transpose(o, (1, 2, 0)) for o in _adamw_t(
        rows_first(a['w_in_e']), rows_first(a['m_w_in_e']), rows_first(a['v_w_in_e']),
        *[h.reshape(2052, 1, D_MODEL // 2) for h in red['in']], c_idx, name="adam_w_in_e"))
    adam_layer('w_ple_proj', 0, 'pp', 0)
    adam_layer('w_ple_proj', 1, 'pp', 256)

    return (loss, grad_x[None], *[res[n][0] for n in _WEIGHTS], *[res[n][1] for n in _WEIGHTS],
            *[res[n][2] for n in _WEIGHTS], *[res[n][3] for n in _WEIGHTS])


def kernel(x, p, norm_mix, norm_mlp, norm_ple, w_in_e, w_out_e, hgrn_lb, g_norm_a, conv_w, a_log, dt_bias, g_norm_b, s5_a_re, s5_a_im, s5_b_re, s5_b_im, s5_c_re, s5_c_im, s5_d, s5_log_dt, w_glu, b_glu, w_out_o, w_up, w_down, w_ple_gate, w_ple_proj, final_norm, loss_target, m_norm_mix, m_norm_mlp, m_norm_ple, m_w_in_e, m_w_out_e, m_hgrn_lb, m_g_norm_a, m_conv_w, m_a_log, m_dt_bias, m_g_norm_b, m_s5_a_re, m_s5_a_im, m_s5_b_re, m_s5_b_im, m_s5_c_re, m_s5_c_im, m_s5_d, m_s5_log_dt, m_w_glu, m_b_glu, m_w_out_o, m_w_up, m_w_down, m_w_ple_gate, m_w_ple_proj, m_final_norm, v_norm_mix, v_norm_mlp, v_norm_ple, v_w_in_e, v_w_out_e, v_hgrn_lb, v_g_norm_a, v_conv_w, v_a_log, v_dt_bias, v_g_norm_b, v_s5_a_re, v_s5_a_im, v_s5_b_re, v_s5_b_im, v_s5_c_re, v_s5_c_im, v_s5_d, v_s5_log_dt, v_w_glu, v_b_glu, v_w_out_o, v_w_up, v_w_down, v_w_ple_gate, v_w_ple_proj, v_final_norm):
    args = (x, p, norm_mix, norm_mlp, norm_ple, w_in_e, w_out_e, hgrn_lb, g_norm_a, conv_w, a_log, dt_bias, g_norm_b, s5_a_re, s5_a_im, s5_b_re, s5_b_im, s5_c_re, s5_c_im, s5_d, s5_log_dt, w_glu, b_glu, w_out_o, w_up, w_down, w_ple_gate, w_ple_proj, final_norm, loss_target, m_norm_mix, m_norm_mlp, m_norm_ple, m_w_in_e, m_w_out_e, m_hgrn_lb, m_g_norm_a, m_conv_w, m_a_log, m_dt_bias, m_g_norm_b, m_s5_a_re, m_s5_a_im, m_s5_b_re, m_s5_b_im, m_s5_c_re, m_s5_c_im, m_s5_d, m_s5_log_dt, m_w_glu, m_b_glu, m_w_out_o, m_w_up, m_w_down, m_w_ple_gate, m_w_ple_proj, m_final_norm, v_norm_mix, v_norm_mlp, v_norm_ple, v_w_in_e, v_w_out_e, v_hgrn_lb, v_g_norm_a, v_conv_w, v_a_log, v_dt_bias, v_g_norm_b, v_s5_a_re, v_s5_a_im, v_s5_b_re, v_s5_b_im, v_s5_c_re, v_s5_c_im, v_s5_d, v_s5_log_dt, v_w_glu, v_b_glu, v_w_out_o, v_w_up, v_w_down, v_w_ple_gate, v_w_ple_proj, v_final_norm)
    return _step(dict(zip(_INPUTS, args)))
```
